```python
import math
import jax, jax.numpy as jnp
from jax import lax
import numpy as np

D_MODEL = 1024
BATCH = 8
SEQ = 4096
DEPTH = 4

N_MIXERS = 3
RMS_EPS = 1e-6
NEG_INF = -1e30
TINY = 1e-30
FORCE_SCORE = 1e9

N_BUCKETS = 32
REL_MAX_DISTANCE = 2048
N_BIAS_HEADS = 16

A_GROUPS = ((128, 1), (512, 4), (2048, 16))
A_HEADS = 16
A_HEAD_DIM = D_MODEL // A_HEADS
A_Q_BLOCK = 128
A_IN_WIDTH = len(A_GROUPS) * 3 * A_HEADS * A_HEAD_DIM

B_HEADS = 16
B_KV_HEADS = 4
B_HEAD_DIM = 64
B_CMP_LEN = 32
B_CMP_STRIDE = 16
B_CMP_HIDDEN = 256
B_SEL_BLOCK = 64
B_TOP_N = 16
B_WINDOW = 512
B_Q_BLOCK = 64
B_WIN_Q_BLOCK = 128
B_IN_WIDTH = B_HEADS * B_HEAD_DIM + 6 * B_KV_HEADS * B_HEAD_DIM + 3 * B_HEADS

C_HEADS = 8
C_HEAD_DIM = 128
C_WIDTH = C_HEADS * C_HEAD_DIM
C_CONV = 4
C_CHUNK = 64
C_IN_WIDTH = 4 * C_WIDTH + 2 * C_HEADS

FFN_HIDDEN = (8 * D_MODEL + 3 * 256 - 1) // (3 * 256) * 256

kernel_name = 'hybrid_dilated_nsa_gdn_trunk'


def rms_norm(x, gain):
    xf = x.astype(jnp.float32)
    y = xf * lax.rsqrt(jnp.mean(xf * xf, axis=-1, keepdims=True) + RMS_EPS)
    return (y * gain.astype(jnp.float32)).astype(x.dtype)


def l2_normalize(x):
    return x * lax.rsqrt(jnp.sum(x * x, axis=-1, keepdims=True) + RMS_EPS)


def rel_bucket(dist):
    dist = jnp.maximum(dist, 0)
    max_exact = N_BUCKETS // 2
    d_f = jnp.maximum(dist, 1).astype(jnp.float32)
    large = max_exact + (jnp.log(d_f / max_exact) / math.log(REL_MAX_DISTANCE / max_exact)
                         * (N_BUCKETS - max_exact)).astype(jnp.int32)
    return jnp.where(dist < max_exact, dist, jnp.minimum(large, N_BUCKETS - 1))


def masked_softmax(logits, mask):
    logits = jnp.where(mask, logits.astype(jnp.float32), NEG_INF)
    m = jnp.max(logits, axis=-1, keepdims=True)
    e = jnp.where(mask, jnp.exp(logits - m), 0.0)
    z = jnp.maximum(jnp.sum(e, axis=-1, keepdims=True), TINY)
    return e / z, (m + jnp.log(z))[..., 0]


def banded_causal_attention(q, k, v, bias_by_dist, window, block):
    bx, n, g, L, dh = q.shape
    n_blocks = -(-L // block)
    Lp = n_blocks * block
    n_prev = -(-window // block)
    span = (n_prev + 1) * block
    q = jnp.pad(q, ((0, 0), (0, 0), (0, 0), (0, Lp - L), (0, 0)))
    kv_pad = ((0, 0), (0, 0), (n_prev * block, Lp - L), (0, 0))
    k = jnp.pad(k, kv_pad)
    v = jnp.pad(v, kv_pad)
    scale = dh ** -0.5

    def one_block(i):
        start = i * block
        qb = lax.dynamic_slice_in_dim(q, start, block, axis=3)
        kb = lax.dynamic_slice_in_dim(k, start, span, axis=2)
        vb = lax.dynamic_slice_in_dim(v, start, span, axis=2)
        q_pos = start + jnp.arange(block)
        k_pos = start - n_prev * block + jnp.arange(span)
        dist = q_pos[:, None] - k_pos[None, :]
        mask = (dist >= 0) & (dist <= window) & (k_pos >= 0)[None, :]
        s = (jnp.einsum('bngqd,bnkd->bngqk', qb, kb).astype(jnp.float32) * scale
             + bias_by_dist[:, :, jnp.clip(dist, 0, window)])
        p, lse = masked_softmax(s, mask)
        return jnp.einsum('bngqk,bnkd->bngqd', p.astype(vb.dtype), vb), lse

    o, lse = lax.map(one_block, jnp.arange(n_blocks))
    o = jnp.moveaxis(o, 0, 3).reshape(bx, n, g, Lp, dh)[:, :, :, :L]
    lse = jnp.moveaxis(lse, 0, 3).reshape(bx, n, g, Lp)[..., :L]
    return o, lse


def dilated_attention(h, w_in, q_gain, k_gain, w_out, rel_bias):
    B, S, _ = h.shape
    H, dh = A_HEADS, A_HEAD_DIM
    proj = (h @ w_in).reshape(B, S, len(A_GROUPS), 3, H, dh)
    outs, lses = [], []
    for gi, (window, dil) in enumerate(A_GROUPS):
        steps = window // dil
        L = S // dil

        def by_stride(t):
            t = t.reshape(B, L, dil, H, dh)
            return t.transpose(0, 2, 3, 1, 4).reshape(B * dil, H, L, dh)

        q = by_stride(rms_norm(proj[:, :, gi, 0], q_gain[gi]))
        k = by_stride(rms_norm(proj[:, :, gi, 1], k_gain[gi]))
        v = by_stride(proj[:, :, gi, 2])
        bias = rel_bias[rel_bucket(jnp.arange(steps + 1) * dil)].T[:, None, :]
        o, lse = banded_causal_attention(q[:, :, None], k, v, bias, steps, A_Q_BLOCK)
        outs.append(o[:, :, 0].reshape(B, dil, H, L, dh).transpose(0, 3, 1, 2, 4).reshape(B, S, H, dh))
        lses.append(lse[:, :, 0].reshape(B, dil, H, L).transpose(0, 3, 1, 2).reshape(B, S, H))
    w = jax.nn.softmax(jnp.stack(lses), axis=0)
    o = jnp.einsum('gbsh,gbshd->bshd', w, jnp.stack(outs).astype(jnp.float32))
    return o.reshape(B, S, H * dh).astype(h.dtype) @ w_out


def native_sparse_attention(h, w_in, q_gain, k_gain, cmp_pos, cmp_w1, cmp_w2, w_out, rel_bias):
    B, S, _ = h.shape
    H, N, dh = B_HEADS, B_KV_HEADS, B_HEAD_DIM
    G = H // N
    q, kv, gate = jnp.split(h @ w_in, [H * dh, H * dh + 6 * N * dh], axis=-1)
    q = rms_norm(q.reshape(B, S, N, G, dh), q_gain).transpose(0, 2, 3, 1, 4)
    kv = kv.reshape(B, S, 3, 2, N, dh).transpose(2, 3, 0, 4, 1, 5)
    gate = jax.nn.sigmoid(gate.astype(jnp.float32)).reshape(B, S, N, G, 3).transpose(4, 0, 2, 3, 1)
    tbl = rel_bias.reshape(N_BUCKETS, N, G).transpose(1, 2, 0)
    scale = dh ** -0.5

    n_cmp = (S - B_CMP_LEN) // B_CMP_STRIDE + 1
    tok = jnp.arange(n_cmp)[:, None] * B_CMP_STRIDE + jnp.arange(B_CMP_LEN)[None, :]

    def compress(t, pos, w1, w2):
        blocks = (t[:, :, tok] + pos).reshape(B, N, n_cmp, B_CMP_LEN * dh)
        return jax.nn.gelu(blocks @ w1) @ w2

    k_cmp = rms_norm(compress(kv[0, 0], cmp_pos[0], cmp_w1[0], cmp_w2[0]), k_gain[0])
    v_cmp = compress(kv[0, 1], cmp_pos[1], cmp_w1[1], cmp_w2[1]).astype(jnp.float32)
    cmp_end = jnp.arange(n_cmp) * B_CMP_STRIDE + B_CMP_LEN - 1

    n_sel = S // B_SEL_BLOCK
    top_n = min(B_TOP_N, n_sel)
    cmp_start = jnp.arange(n_cmp)[:, None] * B_CMP_STRIDE
    sel_start = jnp.arange(n_sel)[None, :] * B_SEL_BLOCK
    cmp_to_sel = ((cmp_start < sel_start + B_SEL_BLOCK)
                  & (cmp_start + B_CMP_LEN > sel_start)).astype(jnp.float32)
    k_sel = rms_norm(kv[1, 0], k_gain[1]).reshape(B, N, n_sel, B_SEL_BLOCK, dh)
    v_sel = kv[1, 1].reshape(B, N, n_sel, B_SEL_BLOCK, dh)
    b_ix = jnp.arange(B)[:, None, None, None]
    n_ix = jnp.arange(N)[None, :, None, None]
    n_ix5 = jnp.arange(N)[None, :, None, None, None]
    g_ix5 = jnp.arange(G)[None, None, :, None, None]
    blk = jnp.arange(n_sel)

    def query_block(i):
        start = i * B_Q_BLOCK
        t = start + jnp.arange(B_Q_BLOCK)
        qb = lax.dynamic_slice_in_dim(q, start, B_Q_BLOCK, axis=3)
        dist_c = t[:, None] - cmp_end[None, :]
        s_c = (jnp.einsum('bngqd,bncd->bngqc', qb, k_cmp).astype(jnp.float32) * scale
               + tbl[:, :, rel_bucket(dist_c)])
        p_c, _ = masked_softmax(s_c, dist_c >= 0)
        o_c = jnp.einsum('bngqc,bncd->bngqd', p_c, v_cmp)
        imp = jnp.einsum('bnqc,cj->bnqj', p_c.sum(axis=2), cmp_to_sel)
        cur = (t // B_SEL_BLOCK)[:, None]
        forced = (blk == 0) | (blk == cur) | (blk == cur - 1)
        imp = jnp.where(forced, FORCE_SCORE, jnp.where(blk * B_SEL_BLOCK <= t[:, None], imp, NEG_INF))
        _, sel = lax.top_k(imp, top_n)
        ks = k_sel[b_ix, n_ix, sel].reshape(B, N, B_Q_BLOCK, top_n * B_SEL_BLOCK, dh)
        vs = v_sel[b_ix, n_ix, sel].reshape(B, N, B_Q_BLOCK, top_n * B_SEL_BLOCK, dh)
        k_pos = (sel[..., None] * B_SEL_BLOCK + jnp.arange(B_SEL_BLOCK)).reshape(B, N, B_Q_BLOCK, -1)
        dist_s = t[:, None] - k_pos
        bias_s = tbl[n_ix5, g_ix5, rel_bucket(dist_s)[:, :, None]]
        s_s = jnp.einsum('bngqd,bnqkd->bngqk', qb, ks).astype(jnp.float32) * scale + bias_s
        p_s, _ = masked_softmax(s_s, (dist_s >= 0)[:, :, None])
        o_s = jnp.einsum('bngqk,bnqkd->bngqd', p_s, vs.astype(jnp.float32))
        return o_c, o_s

    o_c, o_s = lax.map(query_block, jnp.arange(S // B_Q_BLOCK))

    def unblock(o):
        return jnp.moveaxis(o, 0, 3).reshape(B, N, G, S, dh)

    k_win = rms_norm(kv[2, 0], k_gain[2])
    bias_w = tbl[:, :, rel_bucket(jnp.arange(B_WINDOW))]
    o_w, _ = banded_causal_attention(q, k_win, kv[2, 1], bias_w, B_WINDOW - 1, B_WIN_Q_BLOCK)

    o = (gate[0][..., None] * unblock(o_c) + gate[1][..., None] * unblock(o_s)
         + gate[2][..., None] * o_w.astype(jnp.float32))
    o = o.transpose(0, 3, 1, 2, 4).reshape(B, S, H * dh).astype(h.dtype)
    return o @ w_out


def causal_depthwise_conv(x, w):
    width, ch = w.shape
    return lax.conv_general_dilated(x, w[:, None, :].astype(x.dtype), window_strides=(1,),
                                    padding=[(width - 1, 0)], dimension_numbers=('NWC', 'WIO', 'NWC'),
                                    feature_group_count=ch)


def chunk_gated_delta_rule(q, k, v, g, beta):
    B, S, H, dk = q.shape
    dv = v.shape[-1]
    C = C_CHUNK
    nc = S // C

    def chunks(t):
        return jnp.moveaxis(t.reshape(B, nc, C, H, -1), 3, 1)

    q, k, v = chunks(q), chunks(k), chunks(v)
    beta = chunks(beta[..., None])[..., 0]
    g_cum = jnp.cumsum(chunks(g[..., None])[..., 0], axis=-1)
    causal = jnp.tril(jnp.ones((C, C), bool))
    strict = jnp.tril(jnp.ones((C, C), bool), -1)
    decay = jnp.exp(jnp.where(causal, g_cum[..., :, None] - g_cum[..., None, :], NEG_INF))
    k_beta = k * beta[..., None]
    lower = jnp.where(strict, jnp.einsum('bhnid,bhnjd->bhnij', k_beta, k) * decay, 0.0)
    eye = jnp.eye(C, dtype=jnp.float32)
    t_mat = lax.linalg.triangular_solve(eye + lower, jnp.broadcast_to(eye, lower.shape),
                                        left_side=True, lower=True, unit_diagonal=True)
    u = t_mat @ (v * beta[..., None])
    w = t_mat @ (k_beta * jnp.exp(g_cum)[..., None])
    attn = jnp.where(causal, jnp.einsum('bhnid,bhnjd->bhnij', q, k), 0.0) * decay

    def step(state, xs):
        q_i, k_i, u_i, w_i, a_i, g_i = xs
        v_new = u_i - w_i @ state
        o_i = (q_i * jnp.exp(g_i)[..., None]) @ state + a_i @ v_new
        g_last = g_i[..., -1:]
        state = (state * jnp.exp(g_last)[..., None]
                 + jnp.einsum('bhck,bhcv->bhkv', k_i * jnp.exp(g_last - g_i)[..., None], v_new))
        return state, o_i

    xs = tuple(jnp.moveaxis(t, 2, 0) for t in (q, k, u, w, attn, g_cum))
    _, o = lax.scan(step, jnp.zeros((B, H, dk, dv), jnp.float32), xs)
    return o.transpose(1, 0, 3, 2, 4).reshape(B, S, H, dv)


def gated_deltanet(h, w_in, conv_w, a_log, dt_bias, out_gain, w_out):
    B, S, _ = h.shape
    H, dk = C_HEADS, C_HEAD_DIM
    qkv, z, beta, a = jnp.split(h @ w_in, [3 * C_WIDTH, 4 * C_WIDTH, 4 * C_WIDTH + H], axis=-1)
    qkv = jax.nn.silu(causal_depthwise_conv(qkv, conv_w)).astype(jnp.float32)
    q, k, v = (t.reshape(B, S, H, dk) for t in jnp.split(qkv, 3, axis=-1))
    q = l2_normalize(q) * dk ** -0.5
    k = l2_normalize(k)
    beta = jax.nn.sigmoid(beta.astype(jnp.float32))
    g = -jnp.exp(a_log.astype(jnp.float32)) * jax.nn.softplus(a.astype(jnp.float32) + dt_bias.astype(jnp.float32))
    o = chunk_gated_delta_rule(q, k, v, g, beta)
    o = rms_norm(o, out_gain) * jax.nn.silu(z.astype(jnp.float32).reshape(B, S, H, dk))
    return o.reshape(B, S, C_WIDTH).astype(h.dtype) @ w_out


def swiglu(h, w_gate, w_up, w_down):
    return (jax.nn.silu(h @ w_gate) * (h @ w_up)) @ w_down


def setup_inputs(seed: int = 0) -> dict:
    key = jax.random.key(seed)
    keys = iter(jax.random.split(key, 96))

    def normal(shape, std):
        return std * jax.random.normal(next(keys), shape, jnp.float32)

    def dense(fan_in, shape):
        return normal(shape, fan_in ** -0.5)

    def gain(shape):
        return 1.0 + normal(shape, 0.05)

    p = {'x': normal((BATCH, SEQ, D_MODEL), 1.0),
         'rel_bias': normal((N_BUCKETS, N_BIAS_HEADS), 0.5)}
    for layer in range(DEPTH):
        pre = f'l{layer}_'
        p[pre + 'norm1'] = gain((D_MODEL,))
        kind = layer % N_MIXERS
        if kind == 0:
            p[pre + 'a_w_in'] = dense(D_MODEL, (D_MODEL, A_IN_WIDTH))
            p[pre + 'a_q_gain'] = gain((len(A_GROUPS), A_HEAD_DIM))
            p[pre + 'a_k_gain'] = gain((len(A_GROUPS), A_HEAD_DIM))
            p[pre + 'a_w_out'] = dense(A_HEADS * A_HEAD_DIM, (A_HEADS * A_HEAD_DIM, D_MODEL))
        elif kind == 1:
            p[pre + 'b_w_in'] = dense(D_MODEL, (D_MODEL, B_IN_WIDTH))
            p[pre + 'b_q_gain'] = gain((B_HEAD_DIM,))
            p[pre + 'b_k_gain'] = gain((3, B_HEAD_DIM))
            p[pre + 'b_cmp_pos'] = normal((2, B_CMP_LEN, B_HEAD_DIM), 0.02)
            p[pre + 'b_cmp_w1'] = dense(B_CMP_LEN * B_HEAD_DIM, (2, B_CMP_LEN * B_HEAD_DIM, B_CMP_HIDDEN))
            p[pre + 'b_cmp_w2'] = dense(B_CMP_HIDDEN, (2, B_CMP_HIDDEN, B_HEAD_DIM))
            p[pre + 'b_w_out'] = dense(B_HEADS * B_HEAD_DIM, (B_HEADS * B_HEAD_DIM, D_MODEL))
        else:
            p[pre + 'c_w_in'] = dense(D_MODEL, (D_MODEL, C_IN_WIDTH))
            p[pre + 'c_conv_w'] = dense(C_CONV, (C_CONV, 3 * C_WIDTH))
            p[pre + 'c_a_log'] = jnp.log(jax.random.uniform(next(keys), (C_HEADS,), jnp.float32, 1.0, 16.0))
            dt = jnp.exp(jax.random.uniform(next(keys), (C_HEADS,), jnp.float32,
                                            math.log(1e-3), math.log(1e-1)))
            p[pre + 'c_dt_bias'] = dt + jnp.log(-jnp.expm1(-dt))
            p[pre + 'c_out_gain'] = gain((C_HEAD_DIM,))
            p[pre + 'c_w_out'] = dense(C_WIDTH, (C_WIDTH, D_MODEL))
        p[pre + 'norm2'] = gain((D_MODEL,))
        p[pre + 'ffn_w_gate'] = dense(D_MODEL, (D_MODEL, FFN_HIDDEN))
        p[pre + 'ffn_w_up'] = dense(D_MODEL, (D_MODEL, FFN_HIDDEN))
        p[pre + 'ffn_w_down'] = dense(FFN_HIDDEN, (FFN_HIDDEN, D_MODEL))
    return p


def reference(x, rel_bias,
              l0_norm1, l0_a_w_in, l0_a_q_gain, l0_a_k_gain, l0_a_w_out,
              l0_norm2, l0_ffn_w_gate, l0_ffn_w_up, l0_ffn_w_down,
              l1_norm1, l1_b_w_in, l1_b_q_gain, l1_b_k_gain, l1_b_cmp_pos, l1_b_cmp_w1, l1_b_cmp_w2, l1_b_w_out,
              l1_norm2, l1_ffn_w_gate, l1_ffn_w_up, l1_ffn_w_down,
              l2_norm1, l2_c_w_in, l2_c_conv_w, l2_c_a_log, l2_c_dt_bias, l2_c_out_gain, l2_c_w_out,
              l2_norm2, l2_ffn_w_gate, l2_ffn_w_up, l2_ffn_w_down,
              l3_norm1, l3_a_w_in, l3_a_q_gain, l3_a_k_gain, l3_a_w_out,
              l3_norm2, l3_ffn_w_gate, l3_ffn_w_up, l3_ffn_w_down):
    layers = (
        (l0_norm1, (l0_a_w_in, l0_a_q_gain, l0_a_k_gain, l0_a_w_out),
         l0_norm2, (l0_ffn_w_gate, l0_ffn_w_up, l0_ffn_w_down)),
        (l1_norm1, (l1_b_w_in, l1_b_q_gain, l1_b_k_gain, l1_b_cmp_pos, l1_b_cmp_w1, l1_b_cmp_w2, l1_b_w_out),
         l1_norm2, (l1_ffn_w_gate, l1_ffn_w_up, l1_ffn_w_down)),
        (l2_norm1, (l2_c_w_in, l2_c_conv_w, l2_c_a_log, l2_c_dt_bias, l2_c_out_gain, l2_c_w_out),
         l2_norm2, (l2_ffn_w_gate, l2_ffn_w_up, l2_ffn_w_down)),
        (l3_norm1, (l3_a_w_in, l3_a_q_gain, l3_a_k_gain, l3_a_w_out),
         l3_norm2, (l3_ffn_w_gate, l3_ffn_w_up, l3_ffn_w_down)),
    )
    for layer in range(DEPTH):
        norm1, mixer_args, norm2, ffn_args = layers[layer]
        kind = layer % N_MIXERS
        h = rms_norm(x, norm1)
        if kind == 0:
            y = dilated_attention(h, *mixer_args, rel_bias)
        elif kind == 1:
            y = native_sparse_attention(h, *mixer_args, rel_bias)
        else:
            y = gated_deltanet(h, *mixer_args)
        x = x + y
        x = x + swiglu(rms_norm(x, norm2), *ffn_args)
    return x
```

```python
import functools
import math

import numpy as np
import jax
import jax.numpy as jnp
from jax import lax
from jax.experimental import pallas as pl
from jax.experimental.pallas import tpu as pltpu

D_MODEL = 1024
RMS_EPS = 1e-6
NEG_INF = -1e30
TINY = 1e-30
FORCE_SCORE = 1e9

N_BUCKETS = 32
REL_MAX_DISTANCE = 2048
N_HEADS = 16

A_GROUPS = ((128, 1), (512, 4), (2048, 16))
A_HEAD_DIM = 64
A_Q_BLOCK = 128

B_KV_HEADS = 4
B_GROUP = 4
B_HEAD_DIM = 64
B_CMP_LEN = 32
B_CMP_STRIDE = 16
B_CMP_HIDDEN = 256
B_SEL_BLOCK = 64
B_TOP_N = 16
B_WINDOW = 512
B_TILE = 128
B_PROJ_WIDTH = 3072

C_HEADS = 8
C_HEAD_DIM = 128
C_WIDTH = C_HEADS * C_HEAD_DIM
C_CONV = 4
C_CHUNK = 64
C_CHUNKS_PER_STEP = 4

FFN_HIDDEN = 2816
FFN_TILE = 256

ROW_TILE = 512
VMEM_LIMIT = 48 * 1024 * 1024

F32 = jnp.float32
BF16 = jnp.bfloat16
HIGHEST = lax.Precision.HIGHEST

NT_DIMS = (((1,), (1,)), ((), ()))
TN_DIMS = (((0,), (0,)), ((), ()))


def _params(*semantics):
    return pltpu.CompilerParams(dimension_semantics=semantics, vmem_limit_bytes=VMEM_LIMIT)


def _dot(a, b, precision=None):
    return jnp.dot(a, b, preferred_element_type=F32, precision=precision)


def _dot_nt(a, b, precision=None):
    return lax.dot_general(a, b, NT_DIMS, preferred_element_type=F32, precision=precision)


def _dot_tn(a, b, precision=None):
    return lax.dot_general(a, b, TN_DIMS, preferred_element_type=F32, precision=precision)


def _rms(x, gain):
    return x * lax.rsqrt(jnp.mean(x * x, axis=-1, keepdims=True) + RMS_EPS) * gain


def _bucket_thresholds():
    d = np.arange(1 << 15)
    max_exact = N_BUCKETS // 2
    d_f = np.maximum(d, 1).astype(np.float32)
    large = max_exact + (np.log(d_f / np.float32(max_exact)) / np.float32(math.log(REL_MAX_DISTANCE / max_exact))
                         * np.float32(N_BUCKETS - max_exact)).astype(np.int32)
    bucket = np.where(d < max_exact, d, np.minimum(large, N_BUCKETS - 1))
    return [int(np.argmax(bucket >= k)) if np.any(bucket >= k) else int(1 << 30) for k in range(N_BUCKETS)]


_THRESHOLDS = _bucket_thresholds()


def _bias_tile_kernel(tbl_ref, o_ref, *, base, tile_step, row_step, col_step, dmax, dil):
    h = pl.program_id(0)
    t = pl.program_id(1)
    shape = o_ref.shape[2:]
    i = lax.broadcasted_iota(jnp.int32, shape, 0)
    j = lax.broadcasted_iota(jnp.int32, shape, 1)
    dist = base + tile_step * t + row_step * i + col_step * j
    d = dist * dil
    val = jnp.full(shape, tbl_ref[0, h], F32)
    for k in range(1, N_BUCKETS):
        val = jnp.where(d >= _THRESHOLDS[k], tbl_ref[k, h], val)
    valid = (dist >= 0) & (dist <= dmax)
    o_ref[0, 0] = jnp.where(valid, val, NEG_INF)


def _bias_tiles(rel_bias, n_tiles, rows, cols, *, base, tile_step, row_step, col_step, dmax, dil=1):
    kern = functools.partial(_bias_tile_kernel, base=base, tile_step=tile_step, row_step=row_step,
                             col_step=col_step, dmax=dmax, dil=dil)
    return pl.pallas_call(
        kern,
        out_shape=jax.ShapeDtypeStruct((N_HEADS, n_tiles, rows, cols), F32),
        grid=(N_HEADS, n_tiles),
        in_specs=[pl.BlockSpec(memory_space=pltpu.SMEM)],
        out_specs=pl.BlockSpec((1, 1, rows, cols), lambda h, t: (h, t, 0, 0)),
        compiler_params=_params("parallel", "parallel"),
        name="bias_tiles",
    )(rel_bias)


def _norm_matmul_kernel(x_ref, g_ref, w_ref, o_ref, h_ref):
    @pl.when(pl.program_id(1) == 0)
    def _():
        h_ref[...] = _rms(x_ref[...], g_ref[...]).astype(BF16)

    o_ref[...] = _dot(h_ref[...], w_ref[...]).astype(o_ref.dtype)


def _norm_matmul(x, gain, w, out_dtype, tn):
    m, d = x.shape
    n = w.shape[1]
    return pl.pallas_call(
        _norm_matmul_kernel,
        out_shape=jax.ShapeDtypeStruct((m, n), out_dtype),
        grid=(m // ROW_TILE, n // tn),
        in_specs=[pl.BlockSpec((ROW_TILE, d), lambda i, j: (i, 0)),
                  pl.BlockSpec((1, d), lambda i, j: (0, 0)),
                  pl.BlockSpec((d, tn), lambda i, j: (0, j))],
        out_specs=pl.BlockSpec((ROW_TILE, tn), lambda i, j: (i, j)),
        scratch_shapes=[pltpu.VMEM((ROW_TILE, d), BF16)],
        compiler_params=_params("parallel", "arbitrary"),
        name="norm_matmul",
    )(x, gain.reshape(1, d), w)


def _matmul_residual_kernel(a_ref, w_ref, x_ref, o_ref):
    o_ref[...] = x_ref[...] + _dot(a_ref[...], w_ref[...])


def _matmul_residual(a, w, x):
    m, k = a.shape
    d = w.shape[1]
    return pl.pallas_call(
        _matmul_residual_kernel,
        out_shape=jax.ShapeDtypeStruct((m, d), F32),
        grid=(m // ROW_TILE,),
        in_specs=[pl.BlockSpec((ROW_TILE, k), lambda i: (i, 0)),
                  pl.BlockSpec((k, d), lambda i: (0, 0)),
                  pl.BlockSpec((ROW_TILE, d), lambda i: (i, 0))],
        out_specs=pl.BlockSpec((ROW_TILE, d), lambda i: (i, 0)),
        compiler_params=_params("parallel"),
        name="matmul_residual",
    )(a, w, x)


def _ffn_kernel(x_ref, g_ref, wg_ref, wu_ref, wd_ref, o_ref, h_ref, acc_ref):
    j = pl.program_id(1)

    @pl.when(j == 0)
    def _():
        h_ref[...] = _rms(x_ref[...], g_ref[...]).astype(BF16)
        acc_ref[...] = jnp.zeros_like(acc_ref)

    h = h_ref[...]
    a = _dot(h, wg_ref[...])
    b = _dot(h, wu_ref[...])
    t = (a * jax.nn.sigmoid(a) * b).astype(BF16)
    acc_ref[...] += _dot(t, wd_ref[...])

    @pl.when(j == pl.num_programs(1) - 1)
    def _():
        o_ref[...] = x_ref[...] + acc_ref[...]


def _ffn(x, gain, w_gate, w_up, w_down):
    m, d = x.shape
    hidden = w_gate.shape[1]
    return pl.pallas_call(
        _ffn_kernel,
        out_shape=jax.ShapeDtypeStruct((m, d), F32),
        grid=(m // ROW_TILE, hidden // FFN_TILE),
        in_specs=[pl.BlockSpec((ROW_TILE, d), lambda i, j: (i, 0)),
                  pl.BlockSpec((1, d), lambda i, j: (0, 0)),
                  pl.BlockSpec((d, FFN_TILE), lambda i, j: (0, j)),
                  pl.BlockSpec((d, FFN_TILE), lambda i, j: (0, j)),
                  pl.BlockSpec((FFN_TILE, d), lambda i, j: (j, 0))],
        out_specs=pl.BlockSpec((ROW_TILE, d), lambda i, j: (i, 0)),
        scratch_shapes=[pltpu.VMEM((ROW_TILE, d), BF16), pltpu.VMEM((ROW_TILE, d), F32)],
        compiler_params=_params("parallel", "arbitrary"),
        name="ffn",
    )(x, gain.reshape(1, d), w_gate, w_up, w_down)


def _a_attn_kernel(q_ref, kp_ref, kc_ref, vp_ref, vc_ref, qg_ref, kg_ref, bias_ref, o_ref, lse_ref):
    first = pl.program_id(2) == 0
    prev_off = jnp.where(first, NEG_INF, 0.0)
    lane = lax.broadcasted_iota(jnp.int32, (A_Q_BLOCK, 128), 1)
    lse_tile = jnp.zeros((A_Q_BLOCK, 128), F32)
    qg = qg_ref[...] * (A_HEAD_DIM ** -0.5)
    kg = kg_ref[...]
    for h in range(N_HEADS):
        sl = slice(h * A_HEAD_DIM, (h + 1) * A_HEAD_DIM)
        q = _rms(q_ref[0, :, sl].astype(F32), qg).astype(BF16)
        kp = _rms(kp_ref[0, :, sl].astype(F32), kg).astype(BF16)
        kc = _rms(kc_ref[0, :, sl].astype(F32), kg).astype(BF16)
        sp = _dot_nt(q, kp) + bias_ref[h, :, :A_Q_BLOCK] + prev_off
        sc = _dot_nt(q, kc) + bias_ref[h, :, A_Q_BLOCK:]
        m = jnp.maximum(jnp.max(sp, axis=-1, keepdims=True), jnp.max(sc, axis=-1, keepdims=True))
        ep = jnp.exp(sp - m)
        ec = jnp.exp(sc - m)
        z = jnp.sum(ep, axis=-1, keepdims=True) + jnp.sum(ec, axis=-1, keepdims=True)
        inv = 1.0 / z
        o = _dot((ep * inv).astype(BF16), vp_ref[0, :, sl]) + _dot((ec * inv).astype(BF16), vc_ref[0, :, sl])
        o_ref[0, :, sl] = o.astype(o_ref.dtype)
        lse_tile = jnp.where(lane == h, m + jnp.log(z), lse_tile)
    lse_ref[0] = lse_tile


def _a_attention(proj, q_gain, k_gain, bias, gi, dil, batch, seq):
    width = proj.shape[-1]
    length = seq // dil
    nblk = length // A_Q_BLOCK
    hd = N_HEADS * A_HEAD_DIM
    per_res = width // hd
    pv = proj.reshape(batch, length, dil * width)
    col = gi * 3

    def spec(off, prev):
        if prev:
            return pl.BlockSpec((1, A_Q_BLOCK, hd), lambda b, r, i: (b, jnp.maximum(i - 1, 0), r * per_res + col + off))
        return pl.BlockSpec((1, A_Q_BLOCK, hd), lambda b, r, i: (b, i, r * per_res + col + off))

    o, lse = pl.pallas_call(
        _a_attn_kernel,
        out_shape=(jax.ShapeDtypeStruct((batch, length, dil * hd), BF16),
                   jax.ShapeDtypeStruct((batch, length, dil * 128), F32)),
        grid=(batch, dil, nblk),
        in_specs=[spec(0, False), spec(1, True), spec(1, False), spec(2, True), spec(2, False),
                  pl.BlockSpec((1, A_HEAD_DIM), lambda b, r, i: (0, 0)),
                  pl.BlockSpec((1, A_HEAD_DIM), lambda b, r, i: (0, 0)),
                  pl.BlockSpec((N_HEADS, A_Q_BLOCK, 2 * A_Q_BLOCK), lambda b, r, i: (0, 0, 0))],
        out_specs=(pl.BlockSpec((1, A_Q_BLOCK, hd), lambda b, r, i: (b, i, r)),
                   pl.BlockSpec((1, A_Q_BLOCK, 128), lambda b, r, i: (b, i, r))),
        compiler_params=_params("parallel", "parallel", "arbitrary"),
        name="a_attention",
    )(pv, pv, pv, pv, pv, q_gain.reshape(1, -1), k_gain.reshape(1, -1), bias)
    return o.reshape(batch * seq, hd), lse.reshape(batch * seq, 128)


def _a_out_kernel(o0_ref, o1_ref, o2_ref, l0_ref, l1_ref, l2_ref, e_ref, w_ref, x_ref, out_ref):
    ls = [l0_ref[...], l1_ref[...], l2_ref[...]]
    m = jnp.maximum(jnp.maximum(ls[0], ls[1]), ls[2])
    es = [jnp.exp(l - m) for l in ls]
    inv = 1.0 / (es[0] + es[1] + es[2])
    expand = e_ref[...]
    acc = None
    for e, o_ref in zip(es, (o0_ref, o1_ref, o2_ref)):
        wgt = e * inv
        hi = wgt.astype(BF16)
        lo = (wgt - hi.astype(F32)).astype(BF16)
        term = (_dot(hi, expand) + _dot(lo, expand)) * o_ref[...].astype(F32)
        acc = term if acc is None else acc + term
    out_ref[...] = x_ref[...] + _dot(acc.astype(BF16), w_ref[...])


def _a_out(outs, lses, w_out, x):
    m, d = x.shape
    hd = outs[0].shape[1]
    expand = np.zeros((128, hd), np.float32)
    for h in range(N_HEADS):
        expand[h, h * A_HEAD_DIM:(h + 1) * A_HEAD_DIM] = 1.0
    row = lambda width: pl.BlockSpec((ROW_TILE, width), lambda i: (i, 0))
    return pl.pallas_call(
        _a_out_kernel,
        out_shape=jax.ShapeDtypeStruct((m, d), F32),
        grid=(m // ROW_TILE,),
        in_specs=[row(hd), row(hd), row(hd), row(128), row(128), row(128),
                  pl.BlockSpec((128, hd), lambda i: (0, 0)),
                  pl.BlockSpec((hd, d), lambda i: (0, 0)),
                  row(d)],
        out_specs=row(d),
        compiler_params=_params("parallel"),
        name="a_out",
    )(*outs, *lses, jnp.asarray(expand, BF16), w_out, x)


def _mixer_a(x, rel_bias, norm1, w_in, q_gain, k_gain, w_out, batch, seq):
    proj = _norm_matmul(x, norm1, w_in.astype(BF16), BF16, 512).reshape(batch, seq, -1)
    outs, lses = [], []
    for gi, (window, dil) in enumerate(A_GROUPS):
        steps = window // dil
        assert steps == A_Q_BLOCK and (seq // dil) % A_Q_BLOCK == 0
        bias = _bias_tiles(rel_bias, 1, A_Q_BLOCK, 2 * A_Q_BLOCK, base=A_Q_BLOCK, tile_step=0, row_step=1,
                           col_step=-1, dmax=steps, dil=dil)[:, 0]
        o, lse = _a_attention(proj, q_gain[gi], k_gain[gi], bias, gi, dil, batch, seq)
        outs.append(o)
        lses.append(lse)
    return _a_out(outs, lses, w_out.astype(BF16), x)


def _b_prep_kernel(p_ref, qg_ref, kg_ref, q_ref, ck_ref, cv_ref, sk_ref, sv_ref, wk_ref, wv_ref, gate_ref):
    dh = B_HEAD_DIM
    qg = qg_ref[...] * (dh ** -0.5)
    for h in range(N_HEADS):
        q_ref[0, h] = _rms(p_ref[0, :, h * dh:(h + 1) * dh].astype(F32), qg).astype(BF16)
    base = N_HEADS * dh
    outs = ((ck_ref, None), (cv_ref, None), (sk_ref, 1), (sv_ref, None), (wk_ref, 2), (wv_ref, None))
    for idx, (ref, gain_row) in enumerate(outs):
        for n in range(B_KV_HEADS):
            off = base + (idx * B_KV_HEADS + n) * dh
            t = p_ref[0, :, off:off + dh]
            if gain_row is not None:
                t = _rms(t.astype(F32), kg_ref[gain_row:gain_row + 1, :]).astype(BF16)
            ref[0, n] = t
    gate_off = base + 6 * B_KV_HEADS * dh
    gate = jax.nn.sigmoid(p_ref[0, :, gate_off:gate_off + 3 * N_HEADS].astype(F32))
    per = 3 * B_GROUP
    for n in range(B_KV_HEADS):
        gate_ref[0, n] = gate[:, n * per:(n + 1) * per]


def _b_prep(proj, q_gain, k_gain, batch, seq):
    ts = 256
    dh = B_HEAD_DIM
    kv_shape = jax.ShapeDtypeStruct((batch, B_KV_HEADS, seq, dh), BF16)
    kv_spec = pl.BlockSpec((1, B_KV_HEADS, ts, dh), lambda b, i: (b, 0, i, 0))
    return pl.pallas_call(
        _b_prep_kernel,
        out_shape=(jax.ShapeDtypeStruct((batch, N_HEADS, seq, dh), BF16),) + (kv_shape,) * 6
        + (jax.ShapeDtypeStruct((batch, B_KV_HEADS, seq, 3 * B_GROUP), F32),),
        grid=(batch, seq // ts),
        in_specs=[pl.BlockSpec((1, ts, B_PROJ_WIDTH), lambda b, i: (b, i, 0)),
                  pl.BlockSpec((1, dh), lambda b, i: (0, 0)),
                  pl.BlockSpec((3, dh), lambda b, i: (0, 0))],
        out_specs=(pl.BlockSpec((1, N_HEADS, ts, dh), lambda b, i: (b, 0, i, 0)),) + (kv_spec,) * 6
        + (pl.BlockSpec((1, B_KV_HEADS, ts, 3 * B_GROUP), lambda b, i: (b, 0, i, 0)),),
        compiler_params=_params("parallel", "parallel"),
        name="b_prep",
    )(proj, q_gain.reshape(1, dh), k_gain)


def _b_compress_kernel(tk_ref, tv_ref, pos_ref, w1_ref, w2_ref, kg_ref, kc_ref, vc_ref):
    half = (B_CMP_LEN // 2) * B_HEAD_DIM
    for kv, (t_ref, out_ref) in enumerate(((tk_ref, kc_ref), (tv_ref, vc_ref))):
        t = t_ref[0, 0].astype(F32)
        top = (t + pos_ref[kv, 0:1, :]).astype(BF16)
        bot = (t + pos_ref[kv, 1:2, :]).astype(BF16)
        a1 = _dot(top, w1_ref[kv, :half, :])
        a2 = _dot(bot, w1_ref[kv, half:, :])
        hidden = a1 + pltpu.roll(a2, a2.shape[0] - 1, 0)
        out = _dot(jax.nn.gelu(hidden).astype(BF16), w2_ref[kv])
        if kv == 0:
            out = _rms(out, kg_ref[...])
        out_ref[0, 0] = out.astype(out_ref.dtype)


def _b_compress(ck, cv, cmp_pos, cmp_w1, cmp_w2, k_gain0, batch, seq):
    rows = seq // B_CMP_STRIDE
    half = (B_CMP_LEN // 2) * B_HEAD_DIM
    tk = ck.reshape(batch, B_KV_HEADS, rows, half)
    tv = cv.reshape(batch, B_KV_HEADS, rows, half)
    pos = cmp_pos.reshape(2, 2, half)
    t_spec = pl.BlockSpec((1, 1, rows, half), lambda b, n: (b, n, 0, 0))
    o_spec = pl.BlockSpec((1, 1, rows, B_HEAD_DIM), lambda b, n: (b, n, 0, 0))
    shape = jax.ShapeDtypeStruct((batch, B_KV_HEADS, rows, B_HEAD_DIM), BF16)
    return pl.pallas_call(
        _b_compress_kernel,
        out_shape=(shape, shape),
        grid=(batch, B_KV_HEADS),
        in_specs=[t_spec, t_spec,
                  pl.BlockSpec((2, 2, half), lambda b, n: (0, 0, 0)),
                  pl.BlockSpec((2, 2 * half, B_CMP_HIDDEN), lambda b, n: (0, 0, 0)),
                  pl.BlockSpec((2, B_CMP_HIDDEN, B_HEAD_DIM), lambda b, n: (0, 0, 0)),
                  pl.BlockSpec((1, B_HEAD_DIM), lambda b, n: (0, 0))],
        out_specs=(o_spec, o_spec),
        compiler_params=_params("parallel", "parallel"),
        name="b_compress",
    )(tk, tv, pos, cmp_w1.astype(BF16), cmp_w2.astype(BF16), k_gain0.reshape(1, -1))


def _b_cmp_attn_kernel(q_ref, kc_ref, vc_ref, bias_ref, c2s_ref, oc_ref, sel_ref, imp_ref, *, top_n):
    tq = B_TILE
    n_sel = imp_ref.shape[0]
    q = q_ref[0].reshape(B_GROUP * tq, B_HEAD_DIM)
    bias = bias_ref[:, 0].reshape(B_GROUP * tq, -1)
    s = _dot_nt(q, kc_ref[0, 0]) + bias
    valid = bias > 0.5 * NEG_INF
    m = jnp.max(s, axis=-1, keepdims=True)
    e = jnp.where(valid, jnp.exp(s - m), 0.0)
    z = jnp.maximum(jnp.sum(e, axis=-1, keepdims=True), TINY)
    p = e * (1.0 / z)
    oc_ref[0] = _dot(p.astype(BF16), vc_ref[0, 0]).reshape(B_GROUP, tq, B_HEAD_DIM)

    p_sum = p[0:tq] + p[tq:2 * tq] + p[2 * tq:3 * tq] + p[3 * tq:4 * tq]
    hi = p_sum.astype(BF16)
    lo = (p_sum - hi.astype(F32)).astype(BF16)
    c2s = c2s_ref[...]
    imp = _dot_nt(c2s, hi) + _dot_nt(c2s, lo)

    t = pl.program_id(2) * tq + lax.broadcasted_iota(jnp.int32, (n_sel, tq), 1)
    blk = lax.broadcasted_iota(jnp.int32, (n_sel, tq), 0)
    cur = t // B_SEL_BLOCK
    forced = (blk == 0) | (blk == cur) | (blk == cur - 1)
    imp = jnp.where(forced, FORCE_SCORE, jnp.where(blk * B_SEL_BLOCK <= t, imp, NEG_INF))
    imp_ref[...] = imp

    def count(i, rank):
        row = imp_ref[pl.ds(i, 1), :]
        ahead = jnp.where(row > imp, 1.0, jnp.where(row == imp, jnp.where(blk > i, 1.0, 0.0), 0.0))
        return rank + ahead

    rank = lax.fori_loop(0, n_sel, count, jnp.zeros((n_sel, tq), F32))
    sel_ref[0, 0] = jnp.where(rank < top_n, 0.0, NEG_INF).astype(sel_ref.dtype)


def _b_cmp_attn(q, kc, vc, bias_c, batch, seq):
    n_sel = seq // B_SEL_BLOCK
    n_cmp_pad = seq // B_CMP_STRIDE
    n_cmp = (seq - B_CMP_LEN) // B_CMP_STRIDE + 1
    c = np.arange(n_cmp_pad)[None, :] * B_CMP_STRIDE
    j = np.arange(n_sel)[:, None] * B_SEL_BLOCK
    c2s = ((c < j + B_SEL_BLOCK) & (c + B_CMP_LEN > j) & (np.arange(n_cmp_pad)[None, :] < n_cmp)).astype(np.float32)
    kern = functools.partial(_b_cmp_attn_kernel, top_n=min(B_TOP_N, n_sel))
    return pl.pallas_call(
        kern,
        out_shape=(jax.ShapeDtypeStruct((batch, N_HEADS, seq, B_HEAD_DIM), F32),
                   jax.ShapeDtypeStruct((batch, B_KV_HEADS, n_sel, seq), BF16)),
        grid=(batch, B_KV_HEADS, seq // B_TILE),
        in_specs=[pl.BlockSpec((1, B_GROUP, B_TILE, B_HEAD_DIM), lambda b, n, i: (b, n, i, 0)),
                  pl.BlockSpec((1, 1, n_cmp_pad, B_HEAD_DIM), lambda b, n, i: (b, n, 0, 0)),
                  pl.BlockSpec((1, 1, n_cmp_pad, B_HEAD_DIM), lambda b, n, i: (b, n, 0, 0)),
                  pl.BlockSpec((B_GROUP, 1, B_TILE, n_cmp_pad), lambda b, n, i: (n, i, 0, 0)),
                  pl.BlockSpec((n_sel, n_cmp_pad), lambda b, n, i: (0, 0))],
        out_specs=(pl.BlockSpec((1, B_GROUP, B_TILE, B_HEAD_DIM), lambda b, n, i: (b, n, i, 0)),
                   pl.BlockSpec((1, 1, n_sel, B_TILE), lambda b, n, i: (b, n, 0, i))),
        scratch_shapes=[pltpu.VMEM((n_sel, B_TILE), F32)],
        compiler_params=_params("parallel", "parallel", "arbitrary"),
        name="b_cmp_attn",
    )(q, kc, vc, bias_c, jnp.asarray(c2s, BF16))


def _b_sparse_kernel(q_ref, sk_ref, sv_ref, wk_ref, wv_ref, sel_ref, exp_ref, bs_ref, bw_ref, oc_ref, gate_ref,
                     o_ref, m_ref, l_ref, acc_ref, *, delta_max, win_tiles):
    tq = B_TILE
    rows = B_GROUP * tq
    qi = pl.program_id(2)
    q = q_ref[0].reshape(rows, B_HEAD_DIM)

    def sweep(k_ref, v_ref, lo, hi, bias_fn):
        m_ref[...] = jnp.full(m_ref.shape, NEG_INF, F32)
        l_ref[...] = jnp.zeros(l_ref.shape, F32)
        acc_ref[...] = jnp.zeros(acc_ref.shape, F32)

        def step(kt, carry):
            start = pl.multiple_of(kt * tq, tq)
            s = _dot_nt(q, k_ref[0, 0, pl.ds(start, tq), :]) + bias_fn(kt, start)
            m_old = m_ref[...]
            m_new = jnp.maximum(m_old, jnp.max(s, axis=-1, keepdims=True))
            alpha = jnp.exp(m_old - m_new)
            p = jnp.exp(s - m_new)
            l_ref[...] = alpha * l_ref[...] + jnp.sum(p, axis=-1, keepdims=True)
            acc_ref[...] = alpha * acc_ref[...] + _dot(p.astype(BF16), v_ref[0, 0, pl.ds(start, tq), :])
            m_ref[...] = m_new
            return carry

        lax.fori_loop(lo, hi, step, 0)
        return acc_ref[...] * (1.0 / l_ref[...])

    def sel_bias(kt, start):
        mask = _dot_tn(sel_ref[0, 0], exp_ref[:, pl.ds(start, tq)])
        d = jnp.minimum(qi - kt, delta_max)
        return (bs_ref[:, d] + mask[None]).reshape(rows, tq)

    def win_bias(kt, start):
        return bw_ref[:, qi - kt].reshape(rows, tq)

    o_s = sweep(sk_ref, sv_ref, 0, qi + 1, sel_bias)
    o_w = sweep(wk_ref, wv_ref, jnp.maximum(qi - (win_tiles - 1), 0), qi + 1, win_bias)

    gate = gate_ref[0, 0]
    for g in range(B_GROUP):
        rs = slice(g * tq, (g + 1) * tq)
        o = (gate[:, 3 * g:3 * g + 1] * oc_ref[0, g] + gate[:, 3 * g + 1:3 * g + 2] * o_s[rs]
             + gate[:, 3 * g + 2:3 * g + 3] * o_w[rs])
        o_ref[0, :, g * B_HEAD_DIM:(g + 1) * B_HEAD_DIM] = o.astype(o_ref.dtype)


def _b_sparse(q, sk, sv, wk, wv, sel, bias_s, bias_w, oc, gate, batch, seq):
    n_sel = seq // B_SEL_BLOCK
    dh = B_HEAD_DIM
    expand = (np.arange(seq)[None, :] // B_SEL_BLOCK == np.arange(n_sel)[:, None]).astype(np.float32)
    n_ds = bias_s.shape[1]
    n_dw = bias_w.shape[1]
    kern = functools.partial(_b_sparse_kernel, delta_max=n_ds - 1, win_tiles=n_dw)
    kv_spec = pl.BlockSpec((1, 1, seq, dh), lambda b, n, i: (b, n, 0, 0))
    rows = B_GROUP * B_TILE
    return pl.pallas_call(
        kern,
        out_shape=jax.ShapeDtypeStruct((batch, seq, N_HEADS * dh), BF16),
        grid=(batch, B_KV_HEADS, seq // B_TILE),
        in_specs=[pl.BlockSpec((1, B_GROUP, B_TILE, dh), lambda b, n, i: (b, n, i, 0)),
                  kv_spec, kv_spec, kv_spec, kv_spec,
                  pl.BlockSpec((1, 1, n_sel, B_TILE), lambda b, n, i: (b, n, 0, i)),
                  pl.BlockSpec((n_sel, seq), lambda b, n, i: (0, 0)),
                  pl.BlockSpec((B_GROUP, n_ds, B_TILE, B_TILE), lambda b, n, i: (n, 0, 0, 0)),
                  pl.BlockSpec((B_GROUP, n_dw, B_TILE, B_TILE), lambda b, n, i: (n, 0, 0, 0)),
                  pl.BlockSpec((1, B_GROUP, B_TILE, dh), lambda b, n, i: (b, n, i, 0)),
                  pl.BlockSpec((1, 1, B_TILE, 3 * B_GROUP), lambda b, n, i: (b, n, i, 0))],
        out_specs=pl.BlockSpec((1, B_TILE, B_GROUP * dh), lambda b, n, i: (b, i, n)),
        scratch_shapes=[pltpu.VMEM((rows, 1), F32), pltpu.VMEM((rows, 1), F32), pltpu.VMEM((rows, dh), F32)],
        compiler_params=_params("parallel", "parallel", "arbitrary"),
        name="b_sparse",
    )(q, sk, sv, wk, wv, sel, jnp.asarray(expand, BF16), bias_s, bias_w, oc, gate)


def _mixer_b(x, rel_bias, norm1, w_in, q_gain, k_gain, cmp_pos, cmp_w1, cmp_w2, w_out, batch, seq):
    w_pad = jnp.pad(w_in, ((0, 0), (0, B_PROJ_WIDTH - w_in.shape[1]))).astype(BF16)
    proj = _norm_matmul(x, norm1, w_pad, BF16, 512).reshape(batch, seq, B_PROJ_WIDTH)
    q, ck, cv, sk, sv, wk, wv, gate = _b_prep(proj, q_gain, k_gain, batch, seq)
    kc, vc = _b_compress(ck, cv, cmp_pos, cmp_w1, cmp_w2, k_gain[0], batch, seq)
    n_tiles = seq // B_TILE
    bias_c = _bias_tiles(rel_bias, n_tiles, B_TILE, seq // B_CMP_STRIDE, base=1 - B_CMP_LEN, tile_step=B_TILE,
                         row_step=1, col_step=-B_CMP_STRIDE, dmax=1 << 30)
    oc, sel = _b_cmp_attn(q, kc, vc, bias_c, batch, seq)
    delta_max = min(n_tiles - 1, -(-(_THRESHOLDS[-1] + B_TILE - 1) // B_TILE))
    bias_s = _bias_tiles(rel_bias, delta_max + 1, B_TILE, B_TILE, base=0, tile_step=B_TILE, row_step=1,
                         col_step=-1, dmax=1 << 30)
    win_tiles = (B_WINDOW - 1 + B_TILE - 1) // B_TILE + 1
    bias_w = _bias_tiles(rel_bias, win_tiles, B_TILE, B_TILE, base=0, tile_step=B_TILE, row_step=1,
                         col_step=-1, dmax=B_WINDOW - 1)
    o = _b_sparse(q, sk, sv, wk, wv, sel, bias_s, bias_w, oc, gate, batch, seq)
    return _matmul_residual(o.reshape(batch * seq, -1), w_out.astype(BF16), x)


def _c_conv_kernel(cur_ref, halo_ref, w_ref, sm_ref, alog_ref, dtb_ref, qkv_ref, bg_ref):
    ts = cur_ref.shape[1]
    keep = jnp.where(pl.program_id(1) == 0, 0.0, 1.0)
    dk = C_HEAD_DIM
    for c in range(3 * C_HEADS):
        sl = slice(c * dk, (c + 1) * dk)
        xe = jnp.concatenate([halo_ref[0, :, sl].astype(F32) * keep, cur_ref[0, :, sl].astype(F32)], axis=0)
        y = None
        for j in range(C_CONV):
            off = 8 - (C_CONV - 1) + j
            term = w_ref[j:j + 1, sl] * xe[off:off + ts]
            y = term if y is None else y + term
        y = y * jax.nn.sigmoid(y)
        if c < 2 * C_HEADS:
            y = y * lax.rsqrt(jnp.sum(y * y, axis=-1, keepdims=True) + RMS_EPS)
        if c < C_HEADS:
            y = y * (dk ** -0.5)
        qkv_ref[0, :, sl] = y
    sm = sm_ref[0]
    a = sm + dtb_ref[...]
    softplus = jnp.maximum(a, 0.0) + jnp.log1p(jnp.exp(-jnp.abs(a)))
    g = -jnp.exp(alog_ref[...]) * softplus
    lane = lax.broadcasted_iota(jnp.int32, sm.shape, 1)
    bg_ref[0] = jnp.where(lane < C_HEADS, jax.nn.sigmoid(sm), g)


def _c_conv(proj, small, conv_w, a_log, dt_bias, batch, seq):
    ts = 256
    width = 3 * C_WIDTH
    pad = lambda v: jnp.pad(v, (C_HEADS, 128 - 2 * C_HEADS)).reshape(1, 128)
    return pl.pallas_call(
        _c_conv_kernel,
        out_shape=(jax.ShapeDtypeStruct((batch, seq, width), F32),
                   jax.ShapeDtypeStruct((batch, seq, 128), F32)),
        grid=(batch, seq // ts),
        in_specs=[pl.BlockSpec((1, ts, width), lambda b, i: (b, i, 0)),
                  pl.BlockSpec((1, 8, width), lambda b, i: (b, jnp.maximum(i * (ts // 8) - 1, 0), 0)),
                  pl.BlockSpec((C_CONV, width), lambda b, i: (0, 0)),
                  pl.BlockSpec((1, ts, 128), lambda b, i: (b, i, 0)),
                  pl.BlockSpec((1, 128), lambda b, i: (0, 0)),
                  pl.BlockSpec((1, 128), lambda b, i: (0, 0))],
        out_specs=(pl.BlockSpec((1, ts, width), lambda b, i: (b, i, 0)),
                   pl.BlockSpec((1, ts, 128), lambda b, i: (b, i, 0))),
        compiler_params=_params("parallel", "arbitrary"),
        name="c_conv",
    )(proj, proj, conv_w, small, pad(a_log), pad(dt_bias))


def _c_chunk_kernel(qkv_ref, bg_ref, bgt_ref, ltri_ref, utri_ref, u_ref, w_ref, qg_ref, kg_ref, attn_ref, gc_ref):
    cs = C_CHUNK
    dk = C_HEAD_DIM
    row = lax.broadcasted_iota(jnp.int32, (cs, cs), 0)
    col = lax.broadcasted_iota(jnp.int32, (cs, cs), 1)
    causal = row >= col
    strict = row > col
    eye = jnp.where(row == col, 1.0, 0.0)

    def chunk(r, carry):
        rs = pl.ds(pl.multiple_of(r * cs, cs), cs)
        bgc = bg_ref[0, rs, :]
        gcum_col = _dot(ltri_ref[...], bgc, HIGHEST)
        gcum_row = _dot(bgt_ref[0, r], utri_ref[...], HIGHEST)
        gc_ref[0, rs, :] = gcum_col
        for h in range(C_HEADS):
            gc = gcum_col[:, C_HEADS + h:C_HEADS + h + 1]
            gr = gcum_row[C_HEADS + h:C_HEADS + h + 1, :]
            beta = bgc[:, h:h + 1]
            q = qkv_ref[0, rs, h * dk:(h + 1) * dk]
            k = qkv_ref[0, rs, C_WIDTH + h * dk:C_WIDTH + (h + 1) * dk]
            v = qkv_ref[0, rs, 2 * C_WIDTH + h * dk:2 * C_WIDTH + (h + 1) * dk]
            decay = jnp.exp(jnp.where(causal, gc - gr, NEG_INF))
            kb = k * beta
            low = jnp.where(strict, _dot_nt(kb, k, HIGHEST) * decay, 0.0)
            t_mat = eye - low
            power = low
            for _ in range(int(math.log2(cs)) - 1):
                power = _dot(power, power, HIGHEST)
                t_mat = t_mat + _dot(t_mat, power, HIGHEST)
            egc = jnp.exp(gc)
            u_ref[0, rs, h * dk:(h + 1) * dk] = _dot(t_mat, v * beta, HIGHEST)
            w_ref[0, rs, h * dk:(h + 1) * dk] = _dot(t_mat, kb * egc, HIGHEST)
            attn_ref[0, rs, h * cs:(h + 1) * cs] = jnp.where(causal, _dot_nt(q, k, HIGHEST), 0.0) * decay
            qg_ref[0, rs, h * dk:(h + 1) * dk] = q * egc
            kg_ref[0, rs, h * dk:(h + 1) * dk] = k * jnp.exp(gc[cs - 1:cs, :] - gc)
        return carry

    lax.fori_loop(0, C_CHUNKS_PER_STEP, chunk, 0)


def _c_chunks(qkv, bg, bgt, batch, seq):
    rc = C_CHUNKS_PER_STEP * C_CHUNK
    tri = np.tril(np.ones((C_CHUNK, C_CHUNK), np.float32))
    wide = lambda width: pl.BlockSpec((1, rc, width), lambda b, i: (b, i, 0))
    shape = lambda width: jax.ShapeDtypeStruct((batch, seq, width), F32)
    return pl.pallas_call(
        _c_chunk_kernel,
        out_shape=(shape(C_WIDTH),) * 4 + (shape(C_HEADS * C_CHUNK), shape(128)),
        grid=(batch, seq // rc),
        in_specs=[wide(3 * C_WIDTH), wide(128),
                  pl.BlockSpec((1, C_CHUNKS_PER_STEP, 2 * C_HEADS, C_CHUNK), lambda b, i: (b, i, 0, 0)),
                  pl.BlockSpec((C_CHUNK, C_CHUNK), lambda b, i: (0, 0)),
                  pl.BlockSpec((C_CHUNK, C_CHUNK), lambda b, i: (0, 0))],
        out_specs=(wide(C_WIDTH),) * 4 + (wide(C_HEADS * C_CHUNK), wide(128)),
        compiler_params=_params("parallel", "parallel"),
        name="c_chunks",
    )(qkv, bg, bgt, jnp.asarray(tri), jnp.asarray(tri.T))


def _c_scan_kernel(u_ref, w_ref, qg_ref, kg_ref, attn_ref, gc_ref, o_ref, state_ref):
    @pl.when(pl.program_id(1) == 0)
    def _():
        state_ref[...] = jnp.zeros_like(state_ref)

    cs = C_CHUNK
    dk = C_HEAD_DIM
    decay_last = jnp.exp(gc_ref[0, cs - 1:cs, :])
    for h in range(C_HEADS):
        sl = slice(h * dk, (h + 1) * dk)
        state = state_ref[h]
        v_new = u_ref[0, :, sl] - _dot(w_ref[0, :, sl], state, HIGHEST)
        o_ref[0, :, sl] = (_dot(qg_ref[0, :, sl], state, HIGHEST)
                           + _dot(attn_ref[0, :, h * cs:(h + 1) * cs], v_new, HIGHEST))
        state_ref[h] = (state * decay_last[:, C_HEADS + h:C_HEADS + h + 1]
                        + _dot_tn(kg_ref[0, :, sl], v_new, HIGHEST))


def _c_scan(u, w, qg, kg, attn, gc, batch, seq):
    wide = lambda width: pl.BlockSpec((1, C_CHUNK, width), lambda b, c: (b, c, 0))
    return pl.pallas_call(
        _c_scan_kernel,
        out_shape=jax.ShapeDtypeStruct((batch, seq, C_WIDTH), F32),
        grid=(batch, seq // C_CHUNK),
        in_specs=[wide(C_WIDTH)] * 4 + [wide(C_HEADS * C_CHUNK), wide(128)],
        out_specs=wide(C_WIDTH),
        scratch_shapes=[pltpu.VMEM((C_HEADS, C_HEAD_DIM, C_HEAD_DIM), F32)],
        compiler_params=_params("parallel", "arbitrary"),
        name="c_scan",
    )(u, w, qg, kg, attn, gc)


def _c_out_kernel(o_ref, z_ref, g_ref, w_ref, x_ref, out_ref):
    dk = C_HEAD_DIM
    parts = []
    for h in range(C_HEADS):
        sl = slice(h * dk, (h + 1) * dk)
        z = z_ref[:, sl].astype(F32)
        parts.append((_rms(o_ref[:, sl], g_ref[...]) * (z * jax.nn.sigmoid(z))).astype(BF16))
    out_ref[...] = x_ref[...] + _dot(jnp.concatenate(parts, axis=-1), w_ref[...])


def _c_out(o, proj, out_gain, w_out, x):
    m, d = x.shape
    z_block = (3 * C_WIDTH) // C_WIDTH
    row = lambda width: pl.BlockSpec((ROW_TILE, width), lambda i: (i, 0))
    return pl.pallas_call(
        _c_out_kernel,
        out_shape=jax.ShapeDtypeStruct((m, d), F32),
        grid=(m // ROW_TILE,),
        in_specs=[row(C_WIDTH),
                  pl.BlockSpec((ROW_TILE, C_WIDTH), lambda i: (i, z_block)),
                  pl.BlockSpec((1, C_HEAD_DIM), lambda i: (0, 0)),
                  pl.BlockSpec((C_WIDTH, d), lambda i: (0, 0)),
                  row(d)],
        out_specs=row(d),
        compiler_params=_params("parallel"),
        name="c_out",
    )(o, proj, out_gain.reshape(1, -1), w_out, x)


def _mixer_c(x, norm1, w_in, conv_w, a_log, dt_bias, out_gain, w_out, batch, seq):
    main = 4 * C_WIDTH
    proj = _norm_matmul(x, norm1, w_in[:, :main].astype(BF16), BF16, 512)
    w_small = jnp.pad(w_in[:, main:], ((0, 0), (0, 128 - 2 * C_HEADS))).astype(BF16)
    small = _norm_matmul(x, norm1, w_small, F32, 128)
    qkv, bg = _c_conv(proj.reshape(batch, seq, main), small.reshape(batch, seq, 128), conv_w, a_log, dt_bias,
                      batch, seq)
    bgt = bg[:, :, :2 * C_HEADS].reshape(batch, seq // C_CHUNK, C_CHUNK, 2 * C_HEADS).transpose(0, 1, 3, 2)
    u, w, qg, kg, attn, gc = _c_chunks(qkv, bg, bgt, batch, seq)
    o = _c_scan(u, w, qg, kg, attn, gc, batch, seq)
    return _c_out(o.reshape(batch * seq, C_WIDTH), proj, out_gain, w_out.astype(BF16), x)


def kernel(x, rel_bias, l0_norm1, l0_a_w_in, l0_a_q_gain, l0_a_k_gain, l0_a_w_out, l0_norm2, l0_ffn_w_gate, l0_ffn_w_up, l0_ffn_w_down, l1_norm1, l1_b_w_in, l1_b_q_gain, l1_b_k_gain, l1_b_cmp_pos, l1_b_cmp_w1, l1_b_cmp_w2, l1_b_w_out, l1_norm2, l1_ffn_w_gate, l1_ffn_w_up, l1_ffn_w_down, l2_norm1, l2_c_w_in, l2_c_conv_w, l2_c_a_log, l2_c_dt_bias, l2_c_out_gain, l2_c_w_out, l2_norm2, l2_ffn_w_gate, l2_ffn_w_up, l2_ffn_w_down, l3_norm1, l3_a_w_in, l3_a_q_gain, l3_a_k_gain, l3_a_w_out, l3_norm2, l3_ffn_w_gate, l3_ffn_w_up, l3_ffn_w_down):
    batch, seq, d = x.shape
    h = x.reshape(batch * seq, d)

    def ffn(h, norm2, w_gate, w_up, w_down):
        return _ffn(h, norm2, w_gate.astype(BF16), w_up.astype(BF16), w_down.astype(BF16))

    h = _mixer_a(h, rel_bias, l0_norm1, l0_a_w_in, l0_a_q_gain, l0_a_k_gain, l0_a_w_out, batch, seq)
    h = ffn(h, l0_norm2, l0_ffn_w_gate, l0_ffn_w_up, l0_ffn_w_down)
    h = _mixer_b(h, rel_bias, l1_norm1, l1_b_w_in, l1_b_q_gain, l1_b_k_gain, l1_b_cmp_pos, l1_b_cmp_w1,
                 l1_b_cmp_w2, l1_b_w_out, batch, seq)
    h = ffn(h, l1_norm2, l1_ffn_w_gate, l1_ffn_w_up, l1_ffn_w_down)
    h = _mixer_c(h, l2_norm1, l2_c_w_in, l2_c_conv_w, l2_c_a_log, l2_c_dt_bias, l2_c_out_gain, l2_c_w_out,
                 batch, seq)
    h = ffn(h, l2_norm2, l2_ffn_w_gate, l2_ffn_w_up, l2_ffn_w_down)
    h = _mixer_a(h, rel_bias, l3_norm1, l3_a_w_in, l3_a_q_gain, l3_a_k_gain, l3_a_w_out, batch, seq)
    h = ffn(h, l3_norm2, l3_ffn_w_gate, l3_ffn_w_up, l3_ffn_w_down)
    return h.reshape(batch, seq, d)
```

```python
import functools
import math

import numpy as np
import jax
import jax.numpy as jnp
from jax import lax
from jax.experimental import pallas as pl
from jax.experimental.pallas import tpu as pltpu

D_MODEL = 1024
RMS_EPS = 1e-6
NEG_INF = -1e30
TINY = 1e-30
FORCE_SCORE = 1e9

N_BUCKETS = 32
REL_MAX_DISTANCE = 2048
N_HEADS = 16

A_GROUPS = ((128, 1), (512, 4), (2048, 16))
A_HEAD_DIM = 64
A_Q_BLOCK = 128
A_PROJ_TILE = 512

B_KV_HEADS = 4
B_GROUP = 4
B_HEAD_DIM = 64
B_CMP_LEN = 32
B_CMP_STRIDE = 16
B_CMP_HIDDEN = 256
B_SEL_BLOCK = 64
B_TOP_N = 16
B_WINDOW = 512
B_TILE = 128
B_PROJ_WIDTH = 3072

C_HEADS = 8
C_HEAD_DIM = 128
C_WIDTH = C_HEADS * C_HEAD_DIM
C_CONV = 4
C_CHUNK = 64
C_CHUNKS_PER_STEP = 4

FFN_HIDDEN = 2816
FFN_TILE = 256

ROW_TILE = 512
VMEM_LIMIT = 48 * 1024 * 1024

F32 = jnp.float32
BF16 = jnp.bfloat16
HIGHEST = lax.Precision.HIGHEST

NT_DIMS = (((1,), (1,)), ((), ()))
TN_DIMS = (((0,), (0,)), ((), ()))


def _params(*semantics):
    return pltpu.CompilerParams(dimension_semantics=semantics, vmem_limit_bytes=VMEM_LIMIT)


def _dot(a, b, precision=None):
    return jnp.dot(a, b, preferred_element_type=F32, precision=precision)


def _dot_nt(a, b, precision=None):
    return lax.dot_general(a, b, NT_DIMS, preferred_element_type=F32, precision=precision)


def _dot_tn(a, b, precision=None):
    return lax.dot_general(a, b, TN_DIMS, preferred_element_type=F32, precision=precision)


def _rms(x, gain):
    return x * lax.rsqrt(jnp.mean(x * x, axis=-1, keepdims=True) + RMS_EPS) * gain


def _bucket_thresholds():
    d = np.arange(1 << 15)
    max_exact = N_BUCKETS // 2
    d_f = np.maximum(d, 1).astype(np.float32)
    large = max_exact + (np.log(d_f / np.float32(max_exact)) / np.float32(math.log(REL_MAX_DISTANCE / max_exact))
                         * np.float32(N_BUCKETS - max_exact)).astype(np.int32)
    bucket = np.where(d < max_exact, d, np.minimum(large, N_BUCKETS - 1))
    return [int(np.argmax(bucket >= k)) if np.any(bucket >= k) else int(1 << 30) for k in range(N_BUCKETS)]


_THRESHOLDS = _bucket_thresholds()


def _bias_tile_kernel(tbl_ref, o_ref, *, base, tile_step, row_step, col_step, dmax, dil):
    h = pl.program_id(0)
    t = pl.program_id(1)
    shape = o_ref.shape[2:]
    i = lax.broadcasted_iota(jnp.int32, shape, 0)
    j = lax.broadcasted_iota(jnp.int32, shape, 1)
    dist = base + tile_step * t + row_step * i + col_step * j
    d = dist * dil
    val = jnp.full(shape, tbl_ref[0, h], F32)
    for k in range(1, N_BUCKETS):
        val = jnp.where(d >= _THRESHOLDS[k], tbl_ref[k, h], val)
    valid = (dist >= 0) & (dist <= dmax)
    o_ref[0, 0] = jnp.where(valid, val, NEG_INF)


def _bias_tiles(rel_bias, n_tiles, rows, cols, *, base, tile_step, row_step, col_step, dmax, dil=1):
    kern = functools.partial(_bias_tile_kernel, base=base, tile_step=tile_step, row_step=row_step,
                             col_step=col_step, dmax=dmax, dil=dil)
    return pl.pallas_call(
        kern,
        out_shape=jax.ShapeDtypeStruct((N_HEADS, n_tiles, rows, cols), F32),
        grid=(N_HEADS, n_tiles),
        in_specs=[pl.BlockSpec(memory_space=pltpu.SMEM)],
        out_specs=pl.BlockSpec((1, 1, rows, cols), lambda h, t: (h, t, 0, 0)),
        compiler_params=_params("parallel", "parallel"),
        name="bias_tiles",
    )(rel_bias)


def _norm_matmul_kernel(x_ref, g_ref, w_ref, o_ref, h_ref):
    @pl.when(pl.program_id(1) == 0)
    def _():
        h_ref[...] = _rms(x_ref[...], g_ref[...]).astype(BF16)

    o_ref[...] = _dot(h_ref[...], w_ref[...]).astype(o_ref.dtype)


def _norm_matmul(x, gain, w, out_dtype, tn):
    m, d = x.shape
    n = w.shape[1]
    return pl.pallas_call(
        _norm_matmul_kernel,
        out_shape=jax.ShapeDtypeStruct((m, n), out_dtype),
        grid=(m // ROW_TILE, n // tn),
        in_specs=[pl.BlockSpec((ROW_TILE, d), lambda i, j: (i, 0)),
                  pl.BlockSpec((1, d), lambda i, j: (0, 0)),
                  pl.BlockSpec((d, tn), lambda i, j: (0, j))],
        out_specs=pl.BlockSpec((ROW_TILE, tn), lambda i, j: (i, j)),
        scratch_shapes=[pltpu.VMEM((ROW_TILE, d), BF16)],
        compiler_params=_params("parallel", "arbitrary"),
        name="norm_matmul",
    )(x, gain.reshape(1, d), w)


def _matmul_residual_kernel(a_ref, w_ref, x_ref, o_ref):
    o_ref[...] = x_ref[...] + _dot(a_ref[...], w_ref[...])


def _matmul_residual(a, w, x):
    m, k = a.shape
    d = w.shape[1]
    return pl.pallas_call(
        _matmul_residual_kernel,
        out_shape=jax.ShapeDtypeStruct((m, d), F32),
        grid=(m // ROW_TILE,),
        in_specs=[pl.BlockSpec((ROW_TILE, k), lambda i: (i, 0)),
                  pl.BlockSpec((k, d), lambda i: (0, 0)),
                  pl.BlockSpec((ROW_TILE, d), lambda i: (i, 0))],
        out_specs=pl.BlockSpec((ROW_TILE, d), lambda i: (i, 0)),
        compiler_params=_params("parallel"),
        name="matmul_residual",
    )(a, w, x)


def _ffn_kernel(x_ref, g_ref, wg_ref, wu_ref, wd_ref, o_ref, h_ref, acc_ref):
    j = pl.program_id(1)

    @pl.when(j == 0)
    def _():
        h_ref[...] = _rms(x_ref[...], g_ref[...]).astype(BF16)
        acc_ref[...] = jnp.zeros_like(acc_ref)

    h = h_ref[...]
    a = _dot(h, wg_ref[...])
    b = _dot(h, wu_ref[...])
    t = (a * jax.nn.sigmoid(a) * b).astype(BF16)
    acc_ref[...] += _dot(t, wd_ref[...])

    @pl.when(j == pl.num_programs(1) - 1)
    def _():
        o_ref[...] = x_ref[...] + acc_ref[...]


def _ffn(x, gain, w_gate, w_up, w_down):
    m, d = x.shape
    hidden = w_gate.shape[1]
    return pl.pallas_call(
        _ffn_kernel,
        out_shape=jax.ShapeDtypeStruct((m, d), F32),
        grid=(m // ROW_TILE, hidden // FFN_TILE),
        in_specs=[pl.BlockSpec((ROW_TILE, d), lambda i, j: (i, 0)),
                  pl.BlockSpec((1, d), lambda i, j: (0, 0)),
                  pl.BlockSpec((d, FFN_TILE), lambda i, j: (0, j)),
                  pl.BlockSpec((d, FFN_TILE), lambda i, j: (0, j)),
                  pl.BlockSpec((FFN_TILE, d), lambda i, j: (j, 0))],
        out_specs=pl.BlockSpec((ROW_TILE, d), lambda i, j: (i, 0)),
        scratch_shapes=[pltpu.VMEM((ROW_TILE, d), BF16), pltpu.VMEM((ROW_TILE, d), F32)],
        compiler_params=_params("parallel", "arbitrary"),
        name="ffn",
    )(x, gain.reshape(1, d), w_gate, w_up, w_down)


def _a_proj_kernel(x_ref, g_ref, w_ref, qg_ref, kg_ref, o_ref, h_ref, x_scr, *, dil):
    rows = ROW_TILE // dil
    xn = _rms(x_ref[...], g_ref[...])
    if dil == 1:
        h_ref[...] = xn.astype(BF16)
    else:
        slabs = xn.shape[1] // 128
        for c in range(slabs):
            x_scr[c] = xn[:, c * 128:(c + 1) * 128]
        for r in range(dil):
            picked = [x_scr[c, pl.ds(r, rows, stride=dil), :] for c in range(slabs)]
            h_ref[r * rows:(r + 1) * rows, :] = jnp.concatenate(picked, axis=1).astype(BF16)
    h = h_ref[...]
    width = w_ref.shape[1]
    hd = width // 3
    low = lax.broadcasted_iota(jnp.int32, (ROW_TILE, 128), 1) < A_HEAD_DIM
    for j in range(width // A_PROJ_TILE):
        res = _dot(h, w_ref[:, j * A_PROJ_TILE:(j + 1) * A_PROJ_TILE])
        kind = (j * A_PROJ_TILE) // hd
        if kind < 2:
            parts = []
            for c in range(A_PROJ_TILE // 128):
                y = res[:, c * 128:(c + 1) * 128]
                sq = y * y
                tot = jnp.sum(sq, axis=-1, keepdims=True)
                lo = jnp.sum(jnp.where(low, sq, 0.0), axis=-1, keepdims=True)
                ss = jnp.where(low, lo, tot - lo)
                parts.append(y * lax.rsqrt(ss * (1.0 / A_HEAD_DIM) + RMS_EPS))
            res = jnp.concatenate(parts, axis=1) * (qg_ref if kind == 0 else kg_ref)[...]
        res = res.astype(BF16)
        for r in range(dil):
            off = r * width + j * A_PROJ_TILE
            o_ref[:, off:off + A_PROJ_TILE] = res[r * rows:(r + 1) * rows]


def _a_proj(x, gain, w, q_gain, k_gain, dil):
    m, d = x.shape
    width = w.shape[1]
    reps = A_PROJ_TILE // A_HEAD_DIM
    qg = jnp.tile(q_gain * (A_HEAD_DIM ** -0.5), reps).reshape(1, A_PROJ_TILE)
    kg = jnp.tile(k_gain, reps).reshape(1, A_PROJ_TILE)
    return pl.pallas_call(
        functools.partial(_a_proj_kernel, dil=dil),
        out_shape=jax.ShapeDtypeStruct((m // dil, dil * width), BF16),
        grid=(m // ROW_TILE,),
        in_specs=[pl.BlockSpec((ROW_TILE, d), lambda i: (i, 0)),
                  pl.BlockSpec((1, d), lambda i: (0, 0)),
                  pl.BlockSpec((d, width), lambda i: (0, 0)),
                  pl.BlockSpec((1, A_PROJ_TILE), lambda i: (0, 0)),
                  pl.BlockSpec((1, A_PROJ_TILE), lambda i: (0, 0))],
        out_specs=pl.BlockSpec((ROW_TILE // dil, dil * width), lambda i: (i, 0)),
        scratch_shapes=[pltpu.VMEM((ROW_TILE, d), BF16), pltpu.VMEM((d // 128, ROW_TILE, 128), F32)],
        compiler_params=_params("parallel"),
        name="a_proj",
    )(x, gain.reshape(1, d), w, qg, kg)


def _a_attn_kernel(q_ref, kp_ref, kc_ref, vp_ref, vc_ref, bias_ref, o_ref, lse_ref):
    first = (pl.program_id(2) == 0).astype(jnp.int32)
    lane = lax.broadcasted_iota(jnp.int32, (A_Q_BLOCK, 128), 1)
    low = lane < A_HEAD_DIM
    ones = jnp.ones((A_Q_BLOCK, 128), BF16)
    lse_tile = jnp.zeros((A_Q_BLOCK, 128), F32)
    for pair in range(N_HEADS // 2):
        sl = slice(pair * 128, (pair + 1) * 128)
        q = q_ref[0, :, sl]
        kp = kp_ref[0, :, sl]
        kc = kc_ref[0, :, sl]
        vp = jnp.concatenate([vp_ref[0, :, sl], ones], axis=1)
        vc = jnp.concatenate([vc_ref[0, :, sl], ones], axis=1)
        halves = []
        for half in range(2):
            h = 2 * pair + half
            qm = jnp.where(low if half == 0 else jnp.logical_not(low), q, jnp.zeros_like(q))
            bias = bias_ref[h + N_HEADS * first]
            sp = _dot_nt(qm, kp) + bias[:, :A_Q_BLOCK]
            sc = _dot_nt(qm, kc) + bias[:, A_Q_BLOCK:]
            m = jnp.max(jnp.maximum(sp, sc), axis=-1, keepdims=True)
            acc = _dot(jnp.exp(sp - m).astype(BF16), vp) + _dot(jnp.exp(sc - m).astype(BF16), vc)
            z = acc[:, 128:]
            halves.append(acc[:, :128] * (1.0 / z))
            lse_tile = jnp.where(lane == h, m + jnp.log(z), lse_tile)
        o_ref[0, :, sl] = jnp.where(low, halves[0], halves[1]).astype(o_ref.dtype)
    lse_ref[0] = lse_tile


def _a_attention(proj, bias, dil, batch, seq):
    length = seq // dil
    nblk = length // A_Q_BLOCK
    hd = N_HEADS * A_HEAD_DIM
    pv = proj.reshape(batch, length, dil * 3 * hd)

    def spec(off, prev):
        if prev:
            return pl.BlockSpec((1, A_Q_BLOCK, hd), lambda b, r, i: (b, jnp.maximum(i - 1, 0), r * 3 + off))
        return pl.BlockSpec((1, A_Q_BLOCK, hd), lambda b, r, i: (b, i, r * 3 + off))

    o, lse = pl.pallas_call(
        _a_attn_kernel,
        out_shape=(jax.ShapeDtypeStruct((batch, length, dil * hd), BF16),
                   jax.ShapeDtypeStruct((batch, length, dil * 128), F32)),
        grid=(batch, dil, nblk),
        in_specs=[spec(0, False), spec(1, True), spec(1, False), spec(2, True), spec(2, False),
                  pl.BlockSpec((2 * N_HEADS, A_Q_BLOCK, 2 * A_Q_BLOCK), lambda b, r, i: (0, 0, 0))],
        out_specs=(pl.BlockSpec((1, A_Q_BLOCK, hd), lambda b, r, i: (b, i, r)),
                   pl.BlockSpec((1, A_Q_BLOCK, 128), lambda b, r, i: (b, i, r))),
        compiler_params=_params("parallel", "parallel", "arbitrary"),
        name="a_attention",
    )(pv, pv, pv, pv, pv, bias)
    return o.reshape(batch * length, dil * hd), lse.reshape(batch * length, dil * 128)


def _a_out_kernel(o0_ref, o1_ref, o2_ref, l0_ref, l1_ref, l2_ref, e_ref, w_ref, x_ref, out_ref, o_scr, l_scr):
    hd = N_HEADS * A_HEAD_DIM
    for g, (o_ref, l_ref) in enumerate(((o0_ref, l0_ref), (o1_ref, l1_ref), (o2_ref, l2_ref))):
        dil = A_GROUPS[g][1]
        rows = ROW_TILE // dil
        for r in range(dil):
            dst = pl.ds(r, rows, stride=dil) if dil > 1 else slice(None)
            l_scr[g, dst, :] = l_ref[:, r * 128:(r + 1) * 128]
            for c in range(hd // 128):
                o_scr[g, c, dst, :] = o_ref[:, r * hd + c * 128:r * hd + (c + 1) * 128].astype(F32)
    ls = [l_scr[g] for g in range(len(A_GROUPS))]
    m = jnp.maximum(jnp.maximum(ls[0], ls[1]), ls[2])
    es = [jnp.exp(l - m) for l in ls]
    inv = 1.0 / (es[0] + es[1] + es[2])
    expand = e_ref[...]
    acc = None
    for g, e in enumerate(es):
        wgt = e * inv
        hi = wgt.astype(BF16)
        lo = (wgt - hi.astype(F32)).astype(BF16)
        o_g = jnp.concatenate([o_scr[g, c] for c in range(hd // 128)], axis=1)
        term = (_dot(hi, expand) + _dot(lo, expand)) * o_g
        acc = term if acc is None else acc + term
    out_ref[...] = x_ref[...] + _dot(acc.astype(BF16), w_ref[...])


def _a_out(outs, lses, w_out, x):
    m, d = x.shape
    hd = N_HEADS * A_HEAD_DIM
    expand = np.zeros((128, hd), np.float32)
    for h in range(N_HEADS):
        expand[h, h * A_HEAD_DIM:(h + 1) * A_HEAD_DIM] = 1.0
    grouped = lambda width: [pl.BlockSpec((ROW_TILE // dil, dil * width), lambda i: (i, 0)) for _, dil in A_GROUPS]
    return pl.pallas_call(
        _a_out_kernel,
        out_shape=jax.ShapeDtypeStruct((m, d), F32),
        grid=(m // ROW_TILE,),
        in_specs=grouped(hd) + grouped(128) + [
            pl.BlockSpec((128, hd), lambda i: (0, 0)),
            pl.BlockSpec((hd, d), lambda i: (0, 0)),
            pl.BlockSpec((ROW_TILE, d), lambda i: (i, 0))],
        out_specs=pl.BlockSpec((ROW_TILE, d), lambda i: (i, 0)),
        scratch_shapes=[pltpu.VMEM((len(A_GROUPS), hd // 128, ROW_TILE, 128), F32),
                        pltpu.VMEM((len(A_GROUPS), ROW_TILE, 128), F32)],
        compiler_params=_params("parallel"),
        name="a_out",
    )(*outs, *lses, jnp.asarray(expand, BF16), w_out, x)


def _mixer_a(x, rel_bias, norm1, w_in, q_gain, k_gain, w_out, batch, seq):
    w_in = w_in.astype(BF16)
    group_width = 3 * N_HEADS * A_HEAD_DIM
    outs, lses = [], []
    for gi, (window, dil) in enumerate(A_GROUPS):
        steps = window // dil
        assert steps == A_Q_BLOCK and (seq // dil) % A_Q_BLOCK == 0 and seq % ROW_TILE == 0
        bias = _bias_tiles(rel_bias, 1, A_Q_BLOCK, 2 * A_Q_BLOCK, base=A_Q_BLOCK, tile_step=0, row_step=1,
                           col_step=-1, dmax=steps, dil=dil)[:, 0]
        bias = jnp.concatenate([bias, bias.at[:, :, :A_Q_BLOCK].set(NEG_INF)], axis=0)
        proj = _a_proj(x, norm1, w_in[:, gi * group_width:(gi + 1) * group_width], q_gain[gi], k_gain[gi], dil)
        o, lse = _a_attention(proj, bias, dil, batch, seq)
        outs.append(o)
        lses.append(lse)
    return _a_out(outs, lses, w_out.astype(BF16), x)


def _b_prep_kernel(p_ref, qg_ref, kg_ref, q_ref, ck_ref, cv_ref, sk_ref, sv_ref, wk_ref, wv_ref, gate_ref):
    dh = B_HEAD_DIM
    qg = qg_ref[...] * (dh ** -0.5)
    for h in range(N_HEADS):
        q_ref[0, h] = _rms(p_ref[0, :, h * dh:(h + 1) * dh].astype(F32), qg).astype(BF16)
    base = N_HEADS * dh
    outs = ((ck_ref, None), (cv_ref, None), (sk_ref, 1), (sv_ref, None), (wk_ref, 2), (wv_ref, None))
    for idx, (ref, gain_row) in enumerate(outs):
        for n in range(B_KV_HEADS):
            off = base + (idx * B_KV_HEADS + n) * dh
            t = p_ref[0, :, off:off + dh]
            if gain_row is not None:
                t = _rms(t.astype(F32), kg_ref[gain_row:gain_row + 1, :]).astype(BF16)
            ref[0, n] = t
    gate_off = base + 6 * B_KV_HEADS * dh
    gate = jax.nn.sigmoid(p_ref[0, :, gate_off:gate_off + 3 * N_HEADS].astype(F32))
    per = 3 * B_GROUP
    for n in range(B_KV_HEADS):
        gate_ref[0, n] = gate[:, n * per:(n + 1) * per]


def _b_prep(proj, q_gain, k_gain, batch, seq):
    ts = 256
    dh = B_HEAD_DIM
    kv_shape = jax.ShapeDtypeStruct((batch, B_KV_HEADS, seq, dh), BF16)
    kv_spec = pl.BlockSpec((1, B_KV_HEADS, ts, dh), lambda b, i: (b, 0, i, 0))
    return pl.pallas_call(
        _b_prep_kernel,
        out_shape=(jax.ShapeDtypeStruct((batch, N_HEADS, seq, dh), BF16),) + (kv_shape,) * 6
        + (jax.ShapeDtypeStruct((batch, B_KV_HEADS, seq, 3 * B_GROUP), F32),),
        grid=(batch, seq // ts),
        in_specs=[pl.BlockSpec((1, ts, B_PROJ_WIDTH), lambda b, i: (b, i, 0)),
                  pl.BlockSpec((1, dh), lambda b, i: (0, 0)),
                  pl.BlockSpec((3, dh), lambda b, i: (0, 0))],
        out_specs=(pl.BlockSpec((1, N_HEADS, ts, dh), lambda b, i: (b, 0, i, 0)),) + (kv_spec,) * 6
        + (pl.BlockSpec((1, B_KV_HEADS, ts, 3 * B_GROUP), lambda b, i: (b, 0, i, 0)),),
        compiler_params=_params("parallel", "parallel"),
        name="b_prep",
    )(proj, q_gain.reshape(1, dh), k_gain)


def _b_compress_kernel(tk_ref, tv_ref, pos_ref, w1_ref, w2_ref, kg_ref, kc_ref, vc_ref):
    half = (B_CMP_LEN // 2) * B_HEAD_DIM
    for kv, (t_ref, out_ref) in enumerate(((tk_ref, kc_ref), (tv_ref, vc_ref))):
        t = t_ref[0, 0].astype(F32)
        top = (t + pos_ref[kv, 0:1, :]).astype(BF16)
        bot = (t + pos_ref[kv, 1:2, :]).astype(BF16)
        a1 = _dot(top, w1_ref[kv, :half, :])
        a2 = _dot(bot, w1_ref[kv, half:, :])
        hidden = a1 + pltpu.roll(a2, a2.shape[0] - 1, 0)
        out = _dot(jax.nn.gelu(hidden).astype(BF16), w2_ref[kv])
        if kv == 0:
            out = _rms(out, kg_ref[...])
        out_ref[0, 0] = out.astype(out_ref.dtype)


def _b_compress(ck, cv, cmp_pos, cmp_w1, cmp_w2, k_gain0, batch, seq):
    rows = seq // B_CMP_STRIDE
    half = (B_CMP_LEN // 2) * B_HEAD_DIM
    tk = ck.reshape(batch, B_KV_HEADS, rows, half)
    tv = cv.reshape(batch, B_KV_HEADS, rows, half)
    pos = cmp_pos.reshape(2, 2, half)
    t_spec = pl.BlockSpec((1, 1, rows, half), lambda b, n: (b, n, 0, 0))
    o_spec = pl.BlockSpec((1, 1, rows, B_HEAD_DIM), lambda b, n: (b, n, 0, 0))
    shape = jax.ShapeDtypeStruct((batch, B_KV_HEADS, rows, B_HEAD_DIM), BF16)
    return pl.pallas_call(
        _b_compress_kernel,
        out_shape=(shape, shape),
        grid=(batch, B_KV_HEADS),
        in_specs=[t_spec, t_spec,
                  pl.BlockSpec((2, 2, half), lambda b, n: (0, 0, 0)),
                  pl.BlockSpec((2, 2 * half, B_CMP_HIDDEN), lambda b, n: (0, 0, 0)),
                  pl.BlockSpec((2, B_CMP_HIDDEN, B_HEAD_DIM), lambda b, n: (0, 0, 0)),
                  pl.BlockSpec((1, B_HEAD_DIM), lambda b, n: (0, 0))],
        out_specs=(o_spec, o_spec),
        compiler_params=_params("parallel", "parallel"),
        name="b_compress",
    )(tk, tv, pos, cmp_w1.astype(BF16), cmp_w2.astype(BF16), k_gain0.reshape(1, -1))


def _b_cmp_attn_kernel(q_ref, kc_ref, vct_ref, bias_ref, c2s_ref, oc_ref, sel_ref, imp_ref, *, top_n):
    tq = B_TILE
    n_sel = imp_ref.shape[0]
    q = q_ref[0].reshape(B_GROUP * tq, B_HEAD_DIM)
    bias = bias_ref[:, 0].reshape(B_GROUP * tq, -1)
    s = _dot_nt(q, kc_ref[0, 0]) + bias
    valid = bias > 0.5 * NEG_INF
    m = jnp.max(s, axis=-1, keepdims=True)
    e = jnp.where(valid, jnp.exp(s - m), 0.0)
    z = jnp.maximum(jnp.sum(e, axis=-1, keepdims=True), TINY)
    p = e * (1.0 / z)
    oct = _dot_nt(vct_ref[0, 0], p.astype(BF16))
    for g in range(B_GROUP):
        oc_ref[0, g] = oct[:, g * tq:(g + 1) * tq]

    p_sum = p[0:tq] + p[tq:2 * tq] + p[2 * tq:3 * tq] + p[3 * tq:4 * tq]
    hi = p_sum.astype(BF16)
    lo = (p_sum - hi.astype(F32)).astype(BF16)
    c2s = c2s_ref[...]
    imp = _dot_nt(c2s, hi) + _dot_nt(c2s, lo)

    t = pl.program_id(2) * tq + lax.broadcasted_iota(jnp.int32, (n_sel, tq), 1)
    blk = lax.broadcasted_iota(jnp.int32, (n_sel, tq), 0)
    cur = t // B_SEL_BLOCK
    forced = (blk == 0) | (blk == cur) | (blk == cur - 1)
    imp = jnp.where(forced, FORCE_SCORE, jnp.where(blk * B_SEL_BLOCK <= t, imp, NEG_INF))
    imp_ref[...] = imp

    def count(i, rank):
        row = imp_ref[pl.ds(i, 1), :]
        ahead = jnp.where(row > imp, 1.0, jnp.where(row == imp, jnp.where(blk > i, 1.0, 0.0), 0.0))
        return rank + ahead

    rank = lax.fori_loop(0, n_sel, count, jnp.zeros((n_sel, tq), F32))
    sel_ref[0, 0] = jnp.where(rank < top_n, 0.0, NEG_INF).astype(sel_ref.dtype)


def _b_cmp_attn(q, kc, vc, bias_c, batch, seq):
    n_sel = seq // B_SEL_BLOCK
    n_cmp_pad = seq // B_CMP_STRIDE
    n_cmp = (seq - B_CMP_LEN) // B_CMP_STRIDE + 1
    c = np.arange(n_cmp_pad)[None, :] * B_CMP_STRIDE
    j = np.arange(n_sel)[:, None] * B_SEL_BLOCK
    c2s = ((c < j + B_SEL_BLOCK) & (c + B_CMP_LEN > j) & (np.arange(n_cmp_pad)[None, :] < n_cmp)).astype(np.float32)
    kern = functools.partial(_b_cmp_attn_kernel, top_n=min(B_TOP_N, n_sel))
    return pl.pallas_call(
        kern,
        out_shape=(jax.ShapeDtypeStruct((batch, N_HEADS, B_HEAD_DIM, seq), F32),
                   jax.ShapeDtypeStruct((batch, B_KV_HEADS, n_sel, seq), BF16)),
        grid=(batch, B_KV_HEADS, seq // B_TILE),
        in_specs=[pl.BlockSpec((1, B_GROUP, B_TILE, B_HEAD_DIM), lambda b, n, i: (b, n, i, 0)),
                  pl.BlockSpec((1, 1, n_cmp_pad, B_HEAD_DIM), lambda b, n, i: (b, n, 0, 0)),
                  pl.BlockSpec((1, 1, B_HEAD_DIM, n_cmp_pad), lambda b, n, i: (b, n, 0, 0)),
                  pl.BlockSpec((B_GROUP, 1, B_TILE, n_cmp_pad), lambda b, n, i: (n, i, 0, 0)),
                  pl.BlockSpec((n_sel, n_cmp_pad), lambda b, n, i: (0, 0))],
        out_specs=(pl.BlockSpec((1, B_GROUP, B_HEAD_DIM, B_TILE), lambda b, n, i: (b, n, 0, i)),
                   pl.BlockSpec((1, 1, n_sel, B_TILE), lambda b, n, i: (b, n, 0, i))),
        scratch_shapes=[pltpu.VMEM((n_sel, B_TILE), F32)],
        compiler_params=_params("parallel", "parallel", "arbitrary"),
        name="b_cmp_attn",
    )(q, kc, vc.transpose(0, 1, 3, 2), bias_c, jnp.asarray(c2s, BF16))


def _b_sparse_kernel(qt_ref, ka_ref, vs_ref, wk_ref, vw_ref, sel_ref, bs_ref, bw_ref, oc_ref, gate_ref,
                     o_ref, acc_ref, *, delta_max, win_tiles):
    tq = B_TILE
    dh = B_HEAD_DIM
    cols = B_GROUP * tq
    qi = pl.program_id(2)
    qt = jnp.concatenate([qt_ref[0, g] for g in range(B_GROUP)], axis=1)
    q_aug = jnp.concatenate([qt, jnp.concatenate([sel_ref[0, 0]] * B_GROUP, axis=1)], axis=0)

    def sweep(k_ref, vt_ref, rhs, lo, hi, bias_fn):
        acc_ref[...] = jnp.zeros(acc_ref.shape, F32)

        def step(kt, m_old):
            start = pl.multiple_of(kt * tq, tq)
            s = _dot(k_ref[0, 0, pl.ds(start, tq), :], rhs) + bias_fn(kt)
            m_new = jnp.maximum(m_old, jnp.max(s, axis=0, keepdims=True))
            alpha = jnp.exp(m_old - m_new)
            p = jnp.exp(s - m_new).astype(BF16)
            acc_ref[...] = alpha * acc_ref[...] + _dot(vt_ref[0, 0, :, pl.ds(start, tq)], p)
            return m_new

        lax.fori_loop(lo, hi, step, jnp.full((1, cols), NEG_INF, F32))
        acc = acc_ref[...]
        return acc[:dh] * (1.0 / acc[dh:dh + 1])

    def sel_bias(kt):
        d = jnp.minimum(qi - kt, delta_max)
        return jnp.concatenate([bs_ref[g, d] for g in range(B_GROUP)], axis=1)

    def win_bias(kt):
        return jnp.concatenate([bw_ref[g, qi - kt] for g in range(B_GROUP)], axis=1)

    o_s = sweep(ka_ref, vs_ref, q_aug, 0, qi + 1, sel_bias)
    o_w = sweep(wk_ref, vw_ref, qt, jnp.maximum(qi - (win_tiles - 1), 0), qi + 1, win_bias)

    gate = gate_ref[0, 0]
    merged = []
    for g in range(B_GROUP):
        cs = slice(g * tq, (g + 1) * tq)
        merged.append(gate[3 * g:3 * g + 1] * oc_ref[0, g] + gate[3 * g + 1:3 * g + 2] * o_s[:, cs]
                      + gate[3 * g + 2:3 * g + 3] * o_w[:, cs])
    for pair in range(B_GROUP // 2):
        both = jnp.concatenate([merged[2 * pair], merged[2 * pair + 1]], axis=0)
        o_ref[0, :, pair * 2 * dh:(pair + 1) * 2 * dh] = both.T.astype(o_ref.dtype)


def _b_sparse(qt, ka, vs, wk, vw, sel, bias_s, bias_w, oc, gate, batch, seq):
    n_sel = seq // B_SEL_BLOCK
    dh = B_HEAD_DIM
    n_ds = bias_s.shape[1]
    n_dw = bias_w.shape[1]
    vrows = vs.shape[2]
    kern = functools.partial(_b_sparse_kernel, delta_max=n_ds - 1, win_tiles=n_dw)
    whole = lambda rows, width: pl.BlockSpec((1, 1, rows, width), lambda b, n, i: (b, n, 0, 0))
    return pl.pallas_call(
        kern,
        out_shape=jax.ShapeDtypeStruct((batch, seq, N_HEADS * dh), BF16),
        grid=(batch, B_KV_HEADS, seq // B_TILE),
        in_specs=[pl.BlockSpec((1, B_GROUP, dh, B_TILE), lambda b, n, i: (b, n, 0, i)),
                  whole(seq, dh + n_sel), whole(vrows, seq), whole(seq, dh), whole(vrows, seq),
                  pl.BlockSpec((1, 1, n_sel, B_TILE), lambda b, n, i: (b, n, 0, i)),
                  pl.BlockSpec((B_GROUP, n_ds, B_TILE, B_TILE), lambda b, n, i: (n, 0, 0, 0)),
                  pl.BlockSpec((B_GROUP, n_dw, B_TILE, B_TILE), lambda b, n, i: (n, 0, 0, 0)),
                  pl.BlockSpec((1, B_GROUP, dh, B_TILE), lambda b, n, i: (b, n, 0, i)),
                  pl.BlockSpec((1, 1, 4 * B_GROUP, B_TILE), lambda b, n, i: (b, n, 0, i))],
        out_specs=pl.BlockSpec((1, B_TILE, B_GROUP * dh), lambda b, n, i: (b, i, n)),
        scratch_shapes=[pltpu.VMEM((vrows, B_GROUP * B_TILE), F32)],
        compiler_params=_params("parallel", "parallel", "arbitrary"),
        name="b_sparse",
    )(qt, ka, vs, wk, vw, sel, bias_s, bias_w, oc, gate)


def _mixer_b(x, rel_bias, norm1, w_in, q_gain, k_gain, cmp_pos, cmp_w1, cmp_w2, w_out, batch, seq):
    w_pad = jnp.pad(w_in, ((0, 0), (0, B_PROJ_WIDTH - w_in.shape[1]))).astype(BF16)
    proj = _norm_matmul(x, norm1, w_pad, BF16, 512).reshape(batch, seq, B_PROJ_WIDTH)
    q, ck, cv, sk, sv, wk, wv, gate = _b_prep(proj, q_gain, k_gain, batch, seq)
    kc, vc = _b_compress(ck, cv, cmp_pos, cmp_w1, cmp_w2, k_gain[0], batch, seq)
    n_tiles = seq // B_TILE
    bias_c = _bias_tiles(rel_bias, n_tiles, B_TILE, seq // B_CMP_STRIDE, base=1 - B_CMP_LEN, tile_step=B_TILE,
                         row_step=1, col_step=-B_CMP_STRIDE, dmax=1 << 30)
    oc, sel = _b_cmp_attn(q, kc, vc, bias_c, batch, seq)
    delta_max = min(n_tiles - 1, -(-(_THRESHOLDS[-1] + B_TILE - 1) // B_TILE))
    bias_s = _bias_tiles(rel_bias, delta_max + 1, B_TILE, B_TILE, base=0, tile_step=B_TILE, row_step=-1,
                         col_step=1, dmax=1 << 30)
    win_tiles = (B_WINDOW - 1 + B_TILE - 1) // B_TILE + 1
    bias_w = _bias_tiles(rel_bias, win_tiles, B_TILE, B_TILE, base=0, tile_step=B_TILE, row_step=-1,
                         col_step=1, dmax=B_WINDOW - 1)
    n_sel = seq // B_SEL_BLOCK
    onehot = (np.arange(seq)[:, None] // B_SEL_BLOCK == np.arange(n_sel)[None, :]).astype(np.float32)
    ka = jnp.concatenate([sk, jnp.broadcast_to(jnp.asarray(onehot, BF16), sk.shape[:2] + onehot.shape)], axis=-1)
    ones = jnp.ones(sv.shape[:2] + (16, seq), BF16)
    vs = jnp.concatenate([sv.transpose(0, 1, 3, 2), ones], axis=2)
    vw = jnp.concatenate([wv.transpose(0, 1, 3, 2), ones], axis=2)
    gate_t = jnp.pad(gate.transpose(0, 1, 3, 2), ((0, 0), (0, 0), (0, B_GROUP), (0, 0)))
    o = _b_sparse(q.transpose(0, 1, 3, 2), ka, vs, wk, vw, sel, bias_s, bias_w, oc, gate_t, batch, seq)
    return _matmul_residual(o.reshape(batch * seq, -1), w_out.astype(BF16), x)


def _c_conv_kernel(cur_ref, halo_ref, w_ref, sm_ref, alog_ref, dtb_ref, qkv_ref, bg_ref):
    ts = cur_ref.shape[1]
    keep = jnp.where(pl.program_id(1) == 0, 0.0, 1.0)
    dk = C_HEAD_DIM
    for c in range(3 * C_HEADS):
        sl = slice(c * dk, (c + 1) * dk)
        xe = jnp.concatenate([halo_ref[0, :, sl].astype(F32) * keep, cur_ref[0, :, sl].astype(F32)], axis=0)
        y = None
        for j in range(C_CONV):
            off = 8 - (C_CONV - 1) + j
            term = w_ref[j:j + 1, sl] * xe[off:off + ts]
            y = term if y is None else y + term
        y = y * jax.nn.sigmoid(y)
        if c < 2 * C_HEADS:
            y = y * lax.rsqrt(jnp.sum(y * y, axis=-1, keepdims=True) + RMS_EPS)
        if c < C_HEADS:
            y = y * (dk ** -0.5)
        qkv_ref[0, :, sl] = y
    sm = sm_ref[0]
    a = sm + dtb_ref[...]
    softplus = jnp.maximum(a, 0.0) + jnp.log1p(jnp.exp(-jnp.abs(a)))
    g = -jnp.exp(alog_ref[...]) * softplus
    lane = lax.broadcasted_iota(jnp.int32, sm.shape, 1)
    bg_ref[0] = jnp.where(lane < C_HEADS, jax.nn.sigmoid(sm), g)


def _c_conv(proj, small, conv_w, a_log, dt_bias, batch, seq):
    ts = 256
    width = 3 * C_WIDTH
    pad = lambda v: jnp.pad(v, (C_HEADS, 128 - 2 * C_HEADS)).reshape(1, 128)
    return pl.pallas_call(
        _c_conv_kernel,
        out_shape=(jax.ShapeDtypeStruct((batch, seq, width), F32),
                   jax.ShapeDtypeStruct((batch, seq, 128), F32)),
        grid=(batch, seq // ts),
        in_specs=[pl.BlockSpec((1, ts, width), lambda b, i: (b, i, 0)),
                  pl.BlockSpec((1, 8, width), lambda b, i: (b, jnp.maximum(i * (ts // 8) - 1, 0), 0)),
                  pl.BlockSpec((C_CONV, width), lambda b, i: (0, 0)),
                  pl.BlockSpec((1, ts, 128), lambda b, i: (b, i, 0)),
                  pl.BlockSpec((1, 128), lambda b, i: (0, 0)),
                  pl.BlockSpec((1, 128), lambda b, i: (0, 0))],
        out_specs=(pl.BlockSpec((1, ts, width), lambda b, i: (b, i, 0)),
                   pl.BlockSpec((1, ts, 128), lambda b, i: (b, i, 0))),
        compiler_params=_params("parallel", "arbitrary"),
        name="c_conv",
    )(proj, proj, conv_w, small, pad(a_log), pad(dt_bias))


def _c_chunk_kernel(qkv_ref, bg_ref, bgt_ref, ltri_ref, utri_ref, u_ref, w_ref, qg_ref, kg_ref, attn_ref, gc_ref):
    cs = C_CHUNK
    dk = C_HEAD_DIM
    row = lax.broadcasted_iota(jnp.int32, (cs, cs), 0)
    col = lax.broadcasted_iota(jnp.int32, (cs, cs), 1)
    causal = row >= col
    strict = row > col
    eye = jnp.where(row == col, 1.0, 0.0)

    def chunk(r, carry):
        rs = pl.ds(pl.multiple_of(r * cs, cs), cs)
        bgc = bg_ref[0, rs, :]
        gcum_col = _dot(ltri_ref[...], bgc, HIGHEST)
        gcum_row = _dot(bgt_ref[0, r], utri_ref[...], HIGHEST)
        gc_ref[0, rs, :] = gcum_col
        for h in range(C_HEADS):
            gc = gcum_col[:, C_HEADS + h:C_HEADS + h + 1]
            gr = gcum_row[C_HEADS + h:C_HEADS + h + 1, :]
            beta = bgc[:, h:h + 1]
            q = qkv_ref[0, rs, h * dk:(h + 1) * dk]
            k = qkv_ref[0, rs, C_WIDTH + h * dk:C_WIDTH + (h + 1) * dk]
            v = qkv_ref[0, rs, 2 * C_WIDTH + h * dk:2 * C_WIDTH + (h + 1) * dk]
            decay = jnp.exp(jnp.where(causal, gc - gr, NEG_INF))
            kb = k * beta
            low = jnp.where(strict, _dot_nt(kb, k, HIGHEST) * decay, 0.0)
            t_mat = eye - low
            power = low
            for _ in range(int(math.log2(cs)) - 1):
                power = _dot(power, power, HIGHEST)
                t_mat = t_mat + _dot(t_mat, power, HIGHEST)
            egc = jnp.exp(gc)
            u_ref[0, rs, h * dk:(h + 1) * dk] = _dot(t_mat, v * beta, HIGHEST)
            w_ref[0, rs, h * dk:(h + 1) * dk] = _dot(t_mat, kb * egc, HIGHEST)
            attn_ref[0, rs, h * cs:(h + 1) * cs] = jnp.where(causal, _dot_nt(q, k, HIGHEST), 0.0) * decay
            qg_ref[0, rs, h * dk:(h + 1) * dk] = q * egc
            kg_ref[0, rs, h * dk:(h + 1) * dk] = k * jnp.exp(gc[cs - 1:cs, :] - gc)
        return carry

    lax.fori_loop(0, C_CHUNKS_PER_STEP, chunk, 0)


def _c_chunks(qkv, bg, bgt, batch, seq):
    rc = C_CHUNKS_PER_STEP * C_CHUNK
    tri = np.tril(np.ones((C_CHUNK, C_CHUNK), np.float32))
    wide = lambda width: pl.BlockSpec((1, rc, width), lambda b, i: (b, i, 0))
    shape = lambda width: jax.ShapeDtypeStruct((batch, seq, width), F32)
    return pl.pallas_call(
        _c_chunk_kernel,
        out_shape=(shape(C_WIDTH),) * 4 + (shape(C_HEADS * C_CHUNK), shape(128)),
        grid=(batch, seq // rc),
        in_specs=[wide(3 * C_WIDTH), wide(128),
                  pl.BlockSpec((1, C_CHUNKS_PER_STEP, 2 * C_HEADS, C_CHUNK), lambda b, i: (b, i, 0, 0)),
                  pl.BlockSpec((C_CHUNK, C_CHUNK), lambda b, i: (0, 0)),
                  pl.BlockSpec((C_CHUNK, C_CHUNK), lambda b, i: (0, 0))],
        out_specs=(wide(C_WIDTH),) * 4 + (wide(C_HEADS * C_CHUNK), wide(128)),
        compiler_params=_params("parallel", "parallel"),
        name="c_chunks",
    )(qkv, bg, bgt, jnp.asarray(tri), jnp.asarray(tri.T))


def _c_scan_kernel(u_ref, w_ref, qg_ref, kg_ref, attn_ref, gc_ref, o_ref, state_ref):
    @pl.when(pl.program_id(1) == 0)
    def _():
        state_ref[...] = jnp.zeros_like(state_ref)

    cs = C_CHUNK
    dk = C_HEAD_DIM
    decay_last = jnp.exp(gc_ref[0, cs - 1:cs, :])
    for h in range(C_HEADS):
        sl = slice(h * dk, (h + 1) * dk)
        state = state_ref[h]
        v_new = u_ref[0, :, sl] - _dot(w_ref[0, :, sl], state, HIGHEST)
        o_ref[0, :, sl] = (_dot(qg_ref[0, :, sl], state, HIGHEST)
                           + _dot(attn_ref[0, :, h * cs:(h + 1) * cs], v_new, HIGHEST))
        state_ref[h] = (state * decay_last[:, C_HEADS + h:C_HEADS + h + 1]
                        + _dot_tn(kg_ref[0, :, sl], v_new, HIGHEST))


def _c_scan(u, w, qg, kg, attn, gc, batch, seq):
    wide = lambda width: pl.BlockSpec((1, C_CHUNK, width), lambda b, c: (b, c, 0))
    return pl.pallas_call(
        _c_scan_kernel,
        out_shape=jax.ShapeDtypeStruct((batch, seq, C_WIDTH), F32),
        grid=(batch, seq // C_CHUNK),
        in_specs=[wide(C_WIDTH)] * 4 + [wide(C_HEADS * C_CHUNK), wide(128)],
        out_specs=wide(C_WIDTH),
        scratch_shapes=[pltpu.VMEM((C_HEADS, C_HEAD_DIM, C_HEAD_DIM), F32)],
        compiler_params=_params("parallel", "arbitrary"),
        name="c_scan",
    )(u, w, qg, kg, attn, gc)


def _c_out_kernel(o_ref, z_ref, g_ref, w_ref, x_ref, out_ref):
    dk = C_HEAD_DIM
    parts = []
    for h in range(C_HEADS):
        sl = slice(h * dk, (h + 1) * dk)
        z = z_ref[:, sl].astype(F32)
        parts.append((_rms(o_ref[:, sl], g_ref[...]) * (z * jax.nn.sigmoid(z))).astype(BF16))
    out_ref[...] = x_ref[...] + _dot(jnp.concatenate(parts, axis=-1), w_ref[...])


def _c_out(o, proj, out_gain, w_out, x):
    m, d = x.shape
    z_block = (3 * C_WIDTH) // C_WIDTH
    row = lambda width: pl.BlockSpec((ROW_TILE, width), lambda i: (i, 0))
    return pl.pallas_call(
        _c_out_kernel,
        out_shape=jax.ShapeDtypeStruct((m, d), F32),
        grid=(m // ROW_TILE,),
        in_specs=[row(C_WIDTH),
                  pl.BlockSpec((ROW_TILE, C_WIDTH), lambda i: (i, z_block)),
                  pl.BlockSpec((1, C_HEAD_DIM), lambda i: (0, 0)),
                  pl.BlockSpec((C_WIDTH, d), lambda i: (0, 0)),
                  row(d)],
        out_specs=row(d),
        compiler_params=_params("parallel"),
        name="c_out",
    )(o, proj, out_gain.reshape(1, -1), w_out, x)


def _mixer_c(x, norm1, w_in, conv_w, a_log, dt_bias, out_gain, w_out, batch, seq):
    main = 4 * C_WIDTH
    proj = _norm_matmul(x, norm1, w_in[:, :main].astype(BF16), BF16, 512)
    w_small = jnp.pad(w_in[:, main:], ((0, 0), (0, 128 - 2 * C_HEADS))).astype(BF16)
    small = _norm_matmul(x, norm1, w_small, F32, 128)
    qkv, bg = _c_conv(proj.reshape(batch, seq, main), small.reshape(batch, seq, 128), conv_w, a_log, dt_bias,
                      batch, seq)
    bgt = bg[:, :, :2 * C_HEADS].reshape(batch, seq // C_CHUNK, C_CHUNK, 2 * C_HEADS).transpose(0, 1, 3, 2)
    u, w, qg, kg, attn, gc = _c_chunks(qkv, bg, bgt, batch, seq)
    o = _c_scan(u, w, qg, kg, attn, gc, batch, seq)
    return _c_out(o.reshape(batch * seq, C_WIDTH), proj, out_gain, w_out.astype(BF16), x)


def kernel(x, rel_bias, l0_norm1, l0_a_w_in, l0_a_q_gain, l0_a_k_gain, l0_a_w_out, l0_norm2, l0_ffn_w_gate, l0_ffn_w_up, l0_ffn_w_down, l1_norm1, l1_b_w_in, l1_b_q_gain, l1_b_k_gain, l1_b_cmp_pos, l1_b_cmp_w1, l1_b_cmp_w2, l1_b_w_out, l1_norm2, l1_ffn_w_gate, l1_ffn_w_up, l1_ffn_w_down, l2_norm1, l2_c_w_in, l2_c_conv_w, l2_c_a_log, l2_c_dt_bias, l2_c_out_gain, l2_c_w_out, l2_norm2, l2_ffn_w_gate, l2_ffn_w_up, l2_ffn_w_down, l3_norm1, l3_a_w_in, l3_a_q_gain, l3_a_k_gain, l3_a_w_out, l3_norm2, l3_ffn_w_gate, l3_ffn_w_up, l3_ffn_w_down):
    batch, seq, d = x.shape
    h = x.reshape(batch * seq, d)

    def ffn(h, norm2, w_gate, w_up, w_down):
        return _ffn(h, norm2, w_gate.astype(BF16), w_up.astype(BF16), w_down.astype(BF16))

    h = _mixer_a(h, rel_bias, l0_norm1, l0_a_w_in, l0_a_q_gain, l0_a_k_gain, l0_a_w_out, batch, seq)
    h = ffn(h, l0_norm2, l0_ffn_w_gate, l0_ffn_w_up, l0_ffn_w_down)
    h = _mixer_b(h, rel_bias, l1_norm1, l1_b_w_in, l1_b_q_gain, l1_b_k_gain, l1_b_cmp_pos, l1_b_cmp_w1,
                 l1_b_cmp_w2, l1_b_w_out, batch, seq)
    h = ffn(h, l1_norm2, l1_ffn_w_gate, l1_ffn_w_up, l1_ffn_w_down)
    h = _mixer_c(h, l2_norm1, l2_c_w_in, l2_c_conv_w, l2_c_a_log, l2_c_dt_bias, l2_c_out_gain, l2_c_w_out,
                 batch, seq)
    h = ffn(h, l2_norm2, l2_ffn_w_gate, l2_ffn_w_up, l2_ffn_w_down)
    h = _mixer_a(h, rel_bias, l3_norm1, l3_a_w_in, l3_a_q_gain, l3_a_k_gain, l3_a_w_out, batch, seq)
    h = ffn(h, l3_norm2, l3_ffn_w_gate, l3_ffn_w_up, l3_ffn_w_down)
    return h.reshape(batch, seq, d)
```

```python
import functools
import math

import numpy as np
import jax
import jax.numpy as jnp
from jax import lax
from jax.experimental import pallas as pl
from jax.experimental.pallas import tpu as pltpu

D_MODEL = 1024
RMS_EPS = 1e-6
NEG_INF = -1e30
TINY = 1e-30
FORCE_SCORE = 1e9

N_BUCKETS = 32
REL_MAX_DISTANCE = 2048
N_HEADS = 16

A_GROUPS = ((128, 1), (512, 4), (2048, 16))
A_HEAD_DIM = 64
A_Q_BLOCK = 128
A_PROJ_TILE = 512

B_KV_HEADS = 4
B_GROUP = 4
B_HEAD_DIM = 64
B_CMP_LEN = 32
B_CMP_STRIDE = 16
B_CMP_HIDDEN = 256
B_SEL_BLOCK = 64
B_TOP_N = 16
B_WINDOW = 512
B_TILE = 128
B_SWEEP = 256
B_PROJ_WIDTH = 3072

C_HEADS = 8
C_HEAD_DIM = 128
C_WIDTH = C_HEADS * C_HEAD_DIM
C_CONV = 4
C_CHUNK = 64
C_GROUP = 4

FFN_HIDDEN = 2816
FFN_TILE = 1024

ROW_TILE = 512
VMEM_LIMIT = 48 * 1024 * 1024

F32 = jnp.float32
BF16 = jnp.bfloat16
HIGHEST = lax.Precision.HIGHEST

NT_DIMS = (((1,), (1,)), ((), ()))
TN_DIMS = (((0,), (0,)), ((), ()))


def _params(*semantics):
    return pltpu.CompilerParams(dimension_semantics=semantics, vmem_limit_bytes=VMEM_LIMIT)


def _dot(a, b, precision=None):
    return jnp.dot(a, b, preferred_element_type=F32, precision=precision)


def _dot_nt(a, b, precision=None):
    return lax.dot_general(a, b, NT_DIMS, preferred_element_type=F32, precision=precision)


def _dot_tn(a, b, precision=None):
    return lax.dot_general(a, b, TN_DIMS, preferred_element_type=F32, precision=precision)


def _rms(x, gain):
    return x * lax.rsqrt(jnp.mean(x * x, axis=-1, keepdims=True) + RMS_EPS) * gain


def _bucket_thresholds():
    d = np.arange(1 << 15)
    max_exact = N_BUCKETS // 2
    d_f = np.maximum(d, 1).astype(np.float32)
    large = max_exact + (np.log(d_f / np.float32(max_exact)) / np.float32(math.log(REL_MAX_DISTANCE / max_exact))
                         * np.float32(N_BUCKETS - max_exact)).astype(np.int32)
    bucket = np.where(d < max_exact, d, np.minimum(large, N_BUCKETS - 1))
    return [int(np.argmax(bucket >= k)) if np.any(bucket >= k) else int(1 << 30) for k in range(N_BUCKETS)]


_THRESHOLDS = _bucket_thresholds()


def _bias_tile_kernel(tbl_ref, o_ref, *, base, tile_step, row_step, col_step, dmax, dil):
    h = pl.program_id(0)
    t = pl.program_id(1)
    shape = o_ref.shape[2:]
    i = lax.broadcasted_iota(jnp.int32, shape, 0)
    j = lax.broadcasted_iota(jnp.int32, shape, 1)
    dist = base + tile_step * t + row_step * i + col_step * j
    d = dist * dil
    val = jnp.full(shape, tbl_ref[0, h], F32)
    for k in range(1, N_BUCKETS):
        val = jnp.where(d >= _THRESHOLDS[k], tbl_ref[k, h], val)
    valid = (dist >= 0) & (dist <= dmax)
    o_ref[0, 0] = jnp.where(valid, val, NEG_INF)


def _bias_tiles(rel_bias, n_tiles, rows, cols, *, base, tile_step, row_step, col_step, dmax, dil=1):
    kern = functools.partial(_bias_tile_kernel, base=base, tile_step=tile_step, row_step=row_step,
                             col_step=col_step, dmax=dmax, dil=dil)
    return pl.pallas_call(
        kern,
        out_shape=jax.ShapeDtypeStruct((N_HEADS, n_tiles, rows, cols), F32),
        grid=(N_HEADS, n_tiles),
        in_specs=[pl.BlockSpec(memory_space=pltpu.SMEM)],
        out_specs=pl.BlockSpec((1, 1, rows, cols), lambda h, t: (h, t, 0, 0)),
        compiler_params=_params("parallel", "parallel"),
        name="bias_tiles",
    )(rel_bias)


def _resident(shape):
    return pl.BlockSpec(shape, lambda i: (0,) * len(shape), pipeline_mode=pl.Buffered(1))


def _norm_matmul_kernel(x_ref, g_ref, w_ref, o_ref, *, tn):
    h = _rms(x_ref[...], g_ref[...]).astype(BF16)
    for j in range(w_ref.shape[1] // tn):
        o_ref[:, j * tn:(j + 1) * tn] = _dot(h, w_ref[:, j * tn:(j + 1) * tn]).astype(o_ref.dtype)


def _norm_matmul(x, gain, w, out_dtype, tn):
    m, d = x.shape
    n = w.shape[1]
    return pl.pallas_call(
        functools.partial(_norm_matmul_kernel, tn=tn),
        out_shape=jax.ShapeDtypeStruct((m, n), out_dtype),
        grid=(m // ROW_TILE,),
        in_specs=[pl.BlockSpec((ROW_TILE, d), lambda i: (i, 0)),
                  _resident((1, d)),
                  _resident((d, n))],
        out_specs=pl.BlockSpec((ROW_TILE, n), lambda i: (i, 0)),
        compiler_params=_params("parallel"),
        name="norm_matmul",
    )(x, gain.reshape(1, d), w)


def _matmul_residual_kernel(a_ref, w_ref, x_ref, o_ref):
    o_ref[...] = x_ref[...] + _dot(a_ref[...], w_ref[...])


def _matmul_residual(a, w, x):
    m, k = a.shape
    d = w.shape[1]
    return pl.pallas_call(
        _matmul_residual_kernel,
        out_shape=jax.ShapeDtypeStruct((m, d), F32),
        grid=(m // ROW_TILE,),
        in_specs=[pl.BlockSpec((ROW_TILE, k), lambda i: (i, 0)),
                  pl.BlockSpec((k, d), lambda i: (0, 0)),
                  pl.BlockSpec((ROW_TILE, d), lambda i: (i, 0))],
        out_specs=pl.BlockSpec((ROW_TILE, d), lambda i: (i, 0)),
        compiler_params=_params("parallel"),
        name="matmul_residual",
    )(a, w, x)


def _ffn_kernel(x_ref, g_ref, wg_ref, wu_ref, wd_ref, o_ref):
    x = x_ref[...]
    h = _rms(x, g_ref[...]).astype(BF16)
    hidden = wg_ref.shape[1]
    acc = x
    for lo in range(0, hidden, FFN_TILE):
        hi = min(lo + FFN_TILE, hidden)
        a = _dot(h, wg_ref[:, lo:hi])
        b = _dot(h, wu_ref[:, lo:hi])
        acc = acc + _dot((a * jax.nn.sigmoid(a) * b).astype(BF16), wd_ref[lo:hi, :])
    o_ref[...] = acc


def _ffn(x, gain, w_gate, w_up, w_down):
    m, d = x.shape
    hidden = w_gate.shape[1]
    return pl.pallas_call(
        _ffn_kernel,
        out_shape=jax.ShapeDtypeStruct((m, d), F32),
        grid=(m // ROW_TILE,),
        in_specs=[pl.BlockSpec((ROW_TILE, d), lambda i: (i, 0)),
                  _resident((1, d)),
                  _resident((d, hidden)), _resident((d, hidden)), _resident((hidden, d))],
        out_specs=pl.BlockSpec((ROW_TILE, d), lambda i: (i, 0)),
        compiler_params=_params("parallel"),
        name="ffn",
    )(x, gain.reshape(1, d), w_gate, w_up, w_down)


def _a_proj_kernel(x_ref, g_ref, w_ref, qg_ref, kg_ref, o_ref, h_ref, x_scr, *, dil):
    rows = ROW_TILE // dil
    xn = _rms(x_ref[...], g_ref[...])
    if dil == 1:
        h_ref[...] = xn.astype(BF16)
    else:
        slabs = xn.shape[1] // 128
        for c in range(slabs):
            x_scr[c] = xn[:, c * 128:(c + 1) * 128]
        for r in range(dil):
            picked = [x_scr[c, pl.ds(r, rows, stride=dil), :] for c in range(slabs)]
            h_ref[r * rows:(r + 1) * rows, :] = jnp.concatenate(picked, axis=1).astype(BF16)
    h = h_ref[...]
    width = w_ref.shape[1]
    hd = width // 3
    low = lax.broadcasted_iota(jnp.int32, (ROW_TILE, 128), 1) < A_HEAD_DIM
    for j in range(width // A_PROJ_TILE):
        res = _dot(h, w_ref[:, j * A_PROJ_TILE:(j + 1) * A_PROJ_TILE])
        kind = (j * A_PROJ_TILE) // hd
        if kind < 2:
            parts = []
            for c in range(A_PROJ_TILE // 128):
                y = res[:, c * 128:(c + 1) * 128]
                sq = y * y
                tot = jnp.sum(sq, axis=-1, keepdims=True)
                lo = jnp.sum(jnp.where(low, sq, 0.0), axis=-1, keepdims=True)
                ss = jnp.where(low, lo, tot - lo)
                parts.append(y * lax.rsqrt(ss * (1.0 / A_HEAD_DIM) + RMS_EPS))
            res = jnp.concatenate(parts, axis=1) * (qg_ref if kind == 0 else kg_ref)[...]
        res = res.astype(BF16)
        for r in range(dil):
            off = r * width + j * A_PROJ_TILE
            o_ref[:, off:off + A_PROJ_TILE] = res[r * rows:(r + 1) * rows]


def _a_proj(x, gain, w, q_gain, k_gain, dil):
    m, d = x.shape
    width = w.shape[1]
    reps = A_PROJ_TILE // A_HEAD_DIM
    qg = jnp.tile(q_gain * (A_HEAD_DIM ** -0.5), reps).reshape(1, A_PROJ_TILE)
    kg = jnp.tile(k_gain, reps).reshape(1, A_PROJ_TILE)
    return pl.pallas_call(
        functools.partial(_a_proj_kernel, dil=dil),
        out_shape=jax.ShapeDtypeStruct((m // dil, dil * width), BF16),
        grid=(m // ROW_TILE,),
        in_specs=[pl.BlockSpec((ROW_TILE, d), lambda i: (i, 0)),
                  pl.BlockSpec((1, d), lambda i: (0, 0)),
                  pl.BlockSpec((d, width), lambda i: (0, 0)),
                  pl.BlockSpec((1, A_PROJ_TILE), lambda i: (0, 0)),
                  pl.BlockSpec((1, A_PROJ_TILE), lambda i: (0, 0))],
        out_specs=pl.BlockSpec((ROW_TILE // dil, dil * width), lambda i: (i, 0)),
        scratch_shapes=[pltpu.VMEM((ROW_TILE, d), BF16), pltpu.VMEM((d // 128, ROW_TILE, 128), F32)],
        compiler_params=_params("parallel"),
        name="a_proj",
    )(x, gain.reshape(1, d), w, qg, kg)


def _a_attn_kernel(q_ref, kp_ref, kc_ref, vp_ref, vc_ref, bias_ref, o_ref, lse_ref):
    first = (pl.program_id(2) == 0).astype(jnp.int32)
    lane = lax.broadcasted_iota(jnp.int32, (A_Q_BLOCK, 128), 1)
    low = lane < A_HEAD_DIM
    ones = jnp.ones((A_Q_BLOCK, 128), BF16)
    lse_tile = jnp.zeros((A_Q_BLOCK, 128), F32)
    for pair in range(N_HEADS // 2):
        sl = slice(pair * 128, (pair + 1) * 128)
        q = q_ref[0, :, sl]
        kp = kp_ref[0, :, sl]
        kc = kc_ref[0, :, sl]
        vp = jnp.concatenate([vp_ref[0, :, sl], ones], axis=1)
        vc = jnp.concatenate([vc_ref[0, :, sl], ones], axis=1)
        halves = []
        for half in range(2):
            h = 2 * pair + half
            qm = jnp.where(low if half == 0 else jnp.logical_not(low), q, jnp.zeros_like(q))
            bias = bias_ref[h + N_HEADS * first]
            sp = _dot_nt(qm, kp) + bias[:, :A_Q_BLOCK]
            sc = _dot_nt(qm, kc) + bias[:, A_Q_BLOCK:]
            m = jnp.max(jnp.maximum(sp, sc), axis=-1, keepdims=True)
            acc = _dot(jnp.exp(sp - m).astype(BF16), vp) + _dot(jnp.exp(sc - m).astype(BF16), vc)
            z = acc[:, 128:]
            halves.append(acc[:, :128] * (1.0 / z))
            lse_tile = jnp.where(lane == h, m + jnp.log(z), lse_tile)
        o_ref[0, :, sl] = jnp.where(low, halves[0], halves[1]).astype(o_ref.dtype)
    lse_ref[0] = lse_tile


def _a_attention(proj, bias, dil, batch, seq):
    length = seq // dil
    nblk = length // A_Q_BLOCK
    hd = N_HEADS * A_HEAD_DIM
    pv = proj.reshape(batch, length, dil * 3 * hd)

    def spec(off, prev):
        if prev:
            return pl.BlockSpec((1, A_Q_BLOCK, hd), lambda b, r, i: (b, jnp.maximum(i - 1, 0), r * 3 + off))
        return pl.BlockSpec((1, A_Q_BLOCK, hd), lambda b, r, i: (b, i, r * 3 + off))

    o, lse = pl.pallas_call(
        _a_attn_kernel,
        out_shape=(jax.ShapeDtypeStruct((batch, length, dil * hd), BF16),
                   jax.ShapeDtypeStruct((batch, length, dil * 128), F32)),
        grid=(batch, dil, nblk),
        in_specs=[spec(0, False), spec(1, True), spec(1, False), spec(2, True), spec(2, False),
                  pl.BlockSpec((2 * N_HEADS, A_Q_BLOCK, 2 * A_Q_BLOCK), lambda b, r, i: (0, 0, 0))],
        out_specs=(pl.BlockSpec((1, A_Q_BLOCK, hd), lambda b, r, i: (b, i, r)),
                   pl.BlockSpec((1, A_Q_BLOCK, 128), lambda b, r, i: (b, i, r))),
        compiler_params=_params("parallel", "parallel", "arbitrary"),
        name="a_attention",
    )(pv, pv, pv, pv, pv, bias)
    return o.reshape(batch * length, dil * hd), lse.reshape(batch * length, dil * 128)


def _a_out_kernel(o0_ref, o1_ref, o2_ref, l0_ref, l1_ref, l2_ref, e_ref, w_ref, x_ref, out_ref, o_scr, l_scr):
    hd = N_HEADS * A_HEAD_DIM
    for g, (o_ref, l_ref) in enumerate(((o0_ref, l0_ref), (o1_ref, l1_ref), (o2_ref, l2_ref))):
        dil = A_GROUPS[g][1]
        rows = ROW_TILE // dil
        for r in range(dil):
            dst = pl.ds(r, rows, stride=dil) if dil > 1 else slice(None)
            l_scr[g, dst, :] = l_ref[:, r * 128:(r + 1) * 128]
            for c in range(hd // 128):
                o_scr[g, c, dst, :] = o_ref[:, r * hd + c * 128:r * hd + (c + 1) * 128].astype(F32)
    ls = [l_scr[g] for g in range(len(A_GROUPS))]
    m = jnp.maximum(jnp.maximum(ls[0], ls[1]), ls[2])
    es = [jnp.exp(l - m) for l in ls]
    inv = 1.0 / (es[0] + es[1] + es[2])
    expand = e_ref[...]
    acc = None
    for g, e in enumerate(es):
        wgt = e * inv
        hi = wgt.astype(BF16)
        lo = (wgt - hi.astype(F32)).astype(BF16)
        o_g = jnp.concatenate([o_scr[g, c] for c in range(hd // 128)], axis=1)
        term = (_dot(hi, expand) + _dot(lo, expand)) * o_g
        acc = term if acc is None else acc + term
    out_ref[...] = x_ref[...] + _dot(acc.astype(BF16), w_ref[...])


def _a_out(outs, lses, w_out, x):
    m, d = x.shape
    hd = N_HEADS * A_HEAD_DIM
    expand = np.zeros((128, hd), np.float32)
    for h in range(N_HEADS):
        expand[h, h * A_HEAD_DIM:(h + 1) * A_HEAD_DIM] = 1.0
    grouped = lambda width: [pl.BlockSpec((ROW_TILE // dil, dil * width), lambda i: (i, 0)) for _, dil in A_GROUPS]
    return pl.pallas_call(
        _a_out_kernel,
        out_shape=jax.ShapeDtypeStruct((m, d), F32),
        grid=(m // ROW_TILE,),
        in_specs=grouped(hd) + grouped(128) + [
            pl.BlockSpec((128, hd), lambda i: (0, 0)),
            pl.BlockSpec((hd, d), lambda i: (0, 0)),
            pl.BlockSpec((ROW_TILE, d), lambda i: (i, 0))],
        out_specs=pl.BlockSpec((ROW_TILE, d), lambda i: (i, 0)),
        scratch_shapes=[pltpu.VMEM((len(A_GROUPS), hd // 128, ROW_TILE, 128), F32),
                        pltpu.VMEM((len(A_GROUPS), ROW_TILE, 128), F32)],
        compiler_params=_params("parallel"),
        name="a_out",
    )(*outs, *lses, jnp.asarray(expand, BF16), w_out, x)


def _mixer_a(x, rel_bias, norm1, w_in, q_gain, k_gain, w_out, batch, seq):
    w_in = w_in.astype(BF16)
    group_width = 3 * N_HEADS * A_HEAD_DIM
    outs, lses = [], []
    for gi, (window, dil) in enumerate(A_GROUPS):
        steps = window // dil
        assert steps == A_Q_BLOCK and (seq // dil) % A_Q_BLOCK == 0 and seq % ROW_TILE == 0
        bias = _bias_tiles(rel_bias, 1, A_Q_BLOCK, 2 * A_Q_BLOCK, base=A_Q_BLOCK, tile_step=0, row_step=1,
                           col_step=-1, dmax=steps, dil=dil)[:, 0]
        bias = jnp.concatenate([bias, bias.at[:, :, :A_Q_BLOCK].set(NEG_INF)], axis=0)
        proj = _a_proj(x, norm1, w_in[:, gi * group_width:(gi + 1) * group_width], q_gain[gi], k_gain[gi], dil)
        o, lse = _a_attention(proj, bias, dil, batch, seq)
        outs.append(o)
        lses.append(lse)
    return _a_out(outs, lses, w_out.astype(BF16), x)


def _b_prep_kernel(p_ref, qg_ref, kg_ref, q_ref, ck_ref, cv_ref, sk_ref, sv_ref, wk_ref, wv_ref, gate_ref):
    dh = B_HEAD_DIM
    qg = qg_ref[...] * (dh ** -0.5)
    for h in range(N_HEADS):
        q_ref[0, h] = _rms(p_ref[0, :, h * dh:(h + 1) * dh].astype(F32), qg).astype(BF16)
    base = N_HEADS * dh
    outs = ((ck_ref, None), (cv_ref, None), (sk_ref, 1), (sv_ref, None), (wk_ref, 2), (wv_ref, None))
    for idx, (ref, gain_row) in enumerate(outs):
        for n in range(B_KV_HEADS):
            off = base + (idx * B_KV_HEADS + n) * dh
            t = p_ref[0, :, off:off + dh]
            if gain_row is not None:
                t = _rms(t.astype(F32), kg_ref[gain_row:gain_row + 1, :]).astype(BF16)
            ref[0, n] = t
    gate_off = base + 6 * B_KV_HEADS * dh
    gate = jax.nn.sigmoid(p_ref[0, :, gate_off:gate_off + 3 * N_HEADS].astype(F32))
    per = 3 * B_GROUP
    for n in range(B_KV_HEADS):
        gate_ref[0, n] = gate[:, n * per:(n + 1) * per]


def _b_prep(proj, q_gain, k_gain, batch, seq):
    ts = 256
    dh = B_HEAD_DIM
    kv_shape = jax.ShapeDtypeStruct((batch, B_KV_HEADS, seq, dh), BF16)
    kv_spec = pl.BlockSpec((1, B_KV_HEADS, ts, dh), lambda b, i: (b, 0, i, 0))
    return pl.pallas_call(
        _b_prep_kernel,
        out_shape=(jax.ShapeDtypeStruct((batch, N_HEADS, seq, dh), BF16),) + (kv_shape,) * 6
        + (jax.ShapeDtypeStruct((batch, B_KV_HEADS, seq, 3 * B_GROUP), F32),),
        grid=(batch, seq // ts),
        in_specs=[pl.BlockSpec((1, ts, B_PROJ_WIDTH), lambda b, i: (b, i, 0)),
                  pl.BlockSpec((1, dh), lambda b, i: (0, 0)),
                  pl.BlockSpec((3, dh), lambda b, i: (0, 0))],
        out_specs=(pl.BlockSpec((1, N_HEADS, ts, dh), lambda b, i: (b, 0, i, 0)),) + (kv_spec,) * 6
        + (pl.BlockSpec((1, B_KV_HEADS, ts, 3 * B_GROUP), lambda b, i: (b, 0, i, 0)),),
        compiler_params=_params("parallel", "parallel"),
        name="b_prep",
    )(proj, q_gain.reshape(1, dh), k_gain)


def _b_compress_kernel(tk_ref, tv_ref, pos_ref, w1_ref, w2_ref, kg_ref, kc_ref, vc_ref):
    half = (B_CMP_LEN // 2) * B_HEAD_DIM
    for kv, (t_ref, out_ref) in enumerate(((tk_ref, kc_ref), (tv_ref, vc_ref))):
        t = t_ref[0, 0].astype(F32)
        top = (t + pos_ref[kv, 0:1, :]).astype(BF16)
        bot = (t + pos_ref[kv, 1:2, :]).astype(BF16)
        a1 = _dot(top, w1_ref[kv, :half, :])
        a2 = _dot(bot, w1_ref[kv, half:, :])
        hidden = a1 + pltpu.roll(a2, a2.shape[0] - 1, 0)
        out = _dot(jax.nn.gelu(hidden).astype(BF16), w2_ref[kv])
        if kv == 0:
            out = _rms(out, kg_ref[...])
        out_ref[0, 0] = out.astype(out_ref.dtype)


def _b_compress(ck, cv, cmp_pos, cmp_w1, cmp_w2, k_gain0, batch, seq):
    rows = seq // B_CMP_STRIDE
    half = (B_CMP_LEN // 2) * B_HEAD_DIM
    tk = ck.reshape(batch, B_KV_HEADS, rows, half)
    tv = cv.reshape(batch, B_KV_HEADS, rows, half)
    pos = cmp_pos.reshape(2, 2, half)
    t_spec = pl.BlockSpec((1, 1, rows, half), lambda b, n: (b, n, 0, 0))
    o_spec = pl.BlockSpec((1, 1, rows, B_HEAD_DIM), lambda b, n: (b, n, 0, 0))
    shape = jax.ShapeDtypeStruct((batch, B_KV_HEADS, rows, B_HEAD_DIM), BF16)
    return pl.pallas_call(
        _b_compress_kernel,
        out_shape=(shape, shape),
        grid=(batch, B_KV_HEADS),
        in_specs=[t_spec, t_spec,
                  pl.BlockSpec((2, 2, half), lambda b, n: (0, 0, 0)),
                  pl.BlockSpec((2, 2 * half, B_CMP_HIDDEN), lambda b, n: (0, 0, 0)),
                  pl.BlockSpec((2, B_CMP_HIDDEN, B_HEAD_DIM), lambda b, n: (0, 0, 0)),
                  pl.BlockSpec((1, B_HEAD_DIM), lambda b, n: (0, 0))],
        out_specs=(o_spec, o_spec),
        compiler_params=_params("parallel", "parallel"),
        name="b_compress",
    )(tk, tv, pos, cmp_w1.astype(BF16), cmp_w2.astype(BF16), k_gain0.reshape(1, -1))


def _b_cmp_attn_kernel(q_ref, kc_ref, vct_ref, bias_ref, c2s_ref, oc_ref, sel_ref, imp_ref, *, top_n):
    tq = B_TILE
    n_sel = imp_ref.shape[0]
    q = q_ref[0].reshape(B_GROUP * tq, B_HEAD_DIM)
    bias = bias_ref[:, 0].reshape(B_GROUP * tq, -1)
    s = _dot_nt(q, kc_ref[0, 0]) + bias
    valid = bias > 0.5 * NEG_INF
    m = jnp.max(s, axis=-1, keepdims=True)
    e = jnp.where(valid, jnp.exp(s - m), 0.0)
    z = jnp.maximum(jnp.sum(e, axis=-1, keepdims=True), TINY)
    p = e * (1.0 / z)
    oct = _dot_nt(vct_ref[0, 0], p.astype(BF16))
    for g in range(B_GROUP):
        oc_ref[0, g] = oct[:, g * tq:(g + 1) * tq]

    p_sum = p[0:tq] + p[tq:2 * tq] + p[2 * tq:3 * tq] + p[3 * tq:4 * tq]
    hi = p_sum.astype(BF16)
    lo = (p_sum - hi.astype(F32)).astype(BF16)
    c2s = c2s_ref[...]
    imp = _dot_nt(c2s, hi) + _dot_nt(c2s, lo)

    t = pl.program_id(2) * tq + lax.broadcasted_iota(jnp.int32, (n_sel, tq), 1)
    blk = lax.broadcasted_iota(jnp.int32, (n_sel, tq), 0)
    cur = t // B_SEL_BLOCK
    forced = (blk == 0) | (blk == cur) | (blk == cur - 1)
    imp = jnp.where(forced, FORCE_SCORE, jnp.where(blk * B_SEL_BLOCK <= t, imp, NEG_INF))
    imp_ref[...] = imp

    def count(i, rank):
        row = imp_ref[pl.ds(i, 1), :]
        ahead = jnp.where(row > imp, 1.0, jnp.where(row == imp, jnp.where(blk > i, 1.0, 0.0), 0.0))
        return rank + ahead

    rank = lax.fori_loop(0, n_sel, count, jnp.zeros((n_sel, tq), F32))
    sel_ref[0, 0] = jnp.where(rank < top_n, 0.0, NEG_INF).astype(sel_ref.dtype)


def _b_cmp_attn(q, kc, vc, bias_c, batch, seq):
    n_sel = seq // B_SEL_BLOCK
    n_cmp_pad = seq // B_CMP_STRIDE
    n_cmp = (seq - B_CMP_LEN) // B_CMP_STRIDE + 1
    c = np.arange(n_cmp_pad)[None, :] * B_CMP_STRIDE
    j = np.arange(n_sel)[:, None] * B_SEL_BLOCK
    c2s = ((c < j + B_SEL_BLOCK) & (c + B_CMP_LEN > j) & (np.arange(n_cmp_pad)[None, :] < n_cmp)).astype(np.float32)
    kern = functools.partial(_b_cmp_attn_kernel, top_n=min(B_TOP_N, n_sel))
    return pl.pallas_call(
        kern,
        out_shape=(jax.ShapeDtypeStruct((batch, N_HEADS, B_HEAD_DIM, seq), F32),
                   jax.ShapeDtypeStruct((batch, B_KV_HEADS, n_sel, seq), BF16)),
        grid=(batch, B_KV_HEADS, seq // B_TILE),
        in_specs=[pl.BlockSpec((1, B_GROUP, B_TILE, B_HEAD_DIM), lambda b, n, i: (b, n, i, 0)),
                  pl.BlockSpec((1, 1, n_cmp_pad, B_HEAD_DIM), lambda b, n, i: (b, n, 0, 0)),
                  pl.BlockSpec((1, 1, B_HEAD_DIM, n_cmp_pad), lambda b, n, i: (b, n, 0, 0)),
                  pl.BlockSpec((B_GROUP, 1, B_TILE, n_cmp_pad), lambda b, n, i: (n, i, 0, 0)),
                  pl.BlockSpec((n_sel, n_cmp_pad), lambda b, n, i: (0, 0))],
        out_specs=(pl.BlockSpec((1, B_GROUP, B_HEAD_DIM, B_TILE), lambda b, n, i: (b, n, 0, i)),
                   pl.BlockSpec((1, 1, n_sel, B_TILE), lambda b, n, i: (b, n, 0, i))),
        scratch_shapes=[pltpu.VMEM((n_sel, B_TILE), F32)],
        compiler_params=_params("parallel", "parallel", "arbitrary"),
        name="b_cmp_attn",
    )(q, kc, vc.transpose(0, 1, 3, 2), bias_c, jnp.asarray(c2s, BF16))


def _b_sparse_kernel(qt_ref, ka_ref, vs_ref, wk_ref, vw_ref, sel_ref, bs_ref, bw_ref, oc_ref, gate_ref,
                     o_ref, acc_ref, *, delta_max, win_tiles):
    tq = B_SWEEP
    dh = B_HEAD_DIM
    cols = B_GROUP * tq
    qi = pl.program_id(2)
    qt = jnp.concatenate([qt_ref[0, g] for g in range(B_GROUP)], axis=1)
    q_aug = jnp.concatenate([qt, jnp.concatenate([sel_ref[0, 0]] * B_GROUP, axis=1)], axis=0)

    def sweep(k_ref, vt_ref, rhs, lo, hi, bias_fn):
        acc_ref[...] = jnp.zeros(acc_ref.shape, F32)

        def step(kt, m_old):
            start = pl.multiple_of(kt * tq, tq)
            s = _dot(k_ref[0, 0, pl.ds(start, tq), :], rhs) + bias_fn(kt)
            m_new = jnp.maximum(m_old, jnp.max(s, axis=0, keepdims=True))
            alpha = jnp.exp(m_old - m_new)
            p = jnp.exp(s - m_new).astype(BF16)
            acc_ref[...] = alpha * acc_ref[...] + _dot(vt_ref[0, 0, :, pl.ds(start, tq)], p)
            return m_new

        lax.fori_loop(lo, hi, step, jnp.full((1, cols), NEG_INF, F32))
        acc = acc_ref[...]
        return acc[:dh] * (1.0 / acc[dh:dh + 1])

    def sel_bias(kt):
        d = jnp.minimum(qi - kt, delta_max)
        return jnp.concatenate([bs_ref[g, d] for g in range(B_GROUP)], axis=1)

    def win_bias(kt):
        return jnp.concatenate([bw_ref[g, qi - kt] for g in range(B_GROUP)], axis=1)

    o_s = sweep(ka_ref, vs_ref, q_aug, 0, qi + 1, sel_bias)
    o_w = sweep(wk_ref, vw_ref, qt, jnp.maximum(qi - (win_tiles - 1), 0), qi + 1, win_bias)

    gate = gate_ref[0, 0]
    merged = []
    for g in range(B_GROUP):
        cs = slice(g * tq, (g + 1) * tq)
        merged.append(gate[3 * g:3 * g + 1] * oc_ref[0, g] + gate[3 * g + 1:3 * g + 2] * o_s[:, cs]
                      + gate[3 * g + 2:3 * g + 3] * o_w[:, cs])
    for pair in range(B_GROUP // 2):
        both = jnp.concatenate([merged[2 * pair], merged[2 * pair + 1]], axis=0)
        o_ref[0, :, pair * 2 * dh:(pair + 1) * 2 * dh] = both.T.astype(o_ref.dtype)


def _b_sparse(qt, ka, vs, wk, vw, sel, bias_s, bias_w, oc, gate, batch, seq):
    n_sel = seq // B_SEL_BLOCK
    dh = B_HEAD_DIM
    n_ds = bias_s.shape[1]
    n_dw = bias_w.shape[1]
    vrows = vs.shape[2]
    kern = functools.partial(_b_sparse_kernel, delta_max=n_ds - 1, win_tiles=n_dw)
    whole = lambda rows, width: pl.BlockSpec((1, 1, rows, width), lambda b, n, i: (b, n, 0, 0))
    return pl.pallas_call(
        kern,
        out_shape=jax.ShapeDtypeStruct((batch, seq, N_HEADS * dh), BF16),
        grid=(batch, B_KV_HEADS, seq // B_SWEEP),
        in_specs=[pl.BlockSpec((1, B_GROUP, dh, B_SWEEP), lambda b, n, i: (b, n, 0, i)),
                  whole(seq, dh + n_sel), whole(vrows, seq), whole(seq, dh), whole(vrows, seq),
                  pl.BlockSpec((1, 1, n_sel, B_SWEEP), lambda b, n, i: (b, n, 0, i)),
                  pl.BlockSpec((B_GROUP, n_ds, B_SWEEP, B_SWEEP), lambda b, n, i: (n, 0, 0, 0)),
                  pl.BlockSpec((B_GROUP, n_dw, B_SWEEP, B_SWEEP), lambda b, n, i: (n, 0, 0, 0)),
                  pl.BlockSpec((1, B_GROUP, dh, B_SWEEP), lambda b, n, i: (b, n, 0, i)),
                  pl.BlockSpec((1, 1, 4 * B_GROUP, B_SWEEP), lambda b, n, i: (b, n, 0, i))],
        out_specs=pl.BlockSpec((1, B_SWEEP, B_GROUP * dh), lambda b, n, i: (b, i, n)),
        scratch_shapes=[pltpu.VMEM((vrows, B_GROUP * B_SWEEP), F32)],
        compiler_params=_params("parallel", "parallel", "arbitrary"),
        name="b_sparse",
    )(qt, ka, vs, wk, vw, sel, bias_s, bias_w, oc, gate)


def _mixer_b(x, rel_bias, norm1, w_in, q_gain, k_gain, cmp_pos, cmp_w1, cmp_w2, w_out, batch, seq):
    w_pad = jnp.pad(w_in, ((0, 0), (0, B_PROJ_WIDTH - w_in.shape[1]))).astype(BF16)
    proj = _norm_matmul(x, norm1, w_pad, BF16, 512).reshape(batch, seq, B_PROJ_WIDTH)
    q, ck, cv, sk, sv, wk, wv, gate = _b_prep(proj, q_gain, k_gain, batch, seq)
    kc, vc = _b_compress(ck, cv, cmp_pos, cmp_w1, cmp_w2, k_gain[0], batch, seq)
    n_tiles = seq // B_TILE
    bias_c = _bias_tiles(rel_bias, n_tiles, B_TILE, seq // B_CMP_STRIDE, base=1 - B_CMP_LEN, tile_step=B_TILE,
                         row_step=1, col_step=-B_CMP_STRIDE, dmax=1 << 30)
    oc, sel = _b_cmp_attn(q, kc, vc, bias_c, batch, seq)
    delta_max = min(seq // B_SWEEP - 1, -(-(_THRESHOLDS[-1] + B_SWEEP - 1) // B_SWEEP))
    bias_s = _bias_tiles(rel_bias, delta_max + 1, B_SWEEP, B_SWEEP, base=0, tile_step=B_SWEEP, row_step=-1,
                         col_step=1, dmax=1 << 30)
    win_tiles = (B_WINDOW - 1 + B_SWEEP - 1) // B_SWEEP + 1
    bias_w = _bias_tiles(rel_bias, win_tiles, B_SWEEP, B_SWEEP, base=0, tile_step=B_SWEEP, row_step=-1,
                         col_step=1, dmax=B_WINDOW - 1)
    n_sel = seq // B_SEL_BLOCK
    onehot = (np.arange(seq)[:, None] // B_SEL_BLOCK == np.arange(n_sel)[None, :]).astype(np.float32)
    ka = jnp.concatenate([sk, jnp.broadcast_to(jnp.asarray(onehot, BF16), sk.shape[:2] + onehot.shape)], axis=-1)
    ones = jnp.ones(sv.shape[:2] + (16, seq), BF16)
    vs = jnp.concatenate([sv.transpose(0, 1, 3, 2), ones], axis=2)
    vw = jnp.concatenate([wv.transpose(0, 1, 3, 2), ones], axis=2)
    gate_t = jnp.pad(gate.transpose(0, 1, 3, 2), ((0, 0), (0, 0), (0, B_GROUP), (0, 0)))
    o = _b_sparse(q.transpose(0, 1, 3, 2), ka, vs, wk, vw, sel, bias_s, bias_w, oc, gate_t, batch, seq)
    return _matmul_residual(o.reshape(batch * seq, -1), w_out.astype(BF16), x)


def _c_conv_kernel(cur_ref, halo_ref, w_ref, sm_ref, alog_ref, dtb_ref, qkv_ref, bg_ref):
    ts = cur_ref.shape[1]
    keep = jnp.where(pl.program_id(1) == 0, 0.0, 1.0)
    dk = C_HEAD_DIM
    for c in range(3 * C_HEADS):
        sl = slice(c * dk, (c + 1) * dk)
        xe = jnp.concatenate([halo_ref[0, :, sl].astype(F32) * keep, cur_ref[0, :, sl].astype(F32)], axis=0)
        y = None
        for j in range(C_CONV):
            off = 8 - (C_CONV - 1) + j
            term = w_ref[j:j + 1, sl] * xe[off:off + ts]
            y = term if y is None else y + term
        y = y * jax.nn.sigmoid(y)
        if c < 2 * C_HEADS:
            y = y * lax.rsqrt(jnp.sum(y * y, axis=-1, keepdims=True) + RMS_EPS)
        if c < C_HEADS:
            y = y * (dk ** -0.5)
        qkv_ref[0, :, sl] = y
    sm = sm_ref[0]
    a = sm + dtb_ref[...]
    softplus = jnp.maximum(a, 0.0) + jnp.log1p(jnp.exp(-jnp.abs(a)))
    g = -jnp.exp(alog_ref[...]) * softplus
    lane = lax.broadcasted_iota(jnp.int32, sm.shape, 1)
    bg_ref[0] = jnp.where(lane < C_HEADS, jax.nn.sigmoid(sm), g)


def _c_conv(proj, small, conv_w, a_log, dt_bias, batch, seq):
    ts = 256
    width = 3 * C_WIDTH
    pad = lambda v: jnp.pad(v, (C_HEADS, 128 - 2 * C_HEADS)).reshape(1, 128)
    return pl.pallas_call(
        _c_conv_kernel,
        out_shape=(jax.ShapeDtypeStruct((batch, seq, width), F32),
                   jax.ShapeDtypeStruct((batch, seq, 128), F32)),
        grid=(batch, seq // ts),
        in_specs=[pl.BlockSpec((1, ts, width), lambda b, i: (b, i, 0)),
                  pl.BlockSpec((1, 8, width), lambda b, i: (b, jnp.maximum(i * (ts // 8) - 1, 0), 0)),
                  pl.BlockSpec((C_CONV, width), lambda b, i: (0, 0)),
                  pl.BlockSpec((1, ts, 128), lambda b, i: (b, i, 0)),
                  pl.BlockSpec((1, 128), lambda b, i: (0, 0)),
                  pl.BlockSpec((1, 128), lambda b, i: (0, 0))],
        out_specs=(pl.BlockSpec((1, ts, width), lambda b, i: (b, i, 0)),
                   pl.BlockSpec((1, ts, 128), lambda b, i: (b, i, 0))),
        compiler_params=_params("parallel", "arbitrary"),
        name="c_conv",
    )(proj, proj, conv_w, small, pad(a_log), pad(dt_bias))


def _sum3(x, fn):
    hi = x.astype(BF16)
    r = x - hi.astype(F32)
    mid = r.astype(BF16)
    lo = (r - mid.astype(F32)).astype(BF16)
    return fn(hi) + (fn(mid) + fn(lo))


def _c_chunk_kernel(qkv_ref, bg_ref, bgt_ref, tri_ref, trit_ref, blk_ref, u_ref, w_ref, qg_ref, kg_ref, attn_ref,
                    gc_ref):
    cs = C_CHUNK
    dk = C_HEAD_DIM
    gs = C_GROUP * cs
    row = lax.broadcasted_iota(jnp.int32, (gs, gs), 0)
    col = lax.broadcasted_iota(jnp.int32, (gs, gs), 1)
    same = (row // cs) == (col // cs)
    causal = same & (row >= col)
    strict = same & (row > col)
    eye = jnp.where(row == col, 1.0, 0.0)

    bgc = bg_ref[0]
    tri = tri_ref[...]
    gcum_col = _sum3(bgc, lambda p: _dot(tri, p))
    glast_col = _sum3(bgc, lambda p: _dot(blk_ref[...], p))
    gcum_row = _sum3(bgt_ref[0], lambda p: _dot(p, trit_ref[...]))
    gc_ref[0] = gcum_col
    for h in range(C_HEADS):
        gc = gcum_col[:, C_HEADS + h:C_HEADS + h + 1]
        gr = gcum_row[C_HEADS + h:C_HEADS + h + 1, :]
        beta = bgc[:, h:h + 1]
        q = qkv_ref[0, :, h * dk:(h + 1) * dk]
        k = qkv_ref[0, :, C_WIDTH + h * dk:C_WIDTH + (h + 1) * dk]
        v = qkv_ref[0, :, 2 * C_WIDTH + h * dk:2 * C_WIDTH + (h + 1) * dk]
        decay = jnp.exp(jnp.where(causal, gc - gr, NEG_INF))
        kb = k * beta
        k16 = k.astype(BF16)
        low = jnp.where(strict, _dot_nt(kb.astype(BF16), k16) * decay, 0.0)
        t_mat = eye - low
        power = low
        for _ in range(int(math.log2(cs)) - 1):
            p16 = power.astype(BF16)
            power = _dot(p16, p16)
            t_mat = t_mat + _dot(t_mat.astype(BF16), power.astype(BF16))
        t16 = t_mat.astype(BF16)
        egc = jnp.exp(gc)
        u_ref[0, :, h * dk:(h + 1) * dk] = _dot(t16, (v * beta).astype(BF16))
        w_ref[0, :, h * dk:(h + 1) * dk] = _dot(t16, (kb * egc).astype(BF16)).astype(w_ref.dtype)
        attn = jnp.where(causal, _dot_nt(q.astype(BF16), k16), 0.0) * decay
        attn_ref[0, :, h * gs:(h + 1) * gs] = attn.astype(attn_ref.dtype)
        qg_ref[0, :, h * dk:(h + 1) * dk] = (q * egc).astype(qg_ref.dtype)
        glast = glast_col[:, C_HEADS + h:C_HEADS + h + 1]
        kg_ref[0, :, h * dk:(h + 1) * dk] = (k * jnp.exp(glast - gc)).astype(kg_ref.dtype)


def _c_chunks(qkv, bg, bgt, batch, seq):
    gs = C_GROUP * C_CHUNK
    idx = np.arange(gs)
    same = (idx[:, None] // C_CHUNK) == (idx[None, :] // C_CHUNK)
    tri = (same & (idx[:, None] >= idx[None, :])).astype(np.float32)
    wide = lambda width: pl.BlockSpec((1, gs, width), lambda b, i: (b, i, 0))
    shape = lambda width, dtype: jax.ShapeDtypeStruct((batch, seq, width), dtype)
    const = pl.BlockSpec((gs, gs), lambda b, i: (0, 0))
    return pl.pallas_call(
        _c_chunk_kernel,
        out_shape=(shape(C_WIDTH, F32), shape(C_WIDTH, BF16), shape(C_WIDTH, BF16), shape(C_WIDTH, BF16),
                   shape(C_HEADS * gs, BF16), shape(128, F32)),
        grid=(batch, seq // gs),
        in_specs=[wide(3 * C_WIDTH), wide(128),
                  pl.BlockSpec((1, 2 * C_HEADS, gs), lambda b, i: (b, 0, i)),
                  const, const, const],
        out_specs=(wide(C_WIDTH),) * 4 + (wide(C_HEADS * gs), wide(128)),
        compiler_params=_params("parallel", "parallel"),
        name="c_chunks",
    )(qkv, bg, bgt, jnp.asarray(tri, BF16), jnp.asarray(tri.T, BF16), jnp.asarray(same, BF16))


def _c_scan_kernel(u_ref, w_ref, qg_ref, kg_ref, attn_ref, gc_ref, o_ref, state_ref, vnew_ref):
    @pl.when(pl.program_id(1) == 0)
    def _():
        state_ref[...] = jnp.zeros_like(state_ref)

    cs = C_CHUNK
    dk = C_HEAD_DIM
    gs = C_GROUP * cs
    vnew_ref[...] = jnp.zeros_like(vnew_ref)
    for c in range(C_GROUP):
        rs = slice(c * cs, (c + 1) * cs)
        decay_last = jnp.exp(gc_ref[0, (c + 1) * cs - 1:(c + 1) * cs, :])
        for h in range(C_HEADS):
            sl = slice(h * dk, (h + 1) * dk)
            state = state_ref[h]
            s16 = state.astype(BF16)
            v_new = u_ref[0, rs, sl] - _dot(w_ref[0, rs, sl], s16)
            v16 = v_new.astype(BF16)
            vnew_ref[h, rs, :] = v16
            o_ref[0, rs, sl] = _dot(qg_ref[0, rs, sl], s16) + _dot(attn_ref[0, rs, h * gs:(h + 1) * gs], vnew_ref[h])
            state_ref[h] = state * decay_last[:, C_HEADS + h:C_HEADS + h + 1] + _dot_tn(kg_ref[0, rs, sl], v16)


def _c_scan(u, w, qg, kg, attn, gc, batch, seq):
    gs = C_GROUP * C_CHUNK
    wide = lambda width: pl.BlockSpec((1, gs, width), lambda b, c: (b, c, 0))
    return pl.pallas_call(
        _c_scan_kernel,
        out_shape=jax.ShapeDtypeStruct((batch, seq, C_WIDTH), F32),
        grid=(batch, seq // gs),
        in_specs=[wide(C_WIDTH)] * 4 + [wide(C_HEADS * gs), wide(128)],
        out_specs=wide(C_WIDTH),
        scratch_shapes=[pltpu.VMEM((C_HEADS, C_HEAD_DIM, C_HEAD_DIM), F32),
                        pltpu.VMEM((C_HEADS, gs, C_HEAD_DIM), BF16)],
        compiler_params=_params("parallel", "arbitrary"),
        name="c_scan",
    )(u, w, qg, kg, attn, gc)


def _c_out_kernel(o_ref, z_ref, g_ref, w_ref, x_ref, out_ref):
    dk = C_HEAD_DIM
    parts = []
    for h in range(C_HEADS):
        sl = slice(h * dk, (h + 1) * dk)
        z = z_ref[:, sl].astype(F32)
        parts.append((_rms(o_ref[:, sl], g_ref[...]) * (z * jax.nn.sigmoid(z))).astype(BF16))
    out_ref[...] = x_ref[...] + _dot(jnp.concatenate(parts, axis=-1), w_ref[...])


def _c_out(o, proj, out_gain, w_out, x):
    m, d = x.shape
    z_block = (3 * C_WIDTH) // C_WIDTH
    row = lambda width: pl.BlockSpec((ROW_TILE, width), lambda i: (i, 0))
    return pl.pallas_call(
        _c_out_kernel,
        out_shape=jax.ShapeDtypeStruct((m, d), F32),
        grid=(m // ROW_TILE,),
        in_specs=[row(C_WIDTH),
                  pl.BlockSpec((ROW_TILE, C_WIDTH), lambda i: (i, z_block)),
                  pl.BlockSpec((1, C_HEAD_DIM), lambda i: (0, 0)),
                  pl.BlockSpec((C_WIDTH, d), lambda i: (0, 0)),
                  row(d)],
        out_specs=row(d),
        compiler_params=_params("parallel"),
        name="c_out",
    )(o, proj, out_gain.reshape(1, -1), w_out, x)


def _mixer_c(x, norm1, w_in, conv_w, a_log, dt_bias, out_gain, w_out, batch, seq):
    main = 4 * C_WIDTH
    proj = _norm_matmul(x, norm1, w_in[:, :main].astype(BF16), BF16, 512)
    w_small = jnp.pad(w_in[:, main:], ((0, 0), (0, 128 - 2 * C_HEADS))).astype(BF16)
    small = _norm_matmul(x, norm1, w_small, F32, 128)
    qkv, bg = _c_conv(proj.reshape(batch, seq, main), small.reshape(batch, seq, 128), conv_w, a_log, dt_bias,
                      batch, seq)
    bgt = bg[:, :, :2 * C_HEADS].transpose(0, 2, 1)
    u, w, qg, kg, attn, gc = _c_chunks(qkv, bg, bgt, batch, seq)
    o = _c_scan(u, w, qg, kg, attn, gc, batch, seq)
    return _c_out(o.reshape(batch * seq, C_WIDTH), proj, out_gain, w_out.astype(BF16), x)


def kernel(x, rel_bias, l0_norm1, l0_a_w_in, l0_a_q_gain, l0_a_k_gain, l0_a_w_out, l0_norm2, l0_ffn_w_gate, l0_ffn_w_up, l0_ffn_w_down, l1_norm1, l1_b_w_in, l1_b_q_gain, l1_b_k_gain, l1_b_cmp_pos, l1_b_cmp_w1, l1_b_cmp_w2, l1_b_w_out, l1_norm2, l1_ffn_w_gate, l1_ffn_w_up, l1_ffn_w_down, l2_norm1, l2_c_w_in, l2_c_conv_w, l2_c_a_log, l2_c_dt_bias, l2_c_out_gain, l2_c_w_out, l2_norm2, l2_ffn_w_gate, l2_ffn_w_up, l2_ffn_w_down, l3_norm1, l3_a_w_in, l3_a_q_gain, l3_a_k_gain, l3_a_w_out, l3_norm2, l3_ffn_w_gate, l3_ffn_w_up, l3_ffn_w_down):
    batch, seq, d = x.shape
    h = x.reshape(batch * seq, d)

    def ffn(h, norm2, w_gate, w_up, w_down):
        return _ffn(h, norm2, w_gate.astype(BF16), w_up.astype(BF16), w_down.astype(BF16))

    h = _mixer_a(h, rel_bias, l0_norm1, l0_a_w_in, l0_a_q_gain, l0_a_k_gain, l0_a_w_out, batch, seq)
    h = ffn(h, l0_norm2, l0_ffn_w_gate, l0_ffn_w_up, l0_ffn_w_down)
    h = _mixer_b(h, rel_bias, l1_norm1, l1_b_w_in, l1_b_q_gain, l1_b_k_gain, l1_b_cmp_pos, l1_b_cmp_w1,
                 l1_b_cmp_w2, l1_b_w_out, batch, seq)
    h = ffn(h, l1_norm2, l1_ffn_w_gate, l1_ffn_w_up, l1_ffn_w_down)
    h = _mixer_c(h, l2_norm1, l2_c_w_in, l2_c_conv_w, l2_c_a_log, l2_c_dt_bias, l2_c_out_gain, l2_c_w_out,
                 batch, seq)
    h = ffn(h, l2_norm2, l2_ffn_w_gate, l2_ffn_w_up, l2_ffn_w_down)
    h = _mixer_a(h, rel_bias, l3_norm1, l3_a_w_in, l3_a_q_gain, l3_a_k_gain, l3_a_w_out, batch, seq)
    h = ffn(h, l3_norm2, l3_ffn_w_gate, l3_ffn_w_up, l3_ffn_w_down)
    return h.reshape(batch, seq, d)
```

```python
import functools
import math

import numpy as np
import jax
import jax.numpy as jnp
from jax import lax
from jax.experimental import pallas as pl
from jax.experimental.pallas import tpu as pltpu

D_MODEL = 1024
RMS_EPS = 1e-6
NEG_INF = -1e30
TINY = 1e-30
FORCE_SCORE = 1e9

N_BUCKETS = 32
REL_MAX_DISTANCE = 2048
N_HEADS = 16

A_GROUPS = ((128, 1), (512, 4), (2048, 16))
A_HEAD_DIM = 64
A_Q_BLOCK = 128
A_PROJ_TILE = 512

B_KV_HEADS = 4
B_GROUP = 4
B_HEAD_DIM = 64
B_CMP_LEN = 32
B_CMP_STRIDE = 16
B_CMP_HIDDEN = 256
B_SEL_BLOCK = 64
B_TOP_N = 16
B_WINDOW = 512
B_TILE = 128
B_SWEEP = 256
B_PROJ_WIDTH = 3072

C_HEADS = 8
C_HEAD_DIM = 128
C_WIDTH = C_HEADS * C_HEAD_DIM
C_CONV = 4
C_CHUNK = 64
C_GROUP = 4

FFN_HIDDEN = 2816
FFN_TILE = 1024

ROW_TILE = 512
VMEM_LIMIT = 48 * 1024 * 1024

F32 = jnp.float32
BF16 = jnp.bfloat16
HIGHEST = lax.Precision.HIGHEST

NT_DIMS = (((1,), (1,)), ((), ()))
TN_DIMS = (((0,), (0,)), ((), ()))


def _params(*semantics):
    return pltpu.CompilerParams(dimension_semantics=semantics, vmem_limit_bytes=VMEM_LIMIT)


def _dot(a, b, precision=None):
    return jnp.dot(a, b, preferred_element_type=F32, precision=precision)


def _dot_nt(a, b, precision=None):
    return lax.dot_general(a, b, NT_DIMS, preferred_element_type=F32, precision=precision)


def _dot_tn(a, b, precision=None):
    return lax.dot_general(a, b, TN_DIMS, preferred_element_type=F32, precision=precision)


def _rms(x, gain):
    return x * lax.rsqrt(jnp.mean(x * x, axis=-1, keepdims=True) + RMS_EPS) * gain


def _bucket_thresholds():
    d = np.arange(1 << 15)
    max_exact = N_BUCKETS // 2
    d_f = np.maximum(d, 1).astype(np.float32)
    large = max_exact + (np.log(d_f / np.float32(max_exact)) / np.float32(math.log(REL_MAX_DISTANCE / max_exact))
                         * np.float32(N_BUCKETS - max_exact)).astype(np.int32)
    bucket = np.where(d < max_exact, d, np.minimum(large, N_BUCKETS - 1))
    return [int(np.argmax(bucket >= k)) if np.any(bucket >= k) else int(1 << 30) for k in range(N_BUCKETS)]


_THRESHOLDS = _bucket_thresholds()


def _bias_tile_kernel(tbl_ref, o_ref, *, base, tile_step, row_step, col_step, dmax, dil):
    h = pl.program_id(0)
    t = pl.program_id(1)
    shape = o_ref.shape[2:]
    i = lax.broadcasted_iota(jnp.int32, shape, 0)
    j = lax.broadcasted_iota(jnp.int32, shape, 1)
    dist = base + tile_step * t + row_step * i + col_step * j
    d = dist * dil
    val = jnp.full(shape, tbl_ref[0, h], F32)
    for k in range(1, N_BUCKETS):
        val = jnp.where(d >= _THRESHOLDS[k], tbl_ref[k, h], val)
    valid = (dist >= 0) & (dist <= dmax)
    o_ref[0, 0] = jnp.where(valid, val, NEG_INF)


def _bias_tiles(rel_bias, n_tiles, rows, cols, *, base, tile_step, row_step, col_step, dmax, dil=1):
    kern = functools.partial(_bias_tile_kernel, base=base, tile_step=tile_step, row_step=row_step,
                             col_step=col_step, dmax=dmax, dil=dil)
    return pl.pallas_call(
        kern,
        out_shape=jax.ShapeDtypeStruct((N_HEADS, n_tiles, rows, cols), F32),
        grid=(N_HEADS, n_tiles),
        in_specs=[pl.BlockSpec(memory_space=pltpu.SMEM)],
        out_specs=pl.BlockSpec((1, 1, rows, cols), lambda h, t: (h, t, 0, 0)),
        compiler_params=_params("parallel", "parallel"),
        name="bias_tiles",
    )(rel_bias)


def _resident(shape):
    return pl.BlockSpec(shape, lambda i: (0,) * len(shape), pipeline_mode=pl.Buffered(1))


def _norm_matmul_kernel(x_ref, g_ref, w_ref, o_ref, *, tn):
    h = _rms(x_ref[...], g_ref[...]).astype(BF16)
    for j in range(w_ref.shape[1] // tn):
        o_ref[:, j * tn:(j + 1) * tn] = _dot(h, w_ref[:, j * tn:(j + 1) * tn]).astype(o_ref.dtype)


def _norm_matmul(x, gain, w, out_dtype, tn):
    m, d = x.shape
    n = w.shape[1]
    return pl.pallas_call(
        functools.partial(_norm_matmul_kernel, tn=tn),
        out_shape=jax.ShapeDtypeStruct((m, n), out_dtype),
        grid=(m // ROW_TILE,),
        in_specs=[pl.BlockSpec((ROW_TILE, d), lambda i: (i, 0)),
                  _resident((1, d)),
                  _resident((d, n))],
        out_specs=pl.BlockSpec((ROW_TILE, n), lambda i: (i, 0)),
        compiler_params=_params("parallel"),
        name="norm_matmul",
    )(x, gain.reshape(1, d), w)


def _matmul_residual_kernel(a_ref, w_ref, x_ref, o_ref):
    o_ref[...] = x_ref[...] + _dot(a_ref[...], w_ref[...])


def _matmul_residual(a, w, x):
    m, k = a.shape
    d = w.shape[1]
    return pl.pallas_call(
        _matmul_residual_kernel,
        out_shape=jax.ShapeDtypeStruct((m, d), F32),
        grid=(m // ROW_TILE,),
        in_specs=[pl.BlockSpec((ROW_TILE, k), lambda i: (i, 0)),
                  pl.BlockSpec((k, d), lambda i: (0, 0)),
                  pl.BlockSpec((ROW_TILE, d), lambda i: (i, 0))],
        out_specs=pl.BlockSpec((ROW_TILE, d), lambda i: (i, 0)),
        compiler_params=_params("parallel"),
        name="matmul_residual",
    )(a, w, x)


def _ffn_kernel(x_ref, g_ref, wg_ref, wu_ref, wd_ref, o_ref):
    x = x_ref[...]
    h = _rms(x, g_ref[...]).astype(BF16)
    hidden = wg_ref.shape[1]
    acc = x
    for lo in range(0, hidden, FFN_TILE):
        hi = min(lo + FFN_TILE, hidden)
        a = _dot(h, wg_ref[:, lo:hi])
        b = _dot(h, wu_ref[:, lo:hi])
        acc = acc + _dot((a * jax.nn.sigmoid(a) * b).astype(BF16), wd_ref[lo:hi, :])
    o_ref[...] = acc


def _ffn(x, gain, w_gate, w_up, w_down):
    m, d = x.shape
    hidden = w_gate.shape[1]
    return pl.pallas_call(
        _ffn_kernel,
        out_shape=jax.ShapeDtypeStruct((m, d), F32),
        grid=(m // ROW_TILE,),
        in_specs=[pl.BlockSpec((ROW_TILE, d), lambda i: (i, 0)),
                  _resident((1, d)),
                  _resident((d, hidden)), _resident((d, hidden)), _resident((hidden, d))],
        out_specs=pl.BlockSpec((ROW_TILE, d), lambda i: (i, 0)),
        compiler_params=_params("parallel"),
        name="ffn",
    )(x, gain.reshape(1, d), w_gate, w_up, w_down)


def _a_proj_kernel(x_ref, g_ref, w_ref, qg_ref, kg_ref, o_ref, h_ref, x_scr, *, dil):
    rows = ROW_TILE // dil
    xn = _rms(x_ref[...], g_ref[...])
    if dil == 1:
        h_ref[...] = xn.astype(BF16)
    else:
        slabs = xn.shape[1] // 128
        for c in range(slabs):
            x_scr[c] = xn[:, c * 128:(c + 1) * 128]
        for r in range(dil):
            picked = [x_scr[c, pl.ds(r, rows, stride=dil), :] for c in range(slabs)]
            h_ref[r * rows:(r + 1) * rows, :] = jnp.concatenate(picked, axis=1).astype(BF16)
    h = h_ref[...]
    width = w_ref.shape[1]
    hd = width // 3
    low = lax.broadcasted_iota(jnp.int32, (ROW_TILE, 128), 1) < A_HEAD_DIM
    for j in range(width // A_PROJ_TILE):
        res = _dot(h, w_ref[:, j * A_PROJ_TILE:(j + 1) * A_PROJ_TILE])
        kind = (j * A_PROJ_TILE) // hd
        if kind < 2:
            parts = []
            for c in range(A_PROJ_TILE // 128):
                y = res[:, c * 128:(c + 1) * 128]
                sq = y * y
                tot = jnp.sum(sq, axis=-1, keepdims=True)
                lo = jnp.sum(jnp.where(low, sq, 0.0), axis=-1, keepdims=True)
                ss = jnp.where(low, lo, tot - lo)
                parts.append(y * lax.rsqrt(ss * (1.0 / A_HEAD_DIM) + RMS_EPS))
            res = jnp.concatenate(parts, axis=1) * (qg_ref if kind == 0 else kg_ref)[...]
        res = res.astype(BF16)
        for r in range(dil):
            off = r * width + j * A_PROJ_TILE
            o_ref[:, off:off + A_PROJ_TILE] = res[r * rows:(r + 1) * rows]


def _a_proj(x, gain, w, q_gain, k_gain, dil):
    m, d = x.shape
    width = w.shape[1]
    reps = A_PROJ_TILE // A_HEAD_DIM
    qg = jnp.tile(q_gain * (A_HEAD_DIM ** -0.5), reps).reshape(1, A_PROJ_TILE)
    kg = jnp.tile(k_gain, reps).reshape(1, A_PROJ_TILE)
    return pl.pallas_call(
        functools.partial(_a_proj_kernel, dil=dil),
        out_shape=jax.ShapeDtypeStruct((m // dil, dil * width), BF16),
        grid=(m // ROW_TILE,),
        in_specs=[pl.BlockSpec((ROW_TILE, d), lambda i: (i, 0)),
                  pl.BlockSpec((1, d), lambda i: (0, 0)),
                  pl.BlockSpec((d, width), lambda i: (0, 0)),
                  pl.BlockSpec((1, A_PROJ_TILE), lambda i: (0, 0)),
                  pl.BlockSpec((1, A_PROJ_TILE), lambda i: (0, 0))],
        out_specs=pl.BlockSpec((ROW_TILE // dil, dil * width), lambda i: (i, 0)),
        scratch_shapes=[pltpu.VMEM((ROW_TILE, d), BF16), pltpu.VMEM((d // 128, ROW_TILE, 128), F32)],
        compiler_params=_params("parallel"),
        name="a_proj",
    )(x, gain.reshape(1, d), w, qg, kg)


def _a_attn_kernel(q_ref, kp_ref, kc_ref, vp_ref, vc_ref, bias_ref, o_ref, lse_ref):
    first = (pl.program_id(2) == 0).astype(jnp.int32)
    lane = lax.broadcasted_iota(jnp.int32, (A_Q_BLOCK, 128), 1)
    low = lane < A_HEAD_DIM
    ones = jnp.ones((A_Q_BLOCK, 128), BF16)
    lse_tile = jnp.zeros((A_Q_BLOCK, 128), F32)
    nq = A_Q_BLOCK
    for pair in range(N_HEADS // 2):
        sl = slice(pair * 128, (pair + 1) * 128)
        q = q_ref[0, :, sl]
        zero = jnp.zeros_like(q)
        qq = jnp.concatenate([jnp.where(low, q, zero), jnp.where(low, zero, q)], axis=0)
        kk = jnp.concatenate([kp_ref[0, :, sl], kc_ref[0, :, sl]], axis=0)
        vv = jnp.concatenate([jnp.concatenate([vp_ref[0, :, sl], ones], axis=1),
                              jnp.concatenate([vc_ref[0, :, sl], ones], axis=1)], axis=0)
        base = 2 * pair + N_HEADS * first
        s = _dot_nt(qq, kk) + jnp.concatenate([bias_ref[base], bias_ref[base + 1]], axis=0)
        m = jnp.max(s, axis=-1, keepdims=True)
        acc = _dot(jnp.exp(s - m).astype(BF16), vv)
        z = acc[:, 128:]
        o = acc[:, :128] * (1.0 / z)
        lse = m + jnp.log(z)
        o_ref[0, :, sl] = jnp.where(low, o[:nq], o[nq:]).astype(o_ref.dtype)
        lse_tile = jnp.where(lane == 2 * pair, lse[:nq], jnp.where(lane == 2 * pair + 1, lse[nq:], lse_tile))
    lse_ref[0] = lse_tile


def _a_attention(proj, bias, dil, batch, seq):
    length = seq // dil
    nblk = length // A_Q_BLOCK
    hd = N_HEADS * A_HEAD_DIM
    pv = proj.reshape(batch, length, dil * 3 * hd)

    def spec(off, prev):
        if prev:
            return pl.BlockSpec((1, A_Q_BLOCK, hd), lambda b, r, i: (b, jnp.maximum(i - 1, 0), r * 3 + off))
        return pl.BlockSpec((1, A_Q_BLOCK, hd), lambda b, r, i: (b, i, r * 3 + off))

    o, lse = pl.pallas_call(
        _a_attn_kernel,
        out_shape=(jax.ShapeDtypeStruct((batch, length, dil * hd), BF16),
                   jax.ShapeDtypeStruct((batch, length, dil * 128), F32)),
        grid=(batch, dil, nblk),
        in_specs=[spec(0, False), spec(1, True), spec(1, False), spec(2, True), spec(2, False),
                  pl.BlockSpec((2 * N_HEADS, A_Q_BLOCK, 2 * A_Q_BLOCK), lambda b, r, i: (0, 0, 0))],
        out_specs=(pl.BlockSpec((1, A_Q_BLOCK, hd), lambda b, r, i: (b, i, r)),
                   pl.BlockSpec((1, A_Q_BLOCK, 128), lambda b, r, i: (b, i, r))),
        compiler_params=_params("parallel", "parallel", "arbitrary"),
        name="a_attention",
    )(pv, pv, pv, pv, pv, bias)
    return o.reshape(batch * length, dil * hd), lse.reshape(batch * length, dil * 128)


def _a_out_kernel(o0_ref, o1_ref, o2_ref, l0_ref, l1_ref, l2_ref, e_ref, w_ref, x_ref, out_ref, o_scr, l_scr):
    hd = N_HEADS * A_HEAD_DIM
    for g, (o_ref, l_ref) in enumerate(((o0_ref, l0_ref), (o1_ref, l1_ref), (o2_ref, l2_ref))):
        dil = A_GROUPS[g][1]
        rows = ROW_TILE // dil
        for r in range(dil):
            dst = pl.ds(r, rows, stride=dil) if dil > 1 else slice(None)
            l_scr[g, dst, :] = l_ref[:, r * 128:(r + 1) * 128]
            for c in range(hd // 128):
                o_scr[g, c, dst, :] = o_ref[:, r * hd + c * 128:r * hd + (c + 1) * 128].astype(F32)
    ls = [l_scr[g] for g in range(len(A_GROUPS))]
    m = jnp.maximum(jnp.maximum(ls[0], ls[1]), ls[2])
    es = [jnp.exp(l - m) for l in ls]
    inv = 1.0 / (es[0] + es[1] + es[2])
    expand = e_ref[...]
    acc = None
    for g, e in enumerate(es):
        wgt = e * inv
        hi = wgt.astype(BF16)
        lo = (wgt - hi.astype(F32)).astype(BF16)
        o_g = jnp.concatenate([o_scr[g, c] for c in range(hd // 128)], axis=1)
        term = (_dot(hi, expand) + _dot(lo, expand)) * o_g
        acc = term if acc is None else acc + term
    out_ref[...] = x_ref[...] + _dot(acc.astype(BF16), w_ref[...])


def _a_out(outs, lses, w_out, x):
    m, d = x.shape
    hd = N_HEADS * A_HEAD_DIM
    expand = np.zeros((128, hd), np.float32)
    for h in range(N_HEADS):
        expand[h, h * A_HEAD_DIM:(h + 1) * A_HEAD_DIM] = 1.0
    grouped = lambda width: [pl.BlockSpec((ROW_TILE // dil, dil * width), lambda i: (i, 0)) for _, dil in A_GROUPS]
    return pl.pallas_call(
        _a_out_kernel,
        out_shape=jax.ShapeDtypeStruct((m, d), F32),
        grid=(m // ROW_TILE,),
        in_specs=grouped(hd) + grouped(128) + [
            pl.BlockSpec((128, hd), lambda i: (0, 0)),
            pl.BlockSpec((hd, d), lambda i: (0, 0)),
            pl.BlockSpec((ROW_TILE, d), lambda i: (i, 0))],
        out_specs=pl.BlockSpec((ROW_TILE, d), lambda i: (i, 0)),
        scratch_shapes=[pltpu.VMEM((len(A_GROUPS), hd // 128, ROW_TILE, 128), F32),
                        pltpu.VMEM((len(A_GROUPS), ROW_TILE, 128), F32)],
        compiler_params=_params("parallel"),
        name="a_out",
    )(*outs, *lses, jnp.asarray(expand, BF16), w_out, x)


def _mixer_a(x, rel_bias, norm1, w_in, q_gain, k_gain, w_out, batch, seq):
    w_in = w_in.astype(BF16)
    group_width = 3 * N_HEADS * A_HEAD_DIM
    outs, lses = [], []
    for gi, (window, dil) in enumerate(A_GROUPS):
        steps = window // dil
        assert steps == A_Q_BLOCK and (seq // dil) % A_Q_BLOCK == 0 and seq % ROW_TILE == 0
        bias = _bias_tiles(rel_bias, 1, A_Q_BLOCK, 2 * A_Q_BLOCK, base=A_Q_BLOCK, tile_step=0, row_step=1,
                           col_step=-1, dmax=steps, dil=dil)[:, 0]
        bias = jnp.concatenate([bias, bias.at[:, :, :A_Q_BLOCK].set(NEG_INF)], axis=0)
        proj = _a_proj(x, norm1, w_in[:, gi * group_width:(gi + 1) * group_width], q_gain[gi], k_gain[gi], dil)
        o, lse = _a_attention(proj, bias, dil, batch, seq)
        outs.append(o)
        lses.append(lse)
    return _a_out(outs, lses, w_out.astype(BF16), x)


def _b_prep_kernel(p_ref, qg_ref, kg_ref, q_ref, ck_ref, cv_ref, sk_ref, sv_ref, wk_ref, wv_ref, gate_ref):
    dh = B_HEAD_DIM
    qg = qg_ref[...] * (dh ** -0.5)
    for h in range(N_HEADS):
        q_ref[0, h] = _rms(p_ref[0, :, h * dh:(h + 1) * dh].astype(F32), qg).astype(BF16)
    base = N_HEADS * dh
    outs = ((ck_ref, None), (cv_ref, None), (sk_ref, 1), (sv_ref, None), (wk_ref, 2), (wv_ref, None))
    for idx, (ref, gain_row) in enumerate(outs):
        for n in range(B_KV_HEADS):
            off = base + (idx * B_KV_HEADS + n) * dh
            t = p_ref[0, :, off:off + dh]
            if gain_row is not None:
                t = _rms(t.astype(F32), kg_ref[gain_row:gain_row + 1, :]).astype(BF16)
            ref[0, n] = t
    gate_off = base + 6 * B_KV_HEADS * dh
    gate = jax.nn.sigmoid(p_ref[0, :, gate_off:gate_off + 3 * N_HEADS].astype(F32))
    per = 3 * B_GROUP
    for n in range(B_KV_HEADS):
        gate_ref[0, n] = gate[:, n * per:(n + 1) * per]


def _b_prep(proj, q_gain, k_gain, batch, seq):
    ts = 256
    dh = B_HEAD_DIM
    kv_shape = jax.ShapeDtypeStruct((batch, B_KV_HEADS, seq, dh), BF16)
    kv_spec = pl.BlockSpec((1, B_KV_HEADS, ts, dh), lambda b, i: (b, 0, i, 0))
    return pl.pallas_call(
        _b_prep_kernel,
        out_shape=(jax.ShapeDtypeStruct((batch, N_HEADS, seq, dh), BF16),) + (kv_shape,) * 6
        + (jax.ShapeDtypeStruct((batch, B_KV_HEADS, seq, 3 * B_GROUP), F32),),
        grid=(batch, seq // ts),
        in_specs=[pl.BlockSpec((1, ts, B_PROJ_WIDTH), lambda b, i: (b, i, 0)),
                  pl.BlockSpec((1, dh), lambda b, i: (0, 0)),
                  pl.BlockSpec((3, dh), lambda b, i: (0, 0))],
        out_specs=(pl.BlockSpec((1, N_HEADS, ts, dh), lambda b, i: (b, 0, i, 0)),) + (kv_spec,) * 6
        + (pl.BlockSpec((1, B_KV_HEADS, ts, 3 * B_GROUP), lambda b, i: (b, 0, i, 0)),),
        compiler_params=_params("parallel", "parallel"),
        name="b_prep",
    )(proj, q_gain.reshape(1, dh), k_gain)


def _b_compress_kernel(tk_ref, tv_ref, pos_ref, w1_ref, w2_ref, kg_ref, kc_ref, vc_ref):
    half = (B_CMP_LEN // 2) * B_HEAD_DIM
    for kv, (t_ref, out_ref) in enumerate(((tk_ref, kc_ref), (tv_ref, vc_ref))):
        t = t_ref[0, 0].astype(F32)
        top = (t + pos_ref[kv, 0:1, :]).astype(BF16)
        bot = (t + pos_ref[kv, 1:2, :]).astype(BF16)
        a1 = _dot(top, w1_ref[kv, :half, :])
        a2 = _dot(bot, w1_ref[kv, half:, :])
        hidden = a1 + pltpu.roll(a2, a2.shape[0] - 1, 0)
        out = _dot(jax.nn.gelu(hidden).astype(BF16), w2_ref[kv])
        if kv == 0:
            out = _rms(out, kg_ref[...])
        out_ref[0, 0] = out.astype(out_ref.dtype)


def _b_compress(ck, cv, cmp_pos, cmp_w1, cmp_w2, k_gain0, batch, seq):
    rows = seq // B_CMP_STRIDE
    half = (B_CMP_LEN // 2) * B_HEAD_DIM
    tk = ck.reshape(batch, B_KV_HEADS, rows, half)
    tv = cv.reshape(batch, B_KV_HEADS, rows, half)
    pos = cmp_pos.reshape(2, 2, half)
    t_spec = pl.BlockSpec((1, 1, rows, half), lambda b, n: (b, n, 0, 0))
    o_spec = pl.BlockSpec((1, 1, rows, B_HEAD_DIM), lambda b, n: (b, n, 0, 0))
    shape = jax.ShapeDtypeStruct((batch, B_KV_HEADS, rows, B_HEAD_DIM), BF16)
    return pl.pallas_call(
        _b_compress_kernel,
        out_shape=(shape, shape),
        grid=(batch, B_KV_HEADS),
        in_specs=[t_spec, t_spec,
                  pl.BlockSpec((2, 2, half), lambda b, n: (0, 0, 0)),
                  pl.BlockSpec((2, 2 * half, B_CMP_HIDDEN), lambda b, n: (0, 0, 0)),
                  pl.BlockSpec((2, B_CMP_HIDDEN, B_HEAD_DIM), lambda b, n: (0, 0, 0)),
                  pl.BlockSpec((1, B_HEAD_DIM), lambda b, n: (0, 0))],
        out_specs=(o_spec, o_spec),
        compiler_params=_params("parallel", "parallel"),
        name="b_compress",
    )(tk, tv, pos, cmp_w1.astype(BF16), cmp_w2.astype(BF16), k_gain0.reshape(1, -1))


def _b_cmp_attn_kernel(qt_ref, kc_ref, vct_ref, bias_ref, c2s_ref, oc_ref, sel_ref, imp_ref, *, top_n):
    tq = B_SWEEP
    n_sel = imp_ref.shape[0]
    qt = jnp.concatenate([qt_ref[0, g] for g in range(B_GROUP)], axis=1)
    bias = jnp.concatenate([bias_ref[g, 0] for g in range(B_GROUP)], axis=1)
    s = _dot(kc_ref[0, 0], qt) + bias
    valid = bias > 0.5 * NEG_INF
    m = jnp.max(s, axis=0, keepdims=True)
    e = jnp.where(valid, jnp.exp(s - m), 0.0)
    z = jnp.maximum(jnp.sum(e, axis=0, keepdims=True), TINY)
    p = e * (1.0 / z)
    oct = _dot(vct_ref[0, 0], p.astype(BF16))
    for g in range(B_GROUP):
        oc_ref[0, g] = oct[:, g * tq:(g + 1) * tq]

    p_sum = p[:, 0:tq] + p[:, tq:2 * tq] + p[:, 2 * tq:3 * tq] + p[:, 3 * tq:4 * tq]
    hi = p_sum.astype(BF16)
    lo = (p_sum - hi.astype(F32)).astype(BF16)
    c2s = c2s_ref[...]
    imp = _dot(c2s, hi) + _dot(c2s, lo)

    t = pl.program_id(2) * tq + lax.broadcasted_iota(jnp.int32, (n_sel, tq), 1)
    blk = lax.broadcasted_iota(jnp.int32, (n_sel, tq), 0)
    cur = t // B_SEL_BLOCK
    forced = (blk == 0) | (blk == cur) | (blk == cur - 1)
    imp = jnp.where(forced, FORCE_SCORE, jnp.where(blk * B_SEL_BLOCK <= t, imp, NEG_INF))
    imp_ref[...] = imp

    def count(i, rank):
        row = imp_ref[pl.ds(i, 1), :]
        ahead = jnp.where(row > imp, 1.0, jnp.where(row == imp, jnp.where(blk > i, 1.0, 0.0), 0.0))
        return rank + ahead

    rank = lax.fori_loop(0, n_sel, count, jnp.zeros((n_sel, tq), F32))
    sel_ref[0, 0] = jnp.where(rank < top_n, 0.0, NEG_INF).astype(sel_ref.dtype)


def _b_cmp_attn(qt, kc, vc, bias_c, batch, seq):
    n_sel = seq // B_SEL_BLOCK
    n_cmp_pad = seq // B_CMP_STRIDE
    n_cmp = (seq - B_CMP_LEN) // B_CMP_STRIDE + 1
    c = np.arange(n_cmp_pad)[None, :] * B_CMP_STRIDE
    j = np.arange(n_sel)[:, None] * B_SEL_BLOCK
    c2s = ((c < j + B_SEL_BLOCK) & (c + B_CMP_LEN > j) & (np.arange(n_cmp_pad)[None, :] < n_cmp)).astype(np.float32)
    kern = functools.partial(_b_cmp_attn_kernel, top_n=min(B_TOP_N, n_sel))
    return pl.pallas_call(
        kern,
        out_shape=(jax.ShapeDtypeStruct((batch, N_HEADS, B_HEAD_DIM, seq), F32),
                   jax.ShapeDtypeStruct((batch, B_KV_HEADS, n_sel, seq), BF16)),
        grid=(batch, B_KV_HEADS, seq // B_SWEEP),
        in_specs=[pl.BlockSpec((1, B_GROUP, B_HEAD_DIM, B_SWEEP), lambda b, n, i: (b, n, 0, i)),
                  pl.BlockSpec((1, 1, n_cmp_pad, B_HEAD_DIM), lambda b, n, i: (b, n, 0, 0)),
                  pl.BlockSpec((1, 1, B_HEAD_DIM, n_cmp_pad), lambda b, n, i: (b, n, 0, 0)),
                  pl.BlockSpec((B_GROUP, 1, n_cmp_pad, B_SWEEP), lambda b, n, i: (n, i, 0, 0)),
                  pl.BlockSpec((n_sel, n_cmp_pad), lambda b, n, i: (0, 0))],
        out_specs=(pl.BlockSpec((1, B_GROUP, B_HEAD_DIM, B_SWEEP), lambda b, n, i: (b, n, 0, i)),
                   pl.BlockSpec((1, 1, n_sel, B_SWEEP), lambda b, n, i: (b, n, 0, i))),
        scratch_shapes=[pltpu.VMEM((n_sel, B_SWEEP), F32)],
        compiler_params=_params("parallel", "parallel", "arbitrary"),
        name="b_cmp_attn",
    )(qt, kc, vc.transpose(0, 1, 3, 2), bias_c, jnp.asarray(c2s, BF16))


def _b_sparse_kernel(qt_ref, ka_ref, vs_ref, wk_ref, vw_ref, sel_ref, bs_ref, bw_ref, oc_ref, gate_ref,
                     o_ref, acc_ref, *, delta_max, win_tiles):
    tq = B_SWEEP
    dh = B_HEAD_DIM
    cols = B_GROUP * tq
    qi = pl.program_id(2)
    qt = jnp.concatenate([qt_ref[0, g] for g in range(B_GROUP)], axis=1)
    q_aug = jnp.concatenate([qt, jnp.concatenate([sel_ref[0, 0]] * B_GROUP, axis=1)], axis=0)

    def sweep(k_ref, vt_ref, rhs, lo, hi, bias_fn):
        acc_ref[...] = jnp.zeros(acc_ref.shape, F32)

        def step(kt, m_old):
            start = pl.multiple_of(kt * tq, tq)
            s = _dot(k_ref[0, 0, pl.ds(start, tq), :], rhs) + bias_fn(kt)
            m_new = jnp.maximum(m_old, jnp.max(s, axis=0, keepdims=True))
            alpha = jnp.exp(m_old - m_new)
            p = jnp.exp(s - m_new).astype(BF16)
            acc_ref[...] = alpha * acc_ref[...] + _dot(vt_ref[0, 0, :, pl.ds(start, tq)], p)
            return m_new

        lax.fori_loop(lo, hi, step, jnp.full((1, cols), NEG_INF, F32))
        acc = acc_ref[...]
        return acc[:dh] * (1.0 / acc[dh:dh + 1])

    def sel_bias(kt):
        d = jnp.minimum(qi - kt, delta_max)
        return jnp.concatenate([bs_ref[g, d] for g in range(B_GROUP)], axis=1)

    def win_bias(kt):
        return jnp.concatenate([bw_ref[g, qi - kt] for g in range(B_GROUP)], axis=1)

    o_s = sweep(ka_ref, vs_ref, q_aug, 0, qi + 1, sel_bias)
    o_w = sweep(wk_ref, vw_ref, qt, jnp.maximum(qi - (win_tiles - 1), 0), qi + 1, win_bias)

    gate = gate_ref[0, 0]
    merged = []
    for g in range(B_GROUP):
        cs = slice(g * tq, (g + 1) * tq)
        merged.append(gate[3 * g:3 * g + 1] * oc_ref[0, g] + gate[3 * g + 1:3 * g + 2] * o_s[:, cs]
                      + gate[3 * g + 2:3 * g + 3] * o_w[:, cs])
    for pair in range(B_GROUP // 2):
        both = jnp.concatenate([merged[2 * pair], merged[2 * pair + 1]], axis=0)
        o_ref[0, :, pair * 2 * dh:(pair + 1) * 2 * dh] = both.T.astype(o_ref.dtype)


def _b_sparse(qt, ka, vs, wk, vw, sel, bias_s, bias_w, oc, gate, batch, seq):
    n_sel = seq // B_SEL_BLOCK
    dh = B_HEAD_DIM
    n_ds = bias_s.shape[1]
    n_dw = bias_w.shape[1]
    vrows = vs.shape[2]
    kern = functools.partial(_b_sparse_kernel, delta_max=n_ds - 1, win_tiles=n_dw)
    whole = lambda rows, width: pl.BlockSpec((1, 1, rows, width), lambda b, n, i: (b, n, 0, 0))
    return pl.pallas_call(
        kern,
        out_shape=jax.ShapeDtypeStruct((batch, seq, N_HEADS * dh), BF16),
        grid=(batch, B_KV_HEADS, seq // B_SWEEP),
        in_specs=[pl.BlockSpec((1, B_GROUP, dh, B_SWEEP), lambda b, n, i: (b, n, 0, i)),
                  whole(seq, dh + n_sel), whole(vrows, seq), whole(seq, dh), whole(vrows, seq),
                  pl.BlockSpec((1, 1, n_sel, B_SWEEP), lambda b, n, i: (b, n, 0, i)),
                  pl.BlockSpec((B_GROUP, n_ds, B_SWEEP, B_SWEEP), lambda b, n, i: (n, 0, 0, 0)),
                  pl.BlockSpec((B_GROUP, n_dw, B_SWEEP, B_SWEEP), lambda b, n, i: (n, 0, 0, 0)),
                  pl.BlockSpec((1, B_GROUP, dh, B_SWEEP), lambda b, n, i: (b, n, 0, i)),
                  pl.BlockSpec((1, 1, 4 * B_GROUP, B_SWEEP), lambda b, n, i: (b, n, 0, i))],
        out_specs=pl.BlockSpec((1, B_SWEEP, B_GROUP * dh), lambda b, n, i: (b, i, n)),
        scratch_shapes=[pltpu.VMEM((vrows, B_GROUP * B_SWEEP), F32)],
        compiler_params=_params("parallel", "parallel", "arbitrary"),
        name="b_sparse",
    )(qt, ka, vs, wk, vw, sel, bias_s, bias_w, oc, gate)


def _mixer_b(x, rel_bias, norm1, w_in, q_gain, k_gain, cmp_pos, cmp_w1, cmp_w2, w_out, batch, seq):
    w_pad = jnp.pad(w_in, ((0, 0), (0, B_PROJ_WIDTH - w_in.shape[1]))).astype(BF16)
    proj = _norm_matmul(x, norm1, w_pad, BF16, 512).reshape(batch, seq, B_PROJ_WIDTH)
    q, ck, cv, sk, sv, wk, wv, gate = _b_prep(proj, q_gain, k_gain, batch, seq)
    kc, vc = _b_compress(ck, cv, cmp_pos, cmp_w1, cmp_w2, k_gain[0], batch, seq)
    bias_c = _bias_tiles(rel_bias, seq // B_SWEEP, seq // B_CMP_STRIDE, B_SWEEP, base=1 - B_CMP_LEN,
                         tile_step=B_SWEEP, row_step=-B_CMP_STRIDE, col_step=1, dmax=1 << 30)
    qt = q.transpose(0, 1, 3, 2)
    oc, sel = _b_cmp_attn(qt, kc, vc, bias_c, batch, seq)
    delta_max = min(seq // B_SWEEP - 1, -(-(_THRESHOLDS[-1] + B_SWEEP - 1) // B_SWEEP))
    bias_s = _bias_tiles(rel_bias, delta_max + 1, B_SWEEP, B_SWEEP, base=0, tile_step=B_SWEEP, row_step=-1,
                         col_step=1, dmax=1 << 30)
    win_tiles = (B_WINDOW - 1 + B_SWEEP - 1) // B_SWEEP + 1
    bias_w = _bias_tiles(rel_bias, win_tiles, B_SWEEP, B_SWEEP, base=0, tile_step=B_SWEEP, row_step=-1,
                         col_step=1, dmax=B_WINDOW - 1)
    n_sel = seq // B_SEL_BLOCK
    onehot = (np.arange(seq)[:, None] // B_SEL_BLOCK == np.arange(n_sel)[None, :]).astype(np.float32)
    ka = jnp.concatenate([sk, jnp.broadcast_to(jnp.asarray(onehot, BF16), sk.shape[:2] + onehot.shape)], axis=-1)
    ones = jnp.ones(sv.shape[:2] + (16, seq), BF16)
    vs = jnp.concatenate([sv.transpose(0, 1, 3, 2), ones], axis=2)
    vw = jnp.concatenate([wv.transpose(0, 1, 3, 2), ones], axis=2)
    gate_t = jnp.pad(gate.transpose(0, 1, 3, 2), ((0, 0), (0, 0), (0, B_GROUP), (0, 0)))
    o = _b_sparse(qt, ka, vs, wk, vw, sel, bias_s, bias_w, oc, gate_t, batch, seq)
    return _matmul_residual(o.reshape(batch * seq, -1), w_out.astype(BF16), x)


def _c_conv_kernel(cur_ref, halo_ref, w_ref, sm_ref, alog_ref, dtb_ref, qkv_ref, bg_ref):
    ts = cur_ref.shape[1]
    keep = jnp.where(pl.program_id(1) == 0, 0.0, 1.0)
    dk = C_HEAD_DIM
    for c in range(3 * C_HEADS):
        sl = slice(c * dk, (c + 1) * dk)
        xe = jnp.concatenate([halo_ref[0, :, sl].astype(F32) * keep, cur_ref[0, :, sl].astype(F32)], axis=0)
        y = None
        for j in range(C_CONV):
            off = 8 - (C_CONV - 1) + j
            term = w_ref[j:j + 1, sl] * xe[off:off + ts]
            y = term if y is None else y + term
        y = y * jax.nn.sigmoid(y)
        if c < 2 * C_HEADS:
            y = y * lax.rsqrt(jnp.sum(y * y, axis=-1, keepdims=True) + RMS_EPS)
        if c < C_HEADS:
            y = y * (dk ** -0.5)
        qkv_ref[0, :, sl] = y
    sm = sm_ref[0]
    a = sm + dtb_ref[...]
    softplus = jnp.maximum(a, 0.0) + jnp.log1p(jnp.exp(-jnp.abs(a)))
    g = -jnp.exp(alog_ref[...]) * softplus
    lane = lax.broadcasted_iota(jnp.int32, sm.shape, 1)
    bg_ref[0] = jnp.where(lane < C_HEADS, jax.nn.sigmoid(sm), g)


def _c_conv(proj, small, conv_w, a_log, dt_bias, batch, seq):
    ts = 256
    width = 3 * C_WIDTH
    pad = lambda v: jnp.pad(v, (C_HEADS, 128 - 2 * C_HEADS)).reshape(1, 128)
    return pl.pallas_call(
        _c_conv_kernel,
        out_shape=(jax.ShapeDtypeStruct((batch, seq, width), F32),
                   jax.ShapeDtypeStruct((batch, seq, 128), F32)),
        grid=(batch, seq // ts),
        in_specs=[pl.BlockSpec((1, ts, width), lambda b, i: (b, i, 0)),
                  pl.BlockSpec((1, 8, width), lambda b, i: (b, jnp.maximum(i * (ts // 8) - 1, 0), 0)),
                  pl.BlockSpec((C_CONV, width), lambda b, i: (0, 0)),
                  pl.BlockSpec((1, ts, 128), lambda b, i: (b, i, 0)),
                  pl.BlockSpec((1, 128), lambda b, i: (0, 0)),
                  pl.BlockSpec((1, 128), lambda b, i: (0, 0))],
        out_specs=(pl.BlockSpec((1, ts, width), lambda b, i: (b, i, 0)),
                   pl.BlockSpec((1, ts, 128), lambda b, i: (b, i, 0))),
        compiler_params=_params("parallel", "arbitrary"),
        name="c_conv",
    )(proj, proj, conv_w, small, pad(a_log), pad(dt_bias))


def _sum3(x, fn):
    hi = x.astype(BF16)
    r = x - hi.astype(F32)
    mid = r.astype(BF16)
    lo = (r - mid.astype(F32)).astype(BF16)
    return fn(hi) + (fn(mid) + fn(lo))


def _c_chunk_kernel(qkv_ref, bg_ref, bgt_ref, tri_ref, trit_ref, blk_ref, u_ref, w_ref, qg_ref, kg_ref, attn_ref,
                    gc_ref):
    cs = C_CHUNK
    dk = C_HEAD_DIM
    gs = C_GROUP * cs
    row = lax.broadcasted_iota(jnp.int32, (gs, gs), 0)
    col = lax.broadcasted_iota(jnp.int32, (gs, gs), 1)
    same = (row // cs) == (col // cs)
    causal = same & (row >= col)
    strict = same & (row > col)
    eye = jnp.where(row == col, 1.0, 0.0)

    bgc = bg_ref[0]
    tri = tri_ref[...]
    gcum_col = _sum3(bgc, lambda p: _dot(tri, p))
    glast_col = _sum3(bgc, lambda p: _dot(blk_ref[...], p))
    gcum_row = _sum3(bgt_ref[0], lambda p: _dot(p, trit_ref[...]))
    gc_ref[0] = gcum_col
    for h in range(C_HEADS):
        gc = gcum_col[:, C_HEADS + h:C_HEADS + h + 1]
        gr = gcum_row[C_HEADS + h:C_HEADS + h + 1, :]
        beta = bgc[:, h:h + 1]
        q = qkv_ref[0, :, h * dk:(h + 1) * dk]
        k = qkv_ref[0, :, C_WIDTH + h * dk:C_WIDTH + (h + 1) * dk]
        v = qkv_ref[0, :, 2 * C_WIDTH + h * dk:2 * C_WIDTH + (h + 1) * dk]
        decay = jnp.exp(jnp.where(causal, gc - gr, NEG_INF))
        kb = k * beta
        k16 = k.astype(BF16)
        low = jnp.where(strict, _dot_nt(kb.astype(BF16), k16) * decay, 0.0)
        t_mat = eye - low
        power = low
        for _ in range(int(math.log2(cs)) - 1):
            p16 = power.astype(BF16)
            power = _dot(p16, p16)
            t_mat = t_mat + _dot(t_mat.astype(BF16), power.astype(BF16))
        t16 = t_mat.astype(BF16)
        egc = jnp.exp(gc)
        u_ref[0, :, h * dk:(h + 1) * dk] = _dot(t16, (v * beta).astype(BF16))
        w_ref[0, :, h * dk:(h + 1) * dk] = _dot(t16, (kb * egc).astype(BF16)).astype(w_ref.dtype)
        attn = jnp.where(causal, _dot_nt(q.astype(BF16), k16), 0.0) * decay
        attn_ref[0, :, h * gs:(h + 1) * gs] = attn.astype(attn_ref.dtype)
        qg_ref[0, :, h * dk:(h + 1) * dk] = (q * egc).astype(qg_ref.dtype)
        glast = glast_col[:, C_HEADS + h:C_HEADS + h + 1]
        kg_ref[0, :, h * dk:(h + 1) * dk] = (k * jnp.exp(glast - gc)).astype(kg_ref.dtype)


def _c_chunks(qkv, bg, bgt, batch, seq):
    gs = C_GROUP * C_CHUNK
    idx = np.arange(gs)
    same = (idx[:, None] // C_CHUNK) == (idx[None, :] // C_CHUNK)
    tri = (same & (idx[:, None] >= idx[None, :])).astype(np.float32)
    wide = lambda width: pl.BlockSpec((1, gs, width), lambda b, i: (b, i, 0))
    shape = lambda width, dtype: jax.ShapeDtypeStruct((batch, seq, width), dtype)
    const = pl.BlockSpec((gs, gs), lambda b, i: (0, 0))
    return pl.pallas_call(
        _c_chunk_kernel,
        out_shape=(shape(C_WIDTH, F32), shape(C_WIDTH, BF16), shape(C_WIDTH, BF16), shape(C_WIDTH, BF16),
                   shape(C_HEADS * gs, BF16), shape(128, F32)),
        grid=(batch, seq // gs),
        in_specs=[wide(3 * C_WIDTH), wide(128),
                  pl.BlockSpec((1, 2 * C_HEADS, gs), lambda b, i: (b, 0, i)),
                  const, const, const],
        out_specs=(wide(C_WIDTH),) * 4 + (wide(C_HEADS * gs), wide(128)),
        compiler_params=_params("parallel", "parallel"),
        name="c_chunks",
    )(qkv, bg, bgt, jnp.asarray(tri, BF16), jnp.asarray(tri.T, BF16), jnp.asarray(same, BF16))


def _c_scan_kernel(u_ref, w_ref, qg_ref, kg_ref, attn_ref, gc_ref, o_ref, state_ref, vnew_ref):
    @pl.when(pl.program_id(1) == 0)
    def _():
        state_ref[...] = jnp.zeros_like(state_ref)

    cs = C_CHUNK
    dk = C_HEAD_DIM
    gs = C_GROUP * cs
    vnew_ref[...] = jnp.zeros_like(vnew_ref)
    for c in range(C_GROUP):
        rs = slice(c * cs, (c + 1) * cs)
        decay_last = jnp.exp(gc_ref[0, (c + 1) * cs - 1:(c + 1) * cs, :])
        for h in range(C_HEADS):
            sl = slice(h * dk, (h + 1) * dk)
            state = state_ref[h]
            s16 = state.astype(BF16)
            v_new = u_ref[0, rs, sl] - _dot(w_ref[0, rs, sl], s16)
            v16 = v_new.astype(BF16)
            vnew_ref[h, rs, :] = v16
            o_ref[0, rs, sl] = _dot(qg_ref[0, rs, sl], s16) + _dot(attn_ref[0, rs, h * gs:(h + 1) * gs], vnew_ref[h])
            state_ref[h] = state * decay_last[:, C_HEADS + h:C_HEADS + h + 1] + _dot_tn(kg_ref[0, rs, sl], v16)


def _c_scan(u, w, qg, kg, attn, gc, batch, seq):
    gs = C_GROUP * C_CHUNK
    wide = lambda width: pl.BlockSpec((1, gs, width), lambda b, c: (b, c, 0))
    return pl.pallas_call(
        _c_scan_kernel,
        out_shape=jax.ShapeDtypeStruct((batch, seq, C_WIDTH), F32),
        grid=(batch, seq // gs),
        in_specs=[wide(C_WIDTH)] * 4 + [wide(C_HEADS * gs), wide(128)],
        out_specs=wide(C_WIDTH),
        scratch_shapes=[pltpu.VMEM((C_HEADS, C_HEAD_DIM, C_HEAD_DIM), F32),
                        pltpu.VMEM((C_HEADS, gs, C_HEAD_DIM), BF16)],
        compiler_params=_params("parallel", "arbitrary"),
        name="c_scan",
    )(u, w, qg, kg, attn, gc)


def _c_out_kernel(o_ref, z_ref, g_ref, w_ref, x_ref, out_ref):
    dk = C_HEAD_DIM
    parts = []
    for h in range(C_HEADS):
        sl = slice(h * dk, (h + 1) * dk)
        z = z_ref[:, sl].astype(F32)
        parts.append((_rms(o_ref[:, sl], g_ref[...]) * (z * jax.nn.sigmoid(z))).astype(BF16))
    out_ref[...] = x_ref[...] + _dot(jnp.concatenate(parts, axis=-1), w_ref[...])


def _c_out(o, proj, out_gain, w_out, x):
    m, d = x.shape
    z_block = (3 * C_WIDTH) // C_WIDTH
    row = lambda width: pl.BlockSpec((ROW_TILE, width), lambda i: (i, 0))
    return pl.pallas_call(
        _c_out_kernel,
        out_shape=jax.ShapeDtypeStruct((m, d), F32),
        grid=(m // ROW_TILE,),
        in_specs=[row(C_WIDTH),
                  pl.BlockSpec((ROW_TILE, C_WIDTH), lambda i: (i, z_block)),
                  pl.BlockSpec((1, C_HEAD_DIM), lambda i: (0, 0)),
                  pl.BlockSpec((C_WIDTH, d), lambda i: (0, 0)),
                  row(d)],
        out_specs=row(d),
        compiler_params=_params("parallel"),
        name="c_out",
    )(o, proj, out_gain.reshape(1, -1), w_out, x)


def _mixer_c(x, norm1, w_in, conv_w, a_log, dt_bias, out_gain, w_out, batch, seq):
    main = 4 * C_WIDTH
    proj = _norm_matmul(x, norm1, w_in[:, :main].astype(BF16), BF16, 512)
    w_small = jnp.pad(w_in[:, main:], ((0, 0), (0, 128 - 2 * C_HEADS))).astype(BF16)
    small = _norm_matmul(x, norm1, w_small, F32, 128)
    qkv, bg = _c_conv(proj.reshape(batch, seq, main), small.reshape(batch, seq, 128), conv_w, a_log, dt_bias,
                      batch, seq)
    bgt = bg[:, :, :2 * C_HEADS].transpose(0, 2, 1)
    u, w, qg, kg, attn, gc = _c_chunks(qkv, bg, bgt, batch, seq)
    o = _c_scan(u, w, qg, kg, attn, gc, batch, seq)
    return _c_out(o.reshape(batch * seq, C_WIDTH), proj, out_gain, w_out.astype(BF16), x)


def kernel(x, rel_bias, l0_norm1, l0_a_w_in, l0_a_q_gain, l0_a_k_gain, l0_a_w_out, l0_norm2, l0_ffn_w_gate, l0_ffn_w_up, l0_ffn_w_down, l1_norm1, l1_b_w_in, l1_b_q_gain, l1_b_k_gain, l1_b_cmp_pos, l1_b_cmp_w1, l1_b_cmp_w2, l1_b_w_out, l1_norm2, l1_ffn_w_gate, l1_ffn_w_up, l1_ffn_w_down, l2_norm1, l2_c_w_in, l2_c_conv_w, l2_c_a_log, l2_c_dt_bias, l2_c_out_gain, l2_c_w_out, l2_norm2, l2_ffn_w_gate, l2_ffn_w_up, l2_ffn_w_down, l3_norm1, l3_a_w_in, l3_a_q_gain, l3_a_k_gain, l3_a_w_out, l3_norm2, l3_ffn_w_gate, l3_ffn_w_up, l3_ffn_w_down):
    batch, seq, d = x.shape
    h = x.reshape(batch * seq, d)

    def ffn(h, norm2, w_gate, w_up, w_down):
        return _ffn(h, norm2, w_gate.astype(BF16), w_up.astype(BF16), w_down.astype(BF16))

    h = _mixer_a(h, rel_bias, l0_norm1, l0_a_w_in, l0_a_q_gain, l0_a_k_gain, l0_a_w_out, batch, seq)
    h = ffn(h, l0_norm2, l0_ffn_w_gate, l0_ffn_w_up, l0_ffn_w_down)
    h = _mixer_b(h, rel_bias, l1_norm1, l1_b_w_in, l1_b_q_gain, l1_b_k_gain, l1_b_cmp_pos, l1_b_cmp_w1,
                 l1_b_cmp_w2, l1_b_w_out, batch, seq)
    h = ffn(h, l1_norm2, l1_ffn_w_gate, l1_ffn_w_up, l1_ffn_w_down)
    h = _mixer_c(h, l2_norm1, l2_c_w_in, l2_c_conv_w, l2_c_a_log, l2_c_dt_bias, l2_c_out_gain, l2_c_w_out,
                 batch, seq)
    h = ffn(h, l2_norm2, l2_ffn_w_gate, l2_ffn_w_up, l2_ffn_w_down)
    h = _mixer_a(h, rel_bias, l3_norm1, l3_a_w_in, l3_a_q_gain, l3_a_k_gain, l3_a_w_out, batch, seq)
    h = ffn(h, l3_norm2, l3_ffn_w_gate, l3_ffn_w_up, l3_ffn_w_down)
    return h.reshape(batch, seq, d)
```

```python
import functools
import math

import numpy as np
import jax
import jax.numpy as jnp
from jax import lax
from jax.experimental import pallas as pl
from jax.experimental.pallas import tpu as pltpu

D_MODEL = 1024
RMS_EPS = 1e-6
NEG_INF = -1e30
TINY = 1e-30
FORCE_SCORE = 1e9

N_BUCKETS = 32
REL_MAX_DISTANCE = 2048
N_HEADS = 16

A_GROUPS = ((128, 1), (512, 4), (2048, 16))
A_HEAD_DIM = 64
A_Q_BLOCK = 128
A_PROJ_TILE = 512

B_KV_HEADS = 4
B_GROUP = 4
B_HEAD_DIM = 64
B_CMP_LEN = 32
B_CMP_STRIDE = 16
B_CMP_HIDDEN = 256
B_SEL_BLOCK = 64
B_TOP_N = 16
B_WINDOW = 512
B_TILE = 128
B_SWEEP = 256
B_PROJ_WIDTH = 3072

C_HEADS = 8
C_HEAD_DIM = 128
C_WIDTH = C_HEADS * C_HEAD_DIM
C_CONV = 4
C_CHUNK = 64
C_GROUP = 4

FFN_HIDDEN = 2816
FFN_TILE = 1024

ROW_TILE = 512
VMEM_LIMIT = 48 * 1024 * 1024

F32 = jnp.float32
BF16 = jnp.bfloat16
HIGHEST = lax.Precision.HIGHEST

NT_DIMS = (((1,), (1,)), ((), ()))
TN_DIMS = (((0,), (0,)), ((), ()))


def _params(*semantics):
    return pltpu.CompilerParams(dimension_semantics=semantics, vmem_limit_bytes=VMEM_LIMIT)


def _dot(a, b, precision=None):
    return jnp.dot(a, b, preferred_element_type=F32, precision=precision)


def _dot_nt(a, b, precision=None):
    return lax.dot_general(a, b, NT_DIMS, preferred_element_type=F32, precision=precision)


def _dot_tn(a, b, precision=None):
    return lax.dot_general(a, b, TN_DIMS, preferred_element_type=F32, precision=precision)


def _rms(x, gain):
    return x * lax.rsqrt(jnp.mean(x * x, axis=-1, keepdims=True) + RMS_EPS) * gain


def _bucket_thresholds():
    d = np.arange(1 << 15)
    max_exact = N_BUCKETS // 2
    d_f = np.maximum(d, 1).astype(np.float32)
    large = max_exact + (np.log(d_f / np.float32(max_exact)) / np.float32(math.log(REL_MAX_DISTANCE / max_exact))
                         * np.float32(N_BUCKETS - max_exact)).astype(np.int32)
    bucket = np.where(d < max_exact, d, np.minimum(large, N_BUCKETS - 1))
    return [int(np.argmax(bucket >= k)) if np.any(bucket >= k) else int(1 << 30) for k in range(N_BUCKETS)]


_THRESHOLDS = _bucket_thresholds()


def _bias_tile_kernel(tbl_ref, o_ref, *, base, tile_step, row_step, col_step, dmax, dil):
    h = pl.program_id(0)
    t = pl.program_id(1)
    shape = o_ref.shape[2:]
    i = lax.broadcasted_iota(jnp.int32, shape, 0)
    j = lax.broadcasted_iota(jnp.int32, shape, 1)
    dist = base + tile_step * t + row_step * i + col_step * j
    d = dist * dil
    val = jnp.full(shape, tbl_ref[0, h], F32)
    for k in range(1, N_BUCKETS):
        val = jnp.where(d >= _THRESHOLDS[k], tbl_ref[k, h], val)
    valid = (dist >= 0) & (dist <= dmax)
    o_ref[0, 0] = jnp.where(valid, val, NEG_INF)


def _bias_tiles(rel_bias, n_tiles, rows, cols, *, base, tile_step, row_step, col_step, dmax, dil=1):
    kern = functools.partial(_bias_tile_kernel, base=base, tile_step=tile_step, row_step=row_step,
                             col_step=col_step, dmax=dmax, dil=dil)
    return pl.pallas_call(
        kern,
        out_shape=jax.ShapeDtypeStruct((N_HEADS, n_tiles, rows, cols), F32),
        grid=(N_HEADS, n_tiles),
        in_specs=[pl.BlockSpec(memory_space=pltpu.SMEM)],
        out_specs=pl.BlockSpec((1, 1, rows, cols), lambda h, t: (h, t, 0, 0)),
        compiler_params=_params("parallel", "parallel"),
        name="bias_tiles",
    )(rel_bias)


def _resident(shape):
    return pl.BlockSpec(shape, lambda i: (0,) * len(shape), pipeline_mode=pl.Buffered(1))


def _norm_matmul_kernel(x_ref, g_ref, w_ref, o_ref, *, tn):
    h = _rms(x_ref[...], g_ref[...]).astype(BF16)
    for j in range(w_ref.shape[1] // tn):
        o_ref[:, j * tn:(j + 1) * tn] = _dot(h, w_ref[:, j * tn:(j + 1) * tn]).astype(o_ref.dtype)


def _norm_matmul(x, gain, w, out_dtype, tn):
    m, d = x.shape
    n = w.shape[1]
    return pl.pallas_call(
        functools.partial(_norm_matmul_kernel, tn=tn),
        out_shape=jax.ShapeDtypeStruct((m, n), out_dtype),
        grid=(m // ROW_TILE,),
        in_specs=[pl.BlockSpec((ROW_TILE, d), lambda i: (i, 0)),
                  _resident((1, d)),
                  _resident((d, n))],
        out_specs=pl.BlockSpec((ROW_TILE, n), lambda i: (i, 0)),
        compiler_params=_params("parallel"),
        name="norm_matmul",
    )(x, gain.reshape(1, d), w)


def _matmul_residual_kernel(a_ref, w_ref, x_ref, o_ref):
    o_ref[...] = x_ref[...] + _dot(a_ref[...], w_ref[...])


def _matmul_residual(a, w, x):
    m, k = a.shape
    d = w.shape[1]
    return pl.pallas_call(
        _matmul_residual_kernel,
        out_shape=jax.ShapeDtypeStruct((m, d), F32),
        grid=(m // ROW_TILE,),
        in_specs=[pl.BlockSpec((ROW_TILE, k), lambda i: (i, 0)),
                  pl.BlockSpec((k, d), lambda i: (0, 0)),
                  pl.BlockSpec((ROW_TILE, d), lambda i: (i, 0))],
        out_specs=pl.BlockSpec((ROW_TILE, d), lambda i: (i, 0)),
        compiler_params=_params("parallel"),
        name="matmul_residual",
    )(a, w, x)


def _ffn_kernel(x_ref, g_ref, wg_ref, wu_ref, wd_ref, o_ref):
    x = x_ref[...]
    h = _rms(x, g_ref[...]).astype(BF16)
    hidden = wg_ref.shape[1]
    acc = x
    for lo in range(0, hidden, FFN_TILE):
        hi = min(lo + FFN_TILE, hidden)
        a = _dot(h, wg_ref[:, lo:hi])
        b = _dot(h, wu_ref[:, lo:hi])
        acc = acc + _dot((a * jax.nn.sigmoid(a) * b).astype(BF16), wd_ref[lo:hi, :])
    o_ref[...] = acc


def _ffn(x, gain, w_gate, w_up, w_down):
    m, d = x.shape
    hidden = w_gate.shape[1]
    return pl.pallas_call(
        _ffn_kernel,
        out_shape=jax.ShapeDtypeStruct((m, d), F32),
        grid=(m // ROW_TILE,),
        in_specs=[pl.BlockSpec((ROW_TILE, d), lambda i: (i, 0)),
                  _resident((1, d)),
                  _resident((d, hidden)), _resident((d, hidden)), _resident((hidden, d))],
        out_specs=pl.BlockSpec((ROW_TILE, d), lambda i: (i, 0)),
        compiler_params=_params("parallel"),
        name="ffn",
    )(x, gain.reshape(1, d), w_gate, w_up, w_down)


def _a_proj_kernel(x_ref, g_ref, w_ref, qg_ref, kg_ref, o_ref, h_ref, x_scr, *, dil):
    rows = ROW_TILE // dil
    xn = _rms(x_ref[...], g_ref[...])
    if dil == 1:
        h_ref[...] = xn.astype(BF16)
    else:
        slabs = xn.shape[1] // 128
        for c in range(slabs):
            x_scr[c] = xn[:, c * 128:(c + 1) * 128]
        for r in range(dil):
            picked = [x_scr[c, pl.ds(r, rows, stride=dil), :] for c in range(slabs)]
            h_ref[r * rows:(r + 1) * rows, :] = jnp.concatenate(picked, axis=1).astype(BF16)
    h = h_ref[...]
    width = w_ref.shape[1]
    hd = width // 3
    low = lax.broadcasted_iota(jnp.int32, (ROW_TILE, 128), 1) < A_HEAD_DIM
    for j in range(width // A_PROJ_TILE):
        res = _dot(h, w_ref[:, j * A_PROJ_TILE:(j + 1) * A_PROJ_TILE])
        kind = (j * A_PROJ_TILE) // hd
        if kind < 2:
            parts = []
            for c in range(A_PROJ_TILE // 128):
                y = res[:, c * 128:(c + 1) * 128]
                sq = y * y
                tot = jnp.sum(sq, axis=-1, keepdims=True)
                lo = jnp.sum(jnp.where(low, sq, 0.0), axis=-1, keepdims=True)
                ss = jnp.where(low, lo, tot - lo)
                parts.append(y * lax.rsqrt(ss * (1.0 / A_HEAD_DIM) + RMS_EPS))
            res = jnp.concatenate(parts, axis=1) * (qg_ref if kind == 0 else kg_ref)[...]
        res = res.astype(BF16)
        for r in range(dil):
            off = r * width + j * A_PROJ_TILE
            o_ref[:, off:off + A_PROJ_TILE] = res[r * rows:(r + 1) * rows]


def _a_proj(x, gain, w, q_gain, k_gain, dil):
    m, d = x.shape
    width = w.shape[1]
    reps = A_PROJ_TILE // A_HEAD_DIM
    qg = jnp.tile(q_gain * (A_HEAD_DIM ** -0.5), reps).reshape(1, A_PROJ_TILE)
    kg = jnp.tile(k_gain, reps).reshape(1, A_PROJ_TILE)
    return pl.pallas_call(
        functools.partial(_a_proj_kernel, dil=dil),
        out_shape=jax.ShapeDtypeStruct((m // dil, dil * width), BF16),
        grid=(m // ROW_TILE,),
        in_specs=[pl.BlockSpec((ROW_TILE, d), lambda i: (i, 0)),
                  pl.BlockSpec((1, d), lambda i: (0, 0)),
                  pl.BlockSpec((d, width), lambda i: (0, 0)),
                  pl.BlockSpec((1, A_PROJ_TILE), lambda i: (0, 0)),
                  pl.BlockSpec((1, A_PROJ_TILE), lambda i: (0, 0))],
        out_specs=pl.BlockSpec((ROW_TILE // dil, dil * width), lambda i: (i, 0)),
        scratch_shapes=[pltpu.VMEM((ROW_TILE, d), BF16), pltpu.VMEM((d // 128, ROW_TILE, 128), F32)],
        compiler_params=_params("parallel"),
        name="a_proj",
    )(x, gain.reshape(1, d), w, qg, kg)


def _a_attn_kernel(q_ref, kp_ref, kc_ref, vp_ref, vc_ref, bias_ref, o_ref, lse_ref):
    first = (pl.program_id(2) == 0).astype(jnp.int32)
    lane = lax.broadcasted_iota(jnp.int32, (A_Q_BLOCK, 128), 1)
    low = lane < A_HEAD_DIM
    ones = jnp.ones((A_Q_BLOCK, 128), BF16)
    lse_tile = jnp.zeros((A_Q_BLOCK, 128), F32)
    nq = A_Q_BLOCK
    for pair in range(N_HEADS // 2):
        sl = slice(pair * 128, (pair + 1) * 128)
        q = q_ref[0, :, sl]
        zero = jnp.zeros_like(q)
        qq = jnp.concatenate([jnp.where(low, q, zero), jnp.where(low, zero, q)], axis=0)
        kk = jnp.concatenate([kp_ref[0, :, sl], kc_ref[0, :, sl]], axis=0)
        vv = jnp.concatenate([jnp.concatenate([vp_ref[0, :, sl], ones], axis=1),
                              jnp.concatenate([vc_ref[0, :, sl], ones], axis=1)], axis=0)
        base = 2 * pair + N_HEADS * first
        s = _dot_nt(qq, kk) + jnp.concatenate([bias_ref[base], bias_ref[base + 1]], axis=0)
        m = jnp.max(s, axis=-1, keepdims=True)
        acc = _dot(jnp.exp(s - m).astype(BF16), vv)
        z = acc[:, 128:]
        o = acc[:, :128] * (1.0 / z)
        lse = m + jnp.log(z)
        o_ref[0, :, sl] = jnp.where(low, o[:nq], o[nq:]).astype(o_ref.dtype)
        lse_tile = jnp.where(lane == 2 * pair, lse[:nq], jnp.where(lane == 2 * pair + 1, lse[nq:], lse_tile))
    lse_ref[0] = lse_tile


def _a_attention(proj, bias, dil, batch, seq):
    length = seq // dil
    nblk = length // A_Q_BLOCK
    hd = N_HEADS * A_HEAD_DIM
    pv = proj.reshape(batch, length, dil * 3 * hd)

    def spec(off, prev):
        if prev:
            return pl.BlockSpec((1, A_Q_BLOCK, hd), lambda b, r, i: (b, jnp.maximum(i - 1, 0), r * 3 + off))
        return pl.BlockSpec((1, A_Q_BLOCK, hd), lambda b, r, i: (b, i, r * 3 + off))

    o, lse = pl.pallas_call(
        _a_attn_kernel,
        out_shape=(jax.ShapeDtypeStruct((batch, length, dil * hd), BF16),
                   jax.ShapeDtypeStruct((batch, length, dil * 128), F32)),
        grid=(batch, dil, nblk),
        in_specs=[spec(0, False), spec(1, True), spec(1, False), spec(2, True), spec(2, False),
                  pl.BlockSpec((2 * N_HEADS, A_Q_BLOCK, 2 * A_Q_BLOCK), lambda b, r, i: (0, 0, 0))],
        out_specs=(pl.BlockSpec((1, A_Q_BLOCK, hd), lambda b, r, i: (b, i, r)),
                   pl.BlockSpec((1, A_Q_BLOCK, 128), lambda b, r, i: (b, i, r))),
        compiler_params=_params("parallel", "parallel", "arbitrary"),
        name="a_attention",
    )(pv, pv, pv, pv, pv, bias)
    return o.reshape(batch * length, dil * hd), lse.reshape(batch * length, dil * 128)


def _a_out_kernel(o0_ref, o1_ref, o2_ref, l0_ref, l1_ref, l2_ref, e_ref, w_ref, x_ref, out_ref, o_scr, l_scr):
    hd = N_HEADS * A_HEAD_DIM
    for g, (o_ref, l_ref) in enumerate(((o0_ref, l0_ref), (o1_ref, l1_ref), (o2_ref, l2_ref))):
        dil = A_GROUPS[g][1]
        rows = ROW_TILE // dil
        for r in range(dil):
            dst = pl.ds(r, rows, stride=dil) if dil > 1 else slice(None)
            l_scr[g, dst, :] = l_ref[:, r * 128:(r + 1) * 128]
            for c in range(hd // 128):
                o_scr[g, c, dst, :] = o_ref[:, r * hd + c * 128:r * hd + (c + 1) * 128].astype(F32)
    ls = [l_scr[g] for g in range(len(A_GROUPS))]
    m = jnp.maximum(jnp.maximum(ls[0], ls[1]), ls[2])
    es = [jnp.exp(l - m) for l in ls]
    inv = 1.0 / (es[0] + es[1] + es[2])
    expand = e_ref[...]
    acc = None
    for g, e in enumerate(es):
        wgt = e * inv
        hi = wgt.astype(BF16)
        lo = (wgt - hi.astype(F32)).astype(BF16)
        o_g = jnp.concatenate([o_scr[g, c] for c in range(hd // 128)], axis=1)
        term = (_dot(hi, expand) + _dot(lo, expand)) * o_g
        acc = term if acc is None else acc + term
    out_ref[...] = x_ref[...] + _dot(acc.astype(BF16), w_ref[...])


def _a_out(outs, lses, w_out, x):
    m, d = x.shape
    hd = N_HEADS * A_HEAD_DIM
    expand = np.zeros((128, hd), np.float32)
    for h in range(N_HEADS):
        expand[h, h * A_HEAD_DIM:(h + 1) * A_HEAD_DIM] = 1.0
    grouped = lambda width: [pl.BlockSpec((ROW_TILE // dil, dil * width), lambda i: (i, 0)) for _, dil in A_GROUPS]
    return pl.pallas_call(
        _a_out_kernel,
        out_shape=jax.ShapeDtypeStruct((m, d), F32),
        grid=(m // ROW_TILE,),
        in_specs=grouped(hd) + grouped(128) + [
            pl.BlockSpec((128, hd), lambda i: (0, 0)),
            pl.BlockSpec((hd, d), lambda i: (0, 0)),
            pl.BlockSpec((ROW_TILE, d), lambda i: (i, 0))],
        out_specs=pl.BlockSpec((ROW_TILE, d), lambda i: (i, 0)),
        scratch_shapes=[pltpu.VMEM((len(A_GROUPS), hd // 128, ROW_TILE, 128), F32),
                        pltpu.VMEM((len(A_GROUPS), ROW_TILE, 128), F32)],
        compiler_params=_params("parallel"),
        name="a_out",
    )(*outs, *lses, jnp.asarray(expand, BF16), w_out, x)


def _mixer_a(x, rel_bias, norm1, w_in, q_gain, k_gain, w_out, batch, seq):
    w_in = w_in.astype(BF16)
    group_width = 3 * N_HEADS * A_HEAD_DIM
    outs, lses = [], []
    for gi, (window, dil) in enumerate(A_GROUPS):
        steps = window // dil
        assert steps == A_Q_BLOCK and (seq // dil) % A_Q_BLOCK == 0 and seq % ROW_TILE == 0
        bias = _bias_tiles(rel_bias, 1, A_Q_BLOCK, 2 * A_Q_BLOCK, base=A_Q_BLOCK, tile_step=0, row_step=1,
                           col_step=-1, dmax=steps, dil=dil)[:, 0]
        bias = jnp.concatenate([bias, bias.at[:, :, :A_Q_BLOCK].set(NEG_INF)], axis=0)
        proj = _a_proj(x, norm1, w_in[:, gi * group_width:(gi + 1) * group_width], q_gain[gi], k_gain[gi], dil)
        o, lse = _a_attention(proj, bias, dil, batch, seq)
        outs.append(o)
        lses.append(lse)
    return _a_out(outs, lses, w_out.astype(BF16), x)


def _b_prep_kernel(p_ref, qg_ref, kg_ref, q_ref, ck_ref, cv_ref, sk_ref, sv_ref, wk_ref, wv_ref, gate_ref):
    dh = B_HEAD_DIM
    qg = qg_ref[...] * (dh ** -0.5)
    for h in range(N_HEADS):
        q_ref[0, h] = _rms(p_ref[0, :, h * dh:(h + 1) * dh].astype(F32), qg).astype(BF16)
    base = N_HEADS * dh
    outs = ((ck_ref, None), (cv_ref, None), (sk_ref, 1), (sv_ref, None), (wk_ref, 2), (wv_ref, None))
    for idx, (ref, gain_row) in enumerate(outs):
        for n in range(B_KV_HEADS):
            off = base + (idx * B_KV_HEADS + n) * dh
            t = p_ref[0, :, off:off + dh]
            if gain_row is not None:
                t = _rms(t.astype(F32), kg_ref[gain_row:gain_row + 1, :]).astype(BF16)
            ref[0, n] = t
    gate_off = base + 6 * B_KV_HEADS * dh
    gate = jax.nn.sigmoid(p_ref[0, :, gate_off:gate_off + 3 * N_HEADS].astype(F32))
    per = 3 * B_GROUP
    for n in range(B_KV_HEADS):
        gate_ref[0, n] = gate[:, n * per:(n + 1) * per]


def _b_prep(proj, q_gain, k_gain, batch, seq):
    ts = 256
    dh = B_HEAD_DIM
    kv_shape = jax.ShapeDtypeStruct((batch, B_KV_HEADS, seq, dh), BF16)
    kv_spec = pl.BlockSpec((1, B_KV_HEADS, ts, dh), lambda b, i: (b, 0, i, 0))
    return pl.pallas_call(
        _b_prep_kernel,
        out_shape=(jax.ShapeDtypeStruct((batch, N_HEADS, seq, dh), BF16),) + (kv_shape,) * 6
        + (jax.ShapeDtypeStruct((batch, B_KV_HEADS, seq, 3 * B_GROUP), F32),),
        grid=(batch, seq // ts),
        in_specs=[pl.BlockSpec((1, ts, B_PROJ_WIDTH), lambda b, i: (b, i, 0)),
                  pl.BlockSpec((1, dh), lambda b, i: (0, 0)),
                  pl.BlockSpec((3, dh), lambda b, i: (0, 0))],
        out_specs=(pl.BlockSpec((1, N_HEADS, ts, dh), lambda b, i: (b, 0, i, 0)),) + (kv_spec,) * 6
        + (pl.BlockSpec((1, B_KV_HEADS, ts, 3 * B_GROUP), lambda b, i: (b, 0, i, 0)),),
        compiler_params=_params("parallel", "parallel"),
        name="b_prep",
    )(proj, q_gain.reshape(1, dh), k_gain)


def _b_compress_kernel(tk_ref, tv_ref, pos_ref, w1_ref, w2_ref, kg_ref, kc_ref, vc_ref):
    half = (B_CMP_LEN // 2) * B_HEAD_DIM
    for kv, (t_ref, out_ref) in enumerate(((tk_ref, kc_ref), (tv_ref, vc_ref))):
        t = t_ref[0, 0].astype(F32)
        top = (t + pos_ref[kv, 0:1, :]).astype(BF16)
        bot = (t + pos_ref[kv, 1:2, :]).astype(BF16)
        a1 = _dot(top, w1_ref[kv, :half, :])
        a2 = _dot(bot, w1_ref[kv, half:, :])
        hidden = a1 + pltpu.roll(a2, a2.shape[0] - 1, 0)
        out = _dot(jax.nn.gelu(hidden).astype(BF16), w2_ref[kv])
        if kv == 0:
            out = _rms(out, kg_ref[...])
        out_ref[0, 0] = out.astype(out_ref.dtype)


def _b_compress(ck, cv, cmp_pos, cmp_w1, cmp_w2, k_gain0, batch, seq):
    rows = seq // B_CMP_STRIDE
    half = (B_CMP_LEN // 2) * B_HEAD_DIM
    tk = ck.reshape(batch, B_KV_HEADS, rows, half)
    tv = cv.reshape(batch, B_KV_HEADS, rows, half)
    pos = cmp_pos.reshape(2, 2, half)
    t_spec = pl.BlockSpec((1, 1, rows, half), lambda b, n: (b, n, 0, 0))
    o_spec = pl.BlockSpec((1, 1, rows, B_HEAD_DIM), lambda b, n: (b, n, 0, 0))
    shape = jax.ShapeDtypeStruct((batch, B_KV_HEADS, rows, B_HEAD_DIM), BF16)
    return pl.pallas_call(
        _b_compress_kernel,
        out_shape=(shape, shape),
        grid=(batch, B_KV_HEADS),
        in_specs=[t_spec, t_spec,
                  pl.BlockSpec((2, 2, half), lambda b, n: (0, 0, 0)),
                  pl.BlockSpec((2, 2 * half, B_CMP_HIDDEN), lambda b, n: (0, 0, 0)),
                  pl.BlockSpec((2, B_CMP_HIDDEN, B_HEAD_DIM), lambda b, n: (0, 0, 0)),
                  pl.BlockSpec((1, B_HEAD_DIM), lambda b, n: (0, 0))],
        out_specs=(o_spec, o_spec),
        compiler_params=_params("parallel", "parallel"),
        name="b_compress",
    )(tk, tv, pos, cmp_w1.astype(BF16), cmp_w2.astype(BF16), k_gain0.reshape(1, -1))


def _b_cmp_attn_kernel(qt_ref, kc_ref, vct_ref, bias_ref, c2s_ref, oc_ref, sel_ref, imp_ref, *, top_n):
    tq = B_SWEEP
    n_sel = imp_ref.shape[0]
    qt = jnp.concatenate([qt_ref[0, g] for g in range(B_GROUP)], axis=1)
    bias = jnp.concatenate([bias_ref[g, 0] for g in range(B_GROUP)], axis=1)
    s = _dot(kc_ref[0, 0], qt) + bias
    valid = bias > 0.5 * NEG_INF
    m = jnp.max(s, axis=0, keepdims=True)
    e = jnp.where(valid, jnp.exp(s - m), 0.0)
    z = jnp.maximum(jnp.sum(e, axis=0, keepdims=True), TINY)
    p = e * (1.0 / z)
    oct = _dot(vct_ref[0, 0], p.astype(BF16))
    for g in range(B_GROUP):
        oc_ref[0, g] = oct[:, g * tq:(g + 1) * tq]

    p_sum = p[:, 0:tq] + p[:, tq:2 * tq] + p[:, 2 * tq:3 * tq] + p[:, 3 * tq:4 * tq]
    hi = p_sum.astype(BF16)
    lo = (p_sum - hi.astype(F32)).astype(BF16)
    c2s = c2s_ref[...]
    imp = _dot(c2s, hi) + _dot(c2s, lo)

    t = pl.program_id(2) * tq + lax.broadcasted_iota(jnp.int32, (n_sel, tq), 1)
    blk = lax.broadcasted_iota(jnp.int32, (n_sel, tq), 0)
    cur = t // B_SEL_BLOCK
    forced = (blk == 0) | (blk == cur) | (blk == cur - 1)
    imp = jnp.where(forced, FORCE_SCORE, jnp.where(blk * B_SEL_BLOCK <= t, imp, NEG_INF))
    imp_ref[...] = imp

    def count(i, rank):
        row = imp_ref[pl.ds(i, 1), :]
        ahead = jnp.where(row > imp, 1.0, jnp.where(row == imp, jnp.where(blk > i, 1.0, 0.0), 0.0))
        return rank + ahead

    rank = lax.fori_loop(0, n_sel, count, jnp.zeros((n_sel, tq), F32))
    sel_ref[0, 0] = jnp.where(rank < top_n, 0.0, NEG_INF).astype(sel_ref.dtype)


def _b_cmp_attn(qt, kc, vc, bias_c, batch, seq):
    n_sel = seq // B_SEL_BLOCK
    n_cmp_pad = seq // B_CMP_STRIDE
    n_cmp = (seq - B_CMP_LEN) // B_CMP_STRIDE + 1
    c = np.arange(n_cmp_pad)[None, :] * B_CMP_STRIDE
    j = np.arange(n_sel)[:, None] * B_SEL_BLOCK
    c2s = ((c < j + B_SEL_BLOCK) & (c + B_CMP_LEN > j) & (np.arange(n_cmp_pad)[None, :] < n_cmp)).astype(np.float32)
    kern = functools.partial(_b_cmp_attn_kernel, top_n=min(B_TOP_N, n_sel))
    return pl.pallas_call(
        kern,
        out_shape=(jax.ShapeDtypeStruct((batch, N_HEADS, B_HEAD_DIM, seq), F32),
                   jax.ShapeDtypeStruct((batch, B_KV_HEADS, n_sel, seq), BF16)),
        grid=(batch, B_KV_HEADS, seq // B_SWEEP),
        in_specs=[pl.BlockSpec((1, B_GROUP, B_HEAD_DIM, B_SWEEP), lambda b, n, i: (b, n, 0, i)),
                  pl.BlockSpec((1, 1, n_cmp_pad, B_HEAD_DIM), lambda b, n, i: (b, n, 0, 0)),
                  pl.BlockSpec((1, 1, B_HEAD_DIM, n_cmp_pad), lambda b, n, i: (b, n, 0, 0)),
                  pl.BlockSpec((B_GROUP, 1, n_cmp_pad, B_SWEEP), lambda b, n, i: (n, i, 0, 0)),
                  pl.BlockSpec((n_sel, n_cmp_pad), lambda b, n, i: (0, 0))],
        out_specs=(pl.BlockSpec((1, B_GROUP, B_HEAD_DIM, B_SWEEP), lambda b, n, i: (b, n, 0, i)),
                   pl.BlockSpec((1, 1, n_sel, B_SWEEP), lambda b, n, i: (b, n, 0, i))),
        scratch_shapes=[pltpu.VMEM((n_sel, B_SWEEP), F32)],
        compiler_params=_params("parallel", "parallel", "arbitrary"),
        name="b_cmp_attn",
    )(qt, kc, vc.transpose(0, 1, 3, 2), bias_c, jnp.asarray(c2s, BF16))


def _b_sparse_kernel(qt_ref, ka_ref, vs_ref, wk_ref, vw_ref, sel_ref, bs_ref, bw_ref, oc_ref, gate_ref,
                     o_ref, acc_ref, sa_ref, sb_ref, *, delta_max, win_tiles):
    tq = B_SWEEP
    dh = B_HEAD_DIM
    cols = B_GROUP * tq
    qi = pl.program_id(2)
    n_tiles = ka_ref.shape[2] // tq
    qt = jnp.concatenate([qt_ref[0, g] for g in range(B_GROUP)], axis=1)
    q_aug = jnp.concatenate([qt, jnp.concatenate([sel_ref[0, 0]] * B_GROUP, axis=1)], axis=0)

    def tile_start(kt):
        return pl.multiple_of(jnp.clip(kt, 0, n_tiles - 1) * tq, tq)

    def normalised(acc):
        return acc[:dh] * (1.0 / acc[dh:dh + 1])

    def sel_bias(kt):
        d = jnp.clip(qi - kt, -1, delta_max) + 1
        return jnp.concatenate([bs_ref[g, d] for g in range(B_GROUP)], axis=1)

    def sel_scores(kt):
        return _dot(ka_ref[0, 0, pl.ds(tile_start(kt), tq), :], q_aug) + sel_bias(kt)

    def consume(s_buf, kt, m_old):
        s = s_buf[...]
        m_new = jnp.maximum(m_old, jnp.max(s, axis=0, keepdims=True))
        alpha = jnp.exp(m_old - m_new)
        p = jnp.exp(s - m_new).astype(BF16)
        acc_ref[...] = alpha * acc_ref[...] + _dot(vs_ref[0, 0, :, pl.ds(tile_start(kt), tq)], p)
        return m_new

    acc_ref[...] = jnp.zeros(acc_ref.shape, F32)
    sa_ref[...] = sel_scores(0)

    def pair(j, m):
        kt = 2 * j
        sb_ref[...] = sel_scores(kt + 1)
        m = consume(sa_ref, kt, m)
        sa_ref[...] = sel_scores(kt + 2)
        return consume(sb_ref, kt + 1, m)

    lax.fori_loop(0, (qi + 2) // 2, pair, jnp.full((1, cols), NEG_INF, F32))
    o_s = normalised(acc_ref[...])

    tiles = []
    for u in range(win_tiles):
        kt = qi - (win_tiles - 1) + u
        d = jnp.where(kt >= 0, qi - kt, -1) + 1
        bias = jnp.concatenate([bw_ref[g, d] for g in range(B_GROUP)], axis=1)
        tiles.append((_dot(wk_ref[0, 0, pl.ds(tile_start(kt), tq), :], qt) + bias, kt))
    m = None
    for s, _ in tiles:
        tile_max = jnp.max(s, axis=0, keepdims=True)
        m = tile_max if m is None else jnp.maximum(m, tile_max)
    acc = None
    for s, kt in tiles:
        pv = _dot(vw_ref[0, 0, :, pl.ds(tile_start(kt), tq)], jnp.exp(s - m).astype(BF16))
        acc = pv if acc is None else acc + pv
    o_w = normalised(acc)

    gate = gate_ref[0, 0]
    merged = []
    for g in range(B_GROUP):
        cs = slice(g * tq, (g + 1) * tq)
        merged.append(gate[3 * g:3 * g + 1] * oc_ref[0, g] + gate[3 * g + 1:3 * g + 2] * o_s[:, cs]
                      + gate[3 * g + 2:3 * g + 3] * o_w[:, cs])
    for pair in range(B_GROUP // 2):
        both = jnp.concatenate([merged[2 * pair], merged[2 * pair + 1]], axis=0)
        o_ref[0, :, pair * 2 * dh:(pair + 1) * 2 * dh] = both.T.astype(o_ref.dtype)


def _b_sparse(qt, ka, vs, wk, vw, sel, bias_s, bias_w, oc, gate, batch, seq):
    n_sel = seq // B_SEL_BLOCK
    dh = B_HEAD_DIM
    n_ds = bias_s.shape[1]
    n_dw = bias_w.shape[1]
    vrows = vs.shape[2]
    kern = functools.partial(_b_sparse_kernel, delta_max=n_ds - 2, win_tiles=n_dw - 1)
    whole = lambda rows, width: pl.BlockSpec((1, 1, rows, width), lambda b, n, i: (b, n, 0, 0))
    return pl.pallas_call(
        kern,
        out_shape=jax.ShapeDtypeStruct((batch, seq, N_HEADS * dh), BF16),
        grid=(batch, B_KV_HEADS, seq // B_SWEEP),
        in_specs=[pl.BlockSpec((1, B_GROUP, dh, B_SWEEP), lambda b, n, i: (b, n, 0, i)),
                  whole(seq, dh + n_sel), whole(vrows, seq), whole(seq, dh), whole(vrows, seq),
                  pl.BlockSpec((1, 1, n_sel, B_SWEEP), lambda b, n, i: (b, n, 0, i)),
                  pl.BlockSpec((B_GROUP, n_ds, B_SWEEP, B_SWEEP), lambda b, n, i: (n, 0, 0, 0)),
                  pl.BlockSpec((B_GROUP, n_dw, B_SWEEP, B_SWEEP), lambda b, n, i: (n, 0, 0, 0)),
                  pl.BlockSpec((1, B_GROUP, dh, B_SWEEP), lambda b, n, i: (b, n, 0, i)),
                  pl.BlockSpec((1, 1, 4 * B_GROUP, B_SWEEP), lambda b, n, i: (b, n, 0, i))],
        out_specs=pl.BlockSpec((1, B_SWEEP, B_GROUP * dh), lambda b, n, i: (b, i, n)),
        scratch_shapes=[pltpu.VMEM((vrows, B_GROUP * B_SWEEP), F32),
                        pltpu.VMEM((B_SWEEP, B_GROUP * B_SWEEP), F32),
                        pltpu.VMEM((B_SWEEP, B_GROUP * B_SWEEP), F32)],
        compiler_params=_params("parallel", "parallel", "arbitrary"),
        name="b_sparse",
    )(qt, ka, vs, wk, vw, sel, bias_s, bias_w, oc, gate)


def _mixer_b(x, rel_bias, norm1, w_in, q_gain, k_gain, cmp_pos, cmp_w1, cmp_w2, w_out, batch, seq):
    w_pad = jnp.pad(w_in, ((0, 0), (0, B_PROJ_WIDTH - w_in.shape[1]))).astype(BF16)
    proj = _norm_matmul(x, norm1, w_pad, BF16, 512).reshape(batch, seq, B_PROJ_WIDTH)
    q, ck, cv, sk, sv, wk, wv, gate = _b_prep(proj, q_gain, k_gain, batch, seq)
    kc, vc = _b_compress(ck, cv, cmp_pos, cmp_w1, cmp_w2, k_gain[0], batch, seq)
    bias_c = _bias_tiles(rel_bias, seq // B_SWEEP, seq // B_CMP_STRIDE, B_SWEEP, base=1 - B_CMP_LEN,
                         tile_step=B_SWEEP, row_step=-B_CMP_STRIDE, col_step=1, dmax=1 << 30)
    qt = q.transpose(0, 1, 3, 2)
    oc, sel = _b_cmp_attn(qt, kc, vc, bias_c, batch, seq)
    delta_max = min(seq // B_SWEEP - 1, -(-(_THRESHOLDS[-1] + B_SWEEP - 1) // B_SWEEP))
    bias_s = _bias_tiles(rel_bias, delta_max + 2, B_SWEEP, B_SWEEP, base=-B_SWEEP, tile_step=B_SWEEP,
                         row_step=-1, col_step=1, dmax=1 << 30)
    win_tiles = (B_WINDOW - 1 + B_SWEEP - 1) // B_SWEEP + 1
    bias_w = _bias_tiles(rel_bias, win_tiles + 1, B_SWEEP, B_SWEEP, base=-B_SWEEP, tile_step=B_SWEEP,
                         row_step=-1, col_step=1, dmax=B_WINDOW - 1)
    n_sel = seq // B_SEL_BLOCK
    onehot = (np.arange(seq)[:, None] // B_SEL_BLOCK == np.arange(n_sel)[None, :]).astype(np.float32)
    ka = jnp.concatenate([sk, jnp.broadcast_to(jnp.asarray(onehot, BF16), sk.shape[:2] + onehot.shape)], axis=-1)
    ones = jnp.ones(sv.shape[:2] + (16, seq), BF16)
    vs = jnp.concatenate([sv.transpose(0, 1, 3, 2), ones], axis=2)
    vw = jnp.concatenate([wv.transpose(0, 1, 3, 2), ones], axis=2)
    gate_t = jnp.pad(gate.transpose(0, 1, 3, 2), ((0, 0), (0, 0), (0, B_GROUP), (0, 0)))
    o = _b_sparse(qt, ka, vs, wk, vw, sel, bias_s, bias_w, oc, gate_t, batch, seq)
    return _matmul_residual(o.reshape(batch * seq, -1), w_out.astype(BF16), x)


def _c_conv_kernel(cur_ref, halo_ref, w_ref, sm_ref, alog_ref, dtb_ref, qkv_ref, bg_ref):
    ts = cur_ref.shape[1]
    keep = jnp.where(pl.program_id(1) == 0, 0.0, 1.0)
    dk = C_HEAD_DIM
    for c in range(3 * C_HEADS):
        sl = slice(c * dk, (c + 1) * dk)
        xe = jnp.concatenate([halo_ref[0, :, sl].astype(F32) * keep, cur_ref[0, :, sl].astype(F32)], axis=0)
        y = None
        for j in range(C_CONV):
            off = 8 - (C_CONV - 1) + j
            term = w_ref[j:j + 1, sl] * xe[off:off + ts]
            y = term if y is None else y + term
        y = y * jax.nn.sigmoid(y)
        if c < 2 * C_HEADS:
            y = y * lax.rsqrt(jnp.sum(y * y, axis=-1, keepdims=True) + RMS_EPS)
        if c < C_HEADS:
            y = y * (dk ** -0.5)
        qkv_ref[0, :, sl] = y
    sm = sm_ref[0]
    a = sm + dtb_ref[...]
    softplus = jnp.maximum(a, 0.0) + jnp.log1p(jnp.exp(-jnp.abs(a)))
    g = -jnp.exp(alog_ref[...]) * softplus
    lane = lax.broadcasted_iota(jnp.int32, sm.shape, 1)
    bg_ref[0] = jnp.where(lane < C_HEADS, jax.nn.sigmoid(sm), g)


def _c_conv(proj, small, conv_w, a_log, dt_bias, batch, seq):
    ts = 256
    width = 3 * C_WIDTH
    pad = lambda v: jnp.pad(v, (C_HEADS, 128 - 2 * C_HEADS)).reshape(1, 128)
    return pl.pallas_call(
        _c_conv_kernel,
        out_shape=(jax.ShapeDtypeStruct((batch, seq, width), F32),
                   jax.ShapeDtypeStruct((batch, seq, 128), F32)),
        grid=(batch, seq // ts),
        in_specs=[pl.BlockSpec((1, ts, width), lambda b, i: (b, i, 0)),
                  pl.BlockSpec((1, 8, width), lambda b, i: (b, jnp.maximum(i * (ts // 8) - 1, 0), 0)),
                  pl.BlockSpec((C_CONV, width), lambda b, i: (0, 0)),
                  pl.BlockSpec((1, ts, 128), lambda b, i: (b, i, 0)),
                  pl.BlockSpec((1, 128), lambda b, i: (0, 0)),
                  pl.BlockSpec((1, 128), lambda b, i: (0, 0))],
        out_specs=(pl.BlockSpec((1, ts, width), lambda b, i: (b, i, 0)),
                   pl.BlockSpec((1, ts, 128), lambda b, i: (b, i, 0))),
        compiler_params=_params("parallel", "arbitrary"),
        name="c_conv",
    )(proj, proj, conv_w, small, pad(a_log), pad(dt_bias))


def _sum3(x, fn):
    hi = x.astype(BF16)
    r = x - hi.astype(F32)
    mid = r.astype(BF16)
    lo = (r - mid.astype(F32)).astype(BF16)
    return fn(hi) + (fn(mid) + fn(lo))


def _c_chunk_kernel(qkv_ref, bg_ref, bgt_ref, tri_ref, trit_ref, blk_ref, u_ref, w_ref, qg_ref, kg_ref, attn_ref,
                    gc_ref):
    cs = C_CHUNK
    dk = C_HEAD_DIM
    gs = C_GROUP * cs
    row = lax.broadcasted_iota(jnp.int32, (gs, gs), 0)
    col = lax.broadcasted_iota(jnp.int32, (gs, gs), 1)
    same = (row // cs) == (col // cs)
    causal = same & (row >= col)
    strict = same & (row > col)
    eye = jnp.where(row == col, 1.0, 0.0)

    bgc = bg_ref[0]
    tri = tri_ref[...]
    gcum_col = _sum3(bgc, lambda p: _dot(tri, p))
    glast_col = _sum3(bgc, lambda p: _dot(blk_ref[...], p))
    gcum_row = _sum3(bgt_ref[0], lambda p: _dot(p, trit_ref[...]))
    gc_ref[0] = gcum_col
    for h in range(C_HEADS):
        gc = gcum_col[:, C_HEADS + h:C_HEADS + h + 1]
        gr = gcum_row[C_HEADS + h:C_HEADS + h + 1, :]
        beta = bgc[:, h:h + 1]
        q = qkv_ref[0, :, h * dk:(h + 1) * dk]
        k = qkv_ref[0, :, C_WIDTH + h * dk:C_WIDTH + (h + 1) * dk]
        v = qkv_ref[0, :, 2 * C_WIDTH + h * dk:2 * C_WIDTH + (h + 1) * dk]
        decay = jnp.exp(jnp.where(causal, gc - gr, NEG_INF))
        kb = k * beta
        k16 = k.astype(BF16)
        low = jnp.where(strict, _dot_nt(kb.astype(BF16), k16) * decay, 0.0)
        t_mat = eye - low
        power = low
        for _ in range(int(math.log2(cs)) - 1):
            p16 = power.astype(BF16)
            power = _dot(p16, p16)
            t_mat = t_mat + _dot(t_mat.astype(BF16), power.astype(BF16))
        t16 = t_mat.astype(BF16)
        egc = jnp.exp(gc)
        u_ref[0, :, h * dk:(h + 1) * dk] = _dot(t16, (v * beta).astype(BF16))
        w_ref[0, :, h * dk:(h + 1) * dk] = _dot(t16, (kb * egc).astype(BF16)).astype(w_ref.dtype)
        attn = jnp.where(causal, _dot_nt(q.astype(BF16), k16), 0.0) * decay
        attn_ref[0, :, h * gs:(h + 1) * gs] = attn.astype(attn_ref.dtype)
        qg_ref[0, :, h * dk:(h + 1) * dk] = (q * egc).astype(qg_ref.dtype)
        glast = glast_col[:, C_HEADS + h:C_HEADS + h + 1]
        kg_ref[0, :, h * dk:(h + 1) * dk] = (k * jnp.exp(glast - gc)).astype(kg_ref.dtype)


def _c_chunks(qkv, bg, bgt, batch, seq):
    gs = C_GROUP * C_CHUNK
    idx = np.arange(gs)
    same = (idx[:, None] // C_CHUNK) == (idx[None, :] // C_CHUNK)
    tri = (same & (idx[:, None] >= idx[None, :])).astype(np.float32)
    wide = lambda width: pl.BlockSpec((1, gs, width), lambda b, i: (b, i, 0))
    shape = lambda width, dtype: jax.ShapeDtypeStruct((batch, seq, width), dtype)
    const = pl.BlockSpec((gs, gs), lambda b, i: (0, 0))
    return pl.pallas_call(
        _c_chunk_kernel,
        out_shape=(shape(C_WIDTH, F32), shape(C_WIDTH, BF16), shape(C_WIDTH, BF16), shape(C_WIDTH, BF16),
                   shape(C_HEADS * gs, BF16), shape(128, F32)),
        grid=(batch, seq // gs),
        in_specs=[wide(3 * C_WIDTH), wide(128),
                  pl.BlockSpec((1, 2 * C_HEADS, gs), lambda b, i: (b, 0, i)),
                  const, const, const],
        out_specs=(wide(C_WIDTH),) * 4 + (wide(C_HEADS * gs), wide(128)),
        compiler_params=_params("parallel", "parallel"),
        name="c_chunks",
    )(qkv, bg, bgt, jnp.asarray(tri, BF16), jnp.asarray(tri.T, BF16), jnp.asarray(same, BF16))


def _c_scan_kernel(u_ref, w_ref, qg_ref, kg_ref, attn_ref, gc_ref, o_ref, state_ref, vnew_ref):
    @pl.when(pl.program_id(1) == 0)
    def _():
        state_ref[...] = jnp.zeros_like(state_ref)

    cs = C_CHUNK
    dk = C_HEAD_DIM
    gs = C_GROUP * cs
    vnew_ref[...] = jnp.zeros_like(vnew_ref)
    for c in range(C_GROUP):
        rs = slice(c * cs, (c + 1) * cs)
        decay_last = jnp.exp(gc_ref[0, (c + 1) * cs - 1:(c + 1) * cs, :])
        for h in range(C_HEADS):
            sl = slice(h * dk, (h + 1) * dk)
            state = state_ref[h]
            s16 = state.astype(BF16)
            v_new = u_ref[0, rs, sl] - _dot(w_ref[0, rs, sl], s16)
            v16 = v_new.astype(BF16)
            vnew_ref[h, rs, :] = v16
            o_ref[0, rs, sl] = _dot(qg_ref[0, rs, sl], s16) + _dot(attn_ref[0, rs, h * gs:(h + 1) * gs], vnew_ref[h])
            state_ref[h] = state * decay_last[:, C_HEADS + h:C_HEADS + h + 1] + _dot_tn(kg_ref[0, rs, sl], v16)


def _c_scan(u, w, qg, kg, attn, gc, batch, seq):
    gs = C_GROUP * C_CHUNK
    wide = lambda width: pl.BlockSpec((1, gs, width), lambda b, c: (b, c, 0))
    return pl.pallas_call(
        _c_scan_kernel,
        out_shape=jax.ShapeDtypeStruct((batch, seq, C_WIDTH), F32),
        grid=(batch, seq // gs),
        in_specs=[wide(C_WIDTH)] * 4 + [wide(C_HEADS * gs), wide(128)],
        out_specs=wide(C_WIDTH),
        scratch_shapes=[pltpu.VMEM((C_HEADS, C_HEAD_DIM, C_HEAD_DIM), F32),
                        pltpu.VMEM((C_HEADS, gs, C_HEAD_DIM), BF16)],
        compiler_params=_params("parallel", "arbitrary"),
        name="c_scan",
    )(u, w, qg, kg, attn, gc)


def _c_out_kernel(o_ref, z_ref, g_ref, w_ref, x_ref, out_ref):
    dk = C_HEAD_DIM
    parts = []
    for h in range(C_HEADS):
        sl = slice(h * dk, (h + 1) * dk)
        z = z_ref[:, sl].astype(F32)
        parts.append((_rms(o_ref[:, sl], g_ref[...]) * (z * jax.nn.sigmoid(z))).astype(BF16))
    out_ref[...] = x_ref[...] + _dot(jnp.concatenate(parts, axis=-1), w_ref[...])


def _c_out(o, proj, out_gain, w_out, x):
    m, d = x.shape
    z_block = (3 * C_WIDTH) // C_WIDTH
    row = lambda width: pl.BlockSpec((ROW_TILE, width), lambda i: (i, 0))
    return pl.pallas_call(
        _c_out_kernel,
        out_shape=jax.ShapeDtypeStruct((m, d), F32),
        grid=(m // ROW_TILE,),
        in_specs=[row(C_WIDTH),
                  pl.BlockSpec((ROW_TILE, C_WIDTH), lambda i: (i, z_block)),
                  pl.BlockSpec((1, C_HEAD_DIM), lambda i: (0, 0)),
                  pl.BlockSpec((C_WIDTH, d), lambda i: (0, 0)),
                  row(d)],
        out_specs=row(d),
        compiler_params=_params("parallel"),
        name="c_out",
    )(o, proj, out_gain.reshape(1, -1), w_out, x)


def _mixer_c(x, norm1, w_in, conv_w, a_log, dt_bias, out_gain, w_out, batch, seq):
    main = 4 * C_WIDTH
    proj = _norm_matmul(x, norm1, w_in[:, :main].astype(BF16), BF16, 512)
    w_small = jnp.pad(w_in[:, main:], ((0, 0), (0, 128 - 2 * C_HEADS))).astype(BF16)
    small = _norm_matmul(x, norm1, w_small, F32, 128)
    qkv, bg = _c_conv(proj.reshape(batch, seq, main), small.reshape(batch, seq, 128), conv_w, a_log, dt_bias,
                      batch, seq)
    bgt = bg[:, :, :2 * C_HEADS].transpose(0, 2, 1)
    u, w, qg, kg, attn, gc = _c_chunks(qkv, bg, bgt, batch, seq)
    o = _c_scan(u, w, qg, kg, attn, gc, batch, seq)
    return _c_out(o.reshape(batch * seq, C_WIDTH), proj, out_gain, w_out.astype(BF16), x)


def kernel(x, rel_bias, l0_norm1, l0_a_w_in, l0_a_q_gain, l0_a_k_gain, l0_a_w_out, l0_norm2, l0_ffn_w_gate, l0_ffn_w_up, l0_ffn_w_down, l1_norm1, l1_b_w_in, l1_b_q_gain, l1_b_k_gain, l1_b_cmp_pos, l1_b_cmp_w1, l1_b_cmp_w2, l1_b_w_out, l1_norm2, l1_ffn_w_gate, l1_ffn_w_up, l1_ffn_w_down, l2_norm1, l2_c_w_in, l2_c_conv_w, l2_c_a_log, l2_c_dt_bias, l2_c_out_gain, l2_c_w_out, l2_norm2, l2_ffn_w_gate, l2_ffn_w_up, l2_ffn_w_down, l3_norm1, l3_a_w_in, l3_a_q_gain, l3_a_k_gain, l3_a_w_out, l3_norm2, l3_ffn_w_gate, l3_ffn_w_up, l3_ffn_w_down):
    batch, seq, d = x.shape
    h = x.reshape(batch * seq, d)

    def ffn(h, norm2, w_gate, w_up, w_down):
        return _ffn(h, norm2, w_gate.astype(BF16), w_up.astype(BF16), w_down.astype(BF16))

    h = _mixer_a(h, rel_bias, l0_norm1, l0_a_w_in, l0_a_q_gain, l0_a_k_gain, l0_a_w_out, batch, seq)
    h = ffn(h, l0_norm2, l0_ffn_w_gate, l0_ffn_w_up, l0_ffn_w_down)
    h = _mixer_b(h, rel_bias, l1_norm1, l1_b_w_in, l1_b_q_gain, l1_b_k_gain, l1_b_cmp_pos, l1_b_cmp_w1,
                 l1_b_cmp_w2, l1_b_w_out, batch, seq)
    h = ffn(h, l1_norm2, l1_ffn_w_gate, l1_ffn_w_up, l1_ffn_w_down)
    h = _mixer_c(h, l2_norm1, l2_c_w_in, l2_c_conv_w, l2_c_a_log, l2_c_dt_bias, l2_c_out_gain, l2_c_w_out,
                 batch, seq)
    h = ffn(h, l2_norm2, l2_ffn_w_gate, l2_ffn_w_up, l2_ffn_w_down)
    h = _mixer_a(h, rel_bias, l3_norm1, l3_a_w_in, l3_a_q_gain, l3_a_k_gain, l3_a_w_out, batch, seq)
    h = ffn(h, l3_norm2, l3_ffn_w_gate, l3_ffn_w_up, l3_ffn_w_down)
    return h.reshape(batch, seq, d)
```

```python
import functools
import math

import numpy as np
import jax
import jax.numpy as jnp
from jax import lax
from jax.experimental import pallas as pl
from jax.experimental.pallas import tpu as pltpu

D_MODEL = 1024
RMS_EPS = 1e-6
NEG_INF = -1e30
TINY = 1e-30
FORCE_SCORE = 1e9

N_BUCKETS = 32
REL_MAX_DISTANCE = 2048
N_HEADS = 16

A_GROUPS = ((128, 1), (512, 4), (2048, 16))
A_HEAD_DIM = 64
A_Q_BLOCK = 128
A_PROJ_TILE = 512

B_KV_HEADS = 4
B_GROUP = 4
B_HEAD_DIM = 64
B_CMP_LEN = 32
B_CMP_STRIDE = 16
B_CMP_HIDDEN = 256
B_SEL_BLOCK = 64
B_TOP_N = 16
B_WINDOW = 512
B_TILE = 128
B_SWEEP = 256
B_PROJ_WIDTH = 3072

C_HEADS = 8
C_HEAD_DIM = 128
C_WIDTH = C_HEADS * C_HEAD_DIM
C_CONV = 4
C_CHUNK = 64
C_GROUP = 4

FFN_HIDDEN = 2816
FFN_TILE = 1024

ROW_TILE = 512
VMEM_LIMIT = 48 * 1024 * 1024

LOG2E = math.log2(math.e)
LN2 = math.log(2.0)

F32 = jnp.float32
BF16 = jnp.bfloat16
HIGHEST = lax.Precision.HIGHEST

NT_DIMS = (((1,), (1,)), ((), ()))
TN_DIMS = (((0,), (0,)), ((), ()))


def _params(*semantics):
    return pltpu.CompilerParams(dimension_semantics=semantics, vmem_limit_bytes=VMEM_LIMIT)


def _dot(a, b, precision=None):
    return jnp.dot(a, b, preferred_element_type=F32, precision=precision)


def _dot_nt(a, b, precision=None):
    return lax.dot_general(a, b, NT_DIMS, preferred_element_type=F32, precision=precision)


def _dot_tn(a, b, precision=None):
    return lax.dot_general(a, b, TN_DIMS, preferred_element_type=F32, precision=precision)


def _rms(x, gain):
    return x * lax.rsqrt(jnp.mean(x * x, axis=-1, keepdims=True) + RMS_EPS) * gain


def _bucket_thresholds():
    d = np.arange(1 << 15)
    max_exact = N_BUCKETS // 2
    d_f = np.maximum(d, 1).astype(np.float32)
    large = max_exact + (np.log(d_f / np.float32(max_exact)) / np.float32(math.log(REL_MAX_DISTANCE / max_exact))
                         * np.float32(N_BUCKETS - max_exact)).astype(np.int32)
    bucket = np.where(d < max_exact, d, np.minimum(large, N_BUCKETS - 1))
    return [int(np.argmax(bucket >= k)) if np.any(bucket >= k) else int(1 << 30) for k in range(N_BUCKETS)]


_THRESHOLDS = _bucket_thresholds()


def _bias_tile_kernel(tbl_ref, o_ref, *, base, tile_step, row_step, col_step, dmax, dil):
    h = pl.program_id(0)
    t = pl.program_id(1)
    shape = o_ref.shape[2:]
    i = lax.broadcasted_iota(jnp.int32, shape, 0)
    j = lax.broadcasted_iota(jnp.int32, shape, 1)
    dist = base + tile_step * t + row_step * i + col_step * j
    d = dist * dil
    val = jnp.full(shape, tbl_ref[0, h], F32)
    for k in range(1, N_BUCKETS):
        val = jnp.where(d >= _THRESHOLDS[k], tbl_ref[k, h], val)
    valid = (dist >= 0) & (dist <= dmax)
    o_ref[0, 0] = jnp.where(valid, val * LOG2E, NEG_INF)


def _bias_tiles(rel_bias, n_tiles, rows, cols, *, base, tile_step, row_step, col_step, dmax, dil=1):
    kern = functools.partial(_bias_tile_kernel, base=base, tile_step=tile_step, row_step=row_step,
                             col_step=col_step, dmax=dmax, dil=dil)
    return pl.pallas_call(
        kern,
        out_shape=jax.ShapeDtypeStruct((N_HEADS, n_tiles, rows, cols), F32),
        grid=(N_HEADS, n_tiles),
        in_specs=[pl.BlockSpec(memory_space=pltpu.SMEM)],
        out_specs=pl.BlockSpec((1, 1, rows, cols), lambda h, t: (h, t, 0, 0)),
        compiler_params=_params("parallel", "parallel"),
        name="bias_tiles",
    )(rel_bias)


def _resident(shape):
    return pl.BlockSpec(shape, lambda i: (0,) * len(shape), pipeline_mode=pl.Buffered(1))


def _norm_matmul_kernel(x_ref, g_ref, w_ref, o_ref, *, tn):
    h = _rms(x_ref[...], g_ref[...]).astype(BF16)
    for j in range(w_ref.shape[1] // tn):
        o_ref[:, j * tn:(j + 1) * tn] = _dot(h, w_ref[:, j * tn:(j + 1) * tn]).astype(o_ref.dtype)


def _norm_matmul(x, gain, w, out_dtype, tn):
    m, d = x.shape
    n = w.shape[1]
    return pl.pallas_call(
        functools.partial(_norm_matmul_kernel, tn=tn),
        out_shape=jax.ShapeDtypeStruct((m, n), out_dtype),
        grid=(m // ROW_TILE,),
        in_specs=[pl.BlockSpec((ROW_TILE, d), lambda i: (i, 0)),
                  _resident((1, d)),
                  _resident((d, n))],
        out_specs=pl.BlockSpec((ROW_TILE, n), lambda i: (i, 0)),
        compiler_params=_params("parallel"),
        name="norm_matmul",
    )(x, gain.reshape(1, d), w)


def _matmul_residual_kernel(a_ref, w_ref, x_ref, o_ref):
    o_ref[...] = x_ref[...] + _dot(a_ref[...], w_ref[...])


def _matmul_residual(a, w, x):
    m, k = a.shape
    d = w.shape[1]
    return pl.pallas_call(
        _matmul_residual_kernel,
        out_shape=jax.ShapeDtypeStruct((m, d), F32),
        grid=(m // ROW_TILE,),
        in_specs=[pl.BlockSpec((ROW_TILE, k), lambda i: (i, 0)),
                  pl.BlockSpec((k, d), lambda i: (0, 0)),
                  pl.BlockSpec((ROW_TILE, d), lambda i: (i, 0))],
        out_specs=pl.BlockSpec((ROW_TILE, d), lambda i: (i, 0)),
        compiler_params=_params("parallel"),
        name="matmul_residual",
    )(a, w, x)


def _ffn_kernel(x_ref, g_ref, wg_ref, wu_ref, wd_ref, o_ref):
    x = x_ref[...]
    h = _rms(x, g_ref[...]).astype(BF16)
    hidden = wg_ref.shape[1]
    acc = x
    for lo in range(0, hidden, FFN_TILE):
        hi = min(lo + FFN_TILE, hidden)
        a = _dot(h, wg_ref[:, lo:hi])
        b = _dot(h, wu_ref[:, lo:hi])
        acc = acc + _dot((a * jax.nn.sigmoid(a) * b).astype(BF16), wd_ref[lo:hi, :])
    o_ref[...] = acc


def _ffn(x, gain, w_gate, w_up, w_down):
    m, d = x.shape
    hidden = w_gate.shape[1]
    return pl.pallas_call(
        _ffn_kernel,
        out_shape=jax.ShapeDtypeStruct((m, d), F32),
        grid=(m // ROW_TILE,),
        in_specs=[pl.BlockSpec((ROW_TILE, d), lambda i: (i, 0)),
                  _resident((1, d)),
                  _resident((d, hidden)), _resident((d, hidden)), _resident((hidden, d))],
        out_specs=pl.BlockSpec((ROW_TILE, d), lambda i: (i, 0)),
        compiler_params=_params("parallel"),
        name="ffn",
    )(x, gain.reshape(1, d), w_gate, w_up, w_down)


def _a_proj_kernel(x_ref, g_ref, w_ref, qg_ref, kg_ref, o_ref, h_ref, x_scr, *, dil):
    rows = ROW_TILE // dil
    xn = _rms(x_ref[...], g_ref[...])
    if dil == 1:
        h_ref[...] = xn.astype(BF16)
    else:
        slabs = xn.shape[1] // 128
        for c in range(slabs):
            x_scr[c] = xn[:, c * 128:(c + 1) * 128]
        for r in range(dil):
            picked = [x_scr[c, pl.ds(r, rows, stride=dil), :] for c in range(slabs)]
            h_ref[r * rows:(r + 1) * rows, :] = jnp.concatenate(picked, axis=1).astype(BF16)
    h = h_ref[...]
    width = w_ref.shape[1]
    hd = width // 3
    low = lax.broadcasted_iota(jnp.int32, (ROW_TILE, 128), 1) < A_HEAD_DIM
    for j in range(width // A_PROJ_TILE):
        res = _dot(h, w_ref[:, j * A_PROJ_TILE:(j + 1) * A_PROJ_TILE])
        kind = (j * A_PROJ_TILE) // hd
        if kind < 2:
            parts = []
            for c in range(A_PROJ_TILE // 128):
                y = res[:, c * 128:(c + 1) * 128]
                sq = y * y
                tot = jnp.sum(sq, axis=-1, keepdims=True)
                lo = jnp.sum(jnp.where(low, sq, 0.0), axis=-1, keepdims=True)
                ss = jnp.where(low, lo, tot - lo)
                parts.append(y * lax.rsqrt(ss * (1.0 / A_HEAD_DIM) + RMS_EPS))
            res = jnp.concatenate(parts, axis=1) * (qg_ref if kind == 0 else kg_ref)[...]
        res = res.astype(BF16)
        for r in range(dil):
            off = r * width + j * A_PROJ_TILE
            o_ref[:, off:off + A_PROJ_TILE] = res[r * rows:(r + 1) * rows]


def _a_proj(x, gain, w, q_gain, k_gain, dil):
    m, d = x.shape
    width = w.shape[1]
    reps = A_PROJ_TILE // A_HEAD_DIM
    qg = jnp.tile(q_gain * (A_HEAD_DIM ** -0.5 * LOG2E), reps).reshape(1, A_PROJ_TILE)
    kg = jnp.tile(k_gain, reps).reshape(1, A_PROJ_TILE)
    return pl.pallas_call(
        functools.partial(_a_proj_kernel, dil=dil),
        out_shape=jax.ShapeDtypeStruct((m // dil, dil * width), BF16),
        grid=(m // ROW_TILE,),
        in_specs=[pl.BlockSpec((ROW_TILE, d), lambda i: (i, 0)),
                  pl.BlockSpec((1, d), lambda i: (0, 0)),
                  pl.BlockSpec((d, width), lambda i: (0, 0)),
                  pl.BlockSpec((1, A_PROJ_TILE), lambda i: (0, 0)),
                  pl.BlockSpec((1, A_PROJ_TILE), lambda i: (0, 0))],
        out_specs=pl.BlockSpec((ROW_TILE // dil, dil * width), lambda i: (i, 0)),
        scratch_shapes=[pltpu.VMEM((ROW_TILE, d), BF16), pltpu.VMEM((d // 128, ROW_TILE, 128), F32)],
        compiler_params=_params("parallel"),
        name="a_proj",
    )(x, gain.reshape(1, d), w, qg, kg)


def _a_attn_kernel(q_ref, kp_ref, kc_ref, vp_ref, vc_ref, bias_ref, o_ref, lse_ref):
    first = (pl.program_id(2) == 0).astype(jnp.int32)
    lane = lax.broadcasted_iota(jnp.int32, (A_Q_BLOCK, 128), 1)
    low = lane < A_HEAD_DIM
    ones = jnp.ones((A_Q_BLOCK, 128), BF16)
    lse_tile = jnp.zeros((A_Q_BLOCK, 128), F32)
    nq = A_Q_BLOCK
    for pair in range(N_HEADS // 2):
        sl = slice(pair * 128, (pair + 1) * 128)
        q = q_ref[0, :, sl]
        zero = jnp.zeros_like(q)
        qq = jnp.concatenate([jnp.where(low, q, zero), jnp.where(low, zero, q)], axis=0)
        kk = jnp.concatenate([kp_ref[0, :, sl], kc_ref[0, :, sl]], axis=0)
        vv = jnp.concatenate([jnp.concatenate([vp_ref[0, :, sl], ones], axis=1),
                              jnp.concatenate([vc_ref[0, :, sl], ones], axis=1)], axis=0)
        base = 2 * pair + N_HEADS * first
        s = _dot_nt(qq, kk) + jnp.concatenate([bias_ref[base], bias_ref[base + 1]], axis=0)
        m = jnp.max(s, axis=-1, keepdims=True)
        acc = _dot(jnp.exp2(s - m).astype(BF16), vv)
        z = acc[:, 128:]
        o = acc[:, :128] * (1.0 / z)
        lse = m * LN2 + jnp.log(z)
        o_ref[0, :, sl] = jnp.where(low, o[:nq], o[nq:]).astype(o_ref.dtype)
        lse_tile = jnp.where(lane == 2 * pair, lse[:nq], jnp.where(lane == 2 * pair + 1, lse[nq:], lse_tile))
    lse_ref[0] = lse_tile


def _a_attention(proj, bias, dil, batch, seq):
    length = seq // dil
    nblk = length // A_Q_BLOCK
    hd = N_HEADS * A_HEAD_DIM
    pv = proj.reshape(batch, length, dil * 3 * hd)

    def spec(off, prev):
        if prev:
            return pl.BlockSpec((1, A_Q_BLOCK, hd), lambda b, r, i: (b, jnp.maximum(i - 1, 0), r * 3 + off))
        return pl.BlockSpec((1, A_Q_BLOCK, hd), lambda b, r, i: (b, i, r * 3 + off))

    o, lse = pl.pallas_call(
        _a_attn_kernel,
        out_shape=(jax.ShapeDtypeStruct((batch, length, dil * hd), BF16),
                   jax.ShapeDtypeStruct((batch, length, dil * 128), F32)),
        grid=(batch, dil, nblk),
        in_specs=[spec(0, False), spec(1, True), spec(1, False), spec(2, True), spec(2, False),
                  pl.BlockSpec((2 * N_HEADS, A_Q_BLOCK, 2 * A_Q_BLOCK), lambda b, r, i: (0, 0, 0))],
        out_specs=(pl.BlockSpec((1, A_Q_BLOCK, hd), lambda b, r, i: (b, i, r)),
                   pl.BlockSpec((1, A_Q_BLOCK, 128), lambda b, r, i: (b, i, r))),
        compiler_params=_params("parallel", "parallel", "arbitrary"),
        name="a_attention",
    )(pv, pv, pv, pv, pv, bias)
    return o.reshape(batch * length, dil * hd), lse.reshape(batch * length, dil * 128)


def _a_out_kernel(o0_ref, o1_ref, o2_ref, l0_ref, l1_ref, l2_ref, e_ref, w_ref, x_ref, out_ref, o_scr, l_scr):
    hd = N_HEADS * A_HEAD_DIM
    for g, (o_ref, l_ref) in enumerate(((o0_ref, l0_ref), (o1_ref, l1_ref), (o2_ref, l2_ref))):
        dil = A_GROUPS[g][1]
        rows = ROW_TILE // dil
        for r in range(dil):
            dst = pl.ds(r, rows, stride=dil) if dil > 1 else slice(None)
            l_scr[g, dst, :] = l_ref[:, r * 128:(r + 1) * 128]
            for c in range(hd // 128):
                o_scr[g, c, dst, :] = o_ref[:, r * hd + c * 128:r * hd + (c + 1) * 128].astype(F32)
    ls = [l_scr[g] for g in range(len(A_GROUPS))]
    m = jnp.maximum(jnp.maximum(ls[0], ls[1]), ls[2])
    es = [jnp.exp(l - m) for l in ls]
    inv = 1.0 / (es[0] + es[1] + es[2])
    expand = e_ref[...]
    acc = None
    for g, e in enumerate(es):
        wgt = e * inv
        hi = wgt.astype(BF16)
        lo = (wgt - hi.astype(F32)).astype(BF16)
        o_g = jnp.concatenate([o_scr[g, c] for c in range(hd // 128)], axis=1)
        term = (_dot(hi, expand) + _dot(lo, expand)) * o_g
        acc = term if acc is None else acc + term
    out_ref[...] = x_ref[...] + _dot(acc.astype(BF16), w_ref[...])


def _a_out(outs, lses, w_out, x):
    m, d = x.shape
    hd = N_HEADS * A_HEAD_DIM
    expand = np.zeros((128, hd), np.float32)
    for h in range(N_HEADS):
        expand[h, h * A_HEAD_DIM:(h + 1) * A_HEAD_DIM] = 1.0
    grouped = lambda width: [pl.BlockSpec((ROW_TILE // dil, dil * width), lambda i: (i, 0)) for _, dil in A_GROUPS]
    return pl.pallas_call(
        _a_out_kernel,
        out_shape=jax.ShapeDtypeStruct((m, d), F32),
        grid=(m // ROW_TILE,),
        in_specs=grouped(hd) + grouped(128) + [
            pl.BlockSpec((128, hd), lambda i: (0, 0)),
            pl.BlockSpec((hd, d), lambda i: (0, 0)),
            pl.BlockSpec((ROW_TILE, d), lambda i: (i, 0))],
        out_specs=pl.BlockSpec((ROW_TILE, d), lambda i: (i, 0)),
        scratch_shapes=[pltpu.VMEM((len(A_GROUPS), hd // 128, ROW_TILE, 128), F32),
                        pltpu.VMEM((len(A_GROUPS), ROW_TILE, 128), F32)],
        compiler_params=_params("parallel"),
        name="a_out",
    )(*outs, *lses, jnp.asarray(expand, BF16), w_out, x)


def _mixer_a(x, rel_bias, norm1, w_in, q_gain, k_gain, w_out, batch, seq):
    w_in = w_in.astype(BF16)
    group_width = 3 * N_HEADS * A_HEAD_DIM
    outs, lses = [], []
    for gi, (window, dil) in enumerate(A_GROUPS):
        steps = window // dil
        assert steps == A_Q_BLOCK and (seq // dil) % A_Q_BLOCK == 0 and seq % ROW_TILE == 0
        bias = _bias_tiles(rel_bias, 1, A_Q_BLOCK, 2 * A_Q_BLOCK, base=A_Q_BLOCK, tile_step=0, row_step=1,
                           col_step=-1, dmax=steps, dil=dil)[:, 0]
        bias = jnp.concatenate([bias, bias.at[:, :, :A_Q_BLOCK].set(NEG_INF)], axis=0)
        proj = _a_proj(x, norm1, w_in[:, gi * group_width:(gi + 1) * group_width], q_gain[gi], k_gain[gi], dil)
        o, lse = _a_attention(proj, bias, dil, batch, seq)
        outs.append(o)
        lses.append(lse)
    return _a_out(outs, lses, w_out.astype(BF16), x)


def _b_prep_kernel(p_ref, qg_ref, kg_ref, q_ref, ck_ref, cv_ref, sk_ref, sv_ref, wk_ref, wv_ref, gate_ref):
    dh = B_HEAD_DIM
    qg = qg_ref[...] * (dh ** -0.5 * LOG2E)
    for h in range(N_HEADS):
        q_ref[0, h] = _rms(p_ref[0, :, h * dh:(h + 1) * dh].astype(F32), qg).astype(BF16)
    base = N_HEADS * dh
    outs = ((ck_ref, None), (cv_ref, None), (sk_ref, 1), (sv_ref, None), (wk_ref, 2), (wv_ref, None))
    for idx, (ref, gain_row) in enumerate(outs):
        for n in range(B_KV_HEADS):
            off = base + (idx * B_KV_HEADS + n) * dh
            t = p_ref[0, :, off:off + dh]
            if gain_row is not None:
                t = _rms(t.astype(F32), kg_ref[gain_row:gain_row + 1, :]).astype(BF16)
            ref[0, n] = t
    gate_off = base + 6 * B_KV_HEADS * dh
    gate = jax.nn.sigmoid(p_ref[0, :, gate_off:gate_off + 3 * N_HEADS].astype(F32))
    per = 3 * B_GROUP
    for n in range(B_KV_HEADS):
        gate_ref[0, n] = gate[:, n * per:(n + 1) * per]


def _b_prep(proj, q_gain, k_gain, batch, seq):
    ts = 256
    dh = B_HEAD_DIM
    kv_shape = jax.ShapeDtypeStruct((batch, B_KV_HEADS, seq, dh), BF16)
    kv_spec = pl.BlockSpec((1, B_KV_HEADS, ts, dh), lambda b, i: (b, 0, i, 0))
    return pl.pallas_call(
        _b_prep_kernel,
        out_shape=(jax.ShapeDtypeStruct((batch, N_HEADS, seq, dh), BF16),) + (kv_shape,) * 6
        + (jax.ShapeDtypeStruct((batch, B_KV_HEADS, seq, 3 * B_GROUP), F32),),
        grid=(batch, seq // ts),
        in_specs=[pl.BlockSpec((1, ts, B_PROJ_WIDTH), lambda b, i: (b, i, 0)),
                  pl.BlockSpec((1, dh), lambda b, i: (0, 0)),
                  pl.BlockSpec((3, dh), lambda b, i: (0, 0))],
        out_specs=(pl.BlockSpec((1, N_HEADS, ts, dh), lambda b, i: (b, 0, i, 0)),) + (kv_spec,) * 6
        + (pl.BlockSpec((1, B_KV_HEADS, ts, 3 * B_GROUP), lambda b, i: (b, 0, i, 0)),),
        compiler_params=_params("parallel", "parallel"),
        name="b_prep",
    )(proj, q_gain.reshape(1, dh), k_gain)


def _b_compress_kernel(tk_ref, tv_ref, pos_ref, w1_ref, w2_ref, kg_ref, kc_ref, vc_ref):
    half = (B_CMP_LEN // 2) * B_HEAD_DIM
    for kv, (t_ref, out_ref) in enumerate(((tk_ref, kc_ref), (tv_ref, vc_ref))):
        t = t_ref[0, 0].astype(F32)
        top = (t + pos_ref[kv, 0:1, :]).astype(BF16)
        bot = (t + pos_ref[kv, 1:2, :]).astype(BF16)
        a1 = _dot(top, w1_ref[kv, :half, :])
        a2 = _dot(bot, w1_ref[kv, half:, :])
        hidden = a1 + pltpu.roll(a2, a2.shape[0] - 1, 0)
        out = _dot(jax.nn.gelu(hidden).astype(BF16), w2_ref[kv])
        if kv == 0:
            out = _rms(out, kg_ref[...])
        out_ref[0, 0] = out.astype(out_ref.dtype)


def _b_compress(ck, cv, cmp_pos, cmp_w1, cmp_w2, k_gain0, batch, seq):
    rows = seq // B_CMP_STRIDE
    half = (B_CMP_LEN // 2) * B_HEAD_DIM
    tk = ck.reshape(batch, B_KV_HEADS, rows, half)
    tv = cv.reshape(batch, B_KV_HEADS, rows, half)
    pos = cmp_pos.reshape(2, 2, half)
    t_spec = pl.BlockSpec((1, 1, rows, half), lambda b, n: (b, n, 0, 0))
    o_spec = pl.BlockSpec((1, 1, rows, B_HEAD_DIM), lambda b, n: (b, n, 0, 0))
    shape = jax.ShapeDtypeStruct((batch, B_KV_HEADS, rows, B_HEAD_DIM), BF16)
    return pl.pallas_call(
        _b_compress_kernel,
        out_shape=(shape, shape),
        grid=(batch, B_KV_HEADS),
        in_specs=[t_spec, t_spec,
                  pl.BlockSpec((2, 2, half), lambda b, n: (0, 0, 0)),
                  pl.BlockSpec((2, 2 * half, B_CMP_HIDDEN), lambda b, n: (0, 0, 0)),
                  pl.BlockSpec((2, B_CMP_HIDDEN, B_HEAD_DIM), lambda b, n: (0, 0, 0)),
                  pl.BlockSpec((1, B_HEAD_DIM), lambda b, n: (0, 0))],
        out_specs=(o_spec, o_spec),
        compiler_params=_params("parallel", "parallel"),
        name="b_compress",
    )(tk, tv, pos, cmp_w1.astype(BF16), cmp_w2.astype(BF16), k_gain0.reshape(1, -1))


def _b_cmp_attn_kernel(qt_ref, kc_ref, vct_ref, bias_ref, c2s_ref, oc_ref, sel_ref, imp_ref, *, top_n):
    tq = B_SWEEP
    n_sel = imp_ref.shape[0]
    qt = jnp.concatenate([qt_ref[0, g] for g in range(B_GROUP)], axis=1)
    bias = jnp.concatenate([bias_ref[g, 0] for g in range(B_GROUP)], axis=1)
    s = _dot(kc_ref[0, 0], qt) + bias
    valid = bias > 0.5 * NEG_INF
    m = jnp.max(s, axis=0, keepdims=True)
    e = jnp.where(valid, jnp.exp2(s - m), 0.0)
    z = jnp.maximum(jnp.sum(e, axis=0, keepdims=True), TINY)
    p = e * (1.0 / z)
    oct = _dot(vct_ref[0, 0], p.astype(BF16))
    for g in range(B_GROUP):
        oc_ref[0, g] = oct[:, g * tq:(g + 1) * tq]

    p_sum = p[:, 0:tq] + p[:, tq:2 * tq] + p[:, 2 * tq:3 * tq] + p[:, 3 * tq:4 * tq]
    hi = p_sum.astype(BF16)
    lo = (p_sum - hi.astype(F32)).astype(BF16)
    c2s = c2s_ref[...]
    imp = _dot(c2s, hi) + _dot(c2s, lo)

    t = pl.program_id(2) * tq + lax.broadcasted_iota(jnp.int32, (n_sel, tq), 1)
    blk = lax.broadcasted_iota(jnp.int32, (n_sel, tq), 0)
    cur = t // B_SEL_BLOCK
    forced = (blk == 0) | (blk == cur) | (blk == cur - 1)
    imp = jnp.where(forced, FORCE_SCORE, jnp.where(blk * B_SEL_BLOCK <= t, imp, NEG_INF))
    imp_ref[...] = imp

    def count(i, rank):
        row = imp_ref[pl.ds(i, 1), :]
        ahead = jnp.where(row > imp, 1.0, jnp.where(row == imp, jnp.where(blk > i, 1.0, 0.0), 0.0))
        return rank + ahead

    n_live = jnp.minimum(n_sel, (pl.program_id(2) + 1) * (tq // B_SEL_BLOCK))
    rank = lax.fori_loop(0, n_live, count, jnp.zeros((n_sel, tq), F32))
    sel_ref[0, 0] = jnp.where(rank < top_n, 0.0, NEG_INF).astype(sel_ref.dtype)


def _b_cmp_attn(qt, kc, vc, bias_c, batch, seq):
    n_sel = seq // B_SEL_BLOCK
    n_cmp_pad = seq // B_CMP_STRIDE
    n_cmp = (seq - B_CMP_LEN) // B_CMP_STRIDE + 1
    c = np.arange(n_cmp_pad)[None, :] * B_CMP_STRIDE
    j = np.arange(n_sel)[:, None] * B_SEL_BLOCK
    c2s = ((c < j + B_SEL_BLOCK) & (c + B_CMP_LEN > j) & (np.arange(n_cmp_pad)[None, :] < n_cmp)).astype(np.float32)
    kern = functools.partial(_b_cmp_attn_kernel, top_n=min(B_TOP_N, n_sel))
    return pl.pallas_call(
        kern,
        out_shape=(jax.ShapeDtypeStruct((batch, N_HEADS, B_HEAD_DIM, seq), F32),
                   jax.ShapeDtypeStruct((batch, B_KV_HEADS, n_sel, seq), BF16)),
        grid=(batch, B_KV_HEADS, seq // B_SWEEP),
        in_specs=[pl.BlockSpec((1, B_GROUP, B_HEAD_DIM, B_SWEEP), lambda b, n, i: (b, n, 0, i)),
                  pl.BlockSpec((1, 1, n_cmp_pad, B_HEAD_DIM), lambda b, n, i: (b, n, 0, 0)),
                  pl.BlockSpec((1, 1, B_HEAD_DIM, n_cmp_pad), lambda b, n, i: (b, n, 0, 0)),
                  pl.BlockSpec((B_GROUP, 1, n_cmp_pad, B_SWEEP), lambda b, n, i: (n, i, 0, 0)),
                  pl.BlockSpec((n_sel, n_cmp_pad), lambda b, n, i: (0, 0))],
        out_specs=(pl.BlockSpec((1, B_GROUP, B_HEAD_DIM, B_SWEEP), lambda b, n, i: (b, n, 0, i)),
                   pl.BlockSpec((1, 1, n_sel, B_SWEEP), lambda b, n, i: (b, n, 0, i))),
        scratch_shapes=[pltpu.VMEM((n_sel, B_SWEEP), F32)],
        compiler_params=_params("parallel", "parallel", "arbitrary"),
        name="b_cmp_attn",
    )(qt, kc, vc.transpose(0, 1, 3, 2), bias_c, jnp.asarray(c2s, BF16))


def _b_sparse_kernel(qt_ref, ka_ref, vs_ref, wk_ref, vw_ref, sel_ref, bs_ref, bw_ref, oc_ref, gate_ref,
                     o_ref, acc_ref, sa_ref, sb_ref, *, delta_max, win_tiles):
    tq = B_SWEEP
    dh = B_HEAD_DIM
    cols = B_GROUP * tq
    qi = pl.program_id(2)
    n_tiles = ka_ref.shape[2] // tq
    qt = jnp.concatenate([qt_ref[0, g] for g in range(B_GROUP)], axis=1)
    q_aug = jnp.concatenate([qt, jnp.concatenate([sel_ref[0, 0]] * B_GROUP, axis=1)], axis=0)

    def tile_start(kt):
        return pl.multiple_of(jnp.clip(kt, 0, n_tiles - 1) * tq, tq)

    def normalised(acc):
        return acc[:dh] * (1.0 / acc[dh:dh + 1])

    def sel_bias(kt):
        d = jnp.clip(qi - kt, -1, delta_max) + 1
        return jnp.concatenate([bs_ref[g, d] for g in range(B_GROUP)], axis=1)

    def sel_scores(kt):
        return _dot(ka_ref[0, 0, pl.ds(tile_start(kt), tq), :], q_aug) + sel_bias(kt)

    def consume(s_buf, kt, m_old):
        s = s_buf[...]
        m_new = jnp.maximum(m_old, jnp.max(s, axis=0, keepdims=True))
        alpha = jnp.exp2(m_old - m_new)
        p = jnp.exp2(s - m_new).astype(BF16)
        acc_ref[...] = alpha * acc_ref[...] + _dot(vs_ref[0, 0, :, pl.ds(tile_start(kt), tq)], p)
        return m_new

    acc_ref[...] = jnp.zeros(acc_ref.shape, F32)
    sa_ref[...] = sel_scores(0)

    def pair(j, m):
        kt = 2 * j
        sb_ref[...] = sel_scores(kt + 1)
        m = consume(sa_ref, kt, m)
        sa_ref[...] = sel_scores(kt + 2)
        return consume(sb_ref, kt + 1, m)

    lax.fori_loop(0, (qi + 2) // 2, pair, jnp.full((1, cols), NEG_INF, F32))
    o_s = normalised(acc_ref[...])

    tiles = []
    for u in range(win_tiles):
        kt = qi - (win_tiles - 1) + u
        d = jnp.where(kt >= 0, qi - kt, -1) + 1
        bias = jnp.concatenate([bw_ref[g, d] for g in range(B_GROUP)], axis=1)
        tiles.append((_dot(wk_ref[0, 0, pl.ds(tile_start(kt), tq), :], qt) + bias, kt))
    m = None
    for s, _ in tiles:
        tile_max = jnp.max(s, axis=0, keepdims=True)
        m = tile_max if m is None else jnp.maximum(m, tile_max)
    acc = None
    for s, kt in tiles:
        pv = _dot(vw_ref[0, 0, :, pl.ds(tile_start(kt), tq)], jnp.exp2(s - m).astype(BF16))
        acc = pv if acc is None else acc + pv
    o_w = normalised(acc)

    gate = gate_ref[0, 0]
    merged = []
    for g in range(B_GROUP):
        cs = slice(g * tq, (g + 1) * tq)
        merged.append(gate[3 * g:3 * g + 1] * oc_ref[0, g] + gate[3 * g + 1:3 * g + 2] * o_s[:, cs]
                      + gate[3 * g + 2:3 * g + 3] * o_w[:, cs])
    for pair in range(B_GROUP // 2):
        both = jnp.concatenate([merged[2 * pair], merged[2 * pair + 1]], axis=0)
        o_ref[0, :, pair * 2 * dh:(pair + 1) * 2 * dh] = both.T.astype(o_ref.dtype)


def _b_sparse(qt, ka, vs, wk, vw, sel, bias_s, bias_w, oc, gate, batch, seq):
    n_sel = seq // B_SEL_BLOCK
    dh = B_HEAD_DIM
    n_ds = bias_s.shape[1]
    n_dw = bias_w.shape[1]
    vrows = vs.shape[2]
    kern = functools.partial(_b_sparse_kernel, delta_max=n_ds - 2, win_tiles=n_dw - 1)
    whole = lambda rows, width: pl.BlockSpec((1, 1, rows, width), lambda b, n, i: (b, n, 0, 0))
    return pl.pallas_call(
        kern,
        out_shape=jax.ShapeDtypeStruct((batch, seq, N_HEADS * dh), BF16),
        grid=(batch, B_KV_HEADS, seq // B_SWEEP),
        in_specs=[pl.BlockSpec((1, B_GROUP, dh, B_SWEEP), lambda b, n, i: (b, n, 0, i)),
                  whole(seq, dh + n_sel), whole(vrows, seq), whole(seq, dh), whole(vrows, seq),
                  pl.BlockSpec((1, 1, n_sel, B_SWEEP), lambda b, n, i: (b, n, 0, i)),
                  pl.BlockSpec((B_GROUP, n_ds, B_SWEEP, B_SWEEP), lambda b, n, i: (n, 0, 0, 0)),
                  pl.BlockSpec((B_GROUP, n_dw, B_SWEEP, B_SWEEP), lambda b, n, i: (n, 0, 0, 0)),
                  pl.BlockSpec((1, B_GROUP, dh, B_SWEEP), lambda b, n, i: (b, n, 0, i)),
                  pl.BlockSpec((1, 1, 4 * B_GROUP, B_SWEEP), lambda b, n, i: (b, n, 0, i))],
        out_specs=pl.BlockSpec((1, B_SWEEP, B_GROUP * dh), lambda b, n, i: (b, i, n)),
        scratch_shapes=[pltpu.VMEM((vrows, B_GROUP * B_SWEEP), F32),
                        pltpu.VMEM((B_SWEEP, B_GROUP * B_SWEEP), F32),
                        pltpu.VMEM((B_SWEEP, B_GROUP * B_SWEEP), F32)],
        compiler_params=_params("parallel", "parallel", "arbitrary"),
        name="b_sparse",
    )(qt, ka, vs, wk, vw, sel, bias_s, bias_w, oc, gate)


def _mixer_b(x, rel_bias, norm1, w_in, q_gain, k_gain, cmp_pos, cmp_w1, cmp_w2, w_out, batch, seq):
    w_pad = jnp.pad(w_in, ((0, 0), (0, B_PROJ_WIDTH - w_in.shape[1]))).astype(BF16)
    proj = _norm_matmul(x, norm1, w_pad, BF16, 512).reshape(batch, seq, B_PROJ_WIDTH)
    q, ck, cv, sk, sv, wk, wv, gate = _b_prep(proj, q_gain, k_gain, batch, seq)
    kc, vc = _b_compress(ck, cv, cmp_pos, cmp_w1, cmp_w2, k_gain[0], batch, seq)
    bias_c = _bias_tiles(rel_bias, seq // B_SWEEP, seq // B_CMP_STRIDE, B_SWEEP, base=1 - B_CMP_LEN,
                         tile_step=B_SWEEP, row_step=-B_CMP_STRIDE, col_step=1, dmax=1 << 30)
    qt = q.transpose(0, 1, 3, 2)
    oc, sel = _b_cmp_attn(qt, kc, vc, bias_c, batch, seq)
    delta_max = min(seq // B_SWEEP - 1, -(-(_THRESHOLDS[-1] + B_SWEEP - 1) // B_SWEEP))
    bias_s = _bias_tiles(rel_bias, delta_max + 2, B_SWEEP, B_SWEEP, base=-B_SWEEP, tile_step=B_SWEEP,
                         row_step=-1, col_step=1, dmax=1 << 30)
    win_tiles = (B_WINDOW - 1 + B_SWEEP - 1) // B_SWEEP + 1
    bias_w = _bias_tiles(rel_bias, win_tiles + 1, B_SWEEP, B_SWEEP, base=-B_SWEEP, tile_step=B_SWEEP,
                         row_step=-1, col_step=1, dmax=B_WINDOW - 1)
    n_sel = seq // B_SEL_BLOCK
    onehot = (np.arange(seq)[:, None] // B_SEL_BLOCK == np.arange(n_sel)[None, :]).astype(np.float32)
    ka = jnp.concatenate([sk, jnp.broadcast_to(jnp.asarray(onehot, BF16), sk.shape[:2] + onehot.shape)], axis=-1)
    ones = jnp.ones(sv.shape[:2] + (16, seq), BF16)
    vs = jnp.concatenate([sv.transpose(0, 1, 3, 2), ones], axis=2)
    vw = jnp.concatenate([wv.transpose(0, 1, 3, 2), ones], axis=2)
    gate_t = jnp.pad(gate.transpose(0, 1, 3, 2), ((0, 0), (0, 0), (0, B_GROUP), (0, 0)))
    o = _b_sparse(qt, ka, vs, wk, vw, sel, bias_s, bias_w, oc, gate_t, batch, seq)
    return _matmul_residual(o.reshape(batch * seq, -1), w_out.astype(BF16), x)


def _c_conv_kernel(cur_ref, halo_ref, w_ref, sm_ref, alog_ref, dtb_ref, qkv_ref, bg_ref):
    ts = cur_ref.shape[1]
    keep = jnp.where(pl.program_id(1) == 0, 0.0, 1.0)
    dk = C_HEAD_DIM
    for c in range(3 * C_HEADS):
        sl = slice(c * dk, (c + 1) * dk)
        xe = jnp.concatenate([halo_ref[0, :, sl].astype(F32) * keep, cur_ref[0, :, sl].astype(F32)], axis=0)
        y = None
        for j in range(C_CONV):
            off = 8 - (C_CONV - 1) + j
            term = w_ref[j:j + 1, sl] * xe[off:off + ts]
            y = term if y is None else y + term
        y = y * jax.nn.sigmoid(y)
        if c < 2 * C_HEADS:
            y = y * lax.rsqrt(jnp.sum(y * y, axis=-1, keepdims=True) + RMS_EPS)
        if c < C_HEADS:
            y = y * (dk ** -0.5)
        qkv_ref[0, :, sl] = y
    sm = sm_ref[0]
    a = sm + dtb_ref[...]
    softplus = jnp.maximum(a, 0.0) + jnp.log1p(jnp.exp(-jnp.abs(a)))
    g = -jnp.exp(alog_ref[...]) * softplus
    lane = lax.broadcasted_iota(jnp.int32, sm.shape, 1)
    bg_ref[0] = jnp.where(lane < C_HEADS, jax.nn.sigmoid(sm), g)


def _c_conv(proj, small, conv_w, a_log, dt_bias, batch, seq):
    ts = 256
    width = 3 * C_WIDTH
    pad = lambda v: jnp.pad(v, (C_HEADS, 128 - 2 * C_HEADS)).reshape(1, 128)
    return pl.pallas_call(
        _c_conv_kernel,
        out_shape=(jax.ShapeDtypeStruct((batch, seq, width), F32),
                   jax.ShapeDtypeStruct((batch, seq, 128), F32)),
        grid=(batch, seq // ts),
        in_specs=[pl.BlockSpec((1, ts, width), lambda b, i: (b, i, 0)),
                  pl.BlockSpec((1, 8, width), lambda b, i: (b, jnp.maximum(i * (ts // 8) - 1, 0), 0)),
                  pl.BlockSpec((C_CONV, width), lambda b, i: (0, 0)),
                  pl.BlockSpec((1, ts, 128), lambda b, i: (b, i, 0)),
                  pl.BlockSpec((1, 128), lambda b, i: (0, 0)),
                  pl.BlockSpec((1, 128), lambda b, i: (0, 0))],
        out_specs=(pl.BlockSpec((1, ts, width), lambda b, i: (b, i, 0)),
                   pl.BlockSpec((1, ts, 128), lambda b, i: (b, i, 0))),
        compiler_params=_params("parallel", "arbitrary"),
        name="c_conv",
    )(proj, proj, conv_w, small, pad(a_log), pad(dt_bias))


def _sum3(x, fn):
    hi = x.astype(BF16)
    r = x - hi.astype(F32)
    mid = r.astype(BF16)
    lo = (r - mid.astype(F32)).astype(BF16)
    return fn(hi) + (fn(mid) + fn(lo))


def _c_chunk_kernel(qkv_ref, bg_ref, bgt_ref, tri_ref, trit_ref, blk_ref, u_ref, w_ref, qg_ref, kg_ref, attn_ref,
                    gc_ref):
    cs = C_CHUNK
    dk = C_HEAD_DIM
    gs = C_GROUP * cs
    row = lax.broadcasted_iota(jnp.int32, (gs, gs), 0)
    col = lax.broadcasted_iota(jnp.int32, (gs, gs), 1)
    same = (row // cs) == (col // cs)
    causal = same & (row >= col)
    strict = same & (row > col)
    eye = jnp.where(row == col, 1.0, 0.0)

    bgc = bg_ref[0]
    tri = tri_ref[...]
    gcum_col = _sum3(bgc, lambda p: _dot(tri, p))
    glast_col = _sum3(bgc, lambda p: _dot(blk_ref[...], p))
    gcum_row = _sum3(bgt_ref[0], lambda p: _dot(p, trit_ref[...]))
    gc_ref[0] = gcum_col
    t_mats, powers = [], []
    for h in range(C_HEADS):
        gc = gcum_col[:, C_HEADS + h:C_HEADS + h + 1]
        gr = gcum_row[C_HEADS + h:C_HEADS + h + 1, :]
        q = qkv_ref[0, :, h * dk:(h + 1) * dk]
        k = qkv_ref[0, :, C_WIDTH + h * dk:C_WIDTH + (h + 1) * dk]
        decay = jnp.exp(jnp.where(causal, gc - gr, NEG_INF))
        k16 = k.astype(BF16)
        low = jnp.where(strict, _dot_nt((k * bgc[:, h:h + 1]).astype(BF16), k16) * decay, 0.0)
        t_mats.append(eye - low)
        powers.append(low.astype(BF16))
        attn = jnp.where(causal, _dot_nt(q.astype(BF16), k16), 0.0) * decay
        attn_ref[0, :, h * gs:(h + 1) * gs] = attn.astype(attn_ref.dtype)
        qg_ref[0, :, h * dk:(h + 1) * dk] = (q * jnp.exp(gc)).astype(qg_ref.dtype)
        glast = glast_col[:, C_HEADS + h:C_HEADS + h + 1]
        kg_ref[0, :, h * dk:(h + 1) * dk] = (k * jnp.exp(glast - gc)).astype(kg_ref.dtype)
    for _ in range(int(math.log2(cs)) - 1):
        powers = [_dot(p, p).astype(BF16) for p in powers]
        t_mats = [t + _dot(t.astype(BF16), p) for t, p in zip(t_mats, powers)]
    for h in range(C_HEADS):
        gc = gcum_col[:, C_HEADS + h:C_HEADS + h + 1]
        beta = bgc[:, h:h + 1]
        k = qkv_ref[0, :, C_WIDTH + h * dk:C_WIDTH + (h + 1) * dk]
        v = qkv_ref[0, :, 2 * C_WIDTH + h * dk:2 * C_WIDTH + (h + 1) * dk]
        t16 = t_mats[h].astype(BF16)
        u_ref[0, :, h * dk:(h + 1) * dk] = _dot(t16, (v * beta).astype(BF16))
        w_ref[0, :, h * dk:(h + 1) * dk] = _dot(t16, (k * beta * jnp.exp(gc)).astype(BF16)).astype(w_ref.dtype)


def _c_chunks(qkv, bg, bgt, batch, seq):
    gs = C_GROUP * C_CHUNK
    idx = np.arange(gs)
    same = (idx[:, None] // C_CHUNK) == (idx[None, :] // C_CHUNK)
    tri = (same & (idx[:, None] >= idx[None, :])).astype(np.float32)
    wide = lambda width: pl.BlockSpec((1, gs, width), lambda b, i: (b, i, 0))
    shape = lambda width, dtype: jax.ShapeDtypeStruct((batch, seq, width), dtype)
    const = pl.BlockSpec((gs, gs), lambda b, i: (0, 0))
    return pl.pallas_call(
        _c_chunk_kernel,
        out_shape=(shape(C_WIDTH, F32), shape(C_WIDTH, BF16), shape(C_WIDTH, BF16), shape(C_WIDTH, BF16),
                   shape(C_HEADS * gs, BF16), shape(128, F32)),
        grid=(batch, seq // gs),
        in_specs=[wide(3 * C_WIDTH), wide(128),
                  pl.BlockSpec((1, 2 * C_HEADS, gs), lambda b, i: (b, 0, i)),
                  const, const, const],
        out_specs=(wide(C_WIDTH),) * 4 + (wide(C_HEADS * gs), wide(128)),
        compiler_params=_params("parallel", "parallel"),
        name="c_chunks",
    )(qkv, bg, bgt, jnp.asarray(tri, BF16), jnp.asarray(tri.T, BF16), jnp.asarray(same, BF16))


def _c_scan_kernel(u_ref, w_ref, qg_ref, kg_ref, attn_ref, gc_ref, o_ref, state_ref, vnew_ref):
    @pl.when(pl.program_id(1) == 0)
    def _():
        state_ref[...] = jnp.zeros_like(state_ref)

    cs = C_CHUNK
    dk = C_HEAD_DIM
    gs = C_GROUP * cs
    vnew_ref[...] = jnp.zeros_like(vnew_ref)
    for c in range(C_GROUP):
        rs = slice(c * cs, (c + 1) * cs)
        decay_last = jnp.exp(gc_ref[0, (c + 1) * cs - 1:(c + 1) * cs, :])
        for h in range(C_HEADS):
            sl = slice(h * dk, (h + 1) * dk)
            state = state_ref[h]
            s16 = state.astype(BF16)
            v_new = u_ref[0, rs, sl] - _dot(w_ref[0, rs, sl], s16)
            v16 = v_new.astype(BF16)
            vnew_ref[h, rs, :] = v16
            o_ref[0, rs, sl] = _dot(qg_ref[0, rs, sl], s16) + _dot(attn_ref[0, rs, h * gs:(h + 1) * gs], vnew_ref[h])
            state_ref[h] = state * decay_last[:, C_HEADS + h:C_HEADS + h + 1] + _dot_tn(kg_ref[0, rs, sl], v16)


def _c_scan(u, w, qg, kg, attn, gc, batch, seq):
    gs = C_GROUP * C_CHUNK
    wide = lambda width: pl.BlockSpec((1, gs, width), lambda b, c: (b, c, 0))
    return pl.pallas_call(
        _c_scan_kernel,
        out_shape=jax.ShapeDtypeStruct((batch, seq, C_WIDTH), F32),
        grid=(batch, seq // gs),
        in_specs=[wide(C_WIDTH)] * 4 + [wide(C_HEADS * gs), wide(128)],
        out_specs=wide(C_WIDTH),
        scratch_shapes=[pltpu.VMEM((C_HEADS, C_HEAD_DIM, C_HEAD_DIM), F32),
                        pltpu.VMEM((C_HEADS, gs, C_HEAD_DIM), BF16)],
        compiler_params=_params("parallel", "arbitrary"),
        name="c_scan",
    )(u, w, qg, kg, attn, gc)


def _c_out_kernel(o_ref, z_ref, g_ref, w_ref, x_ref, out_ref):
    dk = C_HEAD_DIM
    parts = []
    for h in range(C_HEADS):
        sl = slice(h * dk, (h + 1) * dk)
        z = z_ref[:, sl].astype(F32)
        parts.append((_rms(o_ref[:, sl], g_ref[...]) * (z * jax.nn.sigmoid(z))).astype(BF16))
    out_ref[...] = x_ref[...] + _dot(jnp.concatenate(parts, axis=-1), w_ref[...])


def _c_out(o, proj, out_gain, w_out, x):
    m, d = x.shape
    z_block = (3 * C_WIDTH) // C_WIDTH
    row = lambda width: pl.BlockSpec((ROW_TILE, width), lambda i: (i, 0))
    return pl.pallas_call(
        _c_out_kernel,
        out_shape=jax.ShapeDtypeStruct((m, d), F32),
        grid=(m // ROW_TILE,),
        in_specs=[row(C_WIDTH),
                  pl.BlockSpec((ROW_TILE, C_WIDTH), lambda i: (i, z_block)),
                  pl.BlockSpec((1, C_HEAD_DIM), lambda i: (0, 0)),
                  pl.BlockSpec((C_WIDTH, d), lambda i: (0, 0)),
                  row(d)],
        out_specs=row(d),
        compiler_params=_params("parallel"),
        name="c_out",
    )(o, proj, out_gain.reshape(1, -1), w_out, x)


def _mixer_c(x, norm1, w_in, conv_w, a_log, dt_bias, out_gain, w_out, batch, seq):
    main = 4 * C_WIDTH
    proj = _norm_matmul(x, norm1, w_in[:, :main].astype(BF16), BF16, 512)
    w_small = jnp.pad(w_in[:, main:], ((0, 0), (0, 128 - 2 * C_HEADS))).astype(BF16)
    small = _norm_matmul(x, norm1, w_small, F32, 128)
    qkv, bg = _c_conv(proj.reshape(batch, seq, main), small.reshape(batch, seq, 128), conv_w, a_log, dt_bias,
                      batch, seq)
    bgt = bg[:, :, :2 * C_HEADS].transpose(0, 2, 1)
    u, w, qg, kg, attn, gc = _c_chunks(qkv, bg, bgt, batch, seq)
    o = _c_scan(u, w, qg, kg, attn, gc, batch, seq)
    return _c_out(o.reshape(batch * seq, C_WIDTH), proj, out_gain, w_out.astype(BF16), x)


def kernel(x, rel_bias, l0_norm1, l0_a_w_in, l0_a_q_gain, l0_a_k_gain, l0_a_w_out, l0_norm2, l0_ffn_w_gate, l0_ffn_w_up, l0_ffn_w_down, l1_norm1, l1_b_w_in, l1_b_q_gain, l1_b_k_gain, l1_b_cmp_pos, l1_b_cmp_w1, l1_b_cmp_w2, l1_b_w_out, l1_norm2, l1_ffn_w_gate, l1_ffn_w_up, l1_ffn_w_down, l2_norm1, l2_c_w_in, l2_c_conv_w, l2_c_a_log, l2_c_dt_bias, l2_c_out_gain, l2_c_w_out, l2_norm2, l2_ffn_w_gate, l2_ffn_w_up, l2_ffn_w_down, l3_norm1, l3_a_w_in, l3_a_q_gain, l3_a_k_gain, l3_a_w_out, l3_norm2, l3_ffn_w_gate, l3_ffn_w_up, l3_ffn_w_down):
    batch, seq, d = x.shape
    h = x.reshape(batch * seq, d)

    def ffn(h, norm2, w_gate, w_up, w_down):
        return _ffn(h, norm2, w_gate.astype(BF16), w_up.astype(BF16), w_down.astype(BF16))

    h = _mixer_a(h, rel_bias, l0_norm1, l0_a_w_in, l0_a_q_gain, l0_a_k_gain, l0_a_w_out, batch, seq)
    h = ffn(h, l0_norm2, l0_ffn_w_gate, l0_ffn_w_up, l0_ffn_w_down)
    h = _mixer_b(h, rel_bias, l1_norm1, l1_b_w_in, l1_b_q_gain, l1_b_k_gain, l1_b_cmp_pos, l1_b_cmp_w1,
                 l1_b_cmp_w2, l1_b_w_out, batch, seq)
    h = ffn(h, l1_norm2, l1_ffn_w_gate, l1_ffn_w_up, l1_ffn_w_down)
    h = _mixer_c(h, l2_norm1, l2_c_w_in, l2_c_conv_w, l2_c_a_log, l2_c_dt_bias, l2_c_out_gain, l2_c_w_out,
                 batch, seq)
    h = ffn(h, l2_norm2, l2_ffn_w_gate, l2_ffn_w_up, l2_ffn_w_down)
    h = _mixer_a(h, rel_bias, l3_norm1, l3_a_w_in, l3_a_q_gain, l3_a_k_gain, l3_a_w_out, batch, seq)
    h = ffn(h, l3_norm2, l3_ffn_w_gate, l3_ffn_w_up, l3_ffn_w_down)
    return h.reshape(batch, seq, d)
```

```python
import functools
import math

import numpy as np
import jax
import jax.numpy as jnp
from jax import lax
from jax.experimental import pallas as pl
from jax.experimental.pallas import tpu as pltpu

D_MODEL = 1024
RMS_EPS = 1e-6
NEG_INF = -1e30
TINY = 1e-30
FORCE_SCORE = 1e9

N_BUCKETS = 32
REL_MAX_DISTANCE = 2048
N_HEADS = 16

A_GROUPS = ((128, 1), (512, 4), (2048, 16))
A_HEAD_DIM = 64
A_Q_BLOCK = 128
A_PROJ_TILE = 512

B_KV_HEADS = 4
B_GROUP = 4
B_HEAD_DIM = 64
B_CMP_LEN = 32
B_CMP_STRIDE = 16
B_CMP_HIDDEN = 256
B_SEL_BLOCK = 64
B_TOP_N = 16
B_WINDOW = 512
B_TILE = 128
B_SWEEP = 256
B_PROJ_WIDTH = 3072

C_HEADS = 8
C_HEAD_DIM = 128
C_WIDTH = C_HEADS * C_HEAD_DIM
C_CONV = 4
C_CHUNK = 64
C_GROUP = 4

FFN_HIDDEN = 2816
FFN_TILE = 1024

ROW_TILE = 512
VMEM_LIMIT = 48 * 1024 * 1024

LOG2E = math.log2(math.e)
LN2 = math.log(2.0)

F32 = jnp.float32
BF16 = jnp.bfloat16
HIGHEST = lax.Precision.HIGHEST

NT_DIMS = (((1,), (1,)), ((), ()))
TN_DIMS = (((0,), (0,)), ((), ()))


def _params(*semantics):
    return pltpu.CompilerParams(dimension_semantics=semantics, vmem_limit_bytes=VMEM_LIMIT)


def _dot(a, b, precision=None):
    return jnp.dot(a, b, preferred_element_type=F32, precision=precision)


def _dot_nt(a, b, precision=None):
    return lax.dot_general(a, b, NT_DIMS, preferred_element_type=F32, precision=precision)


def _dot_tn(a, b, precision=None):
    return lax.dot_general(a, b, TN_DIMS, preferred_element_type=F32, precision=precision)


def _rms(x, gain):
    return x * lax.rsqrt(jnp.mean(x * x, axis=-1, keepdims=True) + RMS_EPS) * gain


def _bucket_thresholds():
    d = np.arange(1 << 15)
    max_exact = N_BUCKETS // 2
    d_f = np.maximum(d, 1).astype(np.float32)
    large = max_exact + (np.log(d_f / np.float32(max_exact)) / np.float32(math.log(REL_MAX_DISTANCE / max_exact))
                         * np.float32(N_BUCKETS - max_exact)).astype(np.int32)
    bucket = np.where(d < max_exact, d, np.minimum(large, N_BUCKETS - 1))
    return [int(np.argmax(bucket >= k)) if np.any(bucket >= k) else int(1 << 30) for k in range(N_BUCKETS)]


_THRESHOLDS = _bucket_thresholds()


def _bias_tile_kernel(tbl_ref, o_ref, *, base, tile_step, row_step, col_step, dmax, dil):
    h = pl.program_id(0)
    t = pl.program_id(1)
    shape = o_ref.shape[2:]
    i = lax.broadcasted_iota(jnp.int32, shape, 0)
    j = lax.broadcasted_iota(jnp.int32, shape, 1)
    dist = base + tile_step * t + row_step * i + col_step * j
    d = dist * dil
    val = jnp.full(shape, tbl_ref[0, h], F32)
    for k in range(1, N_BUCKETS):
        val = jnp.where(d >= _THRESHOLDS[k], tbl_ref[k, h], val)
    valid = (dist >= 0) & (dist <= dmax)
    o_ref[0, 0] = jnp.where(valid, val * LOG2E, NEG_INF).astype(o_ref.dtype)


def _bias_tiles(rel_bias, n_tiles, rows, cols, *, base, tile_step, row_step, col_step, dmax, dil=1, dtype=F32):
    kern = functools.partial(_bias_tile_kernel, base=base, tile_step=tile_step, row_step=row_step,
                             col_step=col_step, dmax=dmax, dil=dil)
    return pl.pallas_call(
        kern,
        out_shape=jax.ShapeDtypeStruct((N_HEADS, n_tiles, rows, cols), dtype),
        grid=(N_HEADS, n_tiles),
        in_specs=[pl.BlockSpec(memory_space=pltpu.SMEM)],
        out_specs=pl.BlockSpec((1, 1, rows, cols), lambda h, t: (h, t, 0, 0)),
        compiler_params=_params("parallel", "parallel"),
        name="bias_tiles",
    )(rel_bias)


def _resident(shape):
    return pl.BlockSpec(shape, lambda i: (0,) * len(shape), pipeline_mode=pl.Buffered(1))


def _norm_matmul_kernel(x_ref, g_ref, w_ref, o_ref, *, tn):
    h = _rms(x_ref[...], g_ref[...]).astype(BF16)
    for j in range(w_ref.shape[1] // tn):
        o_ref[:, j * tn:(j + 1) * tn] = _dot(h, w_ref[:, j * tn:(j + 1) * tn]).astype(o_ref.dtype)


def _norm_matmul(x, gain, w, out_dtype, tn):
    m, d = x.shape
    n = w.shape[1]
    return pl.pallas_call(
        functools.partial(_norm_matmul_kernel, tn=tn),
        out_shape=jax.ShapeDtypeStruct((m, n), out_dtype),
        grid=(m // ROW_TILE,),
        in_specs=[pl.BlockSpec((ROW_TILE, d), lambda i: (i, 0)),
                  _resident((1, d)),
                  _resident((d, n))],
        out_specs=pl.BlockSpec((ROW_TILE, n), lambda i: (i, 0)),
        compiler_params=_params("parallel"),
        name="norm_matmul",
    )(x, gain.reshape(1, d), w)


def _matmul_residual_kernel(a_ref, w_ref, x_ref, o_ref):
    o_ref[...] = x_ref[...] + _dot(a_ref[...], w_ref[...])


def _matmul_residual(a, w, x):
    m, k = a.shape
    d = w.shape[1]
    return pl.pallas_call(
        _matmul_residual_kernel,
        out_shape=jax.ShapeDtypeStruct((m, d), F32),
        grid=(m // ROW_TILE,),
        in_specs=[pl.BlockSpec((ROW_TILE, k), lambda i: (i, 0)),
                  pl.BlockSpec((k, d), lambda i: (0, 0)),
                  pl.BlockSpec((ROW_TILE, d), lambda i: (i, 0))],
        out_specs=pl.BlockSpec((ROW_TILE, d), lambda i: (i, 0)),
        compiler_params=_params("parallel"),
        name="matmul_residual",
    )(a, w, x)


def _ffn_kernel(x_ref, g_ref, wg_ref, wu_ref, wd_ref, o_ref):
    x = x_ref[...]
    h = _rms(x, g_ref[...]).astype(BF16)
    hidden = wg_ref.shape[1]
    acc = x
    for lo in range(0, hidden, FFN_TILE):
        hi = min(lo + FFN_TILE, hidden)
        a = _dot(h, wg_ref[:, lo:hi])
        b = _dot(h, wu_ref[:, lo:hi])
        acc = acc + _dot((a * jax.nn.sigmoid(a) * b).astype(BF16), wd_ref[lo:hi, :])
    o_ref[...] = acc


def _ffn(x, gain, w_gate, w_up, w_down):
    m, d = x.shape
    hidden = w_gate.shape[1]
    return pl.pallas_call(
        _ffn_kernel,
        out_shape=jax.ShapeDtypeStruct((m, d), F32),
        grid=(m // ROW_TILE,),
        in_specs=[pl.BlockSpec((ROW_TILE, d), lambda i: (i, 0)),
                  _resident((1, d)),
                  _resident((d, hidden)), _resident((d, hidden)), _resident((hidden, d))],
        out_specs=pl.BlockSpec((ROW_TILE, d), lambda i: (i, 0)),
        compiler_params=_params("parallel"),
        name="ffn",
    )(x, gain.reshape(1, d), w_gate, w_up, w_down)


def _a_proj_kernel(x_ref, g_ref, w_ref, qg_ref, kg_ref, o_ref, h_ref, x_scr, *, dil):
    rows = ROW_TILE // dil
    xn = _rms(x_ref[...], g_ref[...])
    if dil == 1:
        h_ref[...] = xn.astype(BF16)
    else:
        slabs = xn.shape[1] // 128
        for c in range(slabs):
            x_scr[c] = xn[:, c * 128:(c + 1) * 128]
        for r in range(dil):
            picked = [x_scr[c, pl.ds(r, rows, stride=dil), :] for c in range(slabs)]
            h_ref[r * rows:(r + 1) * rows, :] = jnp.concatenate(picked, axis=1).astype(BF16)
    h = h_ref[...]
    width = w_ref.shape[1]
    hd = width // 3
    low = lax.broadcasted_iota(jnp.int32, (ROW_TILE, 128), 1) < A_HEAD_DIM
    for j in range(width // A_PROJ_TILE):
        res = _dot(h, w_ref[:, j * A_PROJ_TILE:(j + 1) * A_PROJ_TILE])
        kind = (j * A_PROJ_TILE) // hd
        if kind < 2:
            parts = []
            for c in range(A_PROJ_TILE // 128):
                y = res[:, c * 128:(c + 1) * 128]
                sq = y * y
                tot = jnp.sum(sq, axis=-1, keepdims=True)
                lo = jnp.sum(jnp.where(low, sq, 0.0), axis=-1, keepdims=True)
                ss = jnp.where(low, lo, tot - lo)
                parts.append(y * lax.rsqrt(ss * (1.0 / A_HEAD_DIM) + RMS_EPS))
            res = jnp.concatenate(parts, axis=1) * (qg_ref if kind == 0 else kg_ref)[...]
        res = res.astype(BF16)
        for r in range(dil):
            off = r * width + j * A_PROJ_TILE
            o_ref[:, off:off + A_PROJ_TILE] = res[r * rows:(r + 1) * rows]


def _a_proj(x, gain, w, q_gain, k_gain, dil):
    m, d = x.shape
    width = w.shape[1]
    reps = A_PROJ_TILE // A_HEAD_DIM
    qg = jnp.tile(q_gain * (A_HEAD_DIM ** -0.5 * LOG2E), reps).reshape(1, A_PROJ_TILE)
    kg = jnp.tile(k_gain, reps).reshape(1, A_PROJ_TILE)
    return pl.pallas_call(
        functools.partial(_a_proj_kernel, dil=dil),
        out_shape=jax.ShapeDtypeStruct((m // dil, dil * width), BF16),
        grid=(m // ROW_TILE,),
        in_specs=[pl.BlockSpec((ROW_TILE, d), lambda i: (i, 0)),
                  pl.BlockSpec((1, d), lambda i: (0, 0)),
                  pl.BlockSpec((d, width), lambda i: (0, 0)),
                  pl.BlockSpec((1, A_PROJ_TILE), lambda i: (0, 0)),
                  pl.BlockSpec((1, A_PROJ_TILE), lambda i: (0, 0))],
        out_specs=pl.BlockSpec((ROW_TILE // dil, dil * width), lambda i: (i, 0)),
        scratch_shapes=[pltpu.VMEM((ROW_TILE, d), BF16), pltpu.VMEM((d // 128, ROW_TILE, 128), F32)],
        compiler_params=_params("parallel"),
        name="a_proj",
    )(x, gain.reshape(1, d), w, qg, kg)


def _a_attn_kernel(q_ref, kp_ref, kc_ref, vp_ref, vc_ref, bias_ref, o_ref, lse_ref):
    first = (pl.program_id(2) == 0).astype(jnp.int32)
    lane = lax.broadcasted_iota(jnp.int32, (A_Q_BLOCK, 128), 1)
    low = lane < A_HEAD_DIM
    ones = jnp.ones((A_Q_BLOCK, 128), BF16)
    lse_tile = jnp.zeros((A_Q_BLOCK, 128), F32)
    nq = A_Q_BLOCK
    for pair in range(N_HEADS // 2):
        sl = slice(pair * 128, (pair + 1) * 128)
        q = q_ref[0, :, sl]
        zero = jnp.zeros_like(q)
        qq = jnp.concatenate([jnp.where(low, q, zero), jnp.where(low, zero, q)], axis=0)
        kk = jnp.concatenate([kp_ref[0, :, sl], kc_ref[0, :, sl]], axis=0)
        vv = jnp.concatenate([jnp.concatenate([vp_ref[0, :, sl], ones], axis=1),
                              jnp.concatenate([vc_ref[0, :, sl], ones], axis=1)], axis=0)
        base = 2 * pair + N_HEADS * first
        s = _dot_nt(qq, kk) + jnp.concatenate([bias_ref[base], bias_ref[base + 1]], axis=0)
        m = jnp.max(s, axis=-1, keepdims=True)
        acc = _dot(jnp.exp2(s - m).astype(BF16), vv)
        z = acc[:, 128:]
        o = acc[:, :128] * (1.0 / z)
        lse = m * LN2 + jnp.log(z)
        o_ref[0, :, sl] = jnp.where(low, o[:nq], o[nq:]).astype(o_ref.dtype)
        lse_tile = jnp.where(lane == 2 * pair, lse[:nq], jnp.where(lane == 2 * pair + 1, lse[nq:], lse_tile))
    lse_ref[0] = lse_tile


def _a_attention(proj, bias, dil, batch, seq):
    length = seq // dil
    nblk = length // A_Q_BLOCK
    hd = N_HEADS * A_HEAD_DIM
    pv = proj.reshape(batch, length, dil * 3 * hd)

    def spec(off, prev):
        if prev:
            return pl.BlockSpec((1, A_Q_BLOCK, hd), lambda b, r, i: (b, jnp.maximum(i - 1, 0), r * 3 + off))
        return pl.BlockSpec((1, A_Q_BLOCK, hd), lambda b, r, i: (b, i, r * 3 + off))

    o, lse = pl.pallas_call(
        _a_attn_kernel,
        out_shape=(jax.ShapeDtypeStruct((batch, length, dil * hd), BF16),
                   jax.ShapeDtypeStruct((batch, length, dil * 128), F32)),
        grid=(batch, dil, nblk),
        in_specs=[spec(0, False), spec(1, True), spec(1, False), spec(2, True), spec(2, False),
                  pl.BlockSpec((2 * N_HEADS, A_Q_BLOCK, 2 * A_Q_BLOCK), lambda b, r, i: (0, 0, 0))],
        out_specs=(pl.BlockSpec((1, A_Q_BLOCK, hd), lambda b, r, i: (b, i, r)),
                   pl.BlockSpec((1, A_Q_BLOCK, 128), lambda b, r, i: (b, i, r))),
        compiler_params=_params("parallel", "parallel", "arbitrary"),
        name="a_attention",
    )(pv, pv, pv, pv, pv, bias)
    return o.reshape(batch * length, dil * hd), lse.reshape(batch * length, dil * 128)


def _a_out_kernel(o0_ref, o1_ref, o2_ref, l0_ref, l1_ref, l2_ref, e_ref, w_ref, x_ref, out_ref, o_scr, l_scr):
    hd = N_HEADS * A_HEAD_DIM
    for g, (o_ref, l_ref) in enumerate(((o0_ref, l0_ref), (o1_ref, l1_ref), (o2_ref, l2_ref))):
        dil = A_GROUPS[g][1]
        rows = ROW_TILE // dil
        for r in range(dil):
            dst = pl.ds(r, rows, stride=dil) if dil > 1 else slice(None)
            l_scr[g, dst, :] = l_ref[:, r * 128:(r + 1) * 128]
            for c in range(hd // 128):
                o_scr[g, c, dst, :] = o_ref[:, r * hd + c * 128:r * hd + (c + 1) * 128].astype(F32)
    ls = [l_scr[g] for g in range(len(A_GROUPS))]
    m = jnp.maximum(jnp.maximum(ls[0], ls[1]), ls[2])
    es = [jnp.exp(l - m) for l in ls]
    inv = 1.0 / (es[0] + es[1] + es[2])
    expand = e_ref[...]
    spread = [_dot((e * inv).astype(BF16), expand) for e in es[:-1]]
    spread.append(1.0 - sum(spread[1:], spread[0]))
    acc = None
    for g, wgt in enumerate(spread):
        o_g = jnp.concatenate([o_scr[g, c] for c in range(hd // 128)], axis=1)
        acc = wgt * o_g if acc is None else acc + wgt * o_g
    out_ref[...] = x_ref[...] + _dot(acc.astype(BF16), w_ref[...])


def _a_out(outs, lses, w_out, x):
    m, d = x.shape
    hd = N_HEADS * A_HEAD_DIM
    expand = np.zeros((128, hd), np.float32)
    for h in range(N_HEADS):
        expand[h, h * A_HEAD_DIM:(h + 1) * A_HEAD_DIM] = 1.0
    grouped = lambda width: [pl.BlockSpec((ROW_TILE // dil, dil * width), lambda i: (i, 0)) for _, dil in A_GROUPS]
    return pl.pallas_call(
        _a_out_kernel,
        out_shape=jax.ShapeDtypeStruct((m, d), F32),
        grid=(m // ROW_TILE,),
        in_specs=grouped(hd) + grouped(128) + [
            pl.BlockSpec((128, hd), lambda i: (0, 0)),
            pl.BlockSpec((hd, d), lambda i: (0, 0)),
            pl.BlockSpec((ROW_TILE, d), lambda i: (i, 0))],
        out_specs=pl.BlockSpec((ROW_TILE, d), lambda i: (i, 0)),
        scratch_shapes=[pltpu.VMEM((len(A_GROUPS), hd // 128, ROW_TILE, 128), F32),
                        pltpu.VMEM((len(A_GROUPS), ROW_TILE, 128), F32)],
        compiler_params=_params("parallel"),
        name="a_out",
    )(*outs, *lses, jnp.asarray(expand, BF16), w_out, x)


def _mixer_a(x, rel_bias, norm1, w_in, q_gain, k_gain, w_out, batch, seq):
    w_in = w_in.astype(BF16)
    group_width = 3 * N_HEADS * A_HEAD_DIM
    outs, lses = [], []
    for gi, (window, dil) in enumerate(A_GROUPS):
        steps = window // dil
        assert steps == A_Q_BLOCK and (seq // dil) % A_Q_BLOCK == 0 and seq % ROW_TILE == 0
        bias = _bias_tiles(rel_bias, 1, A_Q_BLOCK, 2 * A_Q_BLOCK, base=A_Q_BLOCK, tile_step=0, row_step=1,
                           col_step=-1, dmax=steps, dil=dil)[:, 0]
        bias = jnp.concatenate([bias, bias.at[:, :, :A_Q_BLOCK].set(NEG_INF)], axis=0)
        proj = _a_proj(x, norm1, w_in[:, gi * group_width:(gi + 1) * group_width], q_gain[gi], k_gain[gi], dil)
        o, lse = _a_attention(proj, bias, dil, batch, seq)
        outs.append(o)
        lses.append(lse)
    return _a_out(outs, lses, w_out.astype(BF16), x)


def _b_prep_kernel(p_ref, qg_ref, kg_ref, q_ref, ck_ref, cv_ref, sk_ref, sv_ref, wk_ref, wv_ref, gate_ref):
    dh = B_HEAD_DIM
    qg = qg_ref[...] * (dh ** -0.5 * LOG2E)
    for h in range(N_HEADS):
        q_ref[0, h] = _rms(p_ref[0, :, h * dh:(h + 1) * dh].astype(F32), qg).astype(BF16)
    base = N_HEADS * dh
    outs = ((ck_ref, None), (cv_ref, None), (sk_ref, 1), (sv_ref, None), (wk_ref, 2), (wv_ref, None))
    for idx, (ref, gain_row) in enumerate(outs):
        for n in range(B_KV_HEADS):
            off = base + (idx * B_KV_HEADS + n) * dh
            t = p_ref[0, :, off:off + dh]
            if gain_row is not None:
                t = _rms(t.astype(F32), kg_ref[gain_row:gain_row + 1, :]).astype(BF16)
            ref[0, n] = t
    gate_off = base + 6 * B_KV_HEADS * dh
    gate = jax.nn.sigmoid(p_ref[0, :, gate_off:gate_off + 3 * N_HEADS].astype(F32))
    per = 3 * B_GROUP
    for n in range(B_KV_HEADS):
        gate_ref[0, n] = gate[:, n * per:(n + 1) * per]


def _b_prep(proj, q_gain, k_gain, batch, seq):
    ts = 256
    dh = B_HEAD_DIM
    kv_shape = jax.ShapeDtypeStruct((batch, B_KV_HEADS, seq, dh), BF16)
    kv_spec = pl.BlockSpec((1, B_KV_HEADS, ts, dh), lambda b, i: (b, 0, i, 0))
    return pl.pallas_call(
        _b_prep_kernel,
        out_shape=(jax.ShapeDtypeStruct((batch, N_HEADS, seq, dh), BF16),) + (kv_shape,) * 6
        + (jax.ShapeDtypeStruct((batch, B_KV_HEADS, seq, 3 * B_GROUP), F32),),
        grid=(batch, seq // ts),
        in_specs=[pl.BlockSpec((1, ts, B_PROJ_WIDTH), lambda b, i: (b, i, 0)),
                  pl.BlockSpec((1, dh), lambda b, i: (0, 0)),
                  pl.BlockSpec((3, dh), lambda b, i: (0, 0))],
        out_specs=(pl.BlockSpec((1, N_HEADS, ts, dh), lambda b, i: (b, 0, i, 0)),) + (kv_spec,) * 6
        + (pl.BlockSpec((1, B_KV_HEADS, ts, 3 * B_GROUP), lambda b, i: (b, 0, i, 0)),),
        compiler_params=_params("parallel", "parallel"),
        name="b_prep",
    )(proj, q_gain.reshape(1, dh), k_gain)


def _b_compress_kernel(tk_ref, tv_ref, pos_ref, w1_ref, w2_ref, kg_ref, kc_ref, vc_ref):
    half = (B_CMP_LEN // 2) * B_HEAD_DIM
    for kv, (t_ref, out_ref) in enumerate(((tk_ref, kc_ref), (tv_ref, vc_ref))):
        t = t_ref[0, 0].astype(F32)
        top = (t + pos_ref[kv, 0:1, :]).astype(BF16)
        bot = (t + pos_ref[kv, 1:2, :]).astype(BF16)
        a1 = _dot(top, w1_ref[kv, :half, :])
        a2 = _dot(bot, w1_ref[kv, half:, :])
        hidden = a1 + pltpu.roll(a2, a2.shape[0] - 1, 0)
        out = _dot(jax.nn.gelu(hidden).astype(BF16), w2_ref[kv])
        if kv == 0:
            out = _rms(out, kg_ref[...])
        out_ref[0, 0] = out.astype(out_ref.dtype)


def _b_compress(ck, cv, cmp_pos, cmp_w1, cmp_w2, k_gain0, batch, seq):
    rows = seq // B_CMP_STRIDE
    half = (B_CMP_LEN // 2) * B_HEAD_DIM
    tk = ck.reshape(batch, B_KV_HEADS, rows, half)
    tv = cv.reshape(batch, B_KV_HEADS, rows, half)
    pos = cmp_pos.reshape(2, 2, half)
    t_spec = pl.BlockSpec((1, 1, rows, half), lambda b, n: (b, n, 0, 0))
    o_spec = pl.BlockSpec((1, 1, rows, B_HEAD_DIM), lambda b, n: (b, n, 0, 0))
    shape = jax.ShapeDtypeStruct((batch, B_KV_HEADS, rows, B_HEAD_DIM), BF16)
    return pl.pallas_call(
        _b_compress_kernel,
        out_shape=(shape, shape),
        grid=(batch, B_KV_HEADS),
        in_specs=[t_spec, t_spec,
                  pl.BlockSpec((2, 2, half), lambda b, n: (0, 0, 0)),
                  pl.BlockSpec((2, 2 * half, B_CMP_HIDDEN), lambda b, n: (0, 0, 0)),
                  pl.BlockSpec((2, B_CMP_HIDDEN, B_HEAD_DIM), lambda b, n: (0, 0, 0)),
                  pl.BlockSpec((1, B_HEAD_DIM), lambda b, n: (0, 0))],
        out_specs=(o_spec, o_spec),
        compiler_params=_params("parallel", "parallel"),
        name="b_compress",
    )(tk, tv, pos, cmp_w1.astype(BF16), cmp_w2.astype(BF16), k_gain0.reshape(1, -1))


def _b_cmp_attn_kernel(qt_ref, kc_ref, vct_ref, bias_ref, c2s_ref, oc_ref, sel_ref, imp_ref, *, top_n):
    tq = B_SWEEP
    n_sel = imp_ref.shape[0]
    qt = jnp.concatenate([qt_ref[0, g] for g in range(B_GROUP)], axis=1)
    bias = jnp.concatenate([bias_ref[g, 0] for g in range(B_GROUP)], axis=1)
    s = _dot(kc_ref[0, 0], qt) + bias
    valid = bias > 0.5 * NEG_INF
    m = jnp.max(s, axis=0, keepdims=True)
    e = jnp.where(valid, jnp.exp2(s - m), 0.0)
    z = jnp.maximum(jnp.sum(e, axis=0, keepdims=True), TINY)
    p = e * (1.0 / z)
    oct = _dot(vct_ref[0, 0], p.astype(BF16))
    for g in range(B_GROUP):
        oc_ref[0, g] = oct[:, g * tq:(g + 1) * tq]

    p_sum = p[:, 0:tq] + p[:, tq:2 * tq] + p[:, 2 * tq:3 * tq] + p[:, 3 * tq:4 * tq]
    hi = p_sum.astype(BF16)
    lo = (p_sum - hi.astype(F32)).astype(BF16)
    c2s = c2s_ref[...]
    imp = _dot(c2s, hi) + _dot(c2s, lo)

    t = pl.program_id(2) * tq + lax.broadcasted_iota(jnp.int32, (n_sel, tq), 1)
    blk = lax.broadcasted_iota(jnp.int32, (n_sel, tq), 0)
    cur = t // B_SEL_BLOCK
    forced = (blk == 0) | (blk == cur) | (blk == cur - 1)
    imp = jnp.where(forced, FORCE_SCORE, jnp.where(blk * B_SEL_BLOCK <= t, imp, NEG_INF))
    imp_ref[...] = imp

    def count(i, rank):
        row = imp_ref[pl.ds(i, 1), :]
        ahead = jnp.where(row > imp, 1.0, jnp.where(row == imp, jnp.where(blk > i, 1.0, 0.0), 0.0))
        return rank + ahead

    n_live = jnp.minimum(n_sel, (pl.program_id(2) + 1) * (tq // B_SEL_BLOCK))
    rank = lax.fori_loop(0, n_live, count, jnp.zeros((n_sel, tq), F32))
    sel_ref[0, 0] = jnp.where(rank < top_n, 0.0, NEG_INF).astype(sel_ref.dtype)


def _b_cmp_attn(qt, kc, vc, bias_c, batch, seq):
    n_sel = seq // B_SEL_BLOCK
    n_cmp_pad = seq // B_CMP_STRIDE
    n_cmp = (seq - B_CMP_LEN) // B_CMP_STRIDE + 1
    c = np.arange(n_cmp_pad)[None, :] * B_CMP_STRIDE
    j = np.arange(n_sel)[:, None] * B_SEL_BLOCK
    c2s = ((c < j + B_SEL_BLOCK) & (c + B_CMP_LEN > j) & (np.arange(n_cmp_pad)[None, :] < n_cmp)).astype(np.float32)
    kern = functools.partial(_b_cmp_attn_kernel, top_n=min(B_TOP_N, n_sel))
    return pl.pallas_call(
        kern,
        out_shape=(jax.ShapeDtypeStruct((batch, N_HEADS, B_HEAD_DIM, seq), F32),
                   jax.ShapeDtypeStruct((batch, B_KV_HEADS, n_sel, seq), BF16)),
        grid=(batch, B_KV_HEADS, seq // B_SWEEP),
        in_specs=[pl.BlockSpec((1, B_GROUP, B_HEAD_DIM, B_SWEEP), lambda b, n, i: (b, n, 0, i)),
                  pl.BlockSpec((1, 1, n_cmp_pad, B_HEAD_DIM), lambda b, n, i: (b, n, 0, 0)),
                  pl.BlockSpec((1, 1, B_HEAD_DIM, n_cmp_pad), lambda b, n, i: (b, n, 0, 0)),
                  pl.BlockSpec((B_GROUP, 1, n_cmp_pad, B_SWEEP), lambda b, n, i: (n, i, 0, 0)),
                  pl.BlockSpec((n_sel, n_cmp_pad), lambda b, n, i: (0, 0))],
        out_specs=(pl.BlockSpec((1, B_GROUP, B_HEAD_DIM, B_SWEEP), lambda b, n, i: (b, n, 0, i)),
                   pl.BlockSpec((1, 1, n_sel, B_SWEEP), lambda b, n, i: (b, n, 0, i))),
        scratch_shapes=[pltpu.VMEM((n_sel, B_SWEEP), F32)],
        compiler_params=_params("parallel", "parallel", "arbitrary"),
        name="b_cmp_attn",
    )(qt, kc, vc.transpose(0, 1, 3, 2), bias_c, jnp.asarray(c2s, BF16))


def _b_sparse_kernel(qt_ref, ka_ref, vs_ref, wk_ref, vw_ref, sel_ref, bs_ref, bw_ref, oc_ref, gate_ref,
                     o_ref, acc_ref, sa_ref, sb_ref, *, delta_max, win_tiles):
    tq = B_SWEEP
    dh = B_HEAD_DIM
    cols = B_GROUP * tq
    qi = pl.program_id(2)
    n_tiles = ka_ref.shape[2] // tq
    qt = jnp.concatenate([qt_ref[0, g] for g in range(B_GROUP)], axis=1)
    q_aug = jnp.concatenate([qt, jnp.concatenate([sel_ref[0, 0]] * B_GROUP, axis=1)], axis=0)

    def tile_start(kt):
        return pl.multiple_of(jnp.clip(kt, 0, n_tiles - 1) * tq, tq)

    def normalised(acc):
        return acc[:dh] * (1.0 / acc[dh:dh + 1])

    def sel_bias(kt):
        d = jnp.clip(qi - kt, -1, delta_max) + 1
        return jnp.concatenate([bs_ref[g, d] for g in range(B_GROUP)], axis=1)

    def sel_scores(kt):
        return _dot(ka_ref[0, 0, pl.ds(tile_start(kt), tq), :], q_aug).astype(BF16) + sel_bias(kt)

    def consume(s_buf, kt, m_old):
        s = s_buf[...]
        m_new = jnp.maximum(m_old, jnp.max(s, axis=0, keepdims=True).astype(F32))
        alpha = jnp.exp2(m_old - m_new)
        p = jnp.exp2(s - m_new.astype(BF16))
        acc_ref[...] = alpha * acc_ref[...] + _dot(vs_ref[0, 0, :, pl.ds(tile_start(kt), tq)], p)
        return m_new

    acc_ref[...] = jnp.zeros(acc_ref.shape, F32)
    sa_ref[...] = sel_scores(0)

    def pair(j, m):
        kt = 2 * j
        sb_ref[...] = sel_scores(kt + 1)
        m = consume(sa_ref, kt, m)
        sa_ref[...] = sel_scores(kt + 2)
        return consume(sb_ref, kt + 1, m)

    lax.fori_loop(0, (qi + 2) // 2, pair, jnp.full((1, cols), NEG_INF, F32))
    o_s = normalised(acc_ref[...])

    tiles = []
    for u in range(win_tiles):
        kt = qi - (win_tiles - 1) + u
        d = jnp.where(kt >= 0, qi - kt, -1) + 1
        bias = jnp.concatenate([bw_ref[g, d] for g in range(B_GROUP)], axis=1)
        tiles.append((_dot(wk_ref[0, 0, pl.ds(tile_start(kt), tq), :], qt).astype(BF16) + bias, kt))
    m = None
    for s, _ in tiles:
        tile_max = jnp.max(s, axis=0, keepdims=True)
        m = tile_max if m is None else jnp.maximum(m, tile_max)
    acc = None
    for s, kt in tiles:
        pv = _dot(vw_ref[0, 0, :, pl.ds(tile_start(kt), tq)], jnp.exp2(s - m))
        acc = pv if acc is None else acc + pv
    o_w = normalised(acc)

    gate = gate_ref[0, 0]
    merged = []
    for g in range(B_GROUP):
        cs = slice(g * tq, (g + 1) * tq)
        merged.append(gate[3 * g:3 * g + 1] * oc_ref[0, g] + gate[3 * g + 1:3 * g + 2] * o_s[:, cs]
                      + gate[3 * g + 2:3 * g + 3] * o_w[:, cs])
    for pair in range(B_GROUP // 2):
        both = jnp.concatenate([merged[2 * pair], merged[2 * pair + 1]], axis=0)
        o_ref[0, :, pair * 2 * dh:(pair + 1) * 2 * dh] = both.T.astype(o_ref.dtype)


def _b_sparse(qt, ka, vs, wk, vw, sel, bias_s, bias_w, oc, gate, batch, seq):
    n_sel = seq // B_SEL_BLOCK
    dh = B_HEAD_DIM
    n_ds = bias_s.shape[1]
    n_dw = bias_w.shape[1]
    vrows = vs.shape[2]
    kern = functools.partial(_b_sparse_kernel, delta_max=n_ds - 2, win_tiles=n_dw - 1)
    whole = lambda rows, width: pl.BlockSpec((1, 1, rows, width), lambda b, n, i: (b, n, 0, 0))
    return pl.pallas_call(
        kern,
        out_shape=jax.ShapeDtypeStruct((batch, seq, N_HEADS * dh), BF16),
        grid=(batch, B_KV_HEADS, seq // B_SWEEP),
        in_specs=[pl.BlockSpec((1, B_GROUP, dh, B_SWEEP), lambda b, n, i: (b, n, 0, i)),
                  whole(seq, dh + n_sel), whole(vrows, seq), whole(seq, dh), whole(vrows, seq),
                  pl.BlockSpec((1, 1, n_sel, B_SWEEP), lambda b, n, i: (b, n, 0, i)),
                  pl.BlockSpec((B_GROUP, n_ds, B_SWEEP, B_SWEEP), lambda b, n, i: (n, 0, 0, 0)),
                  pl.BlockSpec((B_GROUP, n_dw, B_SWEEP, B_SWEEP), lambda b, n, i: (n, 0, 0, 0)),
                  pl.BlockSpec((1, B_GROUP, dh, B_SWEEP), lambda b, n, i: (b, n, 0, i)),
                  pl.BlockSpec((1, 1, 4 * B_GROUP, B_SWEEP), lambda b, n, i: (b, n, 0, i))],
        out_specs=pl.BlockSpec((1, B_SWEEP, B_GROUP * dh), lambda b, n, i: (b, i, n)),
        scratch_shapes=[pltpu.VMEM((vrows, B_GROUP * B_SWEEP), F32),
                        pltpu.VMEM((B_SWEEP, B_GROUP * B_SWEEP), BF16),
                        pltpu.VMEM((B_SWEEP, B_GROUP * B_SWEEP), BF16)],
        compiler_params=_params("parallel", "parallel", "arbitrary"),
        name="b_sparse",
    )(qt, ka, vs, wk, vw, sel, bias_s, bias_w, oc, gate)


def _mixer_b(x, rel_bias, norm1, w_in, q_gain, k_gain, cmp_pos, cmp_w1, cmp_w2, w_out, batch, seq):
    w_pad = jnp.pad(w_in, ((0, 0), (0, B_PROJ_WIDTH - w_in.shape[1]))).astype(BF16)
    proj = _norm_matmul(x, norm1, w_pad, BF16, 512).reshape(batch, seq, B_PROJ_WIDTH)
    q, ck, cv, sk, sv, wk, wv, gate = _b_prep(proj, q_gain, k_gain, batch, seq)
    kc, vc = _b_compress(ck, cv, cmp_pos, cmp_w1, cmp_w2, k_gain[0], batch, seq)
    bias_c = _bias_tiles(rel_bias, seq // B_SWEEP, seq // B_CMP_STRIDE, B_SWEEP, base=1 - B_CMP_LEN,
                         tile_step=B_SWEEP, row_step=-B_CMP_STRIDE, col_step=1, dmax=1 << 30)
    qt = q.transpose(0, 1, 3, 2)
    oc, sel = _b_cmp_attn(qt, kc, vc, bias_c, batch, seq)
    delta_max = min(seq // B_SWEEP - 1, -(-(_THRESHOLDS[-1] + B_SWEEP - 1) // B_SWEEP))
    bias_s = _bias_tiles(rel_bias, delta_max + 2, B_SWEEP, B_SWEEP, base=-B_SWEEP, tile_step=B_SWEEP,
                         row_step=-1, col_step=1, dmax=1 << 30, dtype=BF16)
    win_tiles = (B_WINDOW - 1 + B_SWEEP - 1) // B_SWEEP + 1
    bias_w = _bias_tiles(rel_bias, win_tiles + 1, B_SWEEP, B_SWEEP, base=-B_SWEEP, tile_step=B_SWEEP,
                         row_step=-1, col_step=1, dmax=B_WINDOW - 1, dtype=BF16)
    n_sel = seq // B_SEL_BLOCK
    onehot = (np.arange(seq)[:, None] // B_SEL_BLOCK == np.arange(n_sel)[None, :]).astype(np.float32)
    ka = jnp.concatenate([sk, jnp.broadcast_to(jnp.asarray(onehot, BF16), sk.shape[:2] + onehot.shape)], axis=-1)
    ones = jnp.ones(sv.shape[:2] + (16, seq), BF16)
    vs = jnp.concatenate([sv.transpose(0, 1, 3, 2), ones], axis=2)
    vw = jnp.concatenate([wv.transpose(0, 1, 3, 2), ones], axis=2)
    gate_t = jnp.pad(gate.transpose(0, 1, 3, 2), ((0, 0), (0, 0), (0, B_GROUP), (0, 0)))
    o = _b_sparse(qt, ka, vs, wk, vw, sel, bias_s, bias_w, oc, gate_t, batch, seq)
    return _matmul_residual(o.reshape(batch * seq, -1), w_out.astype(BF16), x)


def _c_conv_kernel(cur_ref, halo_ref, w_ref, sm_ref, alog_ref, dtb_ref, qkv_ref, bg_ref):
    ts = cur_ref.shape[1]
    keep = jnp.where(pl.program_id(1) == 0, 0.0, 1.0)
    dk = C_HEAD_DIM
    for c in range(3 * C_HEADS):
        sl = slice(c * dk, (c + 1) * dk)
        xe = jnp.concatenate([halo_ref[0, :, sl].astype(F32) * keep, cur_ref[0, :, sl].astype(F32)], axis=0)
        y = None
        for j in range(C_CONV):
            off = 8 - (C_CONV - 1) + j
            term = w_ref[j:j + 1, sl] * xe[off:off + ts]
            y = term if y is None else y + term
        y = y * jax.nn.sigmoid(y)
        if c < 2 * C_HEADS:
            y = y * lax.rsqrt(jnp.sum(y * y, axis=-1, keepdims=True) + RMS_EPS)
        if c < C_HEADS:
            y = y * (dk ** -0.5)
        qkv_ref[0, :, sl] = y
    sm = sm_ref[0]
    a = sm + dtb_ref[...]
    softplus = jnp.maximum(a, 0.0) + jnp.log1p(jnp.exp(-jnp.abs(a)))
    g = -jnp.exp(alog_ref[...]) * softplus
    lane = lax.broadcasted_iota(jnp.int32, sm.shape, 1)
    bg_ref[0] = jnp.where(lane < C_HEADS, jax.nn.sigmoid(sm), g)


def _c_conv(proj, small, conv_w, a_log, dt_bias, batch, seq):
    ts = 256
    width = 3 * C_WIDTH
    pad = lambda v: jnp.pad(v, (C_HEADS, 128 - 2 * C_HEADS)).reshape(1, 128)
    return pl.pallas_call(
        _c_conv_kernel,
        out_shape=(jax.ShapeDtypeStruct((batch, seq, width), F32),
                   jax.ShapeDtypeStruct((batch, seq, 128), F32)),
        grid=(batch, seq // ts),
        in_specs=[pl.BlockSpec((1, ts, width), lambda b, i: (b, i, 0)),
                  pl.BlockSpec((1, 8, width), lambda b, i: (b, jnp.maximum(i * (ts // 8) - 1, 0), 0)),
                  pl.BlockSpec((C_CONV, width), lambda b, i: (0, 0)),
                  pl.BlockSpec((1, ts, 128), lambda b, i: (b, i, 0)),
                  pl.BlockSpec((1, 128), lambda b, i: (0, 0)),
                  pl.BlockSpec((1, 128), lambda b, i: (0, 0))],
        out_specs=(pl.BlockSpec((1, ts, width), lambda b, i: (b, i, 0)),
                   pl.BlockSpec((1, ts, 128), lambda b, i: (b, i, 0))),
        compiler_params=_params("parallel", "arbitrary"),
        name="c_conv",
    )(proj, proj, conv_w, small, pad(a_log), pad(dt_bias))


def _sum3(x, fn):
    hi = x.astype(BF16)
    r = x - hi.astype(F32)
    mid = r.astype(BF16)
    lo = (r - mid.astype(F32)).astype(BF16)
    return fn(hi) + (fn(mid) + fn(lo))


def _c_chunk_kernel(qkv_ref, bg_ref, bgt_ref, tri_ref, trit_ref, blk_ref, u_ref, w_ref, qg_ref, kg_ref, attn_ref,
                    gc_ref):
    cs = C_CHUNK
    dk = C_HEAD_DIM
    gs = C_GROUP * cs
    row = lax.broadcasted_iota(jnp.int32, (gs, gs), 0)
    col = lax.broadcasted_iota(jnp.int32, (gs, gs), 1)
    same = (row // cs) == (col // cs)
    causal = same & (row >= col)
    strict = same & (row > col)
    eye = jnp.where(row == col, 1.0, 0.0)

    bgc = bg_ref[0]
    tri = tri_ref[...]
    gcum_col = _sum3(bgc, lambda p: _dot(tri, p))
    glast_col = _sum3(bgc, lambda p: _dot(blk_ref[...], p))
    gcum_row = _sum3(bgt_ref[0], lambda p: _dot(p, trit_ref[...]))
    gc_ref[0] = gcum_col
    t_mats, powers = [], []
    for h in range(C_HEADS):
        gc = gcum_col[:, C_HEADS + h:C_HEADS + h + 1]
        gr = gcum_row[C_HEADS + h:C_HEADS + h + 1, :]
        q = qkv_ref[0, :, h * dk:(h + 1) * dk]
        k = qkv_ref[0, :, C_WIDTH + h * dk:C_WIDTH + (h + 1) * dk]
        decay = jnp.exp(jnp.where(causal, gc - gr, NEG_INF))
        k16 = k.astype(BF16)
        low = jnp.where(strict, _dot_nt((k * bgc[:, h:h + 1]).astype(BF16), k16) * decay, 0.0)
        t_mats.append(eye - low)
        powers.append(low.astype(BF16))
        attn = jnp.where(causal, _dot_nt(q.astype(BF16), k16), 0.0) * decay
        attn_ref[0, :, h * gs:(h + 1) * gs] = attn.astype(attn_ref.dtype)
        qg_ref[0, :, h * dk:(h + 1) * dk] = (q * jnp.exp(gc)).astype(qg_ref.dtype)
        glast = glast_col[:, C_HEADS + h:C_HEADS + h + 1]
        kg_ref[0, :, h * dk:(h + 1) * dk] = (k * jnp.exp(glast - gc)).astype(kg_ref.dtype)
    for _ in range(int(math.log2(cs)) - 1):
        powers = [_dot(p, p).astype(BF16) for p in powers]
        t_mats = [t + _dot(t.astype(BF16), p) for t, p in zip(t_mats, powers)]
    for h in range(C_HEADS):
        gc = gcum_col[:, C_HEADS + h:C_HEADS + h + 1]
        beta = bgc[:, h:h + 1]
        k = qkv_ref[0, :, C_WIDTH + h * dk:C_WIDTH + (h + 1) * dk]
        v = qkv_ref[0, :, 2 * C_WIDTH + h * dk:2 * C_WIDTH + (h + 1) * dk]
        t16 = t_mats[h].astype(BF16)
        u_ref[0, :, h * dk:(h + 1) * dk] = _dot(t16, (v * beta).astype(BF16))
        w_ref[0, :, h * dk:(h + 1) * dk] = _dot(t16, (k * beta * jnp.exp(gc)).astype(BF16)).astype(w_ref.dtype)


def _c_chunks(qkv, bg, bgt, batch, seq):
    gs = C_GROUP * C_CHUNK
    idx = np.arange(gs)
    same = (idx[:, None] // C_CHUNK) == (idx[None, :] // C_CHUNK)
    tri = (same & (idx[:, None] >= idx[None, :])).astype(np.float32)
    wide = lambda width: pl.BlockSpec((1, gs, width), lambda b, i: (b, i, 0))
    shape = lambda width, dtype: jax.ShapeDtypeStruct((batch, seq, width), dtype)
    const = pl.BlockSpec((gs, gs), lambda b, i: (0, 0))
    return pl.pallas_call(
        _c_chunk_kernel,
        out_shape=(shape(C_WIDTH, F32), shape(C_WIDTH, BF16), shape(C_WIDTH, BF16), shape(C_WIDTH, BF16),
                   shape(C_HEADS * gs, BF16), shape(128, F32)),
        grid=(batch, seq // gs),
        in_specs=[wide(3 * C_WIDTH), wide(128),
                  pl.BlockSpec((1, 2 * C_HEADS, gs), lambda b, i: (b, 0, i)),
                  const, const, const],
        out_specs=(wide(C_WIDTH),) * 4 + (wide(C_HEADS * gs), wide(128)),
        compiler_params=_params("parallel", "parallel"),
        name="c_chunks",
    )(qkv, bg, bgt, jnp.asarray(tri, BF16), jnp.asarray(tri.T, BF16), jnp.asarray(same, BF16))


def _c_scan_kernel(u_ref, w_ref, qg_ref, kg_ref, attn_ref, gc_ref, o_ref, state_ref, vnew_ref):
    @pl.when(pl.program_id(1) == 0)
    def _():
        state_ref[...] = jnp.zeros_like(state_ref)

    cs = C_CHUNK
    dk = C_HEAD_DIM
    gs = C_GROUP * cs
    vnew_ref[...] = jnp.zeros_like(vnew_ref)
    for c in range(C_GROUP):
        rs = slice(c * cs, (c + 1) * cs)
        decay_last = jnp.exp(gc_ref[0, (c + 1) * cs - 1:(c + 1) * cs, :])
        for h in range(C_HEADS):
            sl = slice(h * dk, (h + 1) * dk)
            state = state_ref[h]
            s16 = state.astype(BF16)
            v_new = u_ref[0, rs, sl] - _dot(w_ref[0, rs, sl], s16)
            v16 = v_new.astype(BF16)
            vnew_ref[h, rs, :] = v16
            o_ref[0, rs, sl] = _dot(qg_ref[0, rs, sl], s16) + _dot(attn_ref[0, rs, h * gs:(h + 1) * gs], vnew_ref[h])
            state_ref[h] = state * decay_last[:, C_HEADS + h:C_HEADS + h + 1] + _dot_tn(kg_ref[0, rs, sl], v16)


def _c_scan(u, w, qg, kg, attn, gc, batch, seq):
    gs = C_GROUP * C_CHUNK
    wide = lambda width: pl.BlockSpec((1, gs, width), lambda b, c: (b, c, 0))
    return pl.pallas_call(
        _c_scan_kernel,
        out_shape=jax.ShapeDtypeStruct((batch, seq, C_WIDTH), F32),
        grid=(batch, seq // gs),
        in_specs=[wide(C_WIDTH)] * 4 + [wide(C_HEADS * gs), wide(128)],
        out_specs=wide(C_WIDTH),
        scratch_shapes=[pltpu.VMEM((C_HEADS, C_HEAD_DIM, C_HEAD_DIM), F32),
                        pltpu.VMEM((C_HEADS, gs, C_HEAD_DIM), BF16)],
        compiler_params=_params("parallel", "arbitrary"),
        name="c_scan",
    )(u, w, qg, kg, attn, gc)


def _c_out_kernel(o_ref, z_ref, g_ref, w_ref, x_ref, out_ref):
    dk = C_HEAD_DIM
    parts = []
    for h in range(C_HEADS):
        sl = slice(h * dk, (h + 1) * dk)
        z = z_ref[:, sl].astype(F32)
        parts.append((_rms(o_ref[:, sl], g_ref[...]) * (z * jax.nn.sigmoid(z))).astype(BF16))
    out_ref[...] = x_ref[...] + _dot(jnp.concatenate(parts, axis=-1), w_ref[...])


def _c_out(o, proj, out_gain, w_out, x):
    m, d = x.shape
    z_block = (3 * C_WIDTH) // C_WIDTH
    row = lambda width: pl.BlockSpec((ROW_TILE, width), lambda i: (i, 0))
    return pl.pallas_call(
        _c_out_kernel,
        out_shape=jax.ShapeDtypeStruct((m, d), F32),
        grid=(m // ROW_TILE,),
        in_specs=[row(C_WIDTH),
                  pl.BlockSpec((ROW_TILE, C_WIDTH), lambda i: (i, z_block)),
                  pl.BlockSpec((1, C_HEAD_DIM), lambda i: (0, 0)),
                  pl.BlockSpec((C_WIDTH, d), lambda i: (0, 0)),
                  row(d)],
        out_specs=row(d),
        compiler_params=_params("parallel"),
        name="c_out",
    )(o, proj, out_gain.reshape(1, -1), w_out, x)


def _mixer_c(x, norm1, w_in, conv_w, a_log, dt_bias, out_gain, w_out, batch, seq):
    main = 4 * C_WIDTH
    proj = _norm_matmul(x, norm1, w_in[:, :main].astype(BF16), BF16, 512)
    w_small = jnp.pad(w_in[:, main:], ((0, 0), (0, 128 - 2 * C_HEADS))).astype(BF16)
    small = _norm_matmul(x, norm1, w_small, F32, 128)
    qkv, bg = _c_conv(proj.reshape(batch, seq, main), small.reshape(batch, seq, 128), conv_w, a_log, dt_bias,
                      batch, seq)
    bgt = bg[:, :, :2 * C_HEADS].transpose(0, 2, 1)
    u, w, qg, kg, attn, gc = _c_chunks(qkv, bg, bgt, batch, seq)
    o = _c_scan(u, w, qg, kg, attn, gc, batch, seq)
    return _c_out(o.reshape(batch * seq, C_WIDTH), proj, out_gain, w_out.astype(BF16), x)


def kernel(x, rel_bias, l0_norm1, l0_a_w_in, l0_a_q_gain, l0_a_k_gain, l0_a_w_out, l0_norm2, l0_ffn_w_gate, l0_ffn_w_up, l0_ffn_w_down, l1_norm1, l1_b_w_in, l1_b_q_gain, l1_b_k_gain, l1_b_cmp_pos, l1_b_cmp_w1, l1_b_cmp_w2, l1_b_w_out, l1_norm2, l1_ffn_w_gate, l1_ffn_w_up, l1_ffn_w_down, l2_norm1, l2_c_w_in, l2_c_conv_w, l2_c_a_log, l2_c_dt_bias, l2_c_out_gain, l2_c_w_out, l2_norm2, l2_ffn_w_gate, l2_ffn_w_up, l2_ffn_w_down, l3_norm1, l3_a_w_in, l3_a_q_gain, l3_a_k_gain, l3_a_w_out, l3_norm2, l3_ffn_w_gate, l3_ffn_w_up, l3_ffn_w_down):
    batch, seq, d = x.shape
    h = x.reshape(batch * seq, d)

    def ffn(h, norm2, w_gate, w_up, w_down):
        return _ffn(h, norm2, w_gate.astype(BF16), w_up.astype(BF16), w_down.astype(BF16))

    h = _mixer_a(h, rel_bias, l0_norm1, l0_a_w_in, l0_a_q_gain, l0_a_k_gain, l0_a_w_out, batch, seq)
    h = ffn(h, l0_norm2, l0_ffn_w_gate, l0_ffn_w_up, l0_ffn_w_down)
    h = _mixer_b(h, rel_bias, l1_norm1, l1_b_w_in, l1_b_q_gain, l1_b_k_gain, l1_b_cmp_pos, l1_b_cmp_w1,
                 l1_b_cmp_w2, l1_b_w_out, batch, seq)
    h = ffn(h, l1_norm2, l1_ffn_w_gate, l1_ffn_w_up, l1_ffn_w_down)
    h = _mixer_c(h, l2_norm1, l2_c_w_in, l2_c_conv_w, l2_c_a_log, l2_c_dt_bias, l2_c_out_gain, l2_c_w_out,
                 batch, seq)
    h = ffn(h, l2_norm2, l2_ffn_w_gate, l2_ffn_w_up, l2_ffn_w_down)
    h = _mixer_a(h, rel_bias, l3_norm1, l3_a_w_in, l3_a_q_gain, l3_a_k_gain, l3_a_w_out, batch, seq)
    h = ffn(h, l3_norm2, l3_ffn_w_gate, l3_ffn_w_up, l3_ffn_w_down)
    return h.reshape(batch, seq, d)
```

```python
import functools
import math

import numpy as np
import jax
import jax.numpy as jnp
from jax import lax
from jax.experimental import pallas as pl
from jax.experimental.pallas import tpu as pltpu

D_MODEL = 1024
RMS_EPS = 1e-6
NEG_INF = -1e30
TINY = 1e-30
FORCE_SCORE = 1e9

N_BUCKETS = 32
REL_MAX_DISTANCE = 2048
N_HEADS = 16

A_GROUPS = ((128, 1), (512, 4), (2048, 16))
A_HEAD_DIM = 64
A_Q_BLOCK = 128
A_PROJ_TILE = 512
A_STAT_WIDTH = 256
A_BLOCKS_PER_STEP = 2

B_KV_HEADS = 4
B_GROUP = 4
B_HEAD_DIM = 64
B_CMP_LEN = 32
B_CMP_STRIDE = 16
B_CMP_HIDDEN = 256
B_SEL_BLOCK = 64
B_TOP_N = 16
B_WINDOW = 512
B_TILE = 128
B_SWEEP = 256
B_PROJ_WIDTH = 3072

C_HEADS = 8
C_HEAD_DIM = 128
C_WIDTH = C_HEADS * C_HEAD_DIM
C_CONV = 4
C_CHUNK = 64
C_GROUP = 4

FFN_HIDDEN = 2816
FFN_TILE = 1024

ROW_TILE = 512
VMEM_LIMIT = 48 * 1024 * 1024

LOG2E = math.log2(math.e)

F32 = jnp.float32
BF16 = jnp.bfloat16

NT_DIMS = (((1,), (1,)), ((), ()))
TN_DIMS = (((0,), (0,)), ((), ()))


def _params(*semantics):
    return pltpu.CompilerParams(dimension_semantics=semantics, vmem_limit_bytes=VMEM_LIMIT)


def _dot(a, b):
    return jnp.dot(a, b, preferred_element_type=F32)


def _dot_nt(a, b):
    return lax.dot_general(a, b, NT_DIMS, preferred_element_type=F32)


def _dot_tn(a, b):
    return lax.dot_general(a, b, TN_DIMS, preferred_element_type=F32)


def _rms(x, gain):
    return x * lax.rsqrt(jnp.mean(x * x, axis=-1, keepdims=True) + RMS_EPS) * gain


def _bucket_thresholds():
    d = np.arange(1 << 15)
    max_exact = N_BUCKETS // 2
    d_f = np.maximum(d, 1).astype(np.float32)
    large = max_exact + (np.log(d_f / np.float32(max_exact)) / np.float32(math.log(REL_MAX_DISTANCE / max_exact))
                         * np.float32(N_BUCKETS - max_exact)).astype(np.int32)
    bucket = np.where(d < max_exact, d, np.minimum(large, N_BUCKETS - 1))
    return [int(np.argmax(bucket >= k)) if np.any(bucket >= k) else int(1 << 30) for k in range(N_BUCKETS)]


_THRESHOLDS = _bucket_thresholds()


def _bias_tile_kernel(tbl_ref, o_ref, *, base, tile_step, row_step, col_step, dmax, dil):
    h = pl.program_id(0)
    t = pl.program_id(1)
    shape = o_ref.shape[2:]
    i = lax.broadcasted_iota(jnp.int32, shape, 0)
    j = lax.broadcasted_iota(jnp.int32, shape, 1)
    dist = base + tile_step * t + row_step * i + col_step * j
    d = dist * dil
    val = jnp.full(shape, tbl_ref[0, h], F32)
    for k in range(1, N_BUCKETS):
        val = jnp.where(d >= _THRESHOLDS[k], tbl_ref[k, h], val)
    valid = (dist >= 0) & (dist <= dmax)
    o_ref[0, 0] = jnp.where(valid, val * LOG2E, NEG_INF).astype(o_ref.dtype)


def _bias_tiles(rel_bias, n_tiles, rows, cols, *, base, tile_step, row_step, col_step, dmax, dil=1, dtype=F32):
    kern = functools.partial(_bias_tile_kernel, base=base, tile_step=tile_step, row_step=row_step,
                             col_step=col_step, dmax=dmax, dil=dil)
    return pl.pallas_call(
        kern,
        out_shape=jax.ShapeDtypeStruct((N_HEADS, n_tiles, rows, cols), dtype),
        grid=(N_HEADS, n_tiles),
        in_specs=[pl.BlockSpec(memory_space=pltpu.SMEM)],
        out_specs=pl.BlockSpec((1, 1, rows, cols), lambda h, t: (h, t, 0, 0)),
        compiler_params=_params("parallel", "parallel"),
        name="bias_tiles",
    )(rel_bias)


def _resident(shape):
    return pl.BlockSpec(shape, lambda i: (0,) * len(shape), pipeline_mode=pl.Buffered(1))


def _norm_matmul_kernel(x_ref, g_ref, w_ref, o_ref, *, tn):
    h = _rms(x_ref[...], g_ref[...]).astype(BF16)
    for j in range(w_ref.shape[1] // tn):
        o_ref[:, j * tn:(j + 1) * tn] = _dot(h, w_ref[:, j * tn:(j + 1) * tn]).astype(o_ref.dtype)


def _norm_matmul(x, gain, w, out_dtype, tn):
    m, d = x.shape
    n = w.shape[1]
    return pl.pallas_call(
        functools.partial(_norm_matmul_kernel, tn=tn),
        out_shape=jax.ShapeDtypeStruct((m, n), out_dtype),
        grid=(m // ROW_TILE,),
        in_specs=[pl.BlockSpec((ROW_TILE, d), lambda i: (i, 0)),
                  _resident((1, d)),
                  _resident((d, n))],
        out_specs=pl.BlockSpec((ROW_TILE, n), lambda i: (i, 0)),
        compiler_params=_params("parallel"),
        name="norm_matmul",
    )(x, gain.reshape(1, d), w)


def _matmul_residual_kernel(a_ref, w_ref, x_ref, o_ref):
    o_ref[...] = x_ref[...] + _dot(a_ref[...], w_ref[...])


def _matmul_residual(a, w, x):
    m, k = a.shape
    d = w.shape[1]
    return pl.pallas_call(
        _matmul_residual_kernel,
        out_shape=jax.ShapeDtypeStruct((m, d), F32),
        grid=(m // ROW_TILE,),
        in_specs=[pl.BlockSpec((ROW_TILE, k), lambda i: (i, 0)),
                  pl.BlockSpec((k, d), lambda i: (0, 0)),
                  pl.BlockSpec((ROW_TILE, d), lambda i: (i, 0))],
        out_specs=pl.BlockSpec((ROW_TILE, d), lambda i: (i, 0)),
        compiler_params=_params("parallel"),
        name="matmul_residual",
    )(a, w, x)


def _ffn_kernel(x_ref, g_ref, wg_ref, wu_ref, wd_ref, o_ref):
    x = x_ref[...]
    h = _rms(x, g_ref[...]).astype(BF16)
    hidden = wg_ref.shape[1]
    acc = x
    for lo in range(0, hidden, FFN_TILE):
        hi = min(lo + FFN_TILE, hidden)
        a = _dot(h, wg_ref[:, lo:hi])
        b = _dot(h, wu_ref[:, lo:hi])
        acc = acc + _dot((a * jax.nn.sigmoid(a) * b).astype(BF16), wd_ref[lo:hi, :])
    o_ref[...] = acc


def _ffn(x, gain, w_gate, w_up, w_down):
    m, d = x.shape
    hidden = w_gate.shape[1]
    return pl.pallas_call(
        _ffn_kernel,
        out_shape=jax.ShapeDtypeStruct((m, d), F32),
        grid=(m // ROW_TILE,),
        in_specs=[pl.BlockSpec((ROW_TILE, d), lambda i: (i, 0)),
                  _resident((1, d)),
                  _resident((d, hidden)), _resident((d, hidden)), _resident((hidden, d))],
        out_specs=pl.BlockSpec((ROW_TILE, d), lambda i: (i, 0)),
        compiler_params=_params("parallel"),
        name="ffn",
    )(x, gain.reshape(1, d), w_gate, w_up, w_down)


def _a_proj_kernel(x_ref, g_ref, w_ref, qg_ref, kg_ref, o_ref, h_ref, x_scr, *, dil):
    rows = ROW_TILE // dil
    xn = _rms(x_ref[...], g_ref[...])
    if dil == 1:
        h_ref[...] = xn.astype(BF16)
    else:
        slabs = xn.shape[1] // 128
        for c in range(slabs):
            x_scr[c] = xn[:, c * 128:(c + 1) * 128]
        for r in range(dil):
            picked = [x_scr[c, pl.ds(r, rows, stride=dil), :] for c in range(slabs)]
            h_ref[r * rows:(r + 1) * rows, :] = jnp.concatenate(picked, axis=1).astype(BF16)
    h = h_ref[...]
    width = w_ref.shape[1]
    hd = width // 3
    low = lax.broadcasted_iota(jnp.int32, (ROW_TILE, 128), 1) < A_HEAD_DIM
    for j in range(width // A_PROJ_TILE):
        res = _dot(h, w_ref[:, j * A_PROJ_TILE:(j + 1) * A_PROJ_TILE])
        kind = (j * A_PROJ_TILE) // hd
        if kind < 2:
            parts = []
            for c in range(A_PROJ_TILE // 128):
                y = res[:, c * 128:(c + 1) * 128]
                sq = y * y
                tot = jnp.sum(sq, axis=-1, keepdims=True)
                lo = jnp.sum(jnp.where(low, sq, 0.0), axis=-1, keepdims=True)
                ss = jnp.where(low, lo, tot - lo)
                parts.append(y * lax.rsqrt(ss * (1.0 / A_HEAD_DIM) + RMS_EPS))
            res = jnp.concatenate(parts, axis=1) * (qg_ref if kind == 0 else kg_ref)[...]
        res = res.astype(BF16)
        for r in range(dil):
            off = r * width + j * A_PROJ_TILE
            o_ref[:, off:off + A_PROJ_TILE] = res[r * rows:(r + 1) * rows]


def _a_proj(x, gain, w, q_gain, k_gain, dil):
    m, d = x.shape
    width = w.shape[1]
    reps = A_PROJ_TILE // A_HEAD_DIM
    qg = jnp.tile(q_gain * (A_HEAD_DIM ** -0.5 * LOG2E), reps).reshape(1, A_PROJ_TILE)
    kg = jnp.tile(k_gain, reps).reshape(1, A_PROJ_TILE)
    return pl.pallas_call(
        functools.partial(_a_proj_kernel, dil=dil),
        out_shape=jax.ShapeDtypeStruct((m // dil, dil * width), BF16),
        grid=(m // ROW_TILE,),
        in_specs=[pl.BlockSpec((ROW_TILE, d), lambda i: (i, 0)),
                  pl.BlockSpec((1, d), lambda i: (0, 0)),
                  pl.BlockSpec((d, width), lambda i: (0, 0)),
                  pl.BlockSpec((1, A_PROJ_TILE), lambda i: (0, 0)),
                  pl.BlockSpec((1, A_PROJ_TILE), lambda i: (0, 0))],
        out_specs=pl.BlockSpec((ROW_TILE // dil, dil * width), lambda i: (i, 0)),
        scratch_shapes=[pltpu.VMEM((ROW_TILE, d), BF16), pltpu.VMEM((d // 128, ROW_TILE, 128), F32)],
        compiler_params=_params("parallel"),
        name="a_proj",
    )(x, gain.reshape(1, d), w, qg, kg)


def _a_attn_kernel(q_ref, kp_ref, kc_ref, vp_ref, vc_ref, bias_ref, o_ref, stat_ref):
    first = (pl.program_id(2) == 0).astype(jnp.int32)
    nq = A_Q_BLOCK
    lane = lax.broadcasted_iota(jnp.int32, (nq, 128), 1)
    low = lane < A_HEAD_DIM
    ones = jnp.ones((2 * nq, 128), BF16)
    for sub in range(A_BLOCKS_PER_STEP):
        rows = slice(sub * nq, (sub + 1) * nq)
        max_tile = jnp.zeros((nq, 128), F32)
        den_tile = jnp.ones((nq, 128), F32)
        for pair in range(N_HEADS // 2):
            sl = slice(pair * 128, (pair + 1) * 128)
            q = q_ref[0, rows, sl]
            zero = jnp.zeros_like(q)
            qq = jnp.concatenate([jnp.where(low, q, zero), jnp.where(low, zero, q)], axis=0)
            if sub == 0:
                kk = jnp.concatenate([kp_ref[0, :, sl], kc_ref[0, :nq, sl]], axis=0)
                vv = jnp.concatenate([vp_ref[0, :, sl], vc_ref[0, :nq, sl]], axis=0)
                base = 2 * pair + N_HEADS * first
            else:
                kk = kc_ref[0, :, sl]
                vv = vc_ref[0, :, sl]
                base = 2 * pair
            s = _dot_nt(qq, kk) + jnp.concatenate([bias_ref[base], bias_ref[base + 1]], axis=0)
            m = jnp.max(s, axis=-1, keepdims=True)
            acc = _dot(jnp.exp2(s - m).astype(BF16), jnp.concatenate([vv, ones], axis=1))
            o_ref[0, rows, sl] = jnp.where(low, acc[:nq, :128], acc[nq:, :128]).astype(o_ref.dtype)
            first_head = lane == 2 * pair
            second_head = lane == 2 * pair + 1
            max_tile = jnp.where(first_head, m[:nq], jnp.where(second_head, m[nq:], max_tile))
            den_tile = jnp.where(first_head, acc[:nq, 128:], jnp.where(second_head, acc[nq:, 128:], den_tile))
        stat_ref[0, rows, :128] = max_tile
        stat_ref[0, rows, 128:] = den_tile


def _a_attention(proj, bias, dil, batch, seq):
    length = seq // dil
    nblk = length // A_Q_BLOCK
    hd = N_HEADS * A_HEAD_DIM
    pv = proj.reshape(batch, length, dil * 3 * hd)

    per = A_BLOCKS_PER_STEP
    step_rows = per * A_Q_BLOCK
    assert nblk % per == 0

    def spec(off, prev):
        if prev:
            return pl.BlockSpec((1, A_Q_BLOCK, hd), lambda b, r, i: (b, jnp.maximum(per * i - 1, 0), r * 3 + off))
        return pl.BlockSpec((1, step_rows, hd), lambda b, r, i: (b, i, r * 3 + off))

    o, stats = pl.pallas_call(
        _a_attn_kernel,
        out_shape=(jax.ShapeDtypeStruct((batch, length, dil * hd), BF16),
                   jax.ShapeDtypeStruct((batch, length, dil * A_STAT_WIDTH), F32)),
        grid=(batch, dil, nblk // per),
        in_specs=[spec(0, False), spec(1, True), spec(1, False), spec(2, True), spec(2, False),
                  pl.BlockSpec((2 * N_HEADS, A_Q_BLOCK, 2 * A_Q_BLOCK), lambda b, r, i: (0, 0, 0))],
        out_specs=(pl.BlockSpec((1, step_rows, hd), lambda b, r, i: (b, i, r)),
                   pl.BlockSpec((1, step_rows, A_STAT_WIDTH), lambda b, r, i: (b, i, r))),
        compiler_params=_params("parallel", "parallel", "arbitrary"),
        name="a_attention",
    )(pv, pv, pv, pv, pv, bias)
    return o.reshape(batch * length, dil * hd), stats.reshape(batch * length, dil * A_STAT_WIDTH)


def _a_out_kernel(o0_ref, o1_ref, o2_ref, s0_ref, s1_ref, s2_ref, e_ref, w_ref, x_ref, out_ref, o_scr, s_scr):
    hd = N_HEADS * A_HEAD_DIM
    sw = A_STAT_WIDTH
    for g, (o_ref, s_ref) in enumerate(((o0_ref, s0_ref), (o1_ref, s1_ref), (o2_ref, s2_ref))):
        dil = A_GROUPS[g][1]
        rows = ROW_TILE // dil
        for r in range(dil):
            dst = pl.ds(r, rows, stride=dil) if dil > 1 else slice(None)
            s_scr[g, 0, dst, :] = s_ref[:, r * sw:r * sw + 128]
            s_scr[g, 1, dst, :] = s_ref[:, r * sw + 128:(r + 1) * sw]
            for c in range(hd // 128):
                o_scr[g, c, dst, :] = o_ref[:, r * hd + c * 128:r * hd + (c + 1) * 128].astype(F32)
    groups = range(len(A_GROUPS))
    top = functools.reduce(jnp.maximum, [s_scr[g, 0] for g in groups])
    es = [jnp.exp2(s_scr[g, 0] - top) for g in groups]
    inv = 1.0 / sum(es[g] * s_scr[g, 1] for g in groups)
    expand = e_ref[...]
    acc = None
    for g in groups:
        o_g = jnp.concatenate([o_scr[g, c] for c in range(hd // 128)], axis=1)
        term = _dot((es[g] * inv).astype(BF16), expand) * o_g
        acc = term if acc is None else acc + term
    out_ref[...] = x_ref[...] + _dot(acc.astype(BF16), w_ref[...])


def _a_out(outs, stats, w_out, x):
    m, d = x.shape
    hd = N_HEADS * A_HEAD_DIM
    expand = np.zeros((128, hd), np.float32)
    for h in range(N_HEADS):
        expand[h, h * A_HEAD_DIM:(h + 1) * A_HEAD_DIM] = 1.0
    grouped = lambda width: [pl.BlockSpec((ROW_TILE // dil, dil * width), lambda i: (i, 0)) for _, dil in A_GROUPS]
    return pl.pallas_call(
        _a_out_kernel,
        out_shape=jax.ShapeDtypeStruct((m, d), F32),
        grid=(m // ROW_TILE,),
        in_specs=grouped(hd) + grouped(A_STAT_WIDTH) + [
            pl.BlockSpec((128, hd), lambda i: (0, 0)),
            pl.BlockSpec((hd, d), lambda i: (0, 0)),
            pl.BlockSpec((ROW_TILE, d), lambda i: (i, 0))],
        out_specs=pl.BlockSpec((ROW_TILE, d), lambda i: (i, 0)),
        scratch_shapes=[pltpu.VMEM((len(A_GROUPS), hd // 128, ROW_TILE, 128), F32),
                        pltpu.VMEM((len(A_GROUPS), 2, ROW_TILE, 128), F32)],
        compiler_params=_params("parallel"),
        name="a_out",
    )(*outs, *stats, jnp.asarray(expand, BF16), w_out, x)


def _mixer_a(x, rel_bias, norm1, w_in, q_gain, k_gain, w_out, batch, seq):
    w_in = w_in.astype(BF16)
    group_width = 3 * N_HEADS * A_HEAD_DIM
    outs, stats = [], []
    for gi, (window, dil) in enumerate(A_GROUPS):
        steps = window // dil
        assert steps == A_Q_BLOCK and (seq // dil) % A_Q_BLOCK == 0 and seq % ROW_TILE == 0
        bias = _bias_tiles(rel_bias, 1, A_Q_BLOCK, 2 * A_Q_BLOCK, base=A_Q_BLOCK, tile_step=0, row_step=1,
                           col_step=-1, dmax=steps, dil=dil)[:, 0]
        bias = jnp.concatenate([bias, bias.at[:, :, :A_Q_BLOCK].set(NEG_INF)], axis=0)
        proj = _a_proj(x, norm1, w_in[:, gi * group_width:(gi + 1) * group_width], q_gain[gi], k_gain[gi], dil)
        o, stat = _a_attention(proj, bias, dil, batch, seq)
        outs.append(o)
        stats.append(stat)
    return _a_out(outs, stats, w_out.astype(BF16), x)


def _b_prep_kernel(p_ref, qg_ref, kg_ref, q_ref, ck_ref, cv_ref, sk_ref, sv_ref, wk_ref, wv_ref, gate_ref):
    dh = B_HEAD_DIM
    qg = qg_ref[...] * (dh ** -0.5 * LOG2E)
    for h in range(N_HEADS):
        q_ref[0, h] = _rms(p_ref[0, :, h * dh:(h + 1) * dh].astype(F32), qg).astype(BF16)
    base = N_HEADS * dh
    outs = ((ck_ref, None), (cv_ref, None), (sk_ref, 1), (sv_ref, None), (wk_ref, 2), (wv_ref, None))
    for idx, (ref, gain_row) in enumerate(outs):
        for n in range(B_KV_HEADS):
            off = base + (idx * B_KV_HEADS + n) * dh
            t = p_ref[0, :, off:off + dh]
            if gain_row is not None:
                t = _rms(t.astype(F32), kg_ref[gain_row:gain_row + 1, :]).astype(BF16)
            ref[0, n] = t
    gate_off = base + 6 * B_KV_HEADS * dh
    gate = jax.nn.sigmoid(p_ref[0, :, gate_off:gate_off + 3 * N_HEADS].astype(F32))
    per = 3 * B_GROUP
    for n in range(B_KV_HEADS):
        gate_ref[0, n] = gate[:, n * per:(n + 1) * per]


def _b_prep(proj, q_gain, k_gain, batch, seq):
    ts = 256
    dh = B_HEAD_DIM
    kv_shape = jax.ShapeDtypeStruct((batch, B_KV_HEADS, seq, dh), BF16)
    kv_spec = pl.BlockSpec((1, B_KV_HEADS, ts, dh), lambda b, i: (b, 0, i, 0))
    return pl.pallas_call(
        _b_prep_kernel,
        out_shape=(jax.ShapeDtypeStruct((batch, N_HEADS, seq, dh), BF16),) + (kv_shape,) * 6
        + (jax.ShapeDtypeStruct((batch, B_KV_HEADS, seq, 3 * B_GROUP), F32),),
        grid=(batch, seq // ts),
        in_specs=[pl.BlockSpec((1, ts, B_PROJ_WIDTH), lambda b, i: (b, i, 0)),
                  pl.BlockSpec((1, dh), lambda b, i: (0, 0)),
                  pl.BlockSpec((3, dh), lambda b, i: (0, 0))],
        out_specs=(pl.BlockSpec((1, N_HEADS, ts, dh), lambda b, i: (b, 0, i, 0)),) + (kv_spec,) * 6
        + (pl.BlockSpec((1, B_KV_HEADS, ts, 3 * B_GROUP), lambda b, i: (b, 0, i, 0)),),
        compiler_params=_params("parallel", "parallel"),
        name="b_prep",
    )(proj, q_gain.reshape(1, dh), k_gain)


def _b_compress_kernel(tk_ref, tv_ref, pos_ref, w1_ref, w2_ref, kg_ref, kc_ref, vc_ref):
    half = (B_CMP_LEN // 2) * B_HEAD_DIM
    for kv, (t_ref, out_ref) in enumerate(((tk_ref, kc_ref), (tv_ref, vc_ref))):
        t = t_ref[0, 0].astype(F32)
        top = (t + pos_ref[kv, 0:1, :]).astype(BF16)
        bot = (t + pos_ref[kv, 1:2, :]).astype(BF16)
        a1 = _dot(top, w1_ref[kv, :half, :])
        a2 = _dot(bot, w1_ref[kv, half:, :])
        hidden = a1 + pltpu.roll(a2, a2.shape[0] - 1, 0)
        out = _dot(jax.nn.gelu(hidden).astype(BF16), w2_ref[kv])
        if kv == 0:
            out = _rms(out, kg_ref[...])
        out_ref[0, 0] = out.astype(out_ref.dtype)


def _b_compress(ck, cv, cmp_pos, cmp_w1, cmp_w2, k_gain0, batch, seq):
    rows = seq // B_CMP_STRIDE
    half = (B_CMP_LEN // 2) * B_HEAD_DIM
    tk = ck.reshape(batch, B_KV_HEADS, rows, half)
    tv = cv.reshape(batch, B_KV_HEADS, rows, half)
    pos = cmp_pos.reshape(2, 2, half)
    t_spec = pl.BlockSpec((1, 1, rows, half), lambda b, n: (b, n, 0, 0))
    o_spec = pl.BlockSpec((1, 1, rows, B_HEAD_DIM), lambda b, n: (b, n, 0, 0))
    shape = jax.ShapeDtypeStruct((batch, B_KV_HEADS, rows, B_HEAD_DIM), BF16)
    return pl.pallas_call(
        _b_compress_kernel,
        out_shape=(shape, shape),
        grid=(batch, B_KV_HEADS),
        in_specs=[t_spec, t_spec,
                  pl.BlockSpec((2, 2, half), lambda b, n: (0, 0, 0)),
                  pl.BlockSpec((2, 2 * half, B_CMP_HIDDEN), lambda b, n: (0, 0, 0)),
                  pl.BlockSpec((2, B_CMP_HIDDEN, B_HEAD_DIM), lambda b, n: (0, 0, 0)),
                  pl.BlockSpec((1, B_HEAD_DIM), lambda b, n: (0, 0))],
        out_specs=(o_spec, o_spec),
        compiler_params=_params("parallel", "parallel"),
        name="b_compress",
    )(tk, tv, pos, cmp_w1.astype(BF16), cmp_w2.astype(BF16), k_gain0.reshape(1, -1))


def _b_cmp_attn_kernel(qt_ref, kc_ref, vct_ref, bias_ref, c2s_ref, oc_ref, sel_ref, imp_ref, *, top_n):
    tq = B_SWEEP
    n_sel = imp_ref.shape[0]
    qt = jnp.concatenate([qt_ref[0, g] for g in range(B_GROUP)], axis=1)
    bias = jnp.concatenate([bias_ref[g, 0] for g in range(B_GROUP)], axis=1)
    s = _dot(kc_ref[0, 0], qt) + bias
    valid = bias > 0.5 * NEG_INF
    m = jnp.max(s, axis=0, keepdims=True)
    e = jnp.where(valid, jnp.exp2(s - m), 0.0)
    z = jnp.maximum(jnp.sum(e, axis=0, keepdims=True), TINY)
    p = e * (1.0 / z)
    oct = _dot(vct_ref[0, 0], p.astype(BF16))
    for g in range(B_GROUP):
        oc_ref[0, g] = oct[:, g * tq:(g + 1) * tq]

    p_sum = p[:, 0:tq] + p[:, tq:2 * tq] + p[:, 2 * tq:3 * tq] + p[:, 3 * tq:4 * tq]
    hi = p_sum.astype(BF16)
    lo = (p_sum - hi.astype(F32)).astype(BF16)
    c2s = c2s_ref[...]
    imp = _dot(c2s, hi) + _dot(c2s, lo)

    t = pl.program_id(2) * tq + lax.broadcasted_iota(jnp.int32, (n_sel, tq), 1)
    blk = lax.broadcasted_iota(jnp.int32, (n_sel, tq), 0)
    cur = t // B_SEL_BLOCK
    forced = (blk == 0) | (blk == cur) | (blk == cur - 1)
    imp = jnp.where(forced, FORCE_SCORE, jnp.where(blk * B_SEL_BLOCK <= t, imp, NEG_INF))
    imp_ref[...] = imp

    def count(i, rank):
        row = imp_ref[pl.ds(i, 1), :]
        ahead = jnp.where(row > imp, 1.0, jnp.where(row == imp, jnp.where(blk > i, 1.0, 0.0), 0.0))
        return rank + ahead

    n_live = jnp.minimum(n_sel, (pl.program_id(2) + 1) * (tq // B_SEL_BLOCK))
    rank = lax.fori_loop(0, n_live, count, jnp.zeros((n_sel, tq), F32))
    sel_ref[0, 0] = jnp.where(rank < top_n, 0.0, NEG_INF).astype(sel_ref.dtype)


def _b_cmp_attn(qt, kc, vc, bias_c, batch, seq):
    n_sel = seq // B_SEL_BLOCK
    n_cmp_pad = seq // B_CMP_STRIDE
    n_cmp = (seq - B_CMP_LEN) // B_CMP_STRIDE + 1
    c = np.arange(n_cmp_pad)[None, :] * B_CMP_STRIDE
    j = np.arange(n_sel)[:, None] * B_SEL_BLOCK
    c2s = ((c < j + B_SEL_BLOCK) & (c + B_CMP_LEN > j) & (np.arange(n_cmp_pad)[None, :] < n_cmp)).astype(np.float32)
    kern = functools.partial(_b_cmp_attn_kernel, top_n=min(B_TOP_N, n_sel))
    return pl.pallas_call(
        kern,
        out_shape=(jax.ShapeDtypeStruct((batch, N_HEADS, B_HEAD_DIM, seq), F32),
                   jax.ShapeDtypeStruct((batch, B_KV_HEADS, n_sel, seq), BF16)),
        grid=(batch, B_KV_HEADS, seq // B_SWEEP),
        in_specs=[pl.BlockSpec((1, B_GROUP, B_HEAD_DIM, B_SWEEP), lambda b, n, i: (b, n, 0, i)),
                  pl.BlockSpec((1, 1, n_cmp_pad, B_HEAD_DIM), lambda b, n, i: (b, n, 0, 0)),
                  pl.BlockSpec((1, 1, B_HEAD_DIM, n_cmp_pad), lambda b, n, i: (b, n, 0, 0)),
                  pl.BlockSpec((B_GROUP, 1, n_cmp_pad, B_SWEEP), lambda b, n, i: (n, i, 0, 0)),
                  pl.BlockSpec((n_sel, n_cmp_pad), lambda b, n, i: (0, 0))],
        out_specs=(pl.BlockSpec((1, B_GROUP, B_HEAD_DIM, B_SWEEP), lambda b, n, i: (b, n, 0, i)),
                   pl.BlockSpec((1, 1, n_sel, B_SWEEP), lambda b, n, i: (b, n, 0, i))),
        scratch_shapes=[pltpu.VMEM((n_sel, B_SWEEP), F32)],
        compiler_params=_params("parallel", "parallel", "arbitrary"),
        name="b_cmp_attn",
    )(qt, kc, vc.transpose(0, 1, 3, 2), bias_c, jnp.asarray(c2s, BF16))


def _b_sparse_kernel(qt_ref, ka_ref, vs_ref, wk_ref, vw_ref, sel_ref, bs_ref, bw_ref, oc_ref, gate_ref,
                     o_ref, acc_ref, sa_ref, sb_ref, *, delta_max, win_tiles):
    tq = B_SWEEP
    dh = B_HEAD_DIM
    cols = B_GROUP * tq
    qi = pl.program_id(2)
    n_tiles = ka_ref.shape[2] // tq
    qt = jnp.concatenate([qt_ref[0, g] for g in range(B_GROUP)], axis=1)
    q_aug = jnp.concatenate([qt, jnp.concatenate([sel_ref[0, 0]] * B_GROUP, axis=1)], axis=0)

    def tile_start(kt):
        return pl.multiple_of(jnp.clip(kt, 0, n_tiles - 1) * tq, tq)

    def normalised(acc):
        return acc[:dh] * (1.0 / acc[dh:dh + 1])

    def sel_bias(kt):
        d = jnp.clip(qi - kt, -1, delta_max) + 1
        return jnp.concatenate([bs_ref[g, d] for g in range(B_GROUP)], axis=1)

    def sel_scores(kt):
        return _dot(ka_ref[0, 0, pl.ds(tile_start(kt), tq), :], q_aug).astype(BF16) + sel_bias(kt)

    def consume(s_buf, kt, m_old):
        s = s_buf[...]
        m_new = jnp.maximum(m_old, jnp.max(s, axis=0, keepdims=True).astype(F32))
        alpha = jnp.exp2(m_old - m_new)
        p = jnp.exp2(s - m_new.astype(BF16))
        acc_ref[...] = alpha * acc_ref[...] + _dot(vs_ref[0, 0, :, pl.ds(tile_start(kt), tq)], p)
        return m_new

    acc_ref[...] = jnp.zeros(acc_ref.shape, F32)
    sa_ref[...] = sel_scores(0)

    def pair(j, m):
        kt = 2 * j
        sb_ref[...] = sel_scores(kt + 1)
        m = consume(sa_ref, kt, m)
        sa_ref[...] = sel_scores(kt + 2)
        return consume(sb_ref, kt + 1, m)

    lax.fori_loop(0, (qi + 2) // 2, pair, jnp.full((1, cols), NEG_INF, F32))
    o_s = normalised(acc_ref[...])

    tiles = []
    for u in range(win_tiles):
        kt = qi - (win_tiles - 1) + u
        d = jnp.where(kt >= 0, qi - kt, -1) + 1
        bias = jnp.concatenate([bw_ref[g, d] for g in range(B_GROUP)], axis=1)
        tiles.append((_dot(wk_ref[0, 0, pl.ds(tile_start(kt), tq), :], qt).astype(BF16) + bias, kt))
    m = None
    for s, _ in tiles:
        tile_max = jnp.max(s, axis=0, keepdims=True)
        m = tile_max if m is None else jnp.maximum(m, tile_max)
    acc = None
    for s, kt in tiles:
        pv = _dot(vw_ref[0, 0, :, pl.ds(tile_start(kt), tq)], jnp.exp2(s - m))
        acc = pv if acc is None else acc + pv
    o_w = normalised(acc)

    gate = gate_ref[0, 0]
    merged = []
    for g in range(B_GROUP):
        cs = slice(g * tq, (g + 1) * tq)
        merged.append(gate[3 * g:3 * g + 1] * oc_ref[0, g] + gate[3 * g + 1:3 * g + 2] * o_s[:, cs]
                      + gate[3 * g + 2:3 * g + 3] * o_w[:, cs])
    for pair in range(B_GROUP // 2):
        both = jnp.concatenate([merged[2 * pair], merged[2 * pair + 1]], axis=0)
        o_ref[0, :, pair * 2 * dh:(pair + 1) * 2 * dh] = both.T.astype(o_ref.dtype)


def _b_sparse(qt, ka, vs, wk, vw, sel, bias_s, bias_w, oc, gate, batch, seq):
    n_sel = seq // B_SEL_BLOCK
    dh = B_HEAD_DIM
    n_ds = bias_s.shape[1]
    n_dw = bias_w.shape[1]
    vrows = vs.shape[2]
    kern = functools.partial(_b_sparse_kernel, delta_max=n_ds - 2, win_tiles=n_dw - 1)
    whole = lambda rows, width: pl.BlockSpec((1, 1, rows, width), lambda b, n, i: (b, n, 0, 0))
    return pl.pallas_call(
        kern,
        out_shape=jax.ShapeDtypeStruct((batch, seq, N_HEADS * dh), BF16),
        grid=(batch, B_KV_HEADS, seq // B_SWEEP),
        in_specs=[pl.BlockSpec((1, B_GROUP, dh, B_SWEEP), lambda b, n, i: (b, n, 0, i)),
                  whole(seq, dh + n_sel), whole(vrows, seq), whole(seq, dh), whole(vrows, seq),
                  pl.BlockSpec((1, 1, n_sel, B_SWEEP), lambda b, n, i: (b, n, 0, i)),
                  pl.BlockSpec((B_GROUP, n_ds, B_SWEEP, B_SWEEP), lambda b, n, i: (n, 0, 0, 0)),
                  pl.BlockSpec((B_GROUP, n_dw, B_SWEEP, B_SWEEP), lambda b, n, i: (n, 0, 0, 0)),
                  pl.BlockSpec((1, B_GROUP, dh, B_SWEEP), lambda b, n, i: (b, n, 0, i)),
                  pl.BlockSpec((1, 1, 4 * B_GROUP, B_SWEEP), lambda b, n, i: (b, n, 0, i))],
        out_specs=pl.BlockSpec((1, B_SWEEP, B_GROUP * dh), lambda b, n, i: (b, i, n)),
        scratch_shapes=[pltpu.VMEM((vrows, B_GROUP * B_SWEEP), F32),
                        pltpu.VMEM((B_SWEEP, B_GROUP * B_SWEEP), BF16),
                        pltpu.VMEM((B_SWEEP, B_GROUP * B_SWEEP), BF16)],
        compiler_params=_params("parallel", "parallel", "arbitrary"),
        name="b_sparse",
    )(qt, ka, vs, wk, vw, sel, bias_s, bias_w, oc, gate)


def _mixer_b(x, rel_bias, norm1, w_in, q_gain, k_gain, cmp_pos, cmp_w1, cmp_w2, w_out, batch, seq):
    w_pad = jnp.pad(w_in, ((0, 0), (0, B_PROJ_WIDTH - w_in.shape[1]))).astype(BF16)
    proj = _norm_matmul(x, norm1, w_pad, BF16, 512).reshape(batch, seq, B_PROJ_WIDTH)
    q, ck, cv, sk, sv, wk, wv, gate = _b_prep(proj, q_gain, k_gain, batch, seq)
    kc, vc = _b_compress(ck, cv, cmp_pos, cmp_w1, cmp_w2, k_gain[0], batch, seq)
    bias_c = _bias_tiles(rel_bias, seq // B_SWEEP, seq // B_CMP_STRIDE, B_SWEEP, base=1 - B_CMP_LEN,
                         tile_step=B_SWEEP, row_step=-B_CMP_STRIDE, col_step=1, dmax=1 << 30)
    qt = q.transpose(0, 1, 3, 2)
    oc, sel = _b_cmp_attn(qt, kc, vc, bias_c, batch, seq)
    delta_max = min(seq // B_SWEEP - 1, -(-(_THRESHOLDS[-1] + B_SWEEP - 1) // B_SWEEP))
    bias_s = _bias_tiles(rel_bias, delta_max + 2, B_SWEEP, B_SWEEP, base=-B_SWEEP, tile_step=B_SWEEP,
                         row_step=-1, col_step=1, dmax=1 << 30, dtype=BF16)
    win_tiles = (B_WINDOW - 1 + B_SWEEP - 1) // B_SWEEP + 1
    bias_w = _bias_tiles(rel_bias, win_tiles + 1, B_SWEEP, B_SWEEP, base=-B_SWEEP, tile_step=B_SWEEP,
                         row_step=-1, col_step=1, dmax=B_WINDOW - 1, dtype=BF16)
    n_sel = seq // B_SEL_BLOCK
    onehot = (np.arange(seq)[:, None] // B_SEL_BLOCK == np.arange(n_sel)[None, :]).astype(np.float32)
    ka = jnp.concatenate([sk, jnp.broadcast_to(jnp.asarray(onehot, BF16), sk.shape[:2] + onehot.shape)], axis=-1)
    ones = jnp.ones(sv.shape[:2] + (16, seq), BF16)
    vs = jnp.concatenate([sv.transpose(0, 1, 3, 2), ones], axis=2)
    vw = jnp.concatenate([wv.transpose(0, 1, 3, 2), ones], axis=2)
    gate_t = jnp.pad(gate.transpose(0, 1, 3, 2), ((0, 0), (0, 0), (0, B_GROUP), (0, 0)))
    o = _b_sparse(qt, ka, vs, wk, vw, sel, bias_s, bias_w, oc, gate_t, batch, seq)
    return _matmul_residual(o.reshape(batch * seq, -1), w_out.astype(BF16), x)


def _c_conv_kernel(cur_ref, halo_ref, w_ref, sm_ref, alog_ref, dtb_ref, qkv_ref, bg_ref):
    ts = cur_ref.shape[1]
    keep = jnp.where(pl.program_id(1) == 0, 0.0, 1.0)
    dk = C_HEAD_DIM
    for c in range(3 * C_HEADS):
        sl = slice(c * dk, (c + 1) * dk)
        xe = jnp.concatenate([halo_ref[0, :, sl].astype(F32) * keep, cur_ref[0, :, sl].astype(F32)], axis=0)
        y = None
        for j in range(C_CONV):
            off = 8 - (C_CONV - 1) + j
            term = w_ref[j:j + 1, sl] * xe[off:off + ts]
            y = term if y is None else y + term
        y = y * jax.nn.sigmoid(y)
        if c < 2 * C_HEADS:
            y = y * lax.rsqrt(jnp.sum(y * y, axis=-1, keepdims=True) + RMS_EPS)
        if c < C_HEADS:
            y = y * (dk ** -0.5)
        qkv_ref[0, :, sl] = y
    sm = sm_ref[0]
    a = sm + dtb_ref[...]
    softplus = jnp.maximum(a, 0.0) + jnp.log1p(jnp.exp(-jnp.abs(a)))
    g = -jnp.exp(alog_ref[...]) * softplus
    lane = lax.broadcasted_iota(jnp.int32, sm.shape, 1)
    bg_ref[0] = jnp.where(lane < C_HEADS, jax.nn.sigmoid(sm), g)


def _c_conv(proj, small, conv_w, a_log, dt_bias, batch, seq):
    ts = 256
    width = 3 * C_WIDTH
    pad = lambda v: jnp.pad(v, (C_HEADS, 128 - 2 * C_HEADS)).reshape(1, 128)
    return pl.pallas_call(
        _c_conv_kernel,
        out_shape=(jax.ShapeDtypeStruct((batch, seq, width), F32),
                   jax.ShapeDtypeStruct((batch, seq, 128), F32)),
        grid=(batch, seq // ts),
        in_specs=[pl.BlockSpec((1, ts, width), lambda b, i: (b, i, 0)),
                  pl.BlockSpec((1, 8, width), lambda b, i: (b, jnp.maximum(i * (ts // 8) - 1, 0), 0)),
                  pl.BlockSpec((C_CONV, width), lambda b, i: (0, 0)),
                  pl.BlockSpec((1, ts, 128), lambda b, i: (b, i, 0)),
                  pl.BlockSpec((1, 128), lambda b, i: (0, 0)),
                  pl.BlockSpec((1, 128), lambda b, i: (0, 0))],
        out_specs=(pl.BlockSpec((1, ts, width), lambda b, i: (b, i, 0)),
                   pl.BlockSpec((1, ts, 128), lambda b, i: (b, i, 0))),
        compiler_params=_params("parallel", "arbitrary"),
        name="c_conv",
    )(proj, proj, conv_w, small, pad(a_log), pad(dt_bias))


def _sum3(x, fn):
    hi = x.astype(BF16)
    r = x - hi.astype(F32)
    mid = r.astype(BF16)
    lo = (r - mid.astype(F32)).astype(BF16)
    return fn(hi) + (fn(mid) + fn(lo))


def _c_chunk_kernel(qkv_ref, bg_ref, bgt_ref, tri_ref, trit_ref, blk_ref, u_ref, w_ref, qg_ref, kg_ref, attn_ref,
                    gc_ref):
    cs = C_CHUNK
    dk = C_HEAD_DIM
    gs = C_GROUP * cs
    row = lax.broadcasted_iota(jnp.int32, (gs, gs), 0)
    col = lax.broadcasted_iota(jnp.int32, (gs, gs), 1)
    same = (row // cs) == (col // cs)
    causal = same & (row >= col)
    strict = same & (row > col)
    eye = jnp.where(row == col, 1.0, 0.0)

    bgc = bg_ref[0]
    tri = tri_ref[...]
    gcum_col = _sum3(bgc, lambda p: _dot(tri, p))
    glast_col = _sum3(bgc, lambda p: _dot(blk_ref[...], p))
    gcum_row = _sum3(bgt_ref[0], lambda p: _dot(p, trit_ref[...]))
    gc_ref[0] = gcum_col
    t_mats, powers = [], []
    for h in range(C_HEADS):
        gc = gcum_col[:, C_HEADS + h:C_HEADS + h + 1]
        gr = gcum_row[C_HEADS + h:C_HEADS + h + 1, :]
        q = qkv_ref[0, :, h * dk:(h + 1) * dk]
        k = qkv_ref[0, :, C_WIDTH + h * dk:C_WIDTH + (h + 1) * dk]
        decay = jnp.exp(jnp.where(causal, gc - gr, NEG_INF))
        k16 = k.astype(BF16)
        low = jnp.where(strict, _dot_nt((k * bgc[:, h:h + 1]).astype(BF16), k16) * decay, 0.0)
        t_mats.append(eye - low)
        powers.append(low.astype(BF16))
        attn = jnp.where(causal, _dot_nt(q.astype(BF16), k16), 0.0) * decay
        attn_ref[0, :, h * gs:(h + 1) * gs] = attn.astype(attn_ref.dtype)
        qg_ref[0, :, h * dk:(h + 1) * dk] = (q * jnp.exp(gc)).astype(qg_ref.dtype)
        glast = glast_col[:, C_HEADS + h:C_HEADS + h + 1]
        kg_ref[0, :, h * dk:(h + 1) * dk] = (k * jnp.exp(glast - gc)).astype(kg_ref.dtype)
    for _ in range(int(math.log2(cs)) - 1):
        powers = [_dot(p, p).astype(BF16) for p in powers]
        t_mats = [t + _dot(t.astype(BF16), p) for t, p in zip(t_mats, powers)]
    for h in range(C_HEADS):
        gc = gcum_col[:, C_HEADS + h:C_HEADS + h + 1]
        beta = bgc[:, h:h + 1]
        k = qkv_ref[0, :, C_WIDTH + h * dk:C_WIDTH + (h + 1) * dk]
        v = qkv_ref[0, :, 2 * C_WIDTH + h * dk:2 * C_WIDTH + (h + 1) * dk]
        t16 = t_mats[h].astype(BF16)
        u_ref[0, :, h * dk:(h + 1) * dk] = _dot(t16, (v * beta).astype(BF16))
        w_ref[0, :, h * dk:(h + 1) * dk] = _dot(t16, (k * beta * jnp.exp(gc)).astype(BF16)).astype(w_ref.dtype)


def _c_chunks(qkv, bg, bgt, batch, seq):
    gs = C_GROUP * C_CHUNK
    idx = np.arange(gs)
    same = (idx[:, None] // C_CHUNK) == (idx[None, :] // C_CHUNK)
    tri = (same & (idx[:, None] >= idx[None, :])).astype(np.float32)
    wide = lambda width: pl.BlockSpec((1, gs, width), lambda b, i: (b, i, 0))
    shape = lambda width, dtype: jax.ShapeDtypeStruct((batch, seq, width), dtype)
    const = pl.BlockSpec((gs, gs), lambda b, i: (0, 0))
    return pl.pallas_call(
        _c_chunk_kernel,
        out_shape=(shape(C_WIDTH, F32), shape(C_WIDTH, BF16), shape(C_WIDTH, BF16), shape(C_WIDTH, BF16),
                   shape(C_HEADS * gs, BF16), shape(128, F32)),
        grid=(batch, seq // gs),
        in_specs=[wide(3 * C_WIDTH), wide(128),
                  pl.BlockSpec((1, 2 * C_HEADS, gs), lambda b, i: (b, 0, i)),
                  const, const, const],
        out_specs=(wide(C_WIDTH),) * 4 + (wide(C_HEADS * gs), wide(128)),
        compiler_params=_params("parallel", "parallel"),
        name="c_chunks",
    )(qkv, bg, bgt, jnp.asarray(tri, BF16), jnp.asarray(tri.T, BF16), jnp.asarray(same, BF16))


def _c_scan_kernel(u_ref, w_ref, qg_ref, kg_ref, attn_ref, gc_ref, o_ref, state_ref, vnew_ref):
    @pl.when(pl.program_id(1) == 0)
    def _():
        state_ref[...] = jnp.zeros_like(state_ref)

    cs = C_CHUNK
    dk = C_HEAD_DIM
    gs = C_GROUP * cs
    vnew_ref[...] = jnp.zeros_like(vnew_ref)
    for c in range(C_GROUP):
        rs = slice(c * cs, (c + 1) * cs)
        decay_last = jnp.exp(gc_ref[0, (c + 1) * cs - 1:(c + 1) * cs, :])
        for h in range(C_HEADS):
            sl = slice(h * dk, (h + 1) * dk)
            state = state_ref[h]
            s16 = state.astype(BF16)
            v_new = u_ref[0, rs, sl] - _dot(w_ref[0, rs, sl], s16)
            v16 = v_new.astype(BF16)
            vnew_ref[h, rs, :] = v16
            o_ref[0, rs, sl] = _dot(qg_ref[0, rs, sl], s16) + _dot(attn_ref[0, rs, h * gs:(h + 1) * gs], vnew_ref[h])
            state_ref[h] = state * decay_last[:, C_HEADS + h:C_HEADS + h + 1] + _dot_tn(kg_ref[0, rs, sl], v16)


def _c_scan(u, w, qg, kg, attn, gc, batch, seq):
    gs = C_GROUP * C_CHUNK
    wide = lambda width: pl.BlockSpec((1, gs, width), lambda b, c: (b, c, 0))
    return pl.pallas_call(
        _c_scan_kernel,
        out_shape=jax.ShapeDtypeStruct((batch, seq, C_WIDTH), F32),
        grid=(batch, seq // gs),
        in_specs=[wide(C_WIDTH)] * 4 + [wide(C_HEADS * gs), wide(128)],
        out_specs=wide(C_WIDTH),
        scratch_shapes=[pltpu.VMEM((C_HEADS, C_HEAD_DIM, C_HEAD_DIM), F32),
                        pltpu.VMEM((C_HEADS, gs, C_HEAD_DIM), BF16)],
        compiler_params=_params("parallel", "arbitrary"),
        name="c_scan",
    )(u, w, qg, kg, attn, gc)


def _c_out_kernel(o_ref, z_ref, g_ref, w_ref, x_ref, out_ref):
    dk = C_HEAD_DIM
    parts = []
    for h in range(C_HEADS):
        sl = slice(h * dk, (h + 1) * dk)
        z = z_ref[:, sl].astype(F32)
        parts.append((_rms(o_ref[:, sl], g_ref[...]) * (z * jax.nn.sigmoid(z))).astype(BF16))
    out_ref[...] = x_ref[...] + _dot(jnp.concatenate(parts, axis=-1), w_ref[...])


def _c_out(o, proj, out_gain, w_out, x):
    m, d = x.shape
    z_block = (3 * C_WIDTH) // C_WIDTH
    row = lambda width: pl.BlockSpec((ROW_TILE, width), lambda i: (i, 0))
    return pl.pallas_call(
        _c_out_kernel,
        out_shape=jax.ShapeDtypeStruct((m, d), F32),
        grid=(m // ROW_TILE,),
        in_specs=[row(C_WIDTH),
                  pl.BlockSpec((ROW_TILE, C_WIDTH), lambda i: (i, z_block)),
                  pl.BlockSpec((1, C_HEAD_DIM), lambda i: (0, 0)),
                  pl.BlockSpec((C_WIDTH, d), lambda i: (0, 0)),
                  row(d)],
        out_specs=row(d),
        compiler_params=_params("parallel"),
        name="c_out",
    )(o, proj, out_gain.reshape(1, -1), w_out, x)


def _mixer_c(x, norm1, w_in, conv_w, a_log, dt_bias, out_gain, w_out, batch, seq):
    main = 4 * C_WIDTH
    proj = _norm_matmul(x, norm1, w_in[:, :main].astype(BF16), BF16, 512)
    w_small = jnp.pad(w_in[:, main:], ((0, 0), (0, 128 - 2 * C_HEADS))).astype(BF16)
    small = _norm_matmul(x, norm1, w_small, F32, 128)
    qkv, bg = _c_conv(proj.reshape(batch, seq, main), small.reshape(batch, seq, 128), conv_w, a_log, dt_bias,
                      batch, seq)
    bgt = bg[:, :, :2 * C_HEADS].transpose(0, 2, 1)
    u, w, qg, kg, attn, gc = _c_chunks(qkv, bg, bgt, batch, seq)
    o = _c_scan(u, w, qg, kg, attn, gc, batch, seq)
    return _c_out(o.reshape(batch * seq, C_WIDTH), proj, out_gain, w_out.astype(BF16), x)


def kernel(x, rel_bias, l0_norm1, l0_a_w_in, l0_a_q_gain, l0_a_k_gain, l0_a_w_out, l0_norm2, l0_ffn_w_gate, l0_ffn_w_up, l0_ffn_w_down, l1_norm1, l1_b_w_in, l1_b_q_gain, l1_b_k_gain, l1_b_cmp_pos, l1_b_cmp_w1, l1_b_cmp_w2, l1_b_w_out, l1_norm2, l1_ffn_w_gate, l1_ffn_w_up, l1_ffn_w_down, l2_norm1, l2_c_w_in, l2_c_conv_w, l2_c_a_log, l2_c_dt_bias, l2_c_out_gain, l2_c_w_out, l2_norm2, l2_ffn_w_gate, l2_ffn_w_up, l2_ffn_w_down, l3_norm1, l3_a_w_in, l3_a_q_gain, l3_a_k_gain, l3_a_w_out, l3_norm2, l3_ffn_w_gate, l3_ffn_w_up, l3_ffn_w_down):
    batch, seq, d = x.shape
    h = x.reshape(batch * seq, d)

    def ffn(h, norm2, w_gate, w_up, w_down):
        return _ffn(h, norm2, w_gate.astype(BF16), w_up.astype(BF16), w_down.astype(BF16))

    h = _mixer_a(h, rel_bias, l0_norm1, l0_a_w_in, l0_a_q_gain, l0_a_k_gain, l0_a_w_out, batch, seq)
    h = ffn(h, l0_norm2, l0_ffn_w_gate, l0_ffn_w_up, l0_ffn_w_down)
    h = _mixer_b(h, rel_bias, l1_norm1, l1_b_w_in, l1_b_q_gain, l1_b_k_gain, l1_b_cmp_pos, l1_b_cmp_w1,
                 l1_b_cmp_w2, l1_b_w_out, batch, seq)
    h = ffn(h, l1_norm2, l1_ffn_w_gate, l1_ffn_w_up, l1_ffn_w_down)
    h = _mixer_c(h, l2_norm1, l2_c_w_in, l2_c_conv_w, l2_c_a_log, l2_c_dt_bias, l2_c_out_gain, l2_c_w_out,
                 batch, seq)
    h = ffn(h, l2_norm2, l2_ffn_w_gate, l2_ffn_w_up, l2_ffn_w_down)
    h = _mixer_a(h, rel_bias, l3_norm1, l3_a_w_in, l3_a_q_gain, l3_a_k_gain, l3_a_w_out, batch, seq)
    h = ffn(h, l3_norm2, l3_ffn_w_gate, l3_ffn_w_up, l3_ffn_w_down)
    return h.reshape(batch, seq, d)
```

```python
import functools
import math

import numpy as np
import jax
import jax.numpy as jnp
from jax import lax
from jax.experimental import pallas as pl
from jax.experimental.pallas import tpu as pltpu

D_MODEL = 1024
RMS_EPS = 1e-6
NEG_INF = -1e30
TINY = 1e-30
FORCE_SCORE = 1e9

N_BUCKETS = 32
REL_MAX_DISTANCE = 2048
N_HEADS = 16

A_GROUPS = ((128, 1), (512, 4), (2048, 16))
A_HEAD_DIM = 64
A_Q_BLOCK = 128
A_PROJ_TILE = 512
A_STAT_WIDTH = 256
A_BLOCKS_PER_STEP = 2

B_KV_HEADS = 4
B_GROUP = 4
B_HEAD_DIM = 64
B_CMP_LEN = 32
B_CMP_STRIDE = 16
B_CMP_HIDDEN = 256
B_SEL_BLOCK = 64
B_TOP_N = 16
B_WINDOW = 512
B_TILE = 128
B_SWEEP = 256
B_PROJ_WIDTH = 3072
B_GATE_ROWS = 16

C_HEADS = 8
C_HEAD_DIM = 128
C_WIDTH = C_HEADS * C_HEAD_DIM
C_CONV = 4
C_CHUNK = 64
C_GROUP = 4

FFN_HIDDEN = 2816
FFN_TILE = 1024

ROW_TILE = 512
VMEM_LIMIT = 48 * 1024 * 1024

LOG2E = math.log2(math.e)

F32 = jnp.float32
BF16 = jnp.bfloat16

NT_DIMS = (((1,), (1,)), ((), ()))
TN_DIMS = (((0,), (0,)), ((), ()))


def _params(*semantics):
    return pltpu.CompilerParams(dimension_semantics=semantics, vmem_limit_bytes=VMEM_LIMIT)


def _dot(a, b):
    return jnp.dot(a, b, preferred_element_type=F32)


def _dot_nt(a, b):
    return lax.dot_general(a, b, NT_DIMS, preferred_element_type=F32)


def _dot_tn(a, b):
    return lax.dot_general(a, b, TN_DIMS, preferred_element_type=F32)


def _rms(x, gain):
    return x * lax.rsqrt(jnp.mean(x * x, axis=-1, keepdims=True) + RMS_EPS) * gain


def _bucket_thresholds():
    d = np.arange(1 << 15)
    max_exact = N_BUCKETS // 2
    d_f = np.maximum(d, 1).astype(np.float32)
    large = max_exact + (np.log(d_f / np.float32(max_exact)) / np.float32(math.log(REL_MAX_DISTANCE / max_exact))
                         * np.float32(N_BUCKETS - max_exact)).astype(np.int32)
    bucket = np.where(d < max_exact, d, np.minimum(large, N_BUCKETS - 1))
    return [int(np.argmax(bucket >= k)) if np.any(bucket >= k) else int(1 << 30) for k in range(N_BUCKETS)]


_THRESHOLDS = _bucket_thresholds()


def _bias_tile_kernel(tbl_ref, o_ref, *, base, tile_step, row_step, col_step, dmax, dil):
    h = pl.program_id(0)
    t = pl.program_id(1)
    shape = o_ref.shape[2:]
    i = lax.broadcasted_iota(jnp.int32, shape, 0)
    j = lax.broadcasted_iota(jnp.int32, shape, 1)
    dist = base + tile_step * t + row_step * i + col_step * j
    d = dist * dil
    val = jnp.full(shape, tbl_ref[0, h], F32)
    for k in range(1, N_BUCKETS):
        val = jnp.where(d >= _THRESHOLDS[k], tbl_ref[k, h], val)
    valid = (dist >= 0) & (dist <= dmax)
    o_ref[0, 0] = jnp.where(valid, val * LOG2E, NEG_INF).astype(o_ref.dtype)


def _bias_tiles(rel_bias, n_tiles, rows, cols, *, base, tile_step, row_step, col_step, dmax, dil=1, dtype=F32):
    kern = functools.partial(_bias_tile_kernel, base=base, tile_step=tile_step, row_step=row_step,
                             col_step=col_step, dmax=dmax, dil=dil)
    return pl.pallas_call(
        kern,
        out_shape=jax.ShapeDtypeStruct((N_HEADS, n_tiles, rows, cols), dtype),
        grid=(N_HEADS, n_tiles),
        in_specs=[pl.BlockSpec(memory_space=pltpu.SMEM)],
        out_specs=pl.BlockSpec((1, 1, rows, cols), lambda h, t: (h, t, 0, 0)),
        compiler_params=_params("parallel", "parallel"),
        name="bias_tiles",
    )(rel_bias)


def _resident(shape):
    return pl.BlockSpec(shape, lambda i: (0,) * len(shape), pipeline_mode=pl.Buffered(1))


def _norm_matmul_kernel(x_ref, g_ref, w_ref, o_ref, *, tn):
    h = _rms(x_ref[...], g_ref[...]).astype(BF16)
    for j in range(w_ref.shape[1] // tn):
        o_ref[:, j * tn:(j + 1) * tn] = _dot(h, w_ref[:, j * tn:(j + 1) * tn]).astype(o_ref.dtype)


def _norm_matmul(x, gain, w, out_dtype, tn):
    m, d = x.shape
    n = w.shape[1]
    return pl.pallas_call(
        functools.partial(_norm_matmul_kernel, tn=tn),
        out_shape=jax.ShapeDtypeStruct((m, n), out_dtype),
        grid=(m // ROW_TILE,),
        in_specs=[pl.BlockSpec((ROW_TILE, d), lambda i: (i, 0)),
                  _resident((1, d)),
                  _resident((d, n))],
        out_specs=pl.BlockSpec((ROW_TILE, n), lambda i: (i, 0)),
        compiler_params=_params("parallel"),
        name="norm_matmul",
    )(x, gain.reshape(1, d), w)


def _matmul_residual_kernel(a_ref, w_ref, x_ref, o_ref):
    o_ref[...] = x_ref[...] + _dot(a_ref[...], w_ref[...])


def _matmul_residual(a, w, x):
    m, k = a.shape
    d = w.shape[1]
    return pl.pallas_call(
        _matmul_residual_kernel,
        out_shape=jax.ShapeDtypeStruct((m, d), F32),
        grid=(m // ROW_TILE,),
        in_specs=[pl.BlockSpec((ROW_TILE, k), lambda i: (i, 0)),
                  pl.BlockSpec((k, d), lambda i: (0, 0)),
                  pl.BlockSpec((ROW_TILE, d), lambda i: (i, 0))],
        out_specs=pl.BlockSpec((ROW_TILE, d), lambda i: (i, 0)),
        compiler_params=_params("parallel"),
        name="matmul_residual",
    )(a, w, x)


def _ffn_kernel(x_ref, g_ref, wg_ref, wu_ref, wd_ref, o_ref):
    x = x_ref[...]
    h = _rms(x, g_ref[...]).astype(BF16)
    hidden = wg_ref.shape[1]
    acc = x
    for lo in range(0, hidden, FFN_TILE):
        hi = min(lo + FFN_TILE, hidden)
        a = _dot(h, wg_ref[:, lo:hi])
        b = _dot(h, wu_ref[:, lo:hi])
        acc = acc + _dot((a * jax.nn.sigmoid(a) * b).astype(BF16), wd_ref[lo:hi, :])
    o_ref[...] = acc


def _ffn(x, gain, w_gate, w_up, w_down):
    m, d = x.shape
    hidden = w_gate.shape[1]
    return pl.pallas_call(
        _ffn_kernel,
        out_shape=jax.ShapeDtypeStruct((m, d), F32),
        grid=(m // ROW_TILE,),
        in_specs=[pl.BlockSpec((ROW_TILE, d), lambda i: (i, 0)),
                  _resident((1, d)),
                  _resident((d, hidden)), _resident((d, hidden)), _resident((hidden, d))],
        out_specs=pl.BlockSpec((ROW_TILE, d), lambda i: (i, 0)),
        compiler_params=_params("parallel"),
        name="ffn",
    )(x, gain.reshape(1, d), w_gate, w_up, w_down)


def _a_proj_kernel(x_ref, g_ref, w_ref, qg_ref, kg_ref, o_ref, h_ref, x_scr, *, dil):
    rows = ROW_TILE // dil
    xn = _rms(x_ref[...], g_ref[...])
    if dil == 1:
        h_ref[...] = xn.astype(BF16)
    else:
        slabs = xn.shape[1] // 128
        for c in range(slabs):
            x_scr[c] = xn[:, c * 128:(c + 1) * 128]
        for r in range(dil):
            picked = [x_scr[c, pl.ds(r, rows, stride=dil), :] for c in range(slabs)]
            h_ref[r * rows:(r + 1) * rows, :] = jnp.concatenate(picked, axis=1).astype(BF16)
    h = h_ref[...]
    width = w_ref.shape[1]
    hd = width // 3
    low = lax.broadcasted_iota(jnp.int32, (ROW_TILE, 128), 1) < A_HEAD_DIM
    for j in range(width // A_PROJ_TILE):
        res = _dot(h, w_ref[:, j * A_PROJ_TILE:(j + 1) * A_PROJ_TILE])
        kind = (j * A_PROJ_TILE) // hd
        if kind < 2:
            parts = []
            for c in range(A_PROJ_TILE // 128):
                y = res[:, c * 128:(c + 1) * 128]
                sq = y * y
                tot = jnp.sum(sq, axis=-1, keepdims=True)
                lo = jnp.sum(jnp.where(low, sq, 0.0), axis=-1, keepdims=True)
                ss = jnp.where(low, lo, tot - lo)
                parts.append(y * lax.rsqrt(ss * (1.0 / A_HEAD_DIM) + RMS_EPS))
            res = jnp.concatenate(parts, axis=1) * (qg_ref if kind == 0 else kg_ref)[...]
        res = res.astype(BF16)
        for r in range(dil):
            off = r * width + j * A_PROJ_TILE
            o_ref[:, off:off + A_PROJ_TILE] = res[r * rows:(r + 1) * rows]


def _a_proj(x, gain, w, q_gain, k_gain, dil):
    m, d = x.shape
    width = w.shape[1]
    reps = A_PROJ_TILE // A_HEAD_DIM
    qg = jnp.tile(q_gain * (A_HEAD_DIM ** -0.5 * LOG2E), reps).reshape(1, A_PROJ_TILE)
    kg = jnp.tile(k_gain, reps).reshape(1, A_PROJ_TILE)
    return pl.pallas_call(
        functools.partial(_a_proj_kernel, dil=dil),
        out_shape=jax.ShapeDtypeStruct((m // dil, dil * width), BF16),
        grid=(m // ROW_TILE,),
        in_specs=[pl.BlockSpec((ROW_TILE, d), lambda i: (i, 0)),
                  pl.BlockSpec((1, d), lambda i: (0, 0)),
                  pl.BlockSpec((d, width), lambda i: (0, 0)),
                  pl.BlockSpec((1, A_PROJ_TILE), lambda i: (0, 0)),
                  pl.BlockSpec((1, A_PROJ_TILE), lambda i: (0, 0))],
        out_specs=pl.BlockSpec((ROW_TILE // dil, dil * width), lambda i: (i, 0)),
        scratch_shapes=[pltpu.VMEM((ROW_TILE, d), BF16), pltpu.VMEM((d // 128, ROW_TILE, 128), F32)],
        compiler_params=_params("parallel"),
        name="a_proj",
    )(x, gain.reshape(1, d), w, qg, kg)


def _a_attn_kernel(q_ref, kp_ref, kc_ref, vp_ref, vc_ref, bias_ref, o_ref, stat_ref):
    first = (pl.program_id(2) == 0).astype(jnp.int32)
    nq = A_Q_BLOCK
    lane = lax.broadcasted_iota(jnp.int32, (nq, 128), 1)
    low = lane < A_HEAD_DIM
    ones = jnp.ones((2 * nq, 128), BF16)
    for sub in range(A_BLOCKS_PER_STEP):
        rows = slice(sub * nq, (sub + 1) * nq)
        max_tile = jnp.zeros((nq, 128), F32)
        den_tile = jnp.ones((nq, 128), F32)
        for pair in range(N_HEADS // 2):
            sl = slice(pair * 128, (pair + 1) * 128)
            q = q_ref[0, rows, sl]
            zero = jnp.zeros_like(q)
            qq = jnp.concatenate([jnp.where(low, q, zero), jnp.where(low, zero, q)], axis=0)
            if sub == 0:
                kk = jnp.concatenate([kp_ref[0, :, sl], kc_ref[0, :nq, sl]], axis=0)
                vv = jnp.concatenate([vp_ref[0, :, sl], vc_ref[0, :nq, sl]], axis=0)
                base = 2 * pair + N_HEADS * first
            else:
                kk = kc_ref[0, :, sl]
                vv = vc_ref[0, :, sl]
                base = 2 * pair
            s = _dot_nt(qq, kk) + jnp.concatenate([bias_ref[base], bias_ref[base + 1]], axis=0)
            m = jnp.max(s, axis=-1, keepdims=True)
            acc = _dot(jnp.exp2(s - m).astype(BF16), jnp.concatenate([vv, ones], axis=1))
            o_ref[0, rows, sl] = jnp.where(low, acc[:nq, :128], acc[nq:, :128]).astype(o_ref.dtype)
            first_head = lane == 2 * pair
            second_head = lane == 2 * pair + 1
            max_tile = jnp.where(first_head, m[:nq], jnp.where(second_head, m[nq:], max_tile))
            den_tile = jnp.where(first_head, acc[:nq, 128:], jnp.where(second_head, acc[nq:, 128:], den_tile))
        stat_ref[0, rows, :128] = max_tile
        stat_ref[0, rows, 128:] = den_tile


def _a_attention(proj, bias, dil, batch, seq):
    length = seq // dil
    nblk = length // A_Q_BLOCK
    hd = N_HEADS * A_HEAD_DIM
    pv = proj.reshape(batch, length, dil * 3 * hd)

    per = A_BLOCKS_PER_STEP
    step_rows = per * A_Q_BLOCK
    assert nblk % per == 0

    def spec(off, prev):
        if prev:
            return pl.BlockSpec((1, A_Q_BLOCK, hd), lambda b, r, i: (b, jnp.maximum(per * i - 1, 0), r * 3 + off))
        return pl.BlockSpec((1, step_rows, hd), lambda b, r, i: (b, i, r * 3 + off))

    o, stats = pl.pallas_call(
        _a_attn_kernel,
        out_shape=(jax.ShapeDtypeStruct((batch, length, dil * hd), BF16),
                   jax.ShapeDtypeStruct((batch, length, dil * A_STAT_WIDTH), F32)),
        grid=(batch, dil, nblk // per),
        in_specs=[spec(0, False), spec(1, True), spec(1, False), spec(2, True), spec(2, False),
                  pl.BlockSpec((2 * N_HEADS, A_Q_BLOCK, 2 * A_Q_BLOCK), lambda b, r, i: (0, 0, 0))],
        out_specs=(pl.BlockSpec((1, step_rows, hd), lambda b, r, i: (b, i, r)),
                   pl.BlockSpec((1, step_rows, A_STAT_WIDTH), lambda b, r, i: (b, i, r))),
        compiler_params=_params("parallel", "parallel", "arbitrary"),
        name="a_attention",
    )(pv, pv, pv, pv, pv, bias)
    return o.reshape(batch * length, dil * hd), stats.reshape(batch * length, dil * A_STAT_WIDTH)


def _a_out_kernel(o0_ref, o1_ref, o2_ref, s0_ref, s1_ref, s2_ref, e_ref, w_ref, x_ref, out_ref, o_scr, s_scr):
    hd = N_HEADS * A_HEAD_DIM
    sw = A_STAT_WIDTH
    for g, (o_ref, s_ref) in enumerate(((o0_ref, s0_ref), (o1_ref, s1_ref), (o2_ref, s2_ref))):
        dil = A_GROUPS[g][1]
        rows = ROW_TILE // dil
        for r in range(dil):
            dst = pl.ds(r, rows, stride=dil) if dil > 1 else slice(None)
            s_scr[g, 0, dst, :] = s_ref[:, r * sw:r * sw + 128]
            s_scr[g, 1, dst, :] = s_ref[:, r * sw + 128:(r + 1) * sw]
            for c in range(hd // 128):
                o_scr[g, c, dst, :] = o_ref[:, r * hd + c * 128:r * hd + (c + 1) * 128].astype(F32)
    groups = range(len(A_GROUPS))
    top = functools.reduce(jnp.maximum, [s_scr[g, 0] for g in groups])
    es = [jnp.exp2(s_scr[g, 0] - top) for g in groups]
    inv = 1.0 / sum(es[g] * s_scr[g, 1] for g in groups)
    expand = e_ref[...]
    acc = None
    for g in groups:
        o_g = jnp.concatenate([o_scr[g, c] for c in range(hd // 128)], axis=1)
        term = _dot((es[g] * inv).astype(BF16), expand) * o_g
        acc = term if acc is None else acc + term
    out_ref[...] = x_ref[...] + _dot(acc.astype(BF16), w_ref[...])


def _a_out(outs, stats, w_out, x):
    m, d = x.shape
    hd = N_HEADS * A_HEAD_DIM
    expand = np.zeros((128, hd), np.float32)
    for h in range(N_HEADS):
        expand[h, h * A_HEAD_DIM:(h + 1) * A_HEAD_DIM] = 1.0
    grouped = lambda width: [pl.BlockSpec((ROW_TILE // dil, dil * width), lambda i: (i, 0)) for _, dil in A_GROUPS]
    return pl.pallas_call(
        _a_out_kernel,
        out_shape=jax.ShapeDtypeStruct((m, d), F32),
        grid=(m // ROW_TILE,),
        in_specs=grouped(hd) + grouped(A_STAT_WIDTH) + [
            pl.BlockSpec((128, hd), lambda i: (0, 0)),
            pl.BlockSpec((hd, d), lambda i: (0, 0)),
            pl.BlockSpec((ROW_TILE, d), lambda i: (i, 0))],
        out_specs=pl.BlockSpec((ROW_TILE, d), lambda i: (i, 0)),
        scratch_shapes=[pltpu.VMEM((len(A_GROUPS), hd // 128, ROW_TILE, 128), F32),
                        pltpu.VMEM((len(A_GROUPS), 2, ROW_TILE, 128), F32)],
        compiler_params=_params("parallel"),
        name="a_out",
    )(*outs, *stats, jnp.asarray(expand, BF16), w_out, x)


def _mixer_a(x, rel_bias, norm1, w_in, q_gain, k_gain, w_out, batch, seq):
    w_in = w_in.astype(BF16)
    group_width = 3 * N_HEADS * A_HEAD_DIM
    outs, stats = [], []
    for gi, (window, dil) in enumerate(A_GROUPS):
        steps = window // dil
        assert steps == A_Q_BLOCK and (seq // dil) % A_Q_BLOCK == 0 and seq % ROW_TILE == 0
        bias = _bias_tiles(rel_bias, 1, A_Q_BLOCK, 2 * A_Q_BLOCK, base=A_Q_BLOCK, tile_step=0, row_step=1,
                           col_step=-1, dmax=steps, dil=dil)[:, 0]
        bias = jnp.concatenate([bias, bias.at[:, :, :A_Q_BLOCK].set(NEG_INF)], axis=0)
        proj = _a_proj(x, norm1, w_in[:, gi * group_width:(gi + 1) * group_width], q_gain[gi], k_gain[gi], dil)
        o, stat = _a_attention(proj, bias, dil, batch, seq)
        outs.append(o)
        stats.append(stat)
    return _a_out(outs, stats, w_out.astype(BF16), x)


def _b_prep_kernel(p_ref, qg_ref, kg_ref, qt_ref, ck_ref, cv_ref, ka_ref, vs_ref, wk_ref, vw_ref, gate_ref):
    dh = B_HEAD_DIM
    ts = p_ref.shape[1]
    n_sel = ka_ref.shape[3] - dh
    low = lax.broadcasted_iota(jnp.int32, (ts, 128), 1) < dh

    def slab(c):
        return p_ref[0, :, c * 128:(c + 1) * 128].astype(F32)

    def normed(x, gain):
        sq = x * x
        tot = jnp.sum(sq, axis=-1, keepdims=True)
        first = jnp.sum(jnp.where(low, sq, 0.0), axis=-1, keepdims=True)
        ss = jnp.where(low, first, tot - first)
        return x * lax.rsqrt(ss * (1.0 / dh) + RMS_EPS) * jnp.concatenate([gain, gain], axis=1)

    def halves(x):
        return x[:, :dh], x[:, dh:]

    qg = qg_ref[...] * (dh ** -0.5 * LOG2E)
    for c in range(N_HEADS // 2):
        xt = normed(slab(c), qg).T.astype(BF16)
        qt_ref[0, 2 * c] = xt[:dh]
        qt_ref[0, 2 * c + 1] = xt[dh:]
    base = N_HEADS // 2
    pairs = B_KV_HEADS // 2
    pos = pl.program_id(1) * ts + lax.broadcasted_iota(jnp.int32, (ts, 128), 0)
    lane = lax.broadcasted_iota(jnp.int32, (ts, 128), 1)
    onehot = jnp.where(lane - dh == pos // B_SEL_BLOCK, 1.0, 0.0)
    ones = jnp.ones((vs_ref.shape[2] - dh, ts), BF16)
    for j in range(pairs):
        for ref, off in ((ck_ref, 0), (cv_ref, pairs)):
            a, b = halves(slab(base + off + j))
            ref[0, 2 * j] = a.astype(BF16)
            ref[0, 2 * j + 1] = b.astype(BF16)
        k_sel = normed(slab(base + 2 * pairs + j), kg_ref[1:2, :])
        for n, k in zip((2 * j, 2 * j + 1), (k_sel, pltpu.roll(k_sel, dh, 1))):
            ka_ref[0, n] = jnp.where(low, k, onehot)[:, :dh + n_sel].astype(BF16)
        for n, k in zip((2 * j, 2 * j + 1), halves(normed(slab(base + 4 * pairs + j), kg_ref[2:3, :]))):
            wk_ref[0, n] = k.astype(BF16)
        for ref, off in ((vs_ref, 3 * pairs), (vw_ref, 5 * pairs)):
            xt = slab(base + off + j).T.astype(BF16)
            for n, v in zip((2 * j, 2 * j + 1), (xt[:dh], xt[dh:])):
                ref[0, n, :dh, :] = v
                ref[0, n, dh:, :] = ones
    gate = jax.nn.sigmoid(slab(base + 6 * pairs)).T
    rows = gate_ref.shape[2]
    for n in range(B_KV_HEADS):
        gate_ref[0, n] = gate[n * rows:(n + 1) * rows]


def _b_prep(proj, q_gain, k_gain, batch, seq):
    ts = 256
    dh = B_HEAD_DIM
    n_sel = seq // B_SEL_BLOCK
    assert dh + n_sel <= 128
    rows_shape = lambda width: jax.ShapeDtypeStruct((batch, B_KV_HEADS, seq, width), BF16)
    rows_spec = lambda width: pl.BlockSpec((1, B_KV_HEADS, ts, width), lambda b, i: (b, 0, i, 0))
    cols_shape = lambda heads, height, dtype: jax.ShapeDtypeStruct((batch, heads, height, seq), dtype)
    cols_spec = lambda heads, height: pl.BlockSpec((1, heads, height, ts), lambda b, i: (b, 0, 0, i))
    return pl.pallas_call(
        _b_prep_kernel,
        out_shape=(cols_shape(N_HEADS, dh, BF16), rows_shape(dh), rows_shape(dh), rows_shape(dh + n_sel),
                   cols_shape(B_KV_HEADS, dh + 16, BF16), rows_shape(dh), cols_shape(B_KV_HEADS, dh + 16, BF16),
                   cols_shape(B_KV_HEADS, B_GATE_ROWS, F32)),
        grid=(batch, seq // ts),
        in_specs=[pl.BlockSpec((1, ts, B_PROJ_WIDTH), lambda b, i: (b, i, 0)),
                  pl.BlockSpec((1, dh), lambda b, i: (0, 0)),
                  pl.BlockSpec((3, dh), lambda b, i: (0, 0))],
        out_specs=(cols_spec(N_HEADS, dh), rows_spec(dh), rows_spec(dh), rows_spec(dh + n_sel),
                   cols_spec(B_KV_HEADS, dh + 16), rows_spec(dh), cols_spec(B_KV_HEADS, dh + 16),
                   cols_spec(B_KV_HEADS, B_GATE_ROWS)),
        compiler_params=_params("parallel", "parallel"),
        name="b_prep",
    )(proj, q_gain.reshape(1, dh), k_gain)


def _b_compress_kernel(tk_ref, tv_ref, pos_ref, w1_ref, w2_ref, kg_ref, kc_ref, vc_ref):
    half = (B_CMP_LEN // 2) * B_HEAD_DIM
    for kv, (t_ref, out_ref) in enumerate(((tk_ref, kc_ref), (tv_ref, vc_ref))):
        t = t_ref[0, 0].astype(F32)
        top = (t + pos_ref[kv, 0:1, :]).astype(BF16)
        bot = (t + pos_ref[kv, 1:2, :]).astype(BF16)
        a1 = _dot(top, w1_ref[kv, :half, :])
        a2 = _dot(bot, w1_ref[kv, half:, :])
        hidden = a1 + pltpu.roll(a2, a2.shape[0] - 1, 0)
        out = _dot(jax.nn.gelu(hidden).astype(BF16), w2_ref[kv])
        if kv == 0:
            out = _rms(out, kg_ref[...])
        out_ref[0, 0] = out.astype(out_ref.dtype)


def _b_compress(ck, cv, cmp_pos, cmp_w1, cmp_w2, k_gain0, batch, seq):
    rows = seq // B_CMP_STRIDE
    half = (B_CMP_LEN // 2) * B_HEAD_DIM
    tk = ck.reshape(batch, B_KV_HEADS, rows, half)
    tv = cv.reshape(batch, B_KV_HEADS, rows, half)
    pos = cmp_pos.reshape(2, 2, half)
    t_spec = pl.BlockSpec((1, 1, rows, half), lambda b, n: (b, n, 0, 0))
    o_spec = pl.BlockSpec((1, 1, rows, B_HEAD_DIM), lambda b, n: (b, n, 0, 0))
    shape = jax.ShapeDtypeStruct((batch, B_KV_HEADS, rows, B_HEAD_DIM), BF16)
    return pl.pallas_call(
        _b_compress_kernel,
        out_shape=(shape, shape),
        grid=(batch, B_KV_HEADS),
        in_specs=[t_spec, t_spec,
                  pl.BlockSpec((2, 2, half), lambda b, n: (0, 0, 0)),
                  pl.BlockSpec((2, 2 * half, B_CMP_HIDDEN), lambda b, n: (0, 0, 0)),
                  pl.BlockSpec((2, B_CMP_HIDDEN, B_HEAD_DIM), lambda b, n: (0, 0, 0)),
                  pl.BlockSpec((1, B_HEAD_DIM), lambda b, n: (0, 0))],
        out_specs=(o_spec, o_spec),
        compiler_params=_params("parallel", "parallel"),
        name="b_compress",
    )(tk, tv, pos, cmp_w1.astype(BF16), cmp_w2.astype(BF16), k_gain0.reshape(1, -1))


def _b_cmp_attn_kernel(qt_ref, kc_ref, vct_ref, bias_ref, c2s_ref, oc_ref, sel_ref, imp_ref, *, top_n):
    tq = B_SWEEP
    n_sel = imp_ref.shape[0]
    qt = jnp.concatenate([qt_ref[0, g] for g in range(B_GROUP)], axis=1)
    bias = jnp.concatenate([bias_ref[g, 0] for g in range(B_GROUP)], axis=1)
    s = _dot(kc_ref[0, 0], qt) + bias
    m = jnp.max(s, axis=0, keepdims=True)
    e = jnp.exp2(s - m)
    z = jnp.maximum(jnp.sum(e, axis=0, keepdims=True), TINY)
    pos = pl.program_id(2) * tq + lax.broadcasted_iota(jnp.int32, (1, tq), 1)
    sees_any = jnp.concatenate([pos >= B_CMP_LEN - 1] * B_GROUP, axis=1)
    p = e * jnp.where(sees_any, 1.0 / z, 0.0)
    oct = _dot(vct_ref[0, 0], p.astype(BF16))
    for g in range(B_GROUP):
        oc_ref[0, g] = oct[:, g * tq:(g + 1) * tq]

    p_sum = p[:, 0:tq] + p[:, tq:2 * tq] + p[:, 2 * tq:3 * tq] + p[:, 3 * tq:4 * tq]
    hi = p_sum.astype(BF16)
    lo = (p_sum - hi.astype(F32)).astype(BF16)
    c2s = c2s_ref[...]
    imp = _dot(c2s, hi) + _dot(c2s, lo)

    t = pl.program_id(2) * tq + lax.broadcasted_iota(jnp.int32, (n_sel, tq), 1)
    blk = lax.broadcasted_iota(jnp.int32, (n_sel, tq), 0)
    cur = t // B_SEL_BLOCK
    forced = (blk == 0) | (blk == cur) | (blk == cur - 1)
    imp = jnp.where(forced, FORCE_SCORE, jnp.where(blk * B_SEL_BLOCK <= t, imp, NEG_INF))
    imp_ref[...] = imp

    def count(i, rank):
        row = imp_ref[pl.ds(i, 1), :]
        ahead = jnp.where(row > imp, 1.0, jnp.where(row == imp, jnp.where(blk > i, 1.0, 0.0), 0.0))
        return rank + ahead

    n_live = jnp.minimum(n_sel, (pl.program_id(2) + 1) * (tq // B_SEL_BLOCK))
    rank = lax.fori_loop(0, n_live, count, jnp.zeros((n_sel, tq), F32))
    sel_ref[0, 0] = jnp.where(rank < top_n, 0.0, NEG_INF).astype(sel_ref.dtype)


def _b_cmp_attn(qt, kc, vc, bias_c, batch, seq):
    n_sel = seq // B_SEL_BLOCK
    n_cmp_pad = seq // B_CMP_STRIDE
    n_cmp = (seq - B_CMP_LEN) // B_CMP_STRIDE + 1
    c = np.arange(n_cmp_pad)[None, :] * B_CMP_STRIDE
    j = np.arange(n_sel)[:, None] * B_SEL_BLOCK
    c2s = ((c < j + B_SEL_BLOCK) & (c + B_CMP_LEN > j) & (np.arange(n_cmp_pad)[None, :] < n_cmp)).astype(np.float32)
    kern = functools.partial(_b_cmp_attn_kernel, top_n=min(B_TOP_N, n_sel))
    return pl.pallas_call(
        kern,
        out_shape=(jax.ShapeDtypeStruct((batch, N_HEADS, B_HEAD_DIM, seq), F32),
                   jax.ShapeDtypeStruct((batch, B_KV_HEADS, n_sel, seq), BF16)),
        grid=(batch, B_KV_HEADS, seq // B_SWEEP),
        in_specs=[pl.BlockSpec((1, B_GROUP, B_HEAD_DIM, B_SWEEP), lambda b, n, i: (b, n, 0, i)),
                  pl.BlockSpec((1, 1, n_cmp_pad, B_HEAD_DIM), lambda b, n, i: (b, n, 0, 0)),
                  pl.BlockSpec((1, 1, B_HEAD_DIM, n_cmp_pad), lambda b, n, i: (b, n, 0, 0)),
                  pl.BlockSpec((B_GROUP, 1, n_cmp_pad, B_SWEEP), lambda b, n, i: (n, i, 0, 0)),
                  pl.BlockSpec((n_sel, n_cmp_pad), lambda b, n, i: (0, 0))],
        out_specs=(pl.BlockSpec((1, B_GROUP, B_HEAD_DIM, B_SWEEP), lambda b, n, i: (b, n, 0, i)),
                   pl.BlockSpec((1, 1, n_sel, B_SWEEP), lambda b, n, i: (b, n, 0, i))),
        scratch_shapes=[pltpu.VMEM((n_sel, B_SWEEP), F32)],
        compiler_params=_params("parallel", "parallel", "arbitrary"),
        name="b_cmp_attn",
    )(qt, kc, vc.transpose(0, 1, 3, 2), bias_c, jnp.asarray(c2s, BF16))


def _b_sparse_kernel(qt_ref, ka_ref, vs_ref, wk_ref, vw_ref, sel_ref, bs_ref, bw_ref, oc_ref, gate_ref,
                     o_ref, acc_ref, sa_ref, sb_ref, *, delta_max, win_tiles):
    tq = B_SWEEP
    dh = B_HEAD_DIM
    cols = B_GROUP * tq
    qi = pl.program_id(2)
    n_tiles = ka_ref.shape[2] // tq
    qt = jnp.concatenate([qt_ref[0, g] for g in range(B_GROUP)], axis=1)
    q_aug = jnp.concatenate([qt, jnp.concatenate([sel_ref[0, 0]] * B_GROUP, axis=1)], axis=0)

    def tile_start(kt):
        return pl.multiple_of(jnp.clip(kt, 0, n_tiles - 1) * tq, tq)

    def normalised(acc):
        return acc[:dh] * (1.0 / acc[dh:dh + 1])

    def sel_bias(kt):
        d = jnp.clip(qi - kt, -1, delta_max) + 1
        return jnp.concatenate([bs_ref[g, d] for g in range(B_GROUP)], axis=1)

    def sel_scores(kt):
        return _dot(ka_ref[0, 0, pl.ds(tile_start(kt), tq), :], q_aug).astype(BF16) + sel_bias(kt)

    def consume(s_buf, kt, m_old):
        s = s_buf[...]
        m_new = jnp.maximum(m_old, jnp.max(s, axis=0, keepdims=True).astype(F32))
        alpha = jnp.exp2(m_old - m_new)
        p = jnp.exp2(s - m_new.astype(BF16))
        acc_ref[...] = alpha * acc_ref[...] + _dot(vs_ref[0, 0, :, pl.ds(tile_start(kt), tq)], p)
        return m_new

    acc_ref[...] = jnp.zeros(acc_ref.shape, F32)
    sa_ref[...] = sel_scores(0)

    def pair(j, m):
        kt = 2 * j
        sb_ref[...] = sel_scores(kt + 1)
        m = consume(sa_ref, kt, m)
        sa_ref[...] = sel_scores(kt + 2)
        return consume(sb_ref, kt + 1, m)

    lax.fori_loop(0, (qi + 2) // 2, pair, jnp.full((1, cols), NEG_INF, F32))
    o_s = normalised(acc_ref[...])

    tiles = []
    for u in range(win_tiles):
        kt = qi - (win_tiles - 1) + u
        d = jnp.where(kt >= 0, qi - kt, -1) + 1
        bias = jnp.concatenate([bw_ref[g, d] for g in range(B_GROUP)], axis=1)
        tiles.append((_dot(wk_ref[0, 0, pl.ds(tile_start(kt), tq), :], qt).astype(BF16) + bias, kt))
    m = None
    for s, _ in tiles:
        tile_max = jnp.max(s, axis=0, keepdims=True)
        m = tile_max if m is None else jnp.maximum(m, tile_max)
    acc = None
    for s, kt in tiles:
        pv = _dot(vw_ref[0, 0, :, pl.ds(tile_start(kt), tq)], jnp.exp2(s - m))
        acc = pv if acc is None else acc + pv
    o_w = normalised(acc)

    gate = gate_ref[0, 0]
    merged = []
    for g in range(B_GROUP):
        cs = slice(g * tq, (g + 1) * tq)
        merged.append(gate[3 * g:3 * g + 1] * oc_ref[0, g] + gate[3 * g + 1:3 * g + 2] * o_s[:, cs]
                      + gate[3 * g + 2:3 * g + 3] * o_w[:, cs])
    for pair in range(B_GROUP // 2):
        both = jnp.concatenate([merged[2 * pair], merged[2 * pair + 1]], axis=0)
        o_ref[0, :, pair * 2 * dh:(pair + 1) * 2 * dh] = both.T.astype(o_ref.dtype)


def _b_sparse(qt, ka, vs, wk, vw, sel, bias_s, bias_w, oc, gate, batch, seq):
    n_sel = seq // B_SEL_BLOCK
    dh = B_HEAD_DIM
    n_ds = bias_s.shape[1]
    n_dw = bias_w.shape[1]
    vrows = vs.shape[2]
    kern = functools.partial(_b_sparse_kernel, delta_max=n_ds - 2, win_tiles=n_dw - 1)
    whole = lambda rows, width: pl.BlockSpec((1, 1, rows, width), lambda b, n, i: (b, n, 0, 0))
    return pl.pallas_call(
        kern,
        out_shape=jax.ShapeDtypeStruct((batch, seq, N_HEADS * dh), BF16),
        grid=(batch, B_KV_HEADS, seq // B_SWEEP),
        in_specs=[pl.BlockSpec((1, B_GROUP, dh, B_SWEEP), lambda b, n, i: (b, n, 0, i)),
                  whole(seq, dh + n_sel), whole(vrows, seq), whole(seq, dh), whole(vrows, seq),
                  pl.BlockSpec((1, 1, n_sel, B_SWEEP), lambda b, n, i: (b, n, 0, i)),
                  pl.BlockSpec((B_GROUP, n_ds, B_SWEEP, B_SWEEP), lambda b, n, i: (n, 0, 0, 0)),
                  pl.BlockSpec((B_GROUP, n_dw, B_SWEEP, B_SWEEP), lambda b, n, i: (n, 0, 0, 0)),
                  pl.BlockSpec((1, B_GROUP, dh, B_SWEEP), lambda b, n, i: (b, n, 0, i)),
                  pl.BlockSpec((1, 1, B_GATE_ROWS, B_SWEEP), lambda b, n, i: (b, n, 0, i))],
        out_specs=pl.BlockSpec((1, B_SWEEP, B_GROUP * dh), lambda b, n, i: (b, i, n)),
        scratch_shapes=[pltpu.VMEM((vrows, B_GROUP * B_SWEEP), F32),
                        pltpu.VMEM((B_SWEEP, B_GROUP * B_SWEEP), BF16),
                        pltpu.VMEM((B_SWEEP, B_GROUP * B_SWEEP), BF16)],
        compiler_params=_params("parallel", "parallel", "arbitrary"),
        name="b_sparse",
    )(qt, ka, vs, wk, vw, sel, bias_s, bias_w, oc, gate)


def _mixer_b(x, rel_bias, norm1, w_in, q_gain, k_gain, cmp_pos, cmp_w1, cmp_w2, w_out, batch, seq):
    d = w_in.shape[0]
    qkv_width = w_in.shape[1] - 3 * N_HEADS
    gate_w = w_in[:, qkv_width:].reshape(d, B_KV_HEADS, 3 * B_GROUP)
    gate_w = jnp.pad(gate_w, ((0, 0), (0, 0), (0, B_GATE_ROWS - 3 * B_GROUP))).reshape(d, B_KV_HEADS * B_GATE_ROWS)
    w_pad = jnp.concatenate([w_in[:, :qkv_width], gate_w], axis=1)
    w_pad = jnp.pad(w_pad, ((0, 0), (0, B_PROJ_WIDTH - w_pad.shape[1]))).astype(BF16)
    proj = _norm_matmul(x, norm1, w_pad, BF16, 512).reshape(batch, seq, B_PROJ_WIDTH)
    qt, ck, cv, ka, vs, wk, vw, gate_t = _b_prep(proj, q_gain, k_gain, batch, seq)
    kc, vc = _b_compress(ck, cv, cmp_pos, cmp_w1, cmp_w2, k_gain[0], batch, seq)
    bias_c = _bias_tiles(rel_bias, seq // B_SWEEP, seq // B_CMP_STRIDE, B_SWEEP, base=1 - B_CMP_LEN,
                         tile_step=B_SWEEP, row_step=-B_CMP_STRIDE, col_step=1, dmax=1 << 30)
    oc, sel = _b_cmp_attn(qt, kc, vc, bias_c, batch, seq)
    delta_max = min(seq // B_SWEEP - 1, -(-(_THRESHOLDS[-1] + B_SWEEP - 1) // B_SWEEP))
    bias_s = _bias_tiles(rel_bias, delta_max + 2, B_SWEEP, B_SWEEP, base=-B_SWEEP, tile_step=B_SWEEP,
                         row_step=-1, col_step=1, dmax=1 << 30, dtype=BF16)
    win_tiles = (B_WINDOW - 1 + B_SWEEP - 1) // B_SWEEP + 1
    bias_w = _bias_tiles(rel_bias, win_tiles + 1, B_SWEEP, B_SWEEP, base=-B_SWEEP, tile_step=B_SWEEP,
                         row_step=-1, col_step=1, dmax=B_WINDOW - 1, dtype=BF16)
    o = _b_sparse(qt, ka, vs, wk, vw, sel, bias_s, bias_w, oc, gate_t, batch, seq)
    return _matmul_residual(o.reshape(batch * seq, -1), w_out.astype(BF16), x)


def _c_conv_kernel(cur_ref, halo_ref, w_ref, sm_ref, alog_ref, dtb_ref, qkv_ref, bg_ref):
    ts = cur_ref.shape[1]
    keep = jnp.where(pl.program_id(1) == 0, 0.0, 1.0)
    dk = C_HEAD_DIM
    for c in range(3 * C_HEADS):
        sl = slice(c * dk, (c + 1) * dk)
        xe = jnp.concatenate([halo_ref[0, :, sl].astype(F32) * keep, cur_ref[0, :, sl].astype(F32)], axis=0)
        y = None
        for j in range(C_CONV):
            off = 8 - (C_CONV - 1) + j
            term = w_ref[j:j + 1, sl] * xe[off:off + ts]
            y = term if y is None else y + term
        y = y * jax.nn.sigmoid(y)
        if c < 2 * C_HEADS:
            y = y * lax.rsqrt(jnp.sum(y * y, axis=-1, keepdims=True) + RMS_EPS)
        if c < C_HEADS:
            y = y * (dk ** -0.5)
        qkv_ref[0, :, sl] = y
    sm = sm_ref[0]
    a = sm + dtb_ref[...]
    softplus = jnp.maximum(a, 0.0) + jnp.log1p(jnp.exp(-jnp.abs(a)))
    g = -jnp.exp(alog_ref[...]) * softplus
    lane = lax.broadcasted_iota(jnp.int32, sm.shape, 1)
    bg_ref[0] = jnp.where(lane < C_HEADS, jax.nn.sigmoid(sm), g)


def _c_conv(proj, small, conv_w, a_log, dt_bias, batch, seq):
    ts = 256
    width = 3 * C_WIDTH
    pad = lambda v: jnp.pad(v, (C_HEADS, 128 - 2 * C_HEADS)).reshape(1, 128)
    return pl.pallas_call(
        _c_conv_kernel,
        out_shape=(jax.ShapeDtypeStruct((batch, seq, width), F32),
                   jax.ShapeDtypeStruct((batch, seq, 128), F32)),
        grid=(batch, seq // ts),
        in_specs=[pl.BlockSpec((1, ts, width), lambda b, i: (b, i, 0)),
                  pl.BlockSpec((1, 8, width), lambda b, i: (b, jnp.maximum(i * (ts // 8) - 1, 0), 0)),
                  pl.BlockSpec((C_CONV, width), lambda b, i: (0, 0)),
                  pl.BlockSpec((1, ts, 128), lambda b, i: (b, i, 0)),
                  pl.BlockSpec((1, 128), lambda b, i: (0, 0)),
                  pl.BlockSpec((1, 128), lambda b, i: (0, 0))],
        out_specs=(pl.BlockSpec((1, ts, width), lambda b, i: (b, i, 0)),
                   pl.BlockSpec((1, ts, 128), lambda b, i: (b, i, 0))),
        compiler_params=_params("parallel", "arbitrary"),
        name="c_conv",
    )(proj, proj, conv_w, small, pad(a_log), pad(dt_bias))


def _sum3(x, fn):
    hi = x.astype(BF16)
    r = x - hi.astype(F32)
    mid = r.astype(BF16)
    lo = (r - mid.astype(F32)).astype(BF16)
    return fn(hi) + (fn(mid) + fn(lo))


def _c_chunk_kernel(qkv_ref, bg_ref, bgt_ref, tri_ref, trit_ref, blk_ref, u_ref, w_ref, qg_ref, kg_ref, attn_ref,
                    gc_ref):
    cs = C_CHUNK
    dk = C_HEAD_DIM
    gs = C_GROUP * cs
    row = lax.broadcasted_iota(jnp.int32, (gs, gs), 0)
    col = lax.broadcasted_iota(jnp.int32, (gs, gs), 1)
    same = (row // cs) == (col // cs)
    causal = same & (row >= col)
    strict = same & (row > col)
    eye = jnp.where(row == col, 1.0, 0.0)

    bgc = bg_ref[0]
    tri = tri_ref[...]
    gcum_col = _sum3(bgc, lambda p: _dot(tri, p))
    glast_col = _sum3(bgc, lambda p: _dot(blk_ref[...], p))
    gcum_row = _sum3(bgt_ref[0], lambda p: _dot(p, trit_ref[...]))
    gc_ref[0] = gcum_col
    t_mats, powers = [], []
    for h in range(C_HEADS):
        gc = gcum_col[:, C_HEADS + h:C_HEADS + h + 1]
        gr = gcum_row[C_HEADS + h:C_HEADS + h + 1, :]
        q = qkv_ref[0, :, h * dk:(h + 1) * dk]
        k = qkv_ref[0, :, C_WIDTH + h * dk:C_WIDTH + (h + 1) * dk]
        decay = jnp.exp(jnp.where(causal, gc - gr, NEG_INF))
        k16 = k.astype(BF16)
        low = jnp.where(strict, _dot_nt((k * bgc[:, h:h + 1]).astype(BF16), k16) * decay, 0.0)
        t_mats.append(eye - low)
        powers.append(low.astype(BF16))
        attn = jnp.where(causal, _dot_nt(q.astype(BF16), k16), 0.0) * decay
        attn_ref[0, :, h * gs:(h + 1) * gs] = attn.astype(attn_ref.dtype)
        qg_ref[0, :, h * dk:(h + 1) * dk] = (q * jnp.exp(gc)).astype(qg_ref.dtype)
        glast = glast_col[:, C_HEADS + h:C_HEADS + h + 1]
        kg_ref[0, :, h * dk:(h + 1) * dk] = (k * jnp.exp(glast - gc)).astype(kg_ref.dtype)
    for _ in range(int(math.log2(cs)) - 1):
        powers = [_dot(p, p).astype(BF16) for p in powers]
        t_mats = [t + _dot(t.astype(BF16), p) for t, p in zip(t_mats, powers)]
    for h in range(C_HEADS):
        gc = gcum_col[:, C_HEADS + h:C_HEADS + h + 1]
        beta = bgc[:, h:h + 1]
        k = qkv_ref[0, :, C_WIDTH + h * dk:C_WIDTH + (h + 1) * dk]
        v = qkv_ref[0, :, 2 * C_WIDTH + h * dk:2 * C_WIDTH + (h + 1) * dk]
        t16 = t_mats[h].astype(BF16)
        u_ref[0, :, h * dk:(h + 1) * dk] = _dot(t16, (v * beta).astype(BF16))
        w_ref[0, :, h * dk:(h + 1) * dk] = _dot(t16, (k * beta * jnp.exp(gc)).astype(BF16)).astype(w_ref.dtype)


def _c_chunks(qkv, bg, bgt, batch, seq):
    gs = C_GROUP * C_CHUNK
    idx = np.arange(gs)
    same = (idx[:, None] // C_CHUNK) == (idx[None, :] // C_CHUNK)
    tri = (same & (idx[:, None] >= idx[None, :])).astype(np.float32)
    wide = lambda width: pl.BlockSpec((1, gs, width), lambda b, i: (b, i, 0))
    shape = lambda width, dtype: jax.ShapeDtypeStruct((batch, seq, width), dtype)
    const = pl.BlockSpec((gs, gs), lambda b, i: (0, 0))
    return pl.pallas_call(
        _c_chunk_kernel,
        out_shape=(shape(C_WIDTH, F32), shape(C_WIDTH, BF16), shape(C_WIDTH, BF16), shape(C_WIDTH, BF16),
                   shape(C_HEADS * gs, BF16), shape(128, F32)),
        grid=(batch, seq // gs),
        in_specs=[wide(3 * C_WIDTH), wide(128),
                  pl.BlockSpec((1, 2 * C_HEADS, gs), lambda b, i: (b, 0, i)),
                  const, const, const],
        out_specs=(wide(C_WIDTH),) * 4 + (wide(C_HEADS * gs), wide(128)),
        compiler_params=_params("parallel", "parallel"),
        name="c_chunks",
    )(qkv, bg, bgt, jnp.asarray(tri, BF16), jnp.asarray(tri.T, BF16), jnp.asarray(same, BF16))


def _c_scan_kernel(u_ref, w_ref, qg_ref, kg_ref, attn_ref, gc_ref, o_ref, state_ref, vnew_ref):
    @pl.when(pl.program_id(1) == 0)
    def _():
        state_ref[...] = jnp.zeros_like(state_ref)

    cs = C_CHUNK
    dk = C_HEAD_DIM
    gs = C_GROUP * cs
    vnew_ref[...] = jnp.zeros_like(vnew_ref)
    for c in range(C_GROUP):
        rs = slice(c * cs, (c + 1) * cs)
        decay_last = jnp.exp(gc_ref[0, (c + 1) * cs - 1:(c + 1) * cs, :])
        for h in range(C_HEADS):
            sl = slice(h * dk, (h + 1) * dk)
            state = state_ref[h]
            s16 = state.astype(BF16)
            v_new = u_ref[0, rs, sl] - _dot(w_ref[0, rs, sl], s16)
            v16 = v_new.astype(BF16)
            vnew_ref[h, rs, :] = v16
            o_ref[0, rs, sl] = _dot(qg_ref[0, rs, sl], s16) + _dot(attn_ref[0, rs, h * gs:(h + 1) * gs], vnew_ref[h])
            state_ref[h] = state * decay_last[:, C_HEADS + h:C_HEADS + h + 1] + _dot_tn(kg_ref[0, rs, sl], v16)


def _c_scan(u, w, qg, kg, attn, gc, batch, seq):
    gs = C_GROUP * C_CHUNK
    wide = lambda width: pl.BlockSpec((1, gs, width), lambda b, c: (b, c, 0))
    return pl.pallas_call(
        _c_scan_kernel,
        out_shape=jax.ShapeDtypeStruct((batch, seq, C_WIDTH), F32),
        grid=(batch, seq // gs),
        in_specs=[wide(C_WIDTH)] * 4 + [wide(C_HEADS * gs), wide(128)],
        out_specs=wide(C_WIDTH),
        scratch_shapes=[pltpu.VMEM((C_HEADS, C_HEAD_DIM, C_HEAD_DIM), F32),
                        pltpu.VMEM((C_HEADS, gs, C_HEAD_DIM), BF16)],
        compiler_params=_params("parallel", "arbitrary"),
        name="c_scan",
    )(u, w, qg, kg, attn, gc)


def _c_out_kernel(o_ref, z_ref, g_ref, w_ref, x_ref, out_ref):
    dk = C_HEAD_DIM
    parts = []
    for h in range(C_HEADS):
        sl = slice(h * dk, (h + 1) * dk)
        z = z_ref[:, sl].astype(F32)
        parts.append((_rms(o_ref[:, sl], g_ref[...]) * (z * jax.nn.sigmoid(z))).astype(BF16))
    out_ref[...] = x_ref[...] + _dot(jnp.concatenate(parts, axis=-1), w_ref[...])


def _c_out(o, proj, out_gain, w_out, x):
    m, d = x.shape
    z_block = (3 * C_WIDTH) // C_WIDTH
    row = lambda width: pl.BlockSpec((ROW_TILE, width), lambda i: (i, 0))
    return pl.pallas_call(
        _c_out_kernel,
        out_shape=jax.ShapeDtypeStruct((m, d), F32),
        grid=(m // ROW_TILE,),
        in_specs=[row(C_WIDTH),
                  pl.BlockSpec((ROW_TILE, C_WIDTH), lambda i: (i, z_block)),
                  pl.BlockSpec((1, C_HEAD_DIM), lambda i: (0, 0)),
                  pl.BlockSpec((C_WIDTH, d), lambda i: (0, 0)),
                  row(d)],
        out_specs=row(d),
        compiler_params=_params("parallel"),
        name="c_out",
    )(o, proj, out_gain.reshape(1, -1), w_out, x)


def _mixer_c(x, norm1, w_in, conv_w, a_log, dt_bias, out_gain, w_out, batch, seq):
    main = 4 * C_WIDTH
    proj = _norm_matmul(x, norm1, w_in[:, :main].astype(BF16), BF16, 512)
    w_small = jnp.pad(w_in[:, main:], ((0, 0), (0, 128 - 2 * C_HEADS))).astype(BF16)
    small = _norm_matmul(x, norm1, w_small, F32, 128)
    qkv, bg = _c_conv(proj.reshape(batch, seq, main), small.reshape(batch, seq, 128), conv_w, a_log, dt_bias,
                      batch, seq)
    bgt = bg[:, :, :2 * C_HEADS].transpose(0, 2, 1)
    u, w, qg, kg, attn, gc = _c_chunks(qkv, bg, bgt, batch, seq)
    o = _c_scan(u, w, qg, kg, attn, gc, batch, seq)
    return _c_out(o.reshape(batch * seq, C_WIDTH), proj, out_gain, w_out.astype(BF16), x)


def kernel(x, rel_bias, l0_norm1, l0_a_w_in, l0_a_q_gain, l0_a_k_gain, l0_a_w_out, l0_norm2, l0_ffn_w_gate, l0_ffn_w_up, l0_ffn_w_down, l1_norm1, l1_b_w_in, l1_b_q_gain, l1_b_k_gain, l1_b_cmp_pos, l1_b_cmp_w1, l1_b_cmp_w2, l1_b_w_out, l1_norm2, l1_ffn_w_gate, l1_ffn_w_up, l1_ffn_w_down, l2_norm1, l2_c_w_in, l2_c_conv_w, l2_c_a_log, l2_c_dt_bias, l2_c_out_gain, l2_c_w_out, l2_norm2, l2_ffn_w_gate, l2_ffn_w_up, l2_ffn_w_down, l3_norm1, l3_a_w_in, l3_a_q_gain, l3_a_k_gain, l3_a_w_out, l3_norm2, l3_ffn_w_gate, l3_ffn_w_up, l3_ffn_w_down):
    batch, seq, d = x.shape
    h = x.reshape(batch * seq, d)

    def ffn(h, norm2, w_gate, w_up, w_down):
        return _ffn(h, norm2, w_gate.astype(BF16), w_up.astype(BF16), w_down.astype(BF16))

    h = _mixer_a(h, rel_bias, l0_norm1, l0_a_w_in, l0_a_q_gain, l0_a_k_gain, l0_a_w_out, batch, seq)
    h = ffn(h, l0_norm2, l0_ffn_w_gate, l0_ffn_w_up, l0_ffn_w_down)
    h = _mixer_b(h, rel_bias, l1_norm1, l1_b_w_in, l1_b_q_gain, l1_b_k_gain, l1_b_cmp_pos, l1_b_cmp_w1,
                 l1_b_cmp_w2, l1_b_w_out, batch, seq)
    h = ffn(h, l1_norm2, l1_ffn_w_gate, l1_ffn_w_up, l1_ffn_w_down)
    h = _mixer_c(h, l2_norm1, l2_c_w_in, l2_c_conv_w, l2_c_a_log, l2_c_dt_bias, l2_c_out_gain, l2_c_w_out,
                 batch, seq)
    h = ffn(h, l2_norm2, l2_ffn_w_gate, l2_ffn_w_up, l2_ffn_w_down)
    h = _mixer_a(h, rel_bias, l3_norm1, l3_a_w_in, l3_a_q_gain, l3_a_k_gain, l3_a_w_out, batch, seq)
    h = ffn(h, l3_norm2, l3_ffn_w_gate, l3_ffn_w_up, l3_ffn_w_down)
    return h.reshape(batch, seq, d)
```

```python
import functools
import math

import numpy as np
import jax
import jax.numpy as jnp
from jax import lax
from jax.experimental import pallas as pl
from jax.experimental.pallas import tpu as pltpu

D_MODEL = 1024
RMS_EPS = 1e-6
NEG_INF = -1e30
TINY = 1e-30
FORCE_SCORE = 1e9

N_BUCKETS = 32
REL_MAX_DISTANCE = 2048
N_HEADS = 16

A_GROUPS = ((128, 1), (512, 4), (2048, 16))
A_HEAD_DIM = 64
A_Q_BLOCK = 128
A_PROJ_TILE = 512
A_STAT_WIDTH = 256
A_BLOCKS_PER_STEP = 2

B_KV_HEADS = 4
B_GROUP = 4
B_HEAD_DIM = 64
B_CMP_LEN = 32
B_CMP_STRIDE = 16
B_CMP_HIDDEN = 256
B_SEL_BLOCK = 64
B_TOP_N = 16
B_WINDOW = 512
B_TILE = 128
B_SWEEP = 256
B_PROJ_WIDTH = 3072
B_GATE_ROWS = 16

C_HEADS = 8
C_HEAD_DIM = 128
C_WIDTH = C_HEADS * C_HEAD_DIM
C_CONV = 4
C_CHUNK = 64
C_GROUP = 4

FFN_HIDDEN = 2816
FFN_TILE = 1024

ROW_TILE = 512
VMEM_LIMIT = 48 * 1024 * 1024

LOG2E = math.log2(math.e)

F32 = jnp.float32
BF16 = jnp.bfloat16

NT_DIMS = (((1,), (1,)), ((), ()))
TN_DIMS = (((0,), (0,)), ((), ()))


def _params(*semantics):
    return pltpu.CompilerParams(dimension_semantics=semantics, vmem_limit_bytes=VMEM_LIMIT)


def _dot(a, b):
    return jnp.dot(a, b, preferred_element_type=F32)


def _dot_nt(a, b):
    return lax.dot_general(a, b, NT_DIMS, preferred_element_type=F32)


def _dot_tn(a, b):
    return lax.dot_general(a, b, TN_DIMS, preferred_element_type=F32)


def _rms(x, gain):
    return x * lax.rsqrt(jnp.mean(x * x, axis=-1, keepdims=True) + RMS_EPS) * gain


def _bucket_thresholds():
    d = np.arange(1 << 15)
    max_exact = N_BUCKETS // 2
    d_f = np.maximum(d, 1).astype(np.float32)
    large = max_exact + (np.log(d_f / np.float32(max_exact)) / np.float32(math.log(REL_MAX_DISTANCE / max_exact))
                         * np.float32(N_BUCKETS - max_exact)).astype(np.int32)
    bucket = np.where(d < max_exact, d, np.minimum(large, N_BUCKETS - 1))
    return [int(np.argmax(bucket >= k)) if np.any(bucket >= k) else int(1 << 30) for k in range(N_BUCKETS)]


_THRESHOLDS = _bucket_thresholds()


def _bias_tile_kernel(tbl_ref, o_ref, *, base, tile_step, row_step, col_step, dmax, dil):
    h = pl.program_id(0)
    t = pl.program_id(1)
    shape = o_ref.shape[2:]
    i = lax.broadcasted_iota(jnp.int32, shape, 0)
    j = lax.broadcasted_iota(jnp.int32, shape, 1)
    dist = base + tile_step * t + row_step * i + col_step * j
    d = dist * dil
    val = jnp.full(shape, tbl_ref[0, h], F32)
    for k in range(1, N_BUCKETS):
        val = jnp.where(d >= _THRESHOLDS[k], tbl_ref[k, h], val)
    valid = (dist >= 0) & (dist <= dmax)
    o_ref[0, 0] = jnp.where(valid, val * LOG2E, NEG_INF).astype(o_ref.dtype)


def _bias_tiles(rel_bias, n_tiles, rows, cols, *, base, tile_step, row_step, col_step, dmax, dil=1, dtype=F32):
    kern = functools.partial(_bias_tile_kernel, base=base, tile_step=tile_step, row_step=row_step,
                             col_step=col_step, dmax=dmax, dil=dil)
    return pl.pallas_call(
        kern,
        out_shape=jax.ShapeDtypeStruct((N_HEADS, n_tiles, rows, cols), dtype),
        grid=(N_HEADS, n_tiles),
        in_specs=[pl.BlockSpec(memory_space=pltpu.SMEM)],
        out_specs=pl.BlockSpec((1, 1, rows, cols), lambda h, t: (h, t, 0, 0)),
        compiler_params=_params("parallel", "parallel"),
        name="bias_tiles",
    )(rel_bias)


def _resident(shape):
    return pl.BlockSpec(shape, lambda i: (0,) * len(shape), pipeline_mode=pl.Buffered(1))


def _norm_matmul_kernel(x_ref, g_ref, w_ref, o_ref, *, tn):
    h = _rms(x_ref[...], g_ref[...]).astype(BF16)
    for j in range(w_ref.shape[1] // tn):
        o_ref[:, j * tn:(j + 1) * tn] = _dot(h, w_ref[:, j * tn:(j + 1) * tn]).astype(o_ref.dtype)


def _norm_matmul(x, gain, w, out_dtype, tn):
    m, d = x.shape
    n = w.shape[1]
    return pl.pallas_call(
        functools.partial(_norm_matmul_kernel, tn=tn),
        out_shape=jax.ShapeDtypeStruct((m, n), out_dtype),
        grid=(m // ROW_TILE,),
        in_specs=[pl.BlockSpec((ROW_TILE, d), lambda i: (i, 0)),
                  _resident((1, d)),
                  _resident((d, n))],
        out_specs=pl.BlockSpec((ROW_TILE, n), lambda i: (i, 0)),
        compiler_params=_params("parallel"),
        name="norm_matmul",
    )(x, gain.reshape(1, d), w)


def _matmul_residual_kernel(a_ref, w_ref, x_ref, o_ref):
    o_ref[...] = x_ref[...] + _dot(a_ref[...], w_ref[...])


def _matmul_residual(a, w, x):
    m, k = a.shape
    d = w.shape[1]
    return pl.pallas_call(
        _matmul_residual_kernel,
        out_shape=jax.ShapeDtypeStruct((m, d), F32),
        grid=(m // ROW_TILE,),
        in_specs=[pl.BlockSpec((ROW_TILE, k), lambda i: (i, 0)),
                  pl.BlockSpec((k, d), lambda i: (0, 0)),
                  pl.BlockSpec((ROW_TILE, d), lambda i: (i, 0))],
        out_specs=pl.BlockSpec((ROW_TILE, d), lambda i: (i, 0)),
        compiler_params=_params("parallel"),
        name="matmul_residual",
    )(a, w, x)


def _ffn_kernel(x_ref, g_ref, wg_ref, wu_ref, wd_ref, o_ref):
    x = x_ref[...]
    h = _rms(x, g_ref[...]).astype(BF16)
    hidden = wg_ref.shape[1]
    acc = x
    for lo in range(0, hidden, FFN_TILE):
        hi = min(lo + FFN_TILE, hidden)
        a = _dot(h, wg_ref[:, lo:hi])
        b = _dot(h, wu_ref[:, lo:hi])
        acc = acc + _dot((a * jax.nn.sigmoid(a) * b).astype(BF16), wd_ref[lo:hi, :])
    o_ref[...] = acc


def _ffn(x, gain, w_gate, w_up, w_down):
    m, d = x.shape
    hidden = w_gate.shape[1]
    return pl.pallas_call(
        _ffn_kernel,
        out_shape=jax.ShapeDtypeStruct((m, d), F32),
        grid=(m // ROW_TILE,),
        in_specs=[pl.BlockSpec((ROW_TILE, d), lambda i: (i, 0)),
                  _resident((1, d)),
                  _resident((d, hidden)), _resident((d, hidden)), _resident((hidden, d))],
        out_specs=pl.BlockSpec((ROW_TILE, d), lambda i: (i, 0)),
        compiler_params=_params("parallel"),
        name="ffn",
    )(x, gain.reshape(1, d), w_gate, w_up, w_down)


def _a_proj_kernel(x_ref, g_ref, w_ref, qg_ref, kg_ref, o_ref, h_ref, x_scr, *, dil):
    rows = ROW_TILE // dil
    xn = _rms(x_ref[...], g_ref[...])
    if dil == 1:
        h_ref[...] = xn.astype(BF16)
    else:
        slabs = xn.shape[1] // 128
        for c in range(slabs):
            x_scr[c] = xn[:, c * 128:(c + 1) * 128]
        for r in range(dil):
            picked = [x_scr[c, pl.ds(r, rows, stride=dil), :] for c in range(slabs)]
            h_ref[r * rows:(r + 1) * rows, :] = jnp.concatenate(picked, axis=1).astype(BF16)
    h = h_ref[...]
    width = w_ref.shape[1]
    hd = width // 3
    low = lax.broadcasted_iota(jnp.int32, (ROW_TILE, 128), 1) < A_HEAD_DIM
    for j in range(width // A_PROJ_TILE):
        res = _dot(h, w_ref[:, j * A_PROJ_TILE:(j + 1) * A_PROJ_TILE])
        kind = (j * A_PROJ_TILE) // hd
        if kind < 2:
            parts = []
            for c in range(A_PROJ_TILE // 128):
                y = res[:, c * 128:(c + 1) * 128]
                sq = y * y
                tot = jnp.sum(sq, axis=-1, keepdims=True)
                lo = jnp.sum(jnp.where(low, sq, 0.0), axis=-1, keepdims=True)
                ss = jnp.where(low, lo, tot - lo)
                parts.append(y * lax.rsqrt(ss * (1.0 / A_HEAD_DIM) + RMS_EPS))
            res = jnp.concatenate(parts, axis=1) * (qg_ref if kind == 0 else kg_ref)[...]
        res = res.astype(BF16)
        for r in range(dil):
            off = r * width + j * A_PROJ_TILE
            o_ref[:, off:off + A_PROJ_TILE] = res[r * rows:(r + 1) * rows]


def _a_proj(x, gain, w, q_gain, k_gain, dil):
    m, d = x.shape
    width = w.shape[1]
    reps = A_PROJ_TILE // A_HEAD_DIM
    qg = jnp.tile(q_gain * (A_HEAD_DIM ** -0.5 * LOG2E), reps).reshape(1, A_PROJ_TILE)
    kg = jnp.tile(k_gain, reps).reshape(1, A_PROJ_TILE)
    return pl.pallas_call(
        functools.partial(_a_proj_kernel, dil=dil),
        out_shape=jax.ShapeDtypeStruct((m // dil, dil * width), BF16),
        grid=(m // ROW_TILE,),
        in_specs=[pl.BlockSpec((ROW_TILE, d), lambda i: (i, 0)),
                  pl.BlockSpec((1, d), lambda i: (0, 0)),
                  pl.BlockSpec((d, width), lambda i: (0, 0)),
                  pl.BlockSpec((1, A_PROJ_TILE), lambda i: (0, 0)),
                  pl.BlockSpec((1, A_PROJ_TILE), lambda i: (0, 0))],
        out_specs=pl.BlockSpec((ROW_TILE // dil, dil * width), lambda i: (i, 0)),
        scratch_shapes=[pltpu.VMEM((ROW_TILE, d), BF16), pltpu.VMEM((d // 128, ROW_TILE, 128), F32)],
        compiler_params=_params("parallel"),
        name="a_proj",
    )(x, gain.reshape(1, d), w, qg, kg)


def _a_attn_kernel(q_ref, kp_ref, kc_ref, vp_ref, vc_ref, bias_ref, o_ref, stat_ref):
    first = (pl.program_id(2) == 0).astype(jnp.int32)
    nq = A_Q_BLOCK
    lane = lax.broadcasted_iota(jnp.int32, (nq, 128), 1)
    low = lane < A_HEAD_DIM
    ones = jnp.ones((2 * nq, 128), BF16)
    for sub in range(A_BLOCKS_PER_STEP):
        rows = slice(sub * nq, (sub + 1) * nq)
        max_tile = jnp.zeros((nq, 128), F32)
        den_tile = jnp.ones((nq, 128), F32)
        for pair in range(N_HEADS // 2):
            sl = slice(pair * 128, (pair + 1) * 128)
            q = q_ref[0, rows, sl]
            zero = jnp.zeros_like(q)
            qq = jnp.concatenate([jnp.where(low, q, zero), jnp.where(low, zero, q)], axis=0)
            if sub == 0:
                kk = jnp.concatenate([kp_ref[0, :, sl], kc_ref[0, :nq, sl]], axis=0)
                vv = jnp.concatenate([vp_ref[0, :, sl], vc_ref[0, :nq, sl]], axis=0)
                base = 2 * pair + N_HEADS * first
            else:
                kk = kc_ref[0, :, sl]
                vv = vc_ref[0, :, sl]
                base = 2 * pair
            s = _dot_nt(qq, kk) + jnp.concatenate([bias_ref[base], bias_ref[base + 1]], axis=0)
            m = jnp.max(s, axis=-1, keepdims=True)
            acc = _dot(jnp.exp2(s - m).astype(BF16), jnp.concatenate([vv, ones], axis=1))
            o_ref[0, rows, sl] = jnp.where(low, acc[:nq, :128], acc[nq:, :128]).astype(o_ref.dtype)
            first_head = lane == 2 * pair
            second_head = lane == 2 * pair + 1
            max_tile = jnp.where(first_head, m[:nq], jnp.where(second_head, m[nq:], max_tile))
            den_tile = jnp.where(first_head, acc[:nq, 128:], jnp.where(second_head, acc[nq:, 128:], den_tile))
        stat_ref[0, rows, :128] = max_tile
        stat_ref[0, rows, 128:] = den_tile


def _a_attention(proj, bias, dil, batch, seq):
    length = seq // dil
    nblk = length // A_Q_BLOCK
    hd = N_HEADS * A_HEAD_DIM
    pv = proj.reshape(batch, length, dil * 3 * hd)

    per = A_BLOCKS_PER_STEP
    step_rows = per * A_Q_BLOCK
    assert nblk % per == 0

    def spec(off, prev):
        if prev:
            return pl.BlockSpec((1, A_Q_BLOCK, hd), lambda b, r, i: (b, jnp.maximum(per * i - 1, 0), r * 3 + off))
        return pl.BlockSpec((1, step_rows, hd), lambda b, r, i: (b, i, r * 3 + off))

    o, stats = pl.pallas_call(
        _a_attn_kernel,
        out_shape=(jax.ShapeDtypeStruct((batch, length, dil * hd), BF16),
                   jax.ShapeDtypeStruct((batch, length, dil * A_STAT_WIDTH), F32)),
        grid=(batch, dil, nblk // per),
        in_specs=[spec(0, False), spec(1, True), spec(1, False), spec(2, True), spec(2, False),
                  pl.BlockSpec((2 * N_HEADS, A_Q_BLOCK, 2 * A_Q_BLOCK), lambda b, r, i: (0, 0, 0))],
        out_specs=(pl.BlockSpec((1, step_rows, hd), lambda b, r, i: (b, i, r)),
                   pl.BlockSpec((1, step_rows, A_STAT_WIDTH), lambda b, r, i: (b, i, r))),
        compiler_params=_params("parallel", "parallel", "arbitrary"),
        name="a_attention",
    )(pv, pv, pv, pv, pv, bias)
    return o.reshape(batch * length, dil * hd), stats.reshape(batch * length, dil * A_STAT_WIDTH)


def _a_out_kernel(o0_ref, o1_ref, o2_ref, s0_ref, s1_ref, s2_ref, e_ref, w_ref, x_ref, out_ref, o_scr, s_scr):
    hd = N_HEADS * A_HEAD_DIM
    sw = A_STAT_WIDTH
    for g, (o_ref, s_ref) in enumerate(((o0_ref, s0_ref), (o1_ref, s1_ref), (o2_ref, s2_ref))):
        dil = A_GROUPS[g][1]
        rows = ROW_TILE // dil
        for r in range(dil):
            dst = pl.ds(r, rows, stride=dil) if dil > 1 else slice(None)
            s_scr[g, 0, dst, :] = s_ref[:, r * sw:r * sw + 128]
            s_scr[g, 1, dst, :] = s_ref[:, r * sw + 128:(r + 1) * sw]
            for c in range(hd // 128):
                o_scr[g, c, dst, :] = o_ref[:, r * hd + c * 128:r * hd + (c + 1) * 128].astype(F32)
    groups = range(len(A_GROUPS))
    top = functools.reduce(jnp.maximum, [s_scr[g, 0] for g in groups])
    es = [jnp.exp2(s_scr[g, 0] - top) for g in groups]
    inv = 1.0 / sum(es[g] * s_scr[g, 1] for g in groups)
    expand = e_ref[...]
    acc = None
    for g in groups:
        o_g = jnp.concatenate([o_scr[g, c] for c in range(hd // 128)], axis=1)
        term = _dot((es[g] * inv).astype(BF16), expand) * o_g
        acc = term if acc is None else acc + term
    out_ref[...] = x_ref[...] + _dot(acc.astype(BF16), w_ref[...])


def _a_out(outs, stats, w_out, x):
    m, d = x.shape
    hd = N_HEADS * A_HEAD_DIM
    expand = np.zeros((128, hd), np.float32)
    for h in range(N_HEADS):
        expand[h, h * A_HEAD_DIM:(h + 1) * A_HEAD_DIM] = 1.0
    grouped = lambda width: [pl.BlockSpec((ROW_TILE // dil, dil * width), lambda i: (i, 0)) for _, dil in A_GROUPS]
    return pl.pallas_call(
        _a_out_kernel,
        out_shape=jax.ShapeDtypeStruct((m, d), F32),
        grid=(m // ROW_TILE,),
        in_specs=grouped(hd) + grouped(A_STAT_WIDTH) + [
            pl.BlockSpec((128, hd), lambda i: (0, 0)),
            pl.BlockSpec((hd, d), lambda i: (0, 0)),
            pl.BlockSpec((ROW_TILE, d), lambda i: (i, 0))],
        out_specs=pl.BlockSpec((ROW_TILE, d), lambda i: (i, 0)),
        scratch_shapes=[pltpu.VMEM((len(A_GROUPS), hd // 128, ROW_TILE, 128), F32),
                        pltpu.VMEM((len(A_GROUPS), 2, ROW_TILE, 128), F32)],
        compiler_params=_params("parallel"),
        name="a_out",
    )(*outs, *stats, jnp.asarray(expand, BF16), w_out, x)


def _a_biases(rel_bias):
    biases = []
    for window, dil in A_GROUPS:
        steps = window // dil
        assert steps == A_Q_BLOCK
        bias = _bias_tiles(rel_bias, 1, A_Q_BLOCK, 2 * A_Q_BLOCK, base=A_Q_BLOCK, tile_step=0, row_step=1,
                           col_step=-1, dmax=steps, dil=dil)[:, 0]
        biases.append(jnp.concatenate([bias, bias.at[:, :, :A_Q_BLOCK].set(NEG_INF)], axis=0))
    return biases


def _mixer_a(x, biases, norm1, w_in, q_gain, k_gain, w_out, batch, seq):
    w_in = w_in.astype(BF16)
    group_width = 3 * N_HEADS * A_HEAD_DIM
    outs, stats = [], []
    for gi, (_, dil) in enumerate(A_GROUPS):
        assert (seq // dil) % A_Q_BLOCK == 0 and seq % ROW_TILE == 0
        bias = biases[gi]
        proj = _a_proj(x, norm1, w_in[:, gi * group_width:(gi + 1) * group_width], q_gain[gi], k_gain[gi], dil)
        o, stat = _a_attention(proj, bias, dil, batch, seq)
        outs.append(o)
        stats.append(stat)
    return _a_out(outs, stats, w_out.astype(BF16), x)


def _b_prep_kernel(p_ref, qg_ref, kg_ref, qt_ref, ck_ref, cv_ref, ka_ref, vs_ref, wk_ref, vw_ref, gate_ref):
    dh = B_HEAD_DIM
    ts = p_ref.shape[1]
    n_sel = ka_ref.shape[3] - dh
    low = lax.broadcasted_iota(jnp.int32, (ts, 128), 1) < dh

    def slab(c):
        return p_ref[0, :, c * 128:(c + 1) * 128].astype(F32)

    def normed(x, gain):
        sq = x * x
        tot = jnp.sum(sq, axis=-1, keepdims=True)
        first = jnp.sum(jnp.where(low, sq, 0.0), axis=-1, keepdims=True)
        ss = jnp.where(low, first, tot - first)
        return x * lax.rsqrt(ss * (1.0 / dh) + RMS_EPS) * jnp.concatenate([gain, gain], axis=1)

    def halves(x):
        return x[:, :dh], x[:, dh:]

    qg = qg_ref[...] * (dh ** -0.5 * LOG2E)
    for c in range(N_HEADS // 2):
        xt = normed(slab(c), qg).T.astype(BF16)
        qt_ref[0, 2 * c] = xt[:dh]
        qt_ref[0, 2 * c + 1] = xt[dh:]
    base = N_HEADS // 2
    pairs = B_KV_HEADS // 2
    pos = pl.program_id(1) * ts + lax.broadcasted_iota(jnp.int32, (ts, 128), 0)
    lane = lax.broadcasted_iota(jnp.int32, (ts, 128), 1)
    onehot = jnp.where(lane - dh == pos // B_SEL_BLOCK, 1.0, 0.0)
    ones = jnp.ones((vs_ref.shape[2] - dh, ts), BF16)
    for j in range(pairs):
        for ref, off in ((ck_ref, 0), (cv_ref, pairs)):
            a, b = halves(slab(base + off + j))
            ref[0, 2 * j] = a.astype(BF16)
            ref[0, 2 * j + 1] = b.astype(BF16)
        k_sel = normed(slab(base + 2 * pairs + j), kg_ref[1:2, :])
        for n, k in zip((2 * j, 2 * j + 1), (k_sel, pltpu.roll(k_sel, dh, 1))):
            ka_ref[0, n] = jnp.where(low, k, onehot)[:, :dh + n_sel].astype(BF16)
        for n, k in zip((2 * j, 2 * j + 1), halves(normed(slab(base + 4 * pairs + j), kg_ref[2:3, :]))):
            wk_ref[0, n] = k.astype(BF16)
        for ref, off in ((vs_ref, 3 * pairs), (vw_ref, 5 * pairs)):
            xt = slab(base + off + j).T.astype(BF16)
            for n, v in zip((2 * j, 2 * j + 1), (xt[:dh], xt[dh:])):
                ref[0, n, :dh, :] = v
                ref[0, n, dh:, :] = ones
    gate = jax.nn.sigmoid(slab(base + 6 * pairs)).T
    rows = gate_ref.shape[2]
    for n in range(B_KV_HEADS):
        gate_ref[0, n] = gate[n * rows:(n + 1) * rows]


def _b_prep(proj, q_gain, k_gain, batch, seq):
    ts = 256
    dh = B_HEAD_DIM
    n_sel = seq // B_SEL_BLOCK
    assert dh + n_sel <= 128
    rows_shape = lambda width: jax.ShapeDtypeStruct((batch, B_KV_HEADS, seq, width), BF16)
    rows_spec = lambda width: pl.BlockSpec((1, B_KV_HEADS, ts, width), lambda b, i: (b, 0, i, 0))
    cols_shape = lambda heads, height, dtype: jax.ShapeDtypeStruct((batch, heads, height, seq), dtype)
    cols_spec = lambda heads, height: pl.BlockSpec((1, heads, height, ts), lambda b, i: (b, 0, 0, i))
    return pl.pallas_call(
        _b_prep_kernel,
        out_shape=(cols_shape(N_HEADS, dh, BF16), rows_shape(dh), rows_shape(dh), rows_shape(dh + n_sel),
                   cols_shape(B_KV_HEADS, dh + 16, BF16), rows_shape(dh), cols_shape(B_KV_HEADS, dh + 16, BF16),
                   cols_shape(B_KV_HEADS, B_GATE_ROWS, F32)),
        grid=(batch, seq // ts),
        in_specs=[pl.BlockSpec((1, ts, B_PROJ_WIDTH), lambda b, i: (b, i, 0)),
                  pl.BlockSpec((1, dh), lambda b, i: (0, 0)),
                  pl.BlockSpec((3, dh), lambda b, i: (0, 0))],
        out_specs=(cols_spec(N_HEADS, dh), rows_spec(dh), rows_spec(dh), rows_spec(dh + n_sel),
                   cols_spec(B_KV_HEADS, dh + 16), rows_spec(dh), cols_spec(B_KV_HEADS, dh + 16),
                   cols_spec(B_KV_HEADS, B_GATE_ROWS)),
        compiler_params=_params("parallel", "parallel"),
        name="b_prep",
    )(proj, q_gain.reshape(1, dh), k_gain)


def _b_compress_kernel(tk_ref, tv_ref, pos_ref, w1_ref, w2_ref, kg_ref, kc_ref, vc_ref):
    half = (B_CMP_LEN // 2) * B_HEAD_DIM
    for kv, (t_ref, out_ref) in enumerate(((tk_ref, kc_ref), (tv_ref, vc_ref))):
        t = t_ref[0, 0].astype(F32)
        top = (t + pos_ref[kv, 0:1, :]).astype(BF16)
        bot = (t + pos_ref[kv, 1:2, :]).astype(BF16)
        a1 = _dot(top, w1_ref[kv, :half, :])
        a2 = _dot(bot, w1_ref[kv, half:, :])
        hidden = a1 + pltpu.roll(a2, a2.shape[0] - 1, 0)
        out = _dot(jax.nn.gelu(hidden).astype(BF16), w2_ref[kv])
        if kv == 0:
            out = _rms(out, kg_ref[...])
        out_ref[0, 0] = out.astype(out_ref.dtype)


def _b_compress(ck, cv, cmp_pos, cmp_w1, cmp_w2, k_gain0, batch, seq):
    rows = seq // B_CMP_STRIDE
    half = (B_CMP_LEN // 2) * B_HEAD_DIM
    tk = ck.reshape(batch, B_KV_HEADS, rows, half)
    tv = cv.reshape(batch, B_KV_HEADS, rows, half)
    pos = cmp_pos.reshape(2, 2, half)
    t_spec = pl.BlockSpec((1, 1, rows, half), lambda b, n: (b, n, 0, 0))
    o_spec = pl.BlockSpec((1, 1, rows, B_HEAD_DIM), lambda b, n: (b, n, 0, 0))
    shape = jax.ShapeDtypeStruct((batch, B_KV_HEADS, rows, B_HEAD_DIM), BF16)
    return pl.pallas_call(
        _b_compress_kernel,
        out_shape=(shape, shape),
        grid=(batch, B_KV_HEADS),
        in_specs=[t_spec, t_spec,
                  pl.BlockSpec((2, 2, half), lambda b, n: (0, 0, 0)),
                  pl.BlockSpec((2, 2 * half, B_CMP_HIDDEN), lambda b, n: (0, 0, 0)),
                  pl.BlockSpec((2, B_CMP_HIDDEN, B_HEAD_DIM), lambda b, n: (0, 0, 0)),
                  pl.BlockSpec((1, B_HEAD_DIM), lambda b, n: (0, 0))],
        out_specs=(o_spec, o_spec),
        compiler_params=_params("parallel", "parallel"),
        name="b_compress",
    )(tk, tv, pos, cmp_w1.astype(BF16), cmp_w2.astype(BF16), k_gain0.reshape(1, -1))


def _b_cmp_attn_kernel(qt_ref, kc_ref, vct_ref, bias_ref, c2s_ref, oc_ref, sel_ref, imp_ref, *, top_n):
    tq = B_SWEEP
    n_sel = imp_ref.shape[0]
    qt = jnp.concatenate([qt_ref[0, g] for g in range(B_GROUP)], axis=1)
    n_cmp_pad = kc_ref.shape[2]
    shift = tq // B_CMP_STRIDE
    off = pl.multiple_of((pl.num_programs(2) - 1 - pl.program_id(2)) * shift, shift)
    bias = jnp.concatenate([bias_ref[g, 0, pl.ds(off, n_cmp_pad), :] for g in range(B_GROUP)], axis=1)
    s = _dot(kc_ref[0, 0], qt) + bias
    m = jnp.max(s, axis=0, keepdims=True)
    e = jnp.exp2(s - m)
    z = jnp.maximum(jnp.sum(e, axis=0, keepdims=True), TINY)
    pos = pl.program_id(2) * tq + lax.broadcasted_iota(jnp.int32, (1, tq), 1)
    sees_any = jnp.concatenate([pos >= B_CMP_LEN - 1] * B_GROUP, axis=1)
    p = e * jnp.where(sees_any, 1.0 / z, 0.0)
    oct = _dot(vct_ref[0, 0], p.astype(BF16))
    for g in range(B_GROUP):
        oc_ref[0, g] = oct[:, g * tq:(g + 1) * tq]

    p_sum = p[:, 0:tq] + p[:, tq:2 * tq] + p[:, 2 * tq:3 * tq] + p[:, 3 * tq:4 * tq]
    hi = p_sum.astype(BF16)
    lo = (p_sum - hi.astype(F32)).astype(BF16)
    c2s = c2s_ref[...]
    imp = _dot(c2s, hi) + _dot(c2s, lo)

    t = pl.program_id(2) * tq + lax.broadcasted_iota(jnp.int32, (n_sel, tq), 1)
    blk = lax.broadcasted_iota(jnp.int32, (n_sel, tq), 0)
    cur = t // B_SEL_BLOCK
    forced = (blk == 0) | (blk == cur) | (blk == cur - 1)
    imp = jnp.where(forced, FORCE_SCORE, jnp.where(blk * B_SEL_BLOCK <= t, imp, NEG_INF))
    imp_ref[...] = imp

    def count(i, rank):
        row = imp_ref[pl.ds(i, 1), :]
        ahead = jnp.where(row > imp, 1.0, jnp.where(row == imp, jnp.where(blk > i, 1.0, 0.0), 0.0))
        return rank + ahead

    n_live = jnp.minimum(n_sel, (pl.program_id(2) + 1) * (tq // B_SEL_BLOCK))
    rank = lax.fori_loop(0, n_live, count, jnp.zeros((n_sel, tq), F32))
    sel_ref[0, 0] = jnp.where(rank < top_n, 0.0, NEG_INF).astype(sel_ref.dtype)


def _b_cmp_attn(qt, kc, vc, bias_c, batch, seq):
    n_sel = seq // B_SEL_BLOCK
    n_cmp_pad = seq // B_CMP_STRIDE
    n_cmp = (seq - B_CMP_LEN) // B_CMP_STRIDE + 1
    c = np.arange(n_cmp_pad)[None, :] * B_CMP_STRIDE
    j = np.arange(n_sel)[:, None] * B_SEL_BLOCK
    c2s = ((c < j + B_SEL_BLOCK) & (c + B_CMP_LEN > j) & (np.arange(n_cmp_pad)[None, :] < n_cmp)).astype(np.float32)
    kern = functools.partial(_b_cmp_attn_kernel, top_n=min(B_TOP_N, n_sel))
    return pl.pallas_call(
        kern,
        out_shape=(jax.ShapeDtypeStruct((batch, N_HEADS, B_HEAD_DIM, seq), F32),
                   jax.ShapeDtypeStruct((batch, B_KV_HEADS, n_sel, seq), BF16)),
        grid=(batch, B_KV_HEADS, seq // B_SWEEP),
        in_specs=[pl.BlockSpec((1, B_GROUP, B_HEAD_DIM, B_SWEEP), lambda b, n, i: (b, n, 0, i)),
                  pl.BlockSpec((1, 1, n_cmp_pad, B_HEAD_DIM), lambda b, n, i: (b, n, 0, 0)),
                  pl.BlockSpec((1, 1, B_HEAD_DIM, n_cmp_pad), lambda b, n, i: (b, n, 0, 0)),
                  pl.BlockSpec((B_GROUP, 1, bias_c.shape[2], B_SWEEP), lambda b, n, i: (n, 0, 0, 0)),
                  pl.BlockSpec((n_sel, n_cmp_pad), lambda b, n, i: (0, 0))],
        out_specs=(pl.BlockSpec((1, B_GROUP, B_HEAD_DIM, B_SWEEP), lambda b, n, i: (b, n, 0, i)),
                   pl.BlockSpec((1, 1, n_sel, B_SWEEP), lambda b, n, i: (b, n, 0, i))),
        scratch_shapes=[pltpu.VMEM((n_sel, B_SWEEP), F32)],
        compiler_params=_params("parallel", "parallel", "arbitrary"),
        name="b_cmp_attn",
    )(qt, kc, vc.transpose(0, 1, 3, 2), bias_c, jnp.asarray(c2s, BF16))


def _b_sparse_kernel(qt_ref, ka_ref, vs_ref, wk_ref, vw_ref, sel_ref, bs_ref, bw_ref, oc_ref, gate_ref,
                     o_ref, acc_ref, sa_ref, sb_ref, *, delta_max, win_tiles):
    tq = B_SWEEP
    dh = B_HEAD_DIM
    cols = B_GROUP * tq
    qi = pl.program_id(2)
    n_tiles = ka_ref.shape[2] // tq
    qt = jnp.concatenate([qt_ref[0, g] for g in range(B_GROUP)], axis=1)
    q_aug = jnp.concatenate([qt, jnp.concatenate([sel_ref[0, 0]] * B_GROUP, axis=1)], axis=0)

    def tile_start(kt):
        return pl.multiple_of(jnp.clip(kt, 0, n_tiles - 1) * tq, tq)

    def normalised(acc):
        return acc[:dh] * (1.0 / acc[dh:dh + 1])

    def sel_bias(kt):
        d = jnp.clip(qi - kt, -1, delta_max) + 1
        return jnp.concatenate([bs_ref[g, d] for g in range(B_GROUP)], axis=1)

    def sel_scores(kt):
        return _dot(ka_ref[0, 0, pl.ds(tile_start(kt), tq), :], q_aug).astype(BF16) + sel_bias(kt)

    def consume(s_buf, kt, m_old):
        s = s_buf[...]
        m_new = jnp.maximum(m_old, jnp.max(s, axis=0, keepdims=True).astype(F32))
        alpha = jnp.exp2(m_old - m_new)
        p = jnp.exp2(s - m_new.astype(BF16))
        acc_ref[...] = alpha * acc_ref[...] + _dot(vs_ref[0, 0, :, pl.ds(tile_start(kt), tq)], p)
        return m_new

    acc_ref[...] = jnp.zeros(acc_ref.shape, F32)
    sa_ref[...] = sel_scores(0)

    def pair(j, m):
        kt = 2 * j
        sb_ref[...] = sel_scores(kt + 1)
        m = consume(sa_ref, kt, m)
        sa_ref[...] = sel_scores(kt + 2)
        return consume(sb_ref, kt + 1, m)

    lax.fori_loop(0, (qi + 2) // 2, pair, jnp.full((1, cols), NEG_INF, F32))
    o_s = normalised(acc_ref[...])

    tiles = []
    for u in range(win_tiles):
        kt = qi - (win_tiles - 1) + u
        d = jnp.where(kt >= 0, qi - kt, -1) + 1
        bias = jnp.concatenate([bw_ref[g, d] for g in range(B_GROUP)], axis=1)
        tiles.append((_dot(wk_ref[0, 0, pl.ds(tile_start(kt), tq), :], qt).astype(BF16) + bias, kt))
    m = None
    for s, _ in tiles:
        tile_max = jnp.max(s, axis=0, keepdims=True)
        m = tile_max if m is None else jnp.maximum(m, tile_max)
    acc = None
    for s, kt in tiles:
        pv = _dot(vw_ref[0, 0, :, pl.ds(tile_start(kt), tq)], jnp.exp2(s - m))
        acc = pv if acc is None else acc + pv
    o_w = normalised(acc)

    gate = gate_ref[0, 0]
    merged = []
    for g in range(B_GROUP):
        cs = slice(g * tq, (g + 1) * tq)
        merged.append(gate[3 * g:3 * g + 1] * oc_ref[0, g] + gate[3 * g + 1:3 * g + 2] * o_s[:, cs]
                      + gate[3 * g + 2:3 * g + 3] * o_w[:, cs])
    for pair in range(B_GROUP // 2):
        both = jnp.concatenate([merged[2 * pair], merged[2 * pair + 1]], axis=0)
        o_ref[0, :, pair * 2 * dh:(pair + 1) * 2 * dh] = both.T.astype(o_ref.dtype)


def _b_sparse(qt, ka, vs, wk, vw, sel, bias_s, bias_w, oc, gate, batch, seq):
    n_sel = seq // B_SEL_BLOCK
    dh = B_HEAD_DIM
    n_ds = bias_s.shape[1]
    n_dw = bias_w.shape[1]
    vrows = vs.shape[2]
    kern = functools.partial(_b_sparse_kernel, delta_max=n_ds - 2, win_tiles=n_dw - 1)
    whole = lambda rows, width: pl.BlockSpec((1, 1, rows, width), lambda b, n, i: (b, n, 0, 0))
    return pl.pallas_call(
        kern,
        out_shape=jax.ShapeDtypeStruct((batch, seq, N_HEADS * dh), BF16),
        grid=(batch, B_KV_HEADS, seq // B_SWEEP),
        in_specs=[pl.BlockSpec((1, B_GROUP, dh, B_SWEEP), lambda b, n, i: (b, n, 0, i)),
                  whole(seq, dh + n_sel), whole(vrows, seq), whole(seq, dh), whole(vrows, seq),
                  pl.BlockSpec((1, 1, n_sel, B_SWEEP), lambda b, n, i: (b, n, 0, i)),
                  pl.BlockSpec((B_GROUP, n_ds, B_SWEEP, B_SWEEP), lambda b, n, i: (n, 0, 0, 0)),
                  pl.BlockSpec((B_GROUP, n_dw, B_SWEEP, B_SWEEP), lambda b, n, i: (n, 0, 0, 0)),
                  pl.BlockSpec((1, B_GROUP, dh, B_SWEEP), lambda b, n, i: (b, n, 0, i)),
                  pl.BlockSpec((1, 1, B_GATE_ROWS, B_SWEEP), lambda b, n, i: (b, n, 0, i))],
        out_specs=pl.BlockSpec((1, B_SWEEP, B_GROUP * dh), lambda b, n, i: (b, i, n)),
        scratch_shapes=[pltpu.VMEM((vrows, B_GROUP * B_SWEEP), F32),
                        pltpu.VMEM((B_SWEEP, B_GROUP * B_SWEEP), BF16),
                        pltpu.VMEM((B_SWEEP, B_GROUP * B_SWEEP), BF16)],
        compiler_params=_params("parallel", "parallel", "arbitrary"),
        name="b_sparse",
    )(qt, ka, vs, wk, vw, sel, bias_s, bias_w, oc, gate)


def _mixer_b(x, rel_bias, norm1, w_in, q_gain, k_gain, cmp_pos, cmp_w1, cmp_w2, w_out, batch, seq):
    d = w_in.shape[0]
    qkv_width = w_in.shape[1] - 3 * N_HEADS
    gate_w = w_in[:, qkv_width:].reshape(d, B_KV_HEADS, 3 * B_GROUP)
    gate_w = jnp.pad(gate_w, ((0, 0), (0, 0), (0, B_GATE_ROWS - 3 * B_GROUP))).reshape(d, B_KV_HEADS * B_GATE_ROWS)
    w_pad = jnp.concatenate([w_in[:, :qkv_width], gate_w], axis=1)
    w_pad = jnp.pad(w_pad, ((0, 0), (0, B_PROJ_WIDTH - w_pad.shape[1]))).astype(BF16)
    proj = _norm_matmul(x, norm1, w_pad, BF16, 512).reshape(batch, seq, B_PROJ_WIDTH)
    qt, ck, cv, ka, vs, wk, vw, gate_t = _b_prep(proj, q_gain, k_gain, batch, seq)
    kc, vc = _b_compress(ck, cv, cmp_pos, cmp_w1, cmp_w2, k_gain[0], batch, seq)
    last_tile = seq // B_SWEEP - 1
    bias_c = _bias_tiles(rel_bias, 1, seq // B_CMP_STRIDE + last_tile * (B_SWEEP // B_CMP_STRIDE), B_SWEEP,
                         base=1 - B_CMP_LEN + last_tile * B_SWEEP, tile_step=0, row_step=-B_CMP_STRIDE,
                         col_step=1, dmax=1 << 30)
    oc, sel = _b_cmp_attn(qt, kc, vc, bias_c, batch, seq)
    delta_max = min(seq // B_SWEEP - 1, -(-(_THRESHOLDS[-1] + B_SWEEP - 1) // B_SWEEP))
    bias_s = _bias_tiles(rel_bias, delta_max + 2, B_SWEEP, B_SWEEP, base=-B_SWEEP, tile_step=B_SWEEP,
                         row_step=-1, col_step=1, dmax=1 << 30, dtype=BF16)
    win_tiles = (B_WINDOW - 1 + B_SWEEP - 1) // B_SWEEP + 1
    bias_w = _bias_tiles(rel_bias, win_tiles + 1, B_SWEEP, B_SWEEP, base=-B_SWEEP, tile_step=B_SWEEP,
                         row_step=-1, col_step=1, dmax=B_WINDOW - 1, dtype=BF16)
    o = _b_sparse(qt, ka, vs, wk, vw, sel, bias_s, bias_w, oc, gate_t, batch, seq)
    return _matmul_residual(o.reshape(batch * seq, -1), w_out.astype(BF16), x)


def _c_conv_kernel(cur_ref, halo_ref, w_ref, sm_ref, alog_ref, dtb_ref, qkv_ref, bg_ref):
    ts = cur_ref.shape[1]
    keep = jnp.where(pl.program_id(1) == 0, 0.0, 1.0)
    dk = C_HEAD_DIM
    for c in range(3 * C_HEADS):
        sl = slice(c * dk, (c + 1) * dk)
        xe = jnp.concatenate([halo_ref[0, :, sl].astype(F32) * keep, cur_ref[0, :, sl].astype(F32)], axis=0)
        y = None
        for j in range(C_CONV):
            off = 8 - (C_CONV - 1) + j
            term = w_ref[j:j + 1, sl] * xe[off:off + ts]
            y = term if y is None else y + term
        y = y * jax.nn.sigmoid(y)
        if c < 2 * C_HEADS:
            y = y * lax.rsqrt(jnp.sum(y * y, axis=-1, keepdims=True) + RMS_EPS)
        if c < C_HEADS:
            y = y * (dk ** -0.5)
        qkv_ref[0, :, sl] = y
    sm = sm_ref[0]
    a = sm + dtb_ref[...]
    softplus = jnp.maximum(a, 0.0) + jnp.log1p(jnp.exp(-jnp.abs(a)))
    g = -jnp.exp(alog_ref[...]) * softplus
    lane = lax.broadcasted_iota(jnp.int32, sm.shape, 1)
    bg_ref[0] = jnp.where(lane < C_HEADS, jax.nn.sigmoid(sm), g)


def _c_conv(proj, small, conv_w, a_log, dt_bias, batch, seq):
    ts = 256
    width = 3 * C_WIDTH
    pad = lambda v: jnp.pad(v, (C_HEADS, 128 - 2 * C_HEADS)).reshape(1, 128)
    return pl.pallas_call(
        _c_conv_kernel,
        out_shape=(jax.ShapeDtypeStruct((batch, seq, width), F32),
                   jax.ShapeDtypeStruct((batch, seq, 128), F32)),
        grid=(batch, seq // ts),
        in_specs=[pl.BlockSpec((1, ts, width), lambda b, i: (b, i, 0)),
                  pl.BlockSpec((1, 8, width), lambda b, i: (b, jnp.maximum(i * (ts // 8) - 1, 0), 0)),
                  pl.BlockSpec((C_CONV, width), lambda b, i: (0, 0)),
                  pl.BlockSpec((1, ts, 128), lambda b, i: (b, i, 0)),
                  pl.BlockSpec((1, 128), lambda b, i: (0, 0)),
                  pl.BlockSpec((1, 128), lambda b, i: (0, 0))],
        out_specs=(pl.BlockSpec((1, ts, width), lambda b, i: (b, i, 0)),
                   pl.BlockSpec((1, ts, 128), lambda b, i: (b, i, 0))),
        compiler_params=_params("parallel", "arbitrary"),
        name="c_conv",
    )(proj, proj, conv_w, small, pad(a_log), pad(dt_bias))


def _sum3(x, fn):
    hi = x.astype(BF16)
    r = x - hi.astype(F32)
    mid = r.astype(BF16)
    lo = (r - mid.astype(F32)).astype(BF16)
    return fn(hi) + (fn(mid) + fn(lo))


def _c_chunk_kernel(qkv_ref, bg_ref, bgt_ref, tri_ref, trit_ref, blk_ref, u_ref, w_ref, qg_ref, kg_ref, attn_ref,
                    gc_ref):
    cs = C_CHUNK
    dk = C_HEAD_DIM
    gs = C_GROUP * cs
    row = lax.broadcasted_iota(jnp.int32, (gs, gs), 0)
    col = lax.broadcasted_iota(jnp.int32, (gs, gs), 1)
    same = (row // cs) == (col // cs)
    causal = same & (row >= col)
    strict = same & (row > col)
    eye = jnp.where(row == col, 1.0, 0.0)

    bgc = bg_ref[0]
    tri = tri_ref[...]
    gcum_col = _sum3(bgc, lambda p: _dot(tri, p))
    glast_col = _sum3(bgc, lambda p: _dot(blk_ref[...], p))
    gcum_row = _sum3(bgt_ref[0], lambda p: _dot(p, trit_ref[...]))
    gc_ref[0] = gcum_col
    t_mats, powers = [], []
    for h in range(C_HEADS):
        gc = gcum_col[:, C_HEADS + h:C_HEADS + h + 1]
        gr = gcum_row[C_HEADS + h:C_HEADS + h + 1, :]
        q = qkv_ref[0, :, h * dk:(h + 1) * dk]
        k = qkv_ref[0, :, C_WIDTH + h * dk:C_WIDTH + (h + 1) * dk]
        decay = jnp.exp(jnp.where(causal, gc - gr, NEG_INF))
        k16 = k.astype(BF16)
        low = jnp.where(strict, _dot_nt((k * bgc[:, h:h + 1]).astype(BF16), k16) * decay, 0.0)
        t_mats.append(eye - low)
        powers.append(low.astype(BF16))
        attn = jnp.where(causal, _dot_nt(q.astype(BF16), k16), 0.0) * decay
        attn_ref[0, :, h * gs:(h + 1) * gs] = attn.astype(attn_ref.dtype)
        qg_ref[0, :, h * dk:(h + 1) * dk] = (q * jnp.exp(gc)).astype(qg_ref.dtype)
        glast = glast_col[:, C_HEADS + h:C_HEADS + h + 1]
        kg_ref[0, :, h * dk:(h + 1) * dk] = (k * jnp.exp(glast - gc)).astype(kg_ref.dtype)
    for _ in range(int(math.log2(cs)) - 1):
        powers = [_dot(p, p).astype(BF16) for p in powers]
        t_mats = [t + _dot(t.astype(BF16), p) for t, p in zip(t_mats, powers)]
    for h in range(C_HEADS):
        gc = gcum_col[:, C_HEADS + h:C_HEADS + h + 1]
        beta = bgc[:, h:h + 1]
        k = qkv_ref[0, :, C_WIDTH + h * dk:C_WIDTH + (h + 1) * dk]
        v = qkv_ref[0, :, 2 * C_WIDTH + h * dk:2 * C_WIDTH + (h + 1) * dk]
        t16 = t_mats[h].astype(BF16)
        u_ref[0, :, h * dk:(h + 1) * dk] = _dot(t16, (v * beta).astype(BF16))
        w_ref[0, :, h * dk:(h + 1) * dk] = _dot(t16, (k * beta * jnp.exp(gc)).astype(BF16)).astype(w_ref.dtype)


def _c_chunks(qkv, bg, bgt, batch, seq):
    gs = C_GROUP * C_CHUNK
    idx = np.arange(gs)
    same = (idx[:, None] // C_CHUNK) == (idx[None, :] // C_CHUNK)
    tri = (same & (idx[:, None] >= idx[None, :])).astype(np.float32)
    wide = lambda width: pl.BlockSpec((1, gs, width), lambda b, i: (b, i, 0))
    shape = lambda width, dtype: jax.ShapeDtypeStruct((batch, seq, width), dtype)
    const = pl.BlockSpec((gs, gs), lambda b, i: (0, 0))
    return pl.pallas_call(
        _c_chunk_kernel,
        out_shape=(shape(C_WIDTH, F32), shape(C_WIDTH, BF16), shape(C_WIDTH, BF16), shape(C_WIDTH, BF16),
                   shape(C_HEADS * gs, BF16), shape(128, F32)),
        grid=(batch, seq // gs),
        in_specs=[wide(3 * C_WIDTH), wide(128),
                  pl.BlockSpec((1, 2 * C_HEADS, gs), lambda b, i: (b, 0, i)),
                  const, const, const],
        out_specs=(wide(C_WIDTH),) * 4 + (wide(C_HEADS * gs), wide(128)),
        compiler_params=_params("parallel", "parallel"),
        name="c_chunks",
    )(qkv, bg, bgt, jnp.asarray(tri, BF16), jnp.asarray(tri.T, BF16), jnp.asarray(same, BF16))


def _c_scan_kernel(u_ref, w_ref, qg_ref, kg_ref, attn_ref, gc_ref, o_ref, state_ref, vnew_ref):
    @pl.when(pl.program_id(1) == 0)
    def _():
        state_ref[...] = jnp.zeros_like(state_ref)

    cs = C_CHUNK
    dk = C_HEAD_DIM
    gs = C_GROUP * cs
    vnew_ref[...] = jnp.zeros_like(vnew_ref)
    for c in range(C_GROUP):
        rs = slice(c * cs, (c + 1) * cs)
        decay_last = jnp.exp(gc_ref[0, (c + 1) * cs - 1:(c + 1) * cs, :])
        for h in range(C_HEADS):
            sl = slice(h * dk, (h + 1) * dk)
            state = state_ref[h]
            s16 = state.astype(BF16)
            v_new = u_ref[0, rs, sl] - _dot(w_ref[0, rs, sl], s16)
            v16 = v_new.astype(BF16)
            vnew_ref[h, rs, :] = v16
            o_ref[0, rs, sl] = _dot(qg_ref[0, rs, sl], s16) + _dot(attn_ref[0, rs, h * gs:(h + 1) * gs], vnew_ref[h])
            state_ref[h] = state * decay_last[:, C_HEADS + h:C_HEADS + h + 1] + _dot_tn(kg_ref[0, rs, sl], v16)


def _c_scan(u, w, qg, kg, attn, gc, batch, seq):
    gs = C_GROUP * C_CHUNK
    wide = lambda width: pl.BlockSpec((1, gs, width), lambda b, c: (b, c, 0))
    return pl.pallas_call(
        _c_scan_kernel,
        out_shape=jax.ShapeDtypeStruct((batch, seq, C_WIDTH), F32),
        grid=(batch, seq // gs),
        in_specs=[wide(C_WIDTH)] * 4 + [wide(C_HEADS * gs), wide(128)],
        out_specs=wide(C_WIDTH),
        scratch_shapes=[pltpu.VMEM((C_HEADS, C_HEAD_DIM, C_HEAD_DIM), F32),
                        pltpu.VMEM((C_HEADS, gs, C_HEAD_DIM), BF16)],
        compiler_params=_params("parallel", "arbitrary"),
        name="c_scan",
    )(u, w, qg, kg, attn, gc)


def _c_out_kernel(o_ref, z_ref, g_ref, w_ref, x_ref, out_ref):
    dk = C_HEAD_DIM
    parts = []
    for h in range(C_HEADS):
        sl = slice(h * dk, (h + 1) * dk)
        z = z_ref[:, sl].astype(F32)
        parts.append((_rms(o_ref[:, sl], g_ref[...]) * (z * jax.nn.sigmoid(z))).astype(BF16))
    out_ref[...] = x_ref[...] + _dot(jnp.concatenate(parts, axis=-1), w_ref[...])


def _c_out(o, proj, out_gain, w_out, x):
    m, d = x.shape
    z_block = (3 * C_WIDTH) // C_WIDTH
    row = lambda width: pl.BlockSpec((ROW_TILE, width), lambda i: (i, 0))
    return pl.pallas_call(
        _c_out_kernel,
        out_shape=jax.ShapeDtypeStruct((m, d), F32),
        grid=(m // ROW_TILE,),
        in_specs=[row(C_WIDTH),
                  pl.BlockSpec((ROW_TILE, C_WIDTH), lambda i: (i, z_block)),
                  pl.BlockSpec((1, C_HEAD_DIM), lambda i: (0, 0)),
                  pl.BlockSpec((C_WIDTH, d), lambda i: (0, 0)),
                  row(d)],
        out_specs=row(d),
        compiler_params=_params("parallel"),
        name="c_out",
    )(o, proj, out_gain.reshape(1, -1), w_out, x)


def _mixer_c(x, norm1, w_in, conv_w, a_log, dt_bias, out_gain, w_out, batch, seq):
    main = 4 * C_WIDTH
    proj = _norm_matmul(x, norm1, w_in[:, :main].astype(BF16), BF16, 512)
    w_small = jnp.pad(w_in[:, main:], ((0, 0), (0, 128 - 2 * C_HEADS))).astype(BF16)
    small = _norm_matmul(x, norm1, w_small, F32, 128)
    qkv, bg = _c_conv(proj.reshape(batch, seq, main), small.reshape(batch, seq, 128), conv_w, a_log, dt_bias,
                      batch, seq)
    bgt = bg[:, :, :2 * C_HEADS].transpose(0, 2, 1)
    u, w, qg, kg, attn, gc = _c_chunks(qkv, bg, bgt, batch, seq)
    o = _c_scan(u, w, qg, kg, attn, gc, batch, seq)
    return _c_out(o.reshape(batch * seq, C_WIDTH), proj, out_gain, w_out.astype(BF16), x)


def kernel(x, rel_bias, l0_norm1, l0_a_w_in, l0_a_q_gain, l0_a_k_gain, l0_a_w_out, l0_norm2, l0_ffn_w_gate, l0_ffn_w_up, l0_ffn_w_down, l1_norm1, l1_b_w_in, l1_b_q_gain, l1_b_k_gain, l1_b_cmp_pos, l1_b_cmp_w1, l1_b_cmp_w2, l1_b_w_out, l1_norm2, l1_ffn_w_gate, l1_ffn_w_up, l1_ffn_w_down, l2_norm1, l2_c_w_in, l2_c_conv_w, l2_c_a_log, l2_c_dt_bias, l2_c_out_gain, l2_c_w_out, l2_norm2, l2_ffn_w_gate, l2_ffn_w_up, l2_ffn_w_down, l3_norm1, l3_a_w_in, l3_a_q_gain, l3_a_k_gain, l3_a_w_out, l3_norm2, l3_ffn_w_gate, l3_ffn_w_up, l3_ffn_w_down):
    batch, seq, d = x.shape
    h = x.reshape(batch * seq, d)

    def ffn(h, norm2, w_gate, w_up, w_down):
        return _ffn(h, norm2, w_gate.astype(BF16), w_up.astype(BF16), w_down.astype(BF16))

    a_biases = _a_biases(rel_bias)
    h = _mixer_a(h, a_biases, l0_norm1, l0_a_w_in, l0_a_q_gain, l0_a_k_gain, l0_a_w_out, batch, seq)
    h = ffn(h, l0_norm2, l0_ffn_w_gate, l0_ffn_w_up, l0_ffn_w_down)
    h = _mixer_b(h, rel_bias, l1_norm1, l1_b_w_in, l1_b_q_gain, l1_b_k_gain, l1_b_cmp_pos, l1_b_cmp_w1,
                 l1_b_cmp_w2, l1_b_w_out, batch, seq)
    h = ffn(h, l1_norm2, l1_ffn_w_gate, l1_ffn_w_up, l1_ffn_w_down)
    h = _mixer_c(h, l2_norm1, l2_c_w_in, l2_c_conv_w, l2_c_a_log, l2_c_dt_bias, l2_c_out_gain, l2_c_w_out,
                 batch, seq)
    h = ffn(h, l2_norm2, l2_ffn_w_gate, l2_ffn_w_up, l2_ffn_w_down)
    h = _mixer_a(h, a_biases, l3_norm1, l3_a_w_in, l3_a_q_gain, l3_a_k_gain, l3_a_w_out, batch, seq)
    h = ffn(h, l3_norm2, l3_ffn_w_gate, l3_ffn_w_up, l3_ffn_w_down)
    return h.reshape(batch, seq, d)
```

```python
import functools
import math

import numpy as np
import jax
import jax.numpy as jnp
from jax import lax
from jax.experimental import pallas as pl
from jax.experimental.pallas import tpu as pltpu

D_MODEL = 1024
RMS_EPS = 1e-6
NEG_INF = -1e30
TINY = 1e-30
FORCE_SCORE = 1e9

N_BUCKETS = 32
REL_MAX_DISTANCE = 2048
N_HEADS = 16

A_GROUPS = ((128, 1), (512, 4), (2048, 16))
A_HEAD_DIM = 64
A_Q_BLOCK = 128
A_PROJ_TILE = 512
A_STAT_WIDTH = 256
A_BLOCKS_PER_STEP = 2

B_KV_HEADS = 4
B_GROUP = 4
B_HEAD_DIM = 64
B_CMP_LEN = 32
B_CMP_STRIDE = 16
B_CMP_HIDDEN = 256
B_SEL_BLOCK = 64
B_TOP_N = 16
B_WINDOW = 512
B_TILE = 128
B_SWEEP = 256
B_PROJ_WIDTH = 3072
B_GATE_ROWS = 16

C_HEADS = 8
C_HEAD_DIM = 128
C_WIDTH = C_HEADS * C_HEAD_DIM
C_CONV = 4
C_CHUNK = 64
C_GROUP = 4

FFN_HIDDEN = 2816
FFN_TILE = 1024

ROW_TILE = 512
VMEM_LIMIT = 48 * 1024 * 1024

LOG2E = math.log2(math.e)

F32 = jnp.float32
BF16 = jnp.bfloat16

NT_DIMS = (((1,), (1,)), ((), ()))
TN_DIMS = (((0,), (0,)), ((), ()))


def _params(*semantics):
    return pltpu.CompilerParams(dimension_semantics=semantics, vmem_limit_bytes=VMEM_LIMIT)


def _dot(a, b):
    return jnp.dot(a, b, preferred_element_type=F32)


def _dot_nt(a, b):
    return lax.dot_general(a, b, NT_DIMS, preferred_element_type=F32)


def _dot_tn(a, b):
    return lax.dot_general(a, b, TN_DIMS, preferred_element_type=F32)


def _rms(x, gain):
    return x * lax.rsqrt(jnp.mean(x * x, axis=-1, keepdims=True) + RMS_EPS) * gain


def _bucket_thresholds():
    d = np.arange(1 << 15)
    max_exact = N_BUCKETS // 2
    d_f = np.maximum(d, 1).astype(np.float32)
    large = max_exact + (np.log(d_f / np.float32(max_exact)) / np.float32(math.log(REL_MAX_DISTANCE / max_exact))
                         * np.float32(N_BUCKETS - max_exact)).astype(np.int32)
    bucket = np.where(d < max_exact, d, np.minimum(large, N_BUCKETS - 1))
    return [int(np.argmax(bucket >= k)) if np.any(bucket >= k) else int(1 << 30) for k in range(N_BUCKETS)]


_THRESHOLDS = _bucket_thresholds()


def _bias_tile_kernel(tbl_ref, o_ref, *, base, tile_step, row_step, col_step, dmax, dil):
    h = pl.program_id(0)
    t = pl.program_id(1)
    shape = o_ref.shape[2:]
    i = lax.broadcasted_iota(jnp.int32, shape, 0)
    j = lax.broadcasted_iota(jnp.int32, shape, 1)
    dist = base + tile_step * t + row_step * i + col_step * j
    d = dist * dil
    val = jnp.full(shape, tbl_ref[0, h], F32)
    for k in range(1, N_BUCKETS):
        val = jnp.where(d >= _THRESHOLDS[k], tbl_ref[k, h], val)
    valid = (dist >= 0) & (dist <= dmax)
    o_ref[0, 0] = jnp.where(valid, val * LOG2E, NEG_INF).astype(o_ref.dtype)


def _bias_tiles(rel_bias, n_tiles, rows, cols, *, base, tile_step, row_step, col_step, dmax, dil=1, dtype=F32):
    kern = functools.partial(_bias_tile_kernel, base=base, tile_step=tile_step, row_step=row_step,
                             col_step=col_step, dmax=dmax, dil=dil)
    return pl.pallas_call(
        kern,
        out_shape=jax.ShapeDtypeStruct((N_HEADS, n_tiles, rows, cols), dtype),
        grid=(N_HEADS, n_tiles),
        in_specs=[pl.BlockSpec(memory_space=pltpu.SMEM)],
        out_specs=pl.BlockSpec((1, 1, rows, cols), lambda h, t: (h, t, 0, 0)),
        compiler_params=_params("parallel", "parallel"),
        name="bias_tiles",
    )(rel_bias)


def _resident(shape):
    return pl.BlockSpec(shape, lambda i: (0,) * len(shape), pipeline_mode=pl.Buffered(1))


def _norm_matmul_kernel(x_ref, g_ref, w_ref, o_ref, *, tn):
    h = _rms(x_ref[...], g_ref[...]).astype(BF16)
    for j in range(w_ref.shape[1] // tn):
        o_ref[:, j * tn:(j + 1) * tn] = _dot(h, w_ref[:, j * tn:(j + 1) * tn]).astype(o_ref.dtype)


def _norm_matmul(x, gain, w, out_dtype, tn):
    m, d = x.shape
    n = w.shape[1]
    return pl.pallas_call(
        functools.partial(_norm_matmul_kernel, tn=tn),
        out_shape=jax.ShapeDtypeStruct((m, n), out_dtype),
        grid=(m // ROW_TILE,),
        in_specs=[pl.BlockSpec((ROW_TILE, d), lambda i: (i, 0)),
                  _resident((1, d)),
                  _resident((d, n))],
        out_specs=pl.BlockSpec((ROW_TILE, n), lambda i: (i, 0)),
        compiler_params=_params("parallel"),
        name="norm_matmul",
    )(x, gain.reshape(1, d), w)


def _matmul_residual_kernel(a_ref, w_ref, x_ref, o_ref):
    o_ref[...] = x_ref[...] + _dot(a_ref[...], w_ref[...])


def _matmul_residual(a, w, x):
    m, k = a.shape
    d = w.shape[1]
    return pl.pallas_call(
        _matmul_residual_kernel,
        out_shape=jax.ShapeDtypeStruct((m, d), F32),
        grid=(m // ROW_TILE,),
        in_specs=[pl.BlockSpec((ROW_TILE, k), lambda i: (i, 0)),
                  pl.BlockSpec((k, d), lambda i: (0, 0)),
                  pl.BlockSpec((ROW_TILE, d), lambda i: (i, 0))],
        out_specs=pl.BlockSpec((ROW_TILE, d), lambda i: (i, 0)),
        compiler_params=_params("parallel"),
        name="matmul_residual",
    )(a, w, x)


def _ffn_kernel(x_ref, g_ref, wg_ref, wu_ref, wd_ref, o_ref):
    x = x_ref[...]
    h = _rms(x, g_ref[...]).astype(BF16)
    hidden = wg_ref.shape[1]
    acc = x
    for lo in range(0, hidden, FFN_TILE):
        hi = min(lo + FFN_TILE, hidden)
        a = _dot(h, wg_ref[:, lo:hi])
        b = _dot(h, wu_ref[:, lo:hi])
        acc = acc + _dot((a * jax.nn.sigmoid(a) * b).astype(BF16), wd_ref[lo:hi, :])
    o_ref[...] = acc


def _ffn(x, gain, w_gate, w_up, w_down):
    m, d = x.shape
    hidden = w_gate.shape[1]
    return pl.pallas_call(
        _ffn_kernel,
        out_shape=jax.ShapeDtypeStruct((m, d), F32),
        grid=(m // ROW_TILE,),
        in_specs=[pl.BlockSpec((ROW_TILE, d), lambda i: (i, 0)),
                  _resident((1, d)),
                  _resident((d, hidden)), _resident((d, hidden)), _resident((hidden, d))],
        out_specs=pl.BlockSpec((ROW_TILE, d), lambda i: (i, 0)),
        compiler_params=_params("parallel"),
        name="ffn",
    )(x, gain.reshape(1, d), w_gate, w_up, w_down)


def _a_proj_kernel(x_ref, g_ref, w_ref, qg_ref, kg_ref, o_ref, h_ref, x_scr, *, dil):
    rows = ROW_TILE // dil
    xn = _rms(x_ref[...], g_ref[...])
    if dil == 1:
        h_ref[...] = xn.astype(BF16)
    else:
        slabs = xn.shape[1] // 128
        for c in range(slabs):
            x_scr[c] = xn[:, c * 128:(c + 1) * 128]
        for r in range(dil):
            picked = [x_scr[c, pl.ds(r, rows, stride=dil), :] for c in range(slabs)]
            h_ref[r * rows:(r + 1) * rows, :] = jnp.concatenate(picked, axis=1).astype(BF16)
    h = h_ref[...]
    width = w_ref.shape[1]
    hd = width // 3
    low = lax.broadcasted_iota(jnp.int32, (ROW_TILE, 128), 1) < A_HEAD_DIM
    for j in range(width // A_PROJ_TILE):
        res = _dot(h, w_ref[:, j * A_PROJ_TILE:(j + 1) * A_PROJ_TILE])
        kind = (j * A_PROJ_TILE) // hd
        if kind < 2:
            parts = []
            for c in range(A_PROJ_TILE // 128):
                y = res[:, c * 128:(c + 1) * 128]
                sq = y * y
                tot = jnp.sum(sq, axis=-1, keepdims=True)
                lo = jnp.sum(jnp.where(low, sq, 0.0), axis=-1, keepdims=True)
                ss = jnp.where(low, lo, tot - lo)
                parts.append(y * lax.rsqrt(ss * (1.0 / A_HEAD_DIM) + RMS_EPS))
            res = jnp.concatenate(parts, axis=1) * (qg_ref if kind == 0 else kg_ref)[...]
        res = res.astype(BF16)
        for r in range(dil):
            off = r * width + j * A_PROJ_TILE
            o_ref[:, off:off + A_PROJ_TILE] = res[r * rows:(r + 1) * rows]


def _a_proj(x, gain, w, q_gain, k_gain, dil):
    m, d = x.shape
    width = w.shape[1]
    reps = A_PROJ_TILE // A_HEAD_DIM
    qg = jnp.tile(q_gain * (A_HEAD_DIM ** -0.5 * LOG2E), reps).reshape(1, A_PROJ_TILE)
    kg = jnp.tile(k_gain, reps).reshape(1, A_PROJ_TILE)
    return pl.pallas_call(
        functools.partial(_a_proj_kernel, dil=dil),
        out_shape=jax.ShapeDtypeStruct((m // dil, dil * width), BF16),
        grid=(m // ROW_TILE,),
        in_specs=[pl.BlockSpec((ROW_TILE, d), lambda i: (i, 0)),
                  pl.BlockSpec((1, d), lambda i: (0, 0)),
                  pl.BlockSpec((d, width), lambda i: (0, 0)),
                  pl.BlockSpec((1, A_PROJ_TILE), lambda i: (0, 0)),
                  pl.BlockSpec((1, A_PROJ_TILE), lambda i: (0, 0))],
        out_specs=pl.BlockSpec((ROW_TILE // dil, dil * width), lambda i: (i, 0)),
        scratch_shapes=[pltpu.VMEM((ROW_TILE, d), BF16), pltpu.VMEM((d // 128, ROW_TILE, 128), F32)],
        compiler_params=_params("parallel"),
        name="a_proj",
    )(x, gain.reshape(1, d), w, qg, kg)


def _a_attn_kernel(q_ref, kp_ref, kc_ref, vp_ref, vc_ref, bias_ref, o_ref, stat_ref):
    first = (pl.program_id(2) == 0).astype(jnp.int32)
    nq = A_Q_BLOCK
    lane = lax.broadcasted_iota(jnp.int32, (nq, 128), 1)
    low = lane < A_HEAD_DIM
    ones = jnp.ones((2 * nq, 128), BF16)
    for sub in range(A_BLOCKS_PER_STEP):
        rows = slice(sub * nq, (sub + 1) * nq)
        max_tile = jnp.zeros((nq, 128), F32)
        den_tile = jnp.ones((nq, 128), F32)
        for pair in range(N_HEADS // 2):
            sl = slice(pair * 128, (pair + 1) * 128)
            q = q_ref[0, rows, sl]
            zero = jnp.zeros_like(q)
            qq = jnp.concatenate([jnp.where(low, q, zero), jnp.where(low, zero, q)], axis=0)
            if sub == 0:
                kk = jnp.concatenate([kp_ref[0, :, sl], kc_ref[0, :nq, sl]], axis=0)
                vv = jnp.concatenate([vp_ref[0, :, sl], vc_ref[0, :nq, sl]], axis=0)
                base = 2 * pair + N_HEADS * first
            else:
                kk = kc_ref[0, :, sl]
                vv = vc_ref[0, :, sl]
                base = 2 * pair
            s = _dot_nt(qq, kk) + jnp.concatenate([bias_ref[base], bias_ref[base + 1]], axis=0)
            m = jnp.max(s, axis=-1, keepdims=True)
            acc = _dot(jnp.exp2(s - m).astype(BF16), jnp.concatenate([vv, ones], axis=1))
            o_ref[0, rows, sl] = jnp.where(low, acc[:nq, :128], acc[nq:, :128]).astype(o_ref.dtype)
            first_head = lane == 2 * pair
            second_head = lane == 2 * pair + 1
            max_tile = jnp.where(first_head, m[:nq], jnp.where(second_head, m[nq:], max_tile))
            den_tile = jnp.where(first_head, acc[:nq, 128:], jnp.where(second_head, acc[nq:, 128:], den_tile))
        stat_ref[0, rows, :128] = max_tile
        stat_ref[0, rows, 128:] = den_tile


def _a_attention(proj, bias, dil, batch, seq):
    length = seq // dil
    nblk = length // A_Q_BLOCK
    hd = N_HEADS * A_HEAD_DIM
    pv = proj.reshape(batch, length, dil * 3 * hd)

    per = A_BLOCKS_PER_STEP
    step_rows = per * A_Q_BLOCK
    assert nblk % per == 0

    def spec(off, prev):
        if prev:
            return pl.BlockSpec((1, A_Q_BLOCK, hd), lambda b, r, i: (b, jnp.maximum(per * i - 1, 0), r * 3 + off))
        return pl.BlockSpec((1, step_rows, hd), lambda b, r, i: (b, i, r * 3 + off))

    o, stats = pl.pallas_call(
        _a_attn_kernel,
        out_shape=(jax.ShapeDtypeStruct((batch, length, dil * hd), BF16),
                   jax.ShapeDtypeStruct((batch, length, dil * A_STAT_WIDTH), F32)),
        grid=(batch, dil, nblk // per),
        in_specs=[spec(0, False), spec(1, True), spec(1, False), spec(2, True), spec(2, False),
                  pl.BlockSpec((2 * N_HEADS, A_Q_BLOCK, 2 * A_Q_BLOCK), lambda b, r, i: (0, 0, 0))],
        out_specs=(pl.BlockSpec((1, step_rows, hd), lambda b, r, i: (b, i, r)),
                   pl.BlockSpec((1, step_rows, A_STAT_WIDTH), lambda b, r, i: (b, i, r))),
        compiler_params=_params("parallel", "parallel", "arbitrary"),
        name="a_attention",
    )(pv, pv, pv, pv, pv, bias)
    return o.reshape(batch * length, dil * hd), stats.reshape(batch * length, dil * A_STAT_WIDTH)


def _a_out_kernel(o0_ref, o1_ref, o2_ref, s0_ref, s1_ref, s2_ref, e_ref, w_ref, x_ref, out_ref, o_scr, s_scr):
    hd = N_HEADS * A_HEAD_DIM
    sw = A_STAT_WIDTH
    for g, (o_ref, s_ref) in enumerate(((o0_ref, s0_ref), (o1_ref, s1_ref), (o2_ref, s2_ref))):
        dil = A_GROUPS[g][1]
        rows = ROW_TILE // dil
        for r in range(dil):
            dst = pl.ds(r, rows, stride=dil) if dil > 1 else slice(None)
            s_scr[g, 0, dst, :] = s_ref[:, r * sw:r * sw + 128]
            s_scr[g, 1, dst, :] = s_ref[:, r * sw + 128:(r + 1) * sw]
            for c in range(hd // 128):
                o_scr[g, c, dst, :] = o_ref[:, r * hd + c * 128:r * hd + (c + 1) * 128].astype(F32)
    groups = range(len(A_GROUPS))
    top = functools.reduce(jnp.maximum, [s_scr[g, 0] for g in groups])
    es = [jnp.exp2(s_scr[g, 0] - top) for g in groups]
    inv = 1.0 / sum(es[g] * s_scr[g, 1] for g in groups)
    expand = e_ref[...]
    acc = None
    for g in groups:
        o_g = jnp.concatenate([o_scr[g, c] for c in range(hd // 128)], axis=1)
        term = _dot((es[g] * inv).astype(BF16), expand) * o_g
        acc = term if acc is None else acc + term
    out_ref[...] = x_ref[...] + _dot(acc.astype(BF16), w_ref[...])


def _a_out(outs, stats, w_out, x):
    m, d = x.shape
    hd = N_HEADS * A_HEAD_DIM
    expand = np.zeros((128, hd), np.float32)
    for h in range(N_HEADS):
        expand[h, h * A_HEAD_DIM:(h + 1) * A_HEAD_DIM] = 1.0
    grouped = lambda width: [pl.BlockSpec((ROW_TILE // dil, dil * width), lambda i: (i, 0)) for _, dil in A_GROUPS]
    return pl.pallas_call(
        _a_out_kernel,
        out_shape=jax.ShapeDtypeStruct((m, d), F32),
        grid=(m // ROW_TILE,),
        in_specs=grouped(hd) + grouped(A_STAT_WIDTH) + [
            pl.BlockSpec((128, hd), lambda i: (0, 0)),
            pl.BlockSpec((hd, d), lambda i: (0, 0)),
            pl.BlockSpec((ROW_TILE, d), lambda i: (i, 0))],
        out_specs=pl.BlockSpec((ROW_TILE, d), lambda i: (i, 0)),
        scratch_shapes=[pltpu.VMEM((len(A_GROUPS), hd // 128, ROW_TILE, 128), F32),
                        pltpu.VMEM((len(A_GROUPS), 2, ROW_TILE, 128), F32)],
        compiler_params=_params("parallel"),
        name="a_out",
    )(*outs, *stats, jnp.asarray(expand, BF16), w_out, x)


def _a_biases(rel_bias):
    biases = []
    for window, dil in A_GROUPS:
        steps = window // dil
        assert steps == A_Q_BLOCK
        bias = _bias_tiles(rel_bias, 1, A_Q_BLOCK, 2 * A_Q_BLOCK, base=A_Q_BLOCK, tile_step=0, row_step=1,
                           col_step=-1, dmax=steps, dil=dil)[:, 0]
        biases.append(jnp.concatenate([bias, bias.at[:, :, :A_Q_BLOCK].set(NEG_INF)], axis=0))
    return biases


def _mixer_a(x, biases, norm1, w_in, q_gain, k_gain, w_out, batch, seq):
    w_in = w_in.astype(BF16)
    group_width = 3 * N_HEADS * A_HEAD_DIM
    outs, stats = [], []
    for gi, (_, dil) in enumerate(A_GROUPS):
        assert (seq // dil) % A_Q_BLOCK == 0 and seq % ROW_TILE == 0
        bias = biases[gi]
        proj = _a_proj(x, norm1, w_in[:, gi * group_width:(gi + 1) * group_width], q_gain[gi], k_gain[gi], dil)
        o, stat = _a_attention(proj, bias, dil, batch, seq)
        outs.append(o)
        stats.append(stat)
    return _a_out(outs, stats, w_out.astype(BF16), x)


def _b_prep_kernel(p_ref, qg_ref, kg_ref, qt_ref, ck_ref, cv_ref, ka_ref, vs_ref, wk_ref, vw_ref, gate_ref):
    dh = B_HEAD_DIM
    ts = p_ref.shape[1]
    n_sel = ka_ref.shape[3] - dh
    low = lax.broadcasted_iota(jnp.int32, (ts, 128), 1) < dh

    def slab(c):
        return p_ref[0, :, c * 128:(c + 1) * 128].astype(F32)

    def normed(x, gain):
        sq = x * x
        tot = jnp.sum(sq, axis=-1, keepdims=True)
        first = jnp.sum(jnp.where(low, sq, 0.0), axis=-1, keepdims=True)
        ss = jnp.where(low, first, tot - first)
        return x * lax.rsqrt(ss * (1.0 / dh) + RMS_EPS) * jnp.concatenate([gain, gain], axis=1)

    def halves(x):
        return x[:, :dh], x[:, dh:]

    qg = qg_ref[...] * (dh ** -0.5 * LOG2E)
    for c in range(N_HEADS // 2):
        xt = normed(slab(c), qg).T.astype(BF16)
        qt_ref[0, 2 * c] = xt[:dh]
        qt_ref[0, 2 * c + 1] = xt[dh:]
    base = N_HEADS // 2
    pairs = B_KV_HEADS // 2
    pos = pl.program_id(1) * ts + lax.broadcasted_iota(jnp.int32, (ts, 128), 0)
    lane = lax.broadcasted_iota(jnp.int32, (ts, 128), 1)
    onehot = jnp.where(lane - dh == pos // B_SEL_BLOCK, 1.0, 0.0)
    ones = jnp.ones((vs_ref.shape[2] - dh, ts), BF16)
    for j in range(pairs):
        for ref, off in ((ck_ref, 0), (cv_ref, pairs)):
            a, b = halves(slab(base + off + j))
            ref[0, 2 * j] = a.astype(BF16)
            ref[0, 2 * j + 1] = b.astype(BF16)
        k_sel = normed(slab(base + 2 * pairs + j), kg_ref[1:2, :])
        for n, k in zip((2 * j, 2 * j + 1), (k_sel, pltpu.roll(k_sel, dh, 1))):
            ka_ref[0, n] = jnp.where(low, k, onehot)[:, :dh + n_sel].astype(BF16)
        for n, k in zip((2 * j, 2 * j + 1), halves(normed(slab(base + 4 * pairs + j), kg_ref[2:3, :]))):
            wk_ref[0, n] = k.astype(BF16)
        for ref, off in ((vs_ref, 3 * pairs), (vw_ref, 5 * pairs)):
            xt = slab(base + off + j).T.astype(BF16)
            for n, v in zip((2 * j, 2 * j + 1), (xt[:dh], xt[dh:])):
                ref[0, n, :dh, :] = v
                ref[0, n, dh:, :] = ones
    gate = jax.nn.sigmoid(slab(base + 6 * pairs)).T
    rows = gate_ref.shape[2]
    for n in range(B_KV_HEADS):
        gate_ref[0, n] = gate[n * rows:(n + 1) * rows]


def _b_prep(proj, q_gain, k_gain, batch, seq):
    ts = 256
    dh = B_HEAD_DIM
    n_sel = seq // B_SEL_BLOCK
    assert dh + n_sel <= 128
    rows_shape = lambda width: jax.ShapeDtypeStruct((batch, B_KV_HEADS, seq, width), BF16)
    rows_spec = lambda width: pl.BlockSpec((1, B_KV_HEADS, ts, width), lambda b, i: (b, 0, i, 0))
    cols_shape = lambda heads, height, dtype: jax.ShapeDtypeStruct((batch, heads, height, seq), dtype)
    cols_spec = lambda heads, height: pl.BlockSpec((1, heads, height, ts), lambda b, i: (b, 0, 0, i))
    return pl.pallas_call(
        _b_prep_kernel,
        out_shape=(cols_shape(N_HEADS, dh, BF16), rows_shape(dh), rows_shape(dh), rows_shape(dh + n_sel),
                   cols_shape(B_KV_HEADS, dh + 16, BF16), rows_shape(dh), cols_shape(B_KV_HEADS, dh + 16, BF16),
                   cols_shape(B_KV_HEADS, B_GATE_ROWS, F32)),
        grid=(batch, seq // ts),
        in_specs=[pl.BlockSpec((1, ts, B_PROJ_WIDTH), lambda b, i: (b, i, 0)),
                  pl.BlockSpec((1, dh), lambda b, i: (0, 0)),
                  pl.BlockSpec((3, dh), lambda b, i: (0, 0))],
        out_specs=(cols_spec(N_HEADS, dh), rows_spec(dh), rows_spec(dh), rows_spec(dh + n_sel),
                   cols_spec(B_KV_HEADS, dh + 16), rows_spec(dh), cols_spec(B_KV_HEADS, dh + 16),
                   cols_spec(B_KV_HEADS, B_GATE_ROWS)),
        compiler_params=_params("parallel", "parallel"),
        name="b_prep",
    )(proj, q_gain.reshape(1, dh), k_gain)


def _b_compress_kernel(tk_ref, tv_ref, pos_ref, w1_ref, w2_ref, kg_ref, kc_ref, vc_ref):
    half = (B_CMP_LEN // 2) * B_HEAD_DIM
    for kv, (t_ref, out_ref) in enumerate(((tk_ref, kc_ref), (tv_ref, vc_ref))):
        t = t_ref[0, 0].astype(F32)
        top = (t + pos_ref[kv, 0:1, :]).astype(BF16)
        bot = (t + pos_ref[kv, 1:2, :]).astype(BF16)
        a1 = _dot(top, w1_ref[kv, :half, :])
        a2 = _dot(bot, w1_ref[kv, half:, :])
        hidden = a1 + pltpu.roll(a2, a2.shape[0] - 1, 0)
        out = _dot(jax.nn.gelu(hidden).astype(BF16), w2_ref[kv])
        if kv == 0:
            out = _rms(out, kg_ref[...])
        out_ref[0, 0] = out.astype(out_ref.dtype)


def _b_compress(ck, cv, cmp_pos, cmp_w1, cmp_w2, k_gain0, batch, seq):
    rows = seq // B_CMP_STRIDE
    half = (B_CMP_LEN // 2) * B_HEAD_DIM
    tk = ck.reshape(batch, B_KV_HEADS, rows, half)
    tv = cv.reshape(batch, B_KV_HEADS, rows, half)
    pos = cmp_pos.reshape(2, 2, half)
    t_spec = pl.BlockSpec((1, 1, rows, half), lambda b, n: (b, n, 0, 0))
    o_spec = pl.BlockSpec((1, 1, rows, B_HEAD_DIM), lambda b, n: (b, n, 0, 0))
    shape = jax.ShapeDtypeStruct((batch, B_KV_HEADS, rows, B_HEAD_DIM), BF16)
    return pl.pallas_call(
        _b_compress_kernel,
        out_shape=(shape, shape),
        grid=(batch, B_KV_HEADS),
        in_specs=[t_spec, t_spec,
                  pl.BlockSpec((2, 2, half), lambda b, n: (0, 0, 0)),
                  pl.BlockSpec((2, 2 * half, B_CMP_HIDDEN), lambda b, n: (0, 0, 0)),
                  pl.BlockSpec((2, B_CMP_HIDDEN, B_HEAD_DIM), lambda b, n: (0, 0, 0)),
                  pl.BlockSpec((1, B_HEAD_DIM), lambda b, n: (0, 0))],
        out_specs=(o_spec, o_spec),
        compiler_params=_params("parallel", "parallel"),
        name="b_compress",
    )(tk, tv, pos, cmp_w1.astype(BF16), cmp_w2.astype(BF16), k_gain0.reshape(1, -1))


def _b_cmp_attn_kernel(qt_ref, kc_ref, vct_ref, bias_ref, c2s_ref, oc_ref, sel_ref, imp_ref, *, top_n):
    tq = B_SWEEP
    n_sel = imp_ref.shape[0]
    qt = jnp.concatenate([qt_ref[0, g] for g in range(B_GROUP)], axis=1)
    n_cmp_pad = kc_ref.shape[2]
    shift = tq // B_CMP_STRIDE
    off = pl.multiple_of((pl.num_programs(2) - 1 - pl.program_id(2)) * shift, shift)
    bias = jnp.concatenate([bias_ref[g, 0, pl.ds(off, n_cmp_pad), :] for g in range(B_GROUP)], axis=1)
    s = _dot(kc_ref[0, 0], qt) + bias
    m = jnp.max(s, axis=0, keepdims=True)
    e = jnp.exp2(s - m)
    z = jnp.maximum(jnp.sum(e, axis=0, keepdims=True), TINY)
    pos = pl.program_id(2) * tq + lax.broadcasted_iota(jnp.int32, (1, tq), 1)
    sees_any = jnp.concatenate([pos >= B_CMP_LEN - 1] * B_GROUP, axis=1)
    p = e * jnp.where(sees_any, 1.0 / z, 0.0)
    oct = _dot(vct_ref[0, 0], p.astype(BF16))
    for g in range(B_GROUP):
        oc_ref[0, g] = oct[:, g * tq:(g + 1) * tq]

    p_sum = p[:, 0:tq] + p[:, tq:2 * tq] + p[:, 2 * tq:3 * tq] + p[:, 3 * tq:4 * tq]
    hi = p_sum.astype(BF16)
    lo = (p_sum - hi.astype(F32)).astype(BF16)
    c2s = c2s_ref[...]
    imp = _dot(c2s, hi) + _dot(c2s, lo)

    t = pl.program_id(2) * tq + lax.broadcasted_iota(jnp.int32, (n_sel, tq), 1)
    blk = lax.broadcasted_iota(jnp.int32, (n_sel, tq), 0)
    cur = t // B_SEL_BLOCK
    forced = (blk == 0) | (blk == cur) | (blk == cur - 1)
    imp = jnp.where(forced, FORCE_SCORE, jnp.where(blk * B_SEL_BLOCK <= t, imp, NEG_INF))
    imp_ref[...] = imp

    def count(i, rank):
        row = imp_ref[pl.ds(i, 1), :]
        ahead = jnp.where(row > imp, 1.0, jnp.where(row == imp, jnp.where(blk > i, 1.0, 0.0), 0.0))
        return rank + ahead

    n_live = jnp.minimum(n_sel, (pl.program_id(2) + 1) * (tq // B_SEL_BLOCK))
    rank = lax.fori_loop(0, n_live, count, jnp.zeros((n_sel, tq), F32))
    sel_ref[0, 0] = jnp.where(rank < top_n, 0.0, NEG_INF).astype(sel_ref.dtype)


def _b_cmp_attn(qt, kc, vc, bias_c, batch, seq):
    n_sel = seq // B_SEL_BLOCK
    n_cmp_pad = seq // B_CMP_STRIDE
    n_cmp = (seq - B_CMP_LEN) // B_CMP_STRIDE + 1
    c = np.arange(n_cmp_pad)[None, :] * B_CMP_STRIDE
    j = np.arange(n_sel)[:, None] * B_SEL_BLOCK
    c2s = ((c < j + B_SEL_BLOCK) & (c + B_CMP_LEN > j) & (np.arange(n_cmp_pad)[None, :] < n_cmp)).astype(np.float32)
    kern = functools.partial(_b_cmp_attn_kernel, top_n=min(B_TOP_N, n_sel))
    return pl.pallas_call(
        kern,
        out_shape=(jax.ShapeDtypeStruct((batch, N_HEADS, B_HEAD_DIM, seq), F32),
                   jax.ShapeDtypeStruct((batch, B_KV_HEADS, n_sel, seq), BF16)),
        grid=(batch, B_KV_HEADS, seq // B_SWEEP),
        in_specs=[pl.BlockSpec((1, B_GROUP, B_HEAD_DIM, B_SWEEP), lambda b, n, i: (b, n, 0, i)),
                  pl.BlockSpec((1, 1, n_cmp_pad, B_HEAD_DIM), lambda b, n, i: (b, n, 0, 0)),
                  pl.BlockSpec((1, 1, B_HEAD_DIM, n_cmp_pad), lambda b, n, i: (b, n, 0, 0)),
                  pl.BlockSpec((B_GROUP, 1, bias_c.shape[2], B_SWEEP), lambda b, n, i: (n, 0, 0, 0)),
                  pl.BlockSpec((n_sel, n_cmp_pad), lambda b, n, i: (0, 0))],
        out_specs=(pl.BlockSpec((1, B_GROUP, B_HEAD_DIM, B_SWEEP), lambda b, n, i: (b, n, 0, i)),
                   pl.BlockSpec((1, 1, n_sel, B_SWEEP), lambda b, n, i: (b, n, 0, i))),
        scratch_shapes=[pltpu.VMEM((n_sel, B_SWEEP), F32)],
        compiler_params=_params("parallel", "parallel", "arbitrary"),
        name="b_cmp_attn",
    )(qt, kc, vc.transpose(0, 1, 3, 2), bias_c, jnp.asarray(c2s, BF16))


def _b_sparse_kernel(qt_ref, ka_ref, vs_ref, wk_ref, vw_ref, sel_ref, bs_ref, bw_ref, oc_ref, gate_ref,
                     o_ref, acc_ref, sa_ref, sb_ref, *, delta_max, win_tiles):
    tq = B_SWEEP
    dh = B_HEAD_DIM
    cols = B_GROUP * tq
    qi = pl.program_id(2)
    n_tiles = ka_ref.shape[2] // tq
    qt = jnp.concatenate([qt_ref[0, g] for g in range(B_GROUP)], axis=1)
    q_aug = jnp.concatenate([qt, jnp.concatenate([sel_ref[0, 0]] * B_GROUP, axis=1)], axis=0)

    def tile_start(kt):
        return pl.multiple_of(jnp.clip(kt, 0, n_tiles - 1) * tq, tq)

    def normalised(acc):
        return acc[:dh] * (1.0 / acc[dh:dh + 1])

    def sel_bias(kt):
        d = jnp.clip(qi - kt, -1, delta_max) + 1
        return jnp.concatenate([bs_ref[g, d] for g in range(B_GROUP)], axis=1)

    def sel_scores(kt):
        return _dot(ka_ref[0, 0, pl.ds(tile_start(kt), tq), :], q_aug).astype(BF16) + sel_bias(kt)

    def consume(s_buf, kt, m_old):
        s = s_buf[...]
        m_new = jnp.maximum(m_old, jnp.max(s, axis=0, keepdims=True).astype(F32))
        alpha = jnp.exp2(m_old - m_new)
        p = jnp.exp2(s - m_new.astype(BF16))
        acc_ref[...] = alpha * acc_ref[...] + _dot(vs_ref[0, 0, :, pl.ds(tile_start(kt), tq)], p)
        return m_new

    acc_ref[...] = jnp.zeros(acc_ref.shape, F32)
    sa_ref[...] = sel_scores(0)

    def pair(j, m):
        kt = 2 * j
        sb_ref[...] = sel_scores(kt + 1)
        m = consume(sa_ref, kt, m)
        sa_ref[...] = sel_scores(kt + 2)
        return consume(sb_ref, kt + 1, m)

    lax.fori_loop(0, (qi + 2) // 2, pair, jnp.full((1, cols), NEG_INF, F32))
    o_s = normalised(acc_ref[...])

    tiles = []
    for u in range(win_tiles):
        kt = qi - (win_tiles - 1) + u
        d = jnp.where(kt >= 0, qi - kt, -1) + 1
        bias = jnp.concatenate([bw_ref[g, d] for g in range(B_GROUP)], axis=1)
        tiles.append((_dot(wk_ref[0, 0, pl.ds(tile_start(kt), tq), :], qt).astype(BF16) + bias, kt))
    m = None
    for s, _ in tiles:
        tile_max = jnp.max(s, axis=0, keepdims=True)
        m = tile_max if m is None else jnp.maximum(m, tile_max)
    acc = None
    for s, kt in tiles:
        pv = _dot(vw_ref[0, 0, :, pl.ds(tile_start(kt), tq)], jnp.exp2(s - m))
        acc = pv if acc is None else acc + pv
    o_w = normalised(acc)

    gate = gate_ref[0, 0]
    merged = []
    for g in range(B_GROUP):
        cs = slice(g * tq, (g + 1) * tq)
        merged.append(gate[3 * g:3 * g + 1] * oc_ref[0, g] + gate[3 * g + 1:3 * g + 2] * o_s[:, cs]
                      + gate[3 * g + 2:3 * g + 3] * o_w[:, cs])
    for pair in range(B_GROUP // 2):
        both = jnp.concatenate([merged[2 * pair], merged[2 * pair + 1]], axis=0)
        o_ref[0, :, pair * 2 * dh:(pair + 1) * 2 * dh] = both.T.astype(o_ref.dtype)


def _b_sparse(qt, ka, vs, wk, vw, sel, bias_s, bias_w, oc, gate, batch, seq):
    n_sel = seq // B_SEL_BLOCK
    dh = B_HEAD_DIM
    n_ds = bias_s.shape[1]
    n_dw = bias_w.shape[1]
    vrows = vs.shape[2]
    kern = functools.partial(_b_sparse_kernel, delta_max=n_ds - 2, win_tiles=n_dw - 1)
    whole = lambda rows, width: pl.BlockSpec((1, 1, rows, width), lambda b, n, i: (b, n, 0, 0))
    return pl.pallas_call(
        kern,
        out_shape=jax.ShapeDtypeStruct((batch, seq, N_HEADS * dh), BF16),
        grid=(batch, B_KV_HEADS, seq // B_SWEEP),
        in_specs=[pl.BlockSpec((1, B_GROUP, dh, B_SWEEP), lambda b, n, i: (b, n, 0, i)),
                  whole(seq, dh + n_sel), whole(vrows, seq), whole(seq, dh), whole(vrows, seq),
                  pl.BlockSpec((1, 1, n_sel, B_SWEEP), lambda b, n, i: (b, n, 0, i)),
                  pl.BlockSpec((B_GROUP, n_ds, B_SWEEP, B_SWEEP), lambda b, n, i: (n, 0, 0, 0)),
                  pl.BlockSpec((B_GROUP, n_dw, B_SWEEP, B_SWEEP), lambda b, n, i: (n, 0, 0, 0)),
                  pl.BlockSpec((1, B_GROUP, dh, B_SWEEP), lambda b, n, i: (b, n, 0, i)),
                  pl.BlockSpec((1, 1, B_GATE_ROWS, B_SWEEP), lambda b, n, i: (b, n, 0, i))],
        out_specs=pl.BlockSpec((1, B_SWEEP, B_GROUP * dh), lambda b, n, i: (b, i, n)),
        scratch_shapes=[pltpu.VMEM((vrows, B_GROUP * B_SWEEP), F32),
                        pltpu.VMEM((B_SWEEP, B_GROUP * B_SWEEP), BF16),
                        pltpu.VMEM((B_SWEEP, B_GROUP * B_SWEEP), BF16)],
        compiler_params=_params("parallel", "parallel", "arbitrary"),
        name="b_sparse",
    )(qt, ka, vs, wk, vw, sel, bias_s, bias_w, oc, gate)


def _mixer_b(x, rel_bias, norm1, w_in, q_gain, k_gain, cmp_pos, cmp_w1, cmp_w2, w_out, batch, seq):
    d = w_in.shape[0]
    qkv_width = w_in.shape[1] - 3 * N_HEADS
    gate_w = w_in[:, qkv_width:].reshape(d, B_KV_HEADS, 3 * B_GROUP)
    gate_w = jnp.pad(gate_w, ((0, 0), (0, 0), (0, B_GATE_ROWS - 3 * B_GROUP))).reshape(d, B_KV_HEADS * B_GATE_ROWS)
    w_pad = jnp.concatenate([w_in[:, :qkv_width], gate_w], axis=1)
    w_pad = jnp.pad(w_pad, ((0, 0), (0, B_PROJ_WIDTH - w_pad.shape[1]))).astype(BF16)
    proj = _norm_matmul(x, norm1, w_pad, BF16, 512).reshape(batch, seq, B_PROJ_WIDTH)
    qt, ck, cv, ka, vs, wk, vw, gate_t = _b_prep(proj, q_gain, k_gain, batch, seq)
    kc, vc = _b_compress(ck, cv, cmp_pos, cmp_w1, cmp_w2, k_gain[0], batch, seq)
    last_tile = seq // B_SWEEP - 1
    bias_c = _bias_tiles(rel_bias, 1, seq // B_CMP_STRIDE + last_tile * (B_SWEEP // B_CMP_STRIDE), B_SWEEP,
                         base=1 - B_CMP_LEN + last_tile * B_SWEEP, tile_step=0, row_step=-B_CMP_STRIDE,
                         col_step=1, dmax=1 << 30)
    oc, sel = _b_cmp_attn(qt, kc, vc, bias_c, batch, seq)
    delta_max = min(seq // B_SWEEP - 1, -(-(_THRESHOLDS[-1] + B_SWEEP - 1) // B_SWEEP))
    bias_s = _bias_tiles(rel_bias, delta_max + 2, B_SWEEP, B_SWEEP, base=-B_SWEEP, tile_step=B_SWEEP,
                         row_step=-1, col_step=1, dmax=1 << 30, dtype=BF16)
    win_tiles = (B_WINDOW - 1 + B_SWEEP - 1) // B_SWEEP + 1
    bias_w = _bias_tiles(rel_bias, win_tiles + 1, B_SWEEP, B_SWEEP, base=-B_SWEEP, tile_step=B_SWEEP,
                         row_step=-1, col_step=1, dmax=B_WINDOW - 1, dtype=BF16)
    o = _b_sparse(qt, ka, vs, wk, vw, sel, bias_s, bias_w, oc, gate_t, batch, seq)
    return _matmul_residual(o.reshape(batch * seq, -1), w_out.astype(BF16), x)


def _c_conv_kernel(cur_ref, halo_ref, w_ref, sm_ref, alog_ref, dtb_ref, qkv_ref, bg_ref, xe_ref):
    ts = cur_ref.shape[1]
    keep = jnp.where(pl.program_id(1) == 0, 0.0, 1.0)
    dk = C_HEAD_DIM
    for c in range(3 * C_HEADS):
        sl = slice(c * dk, (c + 1) * dk)
        xe_ref[c, :8, :] = halo_ref[0, :, sl].astype(F32) * keep
        xe_ref[c, 8:, :] = cur_ref[0, :, sl].astype(F32)
        y = None
        for j in range(C_CONV):
            off = 8 - (C_CONV - 1) + j
            term = w_ref[j:j + 1, sl] * xe_ref[c, off:off + ts, :]
            y = term if y is None else y + term
        y = y * jax.nn.sigmoid(y)
        if c < 2 * C_HEADS:
            y = y * lax.rsqrt(jnp.sum(y * y, axis=-1, keepdims=True) + RMS_EPS)
        if c < C_HEADS:
            y = y * (dk ** -0.5)
        qkv_ref[0, :, sl] = y
    sm = sm_ref[0]
    a = sm + dtb_ref[...]
    softplus = jnp.maximum(a, 0.0) + jnp.log1p(jnp.exp(-jnp.abs(a)))
    g = -jnp.exp(alog_ref[...]) * softplus
    lane = lax.broadcasted_iota(jnp.int32, sm.shape, 1)
    bg_ref[0] = jnp.where(lane < C_HEADS, jax.nn.sigmoid(sm), g)


def _c_conv(proj, small, conv_w, a_log, dt_bias, batch, seq):
    ts = 256
    width = 3 * C_WIDTH
    pad = lambda v: jnp.pad(v, (C_HEADS, 128 - 2 * C_HEADS)).reshape(1, 128)
    return pl.pallas_call(
        _c_conv_kernel,
        out_shape=(jax.ShapeDtypeStruct((batch, seq, width), F32),
                   jax.ShapeDtypeStruct((batch, seq, 128), F32)),
        grid=(batch, seq // ts),
        in_specs=[pl.BlockSpec((1, ts, width), lambda b, i: (b, i, 0)),
                  pl.BlockSpec((1, 8, width), lambda b, i: (b, jnp.maximum(i * (ts // 8) - 1, 0), 0)),
                  pl.BlockSpec((C_CONV, width), lambda b, i: (0, 0)),
                  pl.BlockSpec((1, ts, 128), lambda b, i: (b, i, 0)),
                  pl.BlockSpec((1, 128), lambda b, i: (0, 0)),
                  pl.BlockSpec((1, 128), lambda b, i: (0, 0))],
        out_specs=(pl.BlockSpec((1, ts, width), lambda b, i: (b, i, 0)),
                   pl.BlockSpec((1, ts, 128), lambda b, i: (b, i, 0))),
        scratch_shapes=[pltpu.VMEM((3 * C_HEADS, ts + 8, C_HEAD_DIM), F32)],
        compiler_params=_params("parallel", "arbitrary"),
        name="c_conv",
    )(proj, proj, conv_w, small, pad(a_log), pad(dt_bias))


def _sum3(x, fn):
    hi = x.astype(BF16)
    r = x - hi.astype(F32)
    mid = r.astype(BF16)
    lo = (r - mid.astype(F32)).astype(BF16)
    return fn(hi) + (fn(mid) + fn(lo))


def _c_chunk_kernel(qkv_ref, bg_ref, bgt_ref, tri_ref, trit_ref, blk_ref, u_ref, w_ref, qg_ref, kg_ref, attn_ref,
                    gc_ref):
    cs = C_CHUNK
    dk = C_HEAD_DIM
    gs = C_GROUP * cs
    row = lax.broadcasted_iota(jnp.int32, (gs, gs), 0)
    col = lax.broadcasted_iota(jnp.int32, (gs, gs), 1)
    same = (row // cs) == (col // cs)
    causal = same & (row >= col)
    strict = same & (row > col)
    eye = jnp.where(row == col, 1.0, 0.0)

    bgc = bg_ref[0]
    tri = tri_ref[...]
    gcum_col = _sum3(bgc, lambda p: _dot(tri, p))
    glast_col = _sum3(bgc, lambda p: _dot(blk_ref[...], p))
    gcum_row = _sum3(bgt_ref[0], lambda p: _dot(p, trit_ref[...]))
    gc_ref[0] = gcum_col
    t_mats, powers = [], []
    for h in range(C_HEADS):
        gc = gcum_col[:, C_HEADS + h:C_HEADS + h + 1]
        gr = gcum_row[C_HEADS + h:C_HEADS + h + 1, :]
        q = qkv_ref[0, :, h * dk:(h + 1) * dk]
        k = qkv_ref[0, :, C_WIDTH + h * dk:C_WIDTH + (h + 1) * dk]
        decay = jnp.exp(jnp.where(causal, gc - gr, NEG_INF))
        k16 = k.astype(BF16)
        low = jnp.where(strict, _dot_nt((k * bgc[:, h:h + 1]).astype(BF16), k16) * decay, 0.0)
        t_mats.append(eye - low)
        powers.append(low.astype(BF16))
        attn = jnp.where(causal, _dot_nt(q.astype(BF16), k16), 0.0) * decay
        attn_ref[0, :, h * gs:(h + 1) * gs] = attn.astype(attn_ref.dtype)
        qg_ref[0, :, h * dk:(h + 1) * dk] = (q * jnp.exp(gc)).astype(qg_ref.dtype)
        glast = glast_col[:, C_HEADS + h:C_HEADS + h + 1]
        kg_ref[0, :, h * dk:(h + 1) * dk] = (k * jnp.exp(glast - gc)).astype(kg_ref.dtype)
    for _ in range(int(math.log2(cs)) - 1):
        powers = [_dot(p, p).astype(BF16) for p in powers]
        t_mats = [t + _dot(t.astype(BF16), p) for t, p in zip(t_mats, powers)]
    for h in range(C_HEADS):
        gc = gcum_col[:, C_HEADS + h:C_HEADS + h + 1]
        beta = bgc[:, h:h + 1]
        k = qkv_ref[0, :, C_WIDTH + h * dk:C_WIDTH + (h + 1) * dk]
        v = qkv_ref[0, :, 2 * C_WIDTH + h * dk:2 * C_WIDTH + (h + 1) * dk]
        t16 = t_mats[h].astype(BF16)
        u_ref[0, :, h * dk:(h + 1) * dk] = _dot(t16, (v * beta).astype(BF16))
        w_ref[0, :, h * dk:(h + 1) * dk] = _dot(t16, (k * beta * jnp.exp(gc)).astype(BF16)).astype(w_ref.dtype)


def _c_chunks(qkv, bg, bgt, batch, seq):
    gs = C_GROUP * C_CHUNK
    idx = np.arange(gs)
    same = (idx[:, None] // C_CHUNK) == (idx[None, :] // C_CHUNK)
    tri = (same & (idx[:, None] >= idx[None, :])).astype(np.float32)
    wide = lambda width: pl.BlockSpec((1, gs, width), lambda b, i: (b, i, 0))
    shape = lambda width, dtype: jax.ShapeDtypeStruct((batch, seq, width), dtype)
    const = pl.BlockSpec((gs, gs), lambda b, i: (0, 0))
    return pl.pallas_call(
        _c_chunk_kernel,
        out_shape=(shape(C_WIDTH, F32), shape(C_WIDTH, BF16), shape(C_WIDTH, BF16), shape(C_WIDTH, BF16),
                   shape(C_HEADS * gs, BF16), shape(128, F32)),
        grid=(batch, seq // gs),
        in_specs=[wide(3 * C_WIDTH), wide(128),
                  pl.BlockSpec((1, 2 * C_HEADS, gs), lambda b, i: (b, 0, i)),
                  const, const, const],
        out_specs=(wide(C_WIDTH),) * 4 + (wide(C_HEADS * gs), wide(128)),
        compiler_params=_params("parallel", "parallel"),
        name="c_chunks",
    )(qkv, bg, bgt, jnp.asarray(tri, BF16), jnp.asarray(tri.T, BF16), jnp.asarray(same, BF16))


def _c_scan_kernel(u_ref, w_ref, qg_ref, kg_ref, attn_ref, gc_ref, o_ref, state_ref, vnew_ref):
    @pl.when(pl.program_id(1) == 0)
    def _():
        state_ref[...] = jnp.zeros_like(state_ref)

    cs = C_CHUNK
    dk = C_HEAD_DIM
    gs = C_GROUP * cs
    vnew_ref[...] = jnp.zeros_like(vnew_ref)
    heads = range(C_HEADS)
    cols = [slice(h * dk, (h + 1) * dk) for h in heads]
    states = [state_ref[h] for h in heads]
    for c in range(C_GROUP):
        rs = slice(c * cs, (c + 1) * cs)
        decay_last = jnp.exp(gc_ref[0, (c + 1) * cs - 1:(c + 1) * cs, :])
        both = [_dot(jnp.concatenate([w_ref[0, rs, cols[h]], qg_ref[0, rs, cols[h]]], axis=0),
                     states[h].astype(BF16)) for h in heads]
        v16 = [(u_ref[0, rs, cols[h]] - both[h][:cs]).astype(BF16) for h in heads]
        for h in heads:
            vnew_ref[h, rs, :] = v16[h]
        for h in heads:
            o_ref[0, rs, cols[h]] = both[h][cs:] + _dot(attn_ref[0, rs, h * gs:(h + 1) * gs], vnew_ref[h])
        states = [states[h] * decay_last[:, C_HEADS + h:C_HEADS + h + 1] + _dot_tn(kg_ref[0, rs, cols[h]], v16[h])
                  for h in heads]
    for h in heads:
        state_ref[h] = states[h]


def _c_scan(u, w, qg, kg, attn, gc, batch, seq):
    gs = C_GROUP * C_CHUNK
    wide = lambda width: pl.BlockSpec((1, gs, width), lambda b, c: (b, c, 0))
    return pl.pallas_call(
        _c_scan_kernel,
        out_shape=jax.ShapeDtypeStruct((batch, seq, C_WIDTH), F32),
        grid=(batch, seq // gs),
        in_specs=[wide(C_WIDTH)] * 4 + [wide(C_HEADS * gs), wide(128)],
        out_specs=wide(C_WIDTH),
        scratch_shapes=[pltpu.VMEM((C_HEADS, C_HEAD_DIM, C_HEAD_DIM), F32),
                        pltpu.VMEM((C_HEADS, gs, C_HEAD_DIM), BF16)],
        compiler_params=_params("parallel", "arbitrary"),
        name="c_scan",
    )(u, w, qg, kg, attn, gc)


def _c_out_kernel(o_ref, z_ref, g_ref, w_ref, x_ref, out_ref):
    dk = C_HEAD_DIM
    parts = []
    for h in range(C_HEADS):
        sl = slice(h * dk, (h + 1) * dk)
        z = z_ref[:, sl].astype(F32)
        parts.append((_rms(o_ref[:, sl], g_ref[...]) * (z * jax.nn.sigmoid(z))).astype(BF16))
    out_ref[...] = x_ref[...] + _dot(jnp.concatenate(parts, axis=-1), w_ref[...])


def _c_out(o, proj, out_gain, w_out, x):
    m, d = x.shape
    z_block = (3 * C_WIDTH) // C_WIDTH
    row = lambda width: pl.BlockSpec((ROW_TILE, width), lambda i: (i, 0))
    return pl.pallas_call(
        _c_out_kernel,
        out_shape=jax.ShapeDtypeStruct((m, d), F32),
        grid=(m // ROW_TILE,),
        in_specs=[row(C_WIDTH),
                  pl.BlockSpec((ROW_TILE, C_WIDTH), lambda i: (i, z_block)),
                  pl.BlockSpec((1, C_HEAD_DIM), lambda i: (0, 0)),
                  pl.BlockSpec((C_WIDTH, d), lambda i: (0, 0)),
                  row(d)],
        out_specs=row(d),
        compiler_params=_params("parallel"),
        name="c_out",
    )(o, proj, out_gain.reshape(1, -1), w_out, x)


def _mixer_c(x, norm1, w_in, conv_w, a_log, dt_bias, out_gain, w_out, batch, seq):
    main = 4 * C_WIDTH
    proj = _norm_matmul(x, norm1, w_in[:, :main].astype(BF16), BF16, 512)
    w_small = jnp.pad(w_in[:, main:], ((0, 0), (0, 128 - 2 * C_HEADS))).astype(BF16)
    small = _norm_matmul(x, norm1, w_small, F32, 128)
    qkv, bg = _c_conv(proj.reshape(batch, seq, main), small.reshape(batch, seq, 128), conv_w, a_log, dt_bias,
                      batch, seq)
    bgt = bg[:, :, :2 * C_HEADS].transpose(0, 2, 1)
    u, w, qg, kg, attn, gc = _c_chunks(qkv, bg, bgt, batch, seq)
    o = _c_scan(u, w, qg, kg, attn, gc, batch, seq)
    return _c_out(o.reshape(batch * seq, C_WIDTH), proj, out_gain, w_out.astype(BF16), x)


def kernel(x, rel_bias, l0_norm1, l0_a_w_in, l0_a_q_gain, l0_a_k_gain, l0_a_w_out, l0_norm2, l0_ffn_w_gate, l0_ffn_w_up, l0_ffn_w_down, l1_norm1, l1_b_w_in, l1_b_q_gain, l1_b_k_gain, l1_b_cmp_pos, l1_b_cmp_w1, l1_b_cmp_w2, l1_b_w_out, l1_norm2, l1_ffn_w_gate, l1_ffn_w_up, l1_ffn_w_down, l2_norm1, l2_c_w_in, l2_c_conv_w, l2_c_a_log, l2_c_dt_bias, l2_c_out_gain, l2_c_w_out, l2_norm2, l2_ffn_w_gate, l2_ffn_w_up, l2_ffn_w_down, l3_norm1, l3_a_w_in, l3_a_q_gain, l3_a_k_gain, l3_a_w_out, l3_norm2, l3_ffn_w_gate, l3_ffn_w_up, l3_ffn_w_down):
    batch, seq, d = x.shape
    h = x.reshape(batch * seq, d)

    def ffn(h, norm2, w_gate, w_up, w_down):
        return _ffn(h, norm2, w_gate.astype(BF16), w_up.astype(BF16), w_down.astype(BF16))

    a_biases = _a_biases(rel_bias)
    h = _mixer_a(h, a_biases, l0_norm1, l0_a_w_in, l0_a_q_gain, l0_a_k_gain, l0_a_w_out, batch, seq)
    h = ffn(h, l0_norm2, l0_ffn_w_gate, l0_ffn_w_up, l0_ffn_w_down)
    h = _mixer_b(h, rel_bias, l1_norm1, l1_b_w_in, l1_b_q_gain, l1_b_k_gain, l1_b_cmp_pos, l1_b_cmp_w1,
                 l1_b_cmp_w2, l1_b_w_out, batch, seq)
    h = ffn(h, l1_norm2, l1_ffn_w_gate, l1_ffn_w_up, l1_ffn_w_down)
    h = _mixer_c(h, l2_norm1, l2_c_w_in, l2_c_conv_w, l2_c_a_log, l2_c_dt_bias, l2_c_out_gain, l2_c_w_out,
                 batch, seq)
    h = ffn(h, l2_norm2, l2_ffn_w_gate, l2_ffn_w_up, l2_ffn_w_down)
    h = _mixer_a(h, a_biases, l3_norm1, l3_a_w_in, l3_a_q_gain, l3_a_k_gain, l3_a_w_out, batch, seq)
    h = ffn(h, l3_norm2, l3_ffn_w_gate, l3_ffn_w_up, l3_ffn_w_down)
    return h.reshape(batch, seq, d)
```

```python
import functools
import math

import numpy as np
import jax
import jax.numpy as jnp
from jax import lax
from jax.experimental import pallas as pl
from jax.experimental.pallas import tpu as pltpu

D_MODEL = 1024
RMS_EPS = 1e-6
NEG_INF = -1e30
TINY = 1e-30
FORCE_SCORE = 1e9

N_BUCKETS = 32
REL_MAX_DISTANCE = 2048
N_HEADS = 16

A_GROUPS = ((128, 1), (512, 4), (2048, 16))
A_HEAD_DIM = 64
A_Q_BLOCK = 128
A_PROJ_TILE = 512
A_STAT_WIDTH = 256
A_BLOCKS_PER_STEP = 2

B_KV_HEADS = 4
B_GROUP = 4
B_HEAD_DIM = 64
B_CMP_LEN = 32
B_CMP_STRIDE = 16
B_CMP_HIDDEN = 256
B_SEL_BLOCK = 64
B_TOP_N = 16
B_WINDOW = 512
B_TILE = 128
B_SWEEP = 256
B_PROJ_WIDTH = 3072
B_GATE_ROWS = 16

C_HEADS = 8
C_HEAD_DIM = 128
C_WIDTH = C_HEADS * C_HEAD_DIM
C_CONV = 4
C_CHUNK = 64
C_GROUP = 4

FFN_HIDDEN = 2816
FFN_TILE = 1024

ROW_TILE = 512
VMEM_LIMIT = 48 * 1024 * 1024

LOG2E = math.log2(math.e)

F32 = jnp.float32
BF16 = jnp.bfloat16

NT_DIMS = (((1,), (1,)), ((), ()))
TN_DIMS = (((0,), (0,)), ((), ()))


def _params(*semantics):
    return pltpu.CompilerParams(dimension_semantics=semantics, vmem_limit_bytes=VMEM_LIMIT)


def _dot(a, b):
    return jnp.dot(a, b, preferred_element_type=F32)


def _dot_nt(a, b):
    return lax.dot_general(a, b, NT_DIMS, preferred_element_type=F32)


def _dot_tn(a, b):
    return lax.dot_general(a, b, TN_DIMS, preferred_element_type=F32)


def _rms(x, gain):
    return x * lax.rsqrt(jnp.mean(x * x, axis=-1, keepdims=True) + RMS_EPS) * gain


def _bucket_thresholds():
    d = np.arange(1 << 15)
    max_exact = N_BUCKETS // 2
    d_f = np.maximum(d, 1).astype(np.float32)
    large = max_exact + (np.log(d_f / np.float32(max_exact)) / np.float32(math.log(REL_MAX_DISTANCE / max_exact))
                         * np.float32(N_BUCKETS - max_exact)).astype(np.int32)
    bucket = np.where(d < max_exact, d, np.minimum(large, N_BUCKETS - 1))
    return [int(np.argmax(bucket >= k)) if np.any(bucket >= k) else int(1 << 30) for k in range(N_BUCKETS)]


_THRESHOLDS = _bucket_thresholds()


def _bias_tile_kernel(tbl_ref, o_ref, *, base, tile_step, row_step, col_step, dmax, dil):
    h = pl.program_id(0)
    t = pl.program_id(1)
    shape = o_ref.shape[2:]
    i = lax.broadcasted_iota(jnp.int32, shape, 0)
    j = lax.broadcasted_iota(jnp.int32, shape, 1)
    dist = base + tile_step * t + row_step * i + col_step * j
    d = dist * dil
    val = jnp.full(shape, tbl_ref[0, h], F32)
    for k in range(1, N_BUCKETS):
        val = jnp.where(d >= _THRESHOLDS[k], tbl_ref[k, h], val)
    valid = (dist >= 0) & (dist <= dmax)
    o_ref[0, 0] = jnp.where(valid, val * LOG2E, NEG_INF).astype(o_ref.dtype)


def _bias_tiles(rel_bias, n_tiles, rows, cols, *, base, tile_step, row_step, col_step, dmax, dil=1, dtype=F32):
    kern = functools.partial(_bias_tile_kernel, base=base, tile_step=tile_step, row_step=row_step,
                             col_step=col_step, dmax=dmax, dil=dil)
    return pl.pallas_call(
        kern,
        out_shape=jax.ShapeDtypeStruct((N_HEADS, n_tiles, rows, cols), dtype),
        grid=(N_HEADS, n_tiles),
        in_specs=[pl.BlockSpec(memory_space=pltpu.SMEM)],
        out_specs=pl.BlockSpec((1, 1, rows, cols), lambda h, t: (h, t, 0, 0)),
        compiler_params=_params("parallel", "parallel"),
        name="bias_tiles",
    )(rel_bias)


def _resident(shape):
    return pl.BlockSpec(shape, lambda i: (0,) * len(shape), pipeline_mode=pl.Buffered(1))


def _norm_matmul_kernel(x_ref, g_ref, w_ref, o_ref, *, tn):
    h = _rms(x_ref[...], g_ref[...]).astype(BF16)
    for j in range(w_ref.shape[1] // tn):
        o_ref[:, j * tn:(j + 1) * tn] = _dot(h, w_ref[:, j * tn:(j + 1) * tn]).astype(o_ref.dtype)


def _norm_matmul(x, gain, w, out_dtype, tn):
    m, d = x.shape
    n = w.shape[1]
    return pl.pallas_call(
        functools.partial(_norm_matmul_kernel, tn=tn),
        out_shape=jax.ShapeDtypeStruct((m, n), out_dtype),
        grid=(m // ROW_TILE,),
        in_specs=[pl.BlockSpec((ROW_TILE, d), lambda i: (i, 0)),
                  _resident((1, d)),
                  _resident((d, n))],
        out_specs=pl.BlockSpec((ROW_TILE, n), lambda i: (i, 0)),
        compiler_params=_params("parallel"),
        name="norm_matmul",
    )(x, gain.reshape(1, d), w)


def _matmul_residual_kernel(a_ref, w_ref, x_ref, o_ref):
    o_ref[...] = x_ref[...] + _dot(a_ref[...], w_ref[...])


def _matmul_residual(a, w, x):
    m, k = a.shape
    d = w.shape[1]
    return pl.pallas_call(
        _matmul_residual_kernel,
        out_shape=jax.ShapeDtypeStruct((m, d), F32),
        grid=(m // ROW_TILE,),
        in_specs=[pl.BlockSpec((ROW_TILE, k), lambda i: (i, 0)),
                  pl.BlockSpec((k, d), lambda i: (0, 0)),
                  pl.BlockSpec((ROW_TILE, d), lambda i: (i, 0))],
        out_specs=pl.BlockSpec((ROW_TILE, d), lambda i: (i, 0)),
        compiler_params=_params("parallel"),
        name="matmul_residual",
    )(a, w, x)


def _ffn_kernel(x_ref, g_ref, wg_ref, wu_ref, wd_ref, o_ref):
    x = x_ref[...]
    h = _rms(x, g_ref[...]).astype(BF16)
    hidden = wg_ref.shape[1]
    acc = x
    for lo in range(0, hidden, FFN_TILE):
        hi = min(lo + FFN_TILE, hidden)
        a = _dot(h, wg_ref[:, lo:hi])
        b = _dot(h, wu_ref[:, lo:hi])
        acc = acc + _dot((a * jax.nn.sigmoid(a) * b).astype(BF16), wd_ref[lo:hi, :])
    o_ref[...] = acc


def _ffn(x, gain, w_gate, w_up, w_down):
    m, d = x.shape
    hidden = w_gate.shape[1]
    return pl.pallas_call(
        _ffn_kernel,
        out_shape=jax.ShapeDtypeStruct((m, d), F32),
        grid=(m // ROW_TILE,),
        in_specs=[pl.BlockSpec((ROW_TILE, d), lambda i: (i, 0)),
                  _resident((1, d)),
                  _resident((d, hidden)), _resident((d, hidden)), _resident((hidden, d))],
        out_specs=pl.BlockSpec((ROW_TILE, d), lambda i: (i, 0)),
        compiler_params=_params("parallel"),
        name="ffn",
    )(x, gain.reshape(1, d), w_gate, w_up, w_down)


def _a_proj_kernel(x_ref, g_ref, w_ref, qg_ref, kg_ref, o_ref, h_ref, x_scr, *, dil):
    rows = ROW_TILE // dil
    xn = _rms(x_ref[...], g_ref[...])
    if dil == 1:
        h_ref[...] = xn.astype(BF16)
    else:
        slabs = xn.shape[1] // 128
        for c in range(slabs):
            x_scr[c] = xn[:, c * 128:(c + 1) * 128]
        for r in range(dil):
            picked = [x_scr[c, pl.ds(r, rows, stride=dil), :] for c in range(slabs)]
            h_ref[r * rows:(r + 1) * rows, :] = jnp.concatenate(picked, axis=1).astype(BF16)
    h = h_ref[...]
    width = w_ref.shape[1]
    hd = width // 3
    low = lax.broadcasted_iota(jnp.int32, (ROW_TILE, 128), 1) < A_HEAD_DIM
    for j in range(width // A_PROJ_TILE):
        res = _dot(h, w_ref[:, j * A_PROJ_TILE:(j + 1) * A_PROJ_TILE])
        kind = (j * A_PROJ_TILE) // hd
        if kind < 2:
            parts = []
            for c in range(A_PROJ_TILE // 128):
                y = res[:, c * 128:(c + 1) * 128]
                sq = y * y
                tot = jnp.sum(sq, axis=-1, keepdims=True)
                lo = jnp.sum(jnp.where(low, sq, 0.0), axis=-1, keepdims=True)
                ss = jnp.where(low, lo, tot - lo)
                parts.append(y * lax.rsqrt(ss * (1.0 / A_HEAD_DIM) + RMS_EPS))
            res = jnp.concatenate(parts, axis=1) * (qg_ref if kind == 0 else kg_ref)[...]
        res = res.astype(BF16)
        for r in range(dil):
            off = r * width + j * A_PROJ_TILE
            o_ref[:, off:off + A_PROJ_TILE] = res[r * rows:(r + 1) * rows]


def _a_proj(x, gain, w, q_gain, k_gain, dil):
    m, d = x.shape
    width = w.shape[1]
    reps = A_PROJ_TILE // A_HEAD_DIM
    qg = jnp.tile(q_gain * (A_HEAD_DIM ** -0.5 * LOG2E), reps).reshape(1, A_PROJ_TILE)
    kg = jnp.tile(k_gain, reps).reshape(1, A_PROJ_TILE)
    return pl.pallas_call(
        functools.partial(_a_proj_kernel, dil=dil),
        out_shape=jax.ShapeDtypeStruct((m // dil, dil * width), BF16),
        grid=(m // ROW_TILE,),
        in_specs=[pl.BlockSpec((ROW_TILE, d), lambda i: (i, 0)),
                  pl.BlockSpec((1, d), lambda i: (0, 0)),
                  pl.BlockSpec((d, width), lambda i: (0, 0)),
                  pl.BlockSpec((1, A_PROJ_TILE), lambda i: (0, 0)),
                  pl.BlockSpec((1, A_PROJ_TILE), lambda i: (0, 0))],
        out_specs=pl.BlockSpec((ROW_TILE // dil, dil * width), lambda i: (i, 0)),
        scratch_shapes=[pltpu.VMEM((ROW_TILE, d), BF16), pltpu.VMEM((d // 128, ROW_TILE, 128), F32)],
        compiler_params=_params("parallel"),
        name="a_proj",
    )(x, gain.reshape(1, d), w, qg, kg)


def _a_attn_kernel(q_ref, kp_ref, kc_ref, vp_ref, vc_ref, bias_ref, o_ref, stat_ref):
    first = (pl.program_id(2) == 0).astype(jnp.int32)
    nq = A_Q_BLOCK
    lane = lax.broadcasted_iota(jnp.int32, (nq, 128), 1)
    low = lane < A_HEAD_DIM
    ones = jnp.ones((2 * nq, 128), BF16)
    for sub in range(A_BLOCKS_PER_STEP):
        rows = slice(sub * nq, (sub + 1) * nq)
        max_tile = jnp.zeros((nq, 128), F32)
        den_tile = jnp.ones((nq, 128), F32)
        for pair in range(N_HEADS // 2):
            sl = slice(pair * 128, (pair + 1) * 128)
            q = q_ref[0, rows, sl]
            zero = jnp.zeros_like(q)
            qq = jnp.concatenate([jnp.where(low, q, zero), jnp.where(low, zero, q)], axis=0)
            if sub == 0:
                kk = jnp.concatenate([kp_ref[0, :, sl], kc_ref[0, :nq, sl]], axis=0)
                vv = jnp.concatenate([vp_ref[0, :, sl], vc_ref[0, :nq, sl]], axis=0)
                base = 2 * pair + N_HEADS * first
            else:
                kk = kc_ref[0, :, sl]
                vv = vc_ref[0, :, sl]
                base = 2 * pair
            s = _dot_nt(qq, kk) + jnp.concatenate([bias_ref[base], bias_ref[base + 1]], axis=0)
            m = jnp.max(s, axis=-1, keepdims=True)
            acc = _dot(jnp.exp2(s - m).astype(BF16), jnp.concatenate([vv, ones], axis=1))
            o_ref[0, rows, sl] = jnp.where(low, acc[:nq, :128], acc[nq:, :128]).astype(o_ref.dtype)
            first_head = lane == 2 * pair
            second_head = lane == 2 * pair + 1
            max_tile = jnp.where(first_head, m[:nq], jnp.where(second_head, m[nq:], max_tile))
            den_tile = jnp.where(first_head, acc[:nq, 128:], jnp.where(second_head, acc[nq:, 128:], den_tile))
        stat_ref[0, rows, :128] = max_tile
        stat_ref[0, rows, 128:] = den_tile


def _a_attention(proj, bias, dil, batch, seq):
    length = seq // dil
    nblk = length // A_Q_BLOCK
    hd = N_HEADS * A_HEAD_DIM
    pv = proj.reshape(batch, length, dil * 3 * hd)

    per = A_BLOCKS_PER_STEP
    step_rows = per * A_Q_BLOCK
    assert nblk % per == 0

    def spec(off, prev):
        if prev:
            return pl.BlockSpec((1, A_Q_BLOCK, hd), lambda b, r, i: (b, jnp.maximum(per * i - 1, 0), r * 3 + off))
        return pl.BlockSpec((1, step_rows, hd), lambda b, r, i: (b, i, r * 3 + off))

    o, stats = pl.pallas_call(
        _a_attn_kernel,
        out_shape=(jax.ShapeDtypeStruct((batch, length, dil * hd), BF16),
                   jax.ShapeDtypeStruct((batch, length, dil * A_STAT_WIDTH), F32)),
        grid=(batch, dil, nblk // per),
        in_specs=[spec(0, False), spec(1, True), spec(1, False), spec(2, True), spec(2, False),
                  pl.BlockSpec((2 * N_HEADS, A_Q_BLOCK, 2 * A_Q_BLOCK), lambda b, r, i: (0, 0, 0))],
        out_specs=(pl.BlockSpec((1, step_rows, hd), lambda b, r, i: (b, i, r)),
                   pl.BlockSpec((1, step_rows, A_STAT_WIDTH), lambda b, r, i: (b, i, r))),
        compiler_params=_params("parallel", "parallel", "arbitrary"),
        name="a_attention",
    )(pv, pv, pv, pv, pv, bias)
    return o.reshape(batch * length, dil * hd), stats.reshape(batch * length, dil * A_STAT_WIDTH)


def _a_out_kernel(o0_ref, o1_ref, o2_ref, s0_ref, s1_ref, s2_ref, e_ref, w_ref, x_ref, out_ref, o_scr, s_scr):
    hd = N_HEADS * A_HEAD_DIM
    sw = A_STAT_WIDTH
    for g, (o_ref, s_ref) in enumerate(((o0_ref, s0_ref), (o1_ref, s1_ref), (o2_ref, s2_ref))):
        dil = A_GROUPS[g][1]
        rows = ROW_TILE // dil
        for r in range(dil):
            dst = pl.ds(r, rows, stride=dil) if dil > 1 else slice(None)
            s_scr[g, 0, dst, :] = s_ref[:, r * sw:r * sw + 128]
            s_scr[g, 1, dst, :] = s_ref[:, r * sw + 128:(r + 1) * sw]
            for c in range(hd // 128):
                o_scr[g, c, dst, :] = o_ref[:, r * hd + c * 128:r * hd + (c + 1) * 128].astype(F32)
    groups = range(len(A_GROUPS))
    top = functools.reduce(jnp.maximum, [s_scr[g, 0] for g in groups])
    es = [jnp.exp2(s_scr[g, 0] - top) for g in groups]
    inv = 1.0 / sum(es[g] * s_scr[g, 1] for g in groups)
    expand = e_ref[...]
    acc = None
    for g in groups:
        o_g = jnp.concatenate([o_scr[g, c] for c in range(hd // 128)], axis=1)
        term = _dot((es[g] * inv).astype(BF16), expand) * o_g
        acc = term if acc is None else acc + term
    out_ref[...] = x_ref[...] + _dot(acc.astype(BF16), w_ref[...])


def _a_out(outs, stats, w_out, x):
    m, d = x.shape
    hd = N_HEADS * A_HEAD_DIM
    expand = np.zeros((128, hd), np.float32)
    for h in range(N_HEADS):
        expand[h, h * A_HEAD_DIM:(h + 1) * A_HEAD_DIM] = 1.0
    grouped = lambda width: [pl.BlockSpec((ROW_TILE // dil, dil * width), lambda i: (i, 0)) for _, dil in A_GROUPS]
    return pl.pallas_call(
        _a_out_kernel,
        out_shape=jax.ShapeDtypeStruct((m, d), F32),
        grid=(m // ROW_TILE,),
        in_specs=grouped(hd) + grouped(A_STAT_WIDTH) + [
            pl.BlockSpec((128, hd), lambda i: (0, 0)),
            pl.BlockSpec((hd, d), lambda i: (0, 0)),
            pl.BlockSpec((ROW_TILE, d), lambda i: (i, 0))],
        out_specs=pl.BlockSpec((ROW_TILE, d), lambda i: (i, 0)),
        scratch_shapes=[pltpu.VMEM((len(A_GROUPS), hd // 128, ROW_TILE, 128), F32),
                        pltpu.VMEM((len(A_GROUPS), 2, ROW_TILE, 128), F32)],
        compiler_params=_params("parallel"),
        name="a_out",
    )(*outs, *stats, jnp.asarray(expand, BF16), w_out, x)


def _a_biases(rel_bias):
    biases = []
    for window, dil in A_GROUPS:
        steps = window // dil
        assert steps == A_Q_BLOCK
        bias = _bias_tiles(rel_bias, 1, A_Q_BLOCK, 2 * A_Q_BLOCK, base=A_Q_BLOCK, tile_step=0, row_step=1,
                           col_step=-1, dmax=steps, dil=dil)[:, 0]
        biases.append(jnp.concatenate([bias, bias.at[:, :, :A_Q_BLOCK].set(NEG_INF)], axis=0))
    return biases


def _mixer_a(x, biases, norm1, w_in, q_gain, k_gain, w_out, batch, seq):
    w_in = w_in.astype(BF16)
    group_width = 3 * N_HEADS * A_HEAD_DIM
    outs, stats = [], []
    for gi, (_, dil) in enumerate(A_GROUPS):
        assert (seq // dil) % A_Q_BLOCK == 0 and seq % ROW_TILE == 0
        bias = biases[gi]
        proj = _a_proj(x, norm1, w_in[:, gi * group_width:(gi + 1) * group_width], q_gain[gi], k_gain[gi], dil)
        o, stat = _a_attention(proj, bias, dil, batch, seq)
        outs.append(o)
        stats.append(stat)
    return _a_out(outs, stats, w_out.astype(BF16), x)


def _b_prep_kernel(p_ref, qg_ref, kg_ref, qt_ref, ck_ref, cv_ref, ka_ref, vs_ref, wk_ref, vw_ref, gate_ref):
    dh = B_HEAD_DIM
    ts = p_ref.shape[1]
    n_sel = ka_ref.shape[3] - dh
    low = lax.broadcasted_iota(jnp.int32, (ts, 128), 1) < dh

    def slab(c):
        return p_ref[0, :, c * 128:(c + 1) * 128].astype(F32)

    def normed(x, gain):
        sq = x * x
        tot = jnp.sum(sq, axis=-1, keepdims=True)
        first = jnp.sum(jnp.where(low, sq, 0.0), axis=-1, keepdims=True)
        ss = jnp.where(low, first, tot - first)
        return x * lax.rsqrt(ss * (1.0 / dh) + RMS_EPS) * jnp.concatenate([gain, gain], axis=1)

    def halves(x):
        return x[:, :dh], x[:, dh:]

    qg = qg_ref[...] * (dh ** -0.5 * LOG2E)
    for c in range(N_HEADS // 2):
        xt = normed(slab(c), qg).T.astype(BF16)
        qt_ref[0, 2 * c] = xt[:dh]
        qt_ref[0, 2 * c + 1] = xt[dh:]
    base = N_HEADS // 2
    pairs = B_KV_HEADS // 2
    pos = pl.program_id(1) * ts + lax.broadcasted_iota(jnp.int32, (ts, 128), 0)
    lane = lax.broadcasted_iota(jnp.int32, (ts, 128), 1)
    onehot = jnp.where(lane - dh == pos // B_SEL_BLOCK, 1.0, 0.0)
    ones = jnp.ones((vs_ref.shape[2] - dh, ts), BF16)
    for j in range(pairs):
        for ref, off in ((ck_ref, 0), (cv_ref, pairs)):
            a, b = halves(slab(base + off + j))
            ref[0, 2 * j] = a.astype(BF16)
            ref[0, 2 * j + 1] = b.astype(BF16)
        k_sel = normed(slab(base + 2 * pairs + j), kg_ref[1:2, :])
        for n, k in zip((2 * j, 2 * j + 1), (k_sel, pltpu.roll(k_sel, dh, 1))):
            ka_ref[0, n] = jnp.where(low, k, onehot)[:, :dh + n_sel].astype(BF16)
        for n, k in zip((2 * j, 2 * j + 1), halves(normed(slab(base + 4 * pairs + j), kg_ref[2:3, :]))):
            wk_ref[0, n] = k.astype(BF16)
        for ref, off in ((vs_ref, 3 * pairs), (vw_ref, 5 * pairs)):
            xt = slab(base + off + j).T.astype(BF16)
            for n, v in zip((2 * j, 2 * j + 1), (xt[:dh], xt[dh:])):
                ref[0, n, :dh, :] = v
                ref[0, n, dh:, :] = ones
    gate = jax.nn.sigmoid(slab(base + 6 * pairs)).T
    rows = gate_ref.shape[2]
    for n in range(B_KV_HEADS):
        gate_ref[0, n] = gate[n * rows:(n + 1) * rows]


def _b_prep(proj, q_gain, k_gain, batch, seq):
    ts = 256
    dh = B_HEAD_DIM
    n_sel = seq // B_SEL_BLOCK
    assert dh + n_sel <= 128
    rows_shape = lambda width: jax.ShapeDtypeStruct((batch, B_KV_HEADS, seq, width), BF16)
    rows_spec = lambda width: pl.BlockSpec((1, B_KV_HEADS, ts, width), lambda b, i: (b, 0, i, 0))
    cols_shape = lambda heads, height, dtype: jax.ShapeDtypeStruct((batch, heads, height, seq), dtype)
    cols_spec = lambda heads, height: pl.BlockSpec((1, heads, height, ts), lambda b, i: (b, 0, 0, i))
    return pl.pallas_call(
        _b_prep_kernel,
        out_shape=(cols_shape(N_HEADS, dh, BF16), rows_shape(dh), rows_shape(dh), rows_shape(dh + n_sel),
                   cols_shape(B_KV_HEADS, dh + 16, BF16), rows_shape(dh), cols_shape(B_KV_HEADS, dh + 16, BF16),
                   cols_shape(B_KV_HEADS, B_GATE_ROWS, F32)),
        grid=(batch, seq // ts),
        in_specs=[pl.BlockSpec((1, ts, B_PROJ_WIDTH), lambda b, i: (b, i, 0)),
                  pl.BlockSpec((1, dh), lambda b, i: (0, 0)),
                  pl.BlockSpec((3, dh), lambda b, i: (0, 0))],
        out_specs=(cols_spec(N_HEADS, dh), rows_spec(dh), rows_spec(dh), rows_spec(dh + n_sel),
                   cols_spec(B_KV_HEADS, dh + 16), rows_spec(dh), cols_spec(B_KV_HEADS, dh + 16),
                   cols_spec(B_KV_HEADS, B_GATE_ROWS)),
        compiler_params=_params("parallel", "parallel"),
        name="b_prep",
    )(proj, q_gain.reshape(1, dh), k_gain)


def _b_compress_kernel(tk_ref, tv_ref, pos_ref, w1_ref, w2_ref, kg_ref, kc_ref, vc_ref):
    half = (B_CMP_LEN // 2) * B_HEAD_DIM
    for kv, (t_ref, out_ref) in enumerate(((tk_ref, kc_ref), (tv_ref, vc_ref))):
        t = t_ref[0, 0].astype(F32)
        top = (t + pos_ref[kv, 0:1, :]).astype(BF16)
        bot = (t + pos_ref[kv, 1:2, :]).astype(BF16)
        a1 = _dot(top, w1_ref[kv, :half, :])
        a2 = _dot(bot, w1_ref[kv, half:, :])
        hidden = a1 + pltpu.roll(a2, a2.shape[0] - 1, 0)
        out = _dot(jax.nn.gelu(hidden).astype(BF16), w2_ref[kv])
        if kv == 0:
            out = _rms(out, kg_ref[...])
        out_ref[0, 0] = out.astype(out_ref.dtype)


def _b_compress(ck, cv, cmp_pos, cmp_w1, cmp_w2, k_gain0, batch, seq):
    rows = seq // B_CMP_STRIDE
    half = (B_CMP_LEN // 2) * B_HEAD_DIM
    tk = ck.reshape(batch, B_KV_HEADS, rows, half)
    tv = cv.reshape(batch, B_KV_HEADS, rows, half)
    pos = cmp_pos.reshape(2, 2, half)
    t_spec = pl.BlockSpec((1, 1, rows, half), lambda b, n: (b, n, 0, 0))
    o_spec = pl.BlockSpec((1, 1, rows, B_HEAD_DIM), lambda b, n: (b, n, 0, 0))
    shape = jax.ShapeDtypeStruct((batch, B_KV_HEADS, rows, B_HEAD_DIM), BF16)
    return pl.pallas_call(
        _b_compress_kernel,
        out_shape=(shape, shape),
        grid=(batch, B_KV_HEADS),
        in_specs=[t_spec, t_spec,
                  pl.BlockSpec((2, 2, half), lambda b, n: (0, 0, 0)),
                  pl.BlockSpec((2, 2 * half, B_CMP_HIDDEN), lambda b, n: (0, 0, 0)),
                  pl.BlockSpec((2, B_CMP_HIDDEN, B_HEAD_DIM), lambda b, n: (0, 0, 0)),
                  pl.BlockSpec((1, B_HEAD_DIM), lambda b, n: (0, 0))],
        out_specs=(o_spec, o_spec),
        compiler_params=_params("parallel", "parallel"),
        name="b_compress",
    )(tk, tv, pos, cmp_w1.astype(BF16), cmp_w2.astype(BF16), k_gain0.reshape(1, -1))


def _b_cmp_attn_kernel(qt_ref, kc_ref, vct_ref, bias_ref, c2s_ref, oc_ref, sel_ref, imp_ref, *, top_n):
    tq = B_SWEEP
    n_sel = imp_ref.shape[0]
    qt = jnp.concatenate([qt_ref[0, g] for g in range(B_GROUP)], axis=1)
    n_cmp_pad = kc_ref.shape[2]
    shift = tq // B_CMP_STRIDE
    off = pl.multiple_of((pl.num_programs(2) - 1 - pl.program_id(2)) * shift, shift)
    bias = jnp.concatenate([bias_ref[g, 0, pl.ds(off, n_cmp_pad), :] for g in range(B_GROUP)], axis=1)
    s = _dot(kc_ref[0, 0], qt) + bias
    m = jnp.max(s, axis=0, keepdims=True)
    e = jnp.exp2(s - m)
    z = jnp.maximum(jnp.sum(e, axis=0, keepdims=True), TINY)
    pos = pl.program_id(2) * tq + lax.broadcasted_iota(jnp.int32, (1, tq), 1)
    sees_any = jnp.concatenate([pos >= B_CMP_LEN - 1] * B_GROUP, axis=1)
    p = e * jnp.where(sees_any, 1.0 / z, 0.0)
    oct = _dot(vct_ref[0, 0], p.astype(BF16))
    for g in range(B_GROUP):
        oc_ref[0, g] = oct[:, g * tq:(g + 1) * tq]

    p_sum = p[:, 0:tq] + p[:, tq:2 * tq] + p[:, 2 * tq:3 * tq] + p[:, 3 * tq:4 * tq]
    hi = p_sum.astype(BF16)
    lo = (p_sum - hi.astype(F32)).astype(BF16)
    c2s = c2s_ref[...]
    imp = _dot(c2s, hi) + _dot(c2s, lo)

    t = pl.program_id(2) * tq + lax.broadcasted_iota(jnp.int32, (n_sel, tq), 1)
    blk = lax.broadcasted_iota(jnp.int32, (n_sel, tq), 0)
    cur = t // B_SEL_BLOCK
    forced = (blk == 0) | (blk == cur) | (blk == cur - 1)
    imp = jnp.where(forced, FORCE_SCORE, jnp.where(blk * B_SEL_BLOCK <= t, imp, NEG_INF))
    imp_ref[...] = imp

    def count(i, rank):
        row = imp_ref[pl.ds(i, 1), :]
        ahead = jnp.where(row > imp, 1.0, jnp.where(row == imp, jnp.where(blk > i, 1.0, 0.0), 0.0))
        return rank + ahead

    n_live = jnp.minimum(n_sel, (pl.program_id(2) + 1) * (tq // B_SEL_BLOCK))
    rank = lax.fori_loop(0, n_live, count, jnp.zeros((n_sel, tq), F32))
    sel_ref[0, 0] = jnp.where(rank < top_n, 0.0, NEG_INF).astype(sel_ref.dtype)


def _b_cmp_attn(qt, kc, vc, bias_c, batch, seq):
    n_sel = seq // B_SEL_BLOCK
    n_cmp_pad = seq // B_CMP_STRIDE
    n_cmp = (seq - B_CMP_LEN) // B_CMP_STRIDE + 1
    c = np.arange(n_cmp_pad)[None, :] * B_CMP_STRIDE
    j = np.arange(n_sel)[:, None] * B_SEL_BLOCK
    c2s = ((c < j + B_SEL_BLOCK) & (c + B_CMP_LEN > j) & (np.arange(n_cmp_pad)[None, :] < n_cmp)).astype(np.float32)
    kern = functools.partial(_b_cmp_attn_kernel, top_n=min(B_TOP_N, n_sel))
    return pl.pallas_call(
        kern,
        out_shape=(jax.ShapeDtypeStruct((batch, N_HEADS, B_HEAD_DIM, seq), F32),
                   jax.ShapeDtypeStruct((batch, B_KV_HEADS, n_sel, seq), BF16)),
        grid=(batch, B_KV_HEADS, seq // B_SWEEP),
        in_specs=[pl.BlockSpec((1, B_GROUP, B_HEAD_DIM, B_SWEEP), lambda b, n, i: (b, n, 0, i)),
                  pl.BlockSpec((1, 1, n_cmp_pad, B_HEAD_DIM), lambda b, n, i: (b, n, 0, 0)),
                  pl.BlockSpec((1, 1, B_HEAD_DIM, n_cmp_pad), lambda b, n, i: (b, n, 0, 0)),
                  pl.BlockSpec((B_GROUP, 1, bias_c.shape[2], B_SWEEP), lambda b, n, i: (n, 0, 0, 0)),
                  pl.BlockSpec((n_sel, n_cmp_pad), lambda b, n, i: (0, 0))],
        out_specs=(pl.BlockSpec((1, B_GROUP, B_HEAD_DIM, B_SWEEP), lambda b, n, i: (b, n, 0, i)),
                   pl.BlockSpec((1, 1, n_sel, B_SWEEP), lambda b, n, i: (b, n, 0, i))),
        scratch_shapes=[pltpu.VMEM((n_sel, B_SWEEP), F32)],
        compiler_params=_params("parallel", "parallel", "arbitrary"),
        name="b_cmp_attn",
    )(qt, kc, vc.transpose(0, 1, 3, 2), bias_c, jnp.asarray(c2s, BF16))


def _b_sparse_kernel(qt_ref, ka_ref, vs_ref, wk_ref, vw_ref, sel_ref, bs_ref, bw_ref, oc_ref, gate_ref,
                     o_ref, acc_ref, sa_ref, sb_ref, *, delta_max, win_tiles):
    tq = B_SWEEP
    dh = B_HEAD_DIM
    cols = B_GROUP * tq
    qi = pl.program_id(2)
    n_tiles = ka_ref.shape[2] // tq
    qt = jnp.concatenate([qt_ref[0, g] for g in range(B_GROUP)], axis=1)
    q_aug = jnp.concatenate([qt, jnp.concatenate([sel_ref[0, 0]] * B_GROUP, axis=1)], axis=0)

    def tile_start(kt):
        return pl.multiple_of(jnp.clip(kt, 0, n_tiles - 1) * tq, tq)

    def normalised(acc):
        return acc[:dh] * (1.0 / acc[dh:dh + 1])

    def sel_bias(kt):
        d = jnp.clip(qi - kt, -1, delta_max) + 1
        return jnp.concatenate([bs_ref[g, d] for g in range(B_GROUP)], axis=1)

    def sel_scores(kt):
        return _dot(ka_ref[0, 0, pl.ds(tile_start(kt), tq), :], q_aug).astype(BF16) + sel_bias(kt)

    def consume(s_buf, kt, m_old):
        s = s_buf[...]
        m_new = jnp.maximum(m_old, jnp.max(s, axis=0, keepdims=True).astype(F32))
        alpha = jnp.exp2(m_old - m_new)
        p = jnp.exp2(s - m_new.astype(BF16))
        acc_ref[...] = alpha * acc_ref[...] + _dot(vs_ref[0, 0, :, pl.ds(tile_start(kt), tq)], p)
        return m_new

    acc_ref[...] = jnp.zeros(acc_ref.shape, F32)
    sa_ref[...] = sel_scores(0)

    def pair(j, m):
        kt = 2 * j
        sb_ref[...] = sel_scores(kt + 1)
        m = consume(sa_ref, kt, m)
        sa_ref[...] = sel_scores(kt + 2)
        return consume(sb_ref, kt + 1, m)

    lax.fori_loop(0, (qi + 2) // 2, pair, jnp.full((1, cols), NEG_INF, F32))
    o_s = normalised(acc_ref[...])

    tiles = []
    for u in range(win_tiles):
        kt = qi - (win_tiles - 1) + u
        d = jnp.where(kt >= 0, qi - kt, -1) + 1
        bias = jnp.concatenate([bw_ref[g, d] for g in range(B_GROUP)], axis=1)
        tiles.append((_dot(wk_ref[0, 0, pl.ds(tile_start(kt), tq), :], qt).astype(BF16) + bias, kt))
    m = None
    for s, _ in tiles:
        tile_max = jnp.max(s, axis=0, keepdims=True)
        m = tile_max if m is None else jnp.maximum(m, tile_max)
    acc = None
    for s, kt in tiles:
        pv = _dot(vw_ref[0, 0, :, pl.ds(tile_start(kt), tq)], jnp.exp2(s - m))
        acc = pv if acc is None else acc + pv
    o_w = normalised(acc)

    gate = gate_ref[0, 0]
    merged = []
    for g in range(B_GROUP):
        cs = slice(g * tq, (g + 1) * tq)
        merged.append(gate[3 * g:3 * g + 1] * oc_ref[0, g] + gate[3 * g + 1:3 * g + 2] * o_s[:, cs]
                      + gate[3 * g + 2:3 * g + 3] * o_w[:, cs])
    for pair in range(B_GROUP // 2):
        both = jnp.concatenate([merged[2 * pair], merged[2 * pair + 1]], axis=0)
        o_ref[0, :, pair * 2 * dh:(pair + 1) * 2 * dh] = both.T.astype(o_ref.dtype)


def _b_sparse(qt, ka, vs, wk, vw, sel, bias_s, bias_w, oc, gate, batch, seq):
    n_sel = seq // B_SEL_BLOCK
    dh = B_HEAD_DIM
    n_ds = bias_s.shape[1]
    n_dw = bias_w.shape[1]
    vrows = vs.shape[2]
    kern = functools.partial(_b_sparse_kernel, delta_max=n_ds - 2, win_tiles=n_dw - 1)
    whole = lambda rows, width: pl.BlockSpec((1, 1, rows, width), lambda b, n, i: (b, n, 0, 0))
    return pl.pallas_call(
        kern,
        out_shape=jax.ShapeDtypeStruct((batch, seq, N_HEADS * dh), BF16),
        grid=(batch, B_KV_HEADS, seq // B_SWEEP),
        in_specs=[pl.BlockSpec((1, B_GROUP, dh, B_SWEEP), lambda b, n, i: (b, n, 0, i)),
                  whole(seq, dh + n_sel), whole(vrows, seq), whole(seq, dh), whole(vrows, seq),
                  pl.BlockSpec((1, 1, n_sel, B_SWEEP), lambda b, n, i: (b, n, 0, i)),
                  pl.BlockSpec((B_GROUP, n_ds, B_SWEEP, B_SWEEP), lambda b, n, i: (n, 0, 0, 0)),
                  pl.BlockSpec((B_GROUP, n_dw, B_SWEEP, B_SWEEP), lambda b, n, i: (n, 0, 0, 0)),
                  pl.BlockSpec((1, B_GROUP, dh, B_SWEEP), lambda b, n, i: (b, n, 0, i)),
                  pl.BlockSpec((1, 1, B_GATE_ROWS, B_SWEEP), lambda b, n, i: (b, n, 0, i))],
        out_specs=pl.BlockSpec((1, B_SWEEP, B_GROUP * dh), lambda b, n, i: (b, i, n)),
        scratch_shapes=[pltpu.VMEM((vrows, B_GROUP * B_SWEEP), F32),
                        pltpu.VMEM((B_SWEEP, B_GROUP * B_SWEEP), BF16),
                        pltpu.VMEM((B_SWEEP, B_GROUP * B_SWEEP), BF16)],
        compiler_params=_params("parallel", "parallel", "arbitrary"),
        name="b_sparse",
    )(qt, ka, vs, wk, vw, sel, bias_s, bias_w, oc, gate)


def _mixer_b(x, rel_bias, norm1, w_in, q_gain, k_gain, cmp_pos, cmp_w1, cmp_w2, w_out, batch, seq):
    d = w_in.shape[0]
    qkv_width = w_in.shape[1] - 3 * N_HEADS
    gate_w = w_in[:, qkv_width:].reshape(d, B_KV_HEADS, 3 * B_GROUP)
    gate_w = jnp.pad(gate_w, ((0, 0), (0, 0), (0, B_GATE_ROWS - 3 * B_GROUP))).reshape(d, B_KV_HEADS * B_GATE_ROWS)
    w_pad = jnp.concatenate([w_in[:, :qkv_width], gate_w], axis=1)
    w_pad = jnp.pad(w_pad, ((0, 0), (0, B_PROJ_WIDTH - w_pad.shape[1]))).astype(BF16)
    proj = _norm_matmul(x, norm1, w_pad, BF16, 512).reshape(batch, seq, B_PROJ_WIDTH)
    qt, ck, cv, ka, vs, wk, vw, gate_t = _b_prep(proj, q_gain, k_gain, batch, seq)
    kc, vc = _b_compress(ck, cv, cmp_pos, cmp_w1, cmp_w2, k_gain[0], batch, seq)
    last_tile = seq // B_SWEEP - 1
    bias_c = _bias_tiles(rel_bias, 1, seq // B_CMP_STRIDE + last_tile * (B_SWEEP // B_CMP_STRIDE), B_SWEEP,
                         base=1 - B_CMP_LEN + last_tile * B_SWEEP, tile_step=0, row_step=-B_CMP_STRIDE,
                         col_step=1, dmax=1 << 30)
    oc, sel = _b_cmp_attn(qt, kc, vc, bias_c, batch, seq)
    delta_max = min(seq // B_SWEEP - 1, -(-(_THRESHOLDS[-1] + B_SWEEP - 1) // B_SWEEP))
    bias_s = _bias_tiles(rel_bias, delta_max + 2, B_SWEEP, B_SWEEP, base=-B_SWEEP, tile_step=B_SWEEP,
                         row_step=-1, col_step=1, dmax=1 << 30, dtype=BF16)
    win_tiles = (B_WINDOW - 1 + B_SWEEP - 1) // B_SWEEP + 1
    bias_w = _bias_tiles(rel_bias, win_tiles + 1, B_SWEEP, B_SWEEP, base=-B_SWEEP, tile_step=B_SWEEP,
                         row_step=-1, col_step=1, dmax=B_WINDOW - 1, dtype=BF16)
    o = _b_sparse(qt, ka, vs, wk, vw, sel, bias_s, bias_w, oc, gate_t, batch, seq)
    return _matmul_residual(o.reshape(batch * seq, -1), w_out.astype(BF16), x)


def _c_conv_kernel(cur_ref, halo_ref, w_ref, sm_ref, alog_ref, dtb_ref, qkv_ref, bg_ref, xe_ref):
    ts = cur_ref.shape[1]
    keep = jnp.where(pl.program_id(1) == 0, 0.0, 1.0)
    dk = C_HEAD_DIM
    for c in range(3 * C_HEADS):
        sl = slice(c * dk, (c + 1) * dk)
        xe_ref[c, :8, :] = halo_ref[0, :, sl].astype(F32) * keep
        xe_ref[c, 8:, :] = cur_ref[0, :, sl].astype(F32)
        y = None
        for j in range(C_CONV):
            off = 8 - (C_CONV - 1) + j
            term = w_ref[j:j + 1, sl] * xe_ref[c, off:off + ts, :]
            y = term if y is None else y + term
        y = y * jax.nn.sigmoid(y)
        if c < 2 * C_HEADS:
            y = y * lax.rsqrt(jnp.sum(y * y, axis=-1, keepdims=True) + RMS_EPS)
        if c < C_HEADS:
            y = y * (dk ** -0.5)
        qkv_ref[0, :, sl] = y.astype(qkv_ref.dtype)
    sm = sm_ref[0]
    a = sm + dtb_ref[...]
    softplus = jnp.maximum(a, 0.0) + jnp.log1p(jnp.exp(-jnp.abs(a)))
    g = -jnp.exp(alog_ref[...]) * softplus
    lane = lax.broadcasted_iota(jnp.int32, sm.shape, 1)
    bg_ref[0] = jnp.where(lane < C_HEADS, jax.nn.sigmoid(sm), g)


def _c_conv(proj, small, conv_w, a_log, dt_bias, batch, seq):
    ts = 256
    width = 3 * C_WIDTH
    pad = lambda v: jnp.pad(v, (C_HEADS, 128 - 2 * C_HEADS)).reshape(1, 128)
    return pl.pallas_call(
        _c_conv_kernel,
        out_shape=(jax.ShapeDtypeStruct((batch, seq, width), BF16),
                   jax.ShapeDtypeStruct((batch, seq, 128), F32)),
        grid=(batch, seq // ts),
        in_specs=[pl.BlockSpec((1, ts, width), lambda b, i: (b, i, 0)),
                  pl.BlockSpec((1, 8, width), lambda b, i: (b, jnp.maximum(i * (ts // 8) - 1, 0), 0)),
                  pl.BlockSpec((C_CONV, width), lambda b, i: (0, 0)),
                  pl.BlockSpec((1, ts, 128), lambda b, i: (b, i, 0)),
                  pl.BlockSpec((1, 128), lambda b, i: (0, 0)),
                  pl.BlockSpec((1, 128), lambda b, i: (0, 0))],
        out_specs=(pl.BlockSpec((1, ts, width), lambda b, i: (b, i, 0)),
                   pl.BlockSpec((1, ts, 128), lambda b, i: (b, i, 0))),
        scratch_shapes=[pltpu.VMEM((3 * C_HEADS, ts + 8, C_HEAD_DIM), F32)],
        compiler_params=_params("parallel", "arbitrary"),
        name="c_conv",
    )(proj, proj, conv_w, small, pad(a_log), pad(dt_bias))


def _sum3(x, fn):
    hi = x.astype(BF16)
    r = x - hi.astype(F32)
    mid = r.astype(BF16)
    lo = (r - mid.astype(F32)).astype(BF16)
    return fn(hi) + (fn(mid) + fn(lo))


def _c_chunk_kernel(qkv_ref, bg_ref, bgt_ref, tri_ref, trit_ref, blk_ref, u_ref, w_ref, qg_ref, kg_ref, attn_ref,
                    gc_ref):
    cs = C_CHUNK
    dk = C_HEAD_DIM
    gs = C_GROUP * cs
    row = lax.broadcasted_iota(jnp.int32, (gs, gs), 0)
    col = lax.broadcasted_iota(jnp.int32, (gs, gs), 1)
    same = (row // cs) == (col // cs)
    causal = same & (row >= col)
    strict = same & (row > col)
    eye = jnp.where(row == col, 1.0, 0.0)

    bgc = bg_ref[0]
    tri = tri_ref[...]
    gcum_col = _sum3(bgc, lambda p: _dot(tri, p))
    glast_col = _sum3(bgc, lambda p: _dot(blk_ref[...], p))
    gcum_row = _sum3(bgt_ref[0], lambda p: _dot(p, trit_ref[...]))
    gc_ref[0] = gcum_col
    t_mats, powers = [], []
    for h in range(C_HEADS):
        gc = gcum_col[:, C_HEADS + h:C_HEADS + h + 1]
        gr = gcum_row[C_HEADS + h:C_HEADS + h + 1, :]
        q = qkv_ref[0, :, h * dk:(h + 1) * dk]
        k = qkv_ref[0, :, C_WIDTH + h * dk:C_WIDTH + (h + 1) * dk]
        decay = jnp.exp(jnp.where(causal, gc - gr, NEG_INF))
        k16 = k.astype(BF16)
        low = jnp.where(strict, _dot_nt((k * bgc[:, h:h + 1]).astype(BF16), k16) * decay, 0.0)
        t_mats.append(eye - low)
        powers.append(low.astype(BF16))
        attn = jnp.where(causal, _dot_nt(q.astype(BF16), k16), 0.0) * decay
        attn_ref[0, :, h * gs:(h + 1) * gs] = attn.astype(attn_ref.dtype)
        qg_ref[0, :, h * dk:(h + 1) * dk] = (q * jnp.exp(gc)).astype(qg_ref.dtype)
        glast = glast_col[:, C_HEADS + h:C_HEADS + h + 1]
        kg_ref[0, :, h * dk:(h + 1) * dk] = (k * jnp.exp(glast - gc)).astype(kg_ref.dtype)
    for _ in range(int(math.log2(cs)) - 1):
        powers = [_dot(p, p).astype(BF16) for p in powers]
        t_mats = [t + _dot(t.astype(BF16), p) for t, p in zip(t_mats, powers)]
    for h in range(C_HEADS):
        gc = gcum_col[:, C_HEADS + h:C_HEADS + h + 1]
        beta = bgc[:, h:h + 1]
        k = qkv_ref[0, :, C_WIDTH + h * dk:C_WIDTH + (h + 1) * dk]
        v = qkv_ref[0, :, 2 * C_WIDTH + h * dk:2 * C_WIDTH + (h + 1) * dk]
        t16 = t_mats[h].astype(BF16)
        u_ref[0, :, h * dk:(h + 1) * dk] = _dot(t16, (v * beta).astype(BF16)).astype(u_ref.dtype)
        w_ref[0, :, h * dk:(h + 1) * dk] = _dot(t16, (k * beta * jnp.exp(gc)).astype(BF16)).astype(w_ref.dtype)


def _c_chunks(qkv, bg, bgt, batch, seq):
    gs = C_GROUP * C_CHUNK
    idx = np.arange(gs)
    same = (idx[:, None] // C_CHUNK) == (idx[None, :] // C_CHUNK)
    tri = (same & (idx[:, None] >= idx[None, :])).astype(np.float32)
    wide = lambda width: pl.BlockSpec((1, gs, width), lambda b, i: (b, i, 0))
    shape = lambda width, dtype: jax.ShapeDtypeStruct((batch, seq, width), dtype)
    const = pl.BlockSpec((gs, gs), lambda b, i: (0, 0))
    return pl.pallas_call(
        _c_chunk_kernel,
        out_shape=(shape(C_WIDTH, BF16), shape(C_WIDTH, BF16), shape(C_WIDTH, BF16), shape(C_WIDTH, BF16),
                   shape(C_HEADS * gs, BF16), shape(128, F32)),
        grid=(batch, seq // gs),
        in_specs=[wide(3 * C_WIDTH), wide(128),
                  pl.BlockSpec((1, 2 * C_HEADS, gs), lambda b, i: (b, 0, i)),
                  const, const, const],
        out_specs=(wide(C_WIDTH),) * 4 + (wide(C_HEADS * gs), wide(128)),
        compiler_params=_params("parallel", "parallel"),
        name="c_chunks",
    )(qkv, bg, bgt, jnp.asarray(tri, BF16), jnp.asarray(tri.T, BF16), jnp.asarray(same, BF16))


def _c_scan_kernel(u_ref, w_ref, qg_ref, kg_ref, attn_ref, gc_ref, o_ref, state_ref, vnew_ref):
    @pl.when(pl.program_id(1) == 0)
    def _():
        state_ref[...] = jnp.zeros_like(state_ref)

    cs = C_CHUNK
    dk = C_HEAD_DIM
    gs = C_GROUP * cs
    vnew_ref[...] = jnp.zeros_like(vnew_ref)
    heads = range(C_HEADS)
    cols = [slice(h * dk, (h + 1) * dk) for h in heads]
    states = [state_ref[h] for h in heads]
    for c in range(C_GROUP):
        rs = slice(c * cs, (c + 1) * cs)
        decay_last = jnp.exp(gc_ref[0, (c + 1) * cs - 1:(c + 1) * cs, :])
        both = [_dot(jnp.concatenate([w_ref[0, rs, cols[h]], qg_ref[0, rs, cols[h]]], axis=0),
                     states[h].astype(BF16)) for h in heads]
        v16 = [(u_ref[0, rs, cols[h]] - both[h][:cs]).astype(BF16) for h in heads]
        for h in heads:
            vnew_ref[h, rs, :] = v16[h]
        for h in heads:
            out = both[h][cs:] + _dot(attn_ref[0, rs, h * gs:(h + 1) * gs], vnew_ref[h])
            o_ref[0, rs, cols[h]] = out.astype(o_ref.dtype)
        states = [states[h] * decay_last[:, C_HEADS + h:C_HEADS + h + 1] + _dot_tn(kg_ref[0, rs, cols[h]], v16[h])
                  for h in heads]
    for h in heads:
        state_ref[h] = states[h]


def _c_scan(u, w, qg, kg, attn, gc, batch, seq):
    gs = C_GROUP * C_CHUNK
    wide = lambda width: pl.BlockSpec((1, gs, width), lambda b, c: (b, c, 0))
    return pl.pallas_call(
        _c_scan_kernel,
        out_shape=jax.ShapeDtypeStruct((batch, seq, C_WIDTH), BF16),
        grid=(batch, seq // gs),
        in_specs=[wide(C_WIDTH)] * 4 + [wide(C_HEADS * gs), wide(128)],
        out_specs=wide(C_WIDTH),
        scratch_shapes=[pltpu.VMEM((C_HEADS, C_HEAD_DIM, C_HEAD_DIM), F32),
                        pltpu.VMEM((C_HEADS, gs, C_HEAD_DIM), BF16)],
        compiler_params=_params("parallel", "arbitrary"),
        name="c_scan",
    )(u, w, qg, kg, attn, gc)


def _c_out_kernel(o_ref, z_ref, g_ref, w_ref, x_ref, out_ref):
    dk = C_HEAD_DIM
    parts = []
    for h in range(C_HEADS):
        sl = slice(h * dk, (h + 1) * dk)
        z = z_ref[:, sl].astype(F32)
        parts.append((_rms(o_ref[:, sl].astype(F32), g_ref[...]) * (z * jax.nn.sigmoid(z))).astype(BF16))
    out_ref[...] = x_ref[...] + _dot(jnp.concatenate(parts, axis=-1), w_ref[...])


def _c_out(o, proj, out_gain, w_out, x):
    m, d = x.shape
    z_block = (3 * C_WIDTH) // C_WIDTH
    row = lambda width: pl.BlockSpec((ROW_TILE, width), lambda i: (i, 0))
    return pl.pallas_call(
        _c_out_kernel,
        out_shape=jax.ShapeDtypeStruct((m, d), F32),
        grid=(m // ROW_TILE,),
        in_specs=[row(C_WIDTH),
                  pl.BlockSpec((ROW_TILE, C_WIDTH), lambda i: (i, z_block)),
                  pl.BlockSpec((1, C_HEAD_DIM), lambda i: (0, 0)),
                  pl.BlockSpec((C_WIDTH, d), lambda i: (0, 0)),
                  row(d)],
        out_specs=row(d),
        compiler_params=_params("parallel"),
        name="c_out",
    )(o, proj, out_gain.reshape(1, -1), w_out, x)


def _mixer_c(x, norm1, w_in, conv_w, a_log, dt_bias, out_gain, w_out, batch, seq):
    main = 4 * C_WIDTH
    proj = _norm_matmul(x, norm1, w_in[:, :main].astype(BF16), BF16, 512)
    w_small = jnp.pad(w_in[:, main:], ((0, 0), (0, 128 - 2 * C_HEADS))).astype(BF16)
    small = _norm_matmul(x, norm1, w_small, F32, 128)
    qkv, bg = _c_conv(proj.reshape(batch, seq, main), small.reshape(batch, seq, 128), conv_w, a_log, dt_bias,
                      batch, seq)
    bgt = bg[:, :, :2 * C_HEADS].transpose(0, 2, 1)
    u, w, qg, kg, attn, gc = _c_chunks(qkv, bg, bgt, batch, seq)
    o = _c_scan(u, w, qg, kg, attn, gc, batch, seq)
    return _c_out(o.reshape(batch * seq, C_WIDTH), proj, out_gain, w_out.astype(BF16), x)


def kernel(x, rel_bias, l0_norm1, l0_a_w_in, l0_a_q_gain, l0_a_k_gain, l0_a_w_out, l0_norm2, l0_ffn_w_gate, l0_ffn_w_up, l0_ffn_w_down, l1_norm1, l1_b_w_in, l1_b_q_gain, l1_b_k_gain, l1_b_cmp_pos, l1_b_cmp_w1, l1_b_cmp_w2, l1_b_w_out, l1_norm2, l1_ffn_w_gate, l1_ffn_w_up, l1_ffn_w_down, l2_norm1, l2_c_w_in, l2_c_conv_w, l2_c_a_log, l2_c_dt_bias, l2_c_out_gain, l2_c_w_out, l2_norm2, l2_ffn_w_gate, l2_ffn_w_up, l2_ffn_w_down, l3_norm1, l3_a_w_in, l3_a_q_gain, l3_a_k_gain, l3_a_w_out, l3_norm2, l3_ffn_w_gate, l3_ffn_w_up, l3_ffn_w_down):
    batch, seq, d = x.shape
    h = x.reshape(batch * seq, d)

    def ffn(h, norm2, w_gate, w_up, w_down):
        return _ffn(h, norm2, w_gate.astype(BF16), w_up.astype(BF16), w_down.astype(BF16))

    a_biases = _a_biases(rel_bias)
    h = _mixer_a(h, a_biases, l0_norm1, l0_a_w_in, l0_a_q_gain, l0_a_k_gain, l0_a_w_out, batch, seq)
    h = ffn(h, l0_norm2, l0_ffn_w_gate, l0_ffn_w_up, l0_ffn_w_down)
    h = _mixer_b(h, rel_bias, l1_norm1, l1_b_w_in, l1_b_q_gain, l1_b_k_gain, l1_b_cmp_pos, l1_b_cmp_w1,
                 l1_b_cmp_w2, l1_b_w_out, batch, seq)
    h = ffn(h, l1_norm2, l1_ffn_w_gate, l1_ffn_w_up, l1_ffn_w_down)
    h = _mixer_c(h, l2_norm1, l2_c_w_in, l2_c_conv_w, l2_c_a_log, l2_c_dt_bias, l2_c_out_gain, l2_c_w_out,
                 batch, seq)
    h = ffn(h, l2_norm2, l2_ffn_w_gate, l2_ffn_w_up, l2_ffn_w_down)
    h = _mixer_a(h, a_biases, l3_norm1, l3_a_w_in, l3_a_q_gain, l3_a_k_gain, l3_a_w_out, batch, seq)
    h = ffn(h, l3_norm2, l3_ffn_w_gate, l3_ffn_w_up, l3_ffn_w_down)
    return h.reshape(batch, seq, d)
```

```python
import functools
import math

import numpy as np
import jax
import jax.numpy as jnp
from jax import lax
from jax.experimental import pallas as pl
from jax.experimental.pallas import tpu as pltpu

D_MODEL = 1024
RMS_EPS = 1e-6
NEG_INF = -1e30
TINY = 1e-30
FORCE_SCORE = 1e9

N_BUCKETS = 32
REL_MAX_DISTANCE = 2048
N_HEADS = 16

A_GROUPS = ((128, 1), (512, 4), (2048, 16))
A_HEAD_DIM = 64
A_Q_BLOCK = 128
A_PROJ_TILE = 512
A_STAT_WIDTH = 256
A_BLOCKS_PER_STEP = 2

B_KV_HEADS = 4
B_GROUP = 4
B_HEAD_DIM = 64
B_CMP_LEN = 32
B_CMP_STRIDE = 16
B_CMP_HIDDEN = 256
B_SEL_BLOCK = 64
B_TOP_N = 16
B_WINDOW = 512
B_TILE = 128
B_SWEEP = 256
B_PROJ_WIDTH = 3072
B_GATE_ROWS = 16

C_HEADS = 8
C_HEAD_DIM = 128
C_WIDTH = C_HEADS * C_HEAD_DIM
C_CONV = 4
C_CHUNK = 64
C_GROUP = 4

FFN_HIDDEN = 2816
FFN_TILE = 1024

ROW_TILE = 512
VMEM_LIMIT = 48 * 1024 * 1024

LOG2E = math.log2(math.e)

F32 = jnp.float32
BF16 = jnp.bfloat16

NT_DIMS = (((1,), (1,)), ((), ()))
TN_DIMS = (((0,), (0,)), ((), ()))


def _params(*semantics):
    return pltpu.CompilerParams(dimension_semantics=semantics, vmem_limit_bytes=VMEM_LIMIT)


def _dot(a, b):
    return jnp.dot(a, b, preferred_element_type=F32)


def _dot_nt(a, b):
    return lax.dot_general(a, b, NT_DIMS, preferred_element_type=F32)


def _dot_tn(a, b):
    return lax.dot_general(a, b, TN_DIMS, preferred_element_type=F32)


def _rms(x, gain):
    return x * lax.rsqrt(jnp.mean(x * x, axis=-1, keepdims=True) + RMS_EPS) * gain


def _bucket_thresholds():
    d = np.arange(1 << 15)
    max_exact = N_BUCKETS // 2
    d_f = np.maximum(d, 1).astype(np.float32)
    large = max_exact + (np.log(d_f / np.float32(max_exact)) / np.float32(math.log(REL_MAX_DISTANCE / max_exact))
                         * np.float32(N_BUCKETS - max_exact)).astype(np.int32)
    bucket = np.where(d < max_exact, d, np.minimum(large, N_BUCKETS - 1))
    return [int(np.argmax(bucket >= k)) if np.any(bucket >= k) else int(1 << 30) for k in range(N_BUCKETS)]


_THRESHOLDS = _bucket_thresholds()


def _bias_tile_kernel(tbl_ref, o_ref, *, base, tile_step, row_step, col_step, dmax, dil):
    h = pl.program_id(0)
    t = pl.program_id(1)
    shape = o_ref.shape[2:]
    i = lax.broadcasted_iota(jnp.int32, shape, 0)
    j = lax.broadcasted_iota(jnp.int32, shape, 1)
    dist = base + tile_step * t + row_step * i + col_step * j
    d = dist * dil
    val = jnp.full(shape, tbl_ref[0, h], F32)
    for k in range(1, N_BUCKETS):
        val = jnp.where(d >= _THRESHOLDS[k], tbl_ref[k, h], val)
    valid = (dist >= 0) & (dist <= dmax)
    o_ref[0, 0] = jnp.where(valid, val * LOG2E, NEG_INF).astype(o_ref.dtype)


def _bias_tiles(rel_bias, n_tiles, rows, cols, *, base, tile_step, row_step, col_step, dmax, dil=1, dtype=F32):
    kern = functools.partial(_bias_tile_kernel, base=base, tile_step=tile_step, row_step=row_step,
                             col_step=col_step, dmax=dmax, dil=dil)
    return pl.pallas_call(
        kern,
        out_shape=jax.ShapeDtypeStruct((N_HEADS, n_tiles, rows, cols), dtype),
        grid=(N_HEADS, n_tiles),
        in_specs=[pl.BlockSpec(memory_space=pltpu.SMEM)],
        out_specs=pl.BlockSpec((1, 1, rows, cols), lambda h, t: (h, t, 0, 0)),
        compiler_params=_params("parallel", "parallel"),
        name="bias_tiles",
    )(rel_bias)


def _resident(shape):
    return pl.BlockSpec(shape, lambda i: (0,) * len(shape), pipeline_mode=pl.Buffered(1))


def _norm_matmul_kernel(x_ref, g_ref, w_ref, o_ref, *, tn):
    h = _rms(x_ref[...], g_ref[...]).astype(BF16)
    for j in range(w_ref.shape[1] // tn):
        o_ref[:, j * tn:(j + 1) * tn] = _dot(h, w_ref[:, j * tn:(j + 1) * tn]).astype(o_ref.dtype)


def _norm_matmul(x, gain, w, out_dtype, tn):
    m, d = x.shape
    n = w.shape[1]
    return pl.pallas_call(
        functools.partial(_norm_matmul_kernel, tn=tn),
        out_shape=jax.ShapeDtypeStruct((m, n), out_dtype),
        grid=(m // ROW_TILE,),
        in_specs=[pl.BlockSpec((ROW_TILE, d), lambda i: (i, 0)),
                  _resident((1, d)),
                  _resident((d, n))],
        out_specs=pl.BlockSpec((ROW_TILE, n), lambda i: (i, 0)),
        compiler_params=_params("parallel"),
        name="norm_matmul",
    )(x, gain.reshape(1, d), w)


def _matmul_residual_kernel(a_ref, w_ref, x_ref, o_ref):
    o_ref[...] = x_ref[...] + _dot(a_ref[...], w_ref[...])


def _matmul_residual(a, w, x):
    m, k = a.shape
    d = w.shape[1]
    return pl.pallas_call(
        _matmul_residual_kernel,
        out_shape=jax.ShapeDtypeStruct((m, d), F32),
        grid=(m // ROW_TILE,),
        in_specs=[pl.BlockSpec((ROW_TILE, k), lambda i: (i, 0)),
                  pl.BlockSpec((k, d), lambda i: (0, 0)),
                  pl.BlockSpec((ROW_TILE, d), lambda i: (i, 0))],
        out_specs=pl.BlockSpec((ROW_TILE, d), lambda i: (i, 0)),
        compiler_params=_params("parallel"),
        name="matmul_residual",
    )(a, w, x)


def _ffn_kernel(x_ref, g_ref, wg_ref, wu_ref, wd_ref, o_ref):
    x = x_ref[...]
    h = _rms(x, g_ref[...]).astype(BF16)
    hidden = wg_ref.shape[1]
    acc = x
    for lo in range(0, hidden, FFN_TILE):
        hi = min(lo + FFN_TILE, hidden)
        a = _dot(h, wg_ref[:, lo:hi])
        b = _dot(h, wu_ref[:, lo:hi])
        acc = acc + _dot((a * jax.nn.sigmoid(a) * b).astype(BF16), wd_ref[lo:hi, :])
    o_ref[...] = acc


def _ffn(x, gain, w_gate, w_up, w_down):
    m, d = x.shape
    hidden = w_gate.shape[1]
    return pl.pallas_call(
        _ffn_kernel,
        out_shape=jax.ShapeDtypeStruct((m, d), F32),
        grid=(m // ROW_TILE,),
        in_specs=[pl.BlockSpec((ROW_TILE, d), lambda i: (i, 0)),
                  _resident((1, d)),
                  _resident((d, hidden)), _resident((d, hidden)), _resident((hidden, d))],
        out_specs=pl.BlockSpec((ROW_TILE, d), lambda i: (i, 0)),
        compiler_params=_params("parallel"),
        name="ffn",
    )(x, gain.reshape(1, d), w_gate, w_up, w_down)


def _a_proj_kernel(x_ref, g_ref, w_ref, qg_ref, kg_ref, o_ref, h_ref, x_scr, *, dil):
    rows = ROW_TILE // dil
    xn = _rms(x_ref[...], g_ref[...])
    if dil == 1:
        h_ref[...] = xn.astype(BF16)
    else:
        slabs = xn.shape[1] // 128
        for c in range(slabs):
            x_scr[c] = xn[:, c * 128:(c + 1) * 128]
        for r in range(dil):
            picked = [x_scr[c, pl.ds(r, rows, stride=dil), :] for c in range(slabs)]
            h_ref[r * rows:(r + 1) * rows, :] = jnp.concatenate(picked, axis=1).astype(BF16)
    h = h_ref[...]
    width = w_ref.shape[1]
    hd = width // 3
    low = lax.broadcasted_iota(jnp.int32, (ROW_TILE, 128), 1) < A_HEAD_DIM
    for j in range(width // A_PROJ_TILE):
        res = _dot(h, w_ref[:, j * A_PROJ_TILE:(j + 1) * A_PROJ_TILE])
        kind = (j * A_PROJ_TILE) // hd
        if kind < 2:
            parts = []
            for c in range(A_PROJ_TILE // 128):
                y = res[:, c * 128:(c + 1) * 128]
                sq = y * y
                tot = jnp.sum(sq, axis=-1, keepdims=True)
                lo = jnp.sum(jnp.where(low, sq, 0.0), axis=-1, keepdims=True)
                ss = jnp.where(low, lo, tot - lo)
                parts.append(y * lax.rsqrt(ss * (1.0 / A_HEAD_DIM) + RMS_EPS))
            res = jnp.concatenate(parts, axis=1) * (qg_ref if kind == 0 else kg_ref)[...]
        res = res.astype(BF16)
        for r in range(dil):
            off = r * width + j * A_PROJ_TILE
            o_ref[:, off:off + A_PROJ_TILE] = res[r * rows:(r + 1) * rows]


def _a_proj(x, gain, w, q_gain, k_gain, dil):
    m, d = x.shape
    width = w.shape[1]
    reps = A_PROJ_TILE // A_HEAD_DIM
    qg = jnp.tile(q_gain * (A_HEAD_DIM ** -0.5 * LOG2E), reps).reshape(1, A_PROJ_TILE)
    kg = jnp.tile(k_gain, reps).reshape(1, A_PROJ_TILE)
    return pl.pallas_call(
        functools.partial(_a_proj_kernel, dil=dil),
        out_shape=jax.ShapeDtypeStruct((m // dil, dil * width), BF16),
        grid=(m // ROW_TILE,),
        in_specs=[pl.BlockSpec((ROW_TILE, d), lambda i: (i, 0)),
                  pl.BlockSpec((1, d), lambda i: (0, 0)),
                  pl.BlockSpec((d, width), lambda i: (0, 0)),
                  pl.BlockSpec((1, A_PROJ_TILE), lambda i: (0, 0)),
                  pl.BlockSpec((1, A_PROJ_TILE), lambda i: (0, 0))],
        out_specs=pl.BlockSpec((ROW_TILE // dil, dil * width), lambda i: (i, 0)),
        scratch_shapes=[pltpu.VMEM((ROW_TILE, d), BF16), pltpu.VMEM((d // 128, ROW_TILE, 128), F32)],
        compiler_params=_params("parallel"),
        name="a_proj",
    )(x, gain.reshape(1, d), w, qg, kg)


def _a_attn_kernel(q_ref, kp_ref, kc_ref, vp_ref, vc_ref, bias_ref, o_ref, stat_ref):
    first = (pl.program_id(2) == 0).astype(jnp.int32)
    nq = A_Q_BLOCK
    lane = lax.broadcasted_iota(jnp.int32, (nq, 128), 1)
    low = lane < A_HEAD_DIM
    ones = jnp.ones((2 * nq, 128), BF16)
    for sub in range(A_BLOCKS_PER_STEP):
        rows = slice(sub * nq, (sub + 1) * nq)
        max_tile = jnp.zeros((nq, 128), F32)
        den_tile = jnp.ones((nq, 128), F32)
        for pair in range(N_HEADS // 2):
            sl = slice(pair * 128, (pair + 1) * 128)
            q = q_ref[0, rows, sl]
            zero = jnp.zeros_like(q)
            qq = jnp.concatenate([jnp.where(low, q, zero), jnp.where(low, zero, q)], axis=0)
            if sub == 0:
                kk = jnp.concatenate([kp_ref[0, :, sl], kc_ref[0, :nq, sl]], axis=0)
                vv = jnp.concatenate([vp_ref[0, :, sl], vc_ref[0, :nq, sl]], axis=0)
                base = 2 * pair + N_HEADS * first
            else:
                kk = kc_ref[0, :, sl]
                vv = vc_ref[0, :, sl]
                base = 2 * pair
            s = _dot_nt(qq, kk) + jnp.concatenate([bias_ref[base], bias_ref[base + 1]], axis=0)
            m = jnp.max(s, axis=-1, keepdims=True)
            acc = _dot(jnp.exp2(s - m).astype(BF16), jnp.concatenate([vv, ones], axis=1))
            o_ref[0, rows, sl] = jnp.where(low, acc[:nq, :128], acc[nq:, :128]).astype(o_ref.dtype)
            first_head = lane == 2 * pair
            second_head = lane == 2 * pair + 1
            max_tile = jnp.where(first_head, m[:nq], jnp.where(second_head, m[nq:], max_tile))
            den_tile = jnp.where(first_head, acc[:nq, 128:], jnp.where(second_head, acc[nq:, 128:], den_tile))
        stat_ref[0, rows, :128] = max_tile
        stat_ref[0, rows, 128:] = den_tile


def _a_attention(proj, bias, dil, batch, seq):
    length = seq // dil
    nblk = length // A_Q_BLOCK
    hd = N_HEADS * A_HEAD_DIM
    pv = proj.reshape(batch, length, dil * 3 * hd)

    per = A_BLOCKS_PER_STEP
    step_rows = per * A_Q_BLOCK
    assert nblk % per == 0

    def spec(off, prev):
        if prev:
            return pl.BlockSpec((1, A_Q_BLOCK, hd), lambda b, r, i: (b, jnp.maximum(per * i - 1, 0), r * 3 + off))
        return pl.BlockSpec((1, step_rows, hd), lambda b, r, i: (b, i, r * 3 + off))

    o, stats = pl.pallas_call(
        _a_attn_kernel,
        out_shape=(jax.ShapeDtypeStruct((batch, length, dil * hd), BF16),
                   jax.ShapeDtypeStruct((batch, length, dil * A_STAT_WIDTH), F32)),
        grid=(batch, dil, nblk // per),
        in_specs=[spec(0, False), spec(1, True), spec(1, False), spec(2, True), spec(2, False),
                  pl.BlockSpec((2 * N_HEADS, A_Q_BLOCK, 2 * A_Q_BLOCK), lambda b, r, i: (0, 0, 0))],
        out_specs=(pl.BlockSpec((1, step_rows, hd), lambda b, r, i: (b, i, r)),
                   pl.BlockSpec((1, step_rows, A_STAT_WIDTH), lambda b, r, i: (b, i, r))),
        compiler_params=_params("parallel", "parallel", "arbitrary"),
        name="a_attention",
    )(pv, pv, pv, pv, pv, bias)
    return o.reshape(batch * length, dil * hd), stats.reshape(batch * length, dil * A_STAT_WIDTH)


def _a_out_kernel(o0_ref, o1_ref, o2_ref, s0_ref, s1_ref, s2_ref, e_ref, w_ref, x_ref, out_ref, o_scr, s_scr):
    hd = N_HEADS * A_HEAD_DIM
    sw = A_STAT_WIDTH
    for g, (o_ref, s_ref) in enumerate(((o0_ref, s0_ref), (o1_ref, s1_ref), (o2_ref, s2_ref))):
        dil = A_GROUPS[g][1]
        rows = ROW_TILE // dil
        for r in range(dil):
            dst = pl.ds(r, rows, stride=dil) if dil > 1 else slice(None)
            s_scr[g, 0, dst, :] = s_ref[:, r * sw:r * sw + 128]
            s_scr[g, 1, dst, :] = s_ref[:, r * sw + 128:(r + 1) * sw]
            for c in range(hd // 128):
                o_scr[g, c, dst, :] = o_ref[:, r * hd + c * 128:r * hd + (c + 1) * 128].astype(F32)
    groups = range(len(A_GROUPS))
    top = functools.reduce(jnp.maximum, [s_scr[g, 0] for g in groups])
    es = [jnp.exp2(s_scr[g, 0] - top) for g in groups]
    inv = 1.0 / sum(es[g] * s_scr[g, 1] for g in groups)
    expand = e_ref[...]
    acc = None
    for g in groups:
        o_g = jnp.concatenate([o_scr[g, c] for c in range(hd // 128)], axis=1)
        term = _dot((es[g] * inv).astype(BF16), expand) * o_g
        acc = term if acc is None else acc + term
    out_ref[...] = x_ref[...] + _dot(acc.astype(BF16), w_ref[...])


def _a_out(outs, stats, w_out, x):
    m, d = x.shape
    hd = N_HEADS * A_HEAD_DIM
    expand = np.zeros((128, hd), np.float32)
    for h in range(N_HEADS):
        expand[h, h * A_HEAD_DIM:(h + 1) * A_HEAD_DIM] = 1.0
    grouped = lambda width: [pl.BlockSpec((ROW_TILE // dil, dil * width), lambda i: (i, 0)) for _, dil in A_GROUPS]
    return pl.pallas_call(
        _a_out_kernel,
        out_shape=jax.ShapeDtypeStruct((m, d), F32),
        grid=(m // ROW_TILE,),
        in_specs=grouped(hd) + grouped(A_STAT_WIDTH) + [
            pl.BlockSpec((128, hd), lambda i: (0, 0)),
            pl.BlockSpec((hd, d), lambda i: (0, 0)),
            pl.BlockSpec((ROW_TILE, d), lambda i: (i, 0))],
        out_specs=pl.BlockSpec((ROW_TILE, d), lambda i: (i, 0)),
        scratch_shapes=[pltpu.VMEM((len(A_GROUPS), hd // 128, ROW_TILE, 128), F32),
                        pltpu.VMEM((len(A_GROUPS), 2, ROW_TILE, 128), F32)],
        compiler_params=_params("parallel"),
        name="a_out",
    )(*outs, *stats, jnp.asarray(expand, BF16), w_out, x)


def _a_biases(rel_bias):
    biases = []
    for window, dil in A_GROUPS:
        steps = window // dil
        assert steps == A_Q_BLOCK
        bias = _bias_tiles(rel_bias, 1, A_Q_BLOCK, 2 * A_Q_BLOCK, base=A_Q_BLOCK, tile_step=0, row_step=1,
                           col_step=-1, dmax=steps, dil=dil)[:, 0]
        biases.append(jnp.concatenate([bias, bias.at[:, :, :A_Q_BLOCK].set(NEG_INF)], axis=0))
    return biases


def _mixer_a(x, biases, norm1, w_in, q_gain, k_gain, w_out, batch, seq):
    w_in = w_in.astype(BF16)
    group_width = 3 * N_HEADS * A_HEAD_DIM
    outs, stats = [], []
    for gi, (_, dil) in enumerate(A_GROUPS):
        assert (seq // dil) % A_Q_BLOCK == 0 and seq % ROW_TILE == 0
        bias = biases[gi]
        proj = _a_proj(x, norm1, w_in[:, gi * group_width:(gi + 1) * group_width], q_gain[gi], k_gain[gi], dil)
        o, stat = _a_attention(proj, bias, dil, batch, seq)
        outs.append(o)
        stats.append(stat)
    return _a_out(outs, stats, w_out.astype(BF16), x)


def _b_prep_kernel(p_ref, qg_ref, kg_ref, qt_ref, ck_ref, cv_ref, ka_ref, vs_ref, wk_ref, vw_ref, gate_ref):
    dh = B_HEAD_DIM
    ts = p_ref.shape[1]
    n_sel = ka_ref.shape[3] - dh
    low = lax.broadcasted_iota(jnp.int32, (ts, 128), 1) < dh

    def slab(c):
        return p_ref[0, :, c * 128:(c + 1) * 128].astype(F32)

    def normed(x, gain):
        sq = x * x
        tot = jnp.sum(sq, axis=-1, keepdims=True)
        first = jnp.sum(jnp.where(low, sq, 0.0), axis=-1, keepdims=True)
        ss = jnp.where(low, first, tot - first)
        return x * lax.rsqrt(ss * (1.0 / dh) + RMS_EPS) * jnp.concatenate([gain, gain], axis=1)

    def halves(x):
        return x[:, :dh], x[:, dh:]

    qg = qg_ref[...] * (dh ** -0.5 * LOG2E)
    for c in range(N_HEADS // 2):
        xt = normed(slab(c), qg).T.astype(BF16)
        qt_ref[0, 2 * c] = xt[:dh]
        qt_ref[0, 2 * c + 1] = xt[dh:]
    base = N_HEADS // 2
    pairs = B_KV_HEADS // 2
    pos = pl.program_id(1) * ts + lax.broadcasted_iota(jnp.int32, (ts, 128), 0)
    lane = lax.broadcasted_iota(jnp.int32, (ts, 128), 1)
    onehot = jnp.where(lane - dh == pos // B_SEL_BLOCK, 1.0, 0.0)
    ones = jnp.ones((vs_ref.shape[2] - dh, ts), BF16)
    for j in range(pairs):
        for ref, off in ((ck_ref, 0), (cv_ref, pairs)):
            a, b = halves(slab(base + off + j))
            ref[0, 2 * j] = a.astype(BF16)
            ref[0, 2 * j + 1] = b.astype(BF16)
        k_sel = normed(slab(base + 2 * pairs + j), kg_ref[1:2, :])
        for n, k in zip((2 * j, 2 * j + 1), (k_sel, pltpu.roll(k_sel, dh, 1))):
            ka_ref[0, n] = jnp.where(low, k, onehot)[:, :dh + n_sel].astype(BF16)
        for n, k in zip((2 * j, 2 * j + 1), halves(normed(slab(base + 4 * pairs + j), kg_ref[2:3, :]))):
            wk_ref[0, n] = k.astype(BF16)
        for ref, off in ((vs_ref, 3 * pairs), (vw_ref, 5 * pairs)):
            xt = slab(base + off + j).T.astype(BF16)
            for n, v in zip((2 * j, 2 * j + 1), (xt[:dh], xt[dh:])):
                ref[0, n, :dh, :] = v
                ref[0, n, dh:, :] = ones
    gate = jax.nn.sigmoid(slab(base + 6 * pairs)).T
    rows = gate_ref.shape[2]
    for n in range(B_KV_HEADS):
        gate_ref[0, n] = gate[n * rows:(n + 1) * rows]


def _b_prep(proj, q_gain, k_gain, batch, seq):
    ts = 256
    dh = B_HEAD_DIM
    n_sel = seq // B_SEL_BLOCK
    assert dh + n_sel <= 128
    rows_shape = lambda width: jax.ShapeDtypeStruct((batch, B_KV_HEADS, seq, width), BF16)
    rows_spec = lambda width: pl.BlockSpec((1, B_KV_HEADS, ts, width), lambda b, i: (b, 0, i, 0))
    cols_shape = lambda heads, height, dtype: jax.ShapeDtypeStruct((batch, heads, height, seq), dtype)
    cols_spec = lambda heads, height: pl.BlockSpec((1, heads, height, ts), lambda b, i: (b, 0, 0, i))
    return pl.pallas_call(
        _b_prep_kernel,
        out_shape=(cols_shape(N_HEADS, dh, BF16), rows_shape(dh), rows_shape(dh), rows_shape(dh + n_sel),
                   cols_shape(B_KV_HEADS, dh + 16, BF16), rows_shape(dh), cols_shape(B_KV_HEADS, dh + 16, BF16),
                   cols_shape(B_KV_HEADS, B_GATE_ROWS, F32)),
        grid=(batch, seq // ts),
        in_specs=[pl.BlockSpec((1, ts, B_PROJ_WIDTH), lambda b, i: (b, i, 0)),
                  pl.BlockSpec((1, dh), lambda b, i: (0, 0)),
                  pl.BlockSpec((3, dh), lambda b, i: (0, 0))],
        out_specs=(cols_spec(N_HEADS, dh), rows_spec(dh), rows_spec(dh), rows_spec(dh + n_sel),
                   cols_spec(B_KV_HEADS, dh + 16), rows_spec(dh), cols_spec(B_KV_HEADS, dh + 16),
                   cols_spec(B_KV_HEADS, B_GATE_ROWS)),
        compiler_params=_params("parallel", "parallel"),
        name="b_prep",
    )(proj, q_gain.reshape(1, dh), k_gain)


def _b_compress_kernel(tk_ref, tv_ref, pos_ref, w1_ref, w2_ref, kg_ref, kc_ref, vc_ref):
    half = (B_CMP_LEN // 2) * B_HEAD_DIM
    for kv, (t_ref, out_ref) in enumerate(((tk_ref, kc_ref), (tv_ref, vc_ref))):
        t = t_ref[0, 0].astype(F32)
        top = (t + pos_ref[kv, 0:1, :]).astype(BF16)
        bot = (t + pos_ref[kv, 1:2, :]).astype(BF16)
        a1 = _dot(top, w1_ref[kv, :half, :])
        a2 = _dot(bot, w1_ref[kv, half:, :])
        hidden = a1 + pltpu.roll(a2, a2.shape[0] - 1, 0)
        out = _dot(jax.nn.gelu(hidden).astype(BF16), w2_ref[kv])
        if kv == 0:
            out = _rms(out, kg_ref[...])
        out_ref[0, 0] = out.astype(out_ref.dtype)


def _b_compress(ck, cv, cmp_pos, cmp_w1, cmp_w2, k_gain0, batch, seq):
    rows = seq // B_CMP_STRIDE
    half = (B_CMP_LEN // 2) * B_HEAD_DIM
    tk = ck.reshape(batch, B_KV_HEADS, rows, half)
    tv = cv.reshape(batch, B_KV_HEADS, rows, half)
    pos = cmp_pos.reshape(2, 2, half)
    t_spec = pl.BlockSpec((1, 1, rows, half), lambda b, n: (b, n, 0, 0))
    o_spec = pl.BlockSpec((1, 1, rows, B_HEAD_DIM), lambda b, n: (b, n, 0, 0))
    shape = jax.ShapeDtypeStruct((batch, B_KV_HEADS, rows, B_HEAD_DIM), BF16)
    return pl.pallas_call(
        _b_compress_kernel,
        out_shape=(shape, shape),
        grid=(batch, B_KV_HEADS),
        in_specs=[t_spec, t_spec,
                  pl.BlockSpec((2, 2, half), lambda b, n: (0, 0, 0)),
                  pl.BlockSpec((2, 2 * half, B_CMP_HIDDEN), lambda b, n: (0, 0, 0)),
                  pl.BlockSpec((2, B_CMP_HIDDEN, B_HEAD_DIM), lambda b, n: (0, 0, 0)),
                  pl.BlockSpec((1, B_HEAD_DIM), lambda b, n: (0, 0))],
        out_specs=(o_spec, o_spec),
        compiler_params=_params("parallel", "parallel"),
        name="b_compress",
    )(tk, tv, pos, cmp_w1.astype(BF16), cmp_w2.astype(BF16), k_gain0.reshape(1, -1))


def _b_cmp_attn_kernel(qt_ref, kc_ref, vct_ref, bias_ref, c2s_ref, oc_ref, sel_ref, imp_ref, *, top_n):
    tq = B_SWEEP
    n_sel = imp_ref.shape[0]
    qt = jnp.concatenate([qt_ref[0, g] for g in range(B_GROUP)], axis=1)
    n_cmp_pad = kc_ref.shape[2]
    shift = tq // B_CMP_STRIDE
    off = pl.multiple_of((pl.num_programs(2) - 1 - pl.program_id(2)) * shift, shift)
    bias = jnp.concatenate([bias_ref[g, 0, pl.ds(off, n_cmp_pad), :] for g in range(B_GROUP)], axis=1)
    s = _dot(kc_ref[0, 0], qt) + bias
    m = jnp.max(s, axis=0, keepdims=True)
    e = jnp.exp2(s - m)
    z = jnp.maximum(jnp.sum(e, axis=0, keepdims=True), TINY)
    pos = pl.program_id(2) * tq + lax.broadcasted_iota(jnp.int32, (1, tq), 1)
    sees_any = jnp.concatenate([pos >= B_CMP_LEN - 1] * B_GROUP, axis=1)
    p = e * jnp.where(sees_any, 1.0 / z, 0.0)
    oct = _dot(vct_ref[0, 0], p.astype(BF16))
    for g in range(B_GROUP):
        oc_ref[0, g] = oct[:, g * tq:(g + 1) * tq]

    p_sum = p[:, 0:tq] + p[:, tq:2 * tq] + p[:, 2 * tq:3 * tq] + p[:, 3 * tq:4 * tq]
    hi = p_sum.astype(BF16)
    lo = (p_sum - hi.astype(F32)).astype(BF16)
    c2s = c2s_ref[...]
    imp = _dot(c2s, hi) + _dot(c2s, lo)

    t = pl.program_id(2) * tq + lax.broadcasted_iota(jnp.int32, (n_sel, tq), 1)
    blk = lax.broadcasted_iota(jnp.int32, (n_sel, tq), 0)
    cur = t // B_SEL_BLOCK
    forced = (blk == 0) | (blk == cur) | (blk == cur - 1)
    imp = jnp.where(forced, FORCE_SCORE, jnp.where(blk * B_SEL_BLOCK <= t, imp, NEG_INF))
    imp_ref[...] = imp

    def count(i, rank):
        row = imp_ref[pl.ds(i, 1), :]
        ahead = jnp.where(row > imp, 1.0, jnp.where(row == imp, jnp.where(blk > i, 1.0, 0.0), 0.0))
        return rank + ahead

    n_live = jnp.minimum(n_sel, (pl.program_id(2) + 1) * (tq // B_SEL_BLOCK))
    rank = lax.fori_loop(0, n_live, count, jnp.zeros((n_sel, tq), F32))
    sel_ref[0, 0] = jnp.where(rank < top_n, 0.0, NEG_INF).astype(sel_ref.dtype)


def _b_cmp_attn(qt, kc, vc, bias_c, batch, seq):
    n_sel = seq // B_SEL_BLOCK
    n_cmp_pad = seq // B_CMP_STRIDE
    n_cmp = (seq - B_CMP_LEN) // B_CMP_STRIDE + 1
    c = np.arange(n_cmp_pad)[None, :] * B_CMP_STRIDE
    j = np.arange(n_sel)[:, None] * B_SEL_BLOCK
    c2s = ((c < j + B_SEL_BLOCK) & (c + B_CMP_LEN > j) & (np.arange(n_cmp_pad)[None, :] < n_cmp)).astype(np.float32)
    kern = functools.partial(_b_cmp_attn_kernel, top_n=min(B_TOP_N, n_sel))
    return pl.pallas_call(
        kern,
        out_shape=(jax.ShapeDtypeStruct((batch, N_HEADS, B_HEAD_DIM, seq), F32),
                   jax.ShapeDtypeStruct((batch, B_KV_HEADS, n_sel, seq), BF16)),
        grid=(batch, B_KV_HEADS, seq // B_SWEEP),
        in_specs=[pl.BlockSpec((1, B_GROUP, B_HEAD_DIM, B_SWEEP), lambda b, n, i: (b, n, 0, i)),
                  pl.BlockSpec((1, 1, n_cmp_pad, B_HEAD_DIM), lambda b, n, i: (b, n, 0, 0)),
                  pl.BlockSpec((1, 1, B_HEAD_DIM, n_cmp_pad), lambda b, n, i: (b, n, 0, 0)),
                  pl.BlockSpec((B_GROUP, 1, bias_c.shape[2], B_SWEEP), lambda b, n, i: (n, 0, 0, 0)),
                  pl.BlockSpec((n_sel, n_cmp_pad), lambda b, n, i: (0, 0))],
        out_specs=(pl.BlockSpec((1, B_GROUP, B_HEAD_DIM, B_SWEEP), lambda b, n, i: (b, n, 0, i)),
                   pl.BlockSpec((1, 1, n_sel, B_SWEEP), lambda b, n, i: (b, n, 0, i))),
        scratch_shapes=[pltpu.VMEM((n_sel, B_SWEEP), F32)],
        compiler_params=_params("parallel", "parallel", "arbitrary"),
        name="b_cmp_attn",
    )(qt, kc, vc.transpose(0, 1, 3, 2), bias_c, jnp.asarray(c2s, BF16))


def _b_sparse_kernel(qt_ref, ka_ref, vs_ref, wk_ref, vw_ref, sel_ref, bs_ref, bw_ref, oc_ref, gate_ref,
                     o_ref, acc_ref, sa_ref, sb_ref, *, delta_max, win_tiles):
    tq = B_SWEEP
    dh = B_HEAD_DIM
    cols = B_GROUP * tq
    qi = pl.program_id(2)
    n_tiles = ka_ref.shape[2] // tq
    qt = jnp.concatenate([qt_ref[0, g] for g in range(B_GROUP)], axis=1)
    q_aug = jnp.concatenate([qt, jnp.concatenate([sel_ref[0, 0]] * B_GROUP, axis=1)], axis=0)

    def tile_start(kt):
        return pl.multiple_of(jnp.clip(kt, 0, n_tiles - 1) * tq, tq)

    def normalised(acc):
        return acc[:dh] * (1.0 / acc[dh:dh + 1])

    def sel_bias(kt):
        d = jnp.clip(qi - kt, -1, delta_max) + 1
        return jnp.concatenate([bs_ref[g, d] for g in range(B_GROUP)], axis=1)

    def sel_scores(kt):
        return _dot(ka_ref[0, 0, pl.ds(tile_start(kt), tq), :], q_aug).astype(BF16) + sel_bias(kt)

    def consume(s_buf, kt, m_old):
        s = s_buf[...]
        m_new = jnp.maximum(m_old, jnp.max(s, axis=0, keepdims=True).astype(F32))
        alpha = jnp.exp2(m_old - m_new)
        p = jnp.exp2(s - m_new.astype(BF16))
        acc_ref[...] = alpha * acc_ref[...] + _dot(vs_ref[0, 0, :, pl.ds(tile_start(kt), tq)], p)
        return m_new

    acc_ref[...] = jnp.zeros(acc_ref.shape, F32)
    sa_ref[...] = sel_scores(0)

    def pair(kt, m):
        sb_ref[...] = sel_scores(kt + 1)
        m = consume(sa_ref, kt, m)
        sa_ref[...] = sel_scores(kt + 2)
        return consume(sb_ref, kt + 1, m)

    n_pairs = (qi + 2) // 2
    n_double = n_pairs // 2
    m = lax.fori_loop(0, n_double, lambda j, m: pair(4 * j + 2, pair(4 * j, m)),
                      jnp.full((1, cols), NEG_INF, F32))
    lax.fori_loop(2 * n_double, n_pairs, lambda j, m: pair(2 * j, m), m)
    o_s = normalised(acc_ref[...])

    tiles = []
    for u in range(win_tiles):
        kt = qi - (win_tiles - 1) + u
        d = jnp.where(kt >= 0, qi - kt, -1) + 1
        bias = jnp.concatenate([bw_ref[g, d] for g in range(B_GROUP)], axis=1)
        tiles.append((_dot(wk_ref[0, 0, pl.ds(tile_start(kt), tq), :], qt).astype(BF16) + bias, kt))
    m = None
    for s, _ in tiles:
        tile_max = jnp.max(s, axis=0, keepdims=True)
        m = tile_max if m is None else jnp.maximum(m, tile_max)
    acc = None
    for s, kt in tiles:
        pv = _dot(vw_ref[0, 0, :, pl.ds(tile_start(kt), tq)], jnp.exp2(s - m))
        acc = pv if acc is None else acc + pv
    o_w = normalised(acc)

    gate = gate_ref[0, 0]
    merged = []
    for g in range(B_GROUP):
        cs = slice(g * tq, (g + 1) * tq)
        merged.append(gate[3 * g:3 * g + 1] * oc_ref[0, g] + gate[3 * g + 1:3 * g + 2] * o_s[:, cs]
                      + gate[3 * g + 2:3 * g + 3] * o_w[:, cs])
    for pair in range(B_GROUP // 2):
        both = jnp.concatenate([merged[2 * pair], merged[2 * pair + 1]], axis=0)
        o_ref[0, :, pair * 2 * dh:(pair + 1) * 2 * dh] = both.T.astype(o_ref.dtype)


def _b_sparse(qt, ka, vs, wk, vw, sel, bias_s, bias_w, oc, gate, batch, seq):
    n_sel = seq // B_SEL_BLOCK
    dh = B_HEAD_DIM
    n_ds = bias_s.shape[1]
    n_dw = bias_w.shape[1]
    vrows = vs.shape[2]
    kern = functools.partial(_b_sparse_kernel, delta_max=n_ds - 2, win_tiles=n_dw - 1)
    whole = lambda rows, width: pl.BlockSpec((1, 1, rows, width), lambda b, n, i: (b, n, 0, 0))
    return pl.pallas_call(
        kern,
        out_shape=jax.ShapeDtypeStruct((batch, seq, N_HEADS * dh), BF16),
        grid=(batch, B_KV_HEADS, seq // B_SWEEP),
        in_specs=[pl.BlockSpec((1, B_GROUP, dh, B_SWEEP), lambda b, n, i: (b, n, 0, i)),
                  whole(seq, dh + n_sel), whole(vrows, seq), whole(seq, dh), whole(vrows, seq),
                  pl.BlockSpec((1, 1, n_sel, B_SWEEP), lambda b, n, i: (b, n, 0, i)),
                  pl.BlockSpec((B_GROUP, n_ds, B_SWEEP, B_SWEEP), lambda b, n, i: (n, 0, 0, 0)),
                  pl.BlockSpec((B_GROUP, n_dw, B_SWEEP, B_SWEEP), lambda b, n, i: (n, 0, 0, 0)),
                  pl.BlockSpec((1, B_GROUP, dh, B_SWEEP), lambda b, n, i: (b, n, 0, i)),
                  pl.BlockSpec((1, 1, B_GATE_ROWS, B_SWEEP), lambda b, n, i: (b, n, 0, i))],
        out_specs=pl.BlockSpec((1, B_SWEEP, B_GROUP * dh), lambda b, n, i: (b, i, n)),
        scratch_shapes=[pltpu.VMEM((vrows, B_GROUP * B_SWEEP), F32),
                        pltpu.VMEM((B_SWEEP, B_GROUP * B_SWEEP), BF16),
                        pltpu.VMEM((B_SWEEP, B_GROUP * B_SWEEP), BF16)],
        compiler_params=_params("parallel", "parallel", "arbitrary"),
        name="b_sparse",
    )(qt, ka, vs, wk, vw, sel, bias_s, bias_w, oc, gate)


def _mixer_b(x, rel_bias, norm1, w_in, q_gain, k_gain, cmp_pos, cmp_w1, cmp_w2, w_out, batch, seq):
    d = w_in.shape[0]
    qkv_width = w_in.shape[1] - 3 * N_HEADS
    gate_w = w_in[:, qkv_width:].reshape(d, B_KV_HEADS, 3 * B_GROUP)
    gate_w = jnp.pad(gate_w, ((0, 0), (0, 0), (0, B_GATE_ROWS - 3 * B_GROUP))).reshape(d, B_KV_HEADS * B_GATE_ROWS)
    w_pad = jnp.concatenate([w_in[:, :qkv_width], gate_w], axis=1)
    w_pad = jnp.pad(w_pad, ((0, 0), (0, B_PROJ_WIDTH - w_pad.shape[1]))).astype(BF16)
    proj = _norm_matmul(x, norm1, w_pad, BF16, 512).reshape(batch, seq, B_PROJ_WIDTH)
    qt, ck, cv, ka, vs, wk, vw, gate_t = _b_prep(proj, q_gain, k_gain, batch, seq)
    kc, vc = _b_compress(ck, cv, cmp_pos, cmp_w1, cmp_w2, k_gain[0], batch, seq)
    last_tile = seq // B_SWEEP - 1
    bias_c = _bias_tiles(rel_bias, 1, seq // B_CMP_STRIDE + last_tile * (B_SWEEP // B_CMP_STRIDE), B_SWEEP,
                         base=1 - B_CMP_LEN + last_tile * B_SWEEP, tile_step=0, row_step=-B_CMP_STRIDE,
                         col_step=1, dmax=1 << 30)
    oc, sel = _b_cmp_attn(qt, kc, vc, bias_c, batch, seq)
    delta_max = min(seq // B_SWEEP - 1, -(-(_THRESHOLDS[-1] + B_SWEEP - 1) // B_SWEEP))
    bias_s = _bias_tiles(rel_bias, delta_max + 2, B_SWEEP, B_SWEEP, base=-B_SWEEP, tile_step=B_SWEEP,
                         row_step=-1, col_step=1, dmax=1 << 30, dtype=BF16)
    win_tiles = (B_WINDOW - 1 + B_SWEEP - 1) // B_SWEEP + 1
    bias_w = _bias_tiles(rel_bias, win_tiles + 1, B_SWEEP, B_SWEEP, base=-B_SWEEP, tile_step=B_SWEEP,
                         row_step=-1, col_step=1, dmax=B_WINDOW - 1, dtype=BF16)
    o = _b_sparse(qt, ka, vs, wk, vw, sel, bias_s, bias_w, oc, gate_t, batch, seq)
    return _matmul_residual(o.reshape(batch * seq, -1), w_out.astype(BF16), x)


def _c_conv_kernel(cur_ref, halo_ref, w_ref, sm_ref, alog_ref, dtb_ref, qkv_ref, bg_ref, xe_ref):
    ts = cur_ref.shape[1]
    keep = jnp.where(pl.program_id(1) == 0, 0.0, 1.0)
    dk = C_HEAD_DIM
    for c in range(3 * C_HEADS):
        sl = slice(c * dk, (c + 1) * dk)
        xe_ref[c, :8, :] = halo_ref[0, :, sl].astype(F32) * keep
        xe_ref[c, 8:, :] = cur_ref[0, :, sl].astype(F32)
        y = None
        for j in range(C_CONV):
            off = 8 - (C_CONV - 1) + j
            term = w_ref[j:j + 1, sl] * xe_ref[c, off:off + ts, :]
            y = term if y is None else y + term
        y = y * jax.nn.sigmoid(y)
        if c < 2 * C_HEADS:
            y = y * lax.rsqrt(jnp.sum(y * y, axis=-1, keepdims=True) + RMS_EPS)
        if c < C_HEADS:
            y = y * (dk ** -0.5)
        qkv_ref[0, :, sl] = y.astype(qkv_ref.dtype)
    sm = sm_ref[0]
    a = sm + dtb_ref[...]
    softplus = jnp.maximum(a, 0.0) + jnp.log1p(jnp.exp(-jnp.abs(a)))
    g = -jnp.exp(alog_ref[...]) * softplus
    lane = lax.broadcasted_iota(jnp.int32, sm.shape, 1)
    bg_ref[0] = jnp.where(lane < C_HEADS, jax.nn.sigmoid(sm), g)


def _c_conv(proj, small, conv_w, a_log, dt_bias, batch, seq):
    ts = 256
    width = 3 * C_WIDTH
    pad = lambda v: jnp.pad(v, (C_HEADS, 128 - 2 * C_HEADS)).reshape(1, 128)
    return pl.pallas_call(
        _c_conv_kernel,
        out_shape=(jax.ShapeDtypeStruct((batch, seq, width), BF16),
                   jax.ShapeDtypeStruct((batch, seq, 128), F32)),
        grid=(batch, seq // ts),
        in_specs=[pl.BlockSpec((1, ts, width), lambda b, i: (b, i, 0)),
                  pl.BlockSpec((1, 8, width), lambda b, i: (b, jnp.maximum(i * (ts // 8) - 1, 0), 0)),
                  pl.BlockSpec((C_CONV, width), lambda b, i: (0, 0)),
                  pl.BlockSpec((1, ts, 128), lambda b, i: (b, i, 0)),
                  pl.BlockSpec((1, 128), lambda b, i: (0, 0)),
                  pl.BlockSpec((1, 128), lambda b, i: (0, 0))],
        out_specs=(pl.BlockSpec((1, ts, width), lambda b, i: (b, i, 0)),
                   pl.BlockSpec((1, ts, 128), lambda b, i: (b, i, 0))),
        scratch_shapes=[pltpu.VMEM((3 * C_HEADS, ts + 8, C_HEAD_DIM), F32)],
        compiler_params=_params("parallel", "arbitrary"),
        name="c_conv",
    )(proj, proj, conv_w, small, pad(a_log), pad(dt_bias))


def _sum3(x, fn):
    hi = x.astype(BF16)
    r = x - hi.astype(F32)
    mid = r.astype(BF16)
    lo = (r - mid.astype(F32)).astype(BF16)
    return fn(hi) + (fn(mid) + fn(lo))


def _c_chunk_kernel(qkv_ref, bg_ref, bgt_ref, tri_ref, trit_ref, blk_ref, u_ref, w_ref, qg_ref, kg_ref, attn_ref,
                    gc_ref):
    cs = C_CHUNK
    dk = C_HEAD_DIM
    gs = C_GROUP * cs
    row = lax.broadcasted_iota(jnp.int32, (gs, gs), 0)
    col = lax.broadcasted_iota(jnp.int32, (gs, gs), 1)
    same = (row // cs) == (col // cs)
    causal = same & (row >= col)
    strict = same & (row > col)
    eye = jnp.where(row == col, 1.0, 0.0)

    bgc = bg_ref[0]
    tri = tri_ref[...]
    gcum_col = _sum3(bgc, lambda p: _dot(tri, p))
    glast_col = _sum3(bgc, lambda p: _dot(blk_ref[...], p))
    gcum_row = _sum3(bgt_ref[0], lambda p: _dot(p, trit_ref[...]))
    gc_ref[0] = gcum_col
    t_mats, powers = [], []
    for h in range(C_HEADS):
        gc = gcum_col[:, C_HEADS + h:C_HEADS + h + 1]
        gr = gcum_row[C_HEADS + h:C_HEADS + h + 1, :]
        q = qkv_ref[0, :, h * dk:(h + 1) * dk]
        k = qkv_ref[0, :, C_WIDTH + h * dk:C_WIDTH + (h + 1) * dk]
        decay = jnp.exp(jnp.where(causal, gc - gr, NEG_INF))
        k16 = k.astype(BF16)
        low = jnp.where(strict, _dot_nt((k * bgc[:, h:h + 1]).astype(BF16), k16) * decay, 0.0)
        t_mats.append(eye - low)
        powers.append(low.astype(BF16))
        attn = jnp.where(causal, _dot_nt(q.astype(BF16), k16), 0.0) * decay
        attn_ref[0, :, h * gs:(h + 1) * gs] = attn.astype(attn_ref.dtype)
        qg_ref[0, :, h * dk:(h + 1) * dk] = (q * jnp.exp(gc)).astype(qg_ref.dtype)
        glast = glast_col[:, C_HEADS + h:C_HEADS + h + 1]
        kg_ref[0, :, h * dk:(h + 1) * dk] = (k * jnp.exp(glast - gc)).astype(kg_ref.dtype)
    for _ in range(int(math.log2(cs)) - 1):
        powers = [_dot(p, p).astype(BF16) for p in powers]
        t_mats = [t + _dot(t.astype(BF16), p) for t, p in zip(t_mats, powers)]
    for h in range(C_HEADS):
        gc = gcum_col[:, C_HEADS + h:C_HEADS + h + 1]
        beta = bgc[:, h:h + 1]
        k = qkv_ref[0, :, C_WIDTH + h * dk:C_WIDTH + (h + 1) * dk]
        v = qkv_ref[0, :, 2 * C_WIDTH + h * dk:2 * C_WIDTH + (h + 1) * dk]
        t16 = t_mats[h].astype(BF16)
        u_ref[0, :, h * dk:(h + 1) * dk] = _dot(t16, (v * beta).astype(BF16)).astype(u_ref.dtype)
        w_ref[0, :, h * dk:(h + 1) * dk] = _dot(t16, (k * beta * jnp.exp(gc)).astype(BF16)).astype(w_ref.dtype)


def _c_chunks(qkv, bg, bgt, batch, seq):
    gs = C_GROUP * C_CHUNK
    idx = np.arange(gs)
    same = (idx[:, None] // C_CHUNK) == (idx[None, :] // C_CHUNK)
    tri = (same & (idx[:, None] >= idx[None, :])).astype(np.float32)
    wide = lambda width: pl.BlockSpec((1, gs, width), lambda b, i: (b, i, 0))
    shape = lambda width, dtype: jax.ShapeDtypeStruct((batch, seq, width), dtype)
    const = pl.BlockSpec((gs, gs), lambda b, i: (0, 0))
    return pl.pallas_call(
        _c_chunk_kernel,
        out_shape=(shape(C_WIDTH, BF16), shape(C_WIDTH, BF16), shape(C_WIDTH, BF16), shape(C_WIDTH, BF16),
                   shape(C_HEADS * gs, BF16), shape(128, F32)),
        grid=(batch, seq // gs),
        in_specs=[wide(3 * C_WIDTH), wide(128),
                  pl.BlockSpec((1, 2 * C_HEADS, gs), lambda b, i: (b, 0, i)),
                  const, const, const],
        out_specs=(wide(C_WIDTH),) * 4 + (wide(C_HEADS * gs), wide(128)),
        compiler_params=_params("parallel", "parallel"),
        name="c_chunks",
    )(qkv, bg, bgt, jnp.asarray(tri, BF16), jnp.asarray(tri.T, BF16), jnp.asarray(same, BF16))


def _c_scan_kernel(u_ref, w_ref, qg_ref, kg_ref, attn_ref, gc_ref, o_ref, state_ref, vnew_ref):
    @pl.when(pl.program_id(1) == 0)
    def _():
        state_ref[...] = jnp.zeros_like(state_ref)

    cs = C_CHUNK
    dk = C_HEAD_DIM
    gs = C_GROUP * cs
    vnew_ref[...] = jnp.zeros_like(vnew_ref)
    heads = range(C_HEADS)
    cols = [slice(h * dk, (h + 1) * dk) for h in heads]
    states = [state_ref[h] for h in heads]
    for c in range(C_GROUP):
        rs = slice(c * cs, (c + 1) * cs)
        decay_last = jnp.exp(gc_ref[0, (c + 1) * cs - 1:(c + 1) * cs, :])
        both = [_dot(jnp.concatenate([w_ref[0, rs, cols[h]], qg_ref[0, rs, cols[h]]], axis=0),
                     states[h].astype(BF16)) for h in heads]
        v16 = [(u_ref[0, rs, cols[h]] - both[h][:cs]).astype(BF16) for h in heads]
        for h in heads:
            vnew_ref[h, rs, :] = v16[h]
        for h in heads:
            out = both[h][cs:] + _dot(attn_ref[0, rs, h * gs:(h + 1) * gs], vnew_ref[h])
            o_ref[0, rs, cols[h]] = out.astype(o_ref.dtype)
        states = [states[h] * decay_last[:, C_HEADS + h:C_HEADS + h + 1] + _dot_tn(kg_ref[0, rs, cols[h]], v16[h])
                  for h in heads]
    for h in heads:
        state_ref[h] = states[h]


def _c_scan(u, w, qg, kg, attn, gc, batch, seq):
    gs = C_GROUP * C_CHUNK
    wide = lambda width: pl.BlockSpec((1, gs, width), lambda b, c: (b, c, 0))
    return pl.pallas_call(
        _c_scan_kernel,
        out_shape=jax.ShapeDtypeStruct((batch, seq, C_WIDTH), BF16),
        grid=(batch, seq // gs),
        in_specs=[wide(C_WIDTH)] * 4 + [wide(C_HEADS * gs), wide(128)],
        out_specs=wide(C_WIDTH),
        scratch_shapes=[pltpu.VMEM((C_HEADS, C_HEAD_DIM, C_HEAD_DIM), F32),
                        pltpu.VMEM((C_HEADS, gs, C_HEAD_DIM), BF16)],
        compiler_params=_params("parallel", "arbitrary"),
        name="c_scan",
    )(u, w, qg, kg, attn, gc)


def _c_out_kernel(o_ref, z_ref, g_ref, w_ref, x_ref, out_ref):
    dk = C_HEAD_DIM
    parts = []
    for h in range(C_HEADS):
        sl = slice(h * dk, (h + 1) * dk)
        z = z_ref[:, sl].astype(F32)
        parts.append((_rms(o_ref[:, sl].astype(F32), g_ref[...]) * (z * jax.nn.sigmoid(z))).astype(BF16))
    out_ref[...] = x_ref[...] + _dot(jnp.concatenate(parts, axis=-1), w_ref[...])


def _c_out(o, proj, out_gain, w_out, x):
    m, d = x.shape
    z_block = (3 * C_WIDTH) // C_WIDTH
    row = lambda width: pl.BlockSpec((ROW_TILE, width), lambda i: (i, 0))
    return pl.pallas_call(
        _c_out_kernel,
        out_shape=jax.ShapeDtypeStruct((m, d), F32),
        grid=(m // ROW_TILE,),
        in_specs=[row(C_WIDTH),
                  pl.BlockSpec((ROW_TILE, C_WIDTH), lambda i: (i, z_block)),
                  pl.BlockSpec((1, C_HEAD_DIM), lambda i: (0, 0)),
                  pl.BlockSpec((C_WIDTH, d), lambda i: (0, 0)),
                  row(d)],
        out_specs=row(d),
        compiler_params=_params("parallel"),
        name="c_out",
    )(o, proj, out_gain.reshape(1, -1), w_out, x)


def _mixer_c(x, norm1, w_in, conv_w, a_log, dt_bias, out_gain, w_out, batch, seq):
    main = 4 * C_WIDTH
    proj = _norm_matmul(x, norm1, w_in[:, :main].astype(BF16), BF16, 512)
    w_small = jnp.pad(w_in[:, main:], ((0, 0), (0, 128 - 2 * C_HEADS))).astype(BF16)
    small = _norm_matmul(x, norm1, w_small, F32, 128)
    qkv, bg = _c_conv(proj.reshape(batch, seq, main), small.reshape(batch, seq, 128), conv_w, a_log, dt_bias,
                      batch, seq)
    bgt = bg[:, :, :2 * C_HEADS].transpose(0, 2, 1)
    u, w, qg, kg, attn, gc = _c_chunks(qkv, bg, bgt, batch, seq)
    o = _c_scan(u, w, qg, kg, attn, gc, batch, seq)
    return _c_out(o.reshape(batch * seq, C_WIDTH), proj, out_gain, w_out.astype(BF16), x)


def kernel(x, rel_bias, l0_norm1, l0_a_w_in, l0_a_q_gain, l0_a_k_gain, l0_a_w_out, l0_norm2, l0_ffn_w_gate, l0_ffn_w_up, l0_ffn_w_down, l1_norm1, l1_b_w_in, l1_b_q_gain, l1_b_k_gain, l1_b_cmp_pos, l1_b_cmp_w1, l1_b_cmp_w2, l1_b_w_out, l1_norm2, l1_ffn_w_gate, l1_ffn_w_up, l1_ffn_w_down, l2_norm1, l2_c_w_in, l2_c_conv_w, l2_c_a_log, l2_c_dt_bias, l2_c_out_gain, l2_c_w_out, l2_norm2, l2_ffn_w_gate, l2_ffn_w_up, l2_ffn_w_down, l3_norm1, l3_a_w_in, l3_a_q_gain, l3_a_k_gain, l3_a_w_out, l3_norm2, l3_ffn_w_gate, l3_ffn_w_up, l3_ffn_w_down):
    batch, seq, d = x.shape
    h = x.reshape(batch * seq, d)

    def ffn(h, norm2, w_gate, w_up, w_down):
        return _ffn(h, norm2, w_gate.astype(BF16), w_up.astype(BF16), w_down.astype(BF16))

    a_biases = _a_biases(rel_bias)
    h = _mixer_a(h, a_biases, l0_norm1, l0_a_w_in, l0_a_q_gain, l0_a_k_gain, l0_a_w_out, batch, seq)
    h = ffn(h, l0_norm2, l0_ffn_w_gate, l0_ffn_w_up, l0_ffn_w_down)
    h = _mixer_b(h, rel_bias, l1_norm1, l1_b_w_in, l1_b_q_gain, l1_b_k_gain, l1_b_cmp_pos, l1_b_cmp_w1,
                 l1_b_cmp_w2, l1_b_w_out, batch, seq)
    h = ffn(h, l1_norm2, l1_ffn_w_gate, l1_ffn_w_up, l1_ffn_w_down)
    h = _mixer_c(h, l2_norm1, l2_c_w_in, l2_c_conv_w, l2_c_a_log, l2_c_dt_bias, l2_c_out_gain, l2_c_w_out,
                 batch, seq)
    h = ffn(h, l2_norm2, l2_ffn_w_gate, l2_ffn_w_up, l2_ffn_w_down)
    h = _mixer_a(h, a_biases, l3_norm1, l3_a_w_in, l3_a_q_gain, l3_a_k_gain, l3_a_w_out, batch, seq)
    h = ffn(h, l3_norm2, l3_ffn_w_gate, l3_ffn_w_up, l3_ffn_w_down)
    return h.reshape(batch, seq, d)
```

```python
import functools
import math

import numpy as np
import jax
import jax.numpy as jnp
from jax import lax
from jax.experimental import pallas as pl
from jax.experimental.pallas import tpu as pltpu

D_MODEL = 1024
RMS_EPS = 1e-6
NEG_INF = -1e30
TINY = 1e-30
FORCE_SCORE = 1e9

N_BUCKETS = 32
REL_MAX_DISTANCE = 2048
N_HEADS = 16

A_GROUPS = ((128, 1), (512, 4), (2048, 16))
A_HEAD_DIM = 64
A_Q_BLOCK = 128
A_PROJ_TILE = 512
A_STAT_WIDTH = 256
A_BLOCKS_PER_STEP = 8

B_KV_HEADS = 4
B_GROUP = 4
B_HEAD_DIM = 64
B_CMP_LEN = 32
B_CMP_STRIDE = 16
B_CMP_HIDDEN = 256
B_SEL_BLOCK = 64
B_TOP_N = 16
B_WINDOW = 512
B_TILE = 128
B_SWEEP = 256
B_PROJ_WIDTH = 3072
B_GATE_ROWS = 16

C_HEADS = 8
C_HEAD_DIM = 128
C_WIDTH = C_HEADS * C_HEAD_DIM
C_CONV = 4
C_CHUNK = 64
C_GROUP = 4

FFN_HIDDEN = 2816
FFN_TILE = 1024

ROW_TILE = 512
VMEM_LIMIT = 48 * 1024 * 1024

LOG2E = math.log2(math.e)

F32 = jnp.float32
BF16 = jnp.bfloat16

NT_DIMS = (((1,), (1,)), ((), ()))
TN_DIMS = (((0,), (0,)), ((), ()))


def _params(*semantics):
    return pltpu.CompilerParams(dimension_semantics=semantics, vmem_limit_bytes=VMEM_LIMIT)


def _dot(a, b):
    return jnp.dot(a, b, preferred_element_type=F32)


def _dot_nt(a, b):
    return lax.dot_general(a, b, NT_DIMS, preferred_element_type=F32)


def _dot_tn(a, b):
    return lax.dot_general(a, b, TN_DIMS, preferred_element_type=F32)


def _rms(x, gain):
    return x * lax.rsqrt(jnp.mean(x * x, axis=-1, keepdims=True) + RMS_EPS) * gain


def _bucket_thresholds():
    d = np.arange(1 << 15)
    max_exact = N_BUCKETS // 2
    d_f = np.maximum(d, 1).astype(np.float32)
    large = max_exact + (np.log(d_f / np.float32(max_exact)) / np.float32(math.log(REL_MAX_DISTANCE / max_exact))
                         * np.float32(N_BUCKETS - max_exact)).astype(np.int32)
    bucket = np.where(d < max_exact, d, np.minimum(large, N_BUCKETS - 1))
    return [int(np.argmax(bucket >= k)) if np.any(bucket >= k) else int(1 << 30) for k in range(N_BUCKETS)]


_THRESHOLDS = _bucket_thresholds()


def _bias_tile_kernel(tbl_ref, o_ref, *, base, tile_step, row_step, col_step, dmax, dil):
    h = pl.program_id(0)
    t = pl.program_id(1)
    shape = o_ref.shape[2:]
    i = lax.broadcasted_iota(jnp.int32, shape, 0)
    j = lax.broadcasted_iota(jnp.int32, shape, 1)
    dist = base + tile_step * t + row_step * i + col_step * j
    d = dist * dil
    val = jnp.full(shape, tbl_ref[0, h], F32)
    for k in range(1, N_BUCKETS):
        val = jnp.where(d >= _THRESHOLDS[k], tbl_ref[k, h], val)
    valid = (dist >= 0) & (dist <= dmax)
    o_ref[0, 0] = jnp.where(valid, val * LOG2E, NEG_INF).astype(o_ref.dtype)


def _bias_tiles(rel_bias, n_tiles, rows, cols, *, base, tile_step, row_step, col_step, dmax, dil=1, dtype=F32):
    kern = functools.partial(_bias_tile_kernel, base=base, tile_step=tile_step, row_step=row_step,
                             col_step=col_step, dmax=dmax, dil=dil)
    return pl.pallas_call(
        kern,
        out_shape=jax.ShapeDtypeStruct((N_HEADS, n_tiles, rows, cols), dtype),
        grid=(N_HEADS, n_tiles),
        in_specs=[pl.BlockSpec(memory_space=pltpu.SMEM)],
        out_specs=pl.BlockSpec((1, 1, rows, cols), lambda h, t: (h, t, 0, 0)),
        compiler_params=_params("parallel", "parallel"),
        name="bias_tiles",
    )(rel_bias)


def _resident(shape):
    return pl.BlockSpec(shape, lambda i: (0,) * len(shape), pipeline_mode=pl.Buffered(1))


def _norm_matmul_kernel(x_ref, g_ref, w_ref, o_ref, *, tn):
    h = _rms(x_ref[...], g_ref[...]).astype(BF16)
    for j in range(w_ref.shape[1] // tn):
        o_ref[:, j * tn:(j + 1) * tn] = _dot(h, w_ref[:, j * tn:(j + 1) * tn]).astype(o_ref.dtype)


def _norm_matmul(x, gain, w, out_dtype, tn):
    m, d = x.shape
    n = w.shape[1]
    return pl.pallas_call(
        functools.partial(_norm_matmul_kernel, tn=tn),
        out_shape=jax.ShapeDtypeStruct((m, n), out_dtype),
        grid=(m // ROW_TILE,),
        in_specs=[pl.BlockSpec((ROW_TILE, d), lambda i: (i, 0)),
                  _resident((1, d)),
                  _resident((d, n))],
        out_specs=pl.BlockSpec((ROW_TILE, n), lambda i: (i, 0)),
        compiler_params=_params("parallel"),
        name="norm_matmul",
    )(x, gain.reshape(1, d), w)


def _matmul_residual_kernel(a_ref, w_ref, x_ref, o_ref):
    o_ref[...] = x_ref[...] + _dot(a_ref[...], w_ref[...])


def _matmul_residual(a, w, x):
    m, k = a.shape
    d = w.shape[1]
    return pl.pallas_call(
        _matmul_residual_kernel,
        out_shape=jax.ShapeDtypeStruct((m, d), F32),
        grid=(m // ROW_TILE,),
        in_specs=[pl.BlockSpec((ROW_TILE, k), lambda i: (i, 0)),
                  pl.BlockSpec((k, d), lambda i: (0, 0)),
                  pl.BlockSpec((ROW_TILE, d), lambda i: (i, 0))],
        out_specs=pl.BlockSpec((ROW_TILE, d), lambda i: (i, 0)),
        compiler_params=_params("parallel"),
        name="matmul_residual",
    )(a, w, x)


def _ffn_kernel(x_ref, g_ref, wg_ref, wu_ref, wd_ref, o_ref):
    x = x_ref[...]
    h = _rms(x, g_ref[...]).astype(BF16)
    hidden = wg_ref.shape[1]
    acc = x
    for lo in range(0, hidden, FFN_TILE):
        hi = min(lo + FFN_TILE, hidden)
        a = _dot(h, wg_ref[:, lo:hi])
        b = _dot(h, wu_ref[:, lo:hi])
        acc = acc + _dot((a * jax.nn.sigmoid(a) * b).astype(BF16), wd_ref[lo:hi, :])
    o_ref[...] = acc


def _ffn(x, gain, w_gate, w_up, w_down):
    m, d = x.shape
    hidden = w_gate.shape[1]
    return pl.pallas_call(
        _ffn_kernel,
        out_shape=jax.ShapeDtypeStruct((m, d), F32),
        grid=(m // ROW_TILE,),
        in_specs=[pl.BlockSpec((ROW_TILE, d), lambda i: (i, 0)),
                  _resident((1, d)),
                  _resident((d, hidden)), _resident((d, hidden)), _resident((hidden, d))],
        out_specs=pl.BlockSpec((ROW_TILE, d), lambda i: (i, 0)),
        compiler_params=_params("parallel"),
        name="ffn",
    )(x, gain.reshape(1, d), w_gate, w_up, w_down)


def _a_proj_kernel(x_ref, g_ref, w_ref, qg_ref, kg_ref, o_ref, h_ref, x_scr, *, dil):
    rows = ROW_TILE // dil
    xn = _rms(x_ref[...], g_ref[...])
    if dil == 1:
        h_ref[...] = xn.astype(BF16)
    else:
        slabs = xn.shape[1] // 128
        for c in range(slabs):
            x_scr[c] = xn[:, c * 128:(c + 1) * 128]
        for r in range(dil):
            picked = [x_scr[c, pl.ds(r, rows, stride=dil), :] for c in range(slabs)]
            h_ref[r * rows:(r + 1) * rows, :] = jnp.concatenate(picked, axis=1).astype(BF16)
    h = h_ref[...]
    width = w_ref.shape[1]
    hd = width // 3
    low = lax.broadcasted_iota(jnp.int32, (ROW_TILE, 128), 1) < A_HEAD_DIM
    for j in range(width // A_PROJ_TILE):
        res = _dot(h, w_ref[:, j * A_PROJ_TILE:(j + 1) * A_PROJ_TILE])
        kind = (j * A_PROJ_TILE) // hd
        if kind < 2:
            parts = []
            for c in range(A_PROJ_TILE // 128):
                y = res[:, c * 128:(c + 1) * 128]
                sq = y * y
                tot = jnp.sum(sq, axis=-1, keepdims=True)
                lo = jnp.sum(jnp.where(low, sq, 0.0), axis=-1, keepdims=True)
                ss = jnp.where(low, lo, tot - lo)
                parts.append(y * lax.rsqrt(ss * (1.0 / A_HEAD_DIM) + RMS_EPS))
            res = jnp.concatenate(parts, axis=1) * (qg_ref if kind == 0 else kg_ref)[...]
        res = res.astype(BF16)
        for r in range(dil):
            off = r * width + j * A_PROJ_TILE
            o_ref[:, off:off + A_PROJ_TILE] = res[r * rows:(r + 1) * rows]


def _a_proj(x, gain, w, q_gain, k_gain, dil):
    m, d = x.shape
    width = w.shape[1]
    reps = A_PROJ_TILE // A_HEAD_DIM
    qg = jnp.tile(q_gain * (A_HEAD_DIM ** -0.5 * LOG2E), reps).reshape(1, A_PROJ_TILE)
    kg = jnp.tile(k_gain, reps).reshape(1, A_PROJ_TILE)
    return pl.pallas_call(
        functools.partial(_a_proj_kernel, dil=dil),
        out_shape=jax.ShapeDtypeStruct((m // dil, dil * width), BF16),
        grid=(m // ROW_TILE,),
        in_specs=[pl.BlockSpec((ROW_TILE, d), lambda i: (i, 0)),
                  pl.BlockSpec((1, d), lambda i: (0, 0)),
                  pl.BlockSpec((d, width), lambda i: (0, 0)),
                  pl.BlockSpec((1, A_PROJ_TILE), lambda i: (0, 0)),
                  pl.BlockSpec((1, A_PROJ_TILE), lambda i: (0, 0))],
        out_specs=pl.BlockSpec((ROW_TILE // dil, dil * width), lambda i: (i, 0)),
        scratch_shapes=[pltpu.VMEM((ROW_TILE, d), BF16), pltpu.VMEM((d // 128, ROW_TILE, 128), F32)],
        compiler_params=_params("parallel"),
        name="a_proj",
    )(x, gain.reshape(1, d), w, qg, kg)


def _a_attn_kernel(q_ref, kp_ref, kc_ref, vp_ref, vc_ref, bias_ref, o_ref, stat_ref):
    first = (pl.program_id(2) == 0).astype(jnp.int32)
    nq = A_Q_BLOCK
    lane = lax.broadcasted_iota(jnp.int32, (nq, 128), 1)
    low = lane < A_HEAD_DIM
    ones = jnp.ones((2 * nq, 128), BF16)
    for sub in range(q_ref.shape[1] // nq):
        rows = slice(sub * nq, (sub + 1) * nq)
        max_tile = jnp.zeros((nq, 128), F32)
        den_tile = jnp.ones((nq, 128), F32)
        for pair in range(N_HEADS // 2):
            sl = slice(pair * 128, (pair + 1) * 128)
            q = q_ref[0, rows, sl]
            zero = jnp.zeros_like(q)
            qq = jnp.concatenate([jnp.where(low, q, zero), jnp.where(low, zero, q)], axis=0)
            if sub == 0:
                kk = jnp.concatenate([kp_ref[0, :, sl], kc_ref[0, :nq, sl]], axis=0)
                vv = jnp.concatenate([vp_ref[0, :, sl], vc_ref[0, :nq, sl]], axis=0)
                base = 2 * pair + N_HEADS * first
            else:
                kk = kc_ref[0, (sub - 1) * nq:(sub + 1) * nq, sl]
                vv = vc_ref[0, (sub - 1) * nq:(sub + 1) * nq, sl]
                base = 2 * pair
            s = _dot_nt(qq, kk) + jnp.concatenate([bias_ref[base], bias_ref[base + 1]], axis=0)
            m = jnp.max(s, axis=-1, keepdims=True)
            acc = _dot(jnp.exp2(s - m).astype(BF16), jnp.concatenate([vv, ones], axis=1))
            o_ref[0, rows, sl] = jnp.where(low, acc[:nq, :128], acc[nq:, :128]).astype(o_ref.dtype)
            first_head = lane == 2 * pair
            second_head = lane == 2 * pair + 1
            max_tile = jnp.where(first_head, m[:nq], jnp.where(second_head, m[nq:], max_tile))
            den_tile = jnp.where(first_head, acc[:nq, 128:], jnp.where(second_head, acc[nq:, 128:], den_tile))
        stat_ref[0, rows, :128] = max_tile
        stat_ref[0, rows, 128:] = den_tile


def _a_attention(proj, bias, dil, batch, seq):
    length = seq // dil
    nblk = length // A_Q_BLOCK
    hd = N_HEADS * A_HEAD_DIM
    pv = proj.reshape(batch, length, dil * 3 * hd)

    per = min(A_BLOCKS_PER_STEP, nblk)
    step_rows = per * A_Q_BLOCK
    assert nblk % per == 0

    def spec(off, prev):
        if prev:
            return pl.BlockSpec((1, A_Q_BLOCK, hd), lambda b, r, i: (b, jnp.maximum(per * i - 1, 0), r * 3 + off))
        return pl.BlockSpec((1, step_rows, hd), lambda b, r, i: (b, i, r * 3 + off))

    o, stats = pl.pallas_call(
        _a_attn_kernel,
        out_shape=(jax.ShapeDtypeStruct((batch, length, dil * hd), BF16),
                   jax.ShapeDtypeStruct((batch, length, dil * A_STAT_WIDTH), F32)),
        grid=(batch, dil, nblk // per),
        in_specs=[spec(0, False), spec(1, True), spec(1, False), spec(2, True), spec(2, False),
                  pl.BlockSpec((2 * N_HEADS, A_Q_BLOCK, 2 * A_Q_BLOCK), lambda b, r, i: (0, 0, 0))],
        out_specs=(pl.BlockSpec((1, step_rows, hd), lambda b, r, i: (b, i, r)),
                   pl.BlockSpec((1, step_rows, A_STAT_WIDTH), lambda b, r, i: (b, i, r))),
        compiler_params=_params("parallel", "parallel", "arbitrary"),
        name="a_attention",
    )(pv, pv, pv, pv, pv, bias)
    return o.reshape(batch * length, dil * hd), stats.reshape(batch * length, dil * A_STAT_WIDTH)


def _a_out_kernel(o0_ref, o1_ref, o2_ref, s0_ref, s1_ref, s2_ref, e_ref, w_ref, x_ref, out_ref, o_scr, s_scr):
    hd = N_HEADS * A_HEAD_DIM
    sw = A_STAT_WIDTH
    for g, (o_ref, s_ref) in enumerate(((o0_ref, s0_ref), (o1_ref, s1_ref), (o2_ref, s2_ref))):
        dil = A_GROUPS[g][1]
        rows = ROW_TILE // dil
        for r in range(dil):
            dst = pl.ds(r, rows, stride=dil) if dil > 1 else slice(None)
            s_scr[g, 0, dst, :] = s_ref[:, r * sw:r * sw + 128]
            s_scr[g, 1, dst, :] = s_ref[:, r * sw + 128:(r + 1) * sw]
            for c in range(hd // 128):
                o_scr[g, c, dst, :] = o_ref[:, r * hd + c * 128:r * hd + (c + 1) * 128].astype(F32)
    groups = range(len(A_GROUPS))
    top = functools.reduce(jnp.maximum, [s_scr[g, 0] for g in groups])
    es = [jnp.exp2(s_scr[g, 0] - top) for g in groups]
    inv = 1.0 / sum(es[g] * s_scr[g, 1] for g in groups)
    expand = e_ref[...]
    acc = None
    for g in groups:
        o_g = jnp.concatenate([o_scr[g, c] for c in range(hd // 128)], axis=1)
        term = _dot((es[g] * inv).astype(BF16), expand) * o_g
        acc = term if acc is None else acc + term
    out_ref[...] = x_ref[...] + _dot(acc.astype(BF16), w_ref[...])


def _a_out(outs, stats, w_out, x):
    m, d = x.shape
    hd = N_HEADS * A_HEAD_DIM
    expand = np.zeros((128, hd), np.float32)
    for h in range(N_HEADS):
        expand[h, h * A_HEAD_DIM:(h + 1) * A_HEAD_DIM] = 1.0
    grouped = lambda width: [pl.BlockSpec((ROW_TILE // dil, dil * width), lambda i: (i, 0)) for _, dil in A_GROUPS]
    return pl.pallas_call(
        _a_out_kernel,
        out_shape=jax.ShapeDtypeStruct((m, d), F32),
        grid=(m // ROW_TILE,),
        in_specs=grouped(hd) + grouped(A_STAT_WIDTH) + [
            pl.BlockSpec((128, hd), lambda i: (0, 0)),
            pl.BlockSpec((hd, d), lambda i: (0, 0)),
            pl.BlockSpec((ROW_TILE, d), lambda i: (i, 0))],
        out_specs=pl.BlockSpec((ROW_TILE, d), lambda i: (i, 0)),
        scratch_shapes=[pltpu.VMEM((len(A_GROUPS), hd // 128, ROW_TILE, 128), F32),
                        pltpu.VMEM((len(A_GROUPS), 2, ROW_TILE, 128), F32)],
        compiler_params=_params("parallel"),
        name="a_out",
    )(*outs, *stats, jnp.asarray(expand, BF16), w_out, x)


def _a_biases(rel_bias):
    biases = []
    for window, dil in A_GROUPS:
        steps = window // dil
        assert steps == A_Q_BLOCK
        bias = _bias_tiles(rel_bias, 1, A_Q_BLOCK, 2 * A_Q_BLOCK, base=A_Q_BLOCK, tile_step=0, row_step=1,
                           col_step=-1, dmax=steps, dil=dil)[:, 0]
        biases.append(jnp.concatenate([bias, bias.at[:, :, :A_Q_BLOCK].set(NEG_INF)], axis=0))
    return biases


def _mixer_a(x, biases, norm1, w_in, q_gain, k_gain, w_out, batch, seq):
    w_in = w_in.astype(BF16)
    group_width = 3 * N_HEADS * A_HEAD_DIM
    outs, stats = [], []
    for gi, (_, dil) in enumerate(A_GROUPS):
        assert (seq // dil) % A_Q_BLOCK == 0 and seq % ROW_TILE == 0
        bias = biases[gi]
        proj = _a_proj(x, norm1, w_in[:, gi * group_width:(gi + 1) * group_width], q_gain[gi], k_gain[gi], dil)
        o, stat = _a_attention(proj, bias, dil, batch, seq)
        outs.append(o)
        stats.append(stat)
    return _a_out(outs, stats, w_out.astype(BF16), x)


def _b_prep_kernel(p_ref, qg_ref, kg_ref, qt_ref, ck_ref, cv_ref, ka_ref, vs_ref, wk_ref, vw_ref, gate_ref):
    dh = B_HEAD_DIM
    ts = p_ref.shape[1]
    n_sel = ka_ref.shape[3] - dh
    low = lax.broadcasted_iota(jnp.int32, (ts, 128), 1) < dh

    def slab(c):
        return p_ref[0, :, c * 128:(c + 1) * 128].astype(F32)

    def normed(x, gain):
        sq = x * x
        tot = jnp.sum(sq, axis=-1, keepdims=True)
        first = jnp.sum(jnp.where(low, sq, 0.0), axis=-1, keepdims=True)
        ss = jnp.where(low, first, tot - first)
        return x * lax.rsqrt(ss * (1.0 / dh) + RMS_EPS) * jnp.concatenate([gain, gain], axis=1)

    def halves(x):
        return x[:, :dh], x[:, dh:]

    qg = qg_ref[...] * (dh ** -0.5 * LOG2E)
    for c in range(N_HEADS // 2):
        xt = normed(slab(c), qg).T.astype(BF16)
        qt_ref[0, 2 * c] = xt[:dh]
        qt_ref[0, 2 * c + 1] = xt[dh:]
    base = N_HEADS // 2
    pairs = B_KV_HEADS // 2
    pos = pl.program_id(1) * ts + lax.broadcasted_iota(jnp.int32, (ts, 128), 0)
    lane = lax.broadcasted_iota(jnp.int32, (ts, 128), 1)
    onehot = jnp.where(lane - dh == pos // B_SEL_BLOCK, 1.0, 0.0)
    ones = jnp.ones((vs_ref.shape[2] - dh, ts), BF16)
    for j in range(pairs):
        for ref, off in ((ck_ref, 0), (cv_ref, pairs)):
            a, b = halves(slab(base + off + j))
            ref[0, 2 * j] = a.astype(BF16)
            ref[0, 2 * j + 1] = b.astype(BF16)
        k_sel = normed(slab(base + 2 * pairs + j), kg_ref[1:2, :])
        for n, k in zip((2 * j, 2 * j + 1), (k_sel, pltpu.roll(k_sel, dh, 1))):
            ka_ref[0, n] = jnp.where(low, k, onehot)[:, :dh + n_sel].astype(BF16)
        for n, k in zip((2 * j, 2 * j + 1), halves(normed(slab(base + 4 * pairs + j), kg_ref[2:3, :]))):
            wk_ref[0, n] = k.astype(BF16)
        for ref, off in ((vs_ref, 3 * pairs), (vw_ref, 5 * pairs)):
            xt = slab(base + off + j).T.astype(BF16)
            for n, v in zip((2 * j, 2 * j + 1), (xt[:dh], xt[dh:])):
                ref[0, n, :dh, :] = v
                ref[0, n, dh:, :] = ones
    gate = jax.nn.sigmoid(slab(base + 6 * pairs)).T
    rows = gate_ref.shape[2]
    for n in range(B_KV_HEADS):
        gate_ref[0, n] = gate[n * rows:(n + 1) * rows]


def _b_prep(proj, q_gain, k_gain, batch, seq):
    ts = 256
    dh = B_HEAD_DIM
    n_sel = seq // B_SEL_BLOCK
    assert dh + n_sel <= 128
    rows_shape = lambda width: jax.ShapeDtypeStruct((batch, B_KV_HEADS, seq, width), BF16)
    rows_spec = lambda width: pl.BlockSpec((1, B_KV_HEADS, ts, width), lambda b, i: (b, 0, i, 0))
    cols_shape = lambda heads, height, dtype: jax.ShapeDtypeStruct((batch, heads, height, seq), dtype)
    cols_spec = lambda heads, height: pl.BlockSpec((1, heads, height, ts), lambda b, i: (b, 0, 0, i))
    return pl.pallas_call(
        _b_prep_kernel,
        out_shape=(cols_shape(N_HEADS, dh, BF16), rows_shape(dh), rows_shape(dh), rows_shape(dh + n_sel),
                   cols_shape(B_KV_HEADS, dh + 16, BF16), rows_shape(dh), cols_shape(B_KV_HEADS, dh + 16, BF16),
                   cols_shape(B_KV_HEADS, B_GATE_ROWS, F32)),
        grid=(batch, seq // ts),
        in_specs=[pl.BlockSpec((1, ts, B_PROJ_WIDTH), lambda b, i: (b, i, 0)),
                  pl.BlockSpec((1, dh), lambda b, i: (0, 0)),
                  pl.BlockSpec((3, dh), lambda b, i: (0, 0))],
        out_specs=(cols_spec(N_HEADS, dh), rows_spec(dh), rows_spec(dh), rows_spec(dh + n_sel),
                   cols_spec(B_KV_HEADS, dh + 16), rows_spec(dh), cols_spec(B_KV_HEADS, dh + 16),
                   cols_spec(B_KV_HEADS, B_GATE_ROWS)),
        compiler_params=_params("parallel", "parallel"),
        name="b_prep",
    )(proj, q_gain.reshape(1, dh), k_gain)


def _b_compress_kernel(tk_ref, tv_ref, pos_ref, w1_ref, w2_ref, kg_ref, kc_ref, vc_ref):
    half = (B_CMP_LEN // 2) * B_HEAD_DIM
    for kv, (t_ref, out_ref) in enumerate(((tk_ref, kc_ref), (tv_ref, vc_ref))):
        t = t_ref[0, 0].astype(F32)
        top = (t + pos_ref[kv, 0:1, :]).astype(BF16)
        bot = (t + pos_ref[kv, 1:2, :]).astype(BF16)
        a1 = _dot(top, w1_ref[kv, :half, :])
        a2 = _dot(bot, w1_ref[kv, half:, :])
        hidden = a1 + pltpu.roll(a2, a2.shape[0] - 1, 0)
        out = _dot(jax.nn.gelu(hidden).astype(BF16), w2_ref[kv])
        if kv == 0:
            out = _rms(out, kg_ref[...])
        out_ref[0, 0] = out.astype(out_ref.dtype)


def _b_compress(ck, cv, cmp_pos, cmp_w1, cmp_w2, k_gain0, batch, seq):
    rows = seq // B_CMP_STRIDE
    half = (B_CMP_LEN // 2) * B_HEAD_DIM
    tk = ck.reshape(batch, B_KV_HEADS, rows, half)
    tv = cv.reshape(batch, B_KV_HEADS, rows, half)
    pos = cmp_pos.reshape(2, 2, half)
    t_spec = pl.BlockSpec((1, 1, rows, half), lambda b, n: (b, n, 0, 0))
    o_spec = pl.BlockSpec((1, 1, rows, B_HEAD_DIM), lambda b, n: (b, n, 0, 0))
    shape = jax.ShapeDtypeStruct((batch, B_KV_HEADS, rows, B_HEAD_DIM), BF16)
    return pl.pallas_call(
        _b_compress_kernel,
        out_shape=(shape, shape),
        grid=(batch, B_KV_HEADS),
        in_specs=[t_spec, t_spec,
                  pl.BlockSpec((2, 2, half), lambda b, n: (0, 0, 0)),
                  pl.BlockSpec((2, 2 * half, B_CMP_HIDDEN), lambda b, n: (0, 0, 0)),
                  pl.BlockSpec((2, B_CMP_HIDDEN, B_HEAD_DIM), lambda b, n: (0, 0, 0)),
                  pl.BlockSpec((1, B_HEAD_DIM), lambda b, n: (0, 0))],
        out_specs=(o_spec, o_spec),
        compiler_params=_params("parallel", "parallel"),
        name="b_compress",
    )(tk, tv, pos, cmp_w1.astype(BF16), cmp_w2.astype(BF16), k_gain0.reshape(1, -1))


def _b_cmp_attn_kernel(qt_ref, kc_ref, vct_ref, bias_ref, c2s_ref, oc_ref, sel_ref, imp_ref, *, top_n):
    tq = B_SWEEP
    n_sel = imp_ref.shape[0]
    qt = jnp.concatenate([qt_ref[0, g] for g in range(B_GROUP)], axis=1)
    n_cmp_pad = kc_ref.shape[2]
    shift = tq // B_CMP_STRIDE
    off = pl.multiple_of((pl.num_programs(2) - 1 - pl.program_id(2)) * shift, shift)
    bias = jnp.concatenate([bias_ref[g, 0, pl.ds(off, n_cmp_pad), :] for g in range(B_GROUP)], axis=1)
    s = _dot(kc_ref[0, 0], qt) + bias
    m = jnp.max(s, axis=0, keepdims=True)
    e = jnp.exp2(s - m)
    z = jnp.maximum(jnp.sum(e, axis=0, keepdims=True), TINY)
    pos = pl.program_id(2) * tq + lax.broadcasted_iota(jnp.int32, (1, tq), 1)
    sees_any = jnp.concatenate([pos >= B_CMP_LEN - 1] * B_GROUP, axis=1)
    p = e * jnp.where(sees_any, 1.0 / z, 0.0)
    oct = _dot(vct_ref[0, 0], p.astype(BF16))
    for g in range(B_GROUP):
        oc_ref[0, g] = oct[:, g * tq:(g + 1) * tq]

    p_sum = p[:, 0:tq] + p[:, tq:2 * tq] + p[:, 2 * tq:3 * tq] + p[:, 3 * tq:4 * tq]
    hi = p_sum.astype(BF16)
    lo = (p_sum - hi.astype(F32)).astype(BF16)
    c2s = c2s_ref[...]
    imp = _dot(c2s, hi) + _dot(c2s, lo)

    t = pl.program_id(2) * tq + lax.broadcasted_iota(jnp.int32, (n_sel, tq), 1)
    blk = lax.broadcasted_iota(jnp.int32, (n_sel, tq), 0)
    cur = t // B_SEL_BLOCK
    forced = (blk == 0) | (blk == cur) | (blk == cur - 1)
    imp = jnp.where(forced, FORCE_SCORE, jnp.where(blk * B_SEL_BLOCK <= t, imp, NEG_INF))
    imp_ref[...] = imp

    def count(i, rank):
        row = imp_ref[pl.ds(i, 1), :]
        ahead = jnp.where(row > imp, 1.0, jnp.where(row == imp, jnp.where(blk > i, 1.0, 0.0), 0.0))
        return rank + ahead

    n_live = jnp.minimum(n_sel, (pl.program_id(2) + 1) * (tq // B_SEL_BLOCK))
    rank = lax.fori_loop(0, n_live, count, jnp.zeros((n_sel, tq), F32))
    sel_ref[0, 0] = jnp.where(rank < top_n, 0.0, NEG_INF).astype(sel_ref.dtype)


def _b_cmp_attn(qt, kc, vc, bias_c, batch, seq):
    n_sel = seq // B_SEL_BLOCK
    n_cmp_pad = seq // B_CMP_STRIDE
    n_cmp = (seq - B_CMP_LEN) // B_CMP_STRIDE + 1
    c = np.arange(n_cmp_pad)[None, :] * B_CMP_STRIDE
    j = np.arange(n_sel)[:, None] * B_SEL_BLOCK
    c2s = ((c < j + B_SEL_BLOCK) & (c + B_CMP_LEN > j) & (np.arange(n_cmp_pad)[None, :] < n_cmp)).astype(np.float32)
    kern = functools.partial(_b_cmp_attn_kernel, top_n=min(B_TOP_N, n_sel))
    return pl.pallas_call(
        kern,
        out_shape=(jax.ShapeDtypeStruct((batch, N_HEADS, B_HEAD_DIM, seq), F32),
                   jax.ShapeDtypeStruct((batch, B_KV_HEADS, n_sel, seq), BF16)),
        grid=(batch, B_KV_HEADS, seq // B_SWEEP),
        in_specs=[pl.BlockSpec((1, B_GROUP, B_HEAD_DIM, B_SWEEP), lambda b, n, i: (b, n, 0, i)),
                  pl.BlockSpec((1, 1, n_cmp_pad, B_HEAD_DIM), lambda b, n, i: (b, n, 0, 0)),
                  pl.BlockSpec((1, 1, B_HEAD_DIM, n_cmp_pad), lambda b, n, i: (b, n, 0, 0)),
                  pl.BlockSpec((B_GROUP, 1, bias_c.shape[2], B_SWEEP), lambda b, n, i: (n, 0, 0, 0)),
                  pl.BlockSpec((n_sel, n_cmp_pad), lambda b, n, i: (0, 0))],
        out_specs=(pl.BlockSpec((1, B_GROUP, B_HEAD_DIM, B_SWEEP), lambda b, n, i: (b, n, 0, i)),
                   pl.BlockSpec((1, 1, n_sel, B_SWEEP), lambda b, n, i: (b, n, 0, i))),
        scratch_shapes=[pltpu.VMEM((n_sel, B_SWEEP), F32)],
        compiler_params=_params("parallel", "parallel", "arbitrary"),
        name="b_cmp_attn",
    )(qt, kc, vc.transpose(0, 1, 3, 2), bias_c, jnp.asarray(c2s, BF16))


def _b_sparse_kernel(qt_ref, ka_ref, vs_ref, wk_ref, vw_ref, sel_ref, bs_ref, bw_ref, oc_ref, gate_ref,
                     o_ref, acc_ref, sa_ref, sb_ref, *, delta_max, win_tiles):
    tq = B_SWEEP
    dh = B_HEAD_DIM
    cols = B_GROUP * tq
    qi = pl.program_id(2)
    n_tiles = ka_ref.shape[2] // tq
    qt = jnp.concatenate([qt_ref[0, g] for g in range(B_GROUP)], axis=1)
    q_aug = jnp.concatenate([qt, jnp.concatenate([sel_ref[0, 0]] * B_GROUP, axis=1)], axis=0)

    def tile_start(kt):
        return pl.multiple_of(jnp.clip(kt, 0, n_tiles - 1) * tq, tq)

    def normalised(acc):
        return acc[:dh] * (1.0 / acc[dh:dh + 1])

    def sel_bias(kt):
        d = jnp.clip(qi - kt, -1, delta_max) + 1
        return jnp.concatenate([bs_ref[g, d] for g in range(B_GROUP)], axis=1)

    def sel_scores(kt):
        return _dot(ka_ref[0, 0, pl.ds(tile_start(kt), tq), :], q_aug).astype(BF16) + sel_bias(kt)

    def consume(s_buf, kt, m_old):
        s = s_buf[...]
        m_new = jnp.maximum(m_old, jnp.max(s, axis=0, keepdims=True).astype(F32))
        alpha = jnp.exp2(m_old - m_new)
        p = jnp.exp2(s - m_new.astype(BF16))
        acc_ref[...] = alpha * acc_ref[...] + _dot(vs_ref[0, 0, :, pl.ds(tile_start(kt), tq)], p)
        return m_new

    acc_ref[...] = jnp.zeros(acc_ref.shape, F32)
    sa_ref[...] = sel_scores(0)

    def pair(kt, m):
        sb_ref[...] = sel_scores(kt + 1)
        m = consume(sa_ref, kt, m)
        sa_ref[...] = sel_scores(kt + 2)
        return consume(sb_ref, kt + 1, m)

    n_pairs = (qi + 2) // 2
    n_double = n_pairs // 2
    m = lax.fori_loop(0, n_double, lambda j, m: pair(4 * j + 2, pair(4 * j, m)),
                      jnp.full((1, cols), NEG_INF, F32))
    lax.fori_loop(2 * n_double, n_pairs, lambda j, m: pair(2 * j, m), m)
    o_s = normalised(acc_ref[...])

    tiles = []
    for u in range(win_tiles):
        kt = qi - (win_tiles - 1) + u
        d = jnp.where(kt >= 0, qi - kt, -1) + 1
        bias = jnp.concatenate([bw_ref[g, d] for g in range(B_GROUP)], axis=1)
        tiles.append((_dot(wk_ref[0, 0, pl.ds(tile_start(kt), tq), :], qt).astype(BF16) + bias, kt))
    m = None
    for s, _ in tiles:
        tile_max = jnp.max(s, axis=0, keepdims=True)
        m = tile_max if m is None else jnp.maximum(m, tile_max)
    acc = None
    for s, kt in tiles:
        pv = _dot(vw_ref[0, 0, :, pl.ds(tile_start(kt), tq)], jnp.exp2(s - m))
        acc = pv if acc is None else acc + pv
    o_w = normalised(acc)

    gate = gate_ref[0, 0]
    merged = []
    for g in range(B_GROUP):
        cs = slice(g * tq, (g + 1) * tq)
        merged.append(gate[3 * g:3 * g + 1] * oc_ref[0, g] + gate[3 * g + 1:3 * g + 2] * o_s[:, cs]
                      + gate[3 * g + 2:3 * g + 3] * o_w[:, cs])
    for pair in range(B_GROUP // 2):
        both = jnp.concatenate([merged[2 * pair], merged[2 * pair + 1]], axis=0)
        o_ref[0, :, pair * 2 * dh:(pair + 1) * 2 * dh] = both.T.astype(o_ref.dtype)


def _b_sparse(qt, ka, vs, wk, vw, sel, bias_s, bias_w, oc, gate, batch, seq):
    n_sel = seq // B_SEL_BLOCK
    dh = B_HEAD_DIM
    n_ds = bias_s.shape[1]
    n_dw = bias_w.shape[1]
    vrows = vs.shape[2]
    kern = functools.partial(_b_sparse_kernel, delta_max=n_ds - 2, win_tiles=n_dw - 1)
    whole = lambda rows, width: pl.BlockSpec((1, 1, rows, width), lambda b, n, i: (b, n, 0, 0))
    return pl.pallas_call(
        kern,
        out_shape=jax.ShapeDtypeStruct((batch, seq, N_HEADS * dh), BF16),
        grid=(batch, B_KV_HEADS, seq // B_SWEEP),
        in_specs=[pl.BlockSpec((1, B_GROUP, dh, B_SWEEP), lambda b, n, i: (b, n, 0, i)),
                  whole(seq, dh + n_sel), whole(vrows, seq), whole(seq, dh), whole(vrows, seq),
                  pl.BlockSpec((1, 1, n_sel, B_SWEEP), lambda b, n, i: (b, n, 0, i)),
                  pl.BlockSpec((B_GROUP, n_ds, B_SWEEP, B_SWEEP), lambda b, n, i: (n, 0, 0, 0)),
                  pl.BlockSpec((B_GROUP, n_dw, B_SWEEP, B_SWEEP), lambda b, n, i: (n, 0, 0, 0)),
                  pl.BlockSpec((1, B_GROUP, dh, B_SWEEP), lambda b, n, i: (b, n, 0, i)),
                  pl.BlockSpec((1, 1, B_GATE_ROWS, B_SWEEP), lambda b, n, i: (b, n, 0, i))],
        out_specs=pl.BlockSpec((1, B_SWEEP, B_GROUP * dh), lambda b, n, i: (b, i, n)),
        scratch_shapes=[pltpu.VMEM((vrows, B_GROUP * B_SWEEP), F32),
                        pltpu.VMEM((B_SWEEP, B_GROUP * B_SWEEP), BF16),
                        pltpu.VMEM((B_SWEEP, B_GROUP * B_SWEEP), BF16)],
        compiler_params=_params("parallel", "parallel", "arbitrary"),
        name="b_sparse",
    )(qt, ka, vs, wk, vw, sel, bias_s, bias_w, oc, gate)


def _mixer_b(x, rel_bias, norm1, w_in, q_gain, k_gain, cmp_pos, cmp_w1, cmp_w2, w_out, batch, seq):
    d = w_in.shape[0]
    qkv_width = w_in.shape[1] - 3 * N_HEADS
    gate_w = w_in[:, qkv_width:].reshape(d, B_KV_HEADS, 3 * B_GROUP)
    gate_w = jnp.pad(gate_w, ((0, 0), (0, 0), (0, B_GATE_ROWS - 3 * B_GROUP))).reshape(d, B_KV_HEADS * B_GATE_ROWS)
    w_pad = jnp.concatenate([w_in[:, :qkv_width], gate_w], axis=1)
    w_pad = jnp.pad(w_pad, ((0, 0), (0, B_PROJ_WIDTH - w_pad.shape[1]))).astype(BF16)
    proj = _norm_matmul(x, norm1, w_pad, BF16, 512).reshape(batch, seq, B_PROJ_WIDTH)
    qt, ck, cv, ka, vs, wk, vw, gate_t = _b_prep(proj, q_gain, k_gain, batch, seq)
    kc, vc = _b_compress(ck, cv, cmp_pos, cmp_w1, cmp_w2, k_gain[0], batch, seq)
    last_tile = seq // B_SWEEP - 1
    bias_c = _bias_tiles(rel_bias, 1, seq // B_CMP_STRIDE + last_tile * (B_SWEEP // B_CMP_STRIDE), B_SWEEP,
                         base=1 - B_CMP_LEN + last_tile * B_SWEEP, tile_step=0, row_step=-B_CMP_STRIDE,
                         col_step=1, dmax=1 << 30)
    oc, sel = _b_cmp_attn(qt, kc, vc, bias_c, batch, seq)
    delta_max = min(seq // B_SWEEP - 1, -(-(_THRESHOLDS[-1] + B_SWEEP - 1) // B_SWEEP))
    bias_s = _bias_tiles(rel_bias, delta_max + 2, B_SWEEP, B_SWEEP, base=-B_SWEEP, tile_step=B_SWEEP,
                         row_step=-1, col_step=1, dmax=1 << 30, dtype=BF16)
    win_tiles = (B_WINDOW - 1 + B_SWEEP - 1) // B_SWEEP + 1
    bias_w = _bias_tiles(rel_bias, win_tiles + 1, B_SWEEP, B_SWEEP, base=-B_SWEEP, tile_step=B_SWEEP,
                         row_step=-1, col_step=1, dmax=B_WINDOW - 1, dtype=BF16)
    o = _b_sparse(qt, ka, vs, wk, vw, sel, bias_s, bias_w, oc, gate_t, batch, seq)
    return _matmul_residual(o.reshape(batch * seq, -1), w_out.astype(BF16), x)


def _c_conv_kernel(cur_ref, halo_ref, w_ref, sm_ref, alog_ref, dtb_ref, qkv_ref, bg_ref, xe_ref):
    ts = cur_ref.shape[1]
    keep = jnp.where(pl.program_id(1) == 0, 0.0, 1.0)
    dk = C_HEAD_DIM
    for c in range(3 * C_HEADS):
        sl = slice(c * dk, (c + 1) * dk)
        xe_ref[c, :8, :] = halo_ref[0, :, sl].astype(F32) * keep
        xe_ref[c, 8:, :] = cur_ref[0, :, sl].astype(F32)
        y = None
        for j in range(C_CONV):
            off = 8 - (C_CONV - 1) + j
            term = w_ref[j:j + 1, sl] * xe_ref[c, off:off + ts, :]
            y = term if y is None else y + term
        y = y * jax.nn.sigmoid(y)
        if c < 2 * C_HEADS:
            y = y * lax.rsqrt(jnp.sum(y * y, axis=-1, keepdims=True) + RMS_EPS)
        if c < C_HEADS:
            y = y * (dk ** -0.5)
        qkv_ref[0, :, sl] = y.astype(qkv_ref.dtype)
    sm = sm_ref[0]
    a = sm + dtb_ref[...]
    softplus = jnp.maximum(a, 0.0) + jnp.log1p(jnp.exp(-jnp.abs(a)))
    g = -jnp.exp(alog_ref[...]) * softplus
    lane = lax.broadcasted_iota(jnp.int32, sm.shape, 1)
    bg_ref[0] = jnp.where(lane < C_HEADS, jax.nn.sigmoid(sm), g)


def _c_conv(proj, small, conv_w, a_log, dt_bias, batch, seq):
    ts = 256
    width = 3 * C_WIDTH
    pad = lambda v: jnp.pad(v, (C_HEADS, 128 - 2 * C_HEADS)).reshape(1, 128)
    return pl.pallas_call(
        _c_conv_kernel,
        out_shape=(jax.ShapeDtypeStruct((batch, seq, width), BF16),
                   jax.ShapeDtypeStruct((batch, seq, 128), F32)),
        grid=(batch, seq // ts),
        in_specs=[pl.BlockSpec((1, ts, width), lambda b, i: (b, i, 0)),
                  pl.BlockSpec((1, 8, width), lambda b, i: (b, jnp.maximum(i * (ts // 8) - 1, 0), 0)),
                  pl.BlockSpec((C_CONV, width), lambda b, i: (0, 0)),
                  pl.BlockSpec((1, ts, 128), lambda b, i: (b, i, 0)),
                  pl.BlockSpec((1, 128), lambda b, i: (0, 0)),
                  pl.BlockSpec((1, 128), lambda b, i: (0, 0))],
        out_specs=(pl.BlockSpec((1, ts, width), lambda b, i: (b, i, 0)),
                   pl.BlockSpec((1, ts, 128), lambda b, i: (b, i, 0))),
        scratch_shapes=[pltpu.VMEM((3 * C_HEADS, ts + 8, C_HEAD_DIM), F32)],
        compiler_params=_params("parallel", "arbitrary"),
        name="c_conv",
    )(proj, proj, conv_w, small, pad(a_log), pad(dt_bias))


def _sum3(x, fn):
    hi = x.astype(BF16)
    r = x - hi.astype(F32)
    mid = r.astype(BF16)
    lo = (r - mid.astype(F32)).astype(BF16)
    return fn(hi) + (fn(mid) + fn(lo))


def _c_chunk_kernel(qkv_ref, bg_ref, bgt_ref, tri_ref, trit_ref, blk_ref, u_ref, w_ref, qg_ref, kg_ref, attn_ref,
                    gc_ref):
    cs = C_CHUNK
    dk = C_HEAD_DIM
    gs = C_GROUP * cs
    row = lax.broadcasted_iota(jnp.int32, (gs, gs), 0)
    col = lax.broadcasted_iota(jnp.int32, (gs, gs), 1)
    same = (row // cs) == (col // cs)
    causal = same & (row >= col)
    strict = same & (row > col)
    eye = jnp.where(row == col, 1.0, 0.0)

    bgc = bg_ref[0]
    tri = tri_ref[...]
    gcum_col = _sum3(bgc, lambda p: _dot(tri, p))
    glast_col = _sum3(bgc, lambda p: _dot(blk_ref[...], p))
    gcum_row = _sum3(bgt_ref[0], lambda p: _dot(p, trit_ref[...]))
    gc_ref[0] = gcum_col
    t_mats, powers = [], []
    for h in range(C_HEADS):
        gc = gcum_col[:, C_HEADS + h:C_HEADS + h + 1]
        gr = gcum_row[C_HEADS + h:C_HEADS + h + 1, :]
        q = qkv_ref[0, :, h * dk:(h + 1) * dk]
        k = qkv_ref[0, :, C_WIDTH + h * dk:C_WIDTH + (h + 1) * dk]
        decay = jnp.exp(jnp.where(causal, gc - gr, NEG_INF))
        k16 = k.astype(BF16)
        low = jnp.where(strict, _dot_nt((k * bgc[:, h:h + 1]).astype(BF16), k16) * decay, 0.0)
        t_mats.append(eye - low)
        powers.append(low.astype(BF16))
        attn = jnp.where(causal, _dot_nt(q.astype(BF16), k16), 0.0) * decay
        attn_ref[0, :, h * gs:(h + 1) * gs] = attn.astype(attn_ref.dtype)
        qg_ref[0, :, h * dk:(h + 1) * dk] = (q * jnp.exp(gc)).astype(qg_ref.dtype)
        glast = glast_col[:, C_HEADS + h:C_HEADS + h + 1]
        kg_ref[0, :, h * dk:(h + 1) * dk] = (k * jnp.exp(glast - gc)).astype(kg_ref.dtype)
    for _ in range(int(math.log2(cs)) - 1):
        powers = [_dot(p, p).astype(BF16) for p in powers]
        t_mats = [t + _dot(t.astype(BF16), p) for t, p in zip(t_mats, powers)]
    for h in range(C_HEADS):
        gc = gcum_col[:, C_HEADS + h:C_HEADS + h + 1]
        beta = bgc[:, h:h + 1]
        k = qkv_ref[0, :, C_WIDTH + h * dk:C_WIDTH + (h + 1) * dk]
        v = qkv_ref[0, :, 2 * C_WIDTH + h * dk:2 * C_WIDTH + (h + 1) * dk]
        t16 = t_mats[h].astype(BF16)
        u_ref[0, :, h * dk:(h + 1) * dk] = _dot(t16, (v * beta).astype(BF16)).astype(u_ref.dtype)
        w_ref[0, :, h * dk:(h + 1) * dk] = _dot(t16, (k * beta * jnp.exp(gc)).astype(BF16)).astype(w_ref.dtype)


def _c_chunks(qkv, bg, bgt, batch, seq):
    gs = C_GROUP * C_CHUNK
    idx = np.arange(gs)
    same = (idx[:, None] // C_CHUNK) == (idx[None, :] // C_CHUNK)
    tri = (same & (idx[:, None] >= idx[None, :])).astype(np.float32)
    wide = lambda width: pl.BlockSpec((1, gs, width), lambda b, i: (b, i, 0))
    shape = lambda width, dtype: jax.ShapeDtypeStruct((batch, seq, width), dtype)
    const = pl.BlockSpec((gs, gs), lambda b, i: (0, 0))
    return pl.pallas_call(
        _c_chunk_kernel,
        out_shape=(shape(C_WIDTH, BF16), shape(C_WIDTH, BF16), shape(C_WIDTH, BF16), shape(C_WIDTH, BF16),
                   shape(C_HEADS * gs, BF16), shape(128, F32)),
        grid=(batch, seq // gs),
        in_specs=[wide(3 * C_WIDTH), wide(128),
                  pl.BlockSpec((1, 2 * C_HEADS, gs), lambda b, i: (b, 0, i)),
                  const, const, const],
        out_specs=(wide(C_WIDTH),) * 4 + (wide(C_HEADS * gs), wide(128)),
        compiler_params=_params("parallel", "parallel"),
        name="c_chunks",
    )(qkv, bg, bgt, jnp.asarray(tri, BF16), jnp.asarray(tri.T, BF16), jnp.asarray(same, BF16))


def _c_scan_kernel(u_ref, w_ref, qg_ref, kg_ref, attn_ref, gc_ref, o_ref, state_ref, vnew_ref):
    @pl.when(pl.program_id(1) == 0)
    def _():
        state_ref[...] = jnp.zeros_like(state_ref)

    cs = C_CHUNK
    dk = C_HEAD_DIM
    gs = C_GROUP * cs
    vnew_ref[...] = jnp.zeros_like(vnew_ref)
    heads = range(C_HEADS)
    cols = [slice(h * dk, (h + 1) * dk) for h in heads]
    states = [state_ref[h] for h in heads]
    for c in range(C_GROUP):
        rs = slice(c * cs, (c + 1) * cs)
        decay_last = jnp.exp(gc_ref[0, (c + 1) * cs - 1:(c + 1) * cs, :])
        both = [_dot(jnp.concatenate([w_ref[0, rs, cols[h]], qg_ref[0, rs, cols[h]]], axis=0),
                     states[h].astype(BF16)) for h in heads]
        v16 = [(u_ref[0, rs, cols[h]] - both[h][:cs]).astype(BF16) for h in heads]
        for h in heads:
            vnew_ref[h, rs, :] = v16[h]
        for h in heads:
            out = both[h][cs:] + _dot(attn_ref[0, rs, h * gs:(h + 1) * gs], vnew_ref[h])
            o_ref[0, rs, cols[h]] = out.astype(o_ref.dtype)
        states = [states[h] * decay_last[:, C_HEADS + h:C_HEADS + h + 1] + _dot_tn(kg_ref[0, rs, cols[h]], v16[h])
                  for h in heads]
    for h in heads:
        state_ref[h] = states[h]


def _c_scan(u, w, qg, kg, attn, gc, batch, seq):
    gs = C_GROUP * C_CHUNK
    wide = lambda width: pl.BlockSpec((1, gs, width), lambda b, c: (b, c, 0))
    return pl.pallas_call(
        _c_scan_kernel,
        out_shape=jax.ShapeDtypeStruct((batch, seq, C_WIDTH), BF16),
        grid=(batch, seq // gs),
        in_specs=[wide(C_WIDTH)] * 4 + [wide(C_HEADS * gs), wide(128)],
        out_specs=wide(C_WIDTH),
        scratch_shapes=[pltpu.VMEM((C_HEADS, C_HEAD_DIM, C_HEAD_DIM), F32),
                        pltpu.VMEM((C_HEADS, gs, C_HEAD_DIM), BF16)],
        compiler_params=_params("parallel", "arbitrary"),
        name="c_scan",
    )(u, w, qg, kg, attn, gc)


def _c_out_kernel(o_ref, z_ref, g_ref, w_ref, x_ref, out_ref):
    dk = C_HEAD_DIM
    parts = []
    for h in range(C_HEADS):
        sl = slice(h * dk, (h + 1) * dk)
        z = z_ref[:, sl].astype(F32)
        parts.append((_rms(o_ref[:, sl].astype(F32), g_ref[...]) * (z * jax.nn.sigmoid(z))).astype(BF16))
    out_ref[...] = x_ref[...] + _dot(jnp.concatenate(parts, axis=-1), w_ref[...])


def _c_out(o, proj, out_gain, w_out, x):
    m, d = x.shape
    z_block = (3 * C_WIDTH) // C_WIDTH
    row = lambda width: pl.BlockSpec((ROW_TILE, width), lambda i: (i, 0))
    return pl.pallas_call(
        _c_out_kernel,
        out_shape=jax.ShapeDtypeStruct((m, d), F32),
        grid=(m // ROW_TILE,),
        in_specs=[row(C_WIDTH),
                  pl.BlockSpec((ROW_TILE, C_WIDTH), lambda i: (i, z_block)),
                  pl.BlockSpec((1, C_HEAD_DIM), lambda i: (0, 0)),
                  pl.BlockSpec((C_WIDTH, d), lambda i: (0, 0)),
                  row(d)],
        out_specs=row(d),
        compiler_params=_params("parallel"),
        name="c_out",
    )(o, proj, out_gain.reshape(1, -1), w_out, x)


def _mixer_c(x, norm1, w_in, conv_w, a_log, dt_bias, out_gain, w_out, batch, seq):
    main = 4 * C_WIDTH
    proj = _norm_matmul(x, norm1, w_in[:, :main].astype(BF16), BF16, 512)
    w_small = jnp.pad(w_in[:, main:], ((0, 0), (0, 128 - 2 * C_HEADS))).astype(BF16)
    small = _norm_matmul(x, norm1, w_small, F32, 128)
    qkv, bg = _c_conv(proj.reshape(batch, seq, main), small.reshape(batch, seq, 128), conv_w, a_log, dt_bias,
                      batch, seq)
    bgt = bg[:, :, :2 * C_HEADS].transpose(0, 2, 1)
    u, w, qg, kg, attn, gc = _c_chunks(qkv, bg, bgt, batch, seq)
    o = _c_scan(u, w, qg, kg, attn, gc, batch, seq)
    return _c_out(o.reshape(batch * seq, C_WIDTH), proj, out_gain, w_out.astype(BF16), x)


def kernel(x, rel_bias, l0_norm1, l0_a_w_in, l0_a_q_gain, l0_a_k_gain, l0_a_w_out, l0_norm2, l0_ffn_w_gate, l0_ffn_w_up, l0_ffn_w_down, l1_norm1, l1_b_w_in, l1_b_q_gain, l1_b_k_gain, l1_b_cmp_pos, l1_b_cmp_w1, l1_b_cmp_w2, l1_b_w_out, l1_norm2, l1_ffn_w_gate, l1_ffn_w_up, l1_ffn_w_down, l2_norm1, l2_c_w_in, l2_c_conv_w, l2_c_a_log, l2_c_dt_bias, l2_c_out_gain, l2_c_w_out, l2_norm2, l2_ffn_w_gate, l2_ffn_w_up, l2_ffn_w_down, l3_norm1, l3_a_w_in, l3_a_q_gain, l3_a_k_gain, l3_a_w_out, l3_norm2, l3_ffn_w_gate, l3_ffn_w_up, l3_ffn_w_down):
    batch, seq, d = x.shape
    h = x.reshape(batch * seq, d)

    def ffn(h, norm2, w_gate, w_up, w_down):
        return _ffn(h, norm2, w_gate.astype(BF16), w_up.astype(BF16), w_down.astype(BF16))

    a_biases = _a_biases(rel_bias)
    h = _mixer_a(h, a_biases, l0_norm1, l0_a_w_in, l0_a_q_gain, l0_a_k_gain, l0_a_w_out, batch, seq)
    h = ffn(h, l0_norm2, l0_ffn_w_gate, l0_ffn_w_up, l0_ffn_w_down)
    h = _mixer_b(h, rel_bias, l1_norm1, l1_b_w_in, l1_b_q_gain, l1_b_k_gain, l1_b_cmp_pos, l1_b_cmp_w1,
                 l1_b_cmp_w2, l1_b_w_out, batch, seq)
    h = ffn(h, l1_norm2, l1_ffn_w_gate, l1_ffn_w_up, l1_ffn_w_down)
    h = _mixer_c(h, l2_norm1, l2_c_w_in, l2_c_conv_w, l2_c_a_log, l2_c_dt_bias, l2_c_out_gain, l2_c_w_out,
                 batch, seq)
    h = ffn(h, l2_norm2, l2_ffn_w_gate, l2_ffn_w_up, l2_ffn_w_down)
    h = _mixer_a(h, a_biases, l3_norm1, l3_a_w_in, l3_a_q_gain, l3_a_k_gain, l3_a_w_out, batch, seq)
    h = ffn(h, l3_norm2, l3_ffn_w_gate, l3_ffn_w_up, l3_ffn_w_down)
    return h.reshape(batch, seq, d)
```

```python
import functools
import math

import numpy as np
import jax
import jax.numpy as jnp
from jax import lax
from jax.experimental import pallas as pl
from jax.experimental.pallas import tpu as pltpu

D_MODEL = 1024
RMS_EPS = 1e-6
NEG_INF = -1e30
TINY = 1e-30
FORCE_SCORE = 1e9

N_BUCKETS = 32
REL_MAX_DISTANCE = 2048
N_HEADS = 16

A_GROUPS = ((128, 1), (512, 4), (2048, 16))
A_HEAD_DIM = 64
A_Q_BLOCK = 128
A_PROJ_TILE = 512
A_STAT_WIDTH = 256
A_BLOCKS_PER_STEP = 8

B_KV_HEADS = 4
B_GROUP = 4
B_HEAD_DIM = 64
B_CMP_LEN = 32
B_CMP_STRIDE = 16
B_CMP_HIDDEN = 256
B_SEL_BLOCK = 64
B_TOP_N = 16
B_WINDOW = 512
B_TILE = 128
B_SWEEP = 256
B_PROJ_WIDTH = 3072
B_GATE_ROWS = 16

C_HEADS = 8
C_HEAD_DIM = 128
C_WIDTH = C_HEADS * C_HEAD_DIM
C_CONV = 4
C_CHUNK = 64
C_GROUP = 4

FFN_HIDDEN = 2816
FFN_TILE = 1024

ROW_TILE = 512
VMEM_LIMIT = 48 * 1024 * 1024

LOG2E = math.log2(math.e)

F32 = jnp.float32
BF16 = jnp.bfloat16

NT_DIMS = (((1,), (1,)), ((), ()))
TN_DIMS = (((0,), (0,)), ((), ()))


def _params(*semantics):
    return pltpu.CompilerParams(dimension_semantics=semantics, vmem_limit_bytes=VMEM_LIMIT)


def _dot(a, b):
    return jnp.dot(a, b, preferred_element_type=F32)


def _dot_nt(a, b):
    return lax.dot_general(a, b, NT_DIMS, preferred_element_type=F32)


def _dot_tn(a, b):
    return lax.dot_general(a, b, TN_DIMS, preferred_element_type=F32)


def _rms(x, gain):
    return x * lax.rsqrt(jnp.mean(x * x, axis=-1, keepdims=True) + RMS_EPS) * gain


def _bucket_thresholds():
    d = np.arange(1 << 15)
    max_exact = N_BUCKETS // 2
    d_f = np.maximum(d, 1).astype(np.float32)
    large = max_exact + (np.log(d_f / np.float32(max_exact)) / np.float32(math.log(REL_MAX_DISTANCE / max_exact))
                         * np.float32(N_BUCKETS - max_exact)).astype(np.int32)
    bucket = np.where(d < max_exact, d, np.minimum(large, N_BUCKETS - 1))
    return [int(np.argmax(bucket >= k)) if np.any(bucket >= k) else int(1 << 30) for k in range(N_BUCKETS)]


_THRESHOLDS = _bucket_thresholds()


def _bias_tile_kernel(tbl_ref, o_ref, *, base, tile_step, row_step, col_step, dmax, dil):
    h = pl.program_id(0)
    t = pl.program_id(1)
    shape = o_ref.shape[2:]
    i = lax.broadcasted_iota(jnp.int32, shape, 0)
    j = lax.broadcasted_iota(jnp.int32, shape, 1)
    dist = base + tile_step * t + row_step * i + col_step * j
    d = dist * dil
    val = jnp.full(shape, tbl_ref[0, h], F32)
    for k in range(1, N_BUCKETS):
        val = jnp.where(d >= _THRESHOLDS[k], tbl_ref[k, h], val)
    valid = (dist >= 0) & (dist <= dmax)
    o_ref[0, 0] = jnp.where(valid, val * LOG2E, NEG_INF).astype(o_ref.dtype)


def _bias_tiles(rel_bias, n_tiles, rows, cols, *, base, tile_step, row_step, col_step, dmax, dil=1, dtype=F32):
    kern = functools.partial(_bias_tile_kernel, base=base, tile_step=tile_step, row_step=row_step,
                             col_step=col_step, dmax=dmax, dil=dil)
    return pl.pallas_call(
        kern,
        out_shape=jax.ShapeDtypeStruct((N_HEADS, n_tiles, rows, cols), dtype),
        grid=(N_HEADS, n_tiles),
        in_specs=[pl.BlockSpec(memory_space=pltpu.SMEM)],
        out_specs=pl.BlockSpec((1, 1, rows, cols), lambda h, t: (h, t, 0, 0)),
        compiler_params=_params("parallel", "parallel"),
        name="bias_tiles",
    )(rel_bias)


def _resident(shape):
    return pl.BlockSpec(shape, lambda i: (0,) * len(shape), pipeline_mode=pl.Buffered(1))


def _norm_matmul_kernel(x_ref, g_ref, w_ref, o_ref, *, tn):
    h = _rms(x_ref[...], g_ref[...]).astype(BF16)
    for j in range(w_ref.shape[1] // tn):
        o_ref[:, j * tn:(j + 1) * tn] = _dot(h, w_ref[:, j * tn:(j + 1) * tn]).astype(o_ref.dtype)


def _norm_matmul(x, gain, w, out_dtype, tn):
    m, d = x.shape
    n = w.shape[1]
    return pl.pallas_call(
        functools.partial(_norm_matmul_kernel, tn=tn),
        out_shape=jax.ShapeDtypeStruct((m, n), out_dtype),
        grid=(m // ROW_TILE,),
        in_specs=[pl.BlockSpec((ROW_TILE, d), lambda i: (i, 0)),
                  _resident((1, d)),
                  _resident((d, n))],
        out_specs=pl.BlockSpec((ROW_TILE, n), lambda i: (i, 0)),
        compiler_params=_params("parallel"),
        name="norm_matmul",
    )(x, gain.reshape(1, d), w)


def _matmul_residual_kernel(a_ref, w_ref, x_ref, o_ref):
    o_ref[...] = x_ref[...] + _dot(a_ref[...], w_ref[...])


def _matmul_residual(a, w, x):
    m, k = a.shape
    d = w.shape[1]
    return pl.pallas_call(
        _matmul_residual_kernel,
        out_shape=jax.ShapeDtypeStruct((m, d), F32),
        grid=(m // ROW_TILE,),
        in_specs=[pl.BlockSpec((ROW_TILE, k), lambda i: (i, 0)),
                  pl.BlockSpec((k, d), lambda i: (0, 0)),
                  pl.BlockSpec((ROW_TILE, d), lambda i: (i, 0))],
        out_specs=pl.BlockSpec((ROW_TILE, d), lambda i: (i, 0)),
        compiler_params=_params("parallel"),
        name="matmul_residual",
    )(a, w, x)


def _ffn_kernel(x_ref, g_ref, wg_ref, wu_ref, wd_ref, o_ref):
    x = x_ref[...]
    h = _rms(x, g_ref[...]).astype(BF16)
    hidden = wg_ref.shape[1]
    acc = x
    for lo in range(0, hidden, FFN_TILE):
        hi = min(lo + FFN_TILE, hidden)
        a = _dot(h, wg_ref[:, lo:hi])
        b = _dot(h, wu_ref[:, lo:hi])
        acc = acc + _dot((a * jax.nn.sigmoid(a) * b).astype(BF16), wd_ref[lo:hi, :])
    o_ref[...] = acc


def _ffn(x, gain, w_gate, w_up, w_down):
    m, d = x.shape
    hidden = w_gate.shape[1]
    return pl.pallas_call(
        _ffn_kernel,
        out_shape=jax.ShapeDtypeStruct((m, d), F32),
        grid=(m // ROW_TILE,),
        in_specs=[pl.BlockSpec((ROW_TILE, d), lambda i: (i, 0)),
                  _resident((1, d)),
                  _resident((d, hidden)), _resident((d, hidden)), _resident((hidden, d))],
        out_specs=pl.BlockSpec((ROW_TILE, d), lambda i: (i, 0)),
        compiler_params=_params("parallel"),
        name="ffn",
    )(x, gain.reshape(1, d), w_gate, w_up, w_down)


def _a_proj_kernel(x_ref, g_ref, w_ref, qg_ref, kg_ref, o_ref, h_ref, x_scr, *, dil):
    rows = ROW_TILE // dil
    xn = _rms(x_ref[...], g_ref[...])
    if dil == 1:
        h_ref[...] = xn.astype(BF16)
    else:
        slabs = xn.shape[1] // 128
        for c in range(slabs):
            x_scr[c] = xn[:, c * 128:(c + 1) * 128]
        for r in range(dil):
            picked = [x_scr[c, pl.ds(r, rows, stride=dil), :] for c in range(slabs)]
            h_ref[r * rows:(r + 1) * rows, :] = jnp.concatenate(picked, axis=1).astype(BF16)
    h = h_ref[...]
    width = w_ref.shape[1]
    hd = width // 3
    low = lax.broadcasted_iota(jnp.int32, (ROW_TILE, 128), 1) < A_HEAD_DIM
    for j in range(width // A_PROJ_TILE):
        res = _dot(h, w_ref[:, j * A_PROJ_TILE:(j + 1) * A_PROJ_TILE])
        kind = (j * A_PROJ_TILE) // hd
        if kind < 2:
            parts = []
            for c in range(A_PROJ_TILE // 128):
                y = res[:, c * 128:(c + 1) * 128]
                sq = y * y
                tot = jnp.sum(sq, axis=-1, keepdims=True)
                lo = jnp.sum(jnp.where(low, sq, 0.0), axis=-1, keepdims=True)
                ss = jnp.where(low, lo, tot - lo)
                parts.append(y * lax.rsqrt(ss * (1.0 / A_HEAD_DIM) + RMS_EPS))
            res = jnp.concatenate(parts, axis=1) * (qg_ref if kind == 0 else kg_ref)[...]
        res = res.astype(BF16)
        for r in range(dil):
            off = r * width + j * A_PROJ_TILE
            o_ref[:, off:off + A_PROJ_TILE] = res[r * rows:(r + 1) * rows]


def _a_proj(x, gain, w, q_gain, k_gain, dil):
    m, d = x.shape
    width = w.shape[1]
    reps = A_PROJ_TILE // A_HEAD_DIM
    qg = jnp.tile(q_gain * (A_HEAD_DIM ** -0.5 * LOG2E), reps).reshape(1, A_PROJ_TILE)
    kg = jnp.tile(k_gain, reps).reshape(1, A_PROJ_TILE)
    return pl.pallas_call(
        functools.partial(_a_proj_kernel, dil=dil),
        out_shape=jax.ShapeDtypeStruct((m // dil, dil * width), BF16),
        grid=(m // ROW_TILE,),
        in_specs=[pl.BlockSpec((ROW_TILE, d), lambda i: (i, 0)),
                  pl.BlockSpec((1, d), lambda i: (0, 0)),
                  pl.BlockSpec((d, width), lambda i: (0, 0)),
                  pl.BlockSpec((1, A_PROJ_TILE), lambda i: (0, 0)),
                  pl.BlockSpec((1, A_PROJ_TILE), lambda i: (0, 0))],
        out_specs=pl.BlockSpec((ROW_TILE // dil, dil * width), lambda i: (i, 0)),
        scratch_shapes=[pltpu.VMEM((ROW_TILE, d), BF16), pltpu.VMEM((d // 128, ROW_TILE, 128), F32)],
        compiler_params=_params("parallel"),
        name="a_proj",
    )(x, gain.reshape(1, d), w, qg, kg)


def _a_attn_kernel(cur_ref, prev_ref, bias_ref, o_ref, stat_ref):
    first = (pl.program_id(2) == 0).astype(jnp.int32)
    nq = A_Q_BLOCK
    hd = N_HEADS * A_HEAD_DIM
    lane = lax.broadcasted_iota(jnp.int32, (nq, 128), 1)
    low = lane < A_HEAD_DIM
    ones = jnp.ones((2 * nq, 128), BF16)
    for res in range(o_ref.shape[2] // hd):
        for sub in range(cur_ref.shape[1] // nq):
            rows = slice(sub * nq, (sub + 1) * nq)
            max_tile = jnp.zeros((nq, 128), F32)
            den_tile = jnp.ones((nq, 128), F32)
            for pair in range(N_HEADS // 2):
                qc, kc, vc = (slice(res * 3 * hd + part * hd + pair * 128, res * 3 * hd + part * hd + (pair + 1) * 128)
                              for part in range(3))
                q = cur_ref[0, rows, qc]
                zero = jnp.zeros_like(q)
                qq = jnp.concatenate([jnp.where(low, q, zero), jnp.where(low, zero, q)], axis=0)
                if sub == 0:
                    kk = jnp.concatenate([prev_ref[0, :, kc], cur_ref[0, :nq, kc]], axis=0)
                    vv = jnp.concatenate([prev_ref[0, :, vc], cur_ref[0, :nq, vc]], axis=0)
                    base = 2 * pair + N_HEADS * first
                else:
                    kk = cur_ref[0, (sub - 1) * nq:(sub + 1) * nq, kc]
                    vv = cur_ref[0, (sub - 1) * nq:(sub + 1) * nq, vc]
                    base = 2 * pair
                s = _dot_nt(qq, kk) + jnp.concatenate([bias_ref[base], bias_ref[base + 1]], axis=0)
                m = jnp.max(s, axis=-1, keepdims=True)
                acc = _dot(jnp.exp2(s - m).astype(BF16), jnp.concatenate([vv, ones], axis=1))
                oc = slice(res * hd + pair * 128, res * hd + (pair + 1) * 128)
                o_ref[0, rows, oc] = jnp.where(low, acc[:nq, :128], acc[nq:, :128]).astype(o_ref.dtype)
                first_head = lane == 2 * pair
                second_head = lane == 2 * pair + 1
                max_tile = jnp.where(first_head, m[:nq], jnp.where(second_head, m[nq:], max_tile))
                den_tile = jnp.where(first_head, acc[:nq, 128:], jnp.where(second_head, acc[nq:, 128:], den_tile))
            stat_ref[0, rows, res * A_STAT_WIDTH:res * A_STAT_WIDTH + 128] = max_tile
            stat_ref[0, rows, res * A_STAT_WIDTH + 128:(res + 1) * A_STAT_WIDTH] = den_tile


def _a_attention(proj, bias, dil, batch, seq):
    length = seq // dil
    nblk = length // A_Q_BLOCK
    hd = N_HEADS * A_HEAD_DIM
    pv = proj.reshape(batch, length, dil * 3 * hd)

    per = min(A_BLOCKS_PER_STEP, nblk)
    res = min(A_BLOCKS_PER_STEP // per, dil)
    step_rows = per * A_Q_BLOCK
    assert nblk % per == 0 and dil % res == 0

    o, stats = pl.pallas_call(
        _a_attn_kernel,
        out_shape=(jax.ShapeDtypeStruct((batch, length, dil * hd), BF16),
                   jax.ShapeDtypeStruct((batch, length, dil * A_STAT_WIDTH), F32)),
        grid=(batch, dil // res, nblk // per),
        in_specs=[pl.BlockSpec((1, step_rows, res * 3 * hd), lambda b, r, i: (b, i, r)),
                  pl.BlockSpec((1, A_Q_BLOCK, res * 3 * hd), lambda b, r, i: (b, jnp.maximum(per * i - 1, 0), r)),
                  pl.BlockSpec((2 * N_HEADS, A_Q_BLOCK, 2 * A_Q_BLOCK), lambda b, r, i: (0, 0, 0))],
        out_specs=(pl.BlockSpec((1, step_rows, res * hd), lambda b, r, i: (b, i, r)),
                   pl.BlockSpec((1, step_rows, res * A_STAT_WIDTH), lambda b, r, i: (b, i, r))),
        compiler_params=_params("parallel", "parallel", "arbitrary"),
        name="a_attention",
    )(pv, pv, bias)
    return o.reshape(batch * length, dil * hd), stats.reshape(batch * length, dil * A_STAT_WIDTH)


def _a_out_kernel(o0_ref, o1_ref, o2_ref, s0_ref, s1_ref, s2_ref, e_ref, w_ref, x_ref, out_ref, o_scr, s_scr):
    hd = N_HEADS * A_HEAD_DIM
    sw = A_STAT_WIDTH
    for g, (o_ref, s_ref) in enumerate(((o0_ref, s0_ref), (o1_ref, s1_ref), (o2_ref, s2_ref))):
        dil = A_GROUPS[g][1]
        rows = ROW_TILE // dil
        for r in range(dil):
            dst = pl.ds(r, rows, stride=dil) if dil > 1 else slice(None)
            s_scr[g, 0, dst, :] = s_ref[:, r * sw:r * sw + 128]
            s_scr[g, 1, dst, :] = s_ref[:, r * sw + 128:(r + 1) * sw]
            for c in range(hd // 128):
                o_scr[g, c, dst, :] = o_ref[:, r * hd + c * 128:r * hd + (c + 1) * 128].astype(F32)
    groups = range(len(A_GROUPS))
    top = functools.reduce(jnp.maximum, [s_scr[g, 0] for g in groups])
    es = [jnp.exp2(s_scr[g, 0] - top) for g in groups]
    inv = 1.0 / sum(es[g] * s_scr[g, 1] for g in groups)
    expand = e_ref[...]
    acc = None
    for g in groups:
        o_g = jnp.concatenate([o_scr[g, c] for c in range(hd // 128)], axis=1)
        term = _dot((es[g] * inv).astype(BF16), expand) * o_g
        acc = term if acc is None else acc + term
    out_ref[...] = x_ref[...] + _dot(acc.astype(BF16), w_ref[...])


def _a_out(outs, stats, w_out, x):
    m, d = x.shape
    hd = N_HEADS * A_HEAD_DIM
    expand = np.zeros((128, hd), np.float32)
    for h in range(N_HEADS):
        expand[h, h * A_HEAD_DIM:(h + 1) * A_HEAD_DIM] = 1.0
    grouped = lambda width: [pl.BlockSpec((ROW_TILE // dil, dil * width), lambda i: (i, 0)) for _, dil in A_GROUPS]
    return pl.pallas_call(
        _a_out_kernel,
        out_shape=jax.ShapeDtypeStruct((m, d), F32),
        grid=(m // ROW_TILE,),
        in_specs=grouped(hd) + grouped(A_STAT_WIDTH) + [
            pl.BlockSpec((128, hd), lambda i: (0, 0)),
            pl.BlockSpec((hd, d), lambda i: (0, 0)),
            pl.BlockSpec((ROW_TILE, d), lambda i: (i, 0))],
        out_specs=pl.BlockSpec((ROW_TILE, d), lambda i: (i, 0)),
        scratch_shapes=[pltpu.VMEM((len(A_GROUPS), hd // 128, ROW_TILE, 128), F32),
                        pltpu.VMEM((len(A_GROUPS), 2, ROW_TILE, 128), F32)],
        compiler_params=_params("parallel"),
        name="a_out",
    )(*outs, *stats, jnp.asarray(expand, BF16), w_out, x)


def _a_biases(rel_bias):
    biases = []
    for window, dil in A_GROUPS:
        steps = window // dil
        assert steps == A_Q_BLOCK
        bias = _bias_tiles(rel_bias, 1, A_Q_BLOCK, 2 * A_Q_BLOCK, base=A_Q_BLOCK, tile_step=0, row_step=1,
                           col_step=-1, dmax=steps, dil=dil)[:, 0]
        biases.append(jnp.concatenate([bias, bias.at[:, :, :A_Q_BLOCK].set(NEG_INF)], axis=0))
    return biases


def _mixer_a(x, biases, norm1, w_in, q_gain, k_gain, w_out, batch, seq):
    w_in = w_in.astype(BF16)
    group_width = 3 * N_HEADS * A_HEAD_DIM
    outs, stats = [], []
    for gi, (_, dil) in enumerate(A_GROUPS):
        assert (seq // dil) % A_Q_BLOCK == 0 and seq % ROW_TILE == 0
        bias = biases[gi]
        proj = _a_proj(x, norm1, w_in[:, gi * group_width:(gi + 1) * group_width], q_gain[gi], k_gain[gi], dil)
        o, stat = _a_attention(proj, bias, dil, batch, seq)
        outs.append(o)
        stats.append(stat)
    return _a_out(outs, stats, w_out.astype(BF16), x)


def _b_prep_kernel(p_ref, qg_ref, kg_ref, qt_ref, ck_ref, cv_ref, ka_ref, vs_ref, wk_ref, vw_ref, gate_ref):
    dh = B_HEAD_DIM
    ts = p_ref.shape[1]
    n_sel = ka_ref.shape[3] - dh
    low = lax.broadcasted_iota(jnp.int32, (ts, 128), 1) < dh

    def slab(c):
        return p_ref[0, :, c * 128:(c + 1) * 128].astype(F32)

    def normed(x, gain):
        sq = x * x
        tot = jnp.sum(sq, axis=-1, keepdims=True)
        first = jnp.sum(jnp.where(low, sq, 0.0), axis=-1, keepdims=True)
        ss = jnp.where(low, first, tot - first)
        return x * lax.rsqrt(ss * (1.0 / dh) + RMS_EPS) * jnp.concatenate([gain, gain], axis=1)

    def halves(x):
        return x[:, :dh], x[:, dh:]

    qg = qg_ref[...] * (dh ** -0.5 * LOG2E)
    for c in range(N_HEADS // 2):
        xt = normed(slab(c), qg).T.astype(BF16)
        qt_ref[0, 2 * c] = xt[:dh]
        qt_ref[0, 2 * c + 1] = xt[dh:]
    base = N_HEADS // 2
    pairs = B_KV_HEADS // 2
    pos = pl.program_id(1) * ts + lax.broadcasted_iota(jnp.int32, (ts, 128), 0)
    lane = lax.broadcasted_iota(jnp.int32, (ts, 128), 1)
    onehot = jnp.where(lane - dh == pos // B_SEL_BLOCK, 1.0, 0.0)
    ones = jnp.ones((vs_ref.shape[2] - dh, ts), BF16)
    for j in range(pairs):
        for ref, off in ((ck_ref, 0), (cv_ref, pairs)):
            a, b = halves(slab(base + off + j))
            ref[0, 2 * j] = a.astype(BF16)
            ref[0, 2 * j + 1] = b.astype(BF16)
        k_sel = normed(slab(base + 2 * pairs + j), kg_ref[1:2, :])
        for n, k in zip((2 * j, 2 * j + 1), (k_sel, pltpu.roll(k_sel, dh, 1))):
            ka_ref[0, n] = jnp.where(low, k, onehot)[:, :dh + n_sel].astype(BF16)
        for n, k in zip((2 * j, 2 * j + 1), halves(normed(slab(base + 4 * pairs + j), kg_ref[2:3, :]))):
            wk_ref[0, n] = k.astype(BF16)
        for ref, off in ((vs_ref, 3 * pairs), (vw_ref, 5 * pairs)):
            xt = slab(base + off + j).T.astype(BF16)
            for n, v in zip((2 * j, 2 * j + 1), (xt[:dh], xt[dh:])):
                ref[0, n, :dh, :] = v
                ref[0, n, dh:, :] = ones
    gate = jax.nn.sigmoid(slab(base + 6 * pairs)).T
    rows = gate_ref.shape[2]
    for n in range(B_KV_HEADS):
        gate_ref[0, n] = gate[n * rows:(n + 1) * rows]


def _b_prep(proj, q_gain, k_gain, batch, seq):
    ts = 256
    dh = B_HEAD_DIM
    n_sel = seq // B_SEL_BLOCK
    assert dh + n_sel <= 128
    rows_shape = lambda width: jax.ShapeDtypeStruct((batch, B_KV_HEADS, seq, width), BF16)
    rows_spec = lambda width: pl.BlockSpec((1, B_KV_HEADS, ts, width), lambda b, i: (b, 0, i, 0))
    cols_shape = lambda heads, height, dtype: jax.ShapeDtypeStruct((batch, heads, height, seq), dtype)
    cols_spec = lambda heads, height: pl.BlockSpec((1, heads, height, ts), lambda b, i: (b, 0, 0, i))
    return pl.pallas_call(
        _b_prep_kernel,
        out_shape=(cols_shape(N_HEADS, dh, BF16), rows_shape(dh), rows_shape(dh), rows_shape(dh + n_sel),
                   cols_shape(B_KV_HEADS, dh + 16, BF16), rows_shape(dh), cols_shape(B_KV_HEADS, dh + 16, BF16),
                   cols_shape(B_KV_HEADS, B_GATE_ROWS, F32)),
        grid=(batch, seq // ts),
        in_specs=[pl.BlockSpec((1, ts, B_PROJ_WIDTH), lambda b, i: (b, i, 0)),
                  pl.BlockSpec((1, dh), lambda b, i: (0, 0)),
                  pl.BlockSpec((3, dh), lambda b, i: (0, 0))],
        out_specs=(cols_spec(N_HEADS, dh), rows_spec(dh), rows_spec(dh), rows_spec(dh + n_sel),
                   cols_spec(B_KV_HEADS, dh + 16), rows_spec(dh), cols_spec(B_KV_HEADS, dh + 16),
                   cols_spec(B_KV_HEADS, B_GATE_ROWS)),
        compiler_params=_params("parallel", "parallel"),
        name="b_prep",
    )(proj, q_gain.reshape(1, dh), k_gain)


def _b_compress_kernel(tk_ref, tv_ref, pos_ref, w1_ref, w2_ref, kg_ref, kc_ref, vc_ref):
    half = (B_CMP_LEN // 2) * B_HEAD_DIM
    for kv, (t_ref, out_ref) in enumerate(((tk_ref, kc_ref), (tv_ref, vc_ref))):
        t = t_ref[0, 0].astype(F32)
        top = (t + pos_ref[kv, 0:1, :]).astype(BF16)
        bot = (t + pos_ref[kv, 1:2, :]).astype(BF16)
        a1 = _dot(top, w1_ref[kv, :half, :])
        a2 = _dot(bot, w1_ref[kv, half:, :])
        hidden = a1 + pltpu.roll(a2, a2.shape[0] - 1, 0)
        out = _dot(jax.nn.gelu(hidden).astype(BF16), w2_ref[kv])
        if kv == 0:
            out = _rms(out, kg_ref[...])
        out_ref[0, 0] = out.astype(out_ref.dtype)


def _b_compress(ck, cv, cmp_pos, cmp_w1, cmp_w2, k_gain0, batch, seq):
    rows = seq // B_CMP_STRIDE
    half = (B_CMP_LEN // 2) * B_HEAD_DIM
    tk = ck.reshape(batch, B_KV_HEADS, rows, half)
    tv = cv.reshape(batch, B_KV_HEADS, rows, half)
    pos = cmp_pos.reshape(2, 2, half)
    t_spec = pl.BlockSpec((1, 1, rows, half), lambda b, n: (b, n, 0, 0))
    o_spec = pl.BlockSpec((1, 1, rows, B_HEAD_DIM), lambda b, n: (b, n, 0, 0))
    shape = jax.ShapeDtypeStruct((batch, B_KV_HEADS, rows, B_HEAD_DIM), BF16)
    return pl.pallas_call(
        _b_compress_kernel,
        out_shape=(shape, shape),
        grid=(batch, B_KV_HEADS),
        in_specs=[t_spec, t_spec,
                  pl.BlockSpec((2, 2, half), lambda b, n: (0, 0, 0)),
                  pl.BlockSpec((2, 2 * half, B_CMP_HIDDEN), lambda b, n: (0, 0, 0)),
                  pl.BlockSpec((2, B_CMP_HIDDEN, B_HEAD_DIM), lambda b, n: (0, 0, 0)),
                  pl.BlockSpec((1, B_HEAD_DIM), lambda b, n: (0, 0))],
        out_specs=(o_spec, o_spec),
        compiler_params=_params("parallel", "parallel"),
        name="b_compress",
    )(tk, tv, pos, cmp_w1.astype(BF16), cmp_w2.astype(BF16), k_gain0.reshape(1, -1))


def _b_cmp_attn_kernel(qt_ref, kc_ref, vct_ref, bias_ref, c2s_ref, oc_ref, sel_ref, imp_ref, *, top_n):
    tq = B_SWEEP
    n_sel = imp_ref.shape[0]
    qt = jnp.concatenate([qt_ref[0, g] for g in range(B_GROUP)], axis=1)
    n_cmp_pad = kc_ref.shape[2]
    shift = tq // B_CMP_STRIDE
    off = pl.multiple_of((pl.num_programs(2) - 1 - pl.program_id(2)) * shift, shift)
    bias = jnp.concatenate([bias_ref[g, 0, pl.ds(off, n_cmp_pad), :] for g in range(B_GROUP)], axis=1)
    s = _dot(kc_ref[0, 0], qt) + bias
    m = jnp.max(s, axis=0, keepdims=True)
    e = jnp.exp2(s - m)
    z = jnp.maximum(jnp.sum(e, axis=0, keepdims=True), TINY)
    pos = pl.program_id(2) * tq + lax.broadcasted_iota(jnp.int32, (1, tq), 1)
    sees_any = jnp.concatenate([pos >= B_CMP_LEN - 1] * B_GROUP, axis=1)
    p = e * jnp.where(sees_any, 1.0 / z, 0.0)
    oct = _dot(vct_ref[0, 0], p.astype(BF16))
    for g in range(B_GROUP):
        oc_ref[0, g] = oct[:, g * tq:(g + 1) * tq]

    p_sum = p[:, 0:tq] + p[:, tq:2 * tq] + p[:, 2 * tq:3 * tq] + p[:, 3 * tq:4 * tq]
    hi = p_sum.astype(BF16)
    lo = (p_sum - hi.astype(F32)).astype(BF16)
    c2s = c2s_ref[...]
    imp = _dot(c2s, hi) + _dot(c2s, lo)

    t = pl.program_id(2) * tq + lax.broadcasted_iota(jnp.int32, (n_sel, tq), 1)
    blk = lax.broadcasted_iota(jnp.int32, (n_sel, tq), 0)
    cur = t // B_SEL_BLOCK
    forced = (blk == 0) | (blk == cur) | (blk == cur - 1)
    imp = jnp.where(forced, FORCE_SCORE, jnp.where(blk * B_SEL_BLOCK <= t, imp, NEG_INF))
    imp_ref[...] = imp

    def count(i, rank):
        row = imp_ref[pl.ds(i, 1), :]
        ahead = jnp.where(row > imp, 1.0, jnp.where(row == imp, jnp.where(blk > i, 1.0, 0.0), 0.0))
        return rank + ahead

    n_live = jnp.minimum(n_sel, (pl.program_id(2) + 1) * (tq // B_SEL_BLOCK))
    rank = lax.fori_loop(0, n_live, count, jnp.zeros((n_sel, tq), F32))
    sel_ref[0, 0] = jnp.where(rank < top_n, 0.0, NEG_INF).astype(sel_ref.dtype)


def _b_cmp_attn(qt, kc, vc, bias_c, batch, seq):
    n_sel = seq // B_SEL_BLOCK
    n_cmp_pad = seq // B_CMP_STRIDE
    n_cmp = (seq - B_CMP_LEN) // B_CMP_STRIDE + 1
    c = np.arange(n_cmp_pad)[None, :] * B_CMP_STRIDE
    j = np.arange(n_sel)[:, None] * B_SEL_BLOCK
    c2s = ((c < j + B_SEL_BLOCK) & (c + B_CMP_LEN > j) & (np.arange(n_cmp_pad)[None, :] < n_cmp)).astype(np.float32)
    kern = functools.partial(_b_cmp_attn_kernel, top_n=min(B_TOP_N, n_sel))
    return pl.pallas_call(
        kern,
        out_shape=(jax.ShapeDtypeStruct((batch, N_HEADS, B_HEAD_DIM, seq), F32),
                   jax.ShapeDtypeStruct((batch, B_KV_HEADS, n_sel, seq), BF16)),
        grid=(batch, B_KV_HEADS, seq // B_SWEEP),
        in_specs=[pl.BlockSpec((1, B_GROUP, B_HEAD_DIM, B_SWEEP), lambda b, n, i: (b, n, 0, i)),
                  pl.BlockSpec((1, 1, n_cmp_pad, B_HEAD_DIM), lambda b, n, i: (b, n, 0, 0)),
                  pl.BlockSpec((1, 1, B_HEAD_DIM, n_cmp_pad), lambda b, n, i: (b, n, 0, 0)),
                  pl.BlockSpec((B_GROUP, 1, bias_c.shape[2], B_SWEEP), lambda b, n, i: (n, 0, 0, 0)),
                  pl.BlockSpec((n_sel, n_cmp_pad), lambda b, n, i: (0, 0))],
        out_specs=(pl.BlockSpec((1, B_GROUP, B_HEAD_DIM, B_SWEEP), lambda b, n, i: (b, n, 0, i)),
                   pl.BlockSpec((1, 1, n_sel, B_SWEEP), lambda b, n, i: (b, n, 0, i))),
        scratch_shapes=[pltpu.VMEM((n_sel, B_SWEEP), F32)],
        compiler_params=_params("parallel", "parallel", "arbitrary"),
        name="b_cmp_attn",
    )(qt, kc, vc.transpose(0, 1, 3, 2), bias_c, jnp.asarray(c2s, BF16))


def _b_sparse_kernel(qt_ref, ka_ref, vs_ref, wk_ref, vw_ref, sel_ref, bs_ref, bw_ref, oc_ref, gate_ref,
                     o_ref, acc_ref, sa_ref, sb_ref, *, delta_max, win_tiles):
    tq = B_SWEEP
    dh = B_HEAD_DIM
    cols = B_GROUP * tq
    qi = pl.program_id(2)
    n_tiles = ka_ref.shape[2] // tq
    qt = jnp.concatenate([qt_ref[0, g] for g in range(B_GROUP)], axis=1)
    q_aug = jnp.concatenate([qt, jnp.concatenate([sel_ref[0, 0]] * B_GROUP, axis=1)], axis=0)

    def tile_start(kt):
        return pl.multiple_of(jnp.clip(kt, 0, n_tiles - 1) * tq, tq)

    def normalised(acc):
        return acc[:dh] * (1.0 / acc[dh:dh + 1])

    def sel_bias(kt):
        d = jnp.clip(qi - kt, -1, delta_max) + 1
        return jnp.concatenate([bs_ref[g, d] for g in range(B_GROUP)], axis=1)

    def sel_scores(kt):
        return _dot(ka_ref[0, 0, pl.ds(tile_start(kt), tq), :], q_aug).astype(BF16) + sel_bias(kt)

    def consume(s_buf, kt, m_old):
        s = s_buf[...]
        m_new = jnp.maximum(m_old, jnp.max(s, axis=0, keepdims=True).astype(F32))
        alpha = jnp.exp2(m_old - m_new)
        p = jnp.exp2(s - m_new.astype(BF16))
        acc_ref[...] = alpha * acc_ref[...] + _dot(vs_ref[0, 0, :, pl.ds(tile_start(kt), tq)], p)
        return m_new

    acc_ref[...] = jnp.zeros(acc_ref.shape, F32)
    sa_ref[...] = sel_scores(0)

    def pair(kt, m):
        sb_ref[...] = sel_scores(kt + 1)
        m = consume(sa_ref, kt, m)
        sa_ref[...] = sel_scores(kt + 2)
        return consume(sb_ref, kt + 1, m)

    n_pairs = (qi + 2) // 2
    n_double = n_pairs // 2
    m = lax.fori_loop(0, n_double, lambda j, m: pair(4 * j + 2, pair(4 * j, m)),
                      jnp.full((1, cols), NEG_INF, F32))
    lax.fori_loop(2 * n_double, n_pairs, lambda j, m: pair(2 * j, m), m)
    o_s = normalised(acc_ref[...])

    tiles = []
    for u in range(win_tiles):
        kt = qi - (win_tiles - 1) + u
        d = jnp.where(kt >= 0, qi - kt, -1) + 1
        bias = jnp.concatenate([bw_ref[g, d] for g in range(B_GROUP)], axis=1)
        tiles.append((_dot(wk_ref[0, 0, pl.ds(tile_start(kt), tq), :], qt).astype(BF16) + bias, kt))
    m = None
    for s, _ in tiles:
        tile_max = jnp.max(s, axis=0, keepdims=True)
        m = tile_max if m is None else jnp.maximum(m, tile_max)
    acc = None
    for s, kt in tiles:
        pv = _dot(vw_ref[0, 0, :, pl.ds(tile_start(kt), tq)], jnp.exp2(s - m))
        acc = pv if acc is None else acc + pv
    o_w = normalised(acc)

    gate = gate_ref[0, 0]
    merged = []
    for g in range(B_GROUP):
        cs = slice(g * tq, (g + 1) * tq)
        merged.append(gate[3 * g:3 * g + 1] * oc_ref[0, g] + gate[3 * g + 1:3 * g + 2] * o_s[:, cs]
                      + gate[3 * g + 2:3 * g + 3] * o_w[:, cs])
    for pair in range(B_GROUP // 2):
        both = jnp.concatenate([merged[2 * pair], merged[2 * pair + 1]], axis=0)
        o_ref[0, :, pair * 2 * dh:(pair + 1) * 2 * dh] = both.T.astype(o_ref.dtype)


def _b_sparse(qt, ka, vs, wk, vw, sel, bias_s, bias_w, oc, gate, batch, seq):
    n_sel = seq // B_SEL_BLOCK
    dh = B_HEAD_DIM
    n_ds = bias_s.shape[1]
    n_dw = bias_w.shape[1]
    vrows = vs.shape[2]
    kern = functools.partial(_b_sparse_kernel, delta_max=n_ds - 2, win_tiles=n_dw - 1)
    whole = lambda rows, width: pl.BlockSpec((1, 1, rows, width), lambda b, n, i: (b, n, 0, 0))
    return pl.pallas_call(
        kern,
        out_shape=jax.ShapeDtypeStruct((batch, seq, N_HEADS * dh), BF16),
        grid=(batch, B_KV_HEADS, seq // B_SWEEP),
        in_specs=[pl.BlockSpec((1, B_GROUP, dh, B_SWEEP), lambda b, n, i: (b, n, 0, i)),
                  whole(seq, dh + n_sel), whole(vrows, seq), whole(seq, dh), whole(vrows, seq),
                  pl.BlockSpec((1, 1, n_sel, B_SWEEP), lambda b, n, i: (b, n, 0, i)),
                  pl.BlockSpec((B_GROUP, n_ds, B_SWEEP, B_SWEEP), lambda b, n, i: (n, 0, 0, 0)),
                  pl.BlockSpec((B_GROUP, n_dw, B_SWEEP, B_SWEEP), lambda b, n, i: (n, 0, 0, 0)),
                  pl.BlockSpec((1, B_GROUP, dh, B_SWEEP), lambda b, n, i: (b, n, 0, i)),
                  pl.BlockSpec((1, 1, B_GATE_ROWS, B_SWEEP), lambda b, n, i: (b, n, 0, i))],
        out_specs=pl.BlockSpec((1, B_SWEEP, B_GROUP * dh), lambda b, n, i: (b, i, n)),
        scratch_shapes=[pltpu.VMEM((vrows, B_GROUP * B_SWEEP), F32),
                        pltpu.VMEM((B_SWEEP, B_GROUP * B_SWEEP), BF16),
                        pltpu.VMEM((B_SWEEP, B_GROUP * B_SWEEP), BF16)],
        compiler_params=_params("parallel", "parallel", "arbitrary"),
        name="b_sparse",
    )(qt, ka, vs, wk, vw, sel, bias_s, bias_w, oc, gate)


def _mixer_b(x, rel_bias, norm1, w_in, q_gain, k_gain, cmp_pos, cmp_w1, cmp_w2, w_out, batch, seq):
    d = w_in.shape[0]
    qkv_width = w_in.shape[1] - 3 * N_HEADS
    gate_w = w_in[:, qkv_width:].reshape(d, B_KV_HEADS, 3 * B_GROUP)
    gate_w = jnp.pad(gate_w, ((0, 0), (0, 0), (0, B_GATE_ROWS - 3 * B_GROUP))).reshape(d, B_KV_HEADS * B_GATE_ROWS)
    w_pad = jnp.concatenate([w_in[:, :qkv_width], gate_w], axis=1)
    w_pad = jnp.pad(w_pad, ((0, 0), (0, B_PROJ_WIDTH - w_pad.shape[1]))).astype(BF16)
    proj = _norm_matmul(x, norm1, w_pad, BF16, 512).reshape(batch, seq, B_PROJ_WIDTH)
    qt, ck, cv, ka, vs, wk, vw, gate_t = _b_prep(proj, q_gain, k_gain, batch, seq)
    kc, vc = _b_compress(ck, cv, cmp_pos, cmp_w1, cmp_w2, k_gain[0], batch, seq)
    last_tile = seq // B_SWEEP - 1
    bias_c = _bias_tiles(rel_bias, 1, seq // B_CMP_STRIDE + last_tile * (B_SWEEP // B_CMP_STRIDE), B_SWEEP,
                         base=1 - B_CMP_LEN + last_tile * B_SWEEP, tile_step=0, row_step=-B_CMP_STRIDE,
                         col_step=1, dmax=1 << 30)
    oc, sel = _b_cmp_attn(qt, kc, vc, bias_c, batch, seq)
    delta_max = min(seq // B_SWEEP - 1, -(-(_THRESHOLDS[-1] + B_SWEEP - 1) // B_SWEEP))
    bias_s = _bias_tiles(rel_bias, delta_max + 2, B_SWEEP, B_SWEEP, base=-B_SWEEP, tile_step=B_SWEEP,
                         row_step=-1, col_step=1, dmax=1 << 30, dtype=BF16)
    win_tiles = (B_WINDOW - 1 + B_SWEEP - 1) // B_SWEEP + 1
    bias_w = _bias_tiles(rel_bias, win_tiles + 1, B_SWEEP, B_SWEEP, base=-B_SWEEP, tile_step=B_SWEEP,
                         row_step=-1, col_step=1, dmax=B_WINDOW - 1, dtype=BF16)
    o = _b_sparse(qt, ka, vs, wk, vw, sel, bias_s, bias_w, oc, gate_t, batch, seq)
    return _matmul_residual(o.reshape(batch * seq, -1), w_out.astype(BF16), x)


def _c_conv_kernel(cur_ref, halo_ref, w_ref, sm_ref, alog_ref, dtb_ref, qkv_ref, bg_ref, xe_ref):
    ts = cur_ref.shape[1]
    keep = jnp.where(pl.program_id(1) == 0, 0.0, 1.0)
    dk = C_HEAD_DIM
    for c in range(3 * C_HEADS):
        sl = slice(c * dk, (c + 1) * dk)
        xe_ref[c, :8, :] = halo_ref[0, :, sl].astype(F32) * keep
        xe_ref[c, 8:, :] = cur_ref[0, :, sl].astype(F32)
        y = None
        for j in range(C_CONV):
            off = 8 - (C_CONV - 1) + j
            term = w_ref[j:j + 1, sl] * xe_ref[c, off:off + ts, :]
            y = term if y is None else y + term
        y = y * jax.nn.sigmoid(y)
        if c < 2 * C_HEADS:
            y = y * lax.rsqrt(jnp.sum(y * y, axis=-1, keepdims=True) + RMS_EPS)
        if c < C_HEADS:
            y = y * (dk ** -0.5)
        qkv_ref[0, :, sl] = y.astype(qkv_ref.dtype)
    sm = sm_ref[0]
    a = sm + dtb_ref[...]
    softplus = jnp.maximum(a, 0.0) + jnp.log1p(jnp.exp(-jnp.abs(a)))
    g = -jnp.exp(alog_ref[...]) * softplus
    lane = lax.broadcasted_iota(jnp.int32, sm.shape, 1)
    bg_ref[0] = jnp.where(lane < C_HEADS, jax.nn.sigmoid(sm), g)


def _c_conv(proj, small, conv_w, a_log, dt_bias, batch, seq):
    ts = 256
    width = 3 * C_WIDTH
    pad = lambda v: jnp.pad(v, (C_HEADS, 128 - 2 * C_HEADS)).reshape(1, 128)
    return pl.pallas_call(
        _c_conv_kernel,
        out_shape=(jax.ShapeDtypeStruct((batch, seq, width), BF16),
                   jax.ShapeDtypeStruct((batch, seq, 128), F32)),
        grid=(batch, seq // ts),
        in_specs=[pl.BlockSpec((1, ts, width), lambda b, i: (b, i, 0)),
                  pl.BlockSpec((1, 8, width), lambda b, i: (b, jnp.maximum(i * (ts // 8) - 1, 0), 0)),
                  pl.BlockSpec((C_CONV, width), lambda b, i: (0, 0)),
                  pl.BlockSpec((1, ts, 128), lambda b, i: (b, i, 0)),
                  pl.BlockSpec((1, 128), lambda b, i: (0, 0)),
                  pl.BlockSpec((1, 128), lambda b, i: (0, 0))],
        out_specs=(pl.BlockSpec((1, ts, width), lambda b, i: (b, i, 0)),
                   pl.BlockSpec((1, ts, 128), lambda b, i: (b, i, 0))),
        scratch_shapes=[pltpu.VMEM((3 * C_HEADS, ts + 8, C_HEAD_DIM), F32)],
        compiler_params=_params("parallel", "arbitrary"),
        name="c_conv",
    )(proj, proj, conv_w, small, pad(a_log), pad(dt_bias))


def _sum3(x, fn):
    hi = x.astype(BF16)
    r = x - hi.astype(F32)
    mid = r.astype(BF16)
    lo = (r - mid.astype(F32)).astype(BF16)
    return fn(hi) + (fn(mid) + fn(lo))


def _c_chunk_kernel(qkv_ref, bg_ref, bgt_ref, tri_ref, trit_ref, blk_ref, u_ref, w_ref, qg_ref, kg_ref, attn_ref,
                    gc_ref):
    cs = C_CHUNK
    dk = C_HEAD_DIM
    gs = C_GROUP * cs
    row = lax.broadcasted_iota(jnp.int32, (gs, gs), 0)
    col = lax.broadcasted_iota(jnp.int32, (gs, gs), 1)
    same = (row // cs) == (col // cs)
    causal = same & (row >= col)
    strict = same & (row > col)
    eye = jnp.where(row == col, 1.0, 0.0)

    bgc = bg_ref[0]
    tri = tri_ref[...]
    gcum_col = _sum3(bgc, lambda p: _dot(tri, p))
    glast_col = _sum3(bgc, lambda p: _dot(blk_ref[...], p))
    gcum_row = _sum3(bgt_ref[0], lambda p: _dot(p, trit_ref[...]))
    gc_ref[0] = gcum_col
    t_mats, powers = [], []
    for h in range(C_HEADS):
        gc = gcum_col[:, C_HEADS + h:C_HEADS + h + 1]
        gr = gcum_row[C_HEADS + h:C_HEADS + h + 1, :]
        q = qkv_ref[0, :, h * dk:(h + 1) * dk]
        k = qkv_ref[0, :, C_WIDTH + h * dk:C_WIDTH + (h + 1) * dk]
        decay = jnp.exp(jnp.where(causal, gc - gr, NEG_INF))
        k16 = k.astype(BF16)
        low = jnp.where(strict, _dot_nt((k * bgc[:, h:h + 1]).astype(BF16), k16) * decay, 0.0)
        t_mats.append(eye - low)
        powers.append(low.astype(BF16))
        attn = jnp.where(causal, _dot_nt(q.astype(BF16), k16), 0.0) * decay
        attn_ref[0, :, h * gs:(h + 1) * gs] = attn.astype(attn_ref.dtype)
        qg_ref[0, :, h * dk:(h + 1) * dk] = (q * jnp.exp(gc)).astype(qg_ref.dtype)
        glast = glast_col[:, C_HEADS + h:C_HEADS + h + 1]
        kg_ref[0, :, h * dk:(h + 1) * dk] = (k * jnp.exp(glast - gc)).astype(kg_ref.dtype)
    for _ in range(int(math.log2(cs)) - 1):
        powers = [_dot(p, p).astype(BF16) for p in powers]
        t_mats = [t + _dot(t.astype(BF16), p) for t, p in zip(t_mats, powers)]
    for h in range(C_HEADS):
        gc = gcum_col[:, C_HEADS + h:C_HEADS + h + 1]
        beta = bgc[:, h:h + 1]
        k = qkv_ref[0, :, C_WIDTH + h * dk:C_WIDTH + (h + 1) * dk]
        v = qkv_ref[0, :, 2 * C_WIDTH + h * dk:2 * C_WIDTH + (h + 1) * dk]
        t16 = t_mats[h].astype(BF16)
        u_ref[0, :, h * dk:(h + 1) * dk] = _dot(t16, (v * beta).astype(BF16)).astype(u_ref.dtype)
        w_ref[0, :, h * dk:(h + 1) * dk] = _dot(t16, (k * beta * jnp.exp(gc)).astype(BF16)).astype(w_ref.dtype)


def _c_chunks(qkv, bg, bgt, batch, seq):
    gs = C_GROUP * C_CHUNK
    idx = np.arange(gs)
    same = (idx[:, None] // C_CHUNK) == (idx[None, :] // C_CHUNK)
    tri = (same & (idx[:, None] >= idx[None, :])).astype(np.float32)
    wide = lambda width: pl.BlockSpec((1, gs, width), lambda b, i: (b, i, 0))
    shape = lambda width, dtype: jax.ShapeDtypeStruct((batch, seq, width), dtype)
    const = pl.BlockSpec((gs, gs), lambda b, i: (0, 0))
    return pl.pallas_call(
        _c_chunk_kernel,
        out_shape=(shape(C_WIDTH, BF16), shape(C_WIDTH, BF16), shape(C_WIDTH, BF16), shape(C_WIDTH, BF16),
                   shape(C_HEADS * gs, BF16), shape(128, F32)),
        grid=(batch, seq // gs),
        in_specs=[wide(3 * C_WIDTH), wide(128),
                  pl.BlockSpec((1, 2 * C_HEADS, gs), lambda b, i: (b, 0, i)),
                  const, const, const],
        out_specs=(wide(C_WIDTH),) * 4 + (wide(C_HEADS * gs), wide(128)),
        compiler_params=_params("parallel", "parallel"),
        name="c_chunks",
    )(qkv, bg, bgt, jnp.asarray(tri, BF16), jnp.asarray(tri.T, BF16), jnp.asarray(same, BF16))


def _c_scan_kernel(u_ref, w_ref, qg_ref, kg_ref, attn_ref, gc_ref, o_ref, state_ref, vnew_ref):
    @pl.when(pl.program_id(1) == 0)
    def _():
        state_ref[...] = jnp.zeros_like(state_ref)

    cs = C_CHUNK
    dk = C_HEAD_DIM
    gs = C_GROUP * cs
    vnew_ref[...] = jnp.zeros_like(vnew_ref)
    heads = range(C_HEADS)
    cols = [slice(h * dk, (h + 1) * dk) for h in heads]
    states = [state_ref[h] for h in heads]
    for c in range(C_GROUP):
        rs = slice(c * cs, (c + 1) * cs)
        decay_last = jnp.exp(gc_ref[0, (c + 1) * cs - 1:(c + 1) * cs, :])
        both = [_dot(jnp.concatenate([w_ref[0, rs, cols[h]], qg_ref[0, rs, cols[h]]], axis=0),
                     states[h].astype(BF16)) for h in heads]
        v16 = [(u_ref[0, rs, cols[h]] - both[h][:cs]).astype(BF16) for h in heads]
        for h in heads:
            vnew_ref[h, rs, :] = v16[h]
        for h in heads:
            out = both[h][cs:] + _dot(attn_ref[0, rs, h * gs:(h + 1) * gs], vnew_ref[h])
            o_ref[0, rs, cols[h]] = out.astype(o_ref.dtype)
        states = [states[h] * decay_last[:, C_HEADS + h:C_HEADS + h + 1] + _dot_tn(kg_ref[0, rs, cols[h]], v16[h])
                  for h in heads]
    for h in heads:
        state_ref[h] = states[h]


def _c_scan(u, w, qg, kg, attn, gc, batch, seq):
    gs = C_GROUP * C_CHUNK
    wide = lambda width: pl.BlockSpec((1, gs, width), lambda b, c: (b, c, 0))
    return pl.pallas_call(
        _c_scan_kernel,
        out_shape=jax.ShapeDtypeStruct((batch, seq, C_WIDTH), BF16),
        grid=(batch, seq // gs),
        in_specs=[wide(C_WIDTH)] * 4 + [wide(C_HEADS * gs), wide(128)],
        out_specs=wide(C_WIDTH),
        scratch_shapes=[pltpu.VMEM((C_HEADS, C_HEAD_DIM, C_HEAD_DIM), F32),
                        pltpu.VMEM((C_HEADS, gs, C_HEAD_DIM), BF16)],
        compiler_params=_params("parallel", "arbitrary"),
        name="c_scan",
    )(u, w, qg, kg, attn, gc)


def _c_out_kernel(o_ref, z_ref, g_ref, w_ref, x_ref, out_ref):
    dk = C_HEAD_DIM
    parts = []
    for h in range(C_HEADS):
        sl = slice(h * dk, (h + 1) * dk)
        z = z_ref[:, sl].astype(F32)
        parts.append((_rms(o_ref[:, sl].astype(F32), g_ref[...]) * (z * jax.nn.sigmoid(z))).astype(BF16))
    out_ref[...] = x_ref[...] + _dot(jnp.concatenate(parts, axis=-1), w_ref[...])


def _c_out(o, proj, out_gain, w_out, x):
    m, d = x.shape
    z_block = (3 * C_WIDTH) // C_WIDTH
    row = lambda width: pl.BlockSpec((ROW_TILE, width), lambda i: (i, 0))
    return pl.pallas_call(
        _c_out_kernel,
        out_shape=jax.ShapeDtypeStruct((m, d), F32),
        grid=(m // ROW_TILE,),
        in_specs=[row(C_WIDTH),
                  pl.BlockSpec((ROW_TILE, C_WIDTH), lambda i: (i, z_block)),
                  pl.BlockSpec((1, C_HEAD_DIM), lambda i: (0, 0)),
                  pl.BlockSpec((C_WIDTH, d), lambda i: (0, 0)),
                  row(d)],
        out_specs=row(d),
        compiler_params=_params("parallel"),
        name="c_out",
    )(o, proj, out_gain.reshape(1, -1), w_out, x)


def _mixer_c(x, norm1, w_in, conv_w, a_log, dt_bias, out_gain, w_out, batch, seq):
    main = 4 * C_WIDTH
    proj = _norm_matmul(x, norm1, w_in[:, :main].astype(BF16), BF16, 512)
    w_small = jnp.pad(w_in[:, main:], ((0, 0), (0, 128 - 2 * C_HEADS))).astype(BF16)
    small = _norm_matmul(x, norm1, w_small, F32, 128)
    qkv, bg = _c_conv(proj.reshape(batch, seq, main), small.reshape(batch, seq, 128), conv_w, a_log, dt_bias,
                      batch, seq)
    bgt = bg[:, :, :2 * C_HEADS].transpose(0, 2, 1)
    u, w, qg, kg, attn, gc = _c_chunks(qkv, bg, bgt, batch, seq)
    o = _c_scan(u, w, qg, kg, attn, gc, batch, seq)
    return _c_out(o.reshape(batch * seq, C_WIDTH), proj, out_gain, w_out.astype(BF16), x)


def kernel(x, rel_bias, l0_norm1, l0_a_w_in, l0_a_q_gain, l0_a_k_gain, l0_a_w_out, l0_norm2, l0_ffn_w_gate, l0_ffn_w_up, l0_ffn_w_down, l1_norm1, l1_b_w_in, l1_b_q_gain, l1_b_k_gain, l1_b_cmp_pos, l1_b_cmp_w1, l1_b_cmp_w2, l1_b_w_out, l1_norm2, l1_ffn_w_gate, l1_ffn_w_up, l1_ffn_w_down, l2_norm1, l2_c_w_in, l2_c_conv_w, l2_c_a_log, l2_c_dt_bias, l2_c_out_gain, l2_c_w_out, l2_norm2, l2_ffn_w_gate, l2_ffn_w_up, l2_ffn_w_down, l3_norm1, l3_a_w_in, l3_a_q_gain, l3_a_k_gain, l3_a_w_out, l3_norm2, l3_ffn_w_gate, l3_ffn_w_up, l3_ffn_w_down):
    batch, seq, d = x.shape
    h = x.reshape(batch * seq, d)

    def ffn(h, norm2, w_gate, w_up, w_down):
        return _ffn(h, norm2, w_gate.astype(BF16), w_up.astype(BF16), w_down.astype(BF16))

    a_biases = _a_biases(rel_bias)
    h = _mixer_a(h, a_biases, l0_norm1, l0_a_w_in, l0_a_q_gain, l0_a_k_gain, l0_a_w_out, batch, seq)
    h = ffn(h, l0_norm2, l0_ffn_w_gate, l0_ffn_w_up, l0_ffn_w_down)
    h = _mixer_b(h, rel_bias, l1_norm1, l1_b_w_in, l1_b_q_gain, l1_b_k_gain, l1_b_cmp_pos, l1_b_cmp_w1,
                 l1_b_cmp_w2, l1_b_w_out, batch, seq)
    h = ffn(h, l1_norm2, l1_ffn_w_gate, l1_ffn_w_up, l1_ffn_w_down)
    h = _mixer_c(h, l2_norm1, l2_c_w_in, l2_c_conv_w, l2_c_a_log, l2_c_dt_bias, l2_c_out_gain, l2_c_w_out,
                 batch, seq)
    h = ffn(h, l2_norm2, l2_ffn_w_gate, l2_ffn_w_up, l2_ffn_w_down)
    h = _mixer_a(h, a_biases, l3_norm1, l3_a_w_in, l3_a_q_gain, l3_a_k_gain, l3_a_w_out, batch, seq)
    h = ffn(h, l3_norm2, l3_ffn_w_gate, l3_ffn_w_up, l3_ffn_w_down)
    return h.reshape(batch, seq, d)
```

```python
import functools
import math

import numpy as np
import jax
import jax.numpy as jnp
from jax import lax
from jax.experimental import pallas as pl
from jax.experimental.pallas import tpu as pltpu

D_MODEL = 1024
RMS_EPS = 1e-6
NEG_INF = -1e30
TINY = 1e-30
FORCE_SCORE = 1e9

N_BUCKETS = 32
REL_MAX_DISTANCE = 2048
N_HEADS = 16

A_GROUPS = ((128, 1), (512, 4), (2048, 16))
A_HEAD_DIM = 64
A_Q_BLOCK = 128
A_PROJ_TILE = 512
A_STAT_WIDTH = 256
A_BLOCKS_PER_STEP = 8

B_KV_HEADS = 4
B_GROUP = 4
B_HEAD_DIM = 64
B_CMP_LEN = 32
B_CMP_STRIDE = 16
B_CMP_HIDDEN = 256
B_SEL_BLOCK = 64
B_TOP_N = 16
B_WINDOW = 512
B_TILE = 128
B_SWEEP = 256
B_PROJ_WIDTH = 3072
B_GATE_ROWS = 16

C_HEADS = 8
C_HEAD_DIM = 128
C_WIDTH = C_HEADS * C_HEAD_DIM
C_CONV = 4
C_CHUNK = 64
C_GROUP = 4

FFN_HIDDEN = 2816
FFN_TILE = 1024

ROW_TILE = 512
VMEM_LIMIT = 48 * 1024 * 1024

LOG2E = math.log2(math.e)

F32 = jnp.float32
BF16 = jnp.bfloat16

NT_DIMS = (((1,), (1,)), ((), ()))
TN_DIMS = (((0,), (0,)), ((), ()))


def _params(*semantics):
    return pltpu.CompilerParams(dimension_semantics=semantics, vmem_limit_bytes=VMEM_LIMIT)


def _dot(a, b):
    return jnp.dot(a, b, preferred_element_type=F32)


def _dot_nt(a, b):
    return lax.dot_general(a, b, NT_DIMS, preferred_element_type=F32)


def _dot_tn(a, b):
    return lax.dot_general(a, b, TN_DIMS, preferred_element_type=F32)


def _rms(x, gain):
    return x * lax.rsqrt(jnp.mean(x * x, axis=-1, keepdims=True) + RMS_EPS) * gain


def _bucket_thresholds():
    d = np.arange(1 << 15)
    max_exact = N_BUCKETS // 2
    d_f = np.maximum(d, 1).astype(np.float32)
    large = max_exact + (np.log(d_f / np.float32(max_exact)) / np.float32(math.log(REL_MAX_DISTANCE / max_exact))
                         * np.float32(N_BUCKETS - max_exact)).astype(np.int32)
    bucket = np.where(d < max_exact, d, np.minimum(large, N_BUCKETS - 1))
    return [int(np.argmax(bucket >= k)) if np.any(bucket >= k) else int(1 << 30) for k in range(N_BUCKETS)]


_THRESHOLDS = _bucket_thresholds()


def _bias_tile_kernel(tbl_ref, o_ref, *, base, tile_step, row_step, col_step, dmax, dil):
    h = pl.program_id(0)
    t = pl.program_id(1)
    shape = o_ref.shape[2:]
    i = lax.broadcasted_iota(jnp.int32, shape, 0)
    j = lax.broadcasted_iota(jnp.int32, shape, 1)
    corner = base + tile_step * t
    dist = corner + row_step * i + col_step * j
    d = dist * dil
    spans = (row_step * (shape[0] - 1), col_step * (shape[1] - 1))
    d_lo = (corner + sum(min(0, s) for s in spans)) * dil
    d_hi = (corner + sum(max(0, s) for s in spans)) * dil
    start = sum((d_lo >= _THRESHOLDS[k]).astype(jnp.int32) for k in range(1, N_BUCKETS))
    o_ref[0, 0] = jnp.full(shape, tbl_ref[start, h] * LOG2E, F32).astype(o_ref.dtype)
    for k in range(1, N_BUCKETS):
        @pl.when((_THRESHOLDS[k] > d_lo) & (_THRESHOLDS[k] <= d_hi))
        def _(k=k):
            value = (tbl_ref[k, h] * LOG2E).astype(o_ref.dtype)
            o_ref[0, 0] = jnp.where(d >= _THRESHOLDS[k], value, o_ref[0, 0])
    valid = (dist >= 0) & (dist <= dmax)
    o_ref[0, 0] = jnp.where(valid, o_ref[0, 0], NEG_INF).astype(o_ref.dtype)


def _bias_tiles(rel_bias, n_tiles, rows, cols, *, base, tile_step, row_step, col_step, dmax, dil=1, dtype=F32):
    kern = functools.partial(_bias_tile_kernel, base=base, tile_step=tile_step, row_step=row_step,
                             col_step=col_step, dmax=dmax, dil=dil)
    return pl.pallas_call(
        kern,
        out_shape=jax.ShapeDtypeStruct((N_HEADS, n_tiles, rows, cols), dtype),
        grid=(N_HEADS, n_tiles),
        in_specs=[pl.BlockSpec(memory_space=pltpu.SMEM)],
        out_specs=pl.BlockSpec((1, 1, rows, cols), lambda h, t: (h, t, 0, 0)),
        compiler_params=_params("parallel", "parallel"),
        name="bias_tiles",
    )(rel_bias)


def _resident(shape):
    return pl.BlockSpec(shape, lambda i: (0,) * len(shape), pipeline_mode=pl.Buffered(1))


def _norm_matmul_kernel(x_ref, g_ref, w_ref, o_ref, *, tn):
    h = _rms(x_ref[...], g_ref[...]).astype(BF16)
    for j in range(w_ref.shape[1] // tn):
        o_ref[:, j * tn:(j + 1) * tn] = _dot(h, w_ref[:, j * tn:(j + 1) * tn]).astype(o_ref.dtype)


def _norm_matmul(x, gain, w, out_dtype, tn):
    m, d = x.shape
    n = w.shape[1]
    return pl.pallas_call(
        functools.partial(_norm_matmul_kernel, tn=tn),
        out_shape=jax.ShapeDtypeStruct((m, n), out_dtype),
        grid=(m // ROW_TILE,),
        in_specs=[pl.BlockSpec((ROW_TILE, d), lambda i: (i, 0)),
                  _resident((1, d)),
                  _resident((d, n))],
        out_specs=pl.BlockSpec((ROW_TILE, n), lambda i: (i, 0)),
        compiler_params=_params("parallel"),
        name="norm_matmul",
    )(x, gain.reshape(1, d), w)


def _matmul_residual_kernel(a_ref, w_ref, x_ref, o_ref):
    o_ref[...] = x_ref[...] + _dot(a_ref[...], w_ref[...])


def _matmul_residual(a, w, x):
    m, k = a.shape
    d = w.shape[1]
    return pl.pallas_call(
        _matmul_residual_kernel,
        out_shape=jax.ShapeDtypeStruct((m, d), F32),
        grid=(m // ROW_TILE,),
        in_specs=[pl.BlockSpec((ROW_TILE, k), lambda i: (i, 0)),
                  pl.BlockSpec((k, d), lambda i: (0, 0)),
                  pl.BlockSpec((ROW_TILE, d), lambda i: (i, 0))],
        out_specs=pl.BlockSpec((ROW_TILE, d), lambda i: (i, 0)),
        compiler_params=_params("parallel"),
        name="matmul_residual",
    )(a, w, x)


def _ffn_kernel(x_ref, g_ref, wg_ref, wu_ref, wd_ref, o_ref):
    x = x_ref[...]
    h = _rms(x, g_ref[...]).astype(BF16)
    hidden = wg_ref.shape[1]
    acc = x
    for lo in range(0, hidden, FFN_TILE):
        hi = min(lo + FFN_TILE, hidden)
        a = _dot(h, wg_ref[:, lo:hi])
        b = _dot(h, wu_ref[:, lo:hi])
        acc = acc + _dot((a * jax.nn.sigmoid(a) * b).astype(BF16), wd_ref[lo:hi, :])
    o_ref[...] = acc


def _ffn(x, gain, w_gate, w_up, w_down):
    m, d = x.shape
    hidden = w_gate.shape[1]
    return pl.pallas_call(
        _ffn_kernel,
        out_shape=jax.ShapeDtypeStruct((m, d), F32),
        grid=(m // ROW_TILE,),
        in_specs=[pl.BlockSpec((ROW_TILE, d), lambda i: (i, 0)),
                  _resident((1, d)),
                  _resident((d, hidden)), _resident((d, hidden)), _resident((hidden, d))],
        out_specs=pl.BlockSpec((ROW_TILE, d), lambda i: (i, 0)),
        compiler_params=_params("parallel"),
        name="ffn",
    )(x, gain.reshape(1, d), w_gate, w_up, w_down)


def _a_proj_kernel(x_ref, g_ref, w_ref, qg_ref, kg_ref, o_ref, h_ref, x_scr, *, dil):
    rows = ROW_TILE // dil
    xn = _rms(x_ref[...], g_ref[...])
    if dil == 1:
        h_ref[...] = xn.astype(BF16)
    else:
        slabs = xn.shape[1] // 128
        for c in range(slabs):
            x_scr[c] = xn[:, c * 128:(c + 1) * 128]
        for r in range(dil):
            picked = [x_scr[c, pl.ds(r, rows, stride=dil), :] for c in range(slabs)]
            h_ref[r * rows:(r + 1) * rows, :] = jnp.concatenate(picked, axis=1).astype(BF16)
    h = h_ref[...]
    width = w_ref.shape[1]
    hd = width // 3
    low = lax.broadcasted_iota(jnp.int32, (ROW_TILE, 128), 1) < A_HEAD_DIM
    for j in range(width // A_PROJ_TILE):
        res = _dot(h, w_ref[:, j * A_PROJ_TILE:(j + 1) * A_PROJ_TILE])
        kind = (j * A_PROJ_TILE) // hd
        if kind < 2:
            parts = []
            for c in range(A_PROJ_TILE // 128):
                y = res[:, c * 128:(c + 1) * 128]
                sq = y * y
                tot = jnp.sum(sq, axis=-1, keepdims=True)
                lo = jnp.sum(jnp.where(low, sq, 0.0), axis=-1, keepdims=True)
                ss = jnp.where(low, lo, tot - lo)
                parts.append(y * lax.rsqrt(ss * (1.0 / A_HEAD_DIM) + RMS_EPS))
            res = jnp.concatenate(parts, axis=1) * (qg_ref if kind == 0 else kg_ref)[...]
        res = res.astype(BF16)
        for r in range(dil):
            off = r * width + j * A_PROJ_TILE
            o_ref[:, off:off + A_PROJ_TILE] = res[r * rows:(r + 1) * rows]


def _a_proj(x, gain, w, q_gain, k_gain, dil):
    m, d = x.shape
    width = w.shape[1]
    reps = A_PROJ_TILE // A_HEAD_DIM
    qg = jnp.tile(q_gain * (A_HEAD_DIM ** -0.5 * LOG2E), reps).reshape(1, A_PROJ_TILE)
    kg = jnp.tile(k_gain, reps).reshape(1, A_PROJ_TILE)
    return pl.pallas_call(
        functools.partial(_a_proj_kernel, dil=dil),
        out_shape=jax.ShapeDtypeStruct((m // dil, dil * width), BF16),
        grid=(m // ROW_TILE,),
        in_specs=[pl.BlockSpec((ROW_TILE, d), lambda i: (i, 0)),
                  pl.BlockSpec((1, d), lambda i: (0, 0)),
                  pl.BlockSpec((d, width), lambda i: (0, 0)),
                  pl.BlockSpec((1, A_PROJ_TILE), lambda i: (0, 0)),
                  pl.BlockSpec((1, A_PROJ_TILE), lambda i: (0, 0))],
        out_specs=pl.BlockSpec((ROW_TILE // dil, dil * width), lambda i: (i, 0)),
        scratch_shapes=[pltpu.VMEM((ROW_TILE, d), BF16), pltpu.VMEM((d // 128, ROW_TILE, 128), F32)],
        compiler_params=_params("parallel"),
        name="a_proj",
    )(x, gain.reshape(1, d), w, qg, kg)


def _a_attn_kernel(cur_ref, prev_ref, bias_ref, o_ref, stat_ref):
    first = (pl.program_id(2) == 0).astype(jnp.int32)
    nq = A_Q_BLOCK
    hd = N_HEADS * A_HEAD_DIM
    lane = lax.broadcasted_iota(jnp.int32, (nq, 128), 1)
    low = lane < A_HEAD_DIM
    ones = jnp.ones((2 * nq, 128), BF16)
    for res in range(o_ref.shape[2] // hd):
        for sub in range(cur_ref.shape[1] // nq):
            rows = slice(sub * nq, (sub + 1) * nq)
            max_tile = jnp.zeros((nq, 128), F32)
            den_tile = jnp.ones((nq, 128), F32)
            for pair in range(N_HEADS // 2):
                qc, kc, vc = (slice(res * 3 * hd + part * hd + pair * 128, res * 3 * hd + part * hd + (pair + 1) * 128)
                              for part in range(3))
                q = cur_ref[0, rows, qc]
                zero = jnp.zeros_like(q)
                qq = jnp.concatenate([jnp.where(low, q, zero), jnp.where(low, zero, q)], axis=0)
                if sub == 0:
                    kk = jnp.concatenate([prev_ref[0, :, kc], cur_ref[0, :nq, kc]], axis=0)
                    vv = jnp.concatenate([prev_ref[0, :, vc], cur_ref[0, :nq, vc]], axis=0)
                    base = 2 * pair + N_HEADS * first
                else:
                    kk = cur_ref[0, (sub - 1) * nq:(sub + 1) * nq, kc]
                    vv = cur_ref[0, (sub - 1) * nq:(sub + 1) * nq, vc]
                    base = 2 * pair
                s = _dot_nt(qq, kk) + jnp.concatenate([bias_ref[base], bias_ref[base + 1]], axis=0)
                m = jnp.max(s, axis=-1, keepdims=True)
                acc = _dot(jnp.exp2(s - m).astype(BF16), jnp.concatenate([vv, ones], axis=1))
                oc = slice(res * hd + pair * 128, res * hd + (pair + 1) * 128)
                o_ref[0, rows, oc] = jnp.where(low, acc[:nq, :128], acc[nq:, :128]).astype(o_ref.dtype)
                first_head = lane == 2 * pair
                second_head = lane == 2 * pair + 1
                max_tile = jnp.where(first_head, m[:nq], jnp.where(second_head, m[nq:], max_tile))
                den_tile = jnp.where(first_head, acc[:nq, 128:], jnp.where(second_head, acc[nq:, 128:], den_tile))
            stat_ref[0, rows, res * A_STAT_WIDTH:res * A_STAT_WIDTH + 128] = max_tile
            stat_ref[0, rows, res * A_STAT_WIDTH + 128:(res + 1) * A_STAT_WIDTH] = den_tile


def _a_attention(proj, bias, dil, batch, seq):
    length = seq // dil
    nblk = length // A_Q_BLOCK
    hd = N_HEADS * A_HEAD_DIM
    pv = proj.reshape(batch, length, dil * 3 * hd)

    per = min(A_BLOCKS_PER_STEP, nblk)
    res = min(A_BLOCKS_PER_STEP // per, dil)
    step_rows = per * A_Q_BLOCK
    assert nblk % per == 0 and dil % res == 0

    o, stats = pl.pallas_call(
        _a_attn_kernel,
        out_shape=(jax.ShapeDtypeStruct((batch, length, dil * hd), BF16),
                   jax.ShapeDtypeStruct((batch, length, dil * A_STAT_WIDTH), F32)),
        grid=(batch, dil // res, nblk // per),
        in_specs=[pl.BlockSpec((1, step_rows, res * 3 * hd), lambda b, r, i: (b, i, r)),
                  pl.BlockSpec((1, A_Q_BLOCK, res * 3 * hd), lambda b, r, i: (b, jnp.maximum(per * i - 1, 0), r)),
                  pl.BlockSpec((2 * N_HEADS, A_Q_BLOCK, 2 * A_Q_BLOCK), lambda b, r, i: (0, 0, 0))],
        out_specs=(pl.BlockSpec((1, step_rows, res * hd), lambda b, r, i: (b, i, r)),
                   pl.BlockSpec((1, step_rows, res * A_STAT_WIDTH), lambda b, r, i: (b, i, r))),
        compiler_params=_params("parallel", "parallel", "arbitrary"),
        name="a_attention",
    )(pv, pv, bias)
    return o.reshape(batch * length, dil * hd), stats.reshape(batch * length, dil * A_STAT_WIDTH)


def _a_out_kernel(o0_ref, o1_ref, o2_ref, s0_ref, s1_ref, s2_ref, e_ref, w_ref, x_ref, out_ref, o_scr, s_scr):
    hd = N_HEADS * A_HEAD_DIM
    sw = A_STAT_WIDTH
    for g, (o_ref, s_ref) in enumerate(((o0_ref, s0_ref), (o1_ref, s1_ref), (o2_ref, s2_ref))):
        dil = A_GROUPS[g][1]
        rows = ROW_TILE // dil
        for r in range(dil):
            dst = pl.ds(r, rows, stride=dil) if dil > 1 else slice(None)
            s_scr[g, 0, dst, :] = s_ref[:, r * sw:r * sw + 128]
            s_scr[g, 1, dst, :] = s_ref[:, r * sw + 128:(r + 1) * sw]
            for c in range(hd // 128):
                o_scr[g, c, dst, :] = o_ref[:, r * hd + c * 128:r * hd + (c + 1) * 128].astype(F32)
    groups = range(len(A_GROUPS))
    top = functools.reduce(jnp.maximum, [s_scr[g, 0] for g in groups])
    es = [jnp.exp2(s_scr[g, 0] - top) for g in groups]
    inv = 1.0 / sum(es[g] * s_scr[g, 1] for g in groups)
    expand = e_ref[...]
    acc = None
    for g in groups:
        o_g = jnp.concatenate([o_scr[g, c] for c in range(hd // 128)], axis=1)
        term = _dot((es[g] * inv).astype(BF16), expand) * o_g
        acc = term if acc is None else acc + term
    out_ref[...] = x_ref[...] + _dot(acc.astype(BF16), w_ref[...])


def _a_out(outs, stats, w_out, x):
    m, d = x.shape
    hd = N_HEADS * A_HEAD_DIM
    expand = np.zeros((128, hd), np.float32)
    for h in range(N_HEADS):
        expand[h, h * A_HEAD_DIM:(h + 1) * A_HEAD_DIM] = 1.0
    grouped = lambda width: [pl.BlockSpec((ROW_TILE // dil, dil * width), lambda i: (i, 0)) for _, dil in A_GROUPS]
    return pl.pallas_call(
        _a_out_kernel,
        out_shape=jax.ShapeDtypeStruct((m, d), F32),
        grid=(m // ROW_TILE,),
        in_specs=grouped(hd) + grouped(A_STAT_WIDTH) + [
            pl.BlockSpec((128, hd), lambda i: (0, 0)),
            pl.BlockSpec((hd, d), lambda i: (0, 0)),
            pl.BlockSpec((ROW_TILE, d), lambda i: (i, 0))],
        out_specs=pl.BlockSpec((ROW_TILE, d), lambda i: (i, 0)),
        scratch_shapes=[pltpu.VMEM((len(A_GROUPS), hd // 128, ROW_TILE, 128), F32),
                        pltpu.VMEM((len(A_GROUPS), 2, ROW_TILE, 128), F32)],
        compiler_params=_params("parallel"),
        name="a_out",
    )(*outs, *stats, jnp.asarray(expand, BF16), w_out, x)


def _a_biases(rel_bias):
    biases = []
    for window, dil in A_GROUPS:
        steps = window // dil
        assert steps == A_Q_BLOCK
        bias = _bias_tiles(rel_bias, 1, A_Q_BLOCK, 2 * A_Q_BLOCK, base=A_Q_BLOCK, tile_step=0, row_step=1,
                           col_step=-1, dmax=steps, dil=dil)[:, 0]
        biases.append(jnp.concatenate([bias, bias.at[:, :, :A_Q_BLOCK].set(NEG_INF)], axis=0))
    return biases


def _mixer_a(x, biases, norm1, w_in, q_gain, k_gain, w_out, batch, seq):
    w_in = w_in.astype(BF16)
    group_width = 3 * N_HEADS * A_HEAD_DIM
    outs, stats = [], []
    for gi, (_, dil) in enumerate(A_GROUPS):
        assert (seq // dil) % A_Q_BLOCK == 0 and seq % ROW_TILE == 0
        bias = biases[gi]
        proj = _a_proj(x, norm1, w_in[:, gi * group_width:(gi + 1) * group_width], q_gain[gi], k_gain[gi], dil)
        o, stat = _a_attention(proj, bias, dil, batch, seq)
        outs.append(o)
        stats.append(stat)
    return _a_out(outs, stats, w_out.astype(BF16), x)


def _b_prep_kernel(p_ref, qg_ref, kg_ref, qt_ref, ck_ref, cv_ref, ka_ref, vs_ref, wk_ref, vw_ref, gate_ref):
    dh = B_HEAD_DIM
    ts = p_ref.shape[1]
    n_sel = ka_ref.shape[3] - dh
    low = lax.broadcasted_iota(jnp.int32, (ts, 128), 1) < dh

    def slab(c):
        return p_ref[0, :, c * 128:(c + 1) * 128].astype(F32)

    def normed(x, gain):
        sq = x * x
        tot = jnp.sum(sq, axis=-1, keepdims=True)
        first = jnp.sum(jnp.where(low, sq, 0.0), axis=-1, keepdims=True)
        ss = jnp.where(low, first, tot - first)
        return x * lax.rsqrt(ss * (1.0 / dh) + RMS_EPS) * jnp.concatenate([gain, gain], axis=1)

    def halves(x):
        return x[:, :dh], x[:, dh:]

    qg = qg_ref[...] * (dh ** -0.5 * LOG2E)
    for c in range(N_HEADS // 2):
        xt = normed(slab(c), qg).T.astype(BF16)
        qt_ref[0, 2 * c] = xt[:dh]
        qt_ref[0, 2 * c + 1] = xt[dh:]
    base = N_HEADS // 2
    pairs = B_KV_HEADS // 2
    pos = pl.program_id(1) * ts + lax.broadcasted_iota(jnp.int32, (ts, 128), 0)
    lane = lax.broadcasted_iota(jnp.int32, (ts, 128), 1)
    onehot = jnp.where(lane - dh == pos // B_SEL_BLOCK, 1.0, 0.0)
    ones = jnp.ones((vs_ref.shape[2] - dh, ts), BF16)
    for j in range(pairs):
        for ref, off in ((ck_ref, 0), (cv_ref, pairs)):
            a, b = halves(slab(base + off + j))
            ref[0, 2 * j] = a.astype(BF16)
            ref[0, 2 * j + 1] = b.astype(BF16)
        k_sel = normed(slab(base + 2 * pairs + j), kg_ref[1:2, :])
        for n, k in zip((2 * j, 2 * j + 1), (k_sel, pltpu.roll(k_sel, dh, 1))):
            ka_ref[0, n] = jnp.where(low, k, onehot)[:, :dh + n_sel].astype(BF16)
        for n, k in zip((2 * j, 2 * j + 1), halves(normed(slab(base + 4 * pairs + j), kg_ref[2:3, :]))):
            wk_ref[0, n] = k.astype(BF16)
        for ref, off in ((vs_ref, 3 * pairs), (vw_ref, 5 * pairs)):
            xt = slab(base + off + j).T.astype(BF16)
            for n, v in zip((2 * j, 2 * j + 1), (xt[:dh], xt[dh:])):
                ref[0, n, :dh, :] = v
                ref[0, n, dh:, :] = ones
    gate = jax.nn.sigmoid(slab(base + 6 * pairs)).T
    rows = gate_ref.shape[2]
    for n in range(B_KV_HEADS):
        gate_ref[0, n] = gate[n * rows:(n + 1) * rows]


def _b_prep(proj, q_gain, k_gain, batch, seq):
    ts = 256
    dh = B_HEAD_DIM
    n_sel = seq // B_SEL_BLOCK
    assert dh + n_sel <= 128
    rows_shape = lambda width: jax.ShapeDtypeStruct((batch, B_KV_HEADS, seq, width), BF16)
    rows_spec = lambda width: pl.BlockSpec((1, B_KV_HEADS, ts, width), lambda b, i: (b, 0, i, 0))
    cols_shape = lambda heads, height, dtype: jax.ShapeDtypeStruct((batch, heads, height, seq), dtype)
    cols_spec = lambda heads, height: pl.BlockSpec((1, heads, height, ts), lambda b, i: (b, 0, 0, i))
    return pl.pallas_call(
        _b_prep_kernel,
        out_shape=(cols_shape(N_HEADS, dh, BF16), rows_shape(dh), rows_shape(dh), rows_shape(dh + n_sel),
                   cols_shape(B_KV_HEADS, dh + 16, BF16), rows_shape(dh), cols_shape(B_KV_HEADS, dh + 16, BF16),
                   cols_shape(B_KV_HEADS, B_GATE_ROWS, F32)),
        grid=(batch, seq // ts),
        in_specs=[pl.BlockSpec((1, ts, B_PROJ_WIDTH), lambda b, i: (b, i, 0)),
                  pl.BlockSpec((1, dh), lambda b, i: (0, 0)),
                  pl.BlockSpec((3, dh), lambda b, i: (0, 0))],
        out_specs=(cols_spec(N_HEADS, dh), rows_spec(dh), rows_spec(dh), rows_spec(dh + n_sel),
                   cols_spec(B_KV_HEADS, dh + 16), rows_spec(dh), cols_spec(B_KV_HEADS, dh + 16),
                   cols_spec(B_KV_HEADS, B_GATE_ROWS)),
        compiler_params=_params("parallel", "parallel"),
        name="b_prep",
    )(proj, q_gain.reshape(1, dh), k_gain)


def _b_compress_kernel(tk_ref, tv_ref, pos_ref, w1_ref, w2_ref, kg_ref, kc_ref, vc_ref):
    half = (B_CMP_LEN // 2) * B_HEAD_DIM
    for kv, (t_ref, out_ref) in enumerate(((tk_ref, kc_ref), (tv_ref, vc_ref))):
        t = t_ref[0, 0].astype(F32)
        top = (t + pos_ref[kv, 0:1, :]).astype(BF16)
        bot = (t + pos_ref[kv, 1:2, :]).astype(BF16)
        a1 = _dot(top, w1_ref[kv, :half, :])
        a2 = _dot(bot, w1_ref[kv, half:, :])
        hidden = a1 + pltpu.roll(a2, a2.shape[0] - 1, 0)
        out = _dot(jax.nn.gelu(hidden).astype(BF16), w2_ref[kv])
        if kv == 0:
            out = _rms(out, kg_ref[...])
        out_ref[0, 0] = out.astype(out_ref.dtype)


def _b_compress(ck, cv, cmp_pos, cmp_w1, cmp_w2, k_gain0, batch, seq):
    rows = seq // B_CMP_STRIDE
    half = (B_CMP_LEN // 2) * B_HEAD_DIM
    tk = ck.reshape(batch, B_KV_HEADS, rows, half)
    tv = cv.reshape(batch, B_KV_HEADS, rows, half)
    pos = cmp_pos.reshape(2, 2, half)
    t_spec = pl.BlockSpec((1, 1, rows, half), lambda b, n: (b, n, 0, 0))
    o_spec = pl.BlockSpec((1, 1, rows, B_HEAD_DIM), lambda b, n: (b, n, 0, 0))
    shape = jax.ShapeDtypeStruct((batch, B_KV_HEADS, rows, B_HEAD_DIM), BF16)
    return pl.pallas_call(
        _b_compress_kernel,
        out_shape=(shape, shape),
        grid=(batch, B_KV_HEADS),
        in_specs=[t_spec, t_spec,
                  pl.BlockSpec((2, 2, half), lambda b, n: (0, 0, 0)),
                  pl.BlockSpec((2, 2 * half, B_CMP_HIDDEN), lambda b, n: (0, 0, 0)),
                  pl.BlockSpec((2, B_CMP_HIDDEN, B_HEAD_DIM), lambda b, n: (0, 0, 0)),
                  pl.BlockSpec((1, B_HEAD_DIM), lambda b, n: (0, 0))],
        out_specs=(o_spec, o_spec),
        compiler_params=_params("parallel", "parallel"),
        name="b_compress",
    )(tk, tv, pos, cmp_w1.astype(BF16), cmp_w2.astype(BF16), k_gain0.reshape(1, -1))


def _b_cmp_attn_kernel(qt_ref, kc_ref, vct_ref, bias_ref, c2s_ref, oc_ref, sel_ref, imp_ref, *, top_n):
    tq = B_SWEEP
    n_sel = imp_ref.shape[0]
    qt = jnp.concatenate([qt_ref[0, g] for g in range(B_GROUP)], axis=1)
    n_cmp_pad = kc_ref.shape[2]
    shift = tq // B_CMP_STRIDE
    off = pl.multiple_of((pl.num_programs(2) - 1 - pl.program_id(2)) * shift, shift)
    bias = jnp.concatenate([bias_ref[g, 0, pl.ds(off, n_cmp_pad), :] for g in range(B_GROUP)], axis=1)
    s = _dot(kc_ref[0, 0], qt) + bias
    m = jnp.max(s, axis=0, keepdims=True)
    e = jnp.exp2(s - m)
    z = jnp.maximum(jnp.sum(e, axis=0, keepdims=True), TINY)
    pos = pl.program_id(2) * tq + lax.broadcasted_iota(jnp.int32, (1, tq), 1)
    sees_any = jnp.concatenate([pos >= B_CMP_LEN - 1] * B_GROUP, axis=1)
    p = e * jnp.where(sees_any, 1.0 / z, 0.0)
    oct = _dot(vct_ref[0, 0], p.astype(BF16))
    for g in range(B_GROUP):
        oc_ref[0, g] = oct[:, g * tq:(g + 1) * tq]

    p_sum = p[:, 0:tq] + p[:, tq:2 * tq] + p[:, 2 * tq:3 * tq] + p[:, 3 * tq:4 * tq]
    hi = p_sum.astype(BF16)
    lo = (p_sum - hi.astype(F32)).astype(BF16)
    c2s = c2s_ref[...]
    imp = _dot(c2s, hi) + _dot(c2s, lo)

    t = pl.program_id(2) * tq + lax.broadcasted_iota(jnp.int32, (n_sel, tq), 1)
    blk = lax.broadcasted_iota(jnp.int32, (n_sel, tq), 0)
    cur = t // B_SEL_BLOCK
    forced = (blk == 0) | (blk == cur) | (blk == cur - 1)
    imp = jnp.where(forced, FORCE_SCORE, jnp.where(blk * B_SEL_BLOCK <= t, imp, NEG_INF))
    imp_ref[...] = imp

    def count(i, rank):
        row = imp_ref[pl.ds(i, 1), :]
        ahead = jnp.where(row > imp, 1.0, jnp.where(row == imp, jnp.where(blk > i, 1.0, 0.0), 0.0))
        return rank + ahead

    n_live = jnp.minimum(n_sel, (pl.program_id(2) + 1) * (tq // B_SEL_BLOCK))
    rank = lax.fori_loop(0, n_live, count, jnp.zeros((n_sel, tq), F32))
    sel_ref[0, 0] = jnp.where(rank < top_n, 0.0, NEG_INF).astype(sel_ref.dtype)


def _b_cmp_attn(qt, kc, vc, bias_c, batch, seq):
    n_sel = seq // B_SEL_BLOCK
    n_cmp_pad = seq // B_CMP_STRIDE
    n_cmp = (seq - B_CMP_LEN) // B_CMP_STRIDE + 1
    c = np.arange(n_cmp_pad)[None, :] * B_CMP_STRIDE
    j = np.arange(n_sel)[:, None] * B_SEL_BLOCK
    c2s = ((c < j + B_SEL_BLOCK) & (c + B_CMP_LEN > j) & (np.arange(n_cmp_pad)[None, :] < n_cmp)).astype(np.float32)
    kern = functools.partial(_b_cmp_attn_kernel, top_n=min(B_TOP_N, n_sel))
    return pl.pallas_call(
        kern,
        out_shape=(jax.ShapeDtypeStruct((batch, N_HEADS, B_HEAD_DIM, seq), F32),
                   jax.ShapeDtypeStruct((batch, B_KV_HEADS, n_sel, seq), BF16)),
        grid=(batch, B_KV_HEADS, seq // B_SWEEP),
        in_specs=[pl.BlockSpec((1, B_GROUP, B_HEAD_DIM, B_SWEEP), lambda b, n, i: (b, n, 0, i)),
                  pl.BlockSpec((1, 1, n_cmp_pad, B_HEAD_DIM), lambda b, n, i: (b, n, 0, 0)),
                  pl.BlockSpec((1, 1, B_HEAD_DIM, n_cmp_pad), lambda b, n, i: (b, n, 0, 0)),
                  pl.BlockSpec((B_GROUP, 1, bias_c.shape[2], B_SWEEP), lambda b, n, i: (n, 0, 0, 0)),
                  pl.BlockSpec((n_sel, n_cmp_pad), lambda b, n, i: (0, 0))],
        out_specs=(pl.BlockSpec((1, B_GROUP, B_HEAD_DIM, B_SWEEP), lambda b, n, i: (b, n, 0, i)),
                   pl.BlockSpec((1, 1, n_sel, B_SWEEP), lambda b, n, i: (b, n, 0, i))),
        scratch_shapes=[pltpu.VMEM((n_sel, B_SWEEP), F32)],
        compiler_params=_params("parallel", "parallel", "arbitrary"),
        name="b_cmp_attn",
    )(qt, kc, vc.transpose(0, 1, 3, 2), bias_c, jnp.asarray(c2s, BF16))


def _b_sparse_kernel(qt_ref, ka_ref, vs_ref, wk_ref, vw_ref, sel_ref, bs_ref, bw_ref, oc_ref, gate_ref,
                     o_ref, acc_ref, sa_ref, sb_ref, *, delta_max, win_tiles):
    tq = B_SWEEP
    dh = B_HEAD_DIM
    cols = B_GROUP * tq
    qi = pl.program_id(2)
    n_tiles = ka_ref.shape[2] // tq
    qt = jnp.concatenate([qt_ref[0, g] for g in range(B_GROUP)], axis=1)
    q_aug = jnp.concatenate([qt, jnp.concatenate([sel_ref[0, 0]] * B_GROUP, axis=1)], axis=0)

    def tile_start(kt):
        return pl.multiple_of(jnp.clip(kt, 0, n_tiles - 1) * tq, tq)

    def normalised(acc):
        return acc[:dh] * (1.0 / acc[dh:dh + 1])

    def sel_bias(kt):
        d = jnp.clip(qi - kt, -1, delta_max) + 1
        return jnp.concatenate([bs_ref[g, d] for g in range(B_GROUP)], axis=1)

    def sel_scores(kt):
        return _dot(ka_ref[0, 0, pl.ds(tile_start(kt), tq), :], q_aug).astype(BF16) + sel_bias(kt)

    def consume(s_buf, kt, m_old):
        s = s_buf[...]
        m_new = jnp.maximum(m_old, jnp.max(s, axis=0, keepdims=True).astype(F32))
        alpha = jnp.exp2(m_old - m_new)
        p = jnp.exp2(s - m_new.astype(BF16))
        acc_ref[...] = alpha * acc_ref[...] + _dot(vs_ref[0, 0, :, pl.ds(tile_start(kt), tq)], p)
        return m_new

    acc_ref[...] = jnp.zeros(acc_ref.shape, F32)
    sa_ref[...] = sel_scores(0)

    def pair(kt, m):
        sb_ref[...] = sel_scores(kt + 1)
        m = consume(sa_ref, kt, m)
        sa_ref[...] = sel_scores(kt + 2)
        return consume(sb_ref, kt + 1, m)

    n_pairs = (qi + 2) // 2
    n_double = n_pairs // 2
    m = lax.fori_loop(0, n_double, lambda j, m: pair(4 * j + 2, pair(4 * j, m)),
                      jnp.full((1, cols), NEG_INF, F32))
    lax.fori_loop(2 * n_double, n_pairs, lambda j, m: pair(2 * j, m), m)
    o_s = normalised(acc_ref[...])

    tiles = []
    for u in range(win_tiles):
        kt = qi - (win_tiles - 1) + u
        d = jnp.where(kt >= 0, qi - kt, -1) + 1
        bias = jnp.concatenate([bw_ref[g, d] for g in range(B_GROUP)], axis=1)
        tiles.append((_dot(wk_ref[0, 0, pl.ds(tile_start(kt), tq), :], qt).astype(BF16) + bias, kt))
    m = None
    for s, _ in tiles:
        tile_max = jnp.max(s, axis=0, keepdims=True)
        m = tile_max if m is None else jnp.maximum(m, tile_max)
    acc = None
    for s, kt in tiles:
        pv = _dot(vw_ref[0, 0, :, pl.ds(tile_start(kt), tq)], jnp.exp2(s - m))
        acc = pv if acc is None else acc + pv
    o_w = normalised(acc)

    gate = gate_ref[0, 0]
    merged = []
    for g in range(B_GROUP):
        cs = slice(g * tq, (g + 1) * tq)
        merged.append(gate[3 * g:3 * g + 1] * oc_ref[0, g] + gate[3 * g + 1:3 * g + 2] * o_s[:, cs]
                      + gate[3 * g + 2:3 * g + 3] * o_w[:, cs])
    for pair in range(B_GROUP // 2):
        both = jnp.concatenate([merged[2 * pair], merged[2 * pair + 1]], axis=0)
        o_ref[0, :, pair * 2 * dh:(pair + 1) * 2 * dh] = both.T.astype(o_ref.dtype)


def _b_sparse(qt, ka, vs, wk, vw, sel, bias_s, bias_w, oc, gate, batch, seq):
    n_sel = seq // B_SEL_BLOCK
    dh = B_HEAD_DIM
    n_ds = bias_s.shape[1]
    n_dw = bias_w.shape[1]
    vrows = vs.shape[2]
    kern = functools.partial(_b_sparse_kernel, delta_max=n_ds - 2, win_tiles=n_dw - 1)
    whole = lambda rows, width: pl.BlockSpec((1, 1, rows, width), lambda b, n, i: (b, n, 0, 0))
    return pl.pallas_call(
        kern,
        out_shape=jax.ShapeDtypeStruct((batch, seq, N_HEADS * dh), BF16),
        grid=(batch, B_KV_HEADS, seq // B_SWEEP),
        in_specs=[pl.BlockSpec((1, B_GROUP, dh, B_SWEEP), lambda b, n, i: (b, n, 0, i)),
                  whole(seq, dh + n_sel), whole(vrows, seq), whole(seq, dh), whole(vrows, seq),
                  pl.BlockSpec((1, 1, n_sel, B_SWEEP), lambda b, n, i: (b, n, 0, i)),
                  pl.BlockSpec((B_GROUP, n_ds, B_SWEEP, B_SWEEP), lambda b, n, i: (n, 0, 0, 0)),
                  pl.BlockSpec((B_GROUP, n_dw, B_SWEEP, B_SWEEP), lambda b, n, i: (n, 0, 0, 0)),
                  pl.BlockSpec((1, B_GROUP, dh, B_SWEEP), lambda b, n, i: (b, n, 0, i)),
                  pl.BlockSpec((1, 1, B_GATE_ROWS, B_SWEEP), lambda b, n, i: (b, n, 0, i))],
        out_specs=pl.BlockSpec((1, B_SWEEP, B_GROUP * dh), lambda b, n, i: (b, i, n)),
        scratch_shapes=[pltpu.VMEM((vrows, B_GROUP * B_SWEEP), F32),
                        pltpu.VMEM((B_SWEEP, B_GROUP * B_SWEEP), BF16),
                        pltpu.VMEM((B_SWEEP, B_GROUP * B_SWEEP), BF16)],
        compiler_params=_params("parallel", "parallel", "arbitrary"),
        name="b_sparse",
    )(qt, ka, vs, wk, vw, sel, bias_s, bias_w, oc, gate)


def _mixer_b(x, rel_bias, norm1, w_in, q_gain, k_gain, cmp_pos, cmp_w1, cmp_w2, w_out, batch, seq):
    d = w_in.shape[0]
    qkv_width = w_in.shape[1] - 3 * N_HEADS
    gate_w = w_in[:, qkv_width:].reshape(d, B_KV_HEADS, 3 * B_GROUP)
    gate_w = jnp.pad(gate_w, ((0, 0), (0, 0), (0, B_GATE_ROWS - 3 * B_GROUP))).reshape(d, B_KV_HEADS * B_GATE_ROWS)
    w_pad = jnp.concatenate([w_in[:, :qkv_width], gate_w], axis=1)
    w_pad = jnp.pad(w_pad, ((0, 0), (0, B_PROJ_WIDTH - w_pad.shape[1]))).astype(BF16)
    proj = _norm_matmul(x, norm1, w_pad, BF16, 512).reshape(batch, seq, B_PROJ_WIDTH)
    qt, ck, cv, ka, vs, wk, vw, gate_t = _b_prep(proj, q_gain, k_gain, batch, seq)
    kc, vc = _b_compress(ck, cv, cmp_pos, cmp_w1, cmp_w2, k_gain[0], batch, seq)
    last_tile = seq // B_SWEEP - 1
    bias_c = _bias_tiles(rel_bias, 1, seq // B_CMP_STRIDE + last_tile * (B_SWEEP // B_CMP_STRIDE), B_SWEEP,
                         base=1 - B_CMP_LEN + last_tile * B_SWEEP, tile_step=0, row_step=-B_CMP_STRIDE,
                         col_step=1, dmax=1 << 30)
    oc, sel = _b_cmp_attn(qt, kc, vc, bias_c, batch, seq)
    delta_max = min(seq // B_SWEEP - 1, -(-(_THRESHOLDS[-1] + B_SWEEP - 1) // B_SWEEP))
    bias_s = _bias_tiles(rel_bias, delta_max + 2, B_SWEEP, B_SWEEP, base=-B_SWEEP, tile_step=B_SWEEP,
                         row_step=-1, col_step=1, dmax=1 << 30, dtype=BF16)
    win_tiles = (B_WINDOW - 1 + B_SWEEP - 1) // B_SWEEP + 1
    bias_w = _bias_tiles(rel_bias, win_tiles + 1, B_SWEEP, B_SWEEP, base=-B_SWEEP, tile_step=B_SWEEP,
                         row_step=-1, col_step=1, dmax=B_WINDOW - 1, dtype=BF16)
    o = _b_sparse(qt, ka, vs, wk, vw, sel, bias_s, bias_w, oc, gate_t, batch, seq)
    return _matmul_residual(o.reshape(batch * seq, -1), w_out.astype(BF16), x)


def _c_conv_kernel(cur_ref, halo_ref, w_ref, sm_ref, alog_ref, dtb_ref, qkv_ref, bg_ref, xe_ref):
    ts = cur_ref.shape[1]
    keep = jnp.where(pl.program_id(1) == 0, 0.0, 1.0)
    dk = C_HEAD_DIM
    for c in range(3 * C_HEADS):
        sl = slice(c * dk, (c + 1) * dk)
        xe_ref[c, :8, :] = halo_ref[0, :, sl].astype(F32) * keep
        xe_ref[c, 8:, :] = cur_ref[0, :, sl].astype(F32)
        y = None
        for j in range(C_CONV):
            off = 8 - (C_CONV - 1) + j
            term = w_ref[j:j + 1, sl] * xe_ref[c, off:off + ts, :]
            y = term if y is None else y + term
        y = y * jax.nn.sigmoid(y)
        if c < 2 * C_HEADS:
            y = y * lax.rsqrt(jnp.sum(y * y, axis=-1, keepdims=True) + RMS_EPS)
        if c < C_HEADS:
            y = y * (dk ** -0.5)
        qkv_ref[0, :, sl] = y.astype(qkv_ref.dtype)
    sm = sm_ref[0]
    a = sm + dtb_ref[...]
    softplus = jnp.maximum(a, 0.0) + jnp.log1p(jnp.exp(-jnp.abs(a)))
    g = -jnp.exp(alog_ref[...]) * softplus
    lane = lax.broadcasted_iota(jnp.int32, sm.shape, 1)
    bg_ref[0] = jnp.where(lane < C_HEADS, jax.nn.sigmoid(sm), g)


def _c_conv(proj, small, conv_w, a_log, dt_bias, batch, seq):
    ts = 256
    width = 3 * C_WIDTH
    pad = lambda v: jnp.pad(v, (C_HEADS, 128 - 2 * C_HEADS)).reshape(1, 128)
    return pl.pallas_call(
        _c_conv_kernel,
        out_shape=(jax.ShapeDtypeStruct((batch, seq, width), BF16),
                   jax.ShapeDtypeStruct((batch, seq, 128), F32)),
        grid=(batch, seq // ts),
        in_specs=[pl.BlockSpec((1, ts, width), lambda b, i: (b, i, 0)),
                  pl.BlockSpec((1, 8, width), lambda b, i: (b, jnp.maximum(i * (ts // 8) - 1, 0), 0)),
                  pl.BlockSpec((C_CONV, width), lambda b, i: (0, 0)),
                  pl.BlockSpec((1, ts, 128), lambda b, i: (b, i, 0)),
                  pl.BlockSpec((1, 128), lambda b, i: (0, 0)),
                  pl.BlockSpec((1, 128), lambda b, i: (0, 0))],
        out_specs=(pl.BlockSpec((1, ts, width), lambda b, i: (b, i, 0)),
                   pl.BlockSpec((1, ts, 128), lambda b, i: (b, i, 0))),
        scratch_shapes=[pltpu.VMEM((3 * C_HEADS, ts + 8, C_HEAD_DIM), F32)],
        compiler_params=_params("parallel", "arbitrary"),
        name="c_conv",
    )(proj, proj, conv_w, small, pad(a_log), pad(dt_bias))


def _sum3(x, fn):
    hi = x.astype(BF16)
    r = x - hi.astype(F32)
    mid = r.astype(BF16)
    lo = (r - mid.astype(F32)).astype(BF16)
    return fn(hi) + (fn(mid) + fn(lo))


def _c_chunk_kernel(qkv_ref, bg_ref, bgt_ref, tri_ref, trit_ref, blk_ref, u_ref, w_ref, qg_ref, kg_ref, attn_ref,
                    gc_ref):
    cs = C_CHUNK
    dk = C_HEAD_DIM
    gs = C_GROUP * cs
    row = lax.broadcasted_iota(jnp.int32, (gs, gs), 0)
    col = lax.broadcasted_iota(jnp.int32, (gs, gs), 1)
    same = (row // cs) == (col // cs)
    causal = same & (row >= col)
    strict = same & (row > col)
    eye = jnp.where(row == col, 1.0, 0.0)

    bgc = bg_ref[0]
    tri = tri_ref[...]
    gcum_col = _sum3(bgc, lambda p: _dot(tri, p))
    glast_col = _sum3(bgc, lambda p: _dot(blk_ref[...], p))
    gcum_row = _sum3(bgt_ref[0], lambda p: _dot(p, trit_ref[...]))
    gc_ref[0] = gcum_col
    t_mats, powers = [], []
    for h in range(C_HEADS):
        gc = gcum_col[:, C_HEADS + h:C_HEADS + h + 1]
        gr = gcum_row[C_HEADS + h:C_HEADS + h + 1, :]
        q = qkv_ref[0, :, h * dk:(h + 1) * dk]
        k = qkv_ref[0, :, C_WIDTH + h * dk:C_WIDTH + (h + 1) * dk]
        decay = jnp.exp(jnp.where(causal, gc - gr, NEG_INF))
        k16 = k.astype(BF16)
        low = jnp.where(strict, _dot_nt((k * bgc[:, h:h + 1]).astype(BF16), k16) * decay, 0.0)
        t_mats.append(eye - low)
        powers.append(low.astype(BF16))
        attn = jnp.where(causal, _dot_nt(q.astype(BF16), k16), 0.0) * decay
        attn_ref[0, :, h * gs:(h + 1) * gs] = attn.astype(attn_ref.dtype)
        qg_ref[0, :, h * dk:(h + 1) * dk] = (q * jnp.exp(gc)).astype(qg_ref.dtype)
        glast = glast_col[:, C_HEADS + h:C_HEADS + h + 1]
        kg_ref[0, :, h * dk:(h + 1) * dk] = (k * jnp.exp(glast - gc)).astype(kg_ref.dtype)
    for _ in range(int(math.log2(cs)) - 1):
        powers = [_dot(p, p).astype(BF16) for p in powers]
        t_mats = [t + _dot(t.astype(BF16), p) for t, p in zip(t_mats, powers)]
    for h in range(C_HEADS):
        gc = gcum_col[:, C_HEADS + h:C_HEADS + h + 1]
        beta = bgc[:, h:h + 1]
        k = qkv_ref[0, :, C_WIDTH + h * dk:C_WIDTH + (h + 1) * dk]
        v = qkv_ref[0, :, 2 * C_WIDTH + h * dk:2 * C_WIDTH + (h + 1) * dk]
        rhs = jnp.concatenate([(v * beta).astype(BF16), (k * beta * jnp.exp(gc)).astype(BF16)], axis=1)
        both = _dot(t_mats[h].astype(BF16), rhs)
        u_ref[0, :, h * dk:(h + 1) * dk] = both[:, :dk].astype(u_ref.dtype)
        w_ref[0, :, h * dk:(h + 1) * dk] = both[:, dk:].astype(w_ref.dtype)


def _c_chunks(qkv, bg, bgt, batch, seq):
    gs = C_GROUP * C_CHUNK
    idx = np.arange(gs)
    same = (idx[:, None] // C_CHUNK) == (idx[None, :] // C_CHUNK)
    tri = (same & (idx[:, None] >= idx[None, :])).astype(np.float32)
    wide = lambda width: pl.BlockSpec((1, gs, width), lambda b, i: (b, i, 0))
    shape = lambda width, dtype: jax.ShapeDtypeStruct((batch, seq, width), dtype)
    const = pl.BlockSpec((gs, gs), lambda b, i: (0, 0))
    return pl.pallas_call(
        _c_chunk_kernel,
        out_shape=(shape(C_WIDTH, BF16), shape(C_WIDTH, BF16), shape(C_WIDTH, BF16), shape(C_WIDTH, BF16),
                   shape(C_HEADS * gs, BF16), shape(128, F32)),
        grid=(batch, seq // gs),
        in_specs=[wide(3 * C_WIDTH), wide(128),
                  pl.BlockSpec((1, 2 * C_HEADS, gs), lambda b, i: (b, 0, i)),
                  const, const, const],
        out_specs=(wide(C_WIDTH),) * 4 + (wide(C_HEADS * gs), wide(128)),
        compiler_params=_params("parallel", "parallel"),
        name="c_chunks",
    )(qkv, bg, bgt, jnp.asarray(tri, BF16), jnp.asarray(tri.T, BF16), jnp.asarray(same, BF16))


def _c_scan_kernel(u_ref, w_ref, qg_ref, kg_ref, attn_ref, gc_ref, o_ref, state_ref, vnew_ref):
    @pl.when(pl.program_id(1) == 0)
    def _():
        state_ref[...] = jnp.zeros_like(state_ref)

    cs = C_CHUNK
    dk = C_HEAD_DIM
    gs = C_GROUP * cs
    vnew_ref[...] = jnp.zeros_like(vnew_ref)
    heads = range(C_HEADS)
    cols = [slice(h * dk, (h + 1) * dk) for h in heads]
    states = [state_ref[h] for h in heads]
    for c in range(C_GROUP):
        rs = slice(c * cs, (c + 1) * cs)
        decay_last = jnp.exp(gc_ref[0, (c + 1) * cs - 1:(c + 1) * cs, :])
        both = [_dot(jnp.concatenate([w_ref[0, rs, cols[h]], qg_ref[0, rs, cols[h]]], axis=0),
                     states[h].astype(BF16)) for h in heads]
        v16 = [(u_ref[0, rs, cols[h]] - both[h][:cs]).astype(BF16) for h in heads]
        for h in heads:
            vnew_ref[h, rs, :] = v16[h]
        for h in heads:
            out = both[h][cs:] + _dot(attn_ref[0, rs, h * gs:(h + 1) * gs], vnew_ref[h])
            o_ref[0, rs, cols[h]] = out.astype(o_ref.dtype)
        states = [states[h] * decay_last[:, C_HEADS + h:C_HEADS + h + 1] + _dot_tn(kg_ref[0, rs, cols[h]], v16[h])
                  for h in heads]
    for h in heads:
        state_ref[h] = states[h]


def _c_scan(u, w, qg, kg, attn, gc, batch, seq):
    gs = C_GROUP * C_CHUNK
    wide = lambda width: pl.BlockSpec((1, gs, width), lambda b, c: (b, c, 0))
    return pl.pallas_call(
        _c_scan_kernel,
        out_shape=jax.ShapeDtypeStruct((batch, seq, C_WIDTH), BF16),
        grid=(batch, seq // gs),
        in_specs=[wide(C_WIDTH)] * 4 + [wide(C_HEADS * gs), wide(128)],
        out_specs=wide(C_WIDTH),
        scratch_shapes=[pltpu.VMEM((C_HEADS, C_HEAD_DIM, C_HEAD_DIM), F32),
                        pltpu.VMEM((C_HEADS, gs, C_HEAD_DIM), BF16)],
        compiler_params=_params("parallel", "arbitrary"),
        name="c_scan",
    )(u, w, qg, kg, attn, gc)


def _c_out_kernel(o_ref, z_ref, g_ref, w_ref, x_ref, out_ref):
    dk = C_HEAD_DIM
    parts = []
    for h in range(C_HEADS):
        sl = slice(h * dk, (h + 1) * dk)
        z = z_ref[:, sl].astype(F32)
        parts.append((_rms(o_ref[:, sl].astype(F32), g_ref[...]) * (z * jax.nn.sigmoid(z))).astype(BF16))
    out_ref[...] = x_ref[...] + _dot(jnp.concatenate(parts, axis=-1), w_ref[...])


def _c_out(o, proj, out_gain, w_out, x):
    m, d = x.shape
    z_block = (3 * C_WIDTH) // C_WIDTH
    row = lambda width: pl.BlockSpec((ROW_TILE, width), lambda i: (i, 0))
    return pl.pallas_call(
        _c_out_kernel,
        out_shape=jax.ShapeDtypeStruct((m, d), F32),
        grid=(m // ROW_TILE,),
        in_specs=[row(C_WIDTH),
                  pl.BlockSpec((ROW_TILE, C_WIDTH), lambda i: (i, z_block)),
                  pl.BlockSpec((1, C_HEAD_DIM), lambda i: (0, 0)),
                  pl.BlockSpec((C_WIDTH, d), lambda i: (0, 0)),
                  row(d)],
        out_specs=row(d),
        compiler_params=_params("parallel"),
        name="c_out",
    )(o, proj, out_gain.reshape(1, -1), w_out, x)


def _mixer_c(x, norm1, w_in, conv_w, a_log, dt_bias, out_gain, w_out, batch, seq):
    main = 4 * C_WIDTH
    proj = _norm_matmul(x, norm1, w_in[:, :main].astype(BF16), BF16, 512)
    w_small = jnp.pad(w_in[:, main:], ((0, 0), (0, 128 - 2 * C_HEADS))).astype(BF16)
    small = _norm_matmul(x, norm1, w_small, F32, 128)
    qkv, bg = _c_conv(proj.reshape(batch, seq, main), small.reshape(batch, seq, 128), conv_w, a_log, dt_bias,
                      batch, seq)
    bgt = bg[:, :, :2 * C_HEADS].transpose(0, 2, 1)
    u, w, qg, kg, attn, gc = _c_chunks(qkv, bg, bgt, batch, seq)
    o = _c_scan(u, w, qg, kg, attn, gc, batch, seq)
    return _c_out(o.reshape(batch * seq, C_WIDTH), proj, out_gain, w_out.astype(BF16), x)


def kernel(x, rel_bias, l0_norm1, l0_a_w_in, l0_a_q_gain, l0_a_k_gain, l0_a_w_out, l0_norm2, l0_ffn_w_gate, l0_ffn_w_up, l0_ffn_w_down, l1_norm1, l1_b_w_in, l1_b_q_gain, l1_b_k_gain, l1_b_cmp_pos, l1_b_cmp_w1, l1_b_cmp_w2, l1_b_w_out, l1_norm2, l1_ffn_w_gate, l1_ffn_w_up, l1_ffn_w_down, l2_norm1, l2_c_w_in, l2_c_conv_w, l2_c_a_log, l2_c_dt_bias, l2_c_out_gain, l2_c_w_out, l2_norm2, l2_ffn_w_gate, l2_ffn_w_up, l2_ffn_w_down, l3_norm1, l3_a_w_in, l3_a_q_gain, l3_a_k_gain, l3_a_w_out, l3_norm2, l3_ffn_w_gate, l3_ffn_w_up, l3_ffn_w_down):
    batch, seq, d = x.shape
    h = x.reshape(batch * seq, d)

    def ffn(h, norm2, w_gate, w_up, w_down):
        return _ffn(h, norm2, w_gate.astype(BF16), w_up.astype(BF16), w_down.astype(BF16))

    a_biases = _a_biases(rel_bias)
    h = _mixer_a(h, a_biases, l0_norm1, l0_a_w_in, l0_a_q_gain, l0_a_k_gain, l0_a_w_out, batch, seq)
    h = ffn(h, l0_norm2, l0_ffn_w_gate, l0_ffn_w_up, l0_ffn_w_down)
    h = _mixer_b(h, rel_bias, l1_norm1, l1_b_w_in, l1_b_q_gain, l1_b_k_gain, l1_b_cmp_pos, l1_b_cmp_w1,
                 l1_b_cmp_w2, l1_b_w_out, batch, seq)
    h = ffn(h, l1_norm2, l1_ffn_w_gate, l1_ffn_w_up, l1_ffn_w_down)
    h = _mixer_c(h, l2_norm1, l2_c_w_in, l2_c_conv_w, l2_c_a_log, l2_c_dt_bias, l2_c_out_gain, l2_c_w_out,
                 batch, seq)
    h = ffn(h, l2_norm2, l2_ffn_w_gate, l2_ffn_w_up, l2_ffn_w_down)
    h = _mixer_a(h, a_biases, l3_norm1, l3_a_w_in, l3_a_q_gain, l3_a_k_gain, l3_a_w_out, batch, seq)
    h = ffn(h, l3_norm2, l3_ffn_w_gate, l3_ffn_w_up, l3_ffn_w_down)
    return h.reshape(batch, seq, d)
```

```python
import functools
import math

import numpy as np
import jax
import jax.numpy as jnp
from jax import lax
from jax.experimental import pallas as pl
from jax.experimental.pallas import tpu as pltpu

D_MODEL = 1024
RMS_EPS = 1e-6
NEG_INF = -1e30
TINY = 1e-30
FORCE_SCORE = 1e9

N_BUCKETS = 32
REL_MAX_DISTANCE = 2048
N_HEADS = 16

A_GROUPS = ((128, 1), (512, 4), (2048, 16))
A_HEAD_DIM = 64
A_Q_BLOCK = 128
A_PROJ_TILE = 512
A_STAT_WIDTH = 256
A_BLOCKS_PER_STEP = 8

B_KV_HEADS = 4
B_GROUP = 4
B_HEAD_DIM = 64
B_CMP_LEN = 32
B_CMP_STRIDE = 16
B_CMP_HIDDEN = 256
B_SEL_BLOCK = 64
B_TOP_N = 16
B_WINDOW = 512
B_TILE = 128
B_SWEEP = 256
B_PROJ_WIDTH = 3072
B_GATE_ROWS = 16

C_HEADS = 8
C_HEAD_DIM = 128
C_WIDTH = C_HEADS * C_HEAD_DIM
C_CONV = 4
C_CHUNK = 64
C_GROUP = 4

FFN_HIDDEN = 2816
FFN_TILE = 1024

ROW_TILE = 512
VMEM_LIMIT = 48 * 1024 * 1024

LOG2E = math.log2(math.e)

F32 = jnp.float32
BF16 = jnp.bfloat16

NT_DIMS = (((1,), (1,)), ((), ()))
TN_DIMS = (((0,), (0,)), ((), ()))


def _params(*semantics):
    return pltpu.CompilerParams(dimension_semantics=semantics, vmem_limit_bytes=VMEM_LIMIT)


def _dot(a, b):
    return jnp.dot(a, b, preferred_element_type=F32)


def _dot_nt(a, b):
    return lax.dot_general(a, b, NT_DIMS, preferred_element_type=F32)


def _dot_tn(a, b):
    return lax.dot_general(a, b, TN_DIMS, preferred_element_type=F32)


def _rms(x, gain):
    return x * lax.rsqrt(jnp.mean(x * x, axis=-1, keepdims=True) + RMS_EPS) * gain


def _bucket_thresholds():
    d = np.arange(1 << 15)
    max_exact = N_BUCKETS // 2
    d_f = np.maximum(d, 1).astype(np.float32)
    large = max_exact + (np.log(d_f / np.float32(max_exact)) / np.float32(math.log(REL_MAX_DISTANCE / max_exact))
                         * np.float32(N_BUCKETS - max_exact)).astype(np.int32)
    bucket = np.where(d < max_exact, d, np.minimum(large, N_BUCKETS - 1))
    return [int(np.argmax(bucket >= k)) if np.any(bucket >= k) else int(1 << 30) for k in range(N_BUCKETS)]


_THRESHOLDS = _bucket_thresholds()


def _bias_tile_kernel(tbl_ref, o_ref, *, base, tile_step, row_step, col_step, dmax, dil):
    h = pl.program_id(0)
    t = pl.program_id(1)
    shape = o_ref.shape[2:]
    i = lax.broadcasted_iota(jnp.int32, shape, 0)
    j = lax.broadcasted_iota(jnp.int32, shape, 1)
    dist = base + tile_step * t + row_step * i + col_step * j
    d = dist * dil
    val = jnp.full(shape, tbl_ref[0, h], F32)
    for k in range(1, N_BUCKETS):
        val = jnp.where(d >= _THRESHOLDS[k], tbl_ref[k, h], val)
    valid = (dist >= 0) & (dist <= dmax)
    o_ref[0, 0] = jnp.where(valid, val * LOG2E, NEG_INF).astype(o_ref.dtype)


def _bias_tiles(rel_bias, n_tiles, rows, cols, *, base, tile_step, row_step, col_step, dmax, dil=1, dtype=F32):
    kern = functools.partial(_bias_tile_kernel, base=base, tile_step=tile_step, row_step=row_step,
                             col_step=col_step, dmax=dmax, dil=dil)
    return pl.pallas_call(
        kern,
        out_shape=jax.ShapeDtypeStruct((N_HEADS, n_tiles, rows, cols), dtype),
        grid=(N_HEADS, n_tiles),
        in_specs=[pl.BlockSpec(memory_space=pltpu.SMEM)],
        out_specs=pl.BlockSpec((1, 1, rows, cols), lambda h, t: (h, t, 0, 0)),
        compiler_params=_params("parallel", "parallel"),
        name="bias_tiles",
    )(rel_bias)


def _resident(shape):
    return pl.BlockSpec(shape, lambda i: (0,) * len(shape), pipeline_mode=pl.Buffered(1))


def _norm_matmul_kernel(x_ref, g_ref, w_ref, o_ref, *, tn):
    h = _rms(x_ref[...], g_ref[...]).astype(BF16)
    for j in range(w_ref.shape[1] // tn):
        o_ref[:, j * tn:(j + 1) * tn] = _dot(h, w_ref[:, j * tn:(j + 1) * tn]).astype(o_ref.dtype)


def _norm_matmul(x, gain, w, out_dtype, tn):
    m, d = x.shape
    n = w.shape[1]
    return pl.pallas_call(
        functools.partial(_norm_matmul_kernel, tn=tn),
        out_shape=jax.ShapeDtypeStruct((m, n), out_dtype),
        grid=(m // ROW_TILE,),
        in_specs=[pl.BlockSpec((ROW_TILE, d), lambda i: (i, 0)),
                  _resident((1, d)),
                  _resident((d, n))],
        out_specs=pl.BlockSpec((ROW_TILE, n), lambda i: (i, 0)),
        compiler_params=_params("parallel"),
        name="norm_matmul",
    )(x, gain.reshape(1, d), w)


def _matmul_residual_kernel(a_ref, w_ref, x_ref, o_ref):
    o_ref[...] = x_ref[...] + _dot(a_ref[...], w_ref[...])


def _matmul_residual(a, w, x):
    m, k = a.shape
    d = w.shape[1]
    return pl.pallas_call(
        _matmul_residual_kernel,
        out_shape=jax.ShapeDtypeStruct((m, d), F32),
        grid=(m // ROW_TILE,),
        in_specs=[pl.BlockSpec((ROW_TILE, k), lambda i: (i, 0)),
                  pl.BlockSpec((k, d), lambda i: (0, 0)),
                  pl.BlockSpec((ROW_TILE, d), lambda i: (i, 0))],
        out_specs=pl.BlockSpec((ROW_TILE, d), lambda i: (i, 0)),
        compiler_params=_params("parallel"),
        name="matmul_residual",
    )(a, w, x)


def _ffn_kernel(x_ref, g_ref, wg_ref, wu_ref, wd_ref, o_ref):
    x = x_ref[...]
    h = _rms(x, g_ref[...]).astype(BF16)
    hidden = wg_ref.shape[1]
    acc = x
    for lo in range(0, hidden, FFN_TILE):
        hi = min(lo + FFN_TILE, hidden)
        a = _dot(h, wg_ref[:, lo:hi])
        b = _dot(h, wu_ref[:, lo:hi])
        acc = acc + _dot((a * jax.nn.sigmoid(a) * b).astype(BF16), wd_ref[lo:hi, :])
    o_ref[...] = acc


def _ffn(x, gain, w_gate, w_up, w_down):
    m, d = x.shape
    hidden = w_gate.shape[1]
    return pl.pallas_call(
        _ffn_kernel,
        out_shape=jax.ShapeDtypeStruct((m, d), F32),
        grid=(m // ROW_TILE,),
        in_specs=[pl.BlockSpec((ROW_TILE, d), lambda i: (i, 0)),
                  _resident((1, d)),
                  _resident((d, hidden)), _resident((d, hidden)), _resident((hidden, d))],
        out_specs=pl.BlockSpec((ROW_TILE, d), lambda i: (i, 0)),
        compiler_params=_params("parallel"),
        name="ffn",
    )(x, gain.reshape(1, d), w_gate, w_up, w_down)


def _a_proj_kernel(x_ref, g_ref, w_ref, qg_ref, kg_ref, o_ref, h_ref, x_scr, *, dil):
    rows = ROW_TILE // dil
    xn = _rms(x_ref[...], g_ref[...])
    if dil == 1:
        h_ref[...] = xn.astype(BF16)
    else:
        slabs = xn.shape[1] // 128
        for c in range(slabs):
            x_scr[c] = xn[:, c * 128:(c + 1) * 128]
        for r in range(dil):
            picked = [x_scr[c, pl.ds(r, rows, stride=dil), :] for c in range(slabs)]
            h_ref[r * rows:(r + 1) * rows, :] = jnp.concatenate(picked, axis=1).astype(BF16)
    h = h_ref[...]
    width = w_ref.shape[1]
    hd = width // 3
    low = lax.broadcasted_iota(jnp.int32, (ROW_TILE, 128), 1) < A_HEAD_DIM
    for j in range(width // A_PROJ_TILE):
        res = _dot(h, w_ref[:, j * A_PROJ_TILE:(j + 1) * A_PROJ_TILE])
        kind = (j * A_PROJ_TILE) // hd
        if kind < 2:
            parts = []
            for c in range(A_PROJ_TILE // 128):
                y = res[:, c * 128:(c + 1) * 128]
                sq = y * y
                tot = jnp.sum(sq, axis=-1, keepdims=True)
                lo = jnp.sum(jnp.where(low, sq, 0.0), axis=-1, keepdims=True)
                ss = jnp.where(low, lo, tot - lo)
                parts.append(y * lax.rsqrt(ss * (1.0 / A_HEAD_DIM) + RMS_EPS))
            res = jnp.concatenate(parts, axis=1) * (qg_ref if kind == 0 else kg_ref)[...]
        res = res.astype(BF16)
        for r in range(dil):
            off = r * width + j * A_PROJ_TILE
            o_ref[:, off:off + A_PROJ_TILE] = res[r * rows:(r + 1) * rows]


def _a_proj(x, gain, w, q_gain, k_gain, dil):
    m, d = x.shape
    width = w.shape[1]
    reps = A_PROJ_TILE // A_HEAD_DIM
    qg = jnp.tile(q_gain * (A_HEAD_DIM ** -0.5 * LOG2E), reps).reshape(1, A_PROJ_TILE)
    kg = jnp.tile(k_gain, reps).reshape(1, A_PROJ_TILE)
    return pl.pallas_call(
        functools.partial(_a_proj_kernel, dil=dil),
        out_shape=jax.ShapeDtypeStruct((m // dil, dil * width), BF16),
        grid=(m // ROW_TILE,),
        in_specs=[pl.BlockSpec((ROW_TILE, d), lambda i: (i, 0)),
                  pl.BlockSpec((1, d), lambda i: (0, 0)),
                  pl.BlockSpec((d, width), lambda i: (0, 0)),
                  pl.BlockSpec((1, A_PROJ_TILE), lambda i: (0, 0)),
                  pl.BlockSpec((1, A_PROJ_TILE), lambda i: (0, 0))],
        out_specs=pl.BlockSpec((ROW_TILE // dil, dil * width), lambda i: (i, 0)),
        scratch_shapes=[pltpu.VMEM((ROW_TILE, d), BF16), pltpu.VMEM((d // 128, ROW_TILE, 128), F32)],
        compiler_params=_params("parallel"),
        name="a_proj",
    )(x, gain.reshape(1, d), w, qg, kg)


def _a_attn_kernel(cur_ref, prev_ref, bias_ref, o_ref, stat_ref):
    first = (pl.program_id(2) == 0).astype(jnp.int32)
    nq = A_Q_BLOCK
    hd = N_HEADS * A_HEAD_DIM
    lane = lax.broadcasted_iota(jnp.int32, (nq, 128), 1)
    low = lane < A_HEAD_DIM
    ones = jnp.ones((2 * nq, 128), BF16)
    for res in range(o_ref.shape[2] // hd):
        for sub in range(cur_ref.shape[1] // nq):
            rows = slice(sub * nq, (sub + 1) * nq)
            max_tile = jnp.zeros((nq, 128), F32)
            den_tile = jnp.ones((nq, 128), F32)
            for pair in range(N_HEADS // 2):
                qc, kc, vc = (slice(res * 3 * hd + part * hd + pair * 128, res * 3 * hd + part * hd + (pair + 1) * 128)
                              for part in range(3))
                q = cur_ref[0, rows, qc]
                zero = jnp.zeros_like(q)
                qq = jnp.concatenate([jnp.where(low, q, zero), jnp.where(low, zero, q)], axis=0)
                if sub == 0:
                    kk = jnp.concatenate([prev_ref[0, :, kc], cur_ref[0, :nq, kc]], axis=0)
                    vv = jnp.concatenate([prev_ref[0, :, vc], cur_ref[0, :nq, vc]], axis=0)
                    base = 2 * pair + N_HEADS * first
                else:
                    kk = cur_ref[0, (sub - 1) * nq:(sub + 1) * nq, kc]
                    vv = cur_ref[0, (sub - 1) * nq:(sub + 1) * nq, vc]
                    base = 2 * pair
                s = _dot_nt(qq, kk) + jnp.concatenate([bias_ref[base], bias_ref[base + 1]], axis=0)
                m = jnp.max(s, axis=-1, keepdims=True)
                acc = _dot(jnp.exp2(s - m).astype(BF16), jnp.concatenate([vv, ones], axis=1))
                oc = slice(res * hd + pair * 128, res * hd + (pair + 1) * 128)
                o_ref[0, rows, oc] = jnp.where(low, acc[:nq, :128], acc[nq:, :128]).astype(o_ref.dtype)
                first_head = lane == 2 * pair
                second_head = lane == 2 * pair + 1
                max_tile = jnp.where(first_head, m[:nq], jnp.where(second_head, m[nq:], max_tile))
                den_tile = jnp.where(first_head, acc[:nq, 128:], jnp.where(second_head, acc[nq:, 128:], den_tile))
            stat_ref[0, rows, res * A_STAT_WIDTH:res * A_STAT_WIDTH + 128] = max_tile
            stat_ref[0, rows, res * A_STAT_WIDTH + 128:(res + 1) * A_STAT_WIDTH] = den_tile


def _a_attention(proj, bias, dil, batch, seq):
    length = seq // dil
    nblk = length // A_Q_BLOCK
    hd = N_HEADS * A_HEAD_DIM
    pv = proj.reshape(batch, length, dil * 3 * hd)

    per = min(A_BLOCKS_PER_STEP, nblk)
    res = min(A_BLOCKS_PER_STEP // per, dil)
    step_rows = per * A_Q_BLOCK
    assert nblk % per == 0 and dil % res == 0

    o, stats = pl.pallas_call(
        _a_attn_kernel,
        out_shape=(jax.ShapeDtypeStruct((batch, length, dil * hd), BF16),
                   jax.ShapeDtypeStruct((batch, length, dil * A_STAT_WIDTH), F32)),
        grid=(batch, dil // res, nblk // per),
        in_specs=[pl.BlockSpec((1, step_rows, res * 3 * hd), lambda b, r, i: (b, i, r)),
                  pl.BlockSpec((1, A_Q_BLOCK, res * 3 * hd), lambda b, r, i: (b, jnp.maximum(per * i - 1, 0), r)),
                  pl.BlockSpec((2 * N_HEADS, A_Q_BLOCK, 2 * A_Q_BLOCK), lambda b, r, i: (0, 0, 0))],
        out_specs=(pl.BlockSpec((1, step_rows, res * hd), lambda b, r, i: (b, i, r)),
                   pl.BlockSpec((1, step_rows, res * A_STAT_WIDTH), lambda b, r, i: (b, i, r))),
        compiler_params=_params("parallel", "parallel", "arbitrary"),
        name="a_attention",
    )(pv, pv, bias)
    return o.reshape(batch * length, dil * hd), stats.reshape(batch * length, dil * A_STAT_WIDTH)


def _a_out_kernel(o0_ref, o1_ref, o2_ref, s0_ref, s1_ref, s2_ref, e_ref, w_ref, x_ref, out_ref, o_scr, s_scr):
    hd = N_HEADS * A_HEAD_DIM
    sw = A_STAT_WIDTH
    for g, (o_ref, s_ref) in enumerate(((o0_ref, s0_ref), (o1_ref, s1_ref), (o2_ref, s2_ref))):
        dil = A_GROUPS[g][1]
        rows = ROW_TILE // dil
        for r in range(dil):
            dst = pl.ds(r, rows, stride=dil) if dil > 1 else slice(None)
            s_scr[g, 0, dst, :] = s_ref[:, r * sw:r * sw + 128]
            s_scr[g, 1, dst, :] = s_ref[:, r * sw + 128:(r + 1) * sw]
            for c in range(hd // 128):
                o_scr[g, c, dst, :] = o_ref[:, r * hd + c * 128:r * hd + (c + 1) * 128].astype(F32)
    groups = range(len(A_GROUPS))
    top = functools.reduce(jnp.maximum, [s_scr[g, 0] for g in groups])
    es = [jnp.exp2(s_scr[g, 0] - top) for g in groups]
    inv = 1.0 / sum(es[g] * s_scr[g, 1] for g in groups)
    expand = e_ref[...]
    acc = None
    for g in groups:
        o_g = jnp.concatenate([o_scr[g, c] for c in range(hd // 128)], axis=1)
        term = _dot((es[g] * inv).astype(BF16), expand) * o_g
        acc = term if acc is None else acc + term
    out_ref[...] = x_ref[...] + _dot(acc.astype(BF16), w_ref[...])


def _a_out(outs, stats, w_out, x):
    m, d = x.shape
    hd = N_HEADS * A_HEAD_DIM
    expand = np.zeros((128, hd), np.float32)
    for h in range(N_HEADS):
        expand[h, h * A_HEAD_DIM:(h + 1) * A_HEAD_DIM] = 1.0
    grouped = lambda width: [pl.BlockSpec((ROW_TILE // dil, dil * width), lambda i: (i, 0)) for _, dil in A_GROUPS]
    return pl.pallas_call(
        _a_out_kernel,
        out_shape=jax.ShapeDtypeStruct((m, d), F32),
        grid=(m // ROW_TILE,),
        in_specs=grouped(hd) + grouped(A_STAT_WIDTH) + [
            pl.BlockSpec((128, hd), lambda i: (0, 0)),
            pl.BlockSpec((hd, d), lambda i: (0, 0)),
            pl.BlockSpec((ROW_TILE, d), lambda i: (i, 0))],
        out_specs=pl.BlockSpec((ROW_TILE, d), lambda i: (i, 0)),
        scratch_shapes=[pltpu.VMEM((len(A_GROUPS), hd // 128, ROW_TILE, 128), F32),
                        pltpu.VMEM((len(A_GROUPS), 2, ROW_TILE, 128), F32)],
        compiler_params=_params("parallel"),
        name="a_out",
    )(*outs, *stats, jnp.asarray(expand, BF16), w_out, x)


def _a_biases(rel_bias):
    biases = []
    for window, dil in A_GROUPS:
        steps = window // dil
        assert steps == A_Q_BLOCK
        bias = _bias_tiles(rel_bias, 1, A_Q_BLOCK, 2 * A_Q_BLOCK, base=A_Q_BLOCK, tile_step=0, row_step=1,
                           col_step=-1, dmax=steps, dil=dil)[:, 0]
        biases.append(jnp.concatenate([bias, bias.at[:, :, :A_Q_BLOCK].set(NEG_INF)], axis=0))
    return biases


def _mixer_a(x, biases, norm1, w_in, q_gain, k_gain, w_out, batch, seq):
    w_in = w_in.astype(BF16)
    group_width = 3 * N_HEADS * A_HEAD_DIM
    outs, stats = [], []
    for gi, (_, dil) in enumerate(A_GROUPS):
        assert (seq // dil) % A_Q_BLOCK == 0 and seq % ROW_TILE == 0
        bias = biases[gi]
        proj = _a_proj(x, norm1, w_in[:, gi * group_width:(gi + 1) * group_width], q_gain[gi], k_gain[gi], dil)
        o, stat = _a_attention(proj, bias, dil, batch, seq)
        outs.append(o)
        stats.append(stat)
    return _a_out(outs, stats, w_out.astype(BF16), x)


def _b_prep_kernel(p_ref, qg_ref, kg_ref, qt_ref, ck_ref, cv_ref, ka_ref, vs_ref, wk_ref, vw_ref, gate_ref):
    dh = B_HEAD_DIM
    ts = p_ref.shape[1]
    n_sel = ka_ref.shape[3] - dh
    low = lax.broadcasted_iota(jnp.int32, (ts, 128), 1) < dh

    def slab(c):
        return p_ref[0, :, c * 128:(c + 1) * 128].astype(F32)

    def normed(x, gain):
        sq = x * x
        tot = jnp.sum(sq, axis=-1, keepdims=True)
        first = jnp.sum(jnp.where(low, sq, 0.0), axis=-1, keepdims=True)
        ss = jnp.where(low, first, tot - first)
        return x * lax.rsqrt(ss * (1.0 / dh) + RMS_EPS) * jnp.concatenate([gain, gain], axis=1)

    def halves(x):
        return x[:, :dh], x[:, dh:]

    qg = qg_ref[...] * (dh ** -0.5 * LOG2E)
    for c in range(N_HEADS // 2):
        xt = normed(slab(c), qg).T.astype(BF16)
        qt_ref[0, 2 * c] = xt[:dh]
        qt_ref[0, 2 * c + 1] = xt[dh:]
    base = N_HEADS // 2
    pairs = B_KV_HEADS // 2
    pos = pl.program_id(1) * ts + lax.broadcasted_iota(jnp.int32, (ts, 128), 0)
    lane = lax.broadcasted_iota(jnp.int32, (ts, 128), 1)
    onehot = jnp.where(lane - dh == pos // B_SEL_BLOCK, 1.0, 0.0)
    ones = jnp.ones((vs_ref.shape[2] - dh, ts), BF16)
    for j in range(pairs):
        for ref, off in ((ck_ref, 0), (cv_ref, pairs)):
            a, b = halves(slab(base + off + j))
            ref[0, 2 * j] = a.astype(BF16)
            ref[0, 2 * j + 1] = b.astype(BF16)
        k_sel = normed(slab(base + 2 * pairs + j), kg_ref[1:2, :])
        for n, k in zip((2 * j, 2 * j + 1), (k_sel, pltpu.roll(k_sel, dh, 1))):
            ka_ref[0, n] = jnp.where(low, k, onehot)[:, :dh + n_sel].astype(BF16)
        for n, k in zip((2 * j, 2 * j + 1), halves(normed(slab(base + 4 * pairs + j), kg_ref[2:3, :]))):
            wk_ref[0, n] = k.astype(BF16)
        for ref, off in ((vs_ref, 3 * pairs), (vw_ref, 5 * pairs)):
            xt = slab(base + off + j).T.astype(BF16)
            for n, v in zip((2 * j, 2 * j + 1), (xt[:dh], xt[dh:])):
                ref[0, n, :dh, :] = v
                ref[0, n, dh:, :] = ones
    gate = jax.nn.sigmoid(slab(base + 6 * pairs)).T
    rows = gate_ref.shape[2]
    for n in range(B_KV_HEADS):
        gate_ref[0, n] = gate[n * rows:(n + 1) * rows]


def _b_prep(proj, q_gain, k_gain, batch, seq):
    ts = 256
    dh = B_HEAD_DIM
    n_sel = seq // B_SEL_BLOCK
    assert dh + n_sel <= 128
    rows_shape = lambda width: jax.ShapeDtypeStruct((batch, B_KV_HEADS, seq, width), BF16)
    rows_spec = lambda width: pl.BlockSpec((1, B_KV_HEADS, ts, width), lambda b, i: (b, 0, i, 0))
    cols_shape = lambda heads, height, dtype: jax.ShapeDtypeStruct((batch, heads, height, seq), dtype)
    cols_spec = lambda heads, height: pl.BlockSpec((1, heads, height, ts), lambda b, i: (b, 0, 0, i))
    return pl.pallas_call(
        _b_prep_kernel,
        out_shape=(cols_shape(N_HEADS, dh, BF16), rows_shape(dh), rows_shape(dh), rows_shape(dh + n_sel),
                   cols_shape(B_KV_HEADS, dh + 16, BF16), rows_shape(dh), cols_shape(B_KV_HEADS, dh + 16, BF16),
                   cols_shape(B_KV_HEADS, B_GATE_ROWS, F32)),
        grid=(batch, seq // ts),
        in_specs=[pl.BlockSpec((1, ts, B_PROJ_WIDTH), lambda b, i: (b, i, 0)),
                  pl.BlockSpec((1, dh), lambda b, i: (0, 0)),
                  pl.BlockSpec((3, dh), lambda b, i: (0, 0))],
        out_specs=(cols_spec(N_HEADS, dh), rows_spec(dh), rows_spec(dh), rows_spec(dh + n_sel),
                   cols_spec(B_KV_HEADS, dh + 16), rows_spec(dh), cols_spec(B_KV_HEADS, dh + 16),
                   cols_spec(B_KV_HEADS, B_GATE_ROWS)),
        compiler_params=_params("parallel", "parallel"),
        name="b_prep",
    )(proj, q_gain.reshape(1, dh), k_gain)


def _b_compress_kernel(tk_ref, tv_ref, pos_ref, w1_ref, w2_ref, kg_ref, kc_ref, vc_ref):
    half = (B_CMP_LEN // 2) * B_HEAD_DIM
    for kv, (t_ref, out_ref) in enumerate(((tk_ref, kc_ref), (tv_ref, vc_ref))):
        t = t_ref[0, 0].astype(F32)
        top = (t + pos_ref[kv, 0:1, :]).astype(BF16)
        bot = (t + pos_ref[kv, 1:2, :]).astype(BF16)
        a1 = _dot(top, w1_ref[kv, :half, :])
        a2 = _dot(bot, w1_ref[kv, half:, :])
        hidden = a1 + pltpu.roll(a2, a2.shape[0] - 1, 0)
        out = _dot(jax.nn.gelu(hidden).astype(BF16), w2_ref[kv])
        if kv == 0:
            out = _rms(out, kg_ref[...])
        out_ref[0, 0] = out.astype(out_ref.dtype)


def _b_compress(ck, cv, cmp_pos, cmp_w1, cmp_w2, k_gain0, batch, seq):
    rows = seq // B_CMP_STRIDE
    half = (B_CMP_LEN // 2) * B_HEAD_DIM
    tk = ck.reshape(batch, B_KV_HEADS, rows, half)
    tv = cv.reshape(batch, B_KV_HEADS, rows, half)
    pos = cmp_pos.reshape(2, 2, half)
    t_spec = pl.BlockSpec((1, 1, rows, half), lambda b, n: (b, n, 0, 0))
    o_spec = pl.BlockSpec((1, 1, rows, B_HEAD_DIM), lambda b, n: (b, n, 0, 0))
    shape = jax.ShapeDtypeStruct((batch, B_KV_HEADS, rows, B_HEAD_DIM), BF16)
    return pl.pallas_call(
        _b_compress_kernel,
        out_shape=(shape, shape),
        grid=(batch, B_KV_HEADS),
        in_specs=[t_spec, t_spec,
                  pl.BlockSpec((2, 2, half), lambda b, n: (0, 0, 0)),
                  pl.BlockSpec((2, 2 * half, B_CMP_HIDDEN), lambda b, n: (0, 0, 0)),
                  pl.BlockSpec((2, B_CMP_HIDDEN, B_HEAD_DIM), lambda b, n: (0, 0, 0)),
                  pl.BlockSpec((1, B_HEAD_DIM), lambda b, n: (0, 0))],
        out_specs=(o_spec, o_spec),
        compiler_params=_params("parallel", "parallel"),
        name="b_compress",
    )(tk, tv, pos, cmp_w1.astype(BF16), cmp_w2.astype(BF16), k_gain0.reshape(1, -1))


def _b_cmp_attn_kernel(qt_ref, kc_ref, vct_ref, bias_ref, c2s_ref, oc_ref, sel_ref, imp_ref, *, top_n):
    tq = B_SWEEP
    n_sel = imp_ref.shape[0]
    qt = jnp.concatenate([qt_ref[0, g] for g in range(B_GROUP)], axis=1)
    n_cmp_pad = kc_ref.shape[2]
    shift = tq // B_CMP_STRIDE
    off = pl.multiple_of((pl.num_programs(2) - 1 - pl.program_id(2)) * shift, shift)
    bias = jnp.concatenate([bias_ref[g, 0, pl.ds(off, n_cmp_pad), :] for g in range(B_GROUP)], axis=1)
    s = _dot(kc_ref[0, 0], qt) + bias
    m = jnp.max(s, axis=0, keepdims=True)
    e = jnp.exp2(s - m)
    z = jnp.maximum(jnp.sum(e, axis=0, keepdims=True), TINY)
    pos = pl.program_id(2) * tq + lax.broadcasted_iota(jnp.int32, (1, tq), 1)
    sees_any = jnp.concatenate([pos >= B_CMP_LEN - 1] * B_GROUP, axis=1)
    p = e * jnp.where(sees_any, 1.0 / z, 0.0)
    oct = _dot(vct_ref[0, 0], p.astype(BF16))
    for g in range(B_GROUP):
        oc_ref[0, g] = oct[:, g * tq:(g + 1) * tq]

    p_sum = p[:, 0:tq] + p[:, tq:2 * tq] + p[:, 2 * tq:3 * tq] + p[:, 3 * tq:4 * tq]
    hi = p_sum.astype(BF16)
    lo = (p_sum - hi.astype(F32)).astype(BF16)
    c2s = c2s_ref[...]
    imp = _dot(c2s, hi) + _dot(c2s, lo)

    t = pl.program_id(2) * tq + lax.broadcasted_iota(jnp.int32, (n_sel, tq), 1)
    blk = lax.broadcasted_iota(jnp.int32, (n_sel, tq), 0)
    cur = t // B_SEL_BLOCK
    forced = (blk == 0) | (blk == cur) | (blk == cur - 1)
    imp = jnp.where(forced, FORCE_SCORE, jnp.where(blk * B_SEL_BLOCK <= t, imp, NEG_INF))
    imp_ref[...] = imp

    def count(i, rank):
        row = imp_ref[pl.ds(i, 1), :]
        ahead = jnp.where(row > imp, 1.0, jnp.where(row == imp, jnp.where(blk > i, 1.0, 0.0), 0.0))
        return rank + ahead

    n_live = jnp.minimum(n_sel, (pl.program_id(2) + 1) * (tq // B_SEL_BLOCK))
    rank = lax.fori_loop(0, n_live, count, jnp.zeros((n_sel, tq), F32))
    sel_ref[0, 0] = jnp.where(rank < top_n, 0.0, NEG_INF).astype(sel_ref.dtype)


def _b_cmp_attn(qt, kc, vc, bias_c, batch, seq):
    n_sel = seq // B_SEL_BLOCK
    n_cmp_pad = seq // B_CMP_STRIDE
    n_cmp = (seq - B_CMP_LEN) // B_CMP_STRIDE + 1
    c = np.arange(n_cmp_pad)[None, :] * B_CMP_STRIDE
    j = np.arange(n_sel)[:, None] * B_SEL_BLOCK
    c2s = ((c < j + B_SEL_BLOCK) & (c + B_CMP_LEN > j) & (np.arange(n_cmp_pad)[None, :] < n_cmp)).astype(np.float32)
    kern = functools.partial(_b_cmp_attn_kernel, top_n=min(B_TOP_N, n_sel))
    return pl.pallas_call(
        kern,
        out_shape=(jax.ShapeDtypeStruct((batch, N_HEADS, B_HEAD_DIM, seq), F32),
                   jax.ShapeDtypeStruct((batch, B_KV_HEADS, n_sel, seq), BF16)),
        grid=(batch, B_KV_HEADS, seq // B_SWEEP),
        in_specs=[pl.BlockSpec((1, B_GROUP, B_HEAD_DIM, B_SWEEP), lambda b, n, i: (b, n, 0, i)),
                  pl.BlockSpec((1, 1, n_cmp_pad, B_HEAD_DIM), lambda b, n, i: (b, n, 0, 0)),
                  pl.BlockSpec((1, 1, B_HEAD_DIM, n_cmp_pad), lambda b, n, i: (b, n, 0, 0)),
                  pl.BlockSpec((B_GROUP, 1, bias_c.shape[2], B_SWEEP), lambda b, n, i: (n, 0, 0, 0)),
                  pl.BlockSpec((n_sel, n_cmp_pad), lambda b, n, i: (0, 0))],
        out_specs=(pl.BlockSpec((1, B_GROUP, B_HEAD_DIM, B_SWEEP), lambda b, n, i: (b, n, 0, i)),
                   pl.BlockSpec((1, 1, n_sel, B_SWEEP), lambda b, n, i: (b, n, 0, i))),
        scratch_shapes=[pltpu.VMEM((n_sel, B_SWEEP), F32)],
        compiler_params=_params("parallel", "parallel", "arbitrary"),
        name="b_cmp_attn",
    )(qt, kc, vc.transpose(0, 1, 3, 2), bias_c, jnp.asarray(c2s, BF16))


def _b_sparse_kernel(qt_ref, ka_ref, vs_ref, wk_ref, vw_ref, sel_ref, bs_ref, bw_ref, oc_ref, gate_ref,
                     o_ref, acc_ref, sa_ref, sb_ref, *, delta_max, win_tiles):
    tq = B_SWEEP
    dh = B_HEAD_DIM
    cols = B_GROUP * tq
    qi = pl.program_id(2)
    n_tiles = ka_ref.shape[2] // tq
    qt = jnp.concatenate([qt_ref[0, g] for g in range(B_GROUP)], axis=1)
    q_aug = jnp.concatenate([qt, jnp.concatenate([sel_ref[0, 0]] * B_GROUP, axis=1)], axis=0)

    def tile_start(kt):
        return pl.multiple_of(jnp.clip(kt, 0, n_tiles - 1) * tq, tq)

    def normalised(acc):
        return acc[:dh] * (1.0 / acc[dh:dh + 1])

    def sel_bias(kt):
        d = jnp.clip(qi - kt, -1, delta_max) + 1
        return jnp.concatenate([bs_ref[g, d] for g in range(B_GROUP)], axis=1)

    def sel_scores(kt):
        return _dot(ka_ref[0, 0, pl.ds(tile_start(kt), tq), :], q_aug).astype(BF16) + sel_bias(kt)

    def consume(s_buf, kt, m_old):
        s = s_buf[...]
        m_new = jnp.maximum(m_old, jnp.max(s, axis=0, keepdims=True).astype(F32))
        alpha = jnp.exp2(m_old - m_new)
        p = jnp.exp2(s - m_new.astype(BF16))
        acc_ref[...] = alpha * acc_ref[...] + _dot(vs_ref[0, 0, :, pl.ds(tile_start(kt), tq)], p)
        return m_new

    acc_ref[...] = jnp.zeros(acc_ref.shape, F32)
    sa_ref[...] = sel_scores(0)

    def pair(kt, m):
        sb_ref[...] = sel_scores(kt + 1)
        m = consume(sa_ref, kt, m)
        sa_ref[...] = sel_scores(kt + 2)
        return consume(sb_ref, kt + 1, m)

    n_pairs = (qi + 2) // 2
    n_double = n_pairs // 2
    m = lax.fori_loop(0, n_double, lambda j, m: pair(4 * j + 2, pair(4 * j, m)),
                      jnp.full((1, cols), NEG_INF, F32))
    lax.fori_loop(2 * n_double, n_pairs, lambda j, m: pair(2 * j, m), m)
    o_s = normalised(acc_ref[...])

    tiles = []
    for u in range(win_tiles):
        kt = qi - (win_tiles - 1) + u
        d = jnp.where(kt >= 0, qi - kt, -1) + 1
        bias = jnp.concatenate([bw_ref[g, d] for g in range(B_GROUP)], axis=1)
        tiles.append((_dot(wk_ref[0, 0, pl.ds(tile_start(kt), tq), :], qt).astype(BF16) + bias, kt))
    m = None
    for s, _ in tiles:
        tile_max = jnp.max(s, axis=0, keepdims=True)
        m = tile_max if m is None else jnp.maximum(m, tile_max)
    acc = None
    for s, kt in tiles:
        pv = _dot(vw_ref[0, 0, :, pl.ds(tile_start(kt), tq)], jnp.exp2(s - m))
        acc = pv if acc is None else acc + pv
    o_w = normalised(acc)

    gate = gate_ref[0, 0]
    merged = []
    for g in range(B_GROUP):
        cs = slice(g * tq, (g + 1) * tq)
        merged.append(gate[3 * g:3 * g + 1] * oc_ref[0, g] + gate[3 * g + 1:3 * g + 2] * o_s[:, cs]
                      + gate[3 * g + 2:3 * g + 3] * o_w[:, cs])
    for pair in range(B_GROUP // 2):
        both = jnp.concatenate([merged[2 * pair], merged[2 * pair + 1]], axis=0)
        o_ref[0, :, pair * 2 * dh:(pair + 1) * 2 * dh] = both.T.astype(o_ref.dtype)


def _b_sparse(qt, ka, vs, wk, vw, sel, bias_s, bias_w, oc, gate, batch, seq):
    n_sel = seq // B_SEL_BLOCK
    dh = B_HEAD_DIM
    n_ds = bias_s.shape[1]
    n_dw = bias_w.shape[1]
    vrows = vs.shape[2]
    kern = functools.partial(_b_sparse_kernel, delta_max=n_ds - 2, win_tiles=n_dw - 1)
    whole = lambda rows, width: pl.BlockSpec((1, 1, rows, width), lambda b, n, i: (b, n, 0, 0))
    return pl.pallas_call(
        kern,
        out_shape=jax.ShapeDtypeStruct((batch, seq, N_HEADS * dh), BF16),
        grid=(batch, B_KV_HEADS, seq // B_SWEEP),
        in_specs=[pl.BlockSpec((1, B_GROUP, dh, B_SWEEP), lambda b, n, i: (b, n, 0, i)),
                  whole(seq, dh + n_sel), whole(vrows, seq), whole(seq, dh), whole(vrows, seq),
                  pl.BlockSpec((1, 1, n_sel, B_SWEEP), lambda b, n, i: (b, n, 0, i)),
                  pl.BlockSpec((B_GROUP, n_ds, B_SWEEP, B_SWEEP), lambda b, n, i: (n, 0, 0, 0)),
                  pl.BlockSpec((B_GROUP, n_dw, B_SWEEP, B_SWEEP), lambda b, n, i: (n, 0, 0, 0)),
                  pl.BlockSpec((1, B_GROUP, dh, B_SWEEP), lambda b, n, i: (b, n, 0, i)),
                  pl.BlockSpec((1, 1, B_GATE_ROWS, B_SWEEP), lambda b, n, i: (b, n, 0, i))],
        out_specs=pl.BlockSpec((1, B_SWEEP, B_GROUP * dh), lambda b, n, i: (b, i, n)),
        scratch_shapes=[pltpu.VMEM((vrows, B_GROUP * B_SWEEP), F32),
                        pltpu.VMEM((B_SWEEP, B_GROUP * B_SWEEP), BF16),
                        pltpu.VMEM((B_SWEEP, B_GROUP * B_SWEEP), BF16)],
        compiler_params=_params("parallel", "parallel", "arbitrary"),
        name="b_sparse",
    )(qt, ka, vs, wk, vw, sel, bias_s, bias_w, oc, gate)


def _mixer_b(x, rel_bias, norm1, w_in, q_gain, k_gain, cmp_pos, cmp_w1, cmp_w2, w_out, batch, seq):
    d = w_in.shape[0]
    qkv_width = w_in.shape[1] - 3 * N_HEADS
    gate_w = w_in[:, qkv_width:].reshape(d, B_KV_HEADS, 3 * B_GROUP)
    gate_w = jnp.pad(gate_w, ((0, 0), (0, 0), (0, B_GATE_ROWS - 3 * B_GROUP))).reshape(d, B_KV_HEADS * B_GATE_ROWS)
    w_pad = jnp.concatenate([w_in[:, :qkv_width], gate_w], axis=1)
    w_pad = jnp.pad(w_pad, ((0, 0), (0, B_PROJ_WIDTH - w_pad.shape[1]))).astype(BF16)
    proj = _norm_matmul(x, norm1, w_pad, BF16, 512).reshape(batch, seq, B_PROJ_WIDTH)
    qt, ck, cv, ka, vs, wk, vw, gate_t = _b_prep(proj, q_gain, k_gain, batch, seq)
    kc, vc = _b_compress(ck, cv, cmp_pos, cmp_w1, cmp_w2, k_gain[0], batch, seq)
    last_tile = seq // B_SWEEP - 1
    bias_c = _bias_tiles(rel_bias, 1, seq // B_CMP_STRIDE + last_tile * (B_SWEEP // B_CMP_STRIDE), B_SWEEP,
                         base=1 - B_CMP_LEN + last_tile * B_SWEEP, tile_step=0, row_step=-B_CMP_STRIDE,
                         col_step=1, dmax=1 << 30)
    oc, sel = _b_cmp_attn(qt, kc, vc, bias_c, batch, seq)
    delta_max = min(seq // B_SWEEP - 1, -(-(_THRESHOLDS[-1] + B_SWEEP - 1) // B_SWEEP))
    bias_s = _bias_tiles(rel_bias, delta_max + 2, B_SWEEP, B_SWEEP, base=-B_SWEEP, tile_step=B_SWEEP,
                         row_step=-1, col_step=1, dmax=1 << 30, dtype=BF16)
    win_tiles = (B_WINDOW - 1 + B_SWEEP - 1) // B_SWEEP + 1
    bias_w = _bias_tiles(rel_bias, win_tiles + 1, B_SWEEP, B_SWEEP, base=-B_SWEEP, tile_step=B_SWEEP,
                         row_step=-1, col_step=1, dmax=B_WINDOW - 1, dtype=BF16)
    o = _b_sparse(qt, ka, vs, wk, vw, sel, bias_s, bias_w, oc, gate_t, batch, seq)
    return _matmul_residual(o.reshape(batch * seq, -1), w_out.astype(BF16), x)


def _c_conv_kernel(cur_ref, halo_ref, w_ref, sm_ref, alog_ref, dtb_ref, qkv_ref, bg_ref, xe_ref):
    ts = cur_ref.shape[1]
    keep = jnp.where(pl.program_id(1) == 0, 0.0, 1.0)
    dk = C_HEAD_DIM
    for c in range(3 * C_HEADS):
        sl = slice(c * dk, (c + 1) * dk)
        xe_ref[c, :8, :] = halo_ref[0, :, sl].astype(F32) * keep
        xe_ref[c, 8:, :] = cur_ref[0, :, sl].astype(F32)
        y = None
        for j in range(C_CONV):
            off = 8 - (C_CONV - 1) + j
            term = w_ref[j:j + 1, sl] * xe_ref[c, off:off + ts, :]
            y = term if y is None else y + term
        y = y * jax.nn.sigmoid(y)
        if c < 2 * C_HEADS:
            y = y * lax.rsqrt(jnp.sum(y * y, axis=-1, keepdims=True) + RMS_EPS)
        if c < C_HEADS:
            y = y * (dk ** -0.5)
        qkv_ref[0, :, sl] = y.astype(qkv_ref.dtype)
    sm = sm_ref[0]
    a = sm + dtb_ref[...]
    softplus = jnp.maximum(a, 0.0) + jnp.log1p(jnp.exp(-jnp.abs(a)))
    g = -jnp.exp(alog_ref[...]) * softplus
    lane = lax.broadcasted_iota(jnp.int32, sm.shape, 1)
    bg_ref[0] = jnp.where(lane < C_HEADS, jax.nn.sigmoid(sm), g)


def _c_conv(proj, small, conv_w, a_log, dt_bias, batch, seq):
    ts = 256
    width = 3 * C_WIDTH
    pad = lambda v: jnp.pad(v, (C_HEADS, 128 - 2 * C_HEADS)).reshape(1, 128)
    return pl.pallas_call(
        _c_conv_kernel,
        out_shape=(jax.ShapeDtypeStruct((batch, seq, width), BF16),
                   jax.ShapeDtypeStruct((batch, seq, 128), F32)),
        grid=(batch, seq // ts),
        in_specs=[pl.BlockSpec((1, ts, width), lambda b, i: (b, i, 0)),
                  pl.BlockSpec((1, 8, width), lambda b, i: (b, jnp.maximum(i * (ts // 8) - 1, 0), 0)),
                  pl.BlockSpec((C_CONV, width), lambda b, i: (0, 0)),
                  pl.BlockSpec((1, ts, 128), lambda b, i: (b, i, 0)),
                  pl.BlockSpec((1, 128), lambda b, i: (0, 0)),
                  pl.BlockSpec((1, 128), lambda b, i: (0, 0))],
        out_specs=(pl.BlockSpec((1, ts, width), lambda b, i: (b, i, 0)),
                   pl.BlockSpec((1, ts, 128), lambda b, i: (b, i, 0))),
        scratch_shapes=[pltpu.VMEM((3 * C_HEADS, ts + 8, C_HEAD_DIM), F32)],
        compiler_params=_params("parallel", "arbitrary"),
        name="c_conv",
    )(proj, proj, conv_w, small, pad(a_log), pad(dt_bias))


def _sum3(x, fn):
    hi = x.astype(BF16)
    r = x - hi.astype(F32)
    mid = r.astype(BF16)
    lo = (r - mid.astype(F32)).astype(BF16)
    return fn(hi) + (fn(mid) + fn(lo))


def _c_chunk_kernel(qkv_ref, bg_ref, bgt_ref, tri_ref, trit_ref, blk_ref, u_ref, w_ref, qg_ref, kg_ref, attn_ref,
                    gc_ref):
    cs = C_CHUNK
    dk = C_HEAD_DIM
    gs = C_GROUP * cs
    row = lax.broadcasted_iota(jnp.int32, (gs, gs), 0)
    col = lax.broadcasted_iota(jnp.int32, (gs, gs), 1)
    same = (row // cs) == (col // cs)
    causal = same & (row >= col)
    strict = same & (row > col)
    eye = jnp.where(row == col, 1.0, 0.0)

    bgc = bg_ref[0]
    tri = tri_ref[...]
    gcum_col = _sum3(bgc, lambda p: _dot(tri, p))
    glast_col = _sum3(bgc, lambda p: _dot(blk_ref[...], p))
    gcum_row = _sum3(bgt_ref[0], lambda p: _dot(p, trit_ref[...]))
    gc_ref[0] = gcum_col
    t_mats, powers = [], []
    for h in range(C_HEADS):
        gc = gcum_col[:, C_HEADS + h:C_HEADS + h + 1]
        gr = gcum_row[C_HEADS + h:C_HEADS + h + 1, :]
        q = qkv_ref[0, :, h * dk:(h + 1) * dk]
        k = qkv_ref[0, :, C_WIDTH + h * dk:C_WIDTH + (h + 1) * dk]
        decay = jnp.exp(jnp.where(causal, gc - gr, NEG_INF))
        k16 = k.astype(BF16)
        low = jnp.where(strict, _dot_nt((k * bgc[:, h:h + 1]).astype(BF16), k16) * decay, 0.0)
        t_mats.append(eye - low)
        powers.append(low.astype(BF16))
        attn = jnp.where(causal, _dot_nt(q.astype(BF16), k16), 0.0) * decay
        attn_ref[0, :, h * gs:(h + 1) * gs] = attn.astype(attn_ref.dtype)
        qg_ref[0, :, h * dk:(h + 1) * dk] = (q * jnp.exp(gc)).astype(qg_ref.dtype)
        glast = glast_col[:, C_HEADS + h:C_HEADS + h + 1]
        kg_ref[0, :, h * dk:(h + 1) * dk] = (k * jnp.exp(glast - gc)).astype(kg_ref.dtype)
    for _ in range(int(math.log2(cs)) - 1):
        powers = [_dot(p, p).astype(BF16) for p in powers]
        t_mats = [t + _dot(t.astype(BF16), p) for t, p in zip(t_mats, powers)]
    for h in range(C_HEADS):
        gc = gcum_col[:, C_HEADS + h:C_HEADS + h + 1]
        beta = bgc[:, h:h + 1]
        k = qkv_ref[0, :, C_WIDTH + h * dk:C_WIDTH + (h + 1) * dk]
        v = qkv_ref[0, :, 2 * C_WIDTH + h * dk:2 * C_WIDTH + (h + 1) * dk]
        rhs = jnp.concatenate([(v * beta).astype(BF16), (k * beta * jnp.exp(gc)).astype(BF16)], axis=1)
        both = _dot(t_mats[h].astype(BF16), rhs)
        u_ref[0, :, h * dk:(h + 1) * dk] = both[:, :dk].astype(u_ref.dtype)
        w_ref[0, :, h * dk:(h + 1) * dk] = both[:, dk:].astype(w_ref.dtype)


def _c_chunks(qkv, bg, bgt, batch, seq):
    gs = C_GROUP * C_CHUNK
    idx = np.arange(gs)
    same = (idx[:, None] // C_CHUNK) == (idx[None, :] // C_CHUNK)
    tri = (same & (idx[:, None] >= idx[None, :])).astype(np.float32)
    wide = lambda width: pl.BlockSpec((1, gs, width), lambda b, i: (b, i, 0))
    shape = lambda width, dtype: jax.ShapeDtypeStruct((batch, seq, width), dtype)
    const = pl.BlockSpec((gs, gs), lambda b, i: (0, 0))
    return pl.pallas_call(
        _c_chunk_kernel,
        out_shape=(shape(C_WIDTH, BF16), shape(C_WIDTH, BF16), shape(C_WIDTH, BF16), shape(C_WIDTH, BF16),
                   shape(C_HEADS * gs, BF16), shape(128, F32)),
        grid=(batch, seq // gs),
        in_specs=[wide(3 * C_WIDTH), wide(128),
                  pl.BlockSpec((1, 2 * C_HEADS, gs), lambda b, i: (b, 0, i)),
                  const, const, const],
        out_specs=(wide(C_WIDTH),) * 4 + (wide(C_HEADS * gs), wide(128)),
        compiler_params=_params("parallel", "parallel"),
        name="c_chunks",
    )(qkv, bg, bgt, jnp.asarray(tri, BF16), jnp.asarray(tri.T, BF16), jnp.asarray(same, BF16))


def _c_scan_kernel(u_ref, w_ref, qg_ref, kg_ref, attn_ref, gc_ref, o_ref, state_ref, vnew_ref):
    @pl.when(pl.program_id(1) == 0)
    def _():
        state_ref[...] = jnp.zeros_like(state_ref)

    cs = C_CHUNK
    dk = C_HEAD_DIM
    gs = C_GROUP * cs
    vnew_ref[...] = jnp.zeros_like(vnew_ref)
    heads = range(C_HEADS)
    cols = [slice(h * dk, (h + 1) * dk) for h in heads]
    states = [state_ref[h] for h in heads]
    for c in range(C_GROUP):
        rs = slice(c * cs, (c + 1) * cs)
        decay_last = jnp.exp(gc_ref[0, (c + 1) * cs - 1:(c + 1) * cs, :])
        both = [_dot(jnp.concatenate([w_ref[0, rs, cols[h]], qg_ref[0, rs, cols[h]]], axis=0),
                     states[h].astype(BF16)) for h in heads]
        v16 = [(u_ref[0, rs, cols[h]] - both[h][:cs]).astype(BF16) for h in heads]
        for h in heads:
            vnew_ref[h, rs, :] = v16[h]
        for h in heads:
            out = both[h][cs:] + _dot(attn_ref[0, rs, h * gs:(h + 1) * gs], vnew_ref[h])
            o_ref[0, rs, cols[h]] = out.astype(o_ref.dtype)
        states = [states[h] * decay_last[:, C_HEADS + h:C_HEADS + h + 1] + _dot_tn(kg_ref[0, rs, cols[h]], v16[h])
                  for h in heads]
    for h in heads:
        state_ref[h] = states[h]


def _c_scan(u, w, qg, kg, attn, gc, batch, seq):
    gs = C_GROUP * C_CHUNK
    wide = lambda width: pl.BlockSpec((1, gs, width), lambda b, c: (b, c, 0))
    return pl.pallas_call(
        _c_scan_kernel,
        out_shape=jax.ShapeDtypeStruct((batch, seq, C_WIDTH), BF16),
        grid=(batch, seq // gs),
        in_specs=[wide(C_WIDTH)] * 4 + [wide(C_HEADS * gs), wide(128)],
        out_specs=wide(C_WIDTH),
        scratch_shapes=[pltpu.VMEM((C_HEADS, C_HEAD_DIM, C_HEAD_DIM), F32),
                        pltpu.VMEM((C_HEADS, gs, C_HEAD_DIM), BF16)],
        compiler_params=_params("parallel", "arbitrary"),
        name="c_scan",
    )(u, w, qg, kg, attn, gc)


def _c_out_kernel(o_ref, z_ref, g_ref, w_ref, x_ref, out_ref):
    dk = C_HEAD_DIM
    parts = []
    for h in range(C_HEADS):
        sl = slice(h * dk, (h + 1) * dk)
        z = z_ref[:, sl].astype(F32)
        parts.append((_rms(o_ref[:, sl].astype(F32), g_ref[...]) * (z * jax.nn.sigmoid(z))).astype(BF16))
    out_ref[...] = x_ref[...] + _dot(jnp.concatenate(parts, axis=-1), w_ref[...])


def _c_out(o, proj, out_gain, w_out, x):
    m, d = x.shape
    z_block = (3 * C_WIDTH) // C_WIDTH
    row = lambda width: pl.BlockSpec((ROW_TILE, width), lambda i: (i, 0))
    return pl.pallas_call(
        _c_out_kernel,
        out_shape=jax.ShapeDtypeStruct((m, d), F32),
        grid=(m // ROW_TILE,),
        in_specs=[row(C_WIDTH),
                  pl.BlockSpec((ROW_TILE, C_WIDTH), lambda i: (i, z_block)),
                  pl.BlockSpec((1, C_HEAD_DIM), lambda i: (0, 0)),
                  pl.BlockSpec((C_WIDTH, d), lambda i: (0, 0)),
                  row(d)],
        out_specs=row(d),
        compiler_params=_params("parallel"),
        name="c_out",
    )(o, proj, out_gain.reshape(1, -1), w_out, x)


def _mixer_c(x, norm1, w_in, conv_w, a_log, dt_bias, out_gain, w_out, batch, seq):
    main = 4 * C_WIDTH
    proj = _norm_matmul(x, norm1, w_in[:, :main].astype(BF16), BF16, 512)
    w_small = jnp.pad(w_in[:, main:], ((0, 0), (0, 128 - 2 * C_HEADS))).astype(BF16)
    small = _norm_matmul(x, norm1, w_small, F32, 128)
    qkv, bg = _c_conv(proj.reshape(batch, seq, main), small.reshape(batch, seq, 128), conv_w, a_log, dt_bias,
                      batch, seq)
    bgt = bg[:, :, :2 * C_HEADS].transpose(0, 2, 1)
    u, w, qg, kg, attn, gc = _c_chunks(qkv, bg, bgt, batch, seq)
    o = _c_scan(u, w, qg, kg, attn, gc, batch, seq)
    return _c_out(o.reshape(batch * seq, C_WIDTH), proj, out_gain, w_out.astype(BF16), x)


def kernel(x, rel_bias, l0_norm1, l0_a_w_in, l0_a_q_gain, l0_a_k_gain, l0_a_w_out, l0_norm2, l0_ffn_w_gate, l0_ffn_w_up, l0_ffn_w_down, l1_norm1, l1_b_w_in, l1_b_q_gain, l1_b_k_gain, l1_b_cmp_pos, l1_b_cmp_w1, l1_b_cmp_w2, l1_b_w_out, l1_norm2, l1_ffn_w_gate, l1_ffn_w_up, l1_ffn_w_down, l2_norm1, l2_c_w_in, l2_c_conv_w, l2_c_a_log, l2_c_dt_bias, l2_c_out_gain, l2_c_w_out, l2_norm2, l2_ffn_w_gate, l2_ffn_w_up, l2_ffn_w_down, l3_norm1, l3_a_w_in, l3_a_q_gain, l3_a_k_gain, l3_a_w_out, l3_norm2, l3_ffn_w_gate, l3_ffn_w_up, l3_ffn_w_down):
    batch, seq, d = x.shape
    h = x.reshape(batch * seq, d)

    def ffn(h, norm2, w_gate, w_up, w_down):
        return _ffn(h, norm2, w_gate.astype(BF16), w_up.astype(BF16), w_down.astype(BF16))

    a_biases = _a_biases(rel_bias)
    h = _mixer_a(h, a_biases, l0_norm1, l0_a_w_in, l0_a_q_gain, l0_a_k_gain, l0_a_w_out, batch, seq)
    h = ffn(h, l0_norm2, l0_ffn_w_gate, l0_ffn_w_up, l0_ffn_w_down)
    h = _mixer_b(h, rel_bias, l1_norm1, l1_b_w_in, l1_b_q_gain, l1_b_k_gain, l1_b_cmp_pos, l1_b_cmp_w1,
                 l1_b_cmp_w2, l1_b_w_out, batch, seq)
    h = ffn(h, l1_norm2, l1_ffn_w_gate, l1_ffn_w_up, l1_ffn_w_down)
    h = _mixer_c(h, l2_norm1, l2_c_w_in, l2_c_conv_w, l2_c_a_log, l2_c_dt_bias, l2_c_out_gain, l2_c_w_out,
                 batch, seq)
    h = ffn(h, l2_norm2, l2_ffn_w_gate, l2_ffn_w_up, l2_ffn_w_down)
    h = _mixer_a(h, a_biases, l3_norm1, l3_a_w_in, l3_a_q_gain, l3_a_k_gain, l3_a_w_out, batch, seq)
    h = ffn(h, l3_norm2, l3_ffn_w_gate, l3_ffn_w_up, l3_ffn_w_down)
    return h.reshape(batch, seq, d)
```

```python
import functools
import math

import numpy as np
import jax
import jax.numpy as jnp
from jax import lax
from jax.experimental import pallas as pl
from jax.experimental.pallas import tpu as pltpu

D_MODEL = 1024
RMS_EPS = 1e-6
NEG_INF = -1e30
TINY = 1e-30
FORCE_SCORE = 1e9

N_BUCKETS = 32
REL_MAX_DISTANCE = 2048
N_HEADS = 16

A_GROUPS = ((128, 1), (512, 4), (2048, 16))
A_HEAD_DIM = 64
A_Q_BLOCK = 128
A_PROJ_TILE = 512
A_STAT_WIDTH = 256
A_BLOCKS_PER_STEP = 8

B_KV_HEADS = 4
B_GROUP = 4
B_HEAD_DIM = 64
B_CMP_LEN = 32
B_CMP_STRIDE = 16
B_CMP_HIDDEN = 256
B_SEL_BLOCK = 64
B_TOP_N = 16
B_WINDOW = 512
B_TILE = 128
B_SWEEP = 256
B_PROJ_WIDTH = 3072
B_GATE_ROWS = 16

C_HEADS = 8
C_HEAD_DIM = 128
C_WIDTH = C_HEADS * C_HEAD_DIM
C_CONV = 4
C_CHUNK = 64
C_GROUP = 4

FFN_HIDDEN = 2816
FFN_TILE = 1024

ROW_TILE = 512
VMEM_LIMIT = 48 * 1024 * 1024

LOG2E = math.log2(math.e)

F32 = jnp.float32
BF16 = jnp.bfloat16

NT_DIMS = (((1,), (1,)), ((), ()))
TN_DIMS = (((0,), (0,)), ((), ()))


def _params(*semantics):
    return pltpu.CompilerParams(dimension_semantics=semantics, vmem_limit_bytes=VMEM_LIMIT)


def _dot(a, b):
    return jnp.dot(a, b, preferred_element_type=F32)


def _dot_nt(a, b):
    return lax.dot_general(a, b, NT_DIMS, preferred_element_type=F32)


def _dot_tn(a, b):
    return lax.dot_general(a, b, TN_DIMS, preferred_element_type=F32)


def _rms(x, gain):
    return x * lax.rsqrt(jnp.mean(x * x, axis=-1, keepdims=True) + RMS_EPS) * gain


def _bucket_thresholds():
    d = np.arange(1 << 15)
    max_exact = N_BUCKETS // 2
    d_f = np.maximum(d, 1).astype(np.float32)
    large = max_exact + (np.log(d_f / np.float32(max_exact)) / np.float32(math.log(REL_MAX_DISTANCE / max_exact))
                         * np.float32(N_BUCKETS - max_exact)).astype(np.int32)
    bucket = np.where(d < max_exact, d, np.minimum(large, N_BUCKETS - 1))
    return [int(np.argmax(bucket >= k)) if np.any(bucket >= k) else int(1 << 30) for k in range(N_BUCKETS)]


_THRESHOLDS = _bucket_thresholds()


def _bias_tile_kernel(tbl_ref, o_ref, *, base, tile_step, row_step, col_step, dmax, dil):
    h = pl.program_id(0)
    t = pl.program_id(1)
    shape = o_ref.shape[2:]
    i = lax.broadcasted_iota(jnp.int32, shape, 0)
    j = lax.broadcasted_iota(jnp.int32, shape, 1)
    dist = base + tile_step * t + row_step * i + col_step * j
    d = dist * dil
    val = jnp.full(shape, tbl_ref[0, h], F32)
    for k in range(1, N_BUCKETS):
        val = jnp.where(d >= _THRESHOLDS[k], tbl_ref[k, h], val)
    valid = (dist >= 0) & (dist <= dmax)
    o_ref[0, 0] = jnp.where(valid, val * LOG2E, NEG_INF).astype(o_ref.dtype)


def _bias_tiles(rel_bias, n_tiles, rows, cols, *, base, tile_step, row_step, col_step, dmax, dil=1, dtype=F32):
    kern = functools.partial(_bias_tile_kernel, base=base, tile_step=tile_step, row_step=row_step,
                             col_step=col_step, dmax=dmax, dil=dil)
    return pl.pallas_call(
        kern,
        out_shape=jax.ShapeDtypeStruct((N_HEADS, n_tiles, rows, cols), dtype),
        grid=(N_HEADS, n_tiles),
        in_specs=[pl.BlockSpec(memory_space=pltpu.SMEM)],
        out_specs=pl.BlockSpec((1, 1, rows, cols), lambda h, t: (h, t, 0, 0)),
        compiler_params=_params("parallel", "parallel"),
        name="bias_tiles",
    )(rel_bias)


def _resident(shape):
    return pl.BlockSpec(shape, lambda i: (0,) * len(shape), pipeline_mode=pl.Buffered(1))


def _norm_matmul_kernel(x_ref, g_ref, w_ref, o_ref, *, tn):
    h = _rms(x_ref[...], g_ref[...]).astype(BF16)
    for j in range(w_ref.shape[1] // tn):
        o_ref[:, j * tn:(j + 1) * tn] = _dot(h, w_ref[:, j * tn:(j + 1) * tn]).astype(o_ref.dtype)


def _norm_matmul(x, gain, w, out_dtype, tn):
    m, d = x.shape
    n = w.shape[1]
    return pl.pallas_call(
        functools.partial(_norm_matmul_kernel, tn=tn),
        out_shape=jax.ShapeDtypeStruct((m, n), out_dtype),
        grid=(m // ROW_TILE,),
        in_specs=[pl.BlockSpec((ROW_TILE, d), lambda i: (i, 0)),
                  _resident((1, d)),
                  _resident((d, n))],
        out_specs=pl.BlockSpec((ROW_TILE, n), lambda i: (i, 0)),
        compiler_params=_params("parallel"),
        name="norm_matmul",
    )(x, gain.reshape(1, d), w)


def _matmul_residual_kernel(a_ref, w_ref, x_ref, o_ref):
    o_ref[...] = x_ref[...] + _dot(a_ref[...], w_ref[...])


def _matmul_residual(a, w, x):
    m, k = a.shape
    d = w.shape[1]
    return pl.pallas_call(
        _matmul_residual_kernel,
        out_shape=jax.ShapeDtypeStruct((m, d), F32),
        grid=(m // ROW_TILE,),
        in_specs=[pl.BlockSpec((ROW_TILE, k), lambda i: (i, 0)),
                  pl.BlockSpec((k, d), lambda i: (0, 0)),
                  pl.BlockSpec((ROW_TILE, d), lambda i: (i, 0))],
        out_specs=pl.BlockSpec((ROW_TILE, d), lambda i: (i, 0)),
        compiler_params=_params("parallel"),
        name="matmul_residual",
    )(a, w, x)


def _ffn_kernel(x_ref, g_ref, wg_ref, wu_ref, wd_ref, o_ref):
    x = x_ref[...]
    h = _rms(x, g_ref[...]).astype(BF16)
    hidden = wg_ref.shape[1]
    acc = x
    for lo in range(0, hidden, FFN_TILE):
        hi = min(lo + FFN_TILE, hidden)
        a = _dot(h, wg_ref[:, lo:hi])
        b = _dot(h, wu_ref[:, lo:hi])
        acc = acc + _dot((a * jax.nn.sigmoid(a) * b).astype(BF16), wd_ref[lo:hi, :])
    o_ref[...] = acc


def _ffn(x, gain, w_gate, w_up, w_down):
    m, d = x.shape
    hidden = w_gate.shape[1]
    return pl.pallas_call(
        _ffn_kernel,
        out_shape=jax.ShapeDtypeStruct((m, d), F32),
        grid=(m // ROW_TILE,),
        in_specs=[pl.BlockSpec((ROW_TILE, d), lambda i: (i, 0)),
                  _resident((1, d)),
                  _resident((d, hidden)), _resident((d, hidden)), _resident((hidden, d))],
        out_specs=pl.BlockSpec((ROW_TILE, d), lambda i: (i, 0)),
        compiler_params=_params("parallel"),
        name="ffn",
    )(x, gain.reshape(1, d), w_gate, w_up, w_down)


def _a_proj_kernel(x_ref, g_ref, w_ref, qg_ref, kg_ref, o_ref, h_ref, x_scr, *, dil):
    rows = ROW_TILE // dil
    xn = _rms(x_ref[...], g_ref[...])
    if dil == 1:
        h_ref[...] = xn.astype(BF16)
    else:
        slabs = xn.shape[1] // 128
        for c in range(slabs):
            x_scr[c] = xn[:, c * 128:(c + 1) * 128]
        for r in range(dil):
            picked = [x_scr[c, pl.ds(r, rows, stride=dil), :] for c in range(slabs)]
            h_ref[r * rows:(r + 1) * rows, :] = jnp.concatenate(picked, axis=1).astype(BF16)
    h = h_ref[...]
    width = w_ref.shape[1]
    hd = width // 3
    low = lax.broadcasted_iota(jnp.int32, (ROW_TILE, 128), 1) < A_HEAD_DIM
    for j in range(width // A_PROJ_TILE):
        res = _dot(h, w_ref[:, j * A_PROJ_TILE:(j + 1) * A_PROJ_TILE])
        kind = (j * A_PROJ_TILE) // hd
        if kind < 2:
            parts = []
            for c in range(A_PROJ_TILE // 128):
                y = res[:, c * 128:(c + 1) * 128]
                sq = y * y
                tot = jnp.sum(sq, axis=-1, keepdims=True)
                lo = jnp.sum(jnp.where(low, sq, 0.0), axis=-1, keepdims=True)
                ss = jnp.where(low, lo, tot - lo)
                parts.append(y * lax.rsqrt(ss * (1.0 / A_HEAD_DIM) + RMS_EPS))
            res = jnp.concatenate(parts, axis=1) * (qg_ref if kind == 0 else kg_ref)[...]
        res = res.astype(BF16)
        for r in range(dil):
            off = r * width + j * A_PROJ_TILE
            o_ref[:, off:off + A_PROJ_TILE] = res[r * rows:(r + 1) * rows]


def _a_proj(x, gain, w, q_gain, k_gain, dil):
    m, d = x.shape
    width = w.shape[1]
    reps = A_PROJ_TILE // A_HEAD_DIM
    qg = jnp.tile(q_gain * (A_HEAD_DIM ** -0.5 * LOG2E), reps).reshape(1, A_PROJ_TILE)
    kg = jnp.tile(k_gain, reps).reshape(1, A_PROJ_TILE)
    return pl.pallas_call(
        functools.partial(_a_proj_kernel, dil=dil),
        out_shape=jax.ShapeDtypeStruct((m // dil, dil * width), BF16),
        grid=(m // ROW_TILE,),
        in_specs=[pl.BlockSpec((ROW_TILE, d), lambda i: (i, 0)),
                  pl.BlockSpec((1, d), lambda i: (0, 0)),
                  pl.BlockSpec((d, width), lambda i: (0, 0)),
                  pl.BlockSpec((1, A_PROJ_TILE), lambda i: (0, 0)),
                  pl.BlockSpec((1, A_PROJ_TILE), lambda i: (0, 0))],
        out_specs=pl.BlockSpec((ROW_TILE // dil, dil * width), lambda i: (i, 0)),
        scratch_shapes=[pltpu.VMEM((ROW_TILE, d), BF16), pltpu.VMEM((d // 128, ROW_TILE, 128), F32)],
        compiler_params=_params("parallel"),
        name="a_proj",
    )(x, gain.reshape(1, d), w, qg, kg)


def _a_attn_kernel(cur_ref, prev_ref, bias_ref, o_ref, stat_ref):
    first = (pl.program_id(2) == 0).astype(jnp.int32)
    nq = A_Q_BLOCK
    hd = N_HEADS * A_HEAD_DIM
    lane = lax.broadcasted_iota(jnp.int32, (nq, 128), 1)
    low = lane < A_HEAD_DIM
    ones = jnp.ones((2 * nq, 128), BF16)
    for res in range(o_ref.shape[2] // hd):
        for sub in range(cur_ref.shape[1] // nq):
            rows = slice(sub * nq, (sub + 1) * nq)
            max_tile = jnp.zeros((nq, 128), F32)
            den_tile = jnp.ones((nq, 128), F32)
            for pair in range(N_HEADS // 2):
                qc, kc, vc = (slice(res * 3 * hd + part * hd + pair * 128, res * 3 * hd + part * hd + (pair + 1) * 128)
                              for part in range(3))
                q = cur_ref[0, rows, qc]
                zero = jnp.zeros_like(q)
                qq = jnp.concatenate([jnp.where(low, q, zero), jnp.where(low, zero, q)], axis=0)
                if sub == 0:
                    kk = jnp.concatenate([prev_ref[0, :, kc], cur_ref[0, :nq, kc]], axis=0)
                    vv = jnp.concatenate([prev_ref[0, :, vc], cur_ref[0, :nq, vc]], axis=0)
                    base = 2 * pair + N_HEADS * first
                else:
                    kk = cur_ref[0, (sub - 1) * nq:(sub + 1) * nq, kc]
                    vv = cur_ref[0, (sub - 1) * nq:(sub + 1) * nq, vc]
                    base = 2 * pair
                s = _dot_nt(qq, kk) + jnp.concatenate([bias_ref[base], bias_ref[base + 1]], axis=0)
                m = jnp.max(s, axis=-1, keepdims=True)
                acc = _dot(jnp.exp2(s - m).astype(BF16), jnp.concatenate([vv, ones], axis=1))
                oc = slice(res * hd + pair * 128, res * hd + (pair + 1) * 128)
                o_ref[0, rows, oc] = jnp.where(low, acc[:nq, :128], acc[nq:, :128]).astype(o_ref.dtype)
                first_head = lane == 2 * pair
                second_head = lane == 2 * pair + 1
                max_tile = jnp.where(first_head, m[:nq], jnp.where(second_head, m[nq:], max_tile))
                den_tile = jnp.where(first_head, acc[:nq, 128:], jnp.where(second_head, acc[nq:, 128:], den_tile))
            stat_ref[0, rows, res * A_STAT_WIDTH:res * A_STAT_WIDTH + 128] = max_tile
            stat_ref[0, rows, res * A_STAT_WIDTH + 128:(res + 1) * A_STAT_WIDTH] = den_tile


def _a_attention(proj, bias, dil, batch, seq):
    length = seq // dil
    nblk = length // A_Q_BLOCK
    hd = N_HEADS * A_HEAD_DIM
    pv = proj.reshape(batch, length, dil * 3 * hd)

    per = min(A_BLOCKS_PER_STEP, nblk)
    res = min(A_BLOCKS_PER_STEP // per, dil)
    step_rows = per * A_Q_BLOCK
    assert nblk % per == 0 and dil % res == 0

    o, stats = pl.pallas_call(
        _a_attn_kernel,
        out_shape=(jax.ShapeDtypeStruct((batch, length, dil * hd), BF16),
                   jax.ShapeDtypeStruct((batch, length, dil * A_STAT_WIDTH), F32)),
        grid=(batch, dil // res, nblk // per),
        in_specs=[pl.BlockSpec((1, step_rows, res * 3 * hd), lambda b, r, i: (b, i, r)),
                  pl.BlockSpec((1, A_Q_BLOCK, res * 3 * hd), lambda b, r, i: (b, jnp.maximum(per * i - 1, 0), r)),
                  pl.BlockSpec((2 * N_HEADS, A_Q_BLOCK, 2 * A_Q_BLOCK), lambda b, r, i: (0, 0, 0))],
        out_specs=(pl.BlockSpec((1, step_rows, res * hd), lambda b, r, i: (b, i, r)),
                   pl.BlockSpec((1, step_rows, res * A_STAT_WIDTH), lambda b, r, i: (b, i, r))),
        compiler_params=_params("parallel", "parallel", "arbitrary"),
        name="a_attention",
    )(pv, pv, bias)
    return o.reshape(batch * length, dil * hd), stats.reshape(batch * length, dil * A_STAT_WIDTH)


def _a_out_kernel(o0_ref, o1_ref, o2_ref, s0_ref, s1_ref, s2_ref, e_ref, w_ref, x_ref, out_ref, o_scr, s_scr):
    hd = N_HEADS * A_HEAD_DIM
    sw = A_STAT_WIDTH
    for g, (o_ref, s_ref) in enumerate(((o0_ref, s0_ref), (o1_ref, s1_ref), (o2_ref, s2_ref))):
        dil = A_GROUPS[g][1]
        rows = ROW_TILE // dil
        for r in range(dil):
            dst = pl.ds(r, rows, stride=dil) if dil > 1 else slice(None)
            s_scr[g, 0, dst, :] = s_ref[:, r * sw:r * sw + 128]
            s_scr[g, 1, dst, :] = s_ref[:, r * sw + 128:(r + 1) * sw]
            for c in range(hd // 128):
                o_scr[g, c, dst, :] = o_ref[:, r * hd + c * 128:r * hd + (c + 1) * 128].astype(F32)
    groups = range(len(A_GROUPS))
    top = functools.reduce(jnp.maximum, [s_scr[g, 0] for g in groups])
    es = [jnp.exp2(s_scr[g, 0] - top) for g in groups]
    inv = 1.0 / sum(es[g] * s_scr[g, 1] for g in groups)
    expand = e_ref[...]
    acc = None
    for g in groups:
        o_g = jnp.concatenate([o_scr[g, c] for c in range(hd // 128)], axis=1)
        term = _dot((es[g] * inv).astype(BF16), expand) * o_g
        acc = term if acc is None else acc + term
    out_ref[...] = x_ref[...] + _dot(acc.astype(BF16), w_ref[...])


def _a_out(outs, stats, w_out, x):
    m, d = x.shape
    hd = N_HEADS * A_HEAD_DIM
    expand = np.zeros((128, hd), np.float32)
    for h in range(N_HEADS):
        expand[h, h * A_HEAD_DIM:(h + 1) * A_HEAD_DIM] = 1.0
    grouped = lambda width: [pl.BlockSpec((ROW_TILE // dil, dil * width), lambda i: (i, 0)) for _, dil in A_GROUPS]
    return pl.pallas_call(
        _a_out_kernel,
        out_shape=jax.ShapeDtypeStruct((m, d), F32),
        grid=(m // ROW_TILE,),
        in_specs=grouped(hd) + grouped(A_STAT_WIDTH) + [
            pl.BlockSpec((128, hd), lambda i: (0, 0)),
            pl.BlockSpec((hd, d), lambda i: (0, 0)),
            pl.BlockSpec((ROW_TILE, d), lambda i: (i, 0))],
        out_specs=pl.BlockSpec((ROW_TILE, d), lambda i: (i, 0)),
        scratch_shapes=[pltpu.VMEM((len(A_GROUPS), hd // 128, ROW_TILE, 128), F32),
                        pltpu.VMEM((len(A_GROUPS), 2, ROW_TILE, 128), F32)],
        compiler_params=_params("parallel"),
        name="a_out",
    )(*outs, *stats, jnp.asarray(expand, BF16), w_out, x)


def _a_biases(rel_bias):
    biases = []
    for window, dil in A_GROUPS:
        steps = window // dil
        assert steps == A_Q_BLOCK
        bias = _bias_tiles(rel_bias, 1, A_Q_BLOCK, 2 * A_Q_BLOCK, base=A_Q_BLOCK, tile_step=0, row_step=1,
                           col_step=-1, dmax=steps, dil=dil)[:, 0]
        biases.append(jnp.concatenate([bias, bias.at[:, :, :A_Q_BLOCK].set(NEG_INF)], axis=0))
    return biases


def _mixer_a(x, biases, norm1, w_in, q_gain, k_gain, w_out, batch, seq):
    w_in = w_in.astype(BF16)
    group_width = 3 * N_HEADS * A_HEAD_DIM
    outs, stats = [], []
    for gi, (_, dil) in enumerate(A_GROUPS):
        assert (seq // dil) % A_Q_BLOCK == 0 and seq % ROW_TILE == 0
        bias = biases[gi]
        proj = _a_proj(x, norm1, w_in[:, gi * group_width:(gi + 1) * group_width], q_gain[gi], k_gain[gi], dil)
        o, stat = _a_attention(proj, bias, dil, batch, seq)
        outs.append(o)
        stats.append(stat)
    return _a_out(outs, stats, w_out.astype(BF16), x)


def _b_prep_kernel(p_ref, qg_ref, kg_ref, qt_ref, ck_ref, cv_ref, ka_ref, vs_ref, wk_ref, vw_ref, gate_ref):
    dh = B_HEAD_DIM
    ts = p_ref.shape[1]
    n_sel = ka_ref.shape[3] - dh
    low = lax.broadcasted_iota(jnp.int32, (ts, 128), 1) < dh

    def slab(c):
        return p_ref[0, :, c * 128:(c + 1) * 128].astype(F32)

    def normed(x, gain):
        sq = x * x
        tot = jnp.sum(sq, axis=-1, keepdims=True)
        first = jnp.sum(jnp.where(low, sq, 0.0), axis=-1, keepdims=True)
        ss = jnp.where(low, first, tot - first)
        return x * lax.rsqrt(ss * (1.0 / dh) + RMS_EPS) * jnp.concatenate([gain, gain], axis=1)

    def halves(x):
        return x[:, :dh], x[:, dh:]

    qg = qg_ref[...] * (dh ** -0.5 * LOG2E)
    for c in range(N_HEADS // 2):
        xt = normed(slab(c), qg).T.astype(BF16)
        qt_ref[0, 2 * c] = xt[:dh]
        qt_ref[0, 2 * c + 1] = xt[dh:]
    base = N_HEADS // 2
    pairs = B_KV_HEADS // 2
    pos = pl.program_id(1) * ts + lax.broadcasted_iota(jnp.int32, (ts, 128), 0)
    lane = lax.broadcasted_iota(jnp.int32, (ts, 128), 1)
    onehot = jnp.where(lane - dh == pos // B_SEL_BLOCK, 1.0, 0.0)
    ones = jnp.ones((vs_ref.shape[2] - dh, ts), BF16)
    for j in range(pairs):
        for ref, off in ((ck_ref, 0), (cv_ref, pairs)):
            a, b = halves(slab(base + off + j))
            ref[0, 2 * j] = a.astype(BF16)
            ref[0, 2 * j + 1] = b.astype(BF16)
        k_sel = normed(slab(base + 2 * pairs + j), kg_ref[1:2, :])
        for n, k in zip((2 * j, 2 * j + 1), (k_sel, pltpu.roll(k_sel, dh, 1))):
            ka_ref[0, n] = jnp.where(low, k, onehot)[:, :dh + n_sel].astype(BF16)
        for n, k in zip((2 * j, 2 * j + 1), halves(normed(slab(base + 4 * pairs + j), kg_ref[2:3, :]))):
            wk_ref[0, n] = k.astype(BF16)
        for ref, off in ((vs_ref, 3 * pairs), (vw_ref, 5 * pairs)):
            xt = slab(base + off + j).T.astype(BF16)
            for n, v in zip((2 * j, 2 * j + 1), (xt[:dh], xt[dh:])):
                ref[0, n, :dh, :] = v
                ref[0, n, dh:, :] = ones
    gate = jax.nn.sigmoid(slab(base + 6 * pairs)).T
    rows = gate_ref.shape[2]
    for n in range(B_KV_HEADS):
        gate_ref[0, n] = gate[n * rows:(n + 1) * rows]


def _b_prep(proj, q_gain, k_gain, batch, seq):
    ts = ROW_TILE
    dh = B_HEAD_DIM
    n_sel = seq // B_SEL_BLOCK
    assert dh + n_sel <= 128
    rows_shape = lambda width: jax.ShapeDtypeStruct((batch, B_KV_HEADS, seq, width), BF16)
    rows_spec = lambda width: pl.BlockSpec((1, B_KV_HEADS, ts, width), lambda b, i: (b, 0, i, 0))
    cols_shape = lambda heads, height, dtype: jax.ShapeDtypeStruct((batch, heads, height, seq), dtype)
    cols_spec = lambda heads, height: pl.BlockSpec((1, heads, height, ts), lambda b, i: (b, 0, 0, i))
    return pl.pallas_call(
        _b_prep_kernel,
        out_shape=(cols_shape(N_HEADS, dh, BF16), rows_shape(dh), rows_shape(dh), rows_shape(dh + n_sel),
                   cols_shape(B_KV_HEADS, dh + 16, BF16), rows_shape(dh), cols_shape(B_KV_HEADS, dh + 16, BF16),
                   cols_shape(B_KV_HEADS, B_GATE_ROWS, F32)),
        grid=(batch, seq // ts),
        in_specs=[pl.BlockSpec((1, ts, B_PROJ_WIDTH), lambda b, i: (b, i, 0)),
                  pl.BlockSpec((1, dh), lambda b, i: (0, 0)),
                  pl.BlockSpec((3, dh), lambda b, i: (0, 0))],
        out_specs=(cols_spec(N_HEADS, dh), rows_spec(dh), rows_spec(dh), rows_spec(dh + n_sel),
                   cols_spec(B_KV_HEADS, dh + 16), rows_spec(dh), cols_spec(B_KV_HEADS, dh + 16),
                   cols_spec(B_KV_HEADS, B_GATE_ROWS)),
        compiler_params=_params("parallel", "parallel"),
        name="b_prep",
    )(proj, q_gain.reshape(1, dh), k_gain)


def _b_compress_kernel(tk_ref, tv_ref, pos_ref, w1_ref, w2_ref, kg_ref, kc_ref, vc_ref):
    half = (B_CMP_LEN // 2) * B_HEAD_DIM
    for kv, (t_ref, out_ref) in enumerate(((tk_ref, kc_ref), (tv_ref, vc_ref))):
        t = t_ref[0, 0].astype(F32)
        top = (t + pos_ref[kv, 0:1, :]).astype(BF16)
        bot = (t + pos_ref[kv, 1:2, :]).astype(BF16)
        a1 = _dot(top, w1_ref[kv, :half, :])
        a2 = _dot(bot, w1_ref[kv, half:, :])
        hidden = a1 + pltpu.roll(a2, a2.shape[0] - 1, 0)
        out = _dot(jax.nn.gelu(hidden).astype(BF16), w2_ref[kv])
        if kv == 0:
            out = _rms(out, kg_ref[...])
        out_ref[0, 0] = out.astype(out_ref.dtype)


def _b_compress(ck, cv, cmp_pos, cmp_w1, cmp_w2, k_gain0, batch, seq):
    rows = seq // B_CMP_STRIDE
    half = (B_CMP_LEN // 2) * B_HEAD_DIM
    tk = ck.reshape(batch, B_KV_HEADS, rows, half)
    tv = cv.reshape(batch, B_KV_HEADS, rows, half)
    pos = cmp_pos.reshape(2, 2, half)
    t_spec = pl.BlockSpec((1, 1, rows, half), lambda b, n: (b, n, 0, 0))
    o_spec = pl.BlockSpec((1, 1, rows, B_HEAD_DIM), lambda b, n: (b, n, 0, 0))
    shape = jax.ShapeDtypeStruct((batch, B_KV_HEADS, rows, B_HEAD_DIM), BF16)
    return pl.pallas_call(
        _b_compress_kernel,
        out_shape=(shape, shape),
        grid=(batch, B_KV_HEADS),
        in_specs=[t_spec, t_spec,
                  pl.BlockSpec((2, 2, half), lambda b, n: (0, 0, 0)),
                  pl.BlockSpec((2, 2 * half, B_CMP_HIDDEN), lambda b, n: (0, 0, 0)),
                  pl.BlockSpec((2, B_CMP_HIDDEN, B_HEAD_DIM), lambda b, n: (0, 0, 0)),
                  pl.BlockSpec((1, B_HEAD_DIM), lambda b, n: (0, 0))],
        out_specs=(o_spec, o_spec),
        compiler_params=_params("parallel", "parallel"),
        name="b_compress",
    )(tk, tv, pos, cmp_w1.astype(BF16), cmp_w2.astype(BF16), k_gain0.reshape(1, -1))


def _b_cmp_attn_kernel(qt_ref, kc_ref, vct_ref, bias_ref, c2s_ref, oc_ref, sel_ref, imp_ref, *, top_n):
    tq = B_SWEEP
    n_sel = imp_ref.shape[0]
    qt = jnp.concatenate([qt_ref[0, g] for g in range(B_GROUP)], axis=1)
    n_cmp_pad = kc_ref.shape[2]
    shift = tq // B_CMP_STRIDE
    off = pl.multiple_of((pl.num_programs(2) - 1 - pl.program_id(2)) * shift, shift)
    bias = jnp.concatenate([bias_ref[g, 0, pl.ds(off, n_cmp_pad), :] for g in range(B_GROUP)], axis=1)
    s = _dot(kc_ref[0, 0], qt) + bias
    m = jnp.max(s, axis=0, keepdims=True)
    e = jnp.exp2(s - m)
    z = jnp.maximum(jnp.sum(e, axis=0, keepdims=True), TINY)
    pos = pl.program_id(2) * tq + lax.broadcasted_iota(jnp.int32, (1, tq), 1)
    sees_any = jnp.concatenate([pos >= B_CMP_LEN - 1] * B_GROUP, axis=1)
    p = e * jnp.where(sees_any, 1.0 / z, 0.0)
    oct = _dot(vct_ref[0, 0], p.astype(BF16))
    for g in range(B_GROUP):
        oc_ref[0, g] = oct[:, g * tq:(g + 1) * tq]

    p_sum = p[:, 0:tq] + p[:, tq:2 * tq] + p[:, 2 * tq:3 * tq] + p[:, 3 * tq:4 * tq]
    hi = p_sum.astype(BF16)
    lo = (p_sum - hi.astype(F32)).astype(BF16)
    c2s = c2s_ref[...]
    imp = _dot(c2s, hi) + _dot(c2s, lo)

    t = pl.program_id(2) * tq + lax.broadcasted_iota(jnp.int32, (n_sel, tq), 1)
    blk = lax.broadcasted_iota(jnp.int32, (n_sel, tq), 0)
    cur = t // B_SEL_BLOCK
    forced = (blk == 0) | (blk == cur) | (blk == cur - 1)
    imp = jnp.where(forced, FORCE_SCORE, jnp.where(blk * B_SEL_BLOCK <= t, imp, NEG_INF))
    imp_ref[...] = imp

    def count(i, rank):
        row = imp_ref[pl.ds(i, 1), :]
        ahead = jnp.where(row > imp, 1.0, jnp.where(row == imp, jnp.where(blk > i, 1.0, 0.0), 0.0))
        return rank + ahead

    n_live = jnp.minimum(n_sel, (pl.program_id(2) + 1) * (tq // B_SEL_BLOCK))
    rank = lax.fori_loop(0, n_live, count, jnp.zeros((n_sel, tq), F32))
    sel_ref[0, 0] = jnp.where(rank < top_n, 0.0, NEG_INF).astype(sel_ref.dtype)


def _b_cmp_attn(qt, kc, vc, bias_c, batch, seq):
    n_sel = seq // B_SEL_BLOCK
    n_cmp_pad = seq // B_CMP_STRIDE
    n_cmp = (seq - B_CMP_LEN) // B_CMP_STRIDE + 1
    c = np.arange(n_cmp_pad)[None, :] * B_CMP_STRIDE
    j = np.arange(n_sel)[:, None] * B_SEL_BLOCK
    c2s = ((c < j + B_SEL_BLOCK) & (c + B_CMP_LEN > j) & (np.arange(n_cmp_pad)[None, :] < n_cmp)).astype(np.float32)
    kern = functools.partial(_b_cmp_attn_kernel, top_n=min(B_TOP_N, n_sel))
    return pl.pallas_call(
        kern,
        out_shape=(jax.ShapeDtypeStruct((batch, N_HEADS, B_HEAD_DIM, seq), F32),
                   jax.ShapeDtypeStruct((batch, B_KV_HEADS, n_sel, seq), BF16)),
        grid=(batch, B_KV_HEADS, seq // B_SWEEP),
        in_specs=[pl.BlockSpec((1, B_GROUP, B_HEAD_DIM, B_SWEEP), lambda b, n, i: (b, n, 0, i)),
                  pl.BlockSpec((1, 1, n_cmp_pad, B_HEAD_DIM), lambda b, n, i: (b, n, 0, 0)),
                  pl.BlockSpec((1, 1, B_HEAD_DIM, n_cmp_pad), lambda b, n, i: (b, n, 0, 0)),
                  pl.BlockSpec((B_GROUP, 1, bias_c.shape[2], B_SWEEP), lambda b, n, i: (n, 0, 0, 0)),
                  pl.BlockSpec((n_sel, n_cmp_pad), lambda b, n, i: (0, 0))],
        out_specs=(pl.BlockSpec((1, B_GROUP, B_HEAD_DIM, B_SWEEP), lambda b, n, i: (b, n, 0, i)),
                   pl.BlockSpec((1, 1, n_sel, B_SWEEP), lambda b, n, i: (b, n, 0, i))),
        scratch_shapes=[pltpu.VMEM((n_sel, B_SWEEP), F32)],
        compiler_params=_params("parallel", "parallel", "arbitrary"),
        name="b_cmp_attn",
    )(qt, kc, vc.transpose(0, 1, 3, 2), bias_c, jnp.asarray(c2s, BF16))


def _b_sparse_kernel(qt_ref, ka_ref, vs_ref, wk_ref, vw_ref, sel_ref, bs_ref, bw_ref, oc_ref, gate_ref,
                     o_ref, acc_ref, sa_ref, sb_ref, *, delta_max, win_tiles):
    tq = B_SWEEP
    dh = B_HEAD_DIM
    cols = B_GROUP * tq
    qi = pl.program_id(2)
    n_tiles = ka_ref.shape[2] // tq
    qt = jnp.concatenate([qt_ref[0, g] for g in range(B_GROUP)], axis=1)
    q_aug = jnp.concatenate([qt, jnp.concatenate([sel_ref[0, 0]] * B_GROUP, axis=1)], axis=0)

    def tile_start(kt):
        return pl.multiple_of(jnp.clip(kt, 0, n_tiles - 1) * tq, tq)

    def normalised(acc):
        return acc[:dh] * (1.0 / acc[dh:dh + 1])

    def sel_bias(kt):
        d = jnp.clip(qi - kt, -1, delta_max) + 1
        return jnp.concatenate([bs_ref[g, d] for g in range(B_GROUP)], axis=1)

    def sel_scores(kt):
        return _dot(ka_ref[0, 0, pl.ds(tile_start(kt), tq), :], q_aug).astype(BF16) + sel_bias(kt)

    def consume(s_buf, kt, m_old):
        s = s_buf[...]
        m_new = jnp.maximum(m_old, jnp.max(s, axis=0, keepdims=True).astype(F32))
        alpha = jnp.exp2(m_old - m_new)
        p = jnp.exp2(s - m_new.astype(BF16))
        acc_ref[...] = alpha * acc_ref[...] + _dot(vs_ref[0, 0, :, pl.ds(tile_start(kt), tq)], p)
        return m_new

    acc_ref[...] = jnp.zeros(acc_ref.shape, F32)
    sa_ref[...] = sel_scores(0)

    def pair(kt, m):
        sb_ref[...] = sel_scores(kt + 1)
        m = consume(sa_ref, kt, m)
        sa_ref[...] = sel_scores(kt + 2)
        return consume(sb_ref, kt + 1, m)

    n_pairs = (qi + 2) // 2
    n_double = n_pairs // 2
    m = lax.fori_loop(0, n_double, lambda j, m: pair(4 * j + 2, pair(4 * j, m)),
                      jnp.full((1, cols), NEG_INF, F32))
    lax.fori_loop(2 * n_double, n_pairs, lambda j, m: pair(2 * j, m), m)
    o_s = normalised(acc_ref[...])

    tiles = []
    for u in range(win_tiles):
        kt = qi - (win_tiles - 1) + u
        d = jnp.where(kt >= 0, qi - kt, -1) + 1
        bias = jnp.concatenate([bw_ref[g, d] for g in range(B_GROUP)], axis=1)
        tiles.append((_dot(wk_ref[0, 0, pl.ds(tile_start(kt), tq), :], qt).astype(BF16) + bias, kt))
    m = None
    for s, _ in tiles:
        tile_max = jnp.max(s, axis=0, keepdims=True)
        m = tile_max if m is None else jnp.maximum(m, tile_max)
    acc = None
    for s, kt in tiles:
        pv = _dot(vw_ref[0, 0, :, pl.ds(tile_start(kt), tq)], jnp.exp2(s - m))
        acc = pv if acc is None else acc + pv
    o_w = normalised(acc)

    gate = gate_ref[0, 0]
    merged = []
    for g in range(B_GROUP):
        cs = slice(g * tq, (g + 1) * tq)
        merged.append(gate[3 * g:3 * g + 1] * oc_ref[0, g] + gate[3 * g + 1:3 * g + 2] * o_s[:, cs]
                      + gate[3 * g + 2:3 * g + 3] * o_w[:, cs])
    for pair in range(B_GROUP // 2):
        both = jnp.concatenate([merged[2 * pair], merged[2 * pair + 1]], axis=0)
        o_ref[0, :, pair * 2 * dh:(pair + 1) * 2 * dh] = both.T.astype(o_ref.dtype)


def _b_sparse(qt, ka, vs, wk, vw, sel, bias_s, bias_w, oc, gate, batch, seq):
    n_sel = seq // B_SEL_BLOCK
    dh = B_HEAD_DIM
    n_ds = bias_s.shape[1]
    n_dw = bias_w.shape[1]
    vrows = vs.shape[2]
    kern = functools.partial(_b_sparse_kernel, delta_max=n_ds - 2, win_tiles=n_dw - 1)
    whole = lambda rows, width: pl.BlockSpec((1, 1, rows, width), lambda b, n, i: (b, n, 0, 0))
    return pl.pallas_call(
        kern,
        out_shape=jax.ShapeDtypeStruct((batch, seq, N_HEADS * dh), BF16),
        grid=(batch, B_KV_HEADS, seq // B_SWEEP),
        in_specs=[pl.BlockSpec((1, B_GROUP, dh, B_SWEEP), lambda b, n, i: (b, n, 0, i)),
                  whole(seq, dh + n_sel), whole(vrows, seq), whole(seq, dh), whole(vrows, seq),
                  pl.BlockSpec((1, 1, n_sel, B_SWEEP), lambda b, n, i: (b, n, 0, i)),
                  pl.BlockSpec((B_GROUP, n_ds, B_SWEEP, B_SWEEP), lambda b, n, i: (n, 0, 0, 0)),
                  pl.BlockSpec((B_GROUP, n_dw, B_SWEEP, B_SWEEP), lambda b, n, i: (n, 0, 0, 0)),
                  pl.BlockSpec((1, B_GROUP, dh, B_SWEEP), lambda b, n, i: (b, n, 0, i)),
                  pl.BlockSpec((1, 1, B_GATE_ROWS, B_SWEEP), lambda b, n, i: (b, n, 0, i))],
        out_specs=pl.BlockSpec((1, B_SWEEP, B_GROUP * dh), lambda b, n, i: (b, i, n)),
        scratch_shapes=[pltpu.VMEM((vrows, B_GROUP * B_SWEEP), F32),
                        pltpu.VMEM((B_SWEEP, B_GROUP * B_SWEEP), BF16),
                        pltpu.VMEM((B_SWEEP, B_GROUP * B_SWEEP), BF16)],
        compiler_params=_params("parallel", "parallel", "arbitrary"),
        name="b_sparse",
    )(qt, ka, vs, wk, vw, sel, bias_s, bias_w, oc, gate)


def _mixer_b(x, rel_bias, norm1, w_in, q_gain, k_gain, cmp_pos, cmp_w1, cmp_w2, w_out, batch, seq):
    d = w_in.shape[0]
    qkv_width = w_in.shape[1] - 3 * N_HEADS
    gate_w = w_in[:, qkv_width:].reshape(d, B_KV_HEADS, 3 * B_GROUP)
    gate_w = jnp.pad(gate_w, ((0, 0), (0, 0), (0, B_GATE_ROWS - 3 * B_GROUP))).reshape(d, B_KV_HEADS * B_GATE_ROWS)
    w_pad = jnp.concatenate([w_in[:, :qkv_width], gate_w], axis=1)
    w_pad = jnp.pad(w_pad, ((0, 0), (0, B_PROJ_WIDTH - w_pad.shape[1]))).astype(BF16)
    proj = _norm_matmul(x, norm1, w_pad, BF16, 512).reshape(batch, seq, B_PROJ_WIDTH)
    qt, ck, cv, ka, vs, wk, vw, gate_t = _b_prep(proj, q_gain, k_gain, batch, seq)
    kc, vc = _b_compress(ck, cv, cmp_pos, cmp_w1, cmp_w2, k_gain[0], batch, seq)
    last_tile = seq // B_SWEEP - 1
    bias_c = _bias_tiles(rel_bias, 1, seq // B_CMP_STRIDE + last_tile * (B_SWEEP // B_CMP_STRIDE), B_SWEEP,
                         base=1 - B_CMP_LEN + last_tile * B_SWEEP, tile_step=0, row_step=-B_CMP_STRIDE,
                         col_step=1, dmax=1 << 30)
    oc, sel = _b_cmp_attn(qt, kc, vc, bias_c, batch, seq)
    delta_max = min(seq // B_SWEEP - 1, -(-(_THRESHOLDS[-1] + B_SWEEP - 1) // B_SWEEP))
    bias_s = _bias_tiles(rel_bias, delta_max + 2, B_SWEEP, B_SWEEP, base=-B_SWEEP, tile_step=B_SWEEP,
                         row_step=-1, col_step=1, dmax=1 << 30, dtype=BF16)
    win_tiles = (B_WINDOW - 1 + B_SWEEP - 1) // B_SWEEP + 1
    bias_w = _bias_tiles(rel_bias, win_tiles + 1, B_SWEEP, B_SWEEP, base=-B_SWEEP, tile_step=B_SWEEP,
                         row_step=-1, col_step=1, dmax=B_WINDOW - 1, dtype=BF16)
    o = _b_sparse(qt, ka, vs, wk, vw, sel, bias_s, bias_w, oc, gate_t, batch, seq)
    return _matmul_residual(o.reshape(batch * seq, -1), w_out.astype(BF16), x)


def _c_conv_kernel(cur_ref, halo_ref, w_ref, sm_ref, alog_ref, dtb_ref, qkv_ref, bg_ref, xe_ref):
    ts = cur_ref.shape[1]
    keep = jnp.where(pl.program_id(1) == 0, 0.0, 1.0)
    dk = C_HEAD_DIM
    for c in range(3 * C_HEADS):
        sl = slice(c * dk, (c + 1) * dk)
        xe_ref[c, :8, :] = halo_ref[0, :, sl].astype(F32) * keep
        xe_ref[c, 8:, :] = cur_ref[0, :, sl].astype(F32)
        y = None
        for j in range(C_CONV):
            off = 8 - (C_CONV - 1) + j
            term = w_ref[j:j + 1, sl] * xe_ref[c, off:off + ts, :]
            y = term if y is None else y + term
        y = y * jax.nn.sigmoid(y)
        if c < 2 * C_HEADS:
            y = y * lax.rsqrt(jnp.sum(y * y, axis=-1, keepdims=True) + RMS_EPS)
        if c < C_HEADS:
            y = y * (dk ** -0.5)
        qkv_ref[0, :, sl] = y.astype(qkv_ref.dtype)
    sm = sm_ref[0]
    a = sm + dtb_ref[...]
    softplus = jnp.maximum(a, 0.0) + jnp.log1p(jnp.exp(-jnp.abs(a)))
    g = -jnp.exp(alog_ref[...]) * softplus
    lane = lax.broadcasted_iota(jnp.int32, sm.shape, 1)
    bg_ref[0] = jnp.where(lane < C_HEADS, jax.nn.sigmoid(sm), g)


def _c_conv(proj, small, conv_w, a_log, dt_bias, batch, seq):
    ts = ROW_TILE
    width = 3 * C_WIDTH
    pad = lambda v: jnp.pad(v, (C_HEADS, 128 - 2 * C_HEADS)).reshape(1, 128)
    return pl.pallas_call(
        _c_conv_kernel,
        out_shape=(jax.ShapeDtypeStruct((batch, seq, width), BF16),
                   jax.ShapeDtypeStruct((batch, seq, 128), F32)),
        grid=(batch, seq // ts),
        in_specs=[pl.BlockSpec((1, ts, width), lambda b, i: (b, i, 0)),
                  pl.BlockSpec((1, 8, width), lambda b, i: (b, jnp.maximum(i * (ts // 8) - 1, 0), 0)),
                  pl.BlockSpec((C_CONV, width), lambda b, i: (0, 0)),
                  pl.BlockSpec((1, ts, 128), lambda b, i: (b, i, 0)),
                  pl.BlockSpec((1, 128), lambda b, i: (0, 0)),
                  pl.BlockSpec((1, 128), lambda b, i: (0, 0))],
        out_specs=(pl.BlockSpec((1, ts, width), lambda b, i: (b, i, 0)),
                   pl.BlockSpec((1, ts, 128), lambda b, i: (b, i, 0))),
        scratch_shapes=[pltpu.VMEM((3 * C_HEADS, ts + 8, C_HEAD_DIM), F32)],
        compiler_params=_params("parallel", "arbitrary"),
        name="c_conv",
    )(proj, proj, conv_w, small, pad(a_log), pad(dt_bias))


def _sum3(x, fn):
    hi = x.astype(BF16)
    r = x - hi.astype(F32)
    mid = r.astype(BF16)
    lo = (r - mid.astype(F32)).astype(BF16)
    return fn(hi) + (fn(mid) + fn(lo))


def _c_chunk_kernel(qkv_ref, bg_ref, bgt_ref, tri_ref, trit_ref, blk_ref, u_ref, w_ref, qg_ref, kg_ref, attn_ref,
                    gc_ref):
    cs = C_CHUNK
    dk = C_HEAD_DIM
    gs = C_GROUP * cs
    row = lax.broadcasted_iota(jnp.int32, (gs, gs), 0)
    col = lax.broadcasted_iota(jnp.int32, (gs, gs), 1)
    same = (row // cs) == (col // cs)
    causal = same & (row >= col)
    strict = same & (row > col)
    eye = jnp.where(row == col, 1.0, 0.0)

    bgc = bg_ref[0]
    tri = tri_ref[...]
    gcum_col = _sum3(bgc, lambda p: _dot(tri, p))
    glast_col = _sum3(bgc, lambda p: _dot(blk_ref[...], p))
    gcum_row = _sum3(bgt_ref[0], lambda p: _dot(p, trit_ref[...]))
    gc_ref[0] = gcum_col
    t_mats, powers = [], []
    for h in range(C_HEADS):
        gc = gcum_col[:, C_HEADS + h:C_HEADS + h + 1]
        gr = gcum_row[C_HEADS + h:C_HEADS + h + 1, :]
        q = qkv_ref[0, :, h * dk:(h + 1) * dk]
        k = qkv_ref[0, :, C_WIDTH + h * dk:C_WIDTH + (h + 1) * dk]
        decay = jnp.exp(jnp.where(causal, gc - gr, NEG_INF))
        k16 = k.astype(BF16)
        low = jnp.where(strict, _dot_nt((k * bgc[:, h:h + 1]).astype(BF16), k16) * decay, 0.0)
        t_mats.append(eye - low)
        powers.append(low.astype(BF16))
        attn = jnp.where(causal, _dot_nt(q.astype(BF16), k16), 0.0) * decay
        attn_ref[0, :, h * gs:(h + 1) * gs] = attn.astype(attn_ref.dtype)
        qg_ref[0, :, h * dk:(h + 1) * dk] = (q * jnp.exp(gc)).astype(qg_ref.dtype)
        glast = glast_col[:, C_HEADS + h:C_HEADS + h + 1]
        kg_ref[0, :, h * dk:(h + 1) * dk] = (k * jnp.exp(glast - gc)).astype(kg_ref.dtype)
    for _ in range(int(math.log2(cs)) - 1):
        powers = [_dot(p, p).astype(BF16) for p in powers]
        t_mats = [t + _dot(t.astype(BF16), p) for t, p in zip(t_mats, powers)]
    for h in range(C_HEADS):
        gc = gcum_col[:, C_HEADS + h:C_HEADS + h + 1]
        beta = bgc[:, h:h + 1]
        k = qkv_ref[0, :, C_WIDTH + h * dk:C_WIDTH + (h + 1) * dk]
        v = qkv_ref[0, :, 2 * C_WIDTH + h * dk:2 * C_WIDTH + (h + 1) * dk]
        rhs = jnp.concatenate([(v * beta).astype(BF16), (k * beta * jnp.exp(gc)).astype(BF16)], axis=1)
        both = _dot(t_mats[h].astype(BF16), rhs)
        u_ref[0, :, h * dk:(h + 1) * dk] = both[:, :dk].astype(u_ref.dtype)
        w_ref[0, :, h * dk:(h + 1) * dk] = both[:, dk:].astype(w_ref.dtype)


def _c_chunks(qkv, bg, bgt, batch, seq):
    gs = C_GROUP * C_CHUNK
    idx = np.arange(gs)
    same = (idx[:, None] // C_CHUNK) == (idx[None, :] // C_CHUNK)
    tri = (same & (idx[:, None] >= idx[None, :])).astype(np.float32)
    wide = lambda width: pl.BlockSpec((1, gs, width), lambda b, i: (b, i, 0))
    shape = lambda width, dtype: jax.ShapeDtypeStruct((batch, seq, width), dtype)
    const = pl.BlockSpec((gs, gs), lambda b, i: (0, 0))
    return pl.pallas_call(
        _c_chunk_kernel,
        out_shape=(shape(C_WIDTH, BF16), shape(C_WIDTH, BF16), shape(C_WIDTH, BF16), shape(C_WIDTH, BF16),
                   shape(C_HEADS * gs, BF16), shape(128, F32)),
        grid=(batch, seq // gs),
        in_specs=[wide(3 * C_WIDTH), wide(128),
                  pl.BlockSpec((1, 2 * C_HEADS, gs), lambda b, i: (b, 0, i)),
                  const, const, const],
        out_specs=(wide(C_WIDTH),) * 4 + (wide(C_HEADS * gs), wide(128)),
        compiler_params=_params("parallel", "parallel"),
        name="c_chunks",
    )(qkv, bg, bgt, jnp.asarray(tri, BF16), jnp.asarray(tri.T, BF16), jnp.asarray(same, BF16))


def _c_scan_kernel(u_ref, w_ref, qg_ref, kg_ref, attn_ref, gc_ref, o_ref, state_ref, vnew_ref):
    @pl.when(pl.program_id(1) == 0)
    def _():
        state_ref[...] = jnp.zeros_like(state_ref)

    cs = C_CHUNK
    dk = C_HEAD_DIM
    gs = C_GROUP * cs
    vnew_ref[...] = jnp.zeros_like(vnew_ref)
    heads = range(C_HEADS)
    cols = [slice(h * dk, (h + 1) * dk) for h in heads]
    states = [state_ref[h] for h in heads]
    for c in range(C_GROUP):
        rs = slice(c * cs, (c + 1) * cs)
        decay_last = jnp.exp(gc_ref[0, (c + 1) * cs - 1:(c + 1) * cs, :])
        both = [_dot(jnp.concatenate([w_ref[0, rs, cols[h]], qg_ref[0, rs, cols[h]]], axis=0),
                     states[h].astype(BF16)) for h in heads]
        v16 = [(u_ref[0, rs, cols[h]] - both[h][:cs]).astype(BF16) for h in heads]
        for h in heads:
            vnew_ref[h, rs, :] = v16[h]
        for h in heads:
            out = both[h][cs:] + _dot(attn_ref[0, rs, h * gs:(h + 1) * gs], vnew_ref[h])
            o_ref[0, rs, cols[h]] = out.astype(o_ref.dtype)
        states = [states[h] * decay_last[:, C_HEADS + h:C_HEADS + h + 1] + _dot_tn(kg_ref[0, rs, cols[h]], v16[h])
                  for h in heads]
    for h in heads:
        state_ref[h] = states[h]


def _c_scan(u, w, qg, kg, attn, gc, batch, seq):
    gs = C_GROUP * C_CHUNK
    wide = lambda width: pl.BlockSpec((1, gs, width), lambda b, c: (b, c, 0))
    return pl.pallas_call(
        _c_scan_kernel,
        out_shape=jax.ShapeDtypeStruct((batch, seq, C_WIDTH), BF16),
        grid=(batch, seq // gs),
        in_specs=[wide(C_WIDTH)] * 4 + [wide(C_HEADS * gs), wide(128)],
        out_specs=wide(C_WIDTH),
        scratch_shapes=[pltpu.VMEM((C_HEADS, C_HEAD_DIM, C_HEAD_DIM), F32),
                        pltpu.VMEM((C_HEADS, gs, C_HEAD_DIM), BF16)],
        compiler_params=_params("parallel", "arbitrary"),
        name="c_scan",
    )(u, w, qg, kg, attn, gc)


def _c_out_kernel(o_ref, z_ref, g_ref, w_ref, x_ref, out_ref):
    dk = C_HEAD_DIM
    parts = []
    for h in range(C_HEADS):
        sl = slice(h * dk, (h + 1) * dk)
        z = z_ref[:, sl].astype(F32)
        parts.append((_rms(o_ref[:, sl].astype(F32), g_ref[...]) * (z * jax.nn.sigmoid(z))).astype(BF16))
    out_ref[...] = x_ref[...] + _dot(jnp.concatenate(parts, axis=-1), w_ref[...])


def _c_out(o, proj, out_gain, w_out, x):
    m, d = x.shape
    z_block = (3 * C_WIDTH) // C_WIDTH
    row = lambda width: pl.BlockSpec((ROW_TILE, width), lambda i: (i, 0))
    return pl.pallas_call(
        _c_out_kernel,
        out_shape=jax.ShapeDtypeStruct((m, d), F32),
        grid=(m // ROW_TILE,),
        in_specs=[row(C_WIDTH),
                  pl.BlockSpec((ROW_TILE, C_WIDTH), lambda i: (i, z_block)),
                  pl.BlockSpec((1, C_HEAD_DIM), lambda i: (0, 0)),
                  pl.BlockSpec((C_WIDTH, d), lambda i: (0, 0)),
                  row(d)],
        out_specs=row(d),
        compiler_params=_params("parallel"),
        name="c_out",
    )(o, proj, out_gain.reshape(1, -1), w_out, x)


def _mixer_c(x, norm1, w_in, conv_w, a_log, dt_bias, out_gain, w_out, batch, seq):
    main = 4 * C_WIDTH
    proj = _norm_matmul(x, norm1, w_in[:, :main].astype(BF16), BF16, 512)
    w_small = jnp.pad(w_in[:, main:], ((0, 0), (0, 128 - 2 * C_HEADS))).astype(BF16)
    small = _norm_matmul(x, norm1, w_small, F32, 128)
    qkv, bg = _c_conv(proj.reshape(batch, seq, main), small.reshape(batch, seq, 128), conv_w, a_log, dt_bias,
                      batch, seq)
    bgt = bg[:, :, :2 * C_HEADS].transpose(0, 2, 1)
    u, w, qg, kg, attn, gc = _c_chunks(qkv, bg, bgt, batch, seq)
    o = _c_scan(u, w, qg, kg, attn, gc, batch, seq)
    return _c_out(o.reshape(batch * seq, C_WIDTH), proj, out_gain, w_out.astype(BF16), x)


def kernel(x, rel_bias, l0_norm1, l0_a_w_in, l0_a_q_gain, l0_a_k_gain, l0_a_w_out, l0_norm2, l0_ffn_w_gate, l0_ffn_w_up, l0_ffn_w_down, l1_norm1, l1_b_w_in, l1_b_q_gain, l1_b_k_gain, l1_b_cmp_pos, l1_b_cmp_w1, l1_b_cmp_w2, l1_b_w_out, l1_norm2, l1_ffn_w_gate, l1_ffn_w_up, l1_ffn_w_down, l2_norm1, l2_c_w_in, l2_c_conv_w, l2_c_a_log, l2_c_dt_bias, l2_c_out_gain, l2_c_w_out, l2_norm2, l2_ffn_w_gate, l2_ffn_w_up, l2_ffn_w_down, l3_norm1, l3_a_w_in, l3_a_q_gain, l3_a_k_gain, l3_a_w_out, l3_norm2, l3_ffn_w_gate, l3_ffn_w_up, l3_ffn_w_down):
    batch, seq, d = x.shape
    h = x.reshape(batch * seq, d)

    def ffn(h, norm2, w_gate, w_up, w_down):
        return _ffn(h, norm2, w_gate.astype(BF16), w_up.astype(BF16), w_down.astype(BF16))

    a_biases = _a_biases(rel_bias)
    h = _mixer_a(h, a_biases, l0_norm1, l0_a_w_in, l0_a_q_gain, l0_a_k_gain, l0_a_w_out, batch, seq)
    h = ffn(h, l0_norm2, l0_ffn_w_gate, l0_ffn_w_up, l0_ffn_w_down)
    h = _mixer_b(h, rel_bias, l1_norm1, l1_b_w_in, l1_b_q_gain, l1_b_k_gain, l1_b_cmp_pos, l1_b_cmp_w1,
                 l1_b_cmp_w2, l1_b_w_out, batch, seq)
    h = ffn(h, l1_norm2, l1_ffn_w_gate, l1_ffn_w_up, l1_ffn_w_down)
    h = _mixer_c(h, l2_norm1, l2_c_w_in, l2_c_conv_w, l2_c_a_log, l2_c_dt_bias, l2_c_out_gain, l2_c_w_out,
                 batch, seq)
    h = ffn(h, l2_norm2, l2_ffn_w_gate, l2_ffn_w_up, l2_ffn_w_down)
    h = _mixer_a(h, a_biases, l3_norm1, l3_a_w_in, l3_a_q_gain, l3_a_k_gain, l3_a_w_out, batch, seq)
    h = ffn(h, l3_norm2, l3_ffn_w_gate, l3_ffn_w_up, l3_ffn_w_down)
    return h.reshape(batch, seq, d)
```

```python
import functools
import math

import numpy as np
import jax
import jax.numpy as jnp
from jax import lax
from jax.experimental import pallas as pl
from jax.experimental.pallas import tpu as pltpu

D_MODEL = 1024
RMS_EPS = 1e-6
NEG_INF = -1e30
TINY = 1e-30
FORCE_SCORE = 1e9

N_BUCKETS = 32
REL_MAX_DISTANCE = 2048
N_HEADS = 16

A_GROUPS = ((128, 1), (512, 4), (2048, 16))
A_HEAD_DIM = 64
A_Q_BLOCK = 128
A_PROJ_TILE = 512
A_STAT_WIDTH = 256
A_BLOCKS_PER_STEP = 8

B_KV_HEADS = 4
B_GROUP = 4
B_HEAD_DIM = 64
B_CMP_LEN = 32
B_CMP_STRIDE = 16
B_CMP_HIDDEN = 256
B_SEL_BLOCK = 64
B_TOP_N = 16
B_WINDOW = 512
B_TILE = 128
B_SWEEP = 256
B_PROJ_WIDTH = 3072
B_GATE_ROWS = 16

C_HEADS = 8
C_HEAD_DIM = 128
C_WIDTH = C_HEADS * C_HEAD_DIM
C_CONV = 4
C_CHUNK = 64
C_GROUP = 4

FFN_HIDDEN = 2816
FFN_TILE = 1024

ROW_TILE = 512
VMEM_LIMIT = 48 * 1024 * 1024

LOG2E = math.log2(math.e)

F32 = jnp.float32
BF16 = jnp.bfloat16

NT_DIMS = (((1,), (1,)), ((), ()))
TN_DIMS = (((0,), (0,)), ((), ()))


def _params(*semantics):
    return pltpu.CompilerParams(dimension_semantics=semantics, vmem_limit_bytes=VMEM_LIMIT)


def _dot(a, b):
    return jnp.dot(a, b, preferred_element_type=F32)


def _dot_nt(a, b):
    return lax.dot_general(a, b, NT_DIMS, preferred_element_type=F32)


def _dot_tn(a, b):
    return lax.dot_general(a, b, TN_DIMS, preferred_element_type=F32)


def _rms(x, gain):
    return x * lax.rsqrt(jnp.mean(x * x, axis=-1, keepdims=True) + RMS_EPS) * gain


def _bucket_thresholds():
    d = np.arange(1 << 15)
    max_exact = N_BUCKETS // 2
    d_f = np.maximum(d, 1).astype(np.float32)
    large = max_exact + (np.log(d_f / np.float32(max_exact)) / np.float32(math.log(REL_MAX_DISTANCE / max_exact))
                         * np.float32(N_BUCKETS - max_exact)).astype(np.int32)
    bucket = np.where(d < max_exact, d, np.minimum(large, N_BUCKETS - 1))
    return [int(np.argmax(bucket >= k)) if np.any(bucket >= k) else int(1 << 30) for k in range(N_BUCKETS)]


_THRESHOLDS = _bucket_thresholds()


def _bias_tile_kernel(tbl_ref, o_ref, *, base, tile_step, row_step, col_step, dmax, dil):
    h = pl.program_id(0)
    t = pl.program_id(1)
    shape = o_ref.shape[2:]
    i = lax.broadcasted_iota(jnp.int32, shape, 0)
    j = lax.broadcasted_iota(jnp.int32, shape, 1)
    dist = base + tile_step * t + row_step * i + col_step * j
    d = dist * dil
    val = jnp.full(shape, tbl_ref[0, h], F32)
    for k in range(1, N_BUCKETS):
        val = jnp.where(d >= _THRESHOLDS[k], tbl_ref[k, h], val)
    valid = (dist >= 0) & (dist <= dmax)
    o_ref[0, 0] = jnp.where(valid, val * LOG2E, NEG_INF).astype(o_ref.dtype)


def _bias_tiles(rel_bias, n_tiles, rows, cols, *, base, tile_step, row_step, col_step, dmax, dil=1, dtype=F32):
    kern = functools.partial(_bias_tile_kernel, base=base, tile_step=tile_step, row_step=row_step,
                             col_step=col_step, dmax=dmax, dil=dil)
    return pl.pallas_call(
        kern,
        out_shape=jax.ShapeDtypeStruct((N_HEADS, n_tiles, rows, cols), dtype),
        grid=(N_HEADS, n_tiles),
        in_specs=[pl.BlockSpec(memory_space=pltpu.SMEM)],
        out_specs=pl.BlockSpec((1, 1, rows, cols), lambda h, t: (h, t, 0, 0)),
        compiler_params=_params("parallel", "parallel"),
        name="bias_tiles",
    )(rel_bias)


def _resident(shape):
    return pl.BlockSpec(shape, lambda i: (0,) * len(shape), pipeline_mode=pl.Buffered(1))


def _norm_matmul_kernel(x_ref, g_ref, w_ref, o_ref, *, tn):
    h = _rms(x_ref[...], g_ref[...]).astype(BF16)
    for j in range(w_ref.shape[1] // tn):
        o_ref[:, j * tn:(j + 1) * tn] = _dot(h, w_ref[:, j * tn:(j + 1) * tn]).astype(o_ref.dtype)


def _norm_matmul(x, gain, w, out_dtype, tn):
    m, d = x.shape
    n = w.shape[1]
    return pl.pallas_call(
        functools.partial(_norm_matmul_kernel, tn=tn),
        out_shape=jax.ShapeDtypeStruct((m, n), out_dtype),
        grid=(m // ROW_TILE,),
        in_specs=[pl.BlockSpec((ROW_TILE, d), lambda i: (i, 0)),
                  _resident((1, d)),
                  _resident((d, n))],
        out_specs=pl.BlockSpec((ROW_TILE, n), lambda i: (i, 0)),
        compiler_params=_params("parallel"),
        name="norm_matmul",
    )(x, gain.reshape(1, d), w)


def _matmul_residual_kernel(a_ref, w_ref, x_ref, o_ref):
    o_ref[...] = x_ref[...] + _dot(a_ref[...], w_ref[...])


def _matmul_residual(a, w, x):
    m, k = a.shape
    d = w.shape[1]
    return pl.pallas_call(
        _matmul_residual_kernel,
        out_shape=jax.ShapeDtypeStruct((m, d), F32),
        grid=(m // ROW_TILE,),
        in_specs=[pl.BlockSpec((ROW_TILE, k), lambda i: (i, 0)),
                  pl.BlockSpec((k, d), lambda i: (0, 0)),
                  pl.BlockSpec((ROW_TILE, d), lambda i: (i, 0))],
        out_specs=pl.BlockSpec((ROW_TILE, d), lambda i: (i, 0)),
        compiler_params=_params("parallel"),
        name="matmul_residual",
    )(a, w, x)


def _ffn_kernel(x_ref, g_ref, wg_ref, wu_ref, wd_ref, o_ref):
    x = x_ref[...]
    h = _rms(x, g_ref[...]).astype(BF16)
    hidden = wg_ref.shape[1]
    acc = x
    for lo in range(0, hidden, FFN_TILE):
        hi = min(lo + FFN_TILE, hidden)
        a = _dot(h, wg_ref[:, lo:hi])
        b = _dot(h, wu_ref[:, lo:hi])
        acc = acc + _dot((a * jax.nn.sigmoid(a) * b).astype(BF16), wd_ref[lo:hi, :])
    o_ref[...] = acc


def _ffn(x, gain, w_gate, w_up, w_down):
    m, d = x.shape
    hidden = w_gate.shape[1]
    return pl.pallas_call(
        _ffn_kernel,
        out_shape=jax.ShapeDtypeStruct((m, d), F32),
        grid=(m // ROW_TILE,),
        in_specs=[pl.BlockSpec((ROW_TILE, d), lambda i: (i, 0)),
                  _resident((1, d)),
                  _resident((d, hidden)), _resident((d, hidden)), _resident((hidden, d))],
        out_specs=pl.BlockSpec((ROW_TILE, d), lambda i: (i, 0)),
        compiler_params=_params("parallel"),
        name="ffn",
    )(x, gain.reshape(1, d), w_gate, w_up, w_down)


def _a_proj_kernel(x_ref, g_ref, w_ref, qg_ref, kg_ref, o_ref, h_ref, x_scr, *, dil):
    rows = ROW_TILE // dil
    xn = _rms(x_ref[...], g_ref[...])
    if dil == 1:
        h_ref[...] = xn.astype(BF16)
    else:
        slabs = xn.shape[1] // 128
        for c in range(slabs):
            x_scr[c] = xn[:, c * 128:(c + 1) * 128]
        for r in range(dil):
            picked = [x_scr[c, pl.ds(r, rows, stride=dil), :] for c in range(slabs)]
            h_ref[r * rows:(r + 1) * rows, :] = jnp.concatenate(picked, axis=1).astype(BF16)
    h = h_ref[...]
    width = w_ref.shape[1]
    hd = width // 3
    low = lax.broadcasted_iota(jnp.int32, (ROW_TILE, 128), 1) < A_HEAD_DIM
    for j in range(width // A_PROJ_TILE):
        res = _dot(h, w_ref[:, j * A_PROJ_TILE:(j + 1) * A_PROJ_TILE])
        kind = (j * A_PROJ_TILE) // hd
        if kind < 2:
            parts = []
            for c in range(A_PROJ_TILE // 128):
                y = res[:, c * 128:(c + 1) * 128]
                sq = y * y
                tot = jnp.sum(sq, axis=-1, keepdims=True)
                lo = jnp.sum(jnp.where(low, sq, 0.0), axis=-1, keepdims=True)
                ss = jnp.where(low, lo, tot - lo)
                parts.append(y * lax.rsqrt(ss * (1.0 / A_HEAD_DIM) + RMS_EPS))
            res = jnp.concatenate(parts, axis=1) * (qg_ref if kind == 0 else kg_ref)[...]
        res = res.astype(BF16)
        for r in range(dil):
            off = r * width + j * A_PROJ_TILE
            o_ref[:, off:off + A_PROJ_TILE] = res[r * rows:(r + 1) * rows]


def _a_proj(x, gain, w, q_gain, k_gain, dil):
    m, d = x.shape
    width = w.shape[1]
    reps = A_PROJ_TILE // A_HEAD_DIM
    qg = jnp.tile(q_gain * (A_HEAD_DIM ** -0.5 * LOG2E), reps).reshape(1, A_PROJ_TILE)
    kg = jnp.tile(k_gain, reps).reshape(1, A_PROJ_TILE)
    return pl.pallas_call(
        functools.partial(_a_proj_kernel, dil=dil),
        out_shape=jax.ShapeDtypeStruct((m // dil, dil * width), BF16),
        grid=(m // ROW_TILE,),
        in_specs=[pl.BlockSpec((ROW_TILE, d), lambda i: (i, 0)),
                  pl.BlockSpec((1, d), lambda i: (0, 0)),
                  pl.BlockSpec((d, width), lambda i: (0, 0)),
                  pl.BlockSpec((1, A_PROJ_TILE), lambda i: (0, 0)),
                  pl.BlockSpec((1, A_PROJ_TILE), lambda i: (0, 0))],
        out_specs=pl.BlockSpec((ROW_TILE // dil, dil * width), lambda i: (i, 0)),
        scratch_shapes=[pltpu.VMEM((ROW_TILE, d), BF16), pltpu.VMEM((d // 128, ROW_TILE, 128), F32)],
        compiler_params=_params("parallel"),
        name="a_proj",
    )(x, gain.reshape(1, d), w, qg, kg)


def _a_attn_kernel(cur_ref, prev_ref, bias_ref, o_ref, stat_ref):
    first = (pl.program_id(2) == 0).astype(jnp.int32)
    nq = A_Q_BLOCK
    hd = N_HEADS * A_HEAD_DIM
    lane = lax.broadcasted_iota(jnp.int32, (nq, 128), 1)
    low = lane < A_HEAD_DIM
    ones = jnp.ones((2 * nq, 128), BF16)
    for res in range(o_ref.shape[2] // hd):
        for sub in range(cur_ref.shape[1] // nq):
            rows = slice(sub * nq, (sub + 1) * nq)
            max_tile = jnp.zeros((nq, 128), F32)
            den_tile = jnp.ones((nq, 128), F32)
            for pair in range(N_HEADS // 2):
                qc, kc, vc = (slice(res * 3 * hd + part * hd + pair * 128, res * 3 * hd + part * hd + (pair + 1) * 128)
                              for part in range(3))
                q = cur_ref[0, rows, qc]
                zero = jnp.zeros_like(q)
                qq = jnp.concatenate([jnp.where(low, q, zero), jnp.where(low, zero, q)], axis=0)
                if sub == 0:
                    kk = jnp.concatenate([prev_ref[0, :, kc], cur_ref[0, :nq, kc]], axis=0)
                    vv = jnp.concatenate([prev_ref[0, :, vc], cur_ref[0, :nq, vc]], axis=0)
                    base = 2 * pair + N_HEADS * first
                else:
                    kk = cur_ref[0, (sub - 1) * nq:(sub + 1) * nq, kc]
                    vv = cur_ref[0, (sub - 1) * nq:(sub + 1) * nq, vc]
                    base = 2 * pair
                s = _dot_nt(qq, kk) + jnp.concatenate([bias_ref[base], bias_ref[base + 1]], axis=0)
                m = jnp.max(s, axis=-1, keepdims=True)
                acc = _dot(jnp.exp2(s - m).astype(BF16), jnp.concatenate([vv, ones], axis=1))
                oc = slice(res * hd + pair * 128, res * hd + (pair + 1) * 128)
                o_ref[0, rows, oc] = jnp.where(low, acc[:nq, :128], acc[nq:, :128]).astype(o_ref.dtype)
                first_head = lane == 2 * pair
                second_head = lane == 2 * pair + 1
                max_tile = jnp.where(first_head, m[:nq], jnp.where(second_head, m[nq:], max_tile))
                den_tile = jnp.where(first_head, acc[:nq, 128:], jnp.where(second_head, acc[nq:, 128:], den_tile))
            stat_ref[0, rows, res * A_STAT_WIDTH:res * A_STAT_WIDTH + 128] = max_tile
            stat_ref[0, rows, res * A_STAT_WIDTH + 128:(res + 1) * A_STAT_WIDTH] = den_tile


def _a_attention(proj, bias, dil, batch, seq):
    length = seq // dil
    nblk = length // A_Q_BLOCK
    hd = N_HEADS * A_HEAD_DIM
    pv = proj.reshape(batch, length, dil * 3 * hd)

    per = min(A_BLOCKS_PER_STEP, nblk)
    res = min(A_BLOCKS_PER_STEP // per, dil)
    step_rows = per * A_Q_BLOCK
    assert nblk % per == 0 and dil % res == 0

    o, stats = pl.pallas_call(
        _a_attn_kernel,
        out_shape=(jax.ShapeDtypeStruct((batch, length, dil * hd), BF16),
                   jax.ShapeDtypeStruct((batch, length, dil * A_STAT_WIDTH), F32)),
        grid=(batch, dil // res, nblk // per),
        in_specs=[pl.BlockSpec((1, step_rows, res * 3 * hd), lambda b, r, i: (b, i, r)),
                  pl.BlockSpec((1, A_Q_BLOCK, res * 3 * hd), lambda b, r, i: (b, jnp.maximum(per * i - 1, 0), r)),
                  pl.BlockSpec((2 * N_HEADS, A_Q_BLOCK, 2 * A_Q_BLOCK), lambda b, r, i: (0, 0, 0))],
        out_specs=(pl.BlockSpec((1, step_rows, res * hd), lambda b, r, i: (b, i, r)),
                   pl.BlockSpec((1, step_rows, res * A_STAT_WIDTH), lambda b, r, i: (b, i, r))),
        compiler_params=_params("parallel", "parallel", "arbitrary"),
        name="a_attention",
    )(pv, pv, bias)
    return o.reshape(batch * length, dil * hd), stats.reshape(batch * length, dil * A_STAT_WIDTH)


def _a_out_kernel(o0_ref, o1_ref, o2_ref, s0_ref, s1_ref, s2_ref, e_ref, w_ref, x_ref, out_ref, o_scr, s_scr):
    hd = N_HEADS * A_HEAD_DIM
    sw = A_STAT_WIDTH
    for g, (o_ref, s_ref) in enumerate(((o0_ref, s0_ref), (o1_ref, s1_ref), (o2_ref, s2_ref))):
        dil = A_GROUPS[g][1]
        rows = ROW_TILE // dil
        for r in range(dil):
            dst = pl.ds(r, rows, stride=dil) if dil > 1 else slice(None)
            s_scr[g, 0, dst, :] = s_ref[:, r * sw:r * sw + 128]
            s_scr[g, 1, dst, :] = s_ref[:, r * sw + 128:(r + 1) * sw]
            for c in range(hd // 128):
                o_scr[g, c, dst, :] = o_ref[:, r * hd + c * 128:r * hd + (c + 1) * 128].astype(F32)
    groups = range(len(A_GROUPS))
    top = functools.reduce(jnp.maximum, [s_scr[g, 0] for g in groups])
    es = [jnp.exp2(s_scr[g, 0] - top) for g in groups]
    inv = 1.0 / sum(es[g] * s_scr[g, 1] for g in groups)
    expand = e_ref[...]
    acc = None
    for g in groups:
        o_g = jnp.concatenate([o_scr[g, c] for c in range(hd // 128)], axis=1)
        term = _dot((es[g] * inv).astype(BF16), expand) * o_g
        acc = term if acc is None else acc + term
    out_ref[...] = x_ref[...] + _dot(acc.astype(BF16), w_ref[...])


def _a_out(outs, stats, w_out, x):
    m, d = x.shape
    hd = N_HEADS * A_HEAD_DIM
    expand = np.zeros((128, hd), np.float32)
    for h in range(N_HEADS):
        expand[h, h * A_HEAD_DIM:(h + 1) * A_HEAD_DIM] = 1.0
    grouped = lambda width: [pl.BlockSpec((ROW_TILE // dil, dil * width), lambda i: (i, 0)) for _, dil in A_GROUPS]
    return pl.pallas_call(
        _a_out_kernel,
        out_shape=jax.ShapeDtypeStruct((m, d), F32),
        grid=(m // ROW_TILE,),
        in_specs=grouped(hd) + grouped(A_STAT_WIDTH) + [
            pl.BlockSpec((128, hd), lambda i: (0, 0)),
            pl.BlockSpec((hd, d), lambda i: (0, 0)),
            pl.BlockSpec((ROW_TILE, d), lambda i: (i, 0))],
        out_specs=pl.BlockSpec((ROW_TILE, d), lambda i: (i, 0)),
        scratch_shapes=[pltpu.VMEM((len(A_GROUPS), hd // 128, ROW_TILE, 128), F32),
                        pltpu.VMEM((len(A_GROUPS), 2, ROW_TILE, 128), F32)],
        compiler_params=_params("parallel"),
        name="a_out",
    )(*outs, *stats, jnp.asarray(expand, BF16), w_out, x)


def _a_biases(rel_bias):
    biases = []
    for window, dil in A_GROUPS:
        steps = window // dil
        assert steps == A_Q_BLOCK
        bias = _bias_tiles(rel_bias, 1, A_Q_BLOCK, 2 * A_Q_BLOCK, base=A_Q_BLOCK, tile_step=0, row_step=1,
                           col_step=-1, dmax=steps, dil=dil)[:, 0]
        biases.append(jnp.concatenate([bias, bias.at[:, :, :A_Q_BLOCK].set(NEG_INF)], axis=0))
    return biases


def _mixer_a(x, biases, norm1, w_in, q_gain, k_gain, w_out, batch, seq):
    w_in = w_in.astype(BF16)
    group_width = 3 * N_HEADS * A_HEAD_DIM
    outs, stats = [], []
    for gi, (_, dil) in enumerate(A_GROUPS):
        assert (seq // dil) % A_Q_BLOCK == 0 and seq % ROW_TILE == 0
        bias = biases[gi]
        proj = _a_proj(x, norm1, w_in[:, gi * group_width:(gi + 1) * group_width], q_gain[gi], k_gain[gi], dil)
        o, stat = _a_attention(proj, bias, dil, batch, seq)
        outs.append(o)
        stats.append(stat)
    return _a_out(outs, stats, w_out.astype(BF16), x)


def _b_prep_kernel(p_ref, qg_ref, kg_ref, qt_ref, ck_ref, cv_ref, ka_ref, vs_ref, wk_ref, vw_ref, gate_ref):
    dh = B_HEAD_DIM
    ts = p_ref.shape[1]
    n_sel = ka_ref.shape[3] - dh
    low = lax.broadcasted_iota(jnp.int32, (ts, 128), 1) < dh

    def slab(c):
        return p_ref[0, :, c * 128:(c + 1) * 128].astype(F32)

    def normed(x, gain):
        sq = x * x
        tot = jnp.sum(sq, axis=-1, keepdims=True)
        first = jnp.sum(jnp.where(low, sq, 0.0), axis=-1, keepdims=True)
        ss = jnp.where(low, first, tot - first)
        return x * lax.rsqrt(ss * (1.0 / dh) + RMS_EPS) * jnp.concatenate([gain, gain], axis=1)

    def halves(x):
        return x[:, :dh], x[:, dh:]

    qg = qg_ref[...] * (dh ** -0.5 * LOG2E)
    for c in range(N_HEADS // 2):
        xt = normed(slab(c), qg).T.astype(BF16)
        qt_ref[0, 2 * c] = xt[:dh]
        qt_ref[0, 2 * c + 1] = xt[dh:]
    base = N_HEADS // 2
    pairs = B_KV_HEADS // 2
    pos = pl.program_id(1) * ts + lax.broadcasted_iota(jnp.int32, (ts, 128), 0)
    lane = lax.broadcasted_iota(jnp.int32, (ts, 128), 1)
    onehot = jnp.where(lane - dh == pos // B_SEL_BLOCK, 1.0, 0.0)
    ones = jnp.ones((vs_ref.shape[2] - dh, ts), BF16)
    for j in range(pairs):
        for ref, off in ((ck_ref, 0), (cv_ref, pairs)):
            a, b = halves(slab(base + off + j))
            ref[0, 2 * j] = a.astype(BF16)
            ref[0, 2 * j + 1] = b.astype(BF16)
        k_sel = normed(slab(base + 2 * pairs + j), kg_ref[1:2, :])
        for n, k in zip((2 * j, 2 * j + 1), (k_sel, pltpu.roll(k_sel, dh, 1))):
            ka_ref[0, n] = jnp.where(low, k, onehot)[:, :dh + n_sel].astype(BF16)
        for n, k in zip((2 * j, 2 * j + 1), halves(normed(slab(base + 4 * pairs + j), kg_ref[2:3, :]))):
            wk_ref[0, n] = k.astype(BF16)
        for ref, off in ((vs_ref, 3 * pairs), (vw_ref, 5 * pairs)):
            xt = slab(base + off + j).T.astype(BF16)
            for n, v in zip((2 * j, 2 * j + 1), (xt[:dh], xt[dh:])):
                ref[0, n, :dh, :] = v
                ref[0, n, dh:, :] = ones
    gate = jax.nn.sigmoid(slab(base + 6 * pairs)).T
    rows = gate_ref.shape[2]
    for n in range(B_KV_HEADS):
        gate_ref[0, n] = gate[n * rows:(n + 1) * rows]


def _b_prep(proj, q_gain, k_gain, batch, seq):
    ts = ROW_TILE
    dh = B_HEAD_DIM
    n_sel = seq // B_SEL_BLOCK
    assert dh + n_sel <= 128
    rows_shape = lambda width: jax.ShapeDtypeStruct((batch, B_KV_HEADS, seq, width), BF16)
    rows_spec = lambda width: pl.BlockSpec((1, B_KV_HEADS, ts, width), lambda b, i: (b, 0, i, 0))
    cols_shape = lambda heads, height, dtype: jax.ShapeDtypeStruct((batch, heads, height, seq), dtype)
    cols_spec = lambda heads, height: pl.BlockSpec((1, heads, height, ts), lambda b, i: (b, 0, 0, i))
    return pl.pallas_call(
        _b_prep_kernel,
        out_shape=(cols_shape(N_HEADS, dh, BF16), rows_shape(dh), rows_shape(dh), rows_shape(dh + n_sel),
                   cols_shape(B_KV_HEADS, dh + 16, BF16), rows_shape(dh), cols_shape(B_KV_HEADS, dh + 16, BF16),
                   cols_shape(B_KV_HEADS, B_GATE_ROWS, F32)),
        grid=(batch, seq // ts),
        in_specs=[pl.BlockSpec((1, ts, B_PROJ_WIDTH), lambda b, i: (b, i, 0)),
                  pl.BlockSpec((1, dh), lambda b, i: (0, 0)),
                  pl.BlockSpec((3, dh), lambda b, i: (0, 0))],
        out_specs=(cols_spec(N_HEADS, dh), rows_spec(dh), rows_spec(dh), rows_spec(dh + n_sel),
                   cols_spec(B_KV_HEADS, dh + 16), rows_spec(dh), cols_spec(B_KV_HEADS, dh + 16),
                   cols_spec(B_KV_HEADS, B_GATE_ROWS)),
        compiler_params=_params("parallel", "parallel"),
        name="b_prep",
    )(proj, q_gain.reshape(1, dh), k_gain)


def _b_compress_kernel(tk_ref, tv_ref, pos_ref, w1_ref, w2_ref, kg_ref, kc_ref, vc_ref):
    half = (B_CMP_LEN // 2) * B_HEAD_DIM
    for kv, (t_ref, out_ref) in enumerate(((tk_ref, kc_ref), (tv_ref, vc_ref))):
        t = t_ref[0, 0].astype(F32)
        top = (t + pos_ref[kv, 0:1, :]).astype(BF16)
        bot = (t + pos_ref[kv, 1:2, :]).astype(BF16)
        a1 = _dot(top, w1_ref[kv, :half, :])
        a2 = _dot(bot, w1_ref[kv, half:, :])
        hidden = a1 + pltpu.roll(a2, a2.shape[0] - 1, 0)
        out = _dot(jax.nn.gelu(hidden).astype(BF16), w2_ref[kv])
        if kv == 0:
            out = _rms(out, kg_ref[...])
        out_ref[0, 0] = out.astype(out_ref.dtype)


def _b_compress(ck, cv, cmp_pos, cmp_w1, cmp_w2, k_gain0, batch, seq):
    rows = seq // B_CMP_STRIDE
    half = (B_CMP_LEN // 2) * B_HEAD_DIM
    tk = ck.reshape(batch, B_KV_HEADS, rows, half)
    tv = cv.reshape(batch, B_KV_HEADS, rows, half)
    pos = cmp_pos.reshape(2, 2, half)
    t_spec = pl.BlockSpec((1, 1, rows, half), lambda b, n: (b, n, 0, 0))
    o_spec = pl.BlockSpec((1, 1, rows, B_HEAD_DIM), lambda b, n: (b, n, 0, 0))
    shape = jax.ShapeDtypeStruct((batch, B_KV_HEADS, rows, B_HEAD_DIM), BF16)
    return pl.pallas_call(
        _b_compress_kernel,
        out_shape=(shape, shape),
        grid=(batch, B_KV_HEADS),
        in_specs=[t_spec, t_spec,
                  pl.BlockSpec((2, 2, half), lambda b, n: (0, 0, 0)),
                  pl.BlockSpec((2, 2 * half, B_CMP_HIDDEN), lambda b, n: (0, 0, 0)),
                  pl.BlockSpec((2, B_CMP_HIDDEN, B_HEAD_DIM), lambda b, n: (0, 0, 0)),
                  pl.BlockSpec((1, B_HEAD_DIM), lambda b, n: (0, 0))],
        out_specs=(o_spec, o_spec),
        compiler_params=_params("parallel", "parallel"),
        name="b_compress",
    )(tk, tv, pos, cmp_w1.astype(BF16), cmp_w2.astype(BF16), k_gain0.reshape(1, -1))


def _b_cmp_attn_kernel(qt_ref, kc_ref, vct_ref, bias_ref, c2s_ref, oc_ref, sel_ref, imp_ref, *, top_n):
    tq = B_SWEEP
    n_sel = imp_ref.shape[0]
    qt = jnp.concatenate([qt_ref[0, g] for g in range(B_GROUP)], axis=1)
    n_cmp_pad = kc_ref.shape[2]
    shift = tq // B_CMP_STRIDE
    off = pl.multiple_of((pl.num_programs(2) - 1 - pl.program_id(2)) * shift, shift)
    bias = jnp.concatenate([bias_ref[g, 0, pl.ds(off, n_cmp_pad), :] for g in range(B_GROUP)], axis=1)
    s = _dot(kc_ref[0, 0], qt) + bias
    m = jnp.max(s, axis=0, keepdims=True)
    e = jnp.exp2(s - m)
    z = jnp.maximum(jnp.sum(e, axis=0, keepdims=True), TINY)
    pos = pl.program_id(2) * tq + lax.broadcasted_iota(jnp.int32, (1, tq), 1)
    sees_any = jnp.concatenate([pos >= B_CMP_LEN - 1] * B_GROUP, axis=1)
    p = e * jnp.where(sees_any, 1.0 / z, 0.0)
    oct = _dot(vct_ref[0, 0], p.astype(BF16))
    for g in range(B_GROUP):
        oc_ref[0, g] = oct[:, g * tq:(g + 1) * tq]

    p_sum = p[:, 0:tq] + p[:, tq:2 * tq] + p[:, 2 * tq:3 * tq] + p[:, 3 * tq:4 * tq]
    hi = p_sum.astype(BF16)
    lo = (p_sum - hi.astype(F32)).astype(BF16)
    c2s = c2s_ref[...]
    imp = _dot(c2s, hi) + _dot(c2s, lo)

    t = pl.program_id(2) * tq + lax.broadcasted_iota(jnp.int32, (n_sel, tq), 1)
    blk = lax.broadcasted_iota(jnp.int32, (n_sel, tq), 0)
    cur = t // B_SEL_BLOCK
    forced = (blk == 0) | (blk == cur) | (blk == cur - 1)
    imp = jnp.where(forced, FORCE_SCORE, jnp.where(blk * B_SEL_BLOCK <= t, imp, NEG_INF))
    imp_ref[...] = imp

    def count(i, rank):
        row = imp_ref[pl.ds(i, 1), :]
        ahead = jnp.where(row > imp, 1.0, jnp.where(row == imp, jnp.where(blk > i, 1.0, 0.0), 0.0))
        return rank + ahead

    n_live = jnp.minimum(n_sel, (pl.program_id(2) + 1) * (tq // B_SEL_BLOCK))
    rank = lax.fori_loop(0, n_live, count, jnp.zeros((n_sel, tq), F32))
    sel_ref[0, 0] = jnp.where(rank < top_n, 0.0, NEG_INF).astype(sel_ref.dtype)


def _b_cmp_attn(qt, kc, vc, bias_c, batch, seq):
    n_sel = seq // B_SEL_BLOCK
    n_cmp_pad = seq // B_CMP_STRIDE
    n_cmp = (seq - B_CMP_LEN) // B_CMP_STRIDE + 1
    c = np.arange(n_cmp_pad)[None, :] * B_CMP_STRIDE
    j = np.arange(n_sel)[:, None] * B_SEL_BLOCK
    c2s = ((c < j + B_SEL_BLOCK) & (c + B_CMP_LEN > j) & (np.arange(n_cmp_pad)[None, :] < n_cmp)).astype(np.float32)
    kern = functools.partial(_b_cmp_attn_kernel, top_n=min(B_TOP_N, n_sel))
    return pl.pallas_call(
        kern,
        out_shape=(jax.ShapeDtypeStruct((batch, N_HEADS, B_HEAD_DIM, seq), F32),
                   jax.ShapeDtypeStruct((batch, B_KV_HEADS, n_sel, seq), BF16)),
        grid=(batch, B_KV_HEADS, seq // B_SWEEP),
        in_specs=[pl.BlockSpec((1, B_GROUP, B_HEAD_DIM, B_SWEEP), lambda b, n, i: (b, n, 0, i)),
                  pl.BlockSpec((1, 1, n_cmp_pad, B_HEAD_DIM), lambda b, n, i: (b, n, 0, 0)),
                  pl.BlockSpec((1, 1, B_HEAD_DIM, n_cmp_pad), lambda b, n, i: (b, n, 0, 0)),
                  pl.BlockSpec((B_GROUP, 1, bias_c.shape[2], B_SWEEP), lambda b, n, i: (n, 0, 0, 0)),
                  pl.BlockSpec((n_sel, n_cmp_pad), lambda b, n, i: (0, 0))],
        out_specs=(pl.BlockSpec((1, B_GROUP, B_HEAD_DIM, B_SWEEP), lambda b, n, i: (b, n, 0, i)),
                   pl.BlockSpec((1, 1, n_sel, B_SWEEP), lambda b, n, i: (b, n, 0, i))),
        scratch_shapes=[pltpu.VMEM((n_sel, B_SWEEP), F32)],
        compiler_params=_params("parallel", "parallel", "arbitrary"),
        name="b_cmp_attn",
    )(qt, kc, vc.transpose(0, 1, 3, 2), bias_c, jnp.asarray(c2s, BF16))


def _b_sparse_kernel(qt_ref, ka_ref, vs_ref, wk_ref, vw_ref, sel_ref, bs_ref, bw_ref, oc_ref, gate_ref,
                     o_ref, acc_ref, sa_ref, sb_ref, *, delta_max, win_tiles):
    tq = B_SWEEP
    dh = B_HEAD_DIM
    cols = B_GROUP * tq
    qi = pl.program_id(2)
    n_tiles = ka_ref.shape[2] // tq
    qt = jnp.concatenate([qt_ref[0, g] for g in range(B_GROUP)], axis=1)
    q_aug = jnp.concatenate([qt, jnp.concatenate([sel_ref[0, 0]] * B_GROUP, axis=1)], axis=0)

    def tile_start(kt):
        return pl.multiple_of(jnp.clip(kt, 0, n_tiles - 1) * tq, tq)

    def normalised(acc):
        return acc[:dh] * (1.0 / acc[dh:dh + 1])

    def sel_bias(kt):
        d = jnp.clip(qi - kt, -1, delta_max) + 1
        return jnp.concatenate([bs_ref[g, d] for g in range(B_GROUP)], axis=1)

    def sel_scores(kt):
        return _dot(ka_ref[0, 0, pl.ds(tile_start(kt), tq), :], q_aug).astype(BF16) + sel_bias(kt)

    def consume(s_buf, kt, m_old):
        s = s_buf[...]
        m_new = jnp.maximum(m_old, jnp.max(s, axis=0, keepdims=True).astype(F32))
        alpha = jnp.exp2(m_old - m_new)
        p = jnp.exp2(s - m_new.astype(BF16))
        acc_ref[...] = alpha * acc_ref[...] + _dot(vs_ref[0, 0, :, pl.ds(tile_start(kt), tq)], p)
        return m_new

    acc_ref[...] = jnp.zeros(acc_ref.shape, F32)
    sa_ref[...] = sel_scores(0)

    def pair(kt, m):
        sb_ref[...] = sel_scores(kt + 1)
        m = consume(sa_ref, kt, m)
        sa_ref[...] = sel_scores(kt + 2)
        return consume(sb_ref, kt + 1, m)

    n_pairs = (qi + 2) // 2
    n_double = n_pairs // 2
    m = lax.fori_loop(0, n_double, lambda j, m: pair(4 * j + 2, pair(4 * j, m)),
                      jnp.full((1, cols), NEG_INF, F32))
    lax.fori_loop(2 * n_double, n_pairs, lambda j, m: pair(2 * j, m), m)
    o_s = normalised(acc_ref[...])

    tiles = []
    for u in range(win_tiles):
        kt = qi - (win_tiles - 1) + u
        d = jnp.where(kt >= 0, qi - kt, -1) + 1
        bias = jnp.concatenate([bw_ref[g, d] for g in range(B_GROUP)], axis=1)
        tiles.append((_dot(wk_ref[0, 0, pl.ds(tile_start(kt), tq), :], qt).astype(BF16) + bias, kt))
    m = None
    for s, _ in tiles:
        tile_max = jnp.max(s, axis=0, keepdims=True)
        m = tile_max if m is None else jnp.maximum(m, tile_max)
    acc = None
    for s, kt in tiles:
        pv = _dot(vw_ref[0, 0, :, pl.ds(tile_start(kt), tq)], jnp.exp2(s - m))
        acc = pv if acc is None else acc + pv
    o_w = normalised(acc)

    gate = gate_ref[0, 0]
    merged = []
    for g in range(B_GROUP):
        cs = slice(g * tq, (g + 1) * tq)
        merged.append(gate[3 * g:3 * g + 1] * oc_ref[0, g] + gate[3 * g + 1:3 * g + 2] * o_s[:, cs]
                      + gate[3 * g + 2:3 * g + 3] * o_w[:, cs])
    for pair in range(B_GROUP // 2):
        both = jnp.concatenate([merged[2 * pair], merged[2 * pair + 1]], axis=0)
        o_ref[0, :, pair * 2 * dh:(pair + 1) * 2 * dh] = both.T.astype(o_ref.dtype)


def _b_sparse(qt, ka, vs, wk, vw, sel, bias_s, bias_w, oc, gate, batch, seq):
    n_sel = seq // B_SEL_BLOCK
    dh = B_HEAD_DIM
    n_ds = bias_s.shape[1]
    n_dw = bias_w.shape[1]
    vrows = vs.shape[2]
    kern = functools.partial(_b_sparse_kernel, delta_max=n_ds - 2, win_tiles=n_dw - 1)
    whole = lambda rows, width: pl.BlockSpec((1, 1, rows, width), lambda b, n, i: (b, n, 0, 0))
    return pl.pallas_call(
        kern,
        out_shape=jax.ShapeDtypeStruct((batch, seq, N_HEADS * dh), BF16),
        grid=(batch, B_KV_HEADS, seq // B_SWEEP),
        in_specs=[pl.BlockSpec((1, B_GROUP, dh, B_SWEEP), lambda b, n, i: (b, n, 0, i)),
                  whole(seq, dh + n_sel), whole(vrows, seq), whole(seq, dh), whole(vrows, seq),
                  pl.BlockSpec((1, 1, n_sel, B_SWEEP), lambda b, n, i: (b, n, 0, i)),
                  pl.BlockSpec((B_GROUP, n_ds, B_SWEEP, B_SWEEP), lambda b, n, i: (n, 0, 0, 0)),
                  pl.BlockSpec((B_GROUP, n_dw, B_SWEEP, B_SWEEP), lambda b, n, i: (n, 0, 0, 0)),
                  pl.BlockSpec((1, B_GROUP, dh, B_SWEEP), lambda b, n, i: (b, n, 0, i)),
                  pl.BlockSpec((1, 1, B_GATE_ROWS, B_SWEEP), lambda b, n, i: (b, n, 0, i))],
        out_specs=pl.BlockSpec((1, B_SWEEP, B_GROUP * dh), lambda b, n, i: (b, i, n)),
        scratch_shapes=[pltpu.VMEM((vrows, B_GROUP * B_SWEEP), F32),
                        pltpu.VMEM((B_SWEEP, B_GROUP * B_SWEEP), BF16),
                        pltpu.VMEM((B_SWEEP, B_GROUP * B_SWEEP), BF16)],
        compiler_params=_params("parallel", "parallel", "arbitrary"),
        name="b_sparse",
    )(qt, ka, vs, wk, vw, sel, bias_s, bias_w, oc, gate)


def _mixer_b(x, rel_bias, norm1, w_in, q_gain, k_gain, cmp_pos, cmp_w1, cmp_w2, w_out, batch, seq):
    d = w_in.shape[0]
    qkv_width = w_in.shape[1] - 3 * N_HEADS
    gate_w = w_in[:, qkv_width:].reshape(d, B_KV_HEADS, 3 * B_GROUP)
    gate_w = jnp.pad(gate_w, ((0, 0), (0, 0), (0, B_GATE_ROWS - 3 * B_GROUP))).reshape(d, B_KV_HEADS * B_GATE_ROWS)
    w_pad = jnp.concatenate([w_in[:, :qkv_width], gate_w], axis=1)
    w_pad = jnp.pad(w_pad, ((0, 0), (0, B_PROJ_WIDTH - w_pad.shape[1]))).astype(BF16)
    proj = _norm_matmul(x, norm1, w_pad, BF16, 512).reshape(batch, seq, B_PROJ_WIDTH)
    qt, ck, cv, ka, vs, wk, vw, gate_t = _b_prep(proj, q_gain, k_gain, batch, seq)
    kc, vc = _b_compress(ck, cv, cmp_pos, cmp_w1, cmp_w2, k_gain[0], batch, seq)
    last_tile = seq // B_SWEEP - 1
    bias_c = _bias_tiles(rel_bias, 1, seq // B_CMP_STRIDE + last_tile * (B_SWEEP // B_CMP_STRIDE), B_SWEEP,
                         base=1 - B_CMP_LEN + last_tile * B_SWEEP, tile_step=0, row_step=-B_CMP_STRIDE,
                         col_step=1, dmax=1 << 30)
    oc, sel = _b_cmp_attn(qt, kc, vc, bias_c, batch, seq)
    delta_max = min(seq // B_SWEEP - 1, -(-(_THRESHOLDS[-1] + B_SWEEP - 1) // B_SWEEP))
    bias_s = _bias_tiles(rel_bias, delta_max + 2, B_SWEEP, B_SWEEP, base=-B_SWEEP, tile_step=B_SWEEP,
                         row_step=-1, col_step=1, dmax=1 << 30, dtype=BF16)
    win_tiles = (B_WINDOW - 1 + B_SWEEP - 1) // B_SWEEP + 1
    bias_w = _bias_tiles(rel_bias, win_tiles + 1, B_SWEEP, B_SWEEP, base=-B_SWEEP, tile_step=B_SWEEP,
                         row_step=-1, col_step=1, dmax=B_WINDOW - 1, dtype=BF16)
    o = _b_sparse(qt, ka, vs, wk, vw, sel, bias_s, bias_w, oc, gate_t, batch, seq)
    return _matmul_residual(o.reshape(batch * seq, -1), w_out.astype(BF16), x)


def _c_conv_kernel(cur_ref, halo_ref, w_ref, sm_ref, alog_ref, dtb_ref, qkv_ref, bg_ref, xe_ref):
    ts = cur_ref.shape[1]
    keep = jnp.where(pl.program_id(1) == 0, 0.0, 1.0)
    dk = C_HEAD_DIM
    for c in range(3 * C_HEADS):
        sl = slice(c * dk, (c + 1) * dk)
        xe_ref[c, :8, :] = halo_ref[0, :, sl].astype(F32) * keep
        xe_ref[c, 8:, :] = cur_ref[0, :, sl].astype(F32)
        y = None
        for j in range(C_CONV):
            off = 8 - (C_CONV - 1) + j
            term = w_ref[j:j + 1, sl] * xe_ref[c, off:off + ts, :]
            y = term if y is None else y + term
        y = y * jax.nn.sigmoid(y)
        if c < 2 * C_HEADS:
            y = y * lax.rsqrt(jnp.sum(y * y, axis=-1, keepdims=True) + RMS_EPS)
        if c < C_HEADS:
            y = y * (dk ** -0.5)
        qkv_ref[0, :, sl] = y.astype(qkv_ref.dtype)
    sm = sm_ref[0]
    a = sm + dtb_ref[...]
    softplus = jnp.maximum(a, 0.0) + jnp.log1p(jnp.exp(-jnp.abs(a)))
    g = -jnp.exp(alog_ref[...]) * softplus
    lane = lax.broadcasted_iota(jnp.int32, sm.shape, 1)
    bg_ref[0] = jnp.where(lane < C_HEADS, jax.nn.sigmoid(sm), g)


def _c_conv(proj, small, conv_w, a_log, dt_bias, batch, seq):
    ts = C_GROUP * C_CHUNK
    width = 3 * C_WIDTH
    pad = lambda v: jnp.pad(v, (C_HEADS, 128 - 2 * C_HEADS)).reshape(1, 128)
    return pl.pallas_call(
        _c_conv_kernel,
        out_shape=(jax.ShapeDtypeStruct((batch, seq, width), BF16),
                   jax.ShapeDtypeStruct((batch, seq, 128), F32)),
        grid=(batch, seq // ts),
        in_specs=[pl.BlockSpec((1, ts, width), lambda b, i: (b, i, 0)),
                  pl.BlockSpec((1, 8, width), lambda b, i: (b, jnp.maximum(i * (ts // 8) - 1, 0), 0)),
                  pl.BlockSpec((C_CONV, width), lambda b, i: (0, 0)),
                  pl.BlockSpec((1, ts, 128), lambda b, i: (b, i, 0)),
                  pl.BlockSpec((1, 128), lambda b, i: (0, 0)),
                  pl.BlockSpec((1, 128), lambda b, i: (0, 0))],
        out_specs=(pl.BlockSpec((1, ts, width), lambda b, i: (b, i, 0)),
                   pl.BlockSpec((1, ts, 128), lambda b, i: (b, i, 0))),
        scratch_shapes=[pltpu.VMEM((3 * C_HEADS, ts + 8, C_HEAD_DIM), F32)],
        compiler_params=_params("parallel", "arbitrary"),
        name="c_conv",
    )(proj, proj, conv_w, small, pad(a_log), pad(dt_bias))


def _sum3(x, fn):
    hi = x.astype(BF16)
    r = x - hi.astype(F32)
    mid = r.astype(BF16)
    lo = (r - mid.astype(F32)).astype(BF16)
    return fn(hi) + (fn(mid) + fn(lo))


def _c_chunk_kernel(qkv_ref, bg_ref, bgt_ref, tri_ref, trit_ref, blk_ref, u_ref, w_ref, qg_ref, kg_ref, attn_ref,
                    gc_ref):
    cs = C_CHUNK
    dk = C_HEAD_DIM
    gs = C_GROUP * cs
    row = lax.broadcasted_iota(jnp.int32, (gs, gs), 0)
    col = lax.broadcasted_iota(jnp.int32, (gs, gs), 1)
    same = (row // cs) == (col // cs)
    causal = same & (row >= col)
    strict = same & (row > col)
    eye = jnp.where(row == col, 1.0, 0.0)

    bgc = bg_ref[0]
    tri = tri_ref[...]
    gcum_col = _sum3(bgc, lambda p: _dot(tri, p))
    glast_col = _sum3(bgc, lambda p: _dot(blk_ref[...], p))
    gcum_row = _sum3(bgt_ref[0], lambda p: _dot(p, trit_ref[...]))
    gc_ref[0] = gcum_col
    t_mats, powers = [], []
    for h in range(C_HEADS):
        gc = gcum_col[:, C_HEADS + h:C_HEADS + h + 1]
        gr = gcum_row[C_HEADS + h:C_HEADS + h + 1, :]
        q = qkv_ref[0, :, h * dk:(h + 1) * dk]
        k = qkv_ref[0, :, C_WIDTH + h * dk:C_WIDTH + (h + 1) * dk]
        decay = jnp.exp(jnp.where(causal, gc - gr, NEG_INF))
        k16 = k.astype(BF16)
        low = jnp.where(strict, _dot_nt((k * bgc[:, h:h + 1]).astype(BF16), k16) * decay, 0.0)
        t_mats.append(eye - low)
        powers.append(low.astype(BF16))
        attn = jnp.where(causal, _dot_nt(q.astype(BF16), k16), 0.0) * decay
        attn_ref[0, :, h * gs:(h + 1) * gs] = attn.astype(attn_ref.dtype)
        qg_ref[0, :, h * dk:(h + 1) * dk] = (q * jnp.exp(gc)).astype(qg_ref.dtype)
        glast = glast_col[:, C_HEADS + h:C_HEADS + h + 1]
        kg_ref[0, :, h * dk:(h + 1) * dk] = (k * jnp.exp(glast - gc)).astype(kg_ref.dtype)
    for _ in range(int(math.log2(cs)) - 1):
        powers = [_dot(p, p).astype(BF16) for p in powers]
        t_mats = [t + _dot(t.astype(BF16), p) for t, p in zip(t_mats, powers)]
    for h in range(C_HEADS):
        gc = gcum_col[:, C_HEADS + h:C_HEADS + h + 1]
        beta = bgc[:, h:h + 1]
        k = qkv_ref[0, :, C_WIDTH + h * dk:C_WIDTH + (h + 1) * dk]
        v = qkv_ref[0, :, 2 * C_WIDTH + h * dk:2 * C_WIDTH + (h + 1) * dk]
        rhs = jnp.concatenate([(v * beta).astype(BF16), (k * beta * jnp.exp(gc)).astype(BF16)], axis=1)
        both = _dot(t_mats[h].astype(BF16), rhs)
        u_ref[0, :, h * dk:(h + 1) * dk] = both[:, :dk].astype(u_ref.dtype)
        w_ref[0, :, h * dk:(h + 1) * dk] = both[:, dk:].astype(w_ref.dtype)


def _c_chunks(qkv, bg, bgt, batch, seq):
    gs = C_GROUP * C_CHUNK
    idx = np.arange(gs)
    same = (idx[:, None] // C_CHUNK) == (idx[None, :] // C_CHUNK)
    tri = (same & (idx[:, None] >= idx[None, :])).astype(np.float32)
    wide = lambda width: pl.BlockSpec((1, gs, width), lambda b, i: (b, i, 0))
    shape = lambda width, dtype: jax.ShapeDtypeStruct((batch, seq, width), dtype)
    const = pl.BlockSpec((gs, gs), lambda b, i: (0, 0))
    return pl.pallas_call(
        _c_chunk_kernel,
        out_shape=(shape(C_WIDTH, BF16), shape(C_WIDTH, BF16), shape(C_WIDTH, BF16), shape(C_WIDTH, BF16),
                   shape(C_HEADS * gs, BF16), shape(128, F32)),
        grid=(batch, seq // gs),
        in_specs=[wide(3 * C_WIDTH), wide(128),
                  pl.BlockSpec((1, 2 * C_HEADS, gs), lambda b, i: (b, 0, i)),
                  const, const, const],
        out_specs=(wide(C_WIDTH),) * 4 + (wide(C_HEADS * gs), wide(128)),
        compiler_params=_params("parallel", "parallel"),
        name="c_chunks",
    )(qkv, bg, bgt, jnp.asarray(tri, BF16), jnp.asarray(tri.T, BF16), jnp.asarray(same, BF16))


def _c_scan_kernel(u_ref, w_ref, qg_ref, kg_ref, attn_ref, gc_ref, o_ref, state_ref, vnew_ref):
    @pl.when(pl.program_id(1) == 0)
    def _():
        state_ref[...] = jnp.zeros_like(state_ref)

    cs = C_CHUNK
    dk = C_HEAD_DIM
    gs = C_GROUP * cs
    vnew_ref[...] = jnp.zeros_like(vnew_ref)
    heads = range(C_HEADS)
    cols = [slice(h * dk, (h + 1) * dk) for h in heads]
    states = [state_ref[h] for h in heads]
    for c in range(C_GROUP):
        rs = slice(c * cs, (c + 1) * cs)
        decay_last = jnp.exp(gc_ref[0, (c + 1) * cs - 1:(c + 1) * cs, :])
        both = [_dot(jnp.concatenate([w_ref[0, rs, cols[h]], qg_ref[0, rs, cols[h]]], axis=0),
                     states[h].astype(BF16)) for h in heads]
        v16 = [(u_ref[0, rs, cols[h]] - both[h][:cs]).astype(BF16) for h in heads]
        for h in heads:
            vnew_ref[h, rs, :] = v16[h]
        for h in heads:
            out = both[h][cs:] + _dot(attn_ref[0, rs, h * gs:(h + 1) * gs], vnew_ref[h])
            o_ref[0, rs, cols[h]] = out.astype(o_ref.dtype)
        states = [states[h] * decay_last[:, C_HEADS + h:C_HEADS + h + 1] + _dot_tn(kg_ref[0, rs, cols[h]], v16[h])
                  for h in heads]
    for h in heads:
        state_ref[h] = states[h]


def _c_scan(u, w, qg, kg, attn, gc, batch, seq):
    gs = C_GROUP * C_CHUNK
    wide = lambda width: pl.BlockSpec((1, gs, width), lambda b, c: (b, c, 0))
    return pl.pallas_call(
        _c_scan_kernel,
        out_shape=jax.ShapeDtypeStruct((batch, seq, C_WIDTH), BF16),
        grid=(batch, seq // gs),
        in_specs=[wide(C_WIDTH)] * 4 + [wide(C_HEADS * gs), wide(128)],
        out_specs=wide(C_WIDTH),
        scratch_shapes=[pltpu.VMEM((C_HEADS, C_HEAD_DIM, C_HEAD_DIM), F32),
                        pltpu.VMEM((C_HEADS, gs, C_HEAD_DIM), BF16)],
        compiler_params=_params("parallel", "arbitrary"),
        name="c_scan",
    )(u, w, qg, kg, attn, gc)


def _c_out_kernel(o_ref, z_ref, g_ref, w_ref, x_ref, out_ref):
    dk = C_HEAD_DIM
    parts = []
    for h in range(C_HEADS):
        sl = slice(h * dk, (h + 1) * dk)
        z = z_ref[:, sl].astype(F32)
        parts.append((_rms(o_ref[:, sl].astype(F32), g_ref[...]) * (z * jax.nn.sigmoid(z))).astype(BF16))
    out_ref[...] = x_ref[...] + _dot(jnp.concatenate(parts, axis=-1), w_ref[...])


def _c_out(o, proj, out_gain, w_out, x):
    m, d = x.shape
    z_block = (3 * C_WIDTH) // C_WIDTH
    row = lambda width: pl.BlockSpec((ROW_TILE, width), lambda i: (i, 0))
    return pl.pallas_call(
        _c_out_kernel,
        out_shape=jax.ShapeDtypeStruct((m, d), F32),
        grid=(m // ROW_TILE,),
        in_specs=[row(C_WIDTH),
                  pl.BlockSpec((ROW_TILE, C_WIDTH), lambda i: (i, z_block)),
                  pl.BlockSpec((1, C_HEAD_DIM), lambda i: (0, 0)),
                  pl.BlockSpec((C_WIDTH, d), lambda i: (0, 0)),
                  row(d)],
        out_specs=row(d),
        compiler_params=_params("parallel"),
        name="c_out",
    )(o, proj, out_gain.reshape(1, -1), w_out, x)


def _mixer_c(x, norm1, w_in, conv_w, a_log, dt_bias, out_gain, w_out, batch, seq):
    main = 4 * C_WIDTH
    proj = _norm_matmul(x, norm1, w_in[:, :main].astype(BF16), BF16, 512)
    w_small = jnp.pad(w_in[:, main:], ((0, 0), (0, 128 - 2 * C_HEADS))).astype(BF16)
    small = _norm_matmul(x, norm1, w_small, F32, 128)
    qkv, bg = _c_conv(proj.reshape(batch, seq, main), small.reshape(batch, seq, 128), conv_w, a_log, dt_bias,
                      batch, seq)
    bgt = bg[:, :, :2 * C_HEADS].transpose(0, 2, 1)
    u, w, qg, kg, attn, gc = _c_chunks(qkv, bg, bgt, batch, seq)
    o = _c_scan(u, w, qg, kg, attn, gc, batch, seq)
    return _c_out(o.reshape(batch * seq, C_WIDTH), proj, out_gain, w_out.astype(BF16), x)


def kernel(x, rel_bias, l0_norm1, l0_a_w_in, l0_a_q_gain, l0_a_k_gain, l0_a_w_out, l0_norm2, l0_ffn_w_gate, l0_ffn_w_up, l0_ffn_w_down, l1_norm1, l1_b_w_in, l1_b_q_gain, l1_b_k_gain, l1_b_cmp_pos, l1_b_cmp_w1, l1_b_cmp_w2, l1_b_w_out, l1_norm2, l1_ffn_w_gate, l1_ffn_w_up, l1_ffn_w_down, l2_norm1, l2_c_w_in, l2_c_conv_w, l2_c_a_log, l2_c_dt_bias, l2_c_out_gain, l2_c_w_out, l2_norm2, l2_ffn_w_gate, l2_ffn_w_up, l2_ffn_w_down, l3_norm1, l3_a_w_in, l3_a_q_gain, l3_a_k_gain, l3_a_w_out, l3_norm2, l3_ffn_w_gate, l3_ffn_w_up, l3_ffn_w_down):
    batch, seq, d = x.shape
    h = x.reshape(batch * seq, d)

    def ffn(h, norm2, w_gate, w_up, w_down):
        return _ffn(h, norm2, w_gate.astype(BF16), w_up.astype(BF16), w_down.astype(BF16))

    a_biases = _a_biases(rel_bias)
    h = _mixer_a(h, a_biases, l0_norm1, l0_a_w_in, l0_a_q_gain, l0_a_k_gain, l0_a_w_out, batch, seq)
    h = ffn(h, l0_norm2, l0_ffn_w_gate, l0_ffn_w_up, l0_ffn_w_down)
    h = _mixer_b(h, rel_bias, l1_norm1, l1_b_w_in, l1_b_q_gain, l1_b_k_gain, l1_b_cmp_pos, l1_b_cmp_w1,
                 l1_b_cmp_w2, l1_b_w_out, batch, seq)
    h = ffn(h, l1_norm2, l1_ffn_w_gate, l1_ffn_w_up, l1_ffn_w_down)
    h = _mixer_c(h, l2_norm1, l2_c_w_in, l2_c_conv_w, l2_c_a_log, l2_c_dt_bias, l2_c_out_gain, l2_c_w_out,
                 batch, seq)
    h = ffn(h, l2_norm2, l2_ffn_w_gate, l2_ffn_w_up, l2_ffn_w_down)
    h = _mixer_a(h, a_biases, l3_norm1, l3_a_w_in, l3_a_q_gain, l3_a_k_gain, l3_a_w_out, batch, seq)
    h = ffn(h, l3_norm2, l3_ffn_w_gate, l3_ffn_w_up, l3_ffn_w_down)
    return h.reshape(batch, seq, d)
```
